```python
import jax, jax.numpy as jnp
from jax import lax
import numpy as np

D_MODEL = 1024
BATCH = 8
SEQ = 4096
DEPTH = 2

N_A_LAYERS = DEPTH // 2
N_B_LAYERS = DEPTH - N_A_LAYERS
HGRN_EXPAND = 128
HGRN_HEADS = D_MODEL // HGRN_EXPAND
HGRN_DK = HGRN_EXPAND
HGRN_DV = D_MODEL // HGRN_HEADS
HGRN_CHUNK = 64
MLA_HEADS = 16
MLA_NOPE = 128
MLA_ROPE = 64
MLA_V = 128
MLA_Q_LORA = 256
MLA_KV_LORA = 256
ROPE_THETA = 10000.0
QBLOCK = 128
D_FF = 4 * D_MODEL
EPS = 1e-6

kernel_name = 'hybrid_hgrn2_mla_yoco'


def rmsnorm(x, gain):
    xf = x.astype(jnp.float32)
    y = xf * lax.rsqrt(jnp.mean(xf * xf, axis=-1, keepdims=True) + EPS)
    return (y * gain.astype(jnp.float32)).astype(x.dtype)


def rope_tables(seq):
    half = MLA_ROPE // 2
    inv_freq = ROPE_THETA ** (-jnp.arange(half, dtype=jnp.float32) / half)
    ang = jnp.arange(seq, dtype=jnp.float32)[:, None] * inv_freq[None, :]
    return jnp.cos(ang), jnp.sin(ang)


def apply_rope(x, cos, sin):
    half = MLA_ROPE // 2
    xf = x.astype(jnp.float32)
    x1, x2 = xf[..., :half], xf[..., half:]
    return jnp.concatenate([x1 * cos - x2 * sin, x2 * cos + x1 * sin], axis=-1).astype(x.dtype)


def hgrn_lower_bounds(lb_logits):
    return jnp.cumsum(jax.nn.softmax(lb_logits.astype(jnp.float32), axis=0), axis=0)


def hgrn2_mixer(xn, w_q, w_f, w_i, w_g, g_norm, w_o, lb):
    bsz, seq, _ = xn.shape
    nc = seq // HGRN_CHUNK
    f32 = jnp.float32
    q = jax.nn.silu((xn @ w_q).astype(f32))
    forget = lb + (1.0 - lb) * jax.nn.sigmoid((xn @ w_f).astype(f32))
    log_f = jnp.log(forget)
    k = 1.0 - forget
    v = (xn @ w_i).astype(f32)

    def chunks(t, d):
        return t.reshape(bsz, nc, HGRN_CHUNK, HGRN_HEADS, d).transpose(1, 0, 3, 2, 4)

    causal = jnp.tril(jnp.ones((HGRN_CHUNK, HGRN_CHUNK), dtype=bool))

    def step(state, inp):
        qc, kc, vc, gc = inp
        b = jnp.cumsum(gc, axis=2)
        o_inter = jnp.einsum('bhtd,bhdv->bhtv', qc * jnp.exp(b), state)
        diff = b[:, :, :, None, :] - b[:, :, None, :, :]
        decay = jnp.exp(jnp.where(causal[:, :, None], diff, -jnp.inf))
        scores = jnp.einsum('bhtd,bhsd,bhtsd->bhts', qc, kc, decay)
        o_intra = jnp.einsum('bhts,bhsv->bhtv', scores, vc)
        b_last = b[:, :, -1:, :]
        new_state = jnp.exp(b_last[:, :, 0, :])[..., None] * state + jnp.einsum(
            'bhsd,bhsv->bhdv', kc * jnp.exp(b_last - b), vc)
        return new_state, o_inter + o_intra

    state0 = jnp.zeros((bsz, HGRN_HEADS, HGRN_DK, HGRN_DV), f32)
    _, o = lax.scan(step, state0, (chunks(q, HGRN_DK), chunks(k, HGRN_DK),
                                   chunks(v, HGRN_DV), chunks(log_f, HGRN_DK)))
    o = o.transpose(1, 0, 3, 2, 4).reshape(bsz, seq, HGRN_HEADS, HGRN_DV)
    o = rmsnorm(o, g_norm)
    gate = jax.nn.silu((xn @ w_g).astype(f32)).reshape(bsz, seq, HGRN_HEADS, HGRN_DV)
    o = (o * gate).reshape(bsz, seq, D_MODEL).astype(xn.dtype)
    return o @ w_o


def shared_mla_kv(h, in_norm, w_dkv, kv_norm, w_uk, w_uv, cos, sin):
    bsz, seq, _ = h.shape
    hn = rmsnorm(h, in_norm)
    ckr = hn @ w_dkv
    c_kv = rmsnorm(ckr[..., :MLA_KV_LORA], kv_norm)
    k_rope = apply_rope(ckr[..., MLA_KV_LORA:], cos, sin)
    k_nope = (c_kv @ w_uk).reshape(bsz, seq, MLA_HEADS, MLA_NOPE)
    v = (c_kv @ w_uv).reshape(bsz, seq, MLA_HEADS, MLA_V)
    return k_nope, k_rope, v


def mla_mixer(xn, w_dq, q_norm, w_uq, w_o, k_nope, k_rope, v, cos, sin):
    bsz, seq, _ = xn.shape
    nb = seq // QBLOCK
    c_q = rmsnorm(xn @ w_dq, q_norm)
    q = (c_q @ w_uq).reshape(bsz, seq, MLA_HEADS, MLA_NOPE + MLA_ROPE)
    q_nope = q[..., :MLA_NOPE]
    q_rope = apply_rope(q[..., MLA_NOPE:], cos[:, None, :], sin[:, None, :])
    qn_b = q_nope.reshape(bsz, nb, QBLOCK, MLA_HEADS, MLA_NOPE).transpose(1, 0, 2, 3, 4)
    qr_b = q_rope.reshape(bsz, nb, QBLOCK, MLA_HEADS, MLA_ROPE).transpose(1, 0, 2, 3, 4)
    starts = jnp.arange(nb, dtype=jnp.int32) * QBLOCK
    key_pos = jnp.arange(seq, dtype=jnp.int32)
    scale = (MLA_NOPE + MLA_ROPE) ** -0.5

    def block(args):
        qn, qr, start = args
        s = jnp.einsum('bqhd,bkhd->bhqk', qn, k_nope) + jnp.einsum('bqhr,bkr->bhqk', qr, k_rope)
        s = s.astype(jnp.float32) * scale
        q_pos = start + jnp.arange(QBLOCK, dtype=jnp.int32)
        s = jnp.where(key_pos[None, :] <= q_pos[:, None], s, -jnp.inf)
        p = jax.nn.softmax(s, axis=-1).astype(v.dtype)
        return jnp.einsum('bhqk,bkhv->bqhv', p, v)

    o = lax.map(block, (qn_b, qr_b, starts))
    o = o.transpose(1, 0, 2, 3, 4).reshape(bsz, seq, MLA_HEADS * MLA_V)
    return o @ w_o


def sq_relu_mlp(xn, w_up, w_down):
    return jnp.square(jax.nn.relu(xn @ w_up)) @ w_down


def _fwd_setup_inputs(seed: int = 0) -> dict:
    key = jax.random.key(seed)
    ks = jax.random.split(key, 24)
    f32 = jnp.float32

    def w(k, shape, fan_in):
        return jax.random.normal(k, shape, f32) * (fan_in ** -0.5)

    def gain(k, shape):
        return 1.0 + 0.02 * jax.random.normal(k, shape, f32)

    D = D_MODEL
    return {
        'x': jax.random.normal(ks[0], (BATCH, SEQ, D), f32),
        'hgrn_norm': gain(ks[1], (N_A_LAYERS, D)),
        'hgrn_w_q': w(ks[2], (N_A_LAYERS, D, D), D),
        'hgrn_w_f': w(ks[3], (N_A_LAYERS, D, D), D),
        'hgrn_w_i': w(ks[4], (N_A_LAYERS, D, D), D),
        'hgrn_w_g': w(ks[5], (N_A_LAYERS, D, D), D),
        'hgrn_g_norm': gain(ks[6], (N_A_LAYERS, HGRN_DV)),
        'hgrn_w_o': w(ks[7], (N_A_LAYERS, D, D), D),
        'hgrn_lb_logits': 0.5 * jax.random.normal(ks[8], (N_A_LAYERS + 1, D), f32),
        'mla_norm': gain(ks[9], (N_B_LAYERS, D)),
        'mla_w_dq': w(ks[10], (N_B_LAYERS, D, MLA_Q_LORA), D),
        'mla_q_norm': gain(ks[11], (N_B_LAYERS, MLA_Q_LORA)),
        'mla_w_uq': w(ks[12], (N_B_LAYERS, MLA_Q_LORA, MLA_HEADS * (MLA_NOPE + MLA_ROPE)), MLA_Q_LORA),
        'mla_w_o': w(ks[13], (N_B_LAYERS, MLA_HEADS * MLA_V, D), MLA_HEADS * MLA_V),
        'kv_in_norm': gain(ks[14], (D,)),
        'kv_w_dkv': w(ks[15], (D, MLA_KV_LORA + MLA_ROPE), D),
        'kv_norm': gain(ks[16], (MLA_KV_LORA,)),
        'kv_w_uk': w(ks[17], (MLA_KV_LORA, MLA_HEADS * MLA_NOPE), MLA_KV_LORA),
        'kv_w_uv': w(ks[18], (MLA_KV_LORA, MLA_HEADS * MLA_V), MLA_KV_LORA),
        'mlp_norm': gain(ks[19], (DEPTH, D)),
        'mlp_w_up': w(ks[20], (DEPTH, D, D_FF), D),
        'mlp_w_down': w(ks[21], (DEPTH, D_FF, D), D_FF),
        'final_norm': gain(ks[22], (D,)),
    }


def _fwd_reference(x, hgrn_norm, hgrn_w_q, hgrn_w_f, hgrn_w_i, hgrn_w_g, hgrn_g_norm, hgrn_w_o,
              hgrn_lb_logits, mla_norm, mla_w_dq, mla_q_norm, mla_w_uq, mla_w_o,
              kv_in_norm, kv_w_dkv, kv_norm, kv_w_uk, kv_w_uv,
              mlp_norm, mlp_w_up, mlp_w_down, final_norm):
    seq = x.shape[1]
    cos, sin = rope_tables(seq)
    lower_bounds = hgrn_lower_bounds(hgrn_lb_logits)
    h = x
    k_nope = k_rope = v = None
    for l in range(DEPTH):
        if l < N_A_LAYERS:
            h = h + hgrn2_mixer(rmsnorm(h, hgrn_norm[l]), hgrn_w_q[l], hgrn_w_f[l], hgrn_w_i[l],
                                hgrn_w_g[l], hgrn_g_norm[l], hgrn_w_o[l], lower_bounds[l])
        else:
            j = l - N_A_LAYERS
            h = h + mla_mixer(rmsnorm(h, mla_norm[j]), mla_w_dq[j], mla_q_norm[j], mla_w_uq[j],
                              mla_w_o[j], k_nope, k_rope, v, cos, sin)
        h = h + sq_relu_mlp(rmsnorm(h, mlp_norm[l]), mlp_w_up[l], mlp_w_down[l])
        if l == N_A_LAYERS - 1:
            k_nope, k_rope, v = shared_mla_kv(h, kv_in_norm, kv_w_dkv, kv_norm, kv_w_uk, kv_w_uv, cos, sin)
    return rmsnorm(h, final_norm)


import jax as _jax
import jax.numpy as _jnp

TWIN_FORMAT = 'train_step'
FWD_PARAMS = ['x', 'hgrn_norm', 'hgrn_w_q', 'hgrn_w_f', 'hgrn_w_i', 'hgrn_w_g', 'hgrn_g_norm', 'hgrn_w_o', 'hgrn_lb_logits', 'mla_norm', 'mla_w_dq', 'mla_q_norm', 'mla_w_uq', 'mla_w_o', 'kv_in_norm', 'kv_w_dkv', 'kv_norm', 'kv_w_uk', 'kv_w_uv', 'mlp_norm', 'mlp_w_up', 'mlp_w_down', 'final_norm']
TWIN_WEIGHTS = ['hgrn_norm', 'hgrn_w_q', 'hgrn_w_f', 'hgrn_w_i', 'hgrn_w_g', 'hgrn_g_norm', 'hgrn_w_o', 'hgrn_lb_logits', 'mla_norm', 'mla_w_dq', 'mla_q_norm', 'mla_w_uq', 'mla_w_o', 'kv_in_norm', 'kv_w_dkv', 'kv_norm', 'kv_w_uk', 'kv_w_uv', 'mlp_norm', 'mlp_w_up', 'mlp_w_down', 'final_norm']
TWIN_DIFF_INPUT = 'x'
TWIN_INPUTS = ['x', 'hgrn_norm', 'hgrn_w_q', 'hgrn_w_f', 'hgrn_w_i', 'hgrn_w_g', 'hgrn_g_norm', 'hgrn_w_o', 'hgrn_lb_logits', 'mla_norm', 'mla_w_dq', 'mla_q_norm', 'mla_w_uq', 'mla_w_o', 'kv_in_norm', 'kv_w_dkv', 'kv_norm', 'kv_w_uk', 'kv_w_uv', 'mlp_norm', 'mlp_w_up', 'mlp_w_down', 'final_norm', 'loss_target', 'm_hgrn_norm', 'm_hgrn_w_q', 'm_hgrn_w_f', 'm_hgrn_w_i', 'm_hgrn_w_g', 'm_hgrn_g_norm', 'm_hgrn_w_o', 'm_hgrn_lb_logits', 'm_mla_norm', 'm_mla_w_dq', 'm_mla_q_norm', 'm_mla_w_uq', 'm_mla_w_o', 'm_kv_in_norm', 'm_kv_w_dkv', 'm_kv_norm', 'm_kv_w_uk', 'm_kv_w_uv', 'm_mlp_norm', 'm_mlp_w_up', 'm_mlp_w_down', 'm_final_norm', 'v_hgrn_norm', 'v_hgrn_w_q', 'v_hgrn_w_f', 'v_hgrn_w_i', 'v_hgrn_w_g', 'v_hgrn_g_norm', 'v_hgrn_w_o', 'v_hgrn_lb_logits', 'v_mla_norm', 'v_mla_w_dq', 'v_mla_q_norm', 'v_mla_w_uq', 'v_mla_w_o', 'v_kv_in_norm', 'v_kv_w_dkv', 'v_kv_norm', 'v_kv_w_uk', 'v_kv_w_uv', 'v_mlp_norm', 'v_mlp_w_up', 'v_mlp_w_down', 'v_final_norm']
TWIN_OUTPUTS = ['loss', 'grad_x', 'grad_hgrn_norm', 'grad_hgrn_w_q', 'grad_hgrn_w_f', 'grad_hgrn_w_i', 'grad_hgrn_w_g', 'grad_hgrn_g_norm', 'grad_hgrn_w_o', 'grad_hgrn_lb_logits', 'grad_mla_norm', 'grad_mla_w_dq', 'grad_mla_q_norm', 'grad_mla_w_uq', 'grad_mla_w_o', 'grad_kv_in_norm', 'grad_kv_w_dkv', 'grad_kv_norm', 'grad_kv_w_uk', 'grad_kv_w_uv', 'grad_mlp_norm', 'grad_mlp_w_up', 'grad_mlp_w_down', 'grad_final_norm', 'delta_hgrn_norm', 'delta_hgrn_w_q', 'delta_hgrn_w_f', 'delta_hgrn_w_i', 'delta_hgrn_w_g', 'delta_hgrn_g_norm', 'delta_hgrn_w_o', 'delta_hgrn_lb_logits', 'delta_mla_norm', 'delta_mla_w_dq', 'delta_mla_q_norm', 'delta_mla_w_uq', 'delta_mla_w_o', 'delta_kv_in_norm', 'delta_kv_w_dkv', 'delta_kv_norm', 'delta_kv_w_uk', 'delta_kv_w_uv', 'delta_mlp_norm', 'delta_mlp_w_up', 'delta_mlp_w_down', 'delta_final_norm', 'new_m_hgrn_norm', 'new_m_hgrn_w_q', 'new_m_hgrn_w_f', 'new_m_hgrn_w_i', 'new_m_hgrn_w_g', 'new_m_hgrn_g_norm', 'new_m_hgrn_w_o', 'new_m_hgrn_lb_logits', 'new_m_mla_norm', 'new_m_mla_w_dq', 'new_m_mla_q_norm', 'new_m_mla_w_uq', 'new_m_mla_w_o', 'new_m_kv_in_norm', 'new_m_kv_w_dkv', 'new_m_kv_norm', 'new_m_kv_w_uk', 'new_m_kv_w_uv', 'new_m_mlp_norm', 'new_m_mlp_w_up', 'new_m_mlp_w_down', 'new_m_final_norm', 'new_v_hgrn_norm', 'new_v_hgrn_w_q', 'new_v_hgrn_w_f', 'new_v_hgrn_w_i', 'new_v_hgrn_w_g', 'new_v_hgrn_g_norm', 'new_v_hgrn_w_o', 'new_v_hgrn_lb_logits', 'new_v_mla_norm', 'new_v_mla_w_dq', 'new_v_mla_q_norm', 'new_v_mla_w_uq', 'new_v_mla_w_o', 'new_v_kv_in_norm', 'new_v_kv_w_dkv', 'new_v_kv_norm', 'new_v_kv_w_uk', 'new_v_kv_w_uv', 'new_v_mlp_norm', 'new_v_mlp_w_up', 'new_v_mlp_w_down', 'new_v_final_norm']
TWIN_LEAF_KINDS = {'loss': 'loss', 'grad_x': 'grad_x', 'grad_hgrn_norm': 'grad_w', 'grad_hgrn_w_q': 'grad_w', 'grad_hgrn_w_f': 'grad_w', 'grad_hgrn_w_i': 'grad_w', 'grad_hgrn_w_g': 'grad_w', 'grad_hgrn_g_norm': 'grad_w', 'grad_hgrn_w_o': 'grad_w', 'grad_hgrn_lb_logits': 'grad_w', 'grad_mla_norm': 'grad_w', 'grad_mla_w_dq': 'grad_w', 'grad_mla_q_norm': 'grad_w', 'grad_mla_w_uq': 'grad_w', 'grad_mla_w_o': 'grad_w', 'grad_kv_in_norm': 'grad_w', 'grad_kv_w_dkv': 'grad_w', 'grad_kv_norm': 'grad_w', 'grad_kv_w_uk': 'grad_w', 'grad_kv_w_uv': 'grad_w', 'grad_mlp_norm': 'grad_w', 'grad_mlp_w_up': 'grad_w', 'grad_mlp_w_down': 'grad_w', 'grad_final_norm': 'grad_w', 'delta_hgrn_norm': 'delta_w', 'delta_hgrn_w_q': 'delta_w', 'delta_hgrn_w_f': 'delta_w', 'delta_hgrn_w_i': 'delta_w', 'delta_hgrn_w_g': 'delta_w', 'delta_hgrn_g_norm': 'delta_w', 'delta_hgrn_w_o': 'delta_w', 'delta_hgrn_lb_logits': 'delta_w', 'delta_mla_norm': 'delta_w', 'delta_mla_w_dq': 'delta_w', 'delta_mla_q_norm': 'delta_w', 'delta_mla_w_uq': 'delta_w', 'delta_mla_w_o': 'delta_w', 'delta_kv_in_norm': 'delta_w', 'delta_kv_w_dkv': 'delta_w', 'delta_kv_norm': 'delta_w', 'delta_kv_w_uk': 'delta_w', 'delta_kv_w_uv': 'delta_w', 'delta_mlp_norm': 'delta_w', 'delta_mlp_w_up': 'delta_w', 'delta_mlp_w_down': 'delta_w', 'delta_final_norm': 'delta_w', 'new_m_hgrn_norm': 'new_m', 'new_m_hgrn_w_q': 'new_m', 'new_m_hgrn_w_f': 'new_m', 'new_m_hgrn_w_i': 'new_m', 'new_m_hgrn_w_g': 'new_m', 'new_m_hgrn_g_norm': 'new_m', 'new_m_hgrn_w_o': 'new_m', 'new_m_hgrn_lb_logits': 'new_m', 'new_m_mla_norm': 'new_m', 'new_m_mla_w_dq': 'new_m', 'new_m_mla_q_norm': 'new_m', 'new_m_mla_w_uq': 'new_m', 'new_m_mla_w_o': 'new_m', 'new_m_kv_in_norm': 'new_m', 'new_m_kv_w_dkv': 'new_m', 'new_m_kv_norm': 'new_m', 'new_m_kv_w_uk': 'new_m', 'new_m_kv_w_uv': 'new_m', 'new_m_mlp_norm': 'new_m', 'new_m_mlp_w_up': 'new_m', 'new_m_mlp_w_down': 'new_m', 'new_m_final_norm': 'new_m', 'new_v_hgrn_norm': 'new_v', 'new_v_hgrn_w_q': 'new_v', 'new_v_hgrn_w_f': 'new_v', 'new_v_hgrn_w_i': 'new_v', 'new_v_hgrn_w_g': 'new_v', 'new_v_hgrn_g_norm': 'new_v', 'new_v_hgrn_w_o': 'new_v', 'new_v_hgrn_lb_logits': 'new_v', 'new_v_mla_norm': 'new_v', 'new_v_mla_w_dq': 'new_v', 'new_v_mla_q_norm': 'new_v', 'new_v_mla_w_uq': 'new_v', 'new_v_mla_w_o': 'new_v', 'new_v_kv_in_norm': 'new_v', 'new_v_kv_w_dkv': 'new_v', 'new_v_kv_norm': 'new_v', 'new_v_kv_w_uk': 'new_v', 'new_v_kv_w_uv': 'new_v', 'new_v_mlp_norm': 'new_v', 'new_v_mlp_w_up': 'new_v', 'new_v_mlp_w_down': 'new_v', 'new_v_final_norm': 'new_v'}


def _forward(args):
    return _fwd_reference(*[args[k] for k in FWD_PARAMS])


def _output_shape():
    out = _jax.eval_shape(lambda: _forward(_fwd_setup_inputs(0)))
    return out.shape, out.dtype

N_MICROBATCH = 1
ADAM_LR = 0.001
ADAM_B1 = 0.9
ADAM_B2 = 0.999
ADAM_EPS = 1e-08
ADAM_WD = 0.01
ADAM_STEP = 10
PER_EXAMPLE_BATCH_AXIS = {'x': 0, 'loss_target': 0}
SHARED_INPUTS = []
_WEIGHT_DTYPES = {'hgrn_norm': _jnp.float32, 'hgrn_w_q': _jnp.float32, 'hgrn_w_f': _jnp.float32, 'hgrn_w_i': _jnp.float32, 'hgrn_w_g': _jnp.float32, 'hgrn_g_norm': _jnp.float32, 'hgrn_w_o': _jnp.float32, 'hgrn_lb_logits': _jnp.float32, 'mla_norm': _jnp.float32, 'mla_w_dq': _jnp.float32, 'mla_q_norm': _jnp.float32, 'mla_w_uq': _jnp.float32, 'mla_w_o': _jnp.float32, 'kv_in_norm': _jnp.float32, 'kv_w_dkv': _jnp.float32, 'kv_norm': _jnp.float32, 'kv_w_uk': _jnp.float32, 'kv_w_uv': _jnp.float32, 'mlp_norm': _jnp.float32, 'mlp_w_up': _jnp.float32, 'mlp_w_down': _jnp.float32, 'final_norm': _jnp.float32}
MOMENT_SCALE = {'hgrn_norm': 1.497393e-01, 'hgrn_w_q': 1.448632e-02, 'hgrn_w_f': 1.302425e-02, 'hgrn_w_i': 1.042175e-01, 'hgrn_w_g': 1.069296e-01, 'hgrn_g_norm': 3.419922e-01, 'hgrn_w_o': 1.032867e-01, 'hgrn_lb_logits': 9.214745e-03, 'mla_norm': 1.765774e-02, 'mla_w_dq': 3.562958e-02, 'mla_q_norm': 4.075413e-02, 'mla_w_uq': 1.027383e-02, 'mla_w_o': 4.711409e-02, 'kv_in_norm': 4.530338e-02, 'kv_w_dkv': 8.664753e-02, 'kv_norm': 1.115471e-01, 'kv_w_uk': 1.046282e-02, 'kv_w_uv': 3.585610e-02, 'mlp_norm': 1.420252e-01, 'mlp_w_up': 6.979030e-02, 'mlp_w_down': 1.436499e-01, 'final_norm': 3.262511e+01}


def _to_microbatches(a, axis):
    t = _jnp.moveaxis(a, axis, 0)
    t = t.reshape((N_MICROBATCH, t.shape[0] // N_MICROBATCH) + t.shape[1:])
    return _jnp.moveaxis(t, 1, axis + 1)


def setup_inputs(seed: int = 0) -> dict:
    inp = _fwd_setup_inputs(seed)
    key = _jax.random.fold_in(_jax.random.key(seed), 7919)
    shape, _ = _output_shape()
    out = dict(inp)
    out["loss_target"] = _jax.random.normal(_jax.random.fold_in(key, 0), shape, _jnp.float32)
    for i, name in enumerate(TWIN_WEIGHTS):
        w = inp[name].astype(_jnp.float32)
        if MOMENT_SCALE is None:
            s = _jnp.sqrt(_jnp.mean(_jnp.square(w)) + 1e-30)
        else:
            s = MOMENT_SCALE[name]
        km, kv = _jax.random.split(_jax.random.fold_in(key, i + 1))
        out[name] = w
        out["m_" + name] = s * _jax.random.normal(km, w.shape, _jnp.float32)
        out["v_" + name] = (s * s) * _jax.random.uniform(kv, w.shape, _jnp.float32, 0.5, 1.5)
    if N_MICROBATCH > 1:
        for name, axis in PER_EXAMPLE_BATCH_AXIS.items():
            out[name] = _to_microbatches(out[name], axis)
    return {'x': out['x'], 'hgrn_norm': out['hgrn_norm'], 'hgrn_w_q': out['hgrn_w_q'], 'hgrn_w_f': out['hgrn_w_f'], 'hgrn_w_i': out['hgrn_w_i'], 'hgrn_w_g': out['hgrn_w_g'], 'hgrn_g_norm': out['hgrn_g_norm'], 'hgrn_w_o': out['hgrn_w_o'], 'hgrn_lb_logits': out['hgrn_lb_logits'], 'mla_norm': out['mla_norm'], 'mla_w_dq': out['mla_w_dq'], 'mla_q_norm': out['mla_q_norm'], 'mla_w_uq': out['mla_w_uq'], 'mla_w_o': out['mla_w_o'], 'kv_in_norm': out['kv_in_norm'], 'kv_w_dkv': out['kv_w_dkv'], 'kv_norm': out['kv_norm'], 'kv_w_uk': out['kv_w_uk'], 'kv_w_uv': out['kv_w_uv'], 'mlp_norm': out['mlp_norm'], 'mlp_w_up': out['mlp_w_up'], 'mlp_w_down': out['mlp_w_down'], 'final_norm': out['final_norm'], 'loss_target': out['loss_target'], 'm_hgrn_norm': out['m_hgrn_norm'], 'm_hgrn_w_q': out['m_hgrn_w_q'], 'm_hgrn_w_f': out['m_hgrn_w_f'], 'm_hgrn_w_i': out['m_hgrn_w_i'], 'm_hgrn_w_g': out['m_hgrn_w_g'], 'm_hgrn_g_norm': out['m_hgrn_g_norm'], 'm_hgrn_w_o': out['m_hgrn_w_o'], 'm_hgrn_lb_logits': out['m_hgrn_lb_logits'], 'm_mla_norm': out['m_mla_norm'], 'm_mla_w_dq': out['m_mla_w_dq'], 'm_mla_q_norm': out['m_mla_q_norm'], 'm_mla_w_uq': out['m_mla_w_uq'], 'm_mla_w_o': out['m_mla_w_o'], 'm_kv_in_norm': out['m_kv_in_norm'], 'm_kv_w_dkv': out['m_kv_w_dkv'], 'm_kv_norm': out['m_kv_norm'], 'm_kv_w_uk': out['m_kv_w_uk'], 'm_kv_w_uv': out['m_kv_w_uv'], 'm_mlp_norm': out['m_mlp_norm'], 'm_mlp_w_up': out['m_mlp_w_up'], 'm_mlp_w_down': out['m_mlp_w_down'], 'm_final_norm': out['m_final_norm'], 'v_hgrn_norm': out['v_hgrn_norm'], 'v_hgrn_w_q': out['v_hgrn_w_q'], 'v_hgrn_w_f': out['v_hgrn_w_f'], 'v_hgrn_w_i': out['v_hgrn_w_i'], 'v_hgrn_w_g': out['v_hgrn_w_g'], 'v_hgrn_g_norm': out['v_hgrn_g_norm'], 'v_hgrn_w_o': out['v_hgrn_w_o'], 'v_hgrn_lb_logits': out['v_hgrn_lb_logits'], 'v_mla_norm': out['v_mla_norm'], 'v_mla_w_dq': out['v_mla_w_dq'], 'v_mla_q_norm': out['v_mla_q_norm'], 'v_mla_w_uq': out['v_mla_w_uq'], 'v_mla_w_o': out['v_mla_w_o'], 'v_kv_in_norm': out['v_kv_in_norm'], 'v_kv_w_dkv': out['v_kv_w_dkv'], 'v_kv_norm': out['v_kv_norm'], 'v_kv_w_uk': out['v_kv_w_uk'], 'v_kv_w_uv': out['v_kv_w_uv'], 'v_mlp_norm': out['v_mlp_norm'], 'v_mlp_w_up': out['v_mlp_w_up'], 'v_mlp_w_down': out['v_mlp_w_down'], 'v_final_norm': out['v_final_norm']}


def _loss(weights, diff, rest, loss_target):
    with _jax.named_scope("forward"):
        args = {**rest, TWIN_DIFF_INPUT: diff, **{k: w.astype(_WEIGHT_DTYPES[k]) for k, w in weights.items()}}
        y = _forward(args)
    with _jax.named_scope("loss_head"):
        err = _jnp.square(y.astype(_jnp.float32) - loss_target)
        return 0.5 * _jnp.sum(_jnp.mean(err, axis=-1)) if err.ndim else 0.5 * err


def _adamw(w, g, m, v):
    m = ADAM_B1 * m + (1.0 - ADAM_B1) * g
    v = ADAM_B2 * v + (1.0 - ADAM_B2) * _jnp.square(g)
    m_hat = m / (1.0 - ADAM_B1 ** ADAM_STEP)
    v_hat = v / (1.0 - ADAM_B2 ** ADAM_STEP)
    delta = -ADAM_LR * (m_hat / (_jnp.sqrt(v_hat) + ADAM_EPS) + ADAM_WD * w)
    return delta, m, v


def reference(x, hgrn_norm, hgrn_w_q, hgrn_w_f, hgrn_w_i, hgrn_w_g, hgrn_g_norm, hgrn_w_o, hgrn_lb_logits, mla_norm, mla_w_dq, mla_q_norm, mla_w_uq, mla_w_o, kv_in_norm, kv_w_dkv, kv_norm, kv_w_uk, kv_w_uv, mlp_norm, mlp_w_up, mlp_w_down, final_norm, loss_target, m_hgrn_norm, m_hgrn_w_q, m_hgrn_w_f, m_hgrn_w_i, m_hgrn_w_g, m_hgrn_g_norm, m_hgrn_w_o, m_hgrn_lb_logits, m_mla_norm, m_mla_w_dq, m_mla_q_norm, m_mla_w_uq, m_mla_w_o, m_kv_in_norm, m_kv_w_dkv, m_kv_norm, m_kv_w_uk, m_kv_w_uv, m_mlp_norm, m_mlp_w_up, m_mlp_w_down, m_final_norm, v_hgrn_norm, v_hgrn_w_q, v_hgrn_w_f, v_hgrn_w_i, v_hgrn_w_g, v_hgrn_g_norm, v_hgrn_w_o, v_hgrn_lb_logits, v_mla_norm, v_mla_w_dq, v_mla_q_norm, v_mla_w_uq, v_mla_w_o, v_kv_in_norm, v_kv_w_dkv, v_kv_norm, v_kv_w_uk, v_kv_w_uv, v_mlp_norm, v_mlp_w_up, v_mlp_w_down, v_final_norm):
    given = dict(x=x, hgrn_norm=hgrn_norm, hgrn_w_q=hgrn_w_q, hgrn_w_f=hgrn_w_f, hgrn_w_i=hgrn_w_i, hgrn_w_g=hgrn_w_g, hgrn_g_norm=hgrn_g_norm, hgrn_w_o=hgrn_w_o, hgrn_lb_logits=hgrn_lb_logits, mla_norm=mla_norm, mla_w_dq=mla_w_dq, mla_q_norm=mla_q_norm, mla_w_uq=mla_w_uq, mla_w_o=mla_w_o, kv_in_norm=kv_in_norm, kv_w_dkv=kv_w_dkv, kv_norm=kv_norm, kv_w_uk=kv_w_uk, kv_w_uv=kv_w_uv, mlp_norm=mlp_norm, mlp_w_up=mlp_w_up, mlp_w_down=mlp_w_down, final_norm=final_norm, loss_target=loss_target, m_hgrn_norm=m_hgrn_norm, m_hgrn_w_q=m_hgrn_w_q, m_hgrn_w_f=m_hgrn_w_f, m_hgrn_w_i=m_hgrn_w_i, m_hgrn_w_g=m_hgrn_w_g, m_hgrn_g_norm=m_hgrn_g_norm, m_hgrn_w_o=m_hgrn_w_o, m_hgrn_lb_logits=m_hgrn_lb_logits, m_mla_norm=m_mla_norm, m_mla_w_dq=m_mla_w_dq, m_mla_q_norm=m_mla_q_norm, m_mla_w_uq=m_mla_w_uq, m_mla_w_o=m_mla_w_o, m_kv_in_norm=m_kv_in_norm, m_kv_w_dkv=m_kv_w_dkv, m_kv_norm=m_kv_norm, m_kv_w_uk=m_kv_w_uk, m_kv_w_uv=m_kv_w_uv, m_mlp_norm=m_mlp_norm, m_mlp_w_up=m_mlp_w_up, m_mlp_w_down=m_mlp_w_down, m_final_norm=m_final_norm, v_hgrn_norm=v_hgrn_norm, v_hgrn_w_q=v_hgrn_w_q, v_hgrn_w_f=v_hgrn_w_f, v_hgrn_w_i=v_hgrn_w_i, v_hgrn_w_g=v_hgrn_w_g, v_hgrn_g_norm=v_hgrn_g_norm, v_hgrn_w_o=v_hgrn_w_o, v_hgrn_lb_logits=v_hgrn_lb_logits, v_mla_norm=v_mla_norm, v_mla_w_dq=v_mla_w_dq, v_mla_q_norm=v_mla_q_norm, v_mla_w_uq=v_mla_w_uq, v_mla_w_o=v_mla_w_o, v_kv_in_norm=v_kv_in_norm, v_kv_w_dkv=v_kv_w_dkv, v_kv_norm=v_kv_norm, v_kv_w_uk=v_kv_w_uk, v_kv_w_uv=v_kv_w_uv, v_mlp_norm=v_mlp_norm, v_mlp_w_up=v_mlp_w_up, v_mlp_w_down=v_mlp_w_down, v_final_norm=v_final_norm)
    weights = {n: given[n] for n in TWIN_WEIGHTS}
    shared = {n: given[n] for n in SHARED_INPUTS}
    per_example = {n: given[n] for n in ['x']}
    grad_fn = _jax.value_and_grad(_loss, argnums=(0, 1))

    def one_microbatch(ex, loss_target):
        ex = dict(ex)
        diff = ex.pop(TWIN_DIFF_INPUT)
        return grad_fn(weights, diff, {**shared, **ex}, loss_target)

    if N_MICROBATCH == 1:
        loss, (grad_w, grad_x) = one_microbatch(per_example, given["loss_target"])
    else:
        def body(carry, xs):
            loss_sum, grad_sum = carry
            l_k, (gw_k, gx_k) = one_microbatch(xs[0], xs[1])
            with _jax.named_scope("update"):
                return (loss_sum + l_k, _jax.tree.map(_jnp.add, grad_sum, gw_k)), gx_k

        init = (_jnp.zeros((), _jnp.float32), _jax.tree.map(_jnp.zeros_like, weights))
        (loss, grad_w), grad_x = _jax.lax.scan(body, init, (per_example, given["loss_target"]))
    with _jax.named_scope("update"):
        delta_w, new_m, new_v = {}, {}, {}
        for n in TWIN_WEIGHTS:
            delta_w[n], new_m[n], new_v[n] = _adamw(weights[n], grad_w[n], given["m_" + n], given["v_" + n])
    return (loss, grad_x, *[grad_w[n] for n in TWIN_WEIGHTS], *[delta_w[n] for n in TWIN_WEIGHTS],
            *[new_m[n] for n in TWIN_WEIGHTS], *[new_v[n] for n in TWIN_WEIGHTS])
```

```python
import functools

import jax
import jax.numpy as jnp
from jax import lax
from jax.experimental import pallas as pl
from jax.experimental.pallas import tpu as pltpu

f32, bf16 = jnp.float32, jnp.bfloat16
HI = lax.Precision.HIGHEST
MESH = pl.DeviceIdType.MESH

D_MODEL = 1024
D_FF = 4096
EPS = 1e-6
HGRN_HEADS, HGRN_DK, HGRN_CHUNK, HGRN_SUB = 8, 128, 64, 16
MLA_HEADS, MLA_NOPE, MLA_ROPE, MLA_V = 16, 128, 64, 128
MLA_Q_LORA, MLA_KV_LORA = 256, 256
ROPE_THETA = 10000.0
ATT_SCALE = (MLA_NOPE + MLA_ROPE) ** -0.5
EXP_CLAMP = 80.0

ADAM_LR, ADAM_B1, ADAM_B2, ADAM_EPS, ADAM_WD, ADAM_STEP = 0.001, 0.9, 0.999, 1e-08, 0.01, 10

V7X_VMEM_BYTES = 64 * 1024 * 1024
VMEM_LIMIT = V7X_VMEM_BYTES - 8 * 1024 * 1024
LANES = 128

PACK = (("hgrn_w_q", 256), ("hgrn_w_f", 256), ("hgrn_w_i", 256), ("hgrn_w_g", 256), ("hgrn_w_o", 256),
        ("mla_w_dq", 64), ("mla_w_uq", 192), ("mla_w_o", 512), ("kv_w_dkv", 80), ("kv_w_uk", 128),
        ("kv_w_uv", 128), ("mlp_w_up", 2048), ("mlp_w_down", 2048))
PACK_ROWS = sum(r for _, r in PACK)
PACK_PAD = 16
ROWS = PACK_ROWS + PACK_PAD
HALF = ROWS // 2
PACK_OFF = {}
_o = 0
for _n, _r in PACK:
    PACK_OFF[_n] = (_o, _r)
    _o += _r

WEIGHTS = ("hgrn_norm", "hgrn_w_q", "hgrn_w_f", "hgrn_w_i", "hgrn_w_g", "hgrn_g_norm", "hgrn_w_o", "hgrn_lb_logits",
           "mla_norm", "mla_w_dq", "mla_q_norm", "mla_w_uq", "mla_w_o", "kv_in_norm", "kv_w_dkv", "kv_norm", "kv_w_uk",
           "kv_w_uv", "mlp_norm", "mlp_w_up", "mlp_w_down", "final_norm")
SMALL = (("hgrn_norm", 0, 1, 1024), ("hgrn_lb_logits", 1, 2, 1024), ("hgrn_g_norm", 3, 1, 128),
         ("mla_norm", 4, 1, 1024), ("mla_q_norm", 5, 1, 256), ("kv_in_norm", 6, 1, 1024), ("kv_norm", 7, 1, 256),
         ("mlp_norm", 8, 2, 1024), ("final_norm", 10, 1, 1024))
SMALL_ROWS = 16


def _pc(body, *, name, out_shape, grid=None, in_specs=None, out_specs=None, scratch=(), sem=None, grid_spec=None):
    params = pltpu.CompilerParams(dimension_semantics=sem, vmem_limit_bytes=VMEM_LIMIT)
    if grid_spec is not None:
        return pl.pallas_call(body, name=name, out_shape=out_shape, grid_spec=grid_spec, compiler_params=params,
                              interpret=False)
    kw = {k: v for k, v in (("grid", grid), ("in_specs", in_specs), ("out_specs", out_specs)) if v is not None}
    return pl.pallas_call(body, name=name, out_shape=out_shape, scratch_shapes=list(scratch), compiler_params=params,
                          interpret=False, **kw)


def _sds(shape, dtype):
    return jax.ShapeDtypeStruct(tuple(shape), dtype)


def _mm(a, b, *, name, ta=False, tb=False, outs=(f32,), epi=None, extras=(), tm=1024, tn=1024, tk=512):
    m, k = (a.shape[1], a.shape[0]) if ta else a.shape
    n = b.shape[0] if tb else b.shape[1]
    tm, tn, tk = min(tm, m), min(tn, n), min(tk, k)
    assert m % tm == 0 and n % tn == 0 and k % tk == 0, (name, m, n, k)
    nk = k // tk
    a_spec = pl.BlockSpec((tk, tm), lambda i, j, kk: (kk, i)) if ta else pl.BlockSpec((tm, tk), lambda i, j, kk: (i, kk))
    b_spec = pl.BlockSpec((tn, tk), lambda i, j, kk: (j, kk)) if tb else pl.BlockSpec((tk, tn), lambda i, j, kk: (kk, j))
    e_specs = [pl.BlockSpec((tm, tn), lambda i, j, kk: (i, j)) for _ in extras]
    dn = (((0 if ta else 1,), (1 if tb else 0,)), ((), ()))
    n_e, n_o = len(extras), len(outs)

    def body(*refs):
        a_ref, b_ref = refs[0], refs[1]
        e_refs = refs[2:2 + n_e]
        o_refs = refs[2 + n_e:2 + n_e + n_o]
        acc = refs[-1]
        kk = pl.program_id(2)

        @pl.when(kk == 0)
        def _():
            acc[...] = jnp.zeros_like(acc)

        acc[...] += lax.dot_general(a_ref[...].astype(bf16), b_ref[...].astype(bf16), dn, preferred_element_type=f32)

        @pl.when(kk == nk - 1)
        def _():
            r = acc[...]
            res = epi(r, *[e[...] for e in e_refs]) if epi is not None else (r,)
            for o, v in zip(o_refs, res):
                o[...] = v.astype(o.dtype)

    out = _pc(body, name=name, grid=(m // tm, n // tn, nk),
              in_specs=[a_spec, b_spec] + e_specs,
              out_specs=[pl.BlockSpec((tm, tn), lambda i, j, kk: (i, j)) for _ in outs],
              out_shape=[_sds((m, n), dt) for dt in outs],
              scratch=[pltpu.VMEM((tm, tn), f32)],
              sem=("parallel", "parallel", "arbitrary"))(a, b, *extras)
    return out[0] if n_o == 1 else out


def _rw(fn, rows, bcast, outs, accs=(), *, name, tr=256):
    t = rows[0].shape[0]
    tr = min(tr, t)
    assert t % tr == 0
    n_r, n_b, n_o, n_a = len(rows), len(bcast), len(outs), len(accs)

    def body(*refs):
        r_refs = refs[:n_r]
        b_refs = refs[n_r:n_r + n_b]
        o_refs = refs[n_r + n_b:n_r + n_b + n_o]
        a_refs = refs[n_r + n_b + n_o:]
        res = fn(*[r[...] for r in r_refs], *[b[...] for b in b_refs])
        for o, v in zip(o_refs, res[:n_o]):
            o[...] = v.astype(o.dtype)
        i = pl.program_id(0)
        for a_ref, v in zip(a_refs, res[n_o:]):
            @pl.when(i == 0)
            def _(a_ref=a_ref):
                a_ref[...] = jnp.zeros_like(a_ref)
            a_ref[...] += v

    in_specs = [pl.BlockSpec((tr, r.shape[1]), lambda i: (i, 0)) for r in rows]
    in_specs += [pl.BlockSpec(b.shape, lambda i: (0, 0)) for b in bcast]
    out_specs = [pl.BlockSpec((tr, w), lambda i: (i, 0)) for w, _ in outs]
    out_specs += [pl.BlockSpec(s, lambda i: (0, 0)) for s in accs]
    out_shape = [_sds((t, w), dt) for w, dt in outs] + [_sds(s, f32) for s in accs]
    res = _pc(body, name=name, grid=(t // tr,), in_specs=in_specs, out_specs=out_specs, out_shape=out_shape,
              sem=("arbitrary",))(*rows, *bcast)
    return res


def _rms(x, gain):
    return x * lax.rsqrt(jnp.mean(x * x, axis=-1, keepdims=True) + EPS) * gain


def _rms_bwd(x, gain, dy):
    _, vjp = jax.vjp(_rms, x, gain)
    return vjp(dy)


def _lower_bound(lbl):
    l0, l1 = lbl[0:1, :], lbl[1:2, :]
    mx = jnp.maximum(l0, l1)
    e0, e1 = jnp.exp(l0 - mx), jnp.exp(l1 - mx)
    return e0 / (e0 + e1)


def _gates(qpre, fpre, lbl):
    lb = _lower_bound(lbl)
    q = jax.nn.silu(qpre)
    forget = lb + (1.0 - lb) * jax.nn.sigmoid(fpre)
    return q, 1.0 - forget, jnp.log(forget)


def _head_norm_gate(o, gpre, gn):
    return _rms(o, gn) * jax.nn.silu(gpre)


def _swap_halves(x):
    w = x.shape[1]
    lane = lax.broadcasted_iota(jnp.int32, x.shape, 1)
    return jnp.where((lane % MLA_ROPE) < MLA_ROPE // 2, pltpu.roll(x, w - MLA_ROPE // 2, 1),
                     pltpu.roll(x, MLA_ROPE // 2, 1))


def _tile_lanes(tab, w):
    return tab if w == LANES else jnp.concatenate([tab] * (w // LANES), axis=1)


def _rope(x, cos, sgn_sin, sign=1.0):
    w = x.shape[1]
    return x * _tile_lanes(cos, w) + sign * _swap_halves(x) * _tile_lanes(sgn_sin, w)


def _bd(a, b, ca, cb):
    return lax.dot_general(a.astype(bf16), b.astype(bf16), (((ca,), (cb,)), ((), ())), preferred_element_type=f32)


@jax.custom_vjp
def _dot_nn(a, b):
    return _bd(a, b, 1, 0)


@jax.custom_vjp
def _dot_nt(a, b):
    return _bd(a, b, 1, 1)


@jax.custom_vjp
def _dot_tn(a, b):
    return _bd(a, b, 0, 0)


_dot_nn.defvjp(lambda a, b: (_bd(a, b, 1, 0), (a, b)), lambda r, g: (_bd(g, r[1], 1, 1), _bd(r[0], g, 0, 0)))
_dot_nt.defvjp(lambda a, b: (_bd(a, b, 1, 1), (a, b)), lambda r, g: (_bd(g, r[1], 1, 0), _bd(g, r[0], 0, 0)))
_dot_tn.defvjp(lambda a, b: (_bd(a, b, 0, 0), (a, b)), lambda r, g: (_bd(r[1], g, 1, 1), _bd(r[0], g, 1, 0)))


def _hdot(c, g):
    return jnp.dot(c, g, precision=HI, preferred_element_type=f32)


def _gla_consts():
    c, s = HGRN_CHUNK, HGRN_SUB
    row = lax.broadcasted_iota(jnp.int32, (c, c), 0)
    col = lax.broadcasted_iota(jnp.int32, (c, c), 1)
    incl = (col <= row).astype(f32)
    after = (col > row).astype(f32)
    ones = jnp.ones((HGRN_DK, c), f32)
    qs, ks, masks = [], [], []
    for i in range(c // s):
        n = s * (i + 1)
        r = lax.broadcasted_iota(jnp.int32, (s, c), 0) + s * i
        cc = lax.broadcasted_iota(jnp.int32, (s, c), 1)
        qs.append(((cc >= s * i) & (cc <= r)).astype(f32))
        rk = lax.broadcasted_iota(jnp.int32, (n, c), 0)
        ck = lax.broadcasted_iota(jnp.int32, (n, c), 1)
        ks.append((ck < s * i).astype(f32) - (ck <= rk).astype(f32))
        mr = lax.broadcasted_iota(jnp.int32, (s, n), 0) + s * i
        mc = lax.broadcasted_iota(jnp.int32, (s, n), 1)
        masks.append(mc <= mr)
    return incl, after, ones, qs, ks, masks


def _gla_chunk(consts, dots, q_sub, k_sub, v_sub, g, st):
    incl, after, ones, qmats, kmats, masks = consts
    dot_nn, dot_nt, dot_tn = dots
    q_all, k_all, v_all = (jnp.concatenate(t, axis=0) for t in (q_sub, k_sub, v_sub))
    o_inter = dot_nt(q_all * jnp.exp(_hdot(incl, g)), st)
    st_new = st * jnp.exp(_hdot(ones, g)) + dot_tn(v_all, k_all * jnp.exp(_hdot(after, g)))
    intra = []
    for i in range(len(q_sub)):
        qt = q_sub[i] * jnp.exp(_hdot(qmats[i], g))
        kt = jnp.concatenate(k_sub[:i + 1], axis=0) * jnp.exp(jnp.minimum(_hdot(kmats[i], g), EXP_CLAMP))
        sc = jnp.where(masks[i], dot_nt(qt, kt), 0.0)
        intra.append(dot_nn(sc, jnp.concatenate(v_sub[:i + 1], axis=0)))
    return o_inter + jnp.concatenate(intra, axis=0), st_new


_PLAIN_DOTS = (lambda a, b: _bd(a, b, 1, 0), lambda a, b: _bd(a, b, 1, 1), lambda a, b: _bd(a, b, 0, 0))
_VJP_DOTS = (_dot_nn, _dot_nt, _dot_tn)


def _subs(ref):
    return [ref[HGRN_SUB * i:HGRN_SUB * (i + 1), :] for i in range(HGRN_CHUNK // HGRN_SUB)]


def _gla_fwd(q, k, p4, g):
    t = q.shape[0]
    nc = t // HGRN_CHUNK

    def body(q_ref, k_ref, v_ref, g_ref, o_ref, s_ref, st):
        @pl.when(pl.program_id(1) == 0)
        def _():
            st[...] = jnp.zeros_like(st)

        s_ref[0, 0] = st[...]
        o, st_new = _gla_chunk(_gla_consts(), _PLAIN_DOTS, _subs(q_ref), _subs(k_ref), _subs(v_ref), g_ref[...], st[...])
        o_ref[...] = o
        st[...] = st_new

    blk = lambda off: pl.BlockSpec((HGRN_CHUNK, HGRN_DK), lambda h, c: (c, h + off))
    return _pc(body, name="gla_fwd", grid=(HGRN_HEADS, nc),
               in_specs=[blk(0), blk(0), blk(2 * HGRN_HEADS), blk(0)],
               out_specs=[blk(0), pl.BlockSpec((1, 1, HGRN_DK, HGRN_DK), lambda h, c: (h, c, 0, 0))],
               out_shape=[_sds((t, D_MODEL), f32), _sds((HGRN_HEADS, nc, HGRN_DK, HGRN_DK), f32)],
               scratch=[pltpu.VMEM((HGRN_DK, HGRN_DK), f32)], sem=("arbitrary", "arbitrary"))(q, k, p4, g)


def _gla_bwd(q, k, p4, g, states, do):
    t = q.shape[0]
    nc = t // HGRN_CHUNK
    ns = HGRN_CHUNK // HGRN_SUB

    def body(q_ref, k_ref, v_ref, g_ref, s_ref, do_ref, dq_ref, dk_ref, dv_ref, dg_ref, dst):
        @pl.when(pl.program_id(1) == 0)
        def _():
            dst[...] = jnp.zeros_like(dst)

        consts = _gla_consts()
        fn = lambda qs, ks, vs, gg, ss: _gla_chunk(consts, _VJP_DOTS, qs, ks, vs, gg, ss)
        _, vjp = jax.vjp(fn, _subs(q_ref), _subs(k_ref), _subs(v_ref), g_ref[...], s_ref[0, 0])
        dqs, dks, dvs, dg, ds = vjp((do_ref[...], dst[...]))
        for i in range(ns):
            rows = slice(HGRN_SUB * i, HGRN_SUB * (i + 1))
            dq_ref[rows, :] = dqs[i]
            dk_ref[rows, :] = dks[i]
            dv_ref[rows, :] = dvs[i]
        dg_ref[...] = dg
        dst[...] = ds

    blk = lambda off: pl.BlockSpec((HGRN_CHUNK, HGRN_DK), lambda h, c: (nc - 1 - c, h + off))
    return _pc(body, name="gla_bwd", grid=(HGRN_HEADS, nc),
               in_specs=[blk(0), blk(0), blk(2 * HGRN_HEADS), blk(0),
                         pl.BlockSpec((1, 1, HGRN_DK, HGRN_DK), lambda h, c: (h, nc - 1 - c, 0, 0)), blk(0)],
               out_specs=[blk(0)] * 4, out_shape=[_sds((t, D_MODEL), f32)] * 4,
               scratch=[pltpu.VMEM((HGRN_DK, HGRN_DK), f32)], sem=("arbitrary", "arbitrary"))(q, k, p4, g, states, do)


ATT_TQ, ATT_TK = 256, 512
NEG = -1e30


def _pair_masks(shape):
    lane = lax.broadcasted_iota(jnp.int32, shape, 1)
    return lane < MLA_ROPE, lane >= MLA_ROPE


def _scores(qn, qr, kn, kr, row0, col0):
    s = (_bd(qn, kn, 1, 1) + _bd(qr, kr, 1, 1)) * ATT_SCALE
    row = row0 + lax.broadcasted_iota(jnp.int32, s.shape, 0)
    col = col0 + lax.broadcasted_iota(jnp.int32, s.shape, 1)
    return s, col <= row


def _attn_fwd(qn, qr, knv, kr2):
    t = qn.shape[0]
    tq, tk = min(ATT_TQ, t), min(ATT_TK, t)
    npair = MLA_HEADS // 2

    def body(qn_ref, qr_ref, kn_ref, v_ref, kr_ref, o_ref, lse_ref):
        i = pl.program_id(1)
        qr = qr_ref[...]
        masks = _pair_masks(qr.shape)
        nkv = (i * tq + tq + tk - 1) // tk
        outs, lses = [], []
        for e in range(2):
            cols = slice(LANES * e, LANES * (e + 1))
            qn_e = qn_ref[:, cols]
            qr_e = jnp.where(masks[e], qr, jnp.zeros_like(qr))

            def step(j, carry, cols=cols, qn_e=qn_e, qr_e=qr_e):
                m, l, acc = carry
                ks = pl.ds(pl.multiple_of(j * tk, tk), tk)
                s, ok = _scores(qn_e, qr_e, kn_ref[ks, cols], kr_ref[ks, :], i * tq, j * tk)
                s = jnp.where(ok, s, NEG)
                m_new = jnp.maximum(m, jnp.max(s, axis=-1, keepdims=True))
                p = jnp.exp(s - m_new)
                alpha = jnp.exp(m - m_new)
                l = alpha * l + jnp.sum(p, axis=-1, keepdims=True)
                acc = alpha * acc + _bd(p, v_ref[ks, cols], 1, 0)
                return m_new, l, acc

            init = (jnp.full((tq, 1), NEG, f32), jnp.zeros((tq, 1), f32), jnp.zeros((tq, MLA_V), f32))
            m, l, acc = lax.fori_loop(0, nkv, step, init)
            outs.append(acc / l)
            lses.append(m + jnp.log(l))
        o_ref[...] = jnp.concatenate(outs, axis=1).astype(o_ref.dtype)
        lse_ref[...] = jnp.where(masks[0], lses[0], lses[1])

    return _pc(body, name="attn_fwd", grid=(npair, t // tq),
               in_specs=[pl.BlockSpec((tq, 2 * LANES), lambda p, i: (i, p)),
                         pl.BlockSpec((tq, LANES), lambda p, i: (i, p)),
                         pl.BlockSpec((t, 2 * LANES), lambda p, i: (0, p)),
                         pl.BlockSpec((t, 2 * LANES), lambda p, i: (0, npair + p)),
                         pl.BlockSpec((t, LANES), lambda p, i: (0, 0))],
               out_specs=[pl.BlockSpec((tq, 2 * LANES), lambda p, i: (i, p)),
                          pl.BlockSpec((tq, LANES), lambda p, i: (i, p))],
               out_shape=[_sds((t, MLA_HEADS * MLA_V), bf16), _sds((t, npair * LANES), f32)],
               sem=("arbitrary", "arbitrary"))(qn, qr, knv, knv, kr2)


def _attn_bwd(qn, qr, knv, kr2, do, lse, delta):
    t = qn.shape[0]
    tq, tk = min(ATT_TQ, t), min(ATT_TK, t)
    npair = MLA_HEADS // 2
    nq = t // tq

    def body(qn_ref, qr_ref, do_ref, lse_ref, dl_ref, kn_ref, v_ref, kr_ref, dqn_ref, dqr_ref, dkn_ref, dv_ref, dkr_ref):
        j = pl.program_id(1)

        @pl.when(j == 0)
        def _():
            dqn_ref[...] = jnp.zeros_like(dqn_ref)
            dqr_ref[...] = jnp.zeros_like(dqr_ref)

        kr = kr_ref[...]
        masks = _pair_masks((tq, LANES))
        dkr_tot = jnp.zeros((tk, LANES), f32)
        for e in range(2):
            cols = slice(LANES * e, LANES * (e + 1))
            kn_e, v_e = kn_ref[:, cols], v_ref[:, cols]

            def step(i, carry, e=e, cols=cols, kn_e=kn_e, v_e=v_e):
                dk, dkr, dv = carry
                qs = pl.ds(pl.multiple_of(i * tq, tq), tq)
                qn_e = qn_ref[qs, cols]
                qr_e = jnp.where(masks[e], qr_ref[qs, :], jnp.zeros((tq, LANES), bf16))
                do_e = do_ref[qs, cols]
                lse_e = lse_ref[qs, :][:, MLA_ROPE * e:MLA_ROPE * e + 1]
                dl_e = dl_ref[qs, :][:, MLA_ROPE * e:MLA_ROPE * e + 1]
                s, ok = _scores(qn_e, qr_e, kn_e, kr, i * tq, j * tk)
                p = jnp.where(ok, jnp.exp(s - lse_e), 0.0)
                dv = dv + _bd(p, do_e, 0, 0)
                dp = _bd(do_e, v_e, 1, 1)
                ds = (p * (dp - dl_e) * ATT_SCALE).astype(bf16)
                dk = dk + _bd(ds, qn_e, 0, 0)
                dkr = dkr + _bd(ds, qr_e, 0, 0)
                dqn_ref[qs, cols] += _bd(ds, kn_e, 1, 0)
                dqr_ref[qs, :] += jnp.where(masks[e], _bd(ds, kr, 1, 0), 0.0)
                return dk, dkr, dv

            zero = jnp.zeros((tk, LANES), f32)
            dk, dkr, dv = lax.fori_loop((j * tk) // tq, nq, step, (zero, zero, zero))
            dkn_ref[:, cols] = dk.astype(dkn_ref.dtype)
            dv_ref[:, cols] = dv.astype(dv_ref.dtype)
            dkr_tot = dkr_tot + dkr
        dkr_ref[...] = dkr_tot

    res = lambda w: pl.BlockSpec((t, w), lambda p, j: (0, p))
    return _pc(body, name="attn_bwd", grid=(npair, t // tk),
               in_specs=[res(2 * LANES), res(LANES), res(2 * LANES), res(LANES), res(LANES),
                         pl.BlockSpec((tk, 2 * LANES), lambda p, j: (j, p)),
                         pl.BlockSpec((tk, 2 * LANES), lambda p, j: (j, npair + p)),
                         pl.BlockSpec((tk, LANES), lambda p, j: (j, 0))],
               out_specs=[res(2 * LANES), res(LANES),
                          pl.BlockSpec((tk, 2 * LANES), lambda p, j: (j, p)),
                          pl.BlockSpec((tk, 2 * LANES), lambda p, j: (j, p)),
                          pl.BlockSpec((tk, LANES), lambda p, j: (j, p))],
               out_shape=[_sds((t, MLA_HEADS * MLA_NOPE), f32), _sds((t, npair * LANES), f32),
                          _sds((t, MLA_HEADS * MLA_NOPE), bf16), _sds((t, MLA_HEADS * MLA_V), bf16),
                          _sds((t, npair * LANES), f32)],
               sem=("arbitrary", "arbitrary"))(qn, qr, do, lse, delta, knv, knv, kr2)


def _rope_tables(t):
    half = MLA_ROPE // 2
    inv_freq = ROPE_THETA ** (-jnp.arange(half, dtype=f32) / half)
    ang = jnp.arange(t, dtype=f32)[:, None] * inv_freq[None, :]
    cos, sin = jnp.cos(ang), jnp.sin(ang)
    return jnp.concatenate([cos, cos] * 2, axis=1), jnp.concatenate([-sin, sin] * 2, axis=1)


def _relu2_epi(u):
    r = jnp.maximum(u, 0.0)
    return u, r * r


def _add_epi(r, res):
    return (r + res,)


def _drelu2_epi(da, u):
    return (da * 2.0 * jnp.maximum(u.astype(f32), 0.0),)


def _mlp_fwd(h, gain, w_up, w_down, tag):
    xm = _rw(lambda x, g: (_rms(x, g),), [h], [gain], [(D_MODEL, bf16)], name=f"mlp{tag}_norm")[0]
    u, a = _mm(xm, w_up, name=f"mlp{tag}_up", outs=(bf16, bf16), epi=_relu2_epi)
    h_out = _mm(a, w_down, name=f"mlp{tag}_down", epi=_add_epi, extras=(h,))
    return h_out, (xm, u, a)


def _mlp_bwd(dh, h, gain, w_up, w_down, saved, tag):
    xm, u, a = saved
    du = _mm(dh, w_down, tb=True, name=f"mlp{tag}_dact", outs=(bf16,), epi=_drelu2_epi, extras=(u,))
    d_down = _mm(a, dh, ta=True, name=f"mlp{tag}_dwdown")
    d_up = _mm(xm, du, ta=True, name=f"mlp{tag}_dwup")
    dxm = _mm(du, w_up, tb=True, name=f"mlp{tag}_dxm")

    def fn(x, dy, dres, g):
        dx, dg = _rms_bwd(x, g, dy)
        return dx + dres, dg

    dh_in, d_gain = _rw(fn, [h, dxm, dh], [gain], [(D_MODEL, f32)], [(1, D_MODEL)], name=f"mlp{tag}_dnorm")
    return dh_in, d_gain, d_up, d_down


def _local_step(x, target, w):
    t = x.shape[0]
    cos, sgn_sin = _rope_tables(t)
    grads = {}

    xn0 = _rw(lambda xx, g: (_rms(xx, g),), [x], [w["hgrn_norm"]], [(D_MODEL, bf16)], name="hgrn_norm")[0]
    p4 = _mm(xn0, w["hgrn_w4"], name="hgrn_proj")

    def gates_fn(p, lbl):
        return _gates(p[:, :D_MODEL], p[:, D_MODEL:2 * D_MODEL], lbl)

    q, k, g = _rw(gates_fn, [p4], [w["hgrn_lb_logits"]], [(D_MODEL, f32)] * 3, name="hgrn_gates")
    o, states = _gla_fwd(q, k, p4, g)

    def hn_fn(oo, p, gn):
        ys = [_head_norm_gate(oo[:, LANES * h:LANES * (h + 1)], p[:, 3 * D_MODEL + LANES * h:3 * D_MODEL + LANES * (h + 1)], gn)
              for h in range(HGRN_HEADS)]
        return (jnp.concatenate(ys, axis=1),)

    y = _rw(hn_fn, [o, p4], [w["hgrn_g_norm"]], [(D_MODEL, bf16)], name="hgrn_headnorm")[0]
    h1 = _mm(y, w["hgrn_w_o"], name="hgrn_out", epi=_add_epi, extras=(x,))
    h2, mlp0 = _mlp_fwd(h1, w["mlp_norm"][0:1], w["mlp_w_up"][0], w["mlp_w_down"][0], 0)

    hk, xn1 = _rw(lambda hh, g1, g2: (_rms(hh, g1), _rms(hh, g2)), [h2], [w["kv_in_norm"], w["mla_norm"]],
                  [(D_MODEL, bf16)] * 2, name="kv_mla_norm")
    ckr = _mm(hk, w["kv_w_dkv"], name="kv_down")

    def ckv_fn(c, cs, sn, g):
        kr = _rope(c[:, MLA_KV_LORA:], cs, sn)
        return _rms(c[:, :MLA_KV_LORA], g), kr + pltpu.roll(kr, MLA_ROPE, 1)

    c_kv, kr2 = _rw(ckv_fn, [ckr, cos, sgn_sin], [w["kv_norm"]], [(MLA_KV_LORA, bf16), (LANES, bf16)], name="kv_norm_rope")
    knv = _mm(c_kv, w["kv_w_ukv"], name="kv_up", outs=(bf16,))
    cq0 = _mm(xn1, w["mla_w_dq"], name="q_down")
    c_q = _rw(lambda c, g: (_rms(c, g),), [cq0], [w["mla_q_norm"]], [(MLA_Q_LORA, bf16)], name="q_norm")[0]
    qn = _mm(c_q, w["mla_w_uq_n"], name="q_up_nope", outs=(bf16,))
    qr_pre = _mm(c_q, w["mla_w_uq_r"], name="q_up_rope")
    qr = _rw(lambda xx, cs, sn: (_rope(xx, cs, sn),), [qr_pre, cos, sgn_sin], [], [(MLA_HEADS * MLA_ROPE, bf16)],
             name="q_rope")[0]
    o_att, lse = _attn_fwd(qn, qr, knv, kr2)
    h3 = _mm(o_att, w["mla_w_o"], name="mla_out", epi=_add_epi, extras=(h2,))
    h4, mlp1 = _mlp_fwd(h3, w["mlp_norm"][1:2], w["mlp_w_up"][1], w["mlp_w_down"][1], 1)

    def loss_fn(hh, tgt, gain):
        def f(a, b):
            e = _rms(a, b) - tgt
            return 0.5 * jnp.sum(jnp.sum(e * e, axis=-1, keepdims=True) / D_MODEL, axis=0, keepdims=True)
        val, vjp = jax.vjp(f, hh, gain)
        dh, dg = vjp(jnp.ones((1, 1), f32))
        return dh, jnp.broadcast_to(val, (1, LANES)), dg

    dh4, loss_acc, grads["final_norm"] = _rw(loss_fn, [h4, target], [w["final_norm"]], [(D_MODEL, f32)],
                                             [(1, LANES), (1, D_MODEL)], name="loss")
    loss = loss_acc[0, 0]

    dh3, g_n1, g_up1, g_dn1 = _mlp_bwd(dh4, h3, w["mlp_norm"][1:2], w["mlp_w_up"][1], w["mlp_w_down"][1], mlp1, 1)
    do_att = _mm(dh3, w["mla_w_o"], tb=True, name="mla_dout", outs=(bf16,))
    grads["mla_w_o"] = _mm(o_att, dh3, ta=True, name="mla_dwo")

    def delta_fn(a, b):
        prod = a.astype(f32) * b.astype(f32)
        outs = []
        for p in range(MLA_HEADS // 2):
            d0 = jnp.sum(prod[:, 2 * p * LANES:(2 * p + 1) * LANES], axis=-1, keepdims=True)
            d1 = jnp.sum(prod[:, (2 * p + 1) * LANES:(2 * p + 2) * LANES], axis=-1, keepdims=True)
            lo, _ = _pair_masks((a.shape[0], LANES))
            outs.append(jnp.where(lo, d0, d1))
        return (jnp.concatenate(outs, axis=1),)

    delta = _rw(delta_fn, [do_att, o_att], [], [(MLA_HEADS // 2 * LANES, f32)], name="attn_delta")[0]
    dqn, dqr, dkn, dv, dkr_parts = _attn_bwd(qn, qr, knv, kr2, do_att, lse, delta)

    def dq_fn(a, b, cs, sn):
        return (jnp.concatenate([a, _rope(b, cs, sn, -1.0)], axis=1),)

    dqf = _rw(dq_fn, [dqn, dqr, cos, sgn_sin], [], [(MLA_HEADS * (MLA_NOPE + MLA_ROPE), bf16)], name="dq_rope")[0]
    dc_q = _mm(dqf, w["mla_w_uq_nr"], tb=True, name="q_up_dx")
    grads["mla_w_uq_nr"] = _mm(c_q, dqf, ta=True, name="q_up_dw")

    def dqn_fn(c, dy, g):
        return _rms_bwd(c, g, dy)

    dcq0, grads["mla_q_norm"] = _rw(dqn_fn, [cq0, dc_q], [w["mla_q_norm"]], [(MLA_Q_LORA, bf16)], [(1, MLA_Q_LORA)],
                                    name="q_dnorm")
    dxn1 = _mm(dcq0, w["mla_w_dq"], tb=True, name="q_down_dx")
    grads["mla_w_dq"] = _mm(xn1, dcq0, ta=True, name="q_down_dw")

    dc_kv = _mm(dkn, w["kv_w_uk"], tb=True, name="kv_up_dx_k")
    dc_kv = _mm(dv, w["kv_w_uv"], tb=True, name="kv_up_dx_v", epi=_add_epi, extras=(dc_kv,))
    grads["kv_w_uk"] = _mm(c_kv, dkn, ta=True, name="kv_up_dw_k")
    grads["kv_w_uv"] = _mm(c_kv, dv, ta=True, name="kv_up_dw_v")

    def dckr_fn(c, dc, dparts, cs, sn, g):
        tot = dparts[:, :LANES]
        for p in range(1, MLA_HEADS // 2):
            tot = tot + dparts[:, p * LANES:(p + 1) * LANES]
        tot = tot + pltpu.roll(tot, MLA_ROPE, 1)
        lo, _ = _pair_masks(tot.shape)
        dkr = jnp.where(lo, _rope(tot, cs, sn, -1.0), 0.0)
        dcc, dg = _rms_bwd(c[:, :MLA_KV_LORA], g, dc)
        return jnp.concatenate([dcc, dkr], axis=1), dg

    dckr, grads["kv_norm"] = _rw(dckr_fn, [ckr, dc_kv, dkr_parts, cos, sgn_sin], [w["kv_norm"]],
                                 [(MLA_KV_LORA + LANES, bf16)], [(1, MLA_KV_LORA)], name="kv_dnorm_rope")
    dhk = _mm(dckr, w["kv_w_dkv"], tb=True, name="kv_down_dx")
    grads["kv_w_dkv"] = _mm(hk, dckr, ta=True, name="kv_down_dw")

    def dh2_fn(hh, d1, d2, dres, g1, g2):
        a, ga = _rms_bwd(hh, g1, d1)
        b, gb = _rms_bwd(hh, g2, d2)
        return a + b + dres, ga, gb

    dh2, grads["kv_in_norm"], grads["mla_norm"] = _rw(dh2_fn, [h2, dhk, dxn1, dh3], [w["kv_in_norm"], w["mla_norm"]],
                                                      [(D_MODEL, f32)], [(1, D_MODEL)] * 2, name="kv_mla_dnorm")

    dh1, g_n0, g_up0, g_dn0 = _mlp_bwd(dh2, h1, w["mlp_norm"][0:1], w["mlp_w_up"][0], w["mlp_w_down"][0], mlp0, 0)
    grads["mlp_norm"] = jnp.concatenate([g_n0, g_n1], axis=0)
    grads["mlp_w_up"] = (g_up0, g_up1)
    grads["mlp_w_down"] = (g_dn0, g_dn1)
    dy = _mm(dh1, w["hgrn_w_o"], tb=True, name="hgrn_dout")
    grads["hgrn_w_o"] = _mm(y, dh1, ta=True, name="hgrn_dwo")

    def dhn_fn(oo, p, dyy, gn):
        dos, dgs, dgn = [], [], jnp.zeros_like(gn)
        for h in range(HGRN_HEADS):
            cols = slice(LANES * h, LANES * (h + 1))
            _, vjp = jax.vjp(_head_norm_gate, oo[:, cols], p[:, 3 * D_MODEL + LANES * h:3 * D_MODEL + LANES * (h + 1)], gn)
            a, b, c = vjp(dyy[:, cols])
            dos.append(a)
            dgs.append(b)
            dgn = dgn + c
        return jnp.concatenate(dos, axis=1), jnp.concatenate(dgs, axis=1), dgn

    do, dgate, grads["hgrn_g_norm"] = _rw(dhn_fn, [o, p4, dy], [w["hgrn_g_norm"]], [(D_MODEL, f32)] * 2, [(1, HGRN_DK)],
                                          name="hgrn_dheadnorm")
    dq, dk, dv_h, dg = _gla_bwd(q, k, p4, g, states, do)

    def dgates_fn(p, dqq, dkk, dgg, dvv, dgt, lbl):
        _, vjp = jax.vjp(_gates, p[:, :D_MODEL], p[:, D_MODEL:2 * D_MODEL], lbl)
        dqp, dfp, dlbl = vjp((dqq, dkk, dgg))
        return jnp.concatenate([dqp, dfp, dvv, dgt], axis=1), dlbl

    dp4, grads["hgrn_lb_logits"] = _rw(dgates_fn, [p4, dq, dk, dg, dv_h, dgate], [w["hgrn_lb_logits"]],
                                       [(4 * D_MODEL, bf16)], [(2, D_MODEL)], name="hgrn_dgates")
    dxn0 = _mm(dp4, w["hgrn_w4"], tb=True, name="hgrn_proj_dx")
    grads["hgrn_w4"] = _mm(xn0, dp4, ta=True, name="hgrn_proj_dw")

    def dx_fn(xx, dyy, dres, gn):
        dxx, dgn = _rms_bwd(xx, gn, dyy)
        return dxx + dres, dgn

    grad_x, grads["hgrn_norm"] = _rw(dx_fn, [x, dxn0, dh1], [w["hgrn_norm"]], [(D_MODEL, f32)], [(1, D_MODEL)],
                                     name="hgrn_dnorm")
    return loss, grad_x, grads


HBM = pl.BlockSpec(memory_space=pltpu.HBM)


def _me():
    return lax.axis_index("x"), lax.axis_index("y"), lax.axis_index("c")


def _flip(x, y, f):
    return (1 - x if f & 1 else x), (1 - y if f & 2 else y)


def _rcopy(src, dst, sems, k, dev):
    return pltpu.make_async_remote_copy(src_ref=src, dst_ref=dst, send_sem=sems.at[0, k], recv_sem=sems.at[1, k],
                                        device_id=dev, device_id_type=MESH)


def _all_gather_weights(wp, sv):
    def body(wp_ref, sv_ref, out_ref, svs_ref, sems, local_sems):
        x, y, c = _me()
        half = pl.ds(pl.multiple_of(c * HALF, 16), HALF)
        other = pl.ds(pl.multiple_of((1 - c) * HALF, 16), HALF)
        mine = [pltpu.make_async_copy(wp_ref, out_ref.at[2 * x + y], local_sems.at[0]),
                pltpu.make_async_copy(sv_ref, svs_ref.at[2 * x + y], local_sems.at[1])]
        for cp in mine:
            cp.start()
        sends = []
        for f in (1, 2, 3):
            px, py = _flip(x, y, f)
            sends.append(_rcopy(wp_ref.at[half], out_ref.at[2 * x + y, half], sems, f - 1, (px, py, c)))
            sends.append(_rcopy(sv_ref, svs_ref.at[2 * x + y], sems, 5 + f, (px, py, c)))
        for cp in sends:
            cp.start()
        for f in (1, 2, 3):
            px, py = _flip(x, y, f)
            landed = out_ref.at[2 * px + py, half]
            _rcopy(landed, landed, sems, f - 1, (px, py, c)).wait_recv()
            sends.append(_rcopy(landed, landed, sems, 2 + f, (x, y, 1 - c)))
            sends[-1].start()
        for f in (1, 2, 3):
            px, py = _flip(x, y, f)
            theirs = out_ref.at[2 * px + py, other]
            _rcopy(theirs, theirs, sems, 2 + f, (x, y, 1 - c)).wait_recv()
            _rcopy(sv_ref, svs_ref.at[2 * px + py], sems, 5 + f, (px, py, c)).wait_recv()
        for cp in sends:
            cp.wait_send()
        for cp in mine:
            cp.wait()

    return _pc(body, name="weights_all_gather", in_specs=[HBM, HBM], out_specs=[HBM, HBM],
               out_shape=[_sds((4, ROWS, D_MODEL), bf16), _sds((4, 8, 256), f32)],
               scratch=[pltpu.SemaphoreType.DMA((2, 9)), pltpu.SemaphoreType.DMA((2,))])(wp, sv)


def _send_half_to_sibling(gp):
    def body(gp_ref, out_ref, sems):
        x, y, c = _me()
        other = pl.ds(pl.multiple_of((1 - c) * HALF, 8), HALF)
        cp = _rcopy(gp_ref.at[:, other], out_ref, sems, 0, (x, y, 1 - c))
        cp.start()
        cp.wait()

    return _pc(body, name="grads_to_sibling", in_specs=[HBM], out_specs=HBM, out_shape=_sds((4, HALF, D_MODEL), f32),
               scratch=[pltpu.SemaphoreType.DMA((2, 1))])(gp)


def _exchange_chips(sb, small):
    def body(sb_ref, small_ref, out_ref, smalls_ref, sems, local_sem):
        x, y, c = _me()
        me = 4 * x + 2 * y + c
        mine = pltpu.make_async_copy(small_ref, smalls_ref.at[me], local_sem)
        mine.start()
        sends = []
        for f in (1, 2, 3):
            px, py = _flip(x, y, f)
            sends.append(_rcopy(sb_ref.at[2 * px + py], out_ref.at[f - 1], sems, f - 1, (px, py, c)))
            sends[-1].start()
        for f in range(1, 8):
            px, py = _flip(x, y, f)
            pc = 1 - c if f & 4 else c
            sends.append(_rcopy(small_ref, smalls_ref.at[me], sems, 2 + f, (px, py, pc)))
            sends[-1].start()
        for f in (1, 2, 3):
            _rcopy(sb_ref.at[0], out_ref.at[f - 1], sems, f - 1, (x, y, c)).wait_recv()
        for f in range(1, 8):
            px, py = _flip(x, y, f)
            pc = 1 - c if f & 4 else c
            _rcopy(small_ref, smalls_ref.at[4 * px + 2 * py + pc], sems, 2 + f, (x, y, c)).wait_recv()
        for cp in sends:
            cp.wait_send()
        mine.wait()

    return _pc(body, name="grads_exchange_chips", in_specs=[HBM, HBM], out_specs=[HBM, HBM],
               out_shape=[_sds((3, HALF, D_MODEL), bf16), _sds((8, SMALL_ROWS, D_MODEL), f32)],
               scratch=[pltpu.SemaphoreType.DMA((2, 10)), pltpu.SemaphoreType.DMA])(sb, small)


def _exchange_halves(tot):
    def body(tot_ref, out_ref, sems, local_sem):
        x, y, c = _me()
        half = pl.ds(pl.multiple_of(c * HALF, 8), HALF)
        mine = pltpu.make_async_copy(tot_ref, out_ref.at[half], local_sem)
        mine.start()
        cp = _rcopy(tot_ref, out_ref.at[half], sems, 0, (x, y, 1 - c))
        cp.start()
        cp.wait()
        mine.wait()

    return _pc(body, name="grads_exchange_halves", in_specs=[HBM], out_specs=HBM, out_shape=_sds((ROWS, D_MODEL), f32),
               scratch=[pltpu.SemaphoreType.DMA((2, 1)), pltpu.SemaphoreType.DMA])(tot)


SUM_TR = 464


def _sum_over_cores(gp, recv, cq):
    nb = HALF // SUM_TR

    def body(cq_ref, g_ref, r_ref, o32_ref, o16_ref):
        s = g_ref[...] + r_ref[...]
        o32_ref[...] = s
        o16_ref[...] = s.astype(bf16)

    spec = pl.BlockSpec((1, SUM_TR, D_MODEL), lambda b, i, cq_ref: (b, i, 0))
    gs = pltpu.PrefetchScalarGridSpec(
        num_scalar_prefetch=1, grid=(4, nb),
        in_specs=[pl.BlockSpec((1, SUM_TR, D_MODEL), lambda b, i, cq_ref: (b, cq_ref[0] * nb + i, 0)), spec],
        out_specs=[spec, spec])
    return _pc(body, name="grads_sum_cores", grid_spec=gs, sem=("arbitrary", "arbitrary"),
               out_shape=[_sds((4, HALF, D_MODEL), f32), _sds((4, HALF, D_MODEL), bf16)])(cq, gp, recv)


def _sum_over_chips(s32, recv, cq):
    nb = HALF // SUM_TR

    def body(cq_ref, own_ref, r_ref, o_ref):
        o_ref[...] = ((own_ref[0] + r_ref[0].astype(f32)) + r_ref[1].astype(f32)) + r_ref[2].astype(f32)

    gs = pltpu.PrefetchScalarGridSpec(
        num_scalar_prefetch=1, grid=(nb,),
        in_specs=[pl.BlockSpec((1, SUM_TR, D_MODEL), lambda i, cq_ref: (cq_ref[1], i, 0)),
                  pl.BlockSpec((3, SUM_TR, D_MODEL), lambda i, cq_ref: (0, i, 0))],
        out_specs=pl.BlockSpec((SUM_TR, D_MODEL), lambda i, cq_ref: (i, 0)))
    return _pc(body, name="grads_sum_chips", grid_spec=gs, sem=("arbitrary",),
               out_shape=_sds((HALF, D_MODEL), f32))(cq, s32, recv)


def _sum_small(smalls):
    def body(s_ref, o_ref):
        tot = s_ref[0]
        for d in range(1, 8):
            tot = tot + s_ref[d]
        o_ref[...] = tot

    return _pc(body, name="small_sum", out_shape=_sds((SMALL_ROWS, D_MODEL), f32))(smalls)


def _adamw_math(w, g, m, v):
    m = ADAM_B1 * m + (1.0 - ADAM_B1) * g
    v = ADAM_B2 * v + (1.0 - ADAM_B2) * jnp.square(g)
    m_hat = m / (1.0 - ADAM_B1 ** ADAM_STEP)
    v_hat = v / (1.0 - ADAM_B2 ** ADAM_STEP)
    delta = -ADAM_LR * (m_hat / (jnp.sqrt(v_hat) + ADAM_EPS) + ADAM_WD * w)
    return delta, m, v


def _adamw(w, g, m, v, name):
    cols = w.shape[1]
    return _rw(_adamw_math, [w, g, m, v], [], [(cols, f32)] * 3, name=name, tr=256)


def _adamw_small(items):
    n = len(items)

    def body(*refs):
        ins, outs = refs[:4 * n], refs[4 * n:]
        for i in range(n):
            res = _adamw_math(*[r[...] for r in ins[4 * i:4 * i + 4]])
            for o, val in zip(outs[3 * i:3 * i + 3], res):
                o[...] = val

    flat = [a for it in items for a in it]
    out_shape = [_sds(it[0].shape, f32) for it in items for _ in range(3)]
    res = _pc(body, name="adamw_small", out_shape=out_shape)(*flat)
    return [tuple(res[3 * i:3 * i + 3]) for i in range(n)]


def _pack_shards(sh, dtype):
    parts = [sh[n].reshape(-1, D_MODEL).astype(dtype) for n, _ in PACK]
    parts.append(jnp.zeros((PACK_PAD, D_MODEL), dtype))
    return jnp.concatenate(parts, axis=0)


def _unpack_full(g4):
    def rows(name):
        o, r = PACK_OFF[name]
        return g4[:, o:o + r]

    w = {}
    hg = g4[:, 0:1024].reshape(4, 4, 256, D_MODEL)
    w["hgrn_w4"] = hg.transpose(0, 2, 1, 3).reshape(D_MODEL, 4 * D_MODEL)
    w["hgrn_w_o"] = rows("hgrn_w_o").reshape(D_MODEL, D_MODEL)
    w["mla_w_dq"] = rows("mla_w_dq").reshape(D_MODEL, MLA_Q_LORA)
    uq = rows("mla_w_uq").reshape(4, MLA_Q_LORA, 768).transpose(1, 0, 2).reshape(MLA_Q_LORA, MLA_HEADS, MLA_NOPE + MLA_ROPE)
    w["mla_w_uq_n"] = uq[:, :, :MLA_NOPE].reshape(MLA_Q_LORA, MLA_HEADS * MLA_NOPE)
    w["mla_w_uq_r"] = uq[:, :, MLA_NOPE:].reshape(MLA_Q_LORA, MLA_HEADS * MLA_ROPE)
    w["mla_w_uq_nr"] = jnp.concatenate([w["mla_w_uq_n"], w["mla_w_uq_r"]], axis=1)
    w["mla_w_o"] = rows("mla_w_o").reshape(MLA_HEADS * MLA_V, D_MODEL)
    dkv = rows("kv_w_dkv").reshape(D_MODEL, MLA_KV_LORA + MLA_ROPE)
    w["kv_w_dkv"] = jnp.pad(dkv, ((0, 0), (0, LANES - MLA_ROPE)))
    w["kv_w_uk"] = rows("kv_w_uk").reshape(4, MLA_KV_LORA, 512).transpose(1, 0, 2).reshape(MLA_KV_LORA, MLA_HEADS * MLA_NOPE)
    w["kv_w_uv"] = rows("kv_w_uv").reshape(4, MLA_KV_LORA, 512).transpose(1, 0, 2).reshape(MLA_KV_LORA, MLA_HEADS * MLA_V)
    w["kv_w_ukv"] = jnp.concatenate([w["kv_w_uk"], w["kv_w_uv"]], axis=1)
    w["mlp_w_up"] = rows("mlp_w_up").reshape(4, 2, D_MODEL, 1024).transpose(1, 2, 0, 3).reshape(2, D_MODEL, D_FF)
    w["mlp_w_down"] = rows("mlp_w_down").reshape(4, 2, 1024, D_MODEL).transpose(1, 0, 2, 3).reshape(2, D_FF, D_MODEL)
    return w


def _pack_grads(g):
    parts = [g["hgrn_w4"].reshape(4, 256, 4, D_MODEL).transpose(0, 2, 1, 3).reshape(4, 1024, D_MODEL),
             g["hgrn_w_o"].reshape(4, 256, D_MODEL),
             g["mla_w_dq"].reshape(4, 64, D_MODEL)]
    nr = g["mla_w_uq_nr"]
    uq = jnp.concatenate([nr[:, :MLA_HEADS * MLA_NOPE].reshape(MLA_Q_LORA, MLA_HEADS, MLA_NOPE),
                          nr[:, MLA_HEADS * MLA_NOPE:].reshape(MLA_Q_LORA, MLA_HEADS, MLA_ROPE)], axis=2)
    parts.append(uq.reshape(MLA_Q_LORA, 4, 768).transpose(1, 0, 2).reshape(4, 192, D_MODEL))
    parts.append(g["mla_w_o"].reshape(4, 512, D_MODEL))
    parts.append(g["kv_w_dkv"][:, :MLA_KV_LORA + MLA_ROPE].reshape(4, 80, D_MODEL))
    parts.append(g["kv_w_uk"].reshape(MLA_KV_LORA, 4, 512).transpose(1, 0, 2).reshape(4, 128, D_MODEL))
    parts.append(g["kv_w_uv"].reshape(MLA_KV_LORA, 4, 512).transpose(1, 0, 2).reshape(4, 128, D_MODEL))
    up = jnp.stack(g["mlp_w_up"])
    parts.append(up.reshape(2, D_MODEL, 4, 1024).transpose(2, 0, 1, 3).reshape(4, 2048, D_MODEL))
    dn = jnp.stack(g["mlp_w_down"])
    parts.append(dn.reshape(2, 4, 1024, D_MODEL).transpose(1, 0, 2, 3).reshape(4, 2048, D_MODEL))
    parts.append(jnp.zeros((4, PACK_PAD, D_MODEL), f32))
    return jnp.concatenate(parts, axis=1)


def _pack_small(g):
    rows = []
    for name, _, r, wd in SMALL:
        a = g[name].reshape(r, wd)
        rows.append(jnp.pad(a, ((0, 0), (0, D_MODEL - wd))) if wd < D_MODEL else a)
    rows.append(jnp.zeros((SMALL_ROWS - sum(r for _, _, r, _ in SMALL), D_MODEL), f32))
    return jnp.concatenate(rows, axis=0)


def kernel(x, hgrn_norm, hgrn_w_q, hgrn_w_f, hgrn_w_i, hgrn_w_g, hgrn_g_norm, hgrn_w_o, hgrn_lb_logits, mla_norm, mla_w_dq, mla_q_norm, mla_w_uq, mla_w_o, kv_in_norm, kv_w_dkv, kv_norm, kv_w_uk, kv_w_uv, mlp_norm, mlp_w_up, mlp_w_down, final_norm, loss_target, m_hgrn_norm, m_hgrn_w_q, m_hgrn_w_f, m_hgrn_w_i, m_hgrn_w_g, m_hgrn_g_norm, m_hgrn_w_o, m_hgrn_lb_logits, m_mla_norm, m_mla_w_dq, m_mla_q_norm, m_mla_w_uq, m_mla_w_o, m_kv_in_norm, m_kv_w_dkv, m_kv_norm, m_kv_w_uk, m_kv_w_uv, m_mlp_norm, m_mlp_w_up, m_mlp_w_down, m_final_norm, v_hgrn_norm, v_hgrn_w_q, v_hgrn_w_f, v_hgrn_w_i, v_hgrn_w_g, v_hgrn_g_norm, v_hgrn_w_o, v_hgrn_lb_logits, v_mla_norm, v_mla_w_dq, v_mla_q_norm, v_mla_w_uq, v_mla_w_o, v_kv_in_norm, v_kv_w_dkv, v_kv_norm, v_kv_w_uk, v_kv_w_uv, v_mlp_norm, v_mlp_w_up, v_mlp_w_down, v_final_norm):
    given = dict(locals())
    wsh = {n: given[n] for n in WEIGHTS}
    msh = {n: given["m_" + n] for n in WEIGHTS}
    vsh = {n: given["v_" + n] for n in WEIGHTS}
    xi, yi, ci = _me()
    chip = 2 * xi + yi
    cq = jnp.stack([ci, chip]).astype(jnp.int32)

    small_w = {n: wsh[n].reshape(r, -1) for n, _, r, _ in SMALL}
    sv = jnp.concatenate([small_w["hgrn_norm"], small_w["hgrn_lb_logits"], jnp.zeros((5, 256), f32)], axis=0)
    g4, sv4 = _all_gather_weights(_pack_shards(wsh, bf16), sv)
    w = _unpack_full(g4)
    w["hgrn_norm"] = sv4[:, 0, :].reshape(1, D_MODEL)
    w["hgrn_lb_logits"] = sv4[:, 1:3, :].transpose(1, 0, 2).reshape(2, D_MODEL)
    for n in ("hgrn_g_norm", "mla_norm", "mla_q_norm", "kv_in_norm", "kv_norm", "mlp_norm", "final_norm"):
        w[n] = small_w[n]

    loss, grad_x, g = _local_step(x.reshape(-1, D_MODEL), loss_target.reshape(-1, D_MODEL), w)
    loss = lax.psum(loss, ("x", "y", "c"))

    gp = _pack_grads(g)
    from_sibling = _send_half_to_sibling(gp)
    s32, s16 = _sum_over_cores(gp, from_sibling, cq)
    from_chips, smalls = _exchange_chips(s16, _pack_small(g))
    total = _exchange_halves(_sum_over_chips(s32, from_chips, cq))
    small_tot = _sum_small(smalls)

    grad, delta, new_m, new_v = {}, {}, {}, {}
    for n, _ in PACK:
        o, r = PACK_OFF[n]
        shp = wsh[n].shape
        two_d = (-1, shp[-1])
        grad[n] = total[o:o + r].reshape(shp)
        d, m2, v2 = _adamw(wsh[n].reshape(two_d), grad[n].reshape(two_d), msh[n].reshape(two_d), vsh[n].reshape(two_d),
                           "adamw_" + n)
        delta[n], new_m[n], new_v[n] = d.reshape(shp), m2.reshape(shp), v2.reshape(shp)
    items = []
    for n, row, r, wd in SMALL:
        gs = small_tot[row:row + r, :wd]
        if n in ("hgrn_norm", "hgrn_lb_logits"):
            gs = lax.dynamic_slice(gs, (0, 256 * chip), (r, 256))
        grad[n] = gs.reshape(wsh[n].shape)
        items.append((small_w[n], gs, msh[n].reshape(gs.shape), vsh[n].reshape(gs.shape)))
    for (n, _, _, _), (d, m2, v2) in zip(SMALL, _adamw_small(items)):
        shp = wsh[n].shape
        delta[n], new_m[n], new_v[n] = d.reshape(shp), m2.reshape(shp), v2.reshape(shp)

    return (loss, grad_x.reshape(x.shape), *[grad[n] for n in WEIGHTS], *[delta[n] for n in WEIGHTS],
            *[new_m[n] for n in WEIGHTS], *[new_v[n] for n in WEIGHTS])
```

```python
import functools

import jax
import jax.numpy as jnp
from jax import lax
from jax.experimental import pallas as pl
from jax.experimental.pallas import tpu as pltpu

f32, bf16 = jnp.float32, jnp.bfloat16
HI = lax.Precision.HIGHEST
MESH = pl.DeviceIdType.MESH

D_MODEL = 1024
D_FF = 4096
EPS = 1e-6
HGRN_HEADS, HGRN_DK, HGRN_CHUNK, HGRN_SUB = 8, 128, 64, 16
MLA_HEADS, MLA_NOPE, MLA_ROPE, MLA_V = 16, 128, 64, 128
MLA_Q_LORA, MLA_KV_LORA = 256, 256
ROPE_THETA = 10000.0
ATT_SCALE = (MLA_NOPE + MLA_ROPE) ** -0.5
EXP_CLAMP = 80.0

ADAM_LR, ADAM_B1, ADAM_B2, ADAM_EPS, ADAM_WD, ADAM_STEP = 0.001, 0.9, 0.999, 1e-08, 0.01, 10

V7X_VMEM_BYTES = 64 * 1024 * 1024
VMEM_LIMIT = V7X_VMEM_BYTES - 8 * 1024 * 1024
LANES = 128

PACK = (("hgrn_w_q", 256), ("hgrn_w_f", 256), ("hgrn_w_i", 256), ("hgrn_w_g", 256), ("hgrn_w_o", 256),
        ("mla_w_dq", 64), ("mla_w_uq", 192), ("mla_w_o", 512), ("kv_w_dkv", 80), ("kv_w_uk", 128),
        ("kv_w_uv", 128), ("mlp_w_up", 2048), ("mlp_w_down", 2048))
PACK_ROWS = sum(r for _, r in PACK)
PACK_PAD = 16
ROWS = PACK_ROWS + PACK_PAD
HALF = ROWS // 2
PACK_OFF = {}
_o = 0
for _n, _r in PACK:
    PACK_OFF[_n] = (_o, _r)
    _o += _r

WEIGHTS = ("hgrn_norm", "hgrn_w_q", "hgrn_w_f", "hgrn_w_i", "hgrn_w_g", "hgrn_g_norm", "hgrn_w_o", "hgrn_lb_logits",
           "mla_norm", "mla_w_dq", "mla_q_norm", "mla_w_uq", "mla_w_o", "kv_in_norm", "kv_w_dkv", "kv_norm", "kv_w_uk",
           "kv_w_uv", "mlp_norm", "mlp_w_up", "mlp_w_down", "final_norm")
SMALL = (("hgrn_norm", 0, 1, 1024), ("hgrn_lb_logits", 1, 2, 1024), ("hgrn_g_norm", 3, 1, 128),
         ("mla_norm", 4, 1, 1024), ("mla_q_norm", 5, 1, 256), ("kv_in_norm", 6, 1, 1024), ("kv_norm", 7, 1, 256),
         ("mlp_norm", 8, 2, 1024), ("final_norm", 10, 1, 1024))
SMALL_ROWS = 16


def _pc(body, *, name, out_shape, grid=None, in_specs=None, out_specs=None, scratch=(), sem=None, grid_spec=None,
        aliases=None):
    params = pltpu.CompilerParams(dimension_semantics=sem, vmem_limit_bytes=VMEM_LIMIT)
    if grid_spec is not None:
        return pl.pallas_call(body, name=name, out_shape=out_shape, grid_spec=grid_spec, compiler_params=params,
                              interpret=False)
    kw = {k: v for k, v in (("grid", grid), ("in_specs", in_specs), ("out_specs", out_specs),
                            ("input_output_aliases", aliases)) if v is not None}
    return pl.pallas_call(body, name=name, out_shape=out_shape, scratch_shapes=list(scratch), compiler_params=params,
                          interpret=False, **kw)


def _sds(shape, dtype):
    return jax.ShapeDtypeStruct(tuple(shape), dtype)


def _mm(a, b, *, name, ta=False, tb=False, outs=(f32,), epi=None, extras=(), tm=1024, tn=1024, tk=512):
    m, k = (a.shape[1], a.shape[0]) if ta else a.shape
    n = b.shape[0] if tb else b.shape[1]
    tm, tn, tk = min(tm, m), min(tn, n), min(tk, k)
    assert m % tm == 0 and n % tn == 0 and k % tk == 0, (name, m, n, k)
    nk = k // tk
    a_spec = pl.BlockSpec((tk, tm), lambda i, j, kk: (kk, i)) if ta else pl.BlockSpec((tm, tk), lambda i, j, kk: (i, kk))
    b_spec = pl.BlockSpec((tn, tk), lambda i, j, kk: (j, kk)) if tb else pl.BlockSpec((tk, tn), lambda i, j, kk: (kk, j))
    e_specs = [pl.BlockSpec((tm, tn), lambda i, j, kk: (i, j)) for _ in extras]
    dn = (((0 if ta else 1,), (1 if tb else 0,)), ((), ()))
    n_e, n_o = len(extras), len(outs)

    def body(*refs):
        a_ref, b_ref = refs[0], refs[1]
        e_refs = refs[2:2 + n_e]
        o_refs = refs[2 + n_e:2 + n_e + n_o]
        acc = refs[-1]
        kk = pl.program_id(2)

        @pl.when(kk == 0)
        def _():
            acc[...] = jnp.zeros_like(acc)

        acc[...] += lax.dot_general(a_ref[...].astype(bf16), b_ref[...].astype(bf16), dn, preferred_element_type=f32)

        @pl.when(kk == nk - 1)
        def _():
            r = acc[...]
            res = epi(r, *[e[...] for e in e_refs]) if epi is not None else (r,)
            for o, v in zip(o_refs, res):
                o[...] = v.astype(o.dtype)

    out = _pc(body, name=name, grid=(m // tm, n // tn, nk),
              in_specs=[a_spec, b_spec] + e_specs,
              out_specs=[pl.BlockSpec((tm, tn), lambda i, j, kk: (i, j)) for _ in outs],
              out_shape=[_sds((m, n), dt) for dt in outs],
              scratch=[pltpu.VMEM((tm, tn), f32)],
              sem=("parallel", "parallel", "arbitrary"))(a, b, *extras)
    return out[0] if n_o == 1 else out


def _rw(fn, rows, bcast, outs, accs=(), *, name, tr=256):
    t = rows[0].shape[0]
    tr = min(tr, t)
    assert t % tr == 0
    n_r, n_b, n_o, n_a = len(rows), len(bcast), len(outs), len(accs)

    def body(*refs):
        r_refs = refs[:n_r]
        b_refs = refs[n_r:n_r + n_b]
        o_refs = refs[n_r + n_b:n_r + n_b + n_o]
        a_refs = refs[n_r + n_b + n_o:]
        res = fn(*[r[...] for r in r_refs], *[b[...] for b in b_refs])
        for o, v in zip(o_refs, res[:n_o]):
            o[...] = v.astype(o.dtype)
        i = pl.program_id(0)
        for a_ref, v in zip(a_refs, res[n_o:]):
            @pl.when(i == 0)
            def _(a_ref=a_ref):
                a_ref[...] = jnp.zeros_like(a_ref)
            a_ref[...] += v

    in_specs = [pl.BlockSpec((tr, r.shape[1]), lambda i: (i, 0)) for r in rows]
    in_specs += [pl.BlockSpec(b.shape, lambda i: (0, 0)) for b in bcast]
    out_specs = [pl.BlockSpec((tr, w), lambda i: (i, 0)) for w, _ in outs]
    out_specs += [pl.BlockSpec(s, lambda i: (0, 0)) for s in accs]
    out_shape = [_sds((t, w), dt) for w, dt in outs] + [_sds(s, f32) for s in accs]
    res = _pc(body, name=name, grid=(t // tr,), in_specs=in_specs, out_specs=out_specs, out_shape=out_shape,
              sem=("arbitrary",))(*rows, *bcast)
    return res


def _rms(x, gain):
    return x * lax.rsqrt(jnp.mean(x * x, axis=-1, keepdims=True) + EPS) * gain


def _rms_bwd(x, gain, dy):
    _, vjp = jax.vjp(_rms, x, gain)
    return vjp(dy)


def _lower_bound(lbl):
    l0, l1 = lbl[0:1, :], lbl[1:2, :]
    mx = jnp.maximum(l0, l1)
    e0, e1 = jnp.exp(l0 - mx), jnp.exp(l1 - mx)
    return e0 / (e0 + e1)


def _gates(qpre, fpre, lbl):
    lb = _lower_bound(lbl)
    q = jax.nn.silu(qpre)
    forget = lb + (1.0 - lb) * jax.nn.sigmoid(fpre)
    return q, 1.0 - forget, jnp.log(forget)


def _head_norm_gate(o, gpre, gn):
    return _rms(o, gn) * jax.nn.silu(gpre)


def _swap_halves(x):
    w = x.shape[1]
    lane = lax.broadcasted_iota(jnp.int32, x.shape, 1)
    return jnp.where((lane % MLA_ROPE) < MLA_ROPE // 2, pltpu.roll(x, w - MLA_ROPE // 2, 1),
                     pltpu.roll(x, MLA_ROPE // 2, 1))


def _tile_lanes(tab, w):
    return tab if w == LANES else jnp.concatenate([tab] * (w // LANES), axis=1)


def _rope(x, cos, sgn_sin, sign=1.0):
    w = x.shape[1]
    return x * _tile_lanes(cos, w) + sign * _swap_halves(x) * _tile_lanes(sgn_sin, w)


def _bd(a, b, ca, cb):
    return lax.dot_general(a.astype(bf16), b.astype(bf16), (((ca,), (cb,)), ((), ())), preferred_element_type=f32)


@jax.custom_vjp
def _dot_nn(a, b):
    return _bd(a, b, 1, 0)


@jax.custom_vjp
def _dot_nt(a, b):
    return _bd(a, b, 1, 1)


@jax.custom_vjp
def _dot_tn(a, b):
    return _bd(a, b, 0, 0)


_dot_nn.defvjp(lambda a, b: (_bd(a, b, 1, 0), (a, b)), lambda r, g: (_bd(g, r[1], 1, 1), _bd(r[0], g, 0, 0)))
_dot_nt.defvjp(lambda a, b: (_bd(a, b, 1, 1), (a, b)), lambda r, g: (_bd(g, r[1], 1, 0), _bd(g, r[0], 0, 0)))
_dot_tn.defvjp(lambda a, b: (_bd(a, b, 0, 0), (a, b)), lambda r, g: (_bd(r[1], g, 1, 1), _bd(r[0], g, 1, 0)))


def _hdot(c, g):
    return jnp.dot(c, g, precision=HI, preferred_element_type=f32)


def _gla_consts():
    c, s = HGRN_CHUNK, HGRN_SUB
    row = lax.broadcasted_iota(jnp.int32, (c, c), 0)
    col = lax.broadcasted_iota(jnp.int32, (c, c), 1)
    incl = (col <= row).astype(f32)
    masks = []
    for i in range(c // s):
        n = s * (i + 1)
        mr = lax.broadcasted_iota(jnp.int32, (s, n), 0) + s * i
        mc = lax.broadcasted_iota(jnp.int32, (s, n), 1)
        masks.append(mc <= mr)
    return incl, masks


def _gla_chunk(consts, dots, q, k, v, g, st):
    incl, masks = consts
    dot_nn, dot_nt, dot_tn = dots
    c, s = HGRN_CHUNK, HGRN_SUB
    b = _hdot(incl, g)
    b_last = b[c - 1:c, :]
    o_inter = dot_nt(q * jnp.exp(b), st)
    st_new = st * jnp.exp(b_last) + dot_tn(v, k * jnp.exp(b_last - b))
    intra = []
    for i in range(c // s):
        n = s * (i + 1)
        ref = b[s * i - 1:s * i, :] if i else jnp.zeros_like(b_last)
        qt = q[s * i:n] * jnp.exp(b[s * i:n] - ref)
        kt = k[:n] * jnp.exp(jnp.minimum(ref - b[:n], EXP_CLAMP))
        sc = jnp.where(masks[i], dot_nt(qt, kt), 0.0)
        intra.append(dot_nn(sc, v[:n]))
    return o_inter + jnp.concatenate(intra, axis=0), st_new


_PLAIN_DOTS = (lambda a, b: _bd(a, b, 1, 0), lambda a, b: _bd(a, b, 1, 1), lambda a, b: _bd(a, b, 0, 0))
_VJP_DOTS = (_dot_nn, _dot_nt, _dot_tn)


def _head_cols(h):
    return slice(HGRN_DK * h, HGRN_DK * (h + 1))


def _gla_fwd(q, k, p4, g):
    t = q.shape[0]
    nc = t // HGRN_CHUNK

    def body(q_ref, k_ref, v_ref, g_ref, o_ref, s_ref, st):
        @pl.when(pl.program_id(0) == 0)
        def _():
            st[...] = jnp.zeros_like(st)

        consts = _gla_consts()
        for h in range(HGRN_HEADS):
            cols = _head_cols(h)
            s_in = st[h]
            s_ref[0, h] = s_in
            o, st_new = _gla_chunk(consts, _PLAIN_DOTS, q_ref[:, cols], k_ref[:, cols], v_ref[:, cols], g_ref[:, cols], s_in)
            o_ref[:, cols] = o
            st[h] = st_new

    blk = lambda off: pl.BlockSpec((HGRN_CHUNK, D_MODEL), lambda c: (c, off))
    state_shape = (HGRN_HEADS, HGRN_DK, HGRN_DK)
    return _pc(body, name="gla_fwd", grid=(nc,),
               in_specs=[blk(0), blk(0), blk(2), blk(0)],
               out_specs=[blk(0), pl.BlockSpec((1,) + state_shape, lambda c: (c, 0, 0, 0))],
               out_shape=[_sds((t, D_MODEL), f32), _sds((nc,) + state_shape, f32)],
               scratch=[pltpu.VMEM(state_shape, f32)], sem=("arbitrary",))(q, k, p4, g)


def _gla_bwd(q, k, p4, g, states, do):
    t = q.shape[0]
    nc = t // HGRN_CHUNK

    def body(q_ref, k_ref, v_ref, g_ref, s_ref, do_ref, dq_ref, dk_ref, dv_ref, dg_ref, dst):
        @pl.when(pl.program_id(0) == 0)
        def _():
            dst[...] = jnp.zeros_like(dst)

        consts = _gla_consts()
        fn = lambda qq, kk, vv, gg, ss: _gla_chunk(consts, _VJP_DOTS, qq, kk, vv, gg, ss)
        for h in range(HGRN_HEADS):
            cols = _head_cols(h)
            _, vjp = jax.vjp(fn, q_ref[:, cols], k_ref[:, cols], v_ref[:, cols], g_ref[:, cols], s_ref[0, h])
            dq, dk, dv, dg, ds = vjp((do_ref[:, cols], dst[h]))
            dq_ref[:, cols] = dq
            dk_ref[:, cols] = dk
            dv_ref[:, cols] = dv
            dg_ref[:, cols] = dg
            dst[h] = ds

    blk = lambda off: pl.BlockSpec((HGRN_CHUNK, D_MODEL), lambda c: (nc - 1 - c, off))
    state_shape = (HGRN_HEADS, HGRN_DK, HGRN_DK)
    return _pc(body, name="gla_bwd", grid=(nc,),
               in_specs=[blk(0), blk(0), blk(2), blk(0),
                         pl.BlockSpec((1,) + state_shape, lambda c: (nc - 1 - c, 0, 0, 0)), blk(0)],
               out_specs=[blk(0)] * 4, out_shape=[_sds((t, D_MODEL), f32)] * 4,
               scratch=[pltpu.VMEM(state_shape, f32)], sem=("arbitrary",))(q, k, p4, g, states, do)


ATT_TQ, ATT_TK = 256, 512
NEG = -1e30


def _pair_masks(shape):
    lane = lax.broadcasted_iota(jnp.int32, shape, 1)
    return lane < MLA_ROPE, lane >= MLA_ROPE


def _scores(qn, qr, kn, kr):
    return _bd(qn, kn, 1, 1) + _bd(qr, kr, 1, 1)


def _causal(shape, row0, col0):
    row = row0 + lax.broadcasted_iota(jnp.int32, shape, 0)
    col = col0 + lax.broadcasted_iota(jnp.int32, shape, 1)
    return col <= row


def _attn_fwd(qn, qr, knv, kr2):
    t = qn.shape[0]
    tq, tk = min(ATT_TQ, t), min(ATT_TK, t)
    npair = MLA_HEADS // 2

    def body(qn_ref, qr_ref, kn_ref, v_ref, kr_ref, o_ref, lse_ref):
        i = pl.program_id(1)
        qr = qr_ref[...]
        masks = _pair_masks(qr.shape)
        n_full = (i * tq + 1) // tk
        nkv = (i * tq + tq + tk - 1) // tk
        outs, lses = [], []
        for e in range(2):
            cols = slice(LANES * e, LANES * (e + 1))
            qn_e = qn_ref[:, cols]
            qr_e = jnp.where(masks[e], qr, jnp.zeros_like(qr))

            def step(j, carry, masked, cols=cols, qn_e=qn_e, qr_e=qr_e):
                m, l, acc = carry
                ks = pl.ds(pl.multiple_of(j * tk, tk), tk)
                s = _scores(qn_e, qr_e, kn_ref[ks, cols], kr_ref[ks, :])
                if masked:
                    s = jnp.where(_causal(s.shape, i * tq, j * tk), s, NEG)
                m_new = jnp.maximum(m, jnp.max(s, axis=-1, keepdims=True))
                p = jnp.exp(s - m_new)
                alpha = jnp.exp(m - m_new)
                l = alpha * l + jnp.sum(p, axis=-1, keepdims=True)
                acc = alpha * acc + _bd(p, v_ref[ks, cols], 1, 0)
                return m_new, l, acc

            init = (jnp.full((tq, 1), NEG, f32), jnp.zeros((tq, 1), f32), jnp.zeros((tq, MLA_V), f32))
            carry = lax.fori_loop(0, n_full, functools.partial(step, masked=False), init)
            m, l, acc = lax.fori_loop(n_full, nkv, functools.partial(step, masked=True), carry)
            outs.append(acc / l)
            lses.append(m + jnp.log(l))
        o_ref[...] = jnp.concatenate(outs, axis=1).astype(o_ref.dtype)
        lse_ref[...] = jnp.where(masks[0], lses[0], lses[1])

    return _pc(body, name="attn_fwd", grid=(npair, t // tq),
               in_specs=[pl.BlockSpec((tq, 2 * LANES), lambda p, i: (i, p)),
                         pl.BlockSpec((tq, LANES), lambda p, i: (i, p)),
                         pl.BlockSpec((t, 2 * LANES), lambda p, i: (0, p)),
                         pl.BlockSpec((t, 2 * LANES), lambda p, i: (0, npair + p)),
                         pl.BlockSpec((t, LANES), lambda p, i: (0, 0))],
               out_specs=[pl.BlockSpec((tq, 2 * LANES), lambda p, i: (i, p)),
                          pl.BlockSpec((tq, LANES), lambda p, i: (i, p))],
               out_shape=[_sds((t, MLA_HEADS * MLA_V), bf16), _sds((t, npair * LANES), f32)],
               sem=("arbitrary", "arbitrary"))(qn, qr, knv, knv, kr2)


def _attn_bwd(qn, qr, knv, kr2, do, lse, delta):
    t = qn.shape[0]
    tq, tk = min(ATT_TQ, t), min(ATT_TK, t)
    npair = MLA_HEADS // 2
    nq = t // tq

    def body(qn_ref, qr_ref, do_ref, lse_ref, dl_ref, kn_ref, v_ref, kr_ref, dqn_ref, dqr_ref, dkn_ref, dv_ref, dkr_ref):
        j = pl.program_id(1)

        @pl.when(j == 0)
        def _():
            dqn_ref[...] = jnp.zeros_like(dqn_ref)
            dqr_ref[...] = jnp.zeros_like(dqr_ref)

        kr = kr_ref[...]
        masks = _pair_masks((tq, LANES))
        dkr_tot = jnp.zeros((tk, LANES), f32)
        for e in range(2):
            cols = slice(LANES * e, LANES * (e + 1))
            kn_e, v_e = kn_ref[:, cols], v_ref[:, cols]

            def step(i, carry, masked, e=e, cols=cols, kn_e=kn_e, v_e=v_e):
                dk, dkr, dv = carry
                qs = pl.ds(pl.multiple_of(i * tq, tq), tq)
                qn_e = qn_ref[qs, cols]
                qr_e = jnp.where(masks[e], qr_ref[qs, :], jnp.zeros((tq, LANES), bf16))
                do_e = do_ref[qs, cols]
                lse_e = lse_ref[qs, :][:, MLA_ROPE * e:MLA_ROPE * e + 1]
                dl_e = dl_ref[qs, :][:, MLA_ROPE * e:MLA_ROPE * e + 1]
                p = jnp.exp(_scores(qn_e, qr_e, kn_e, kr) - lse_e)
                if masked:
                    p = jnp.where(_causal(p.shape, i * tq, j * tk), p, 0.0)
                dv = dv + _bd(p, do_e, 0, 0)
                dp = _bd(do_e, v_e, 1, 1)
                ds = (p * (dp - dl_e)).astype(bf16)
                dk = dk + _bd(ds, qn_e, 0, 0)
                dkr = dkr + _bd(ds, qr_e, 0, 0)
                dqn_ref[qs, cols] += _bd(ds, kn_e, 1, 0)
                dqr_ref[qs, :] += jnp.where(masks[e], _bd(ds, kr, 1, 0), 0.0)
                return dk, dkr, dv

            zero = jnp.zeros((tk, LANES), f32)
            i_full = jnp.minimum((j * tk + tk + tq - 2) // tq, nq)
            carry = lax.fori_loop((j * tk) // tq, i_full, functools.partial(step, masked=True), (zero, zero, zero))
            dk, dkr, dv = lax.fori_loop(i_full, nq, functools.partial(step, masked=False), carry)
            dkn_ref[:, cols] = dk.astype(dkn_ref.dtype)
            dv_ref[:, cols] = dv.astype(dv_ref.dtype)
            dkr_tot = dkr_tot + dkr
        dkr_ref[...] = dkr_tot

    res = lambda w: pl.BlockSpec((t, w), lambda p, j: (0, p))
    return _pc(body, name="attn_bwd", grid=(npair, t // tk),
               in_specs=[res(2 * LANES), res(LANES), res(2 * LANES), res(LANES), res(LANES),
                         pl.BlockSpec((tk, 2 * LANES), lambda p, j: (j, p)),
                         pl.BlockSpec((tk, 2 * LANES), lambda p, j: (j, npair + p)),
                         pl.BlockSpec((tk, LANES), lambda p, j: (j, 0))],
               out_specs=[res(2 * LANES), res(LANES),
                          pl.BlockSpec((tk, 2 * LANES), lambda p, j: (j, p)),
                          pl.BlockSpec((tk, 2 * LANES), lambda p, j: (j, p)),
                          pl.BlockSpec((tk, LANES), lambda p, j: (j, p))],
               out_shape=[_sds((t, MLA_HEADS * MLA_NOPE), f32), _sds((t, npair * LANES), f32),
                          _sds((t, MLA_HEADS * MLA_NOPE), bf16), _sds((t, MLA_HEADS * MLA_V), bf16),
                          _sds((t, npair * LANES), f32)],
               sem=("arbitrary", "arbitrary"))(qn, qr, do, lse, delta, knv, knv, kr2)


def _rope_tables(t):
    half = MLA_ROPE // 2
    inv_freq = ROPE_THETA ** (-jnp.arange(half, dtype=f32) / half)
    ang = jnp.arange(t, dtype=f32)[:, None] * inv_freq[None, :]
    cos, sin = jnp.cos(ang), jnp.sin(ang)
    return jnp.concatenate([cos, cos] * 2, axis=1), jnp.concatenate([-sin, sin] * 2, axis=1)


def _relu2_epi(u):
    r = jnp.maximum(u, 0.0)
    return u, r * r


def _add_epi(r, res):
    return (r + res,)


def _drelu2_epi(da, u):
    return (da * 2.0 * jnp.maximum(u.astype(f32), 0.0),)


def _mlp_fwd(h, gain, w_up, w_down, tag):
    xm = _rw(lambda x, g: (_rms(x, g),), [h], [gain], [(D_MODEL, bf16)], name=f"mlp{tag}_norm")[0]
    u, a = _mm(xm, w_up, name=f"mlp{tag}_up", outs=(bf16, bf16), epi=_relu2_epi)
    h_out = _mm(a, w_down, name=f"mlp{tag}_down", epi=_add_epi, extras=(h,))
    return h_out, (xm, u, a)


def _mlp_bwd(dh, h, gain, w_up, w_down, saved, tag):
    xm, u, a = saved
    du = _mm(dh, w_down, tb=True, name=f"mlp{tag}_dact", outs=(bf16,), epi=_drelu2_epi, extras=(u,))
    d_down = _mm(a, dh, ta=True, name=f"mlp{tag}_dwdown")
    d_up = _mm(xm, du, ta=True, name=f"mlp{tag}_dwup")
    dxm = _mm(du, w_up, tb=True, name=f"mlp{tag}_dxm")

    def fn(x, dy, dres, g):
        dx, dg = _rms_bwd(x, g, dy)
        return dx + dres, dg

    dh_in, d_gain = _rw(fn, [h, dxm, dh], [gain], [(D_MODEL, f32)], [(1, D_MODEL)], name=f"mlp{tag}_dnorm")
    return dh_in, d_gain, d_up, d_down


def _local_step(x, target, w):
    t = x.shape[0]
    cos, sgn_sin = _rope_tables(t)
    grads = {}

    xn0 = _rw(lambda xx, g: (_rms(xx, g),), [x], [w["hgrn_norm"]], [(D_MODEL, bf16)], name="hgrn_norm")[0]
    p4 = _mm(xn0, w["hgrn_w4"], name="hgrn_proj")

    def gates_fn(p, lbl):
        return _gates(p[:, :D_MODEL], p[:, D_MODEL:2 * D_MODEL], lbl)

    q, k, g = _rw(gates_fn, [p4], [w["hgrn_lb_logits"]], [(D_MODEL, f32)] * 3, name="hgrn_gates")
    o, states = _gla_fwd(q, k, p4, g)

    def hn_fn(oo, p, gn):
        ys = [_head_norm_gate(oo[:, LANES * h:LANES * (h + 1)], p[:, 3 * D_MODEL + LANES * h:3 * D_MODEL + LANES * (h + 1)], gn)
              for h in range(HGRN_HEADS)]
        return (jnp.concatenate(ys, axis=1),)

    y = _rw(hn_fn, [o, p4], [w["hgrn_g_norm"]], [(D_MODEL, bf16)], name="hgrn_headnorm")[0]
    h1 = _mm(y, w["hgrn_w_o"], name="hgrn_out", epi=_add_epi, extras=(x,))
    h2, mlp0 = _mlp_fwd(h1, w["mlp_norm"][0:1], w["mlp_w_up"][0], w["mlp_w_down"][0], 0)

    hk, xn1 = _rw(lambda hh, g1, g2: (_rms(hh, g1), _rms(hh, g2)), [h2], [w["kv_in_norm"], w["mla_norm"]],
                  [(D_MODEL, bf16)] * 2, name="kv_mla_norm")
    ckr = _mm(hk, w["kv_w_dkv"], name="kv_down")

    def ckv_fn(c, cs, sn, g):
        kr = _rope(c[:, MLA_KV_LORA:], cs, sn)
        return _rms(c[:, :MLA_KV_LORA], g), kr + pltpu.roll(kr, MLA_ROPE, 1)

    c_kv, kr2 = _rw(ckv_fn, [ckr, cos, sgn_sin], [w["kv_norm"]], [(MLA_KV_LORA, bf16), (LANES, bf16)], name="kv_norm_rope")
    knv = _mm(c_kv, w["kv_w_ukv"], name="kv_up", outs=(bf16,))
    cq0 = _mm(xn1, w["mla_w_dq"], name="q_down")
    c_q = _rw(lambda c, g: (_rms(c, g),), [cq0], [w["mla_q_norm"]], [(MLA_Q_LORA, bf16)], name="q_norm")[0]
    qn = _mm(c_q, w["mla_w_uq_n"], name="q_up_nope", outs=(bf16,), epi=lambda r: (r * ATT_SCALE,))
    qr_pre = _mm(c_q, w["mla_w_uq_r"], name="q_up_rope")
    qr = _rw(lambda xx, cs, sn: (_rope(xx, cs, sn) * ATT_SCALE,), [qr_pre, cos, sgn_sin], [],
             [(MLA_HEADS * MLA_ROPE, bf16)], name="q_rope")[0]
    o_att, lse = _attn_fwd(qn, qr, knv, kr2)
    h3 = _mm(o_att, w["mla_w_o"], name="mla_out", epi=_add_epi, extras=(h2,))
    h4, mlp1 = _mlp_fwd(h3, w["mlp_norm"][1:2], w["mlp_w_up"][1], w["mlp_w_down"][1], 1)

    def loss_fn(hh, tgt, gain):
        def f(a, b):
            e = _rms(a, b) - tgt
            return 0.5 * jnp.sum(jnp.sum(e * e, axis=-1, keepdims=True) / D_MODEL, axis=0, keepdims=True)
        val, vjp = jax.vjp(f, hh, gain)
        dh, dg = vjp(jnp.ones((1, 1), f32))
        return dh, jnp.broadcast_to(val, (1, LANES)), dg

    dh4, loss_acc, grads["final_norm"] = _rw(loss_fn, [h4, target], [w["final_norm"]], [(D_MODEL, f32)],
                                             [(1, LANES), (1, D_MODEL)], name="loss")
    loss = loss_acc[0, 0]

    dh3, g_n1, g_up1, g_dn1 = _mlp_bwd(dh4, h3, w["mlp_norm"][1:2], w["mlp_w_up"][1], w["mlp_w_down"][1], mlp1, 1)
    do_att = _mm(dh3, w["mla_w_o"], tb=True, name="mla_dout", outs=(bf16,))
    grads["mla_w_o"] = _mm(o_att, dh3, ta=True, name="mla_dwo")

    def delta_fn(a, b):
        prod = a.astype(f32) * b.astype(f32)
        outs = []
        for p in range(MLA_HEADS // 2):
            d0 = jnp.sum(prod[:, 2 * p * LANES:(2 * p + 1) * LANES], axis=-1, keepdims=True)
            d1 = jnp.sum(prod[:, (2 * p + 1) * LANES:(2 * p + 2) * LANES], axis=-1, keepdims=True)
            lo, _ = _pair_masks((a.shape[0], LANES))
            outs.append(jnp.where(lo, d0, d1))
        return (jnp.concatenate(outs, axis=1),)

    delta = _rw(delta_fn, [do_att, o_att], [], [(MLA_HEADS // 2 * LANES, f32)], name="attn_delta")[0]
    dqn, dqr, dkn, dv, dkr_parts = _attn_bwd(qn, qr, knv, kr2, do_att, lse, delta)

    def dq_fn(a, b, cs, sn):
        return (jnp.concatenate([a, _rope(b, cs, sn, -1.0)], axis=1) * ATT_SCALE,)

    dqf = _rw(dq_fn, [dqn, dqr, cos, sgn_sin], [], [(MLA_HEADS * (MLA_NOPE + MLA_ROPE), bf16)], name="dq_rope")[0]
    dc_q = _mm(dqf, w["mla_w_uq_nr"], tb=True, name="q_up_dx")
    grads["mla_w_uq_nr"] = _mm(c_q, dqf, ta=True, name="q_up_dw")

    def dqn_fn(c, dy, g):
        return _rms_bwd(c, g, dy)

    dcq0, grads["mla_q_norm"] = _rw(dqn_fn, [cq0, dc_q], [w["mla_q_norm"]], [(MLA_Q_LORA, bf16)], [(1, MLA_Q_LORA)],
                                    name="q_dnorm")
    dxn1 = _mm(dcq0, w["mla_w_dq"], tb=True, name="q_down_dx")
    grads["mla_w_dq"] = _mm(xn1, dcq0, ta=True, name="q_down_dw")

    dc_kv = _mm(dkn, w["kv_w_uk"], tb=True, name="kv_up_dx_k")
    dc_kv = _mm(dv, w["kv_w_uv"], tb=True, name="kv_up_dx_v", epi=_add_epi, extras=(dc_kv,))
    grads["kv_w_uk"] = _mm(c_kv, dkn, ta=True, name="kv_up_dw_k")
    grads["kv_w_uv"] = _mm(c_kv, dv, ta=True, name="kv_up_dw_v")

    def dckr_fn(c, dc, dparts, cs, sn, g):
        tot = dparts[:, :LANES]
        for p in range(1, MLA_HEADS // 2):
            tot = tot + dparts[:, p * LANES:(p + 1) * LANES]
        tot = tot + pltpu.roll(tot, MLA_ROPE, 1)
        lo, _ = _pair_masks(tot.shape)
        dkr = jnp.where(lo, _rope(tot, cs, sn, -1.0), 0.0)
        dcc, dg = _rms_bwd(c[:, :MLA_KV_LORA], g, dc)
        return jnp.concatenate([dcc, dkr], axis=1), dg

    dckr, grads["kv_norm"] = _rw(dckr_fn, [ckr, dc_kv, dkr_parts, cos, sgn_sin], [w["kv_norm"]],
                                 [(MLA_KV_LORA + LANES, bf16)], [(1, MLA_KV_LORA)], name="kv_dnorm_rope")
    dhk = _mm(dckr, w["kv_w_dkv"], tb=True, name="kv_down_dx")
    grads["kv_w_dkv"] = _mm(hk, dckr, ta=True, name="kv_down_dw")

    def dh2_fn(hh, d1, d2, dres, g1, g2):
        a, ga = _rms_bwd(hh, g1, d1)
        b, gb = _rms_bwd(hh, g2, d2)
        return a + b + dres, ga, gb

    dh2, grads["kv_in_norm"], grads["mla_norm"] = _rw(dh2_fn, [h2, dhk, dxn1, dh3], [w["kv_in_norm"], w["mla_norm"]],
                                                      [(D_MODEL, f32)], [(1, D_MODEL)] * 2, name="kv_mla_dnorm")

    dh1, g_n0, g_up0, g_dn0 = _mlp_bwd(dh2, h1, w["mlp_norm"][0:1], w["mlp_w_up"][0], w["mlp_w_down"][0], mlp0, 0)
    grads["mlp_norm"] = jnp.concatenate([g_n0, g_n1], axis=0)
    grads["mlp_w_up"] = (g_up0, g_up1)
    grads["mlp_w_down"] = (g_dn0, g_dn1)
    dy = _mm(dh1, w["hgrn_w_o"], tb=True, name="hgrn_dout")
    grads["hgrn_w_o"] = _mm(y, dh1, ta=True, name="hgrn_dwo")

    def dhn_fn(oo, p, dyy, gn):
        dos, dgs, dgn = [], [], jnp.zeros_like(gn)
        for h in range(HGRN_HEADS):
            cols = slice(LANES * h, LANES * (h + 1))
            _, vjp = jax.vjp(_head_norm_gate, oo[:, cols], p[:, 3 * D_MODEL + LANES * h:3 * D_MODEL + LANES * (h + 1)], gn)
            a, b, c = vjp(dyy[:, cols])
            dos.append(a)
            dgs.append(b)
            dgn = dgn + c
        return jnp.concatenate(dos, axis=1), jnp.concatenate(dgs, axis=1), dgn

    do, dgate, grads["hgrn_g_norm"] = _rw(dhn_fn, [o, p4, dy], [w["hgrn_g_norm"]], [(D_MODEL, f32)] * 2, [(1, HGRN_DK)],
                                          name="hgrn_dheadnorm")
    dq, dk, dv_h, dg = _gla_bwd(q, k, p4, g, states, do)

    def dgates_fn(p, dqq, dkk, dgg, dvv, dgt, lbl):
        _, vjp = jax.vjp(_gates, p[:, :D_MODEL], p[:, D_MODEL:2 * D_MODEL], lbl)
        dqp, dfp, dlbl = vjp((dqq, dkk, dgg))
        return jnp.concatenate([dqp, dfp, dvv, dgt], axis=1), dlbl

    dp4, grads["hgrn_lb_logits"] = _rw(dgates_fn, [p4, dq, dk, dg, dv_h, dgate], [w["hgrn_lb_logits"]],
                                       [(4 * D_MODEL, bf16)], [(2, D_MODEL)], name="hgrn_dgates")
    dxn0 = _mm(dp4, w["hgrn_w4"], tb=True, name="hgrn_proj_dx")
    grads["hgrn_w4"] = _mm(xn0, dp4, ta=True, name="hgrn_proj_dw")

    def dx_fn(xx, dyy, dres, gn):
        dxx, dgn = _rms_bwd(xx, gn, dyy)
        return dxx + dres, dgn

    grad_x, grads["hgrn_norm"] = _rw(dx_fn, [x, dxn0, dh1], [w["hgrn_norm"]], [(D_MODEL, f32)], [(1, D_MODEL)],
                                     name="hgrn_dnorm")
    return loss, grad_x, grads


HBM = pl.BlockSpec(memory_space=pltpu.HBM)


def _me():
    return lax.axis_index("x"), lax.axis_index("y"), lax.axis_index("c")


def _flip(x, y, f):
    return (1 - x if f & 1 else x), (1 - y if f & 2 else y)


def _rcopy(src, dst, sems, k, dev):
    return pltpu.make_async_remote_copy(src_ref=src, dst_ref=dst, send_sem=sems.at[0, k], recv_sem=sems.at[1, k],
                                        device_id=dev, device_id_type=MESH)


def _all_gather_weights(wp, sv):
    def body(wp_ref, sv_ref, base_ref, out_ref, svs_ref, sems, local_sem):
        x, y, c = _me()
        half = pl.ds(pl.multiple_of(c * HALF, 16), HALF)
        other = pl.ds(pl.multiple_of((1 - c) * HALF, 16), HALF)
        mine = [pltpu.make_async_copy(sv_ref, svs_ref.at[2 * x + y], local_sem)]
        for cp in mine:
            cp.start()
        sends = []
        for f in (1, 2, 3):
            px, py = _flip(x, y, f)
            sends.append(_rcopy(wp_ref.at[half], out_ref.at[2 * x + y, half], sems, f - 1, (px, py, c)))
            sends.append(_rcopy(sv_ref, svs_ref.at[2 * x + y], sems, 5 + f, (px, py, c)))
        for cp in sends:
            cp.start()
        for f in (1, 2, 3):
            px, py = _flip(x, y, f)
            landed = out_ref.at[2 * px + py, half]
            _rcopy(landed, landed, sems, f - 1, (px, py, c)).wait_recv()
            sends.append(_rcopy(landed, landed, sems, 2 + f, (x, y, 1 - c)))
            sends[-1].start()
        for f in (1, 2, 3):
            px, py = _flip(x, y, f)
            theirs = out_ref.at[2 * px + py, other]
            _rcopy(theirs, theirs, sems, 2 + f, (x, y, 1 - c)).wait_recv()
            _rcopy(sv_ref, svs_ref.at[2 * px + py], sems, 5 + f, (px, py, c)).wait_recv()
        for cp in sends:
            cp.wait_send()
        for cp in mine:
            cp.wait()

    base = jnp.broadcast_to(wp[None], (4, ROWS, D_MODEL))
    return _pc(body, name="weights_all_gather", in_specs=[HBM, HBM, HBM], out_specs=[HBM, HBM],
               out_shape=[_sds((4, ROWS, D_MODEL), bf16), _sds((4, 8, 256), f32)], aliases={2: 0},
               scratch=[pltpu.SemaphoreType.DMA((2, 9)), pltpu.SemaphoreType.DMA])(wp, sv, base)


def _send_half_to_sibling(gp):
    def body(gp_ref, out_ref, sems):
        x, y, c = _me()
        other = pl.ds(pl.multiple_of((1 - c) * HALF, 8), HALF)
        cp = _rcopy(gp_ref.at[:, other], out_ref, sems, 0, (x, y, 1 - c))
        cp.start()
        cp.wait()

    return _pc(body, name="grads_to_sibling", in_specs=[HBM], out_specs=HBM, out_shape=_sds((4, HALF, D_MODEL), f32),
               scratch=[pltpu.SemaphoreType.DMA((2, 1))])(gp)


def _exchange_chips(sb, small):
    def body(sb_ref, small_ref, out_ref, smalls_ref, sems, local_sem):
        x, y, c = _me()
        me = 4 * x + 2 * y + c
        mine = pltpu.make_async_copy(small_ref, smalls_ref.at[me], local_sem)
        mine.start()
        sends = []
        for f in (1, 2, 3):
            px, py = _flip(x, y, f)
            sends.append(_rcopy(sb_ref.at[2 * px + py], out_ref.at[f - 1], sems, f - 1, (px, py, c)))
            sends[-1].start()
        for f in range(1, 8):
            px, py = _flip(x, y, f)
            pc = 1 - c if f & 4 else c
            sends.append(_rcopy(small_ref, smalls_ref.at[me], sems, 2 + f, (px, py, pc)))
            sends[-1].start()
        for f in (1, 2, 3):
            _rcopy(sb_ref.at[0], out_ref.at[f - 1], sems, f - 1, (x, y, c)).wait_recv()
        for f in range(1, 8):
            px, py = _flip(x, y, f)
            pc = 1 - c if f & 4 else c
            _rcopy(small_ref, smalls_ref.at[4 * px + 2 * py + pc], sems, 2 + f, (x, y, c)).wait_recv()
        for cp in sends:
            cp.wait_send()
        mine.wait()

    return _pc(body, name="grads_exchange_chips", in_specs=[HBM, HBM], out_specs=[HBM, HBM],
               out_shape=[_sds((3, HALF, D_MODEL), bf16), _sds((8, SMALL_ROWS, D_MODEL), f32)],
               scratch=[pltpu.SemaphoreType.DMA((2, 10)), pltpu.SemaphoreType.DMA])(sb, small)


def _exchange_halves(tot):
    def body(tot_ref, out_ref, sems):
        x, y, c = _me()
        half = pl.ds(pl.multiple_of(c * HALF, 8), HALF)
        cp = _rcopy(tot_ref.at[half], out_ref.at[half], sems, 0, (x, y, 1 - c))
        cp.start()
        cp.wait()

    return _pc(body, name="grads_exchange_halves", in_specs=[HBM], out_specs=HBM, out_shape=_sds((ROWS, D_MODEL), f32),
               aliases={0: 0}, scratch=[pltpu.SemaphoreType.DMA((2, 1))])(tot)


SUM_TR = 464


def _sum_over_cores(gp, recv, cq):
    nb = HALF // SUM_TR

    def body(cq_ref, g_ref, r_ref, o32_ref, o16_ref):
        s = g_ref[...] + r_ref[...]
        o32_ref[...] = s
        o16_ref[...] = s.astype(bf16)

    spec = pl.BlockSpec((1, SUM_TR, D_MODEL), lambda b, i, cq_ref: (b, i, 0))
    gs = pltpu.PrefetchScalarGridSpec(
        num_scalar_prefetch=1, grid=(4, nb),
        in_specs=[pl.BlockSpec((1, SUM_TR, D_MODEL), lambda b, i, cq_ref: (b, cq_ref[0] * nb + i, 0)), spec],
        out_specs=[spec, spec])
    return _pc(body, name="grads_sum_cores", grid_spec=gs, sem=("arbitrary", "arbitrary"),
               out_shape=[_sds((4, HALF, D_MODEL), f32), _sds((4, HALF, D_MODEL), bf16)])(cq, gp, recv)


def _sum_over_chips(s32, recv, cq):
    nb = HALF // SUM_TR

    def body(cq_ref, own_ref, r_ref, o_ref):
        o_ref[...] = ((own_ref[0] + r_ref[0].astype(f32)) + r_ref[1].astype(f32)) + r_ref[2].astype(f32)

    gs = pltpu.PrefetchScalarGridSpec(
        num_scalar_prefetch=1, grid=(nb,),
        in_specs=[pl.BlockSpec((1, SUM_TR, D_MODEL), lambda i, cq_ref: (cq_ref[1], i, 0)),
                  pl.BlockSpec((3, SUM_TR, D_MODEL), lambda i, cq_ref: (0, i, 0))],
        out_specs=pl.BlockSpec((SUM_TR, D_MODEL), lambda i, cq_ref: (cq_ref[0] * nb + i, 0)))
    return _pc(body, name="grads_sum_chips", grid_spec=gs, sem=("arbitrary",),
               out_shape=_sds((ROWS, D_MODEL), f32))(cq, s32, recv)


def _sum_small(smalls):
    def body(s_ref, o_ref):
        tot = s_ref[0]
        for d in range(1, 8):
            tot = tot + s_ref[d]
        o_ref[...] = tot

    return _pc(body, name="small_sum", out_shape=_sds((SMALL_ROWS, D_MODEL), f32))(smalls)


def _adamw_math(w, g, m, v):
    m = ADAM_B1 * m + (1.0 - ADAM_B1) * g
    v = ADAM_B2 * v + (1.0 - ADAM_B2) * jnp.square(g)
    m_hat = m / (1.0 - ADAM_B1 ** ADAM_STEP)
    v_hat = v / (1.0 - ADAM_B2 ** ADAM_STEP)
    delta = -ADAM_LR * (m_hat / (jnp.sqrt(v_hat) + ADAM_EPS) + ADAM_WD * w)
    return delta, m, v


def _adamw(w, g, m, v, name):
    cols = w.shape[1]
    return _rw(_adamw_math, [w, g, m, v], [], [(cols, f32)] * 3, name=name, tr=256)


def _adamw_small(items):
    n = len(items)

    def body(*refs):
        ins, outs = refs[:4 * n], refs[4 * n:]
        for i in range(n):
            res = _adamw_math(*[r[...] for r in ins[4 * i:4 * i + 4]])
            for o, val in zip(outs[3 * i:3 * i + 3], res):
                o[...] = val

    flat = [a for it in items for a in it]
    out_shape = [_sds(it[0].shape, f32) for it in items for _ in range(3)]
    res = _pc(body, name="adamw_small", out_shape=out_shape)(*flat)
    return [tuple(res[3 * i:3 * i + 3]) for i in range(n)]


def _pack_shards(sh, dtype):
    parts = [sh[n].reshape(-1, D_MODEL).astype(dtype) for n, _ in PACK]
    parts.append(jnp.zeros((PACK_PAD, D_MODEL), dtype))
    return jnp.concatenate(parts, axis=0)


def _unpack_full(g4):
    def rows(name):
        o, r = PACK_OFF[name]
        return g4[:, o:o + r]

    w = {}
    hg = g4[:, 0:1024].reshape(4, 4, 256, D_MODEL)
    w["hgrn_w4"] = hg.transpose(0, 2, 1, 3).reshape(D_MODEL, 4 * D_MODEL)
    w["hgrn_w_o"] = rows("hgrn_w_o").reshape(D_MODEL, D_MODEL)
    w["mla_w_dq"] = rows("mla_w_dq").reshape(D_MODEL, MLA_Q_LORA)
    uq = rows("mla_w_uq").reshape(4, MLA_Q_LORA, 768).transpose(1, 0, 2).reshape(MLA_Q_LORA, MLA_HEADS, MLA_NOPE + MLA_ROPE)
    w["mla_w_uq_n"] = uq[:, :, :MLA_NOPE].reshape(MLA_Q_LORA, MLA_HEADS * MLA_NOPE)
    w["mla_w_uq_r"] = uq[:, :, MLA_NOPE:].reshape(MLA_Q_LORA, MLA_HEADS * MLA_ROPE)
    w["mla_w_uq_nr"] = jnp.concatenate([w["mla_w_uq_n"], w["mla_w_uq_r"]], axis=1)
    w["mla_w_o"] = rows("mla_w_o").reshape(MLA_HEADS * MLA_V, D_MODEL)
    dkv = rows("kv_w_dkv").reshape(D_MODEL, MLA_KV_LORA + MLA_ROPE)
    w["kv_w_dkv"] = jnp.pad(dkv, ((0, 0), (0, LANES - MLA_ROPE)))
    w["kv_w_uk"] = rows("kv_w_uk").reshape(4, MLA_KV_LORA, 512).transpose(1, 0, 2).reshape(MLA_KV_LORA, MLA_HEADS * MLA_NOPE)
    w["kv_w_uv"] = rows("kv_w_uv").reshape(4, MLA_KV_LORA, 512).transpose(1, 0, 2).reshape(MLA_KV_LORA, MLA_HEADS * MLA_V)
    w["kv_w_ukv"] = jnp.concatenate([w["kv_w_uk"], w["kv_w_uv"]], axis=1)
    w["mlp_w_up"] = rows("mlp_w_up").reshape(4, 2, D_MODEL, 1024).transpose(1, 2, 0, 3).reshape(2, D_MODEL, D_FF)
    w["mlp_w_down"] = rows("mlp_w_down").reshape(4, 2, 1024, D_MODEL).transpose(1, 0, 2, 3).reshape(2, D_FF, D_MODEL)
    return w


def _pack_grads(g):
    parts = [g["hgrn_w4"].reshape(4, 256, 4, D_MODEL).transpose(0, 2, 1, 3).reshape(4, 1024, D_MODEL),
             g["hgrn_w_o"].reshape(4, 256, D_MODEL),
             g["mla_w_dq"].reshape(4, 64, D_MODEL)]
    nr = g["mla_w_uq_nr"]
    uq = jnp.concatenate([nr[:, :MLA_HEADS * MLA_NOPE].reshape(MLA_Q_LORA, MLA_HEADS, MLA_NOPE),
                          nr[:, MLA_HEADS * MLA_NOPE:].reshape(MLA_Q_LORA, MLA_HEADS, MLA_ROPE)], axis=2)
    parts.append(uq.reshape(MLA_Q_LORA, 4, 768).transpose(1, 0, 2).reshape(4, 192, D_MODEL))
    parts.append(g["mla_w_o"].reshape(4, 512, D_MODEL))
    parts.append(g["kv_w_dkv"][:, :MLA_KV_LORA + MLA_ROPE].reshape(4, 80, D_MODEL))
    parts.append(g["kv_w_uk"].reshape(MLA_KV_LORA, 4, 512).transpose(1, 0, 2).reshape(4, 128, D_MODEL))
    parts.append(g["kv_w_uv"].reshape(MLA_KV_LORA, 4, 512).transpose(1, 0, 2).reshape(4, 128, D_MODEL))
    up = jnp.stack(g["mlp_w_up"])
    parts.append(up.reshape(2, D_MODEL, 4, 1024).transpose(2, 0, 1, 3).reshape(4, 2048, D_MODEL))
    dn = jnp.stack(g["mlp_w_down"])
    parts.append(dn.reshape(2, 4, 1024, D_MODEL).transpose(1, 0, 2, 3).reshape(4, 2048, D_MODEL))
    parts.append(jnp.zeros((4, PACK_PAD, D_MODEL), f32))
    return jnp.concatenate(parts, axis=1)


LOSS_ROW = 11


def _pack_small(g, loss):
    rows = []
    for name, _, r, wd in SMALL:
        a = g[name].reshape(r, wd)
        rows.append(jnp.pad(a, ((0, 0), (0, D_MODEL - wd))) if wd < D_MODEL else a)
    assert sum(r for _, _, r, _ in SMALL) == LOSS_ROW
    rows.append(jnp.full((1, D_MODEL), loss, f32))
    rows.append(jnp.zeros((SMALL_ROWS - LOSS_ROW - 1, D_MODEL), f32))
    return jnp.concatenate(rows, axis=0)


def kernel(x, hgrn_norm, hgrn_w_q, hgrn_w_f, hgrn_w_i, hgrn_w_g, hgrn_g_norm, hgrn_w_o, hgrn_lb_logits, mla_norm, mla_w_dq, mla_q_norm, mla_w_uq, mla_w_o, kv_in_norm, kv_w_dkv, kv_norm, kv_w_uk, kv_w_uv, mlp_norm, mlp_w_up, mlp_w_down, final_norm, loss_target, m_hgrn_norm, m_hgrn_w_q, m_hgrn_w_f, m_hgrn_w_i, m_hgrn_w_g, m_hgrn_g_norm, m_hgrn_w_o, m_hgrn_lb_logits, m_mla_norm, m_mla_w_dq, m_mla_q_norm, m_mla_w_uq, m_mla_w_o, m_kv_in_norm, m_kv_w_dkv, m_kv_norm, m_kv_w_uk, m_kv_w_uv, m_mlp_norm, m_mlp_w_up, m_mlp_w_down, m_final_norm, v_hgrn_norm, v_hgrn_w_q, v_hgrn_w_f, v_hgrn_w_i, v_hgrn_w_g, v_hgrn_g_norm, v_hgrn_w_o, v_hgrn_lb_logits, v_mla_norm, v_mla_w_dq, v_mla_q_norm, v_mla_w_uq, v_mla_w_o, v_kv_in_norm, v_kv_w_dkv, v_kv_norm, v_kv_w_uk, v_kv_w_uv, v_mlp_norm, v_mlp_w_up, v_mlp_w_down, v_final_norm):
    given = dict(locals())
    wsh = {n: given[n] for n in WEIGHTS}
    msh = {n: given["m_" + n] for n in WEIGHTS}
    vsh = {n: given["v_" + n] for n in WEIGHTS}
    xi, yi, ci = _me()
    chip = 2 * xi + yi
    cq = jnp.stack([ci, chip]).astype(jnp.int32)

    small_w = {n: wsh[n].reshape(r, -1) for n, _, r, _ in SMALL}
    sv = jnp.concatenate([small_w["hgrn_norm"], small_w["hgrn_lb_logits"], jnp.zeros((5, 256), f32)], axis=0)
    g4, sv4 = _all_gather_weights(_pack_shards(wsh, bf16), sv)
    w = _unpack_full(g4)
    w["hgrn_norm"] = sv4[:, 0, :].reshape(1, D_MODEL)
    w["hgrn_lb_logits"] = sv4[:, 1:3, :].transpose(1, 0, 2).reshape(2, D_MODEL)
    for n in ("hgrn_g_norm", "mla_norm", "mla_q_norm", "kv_in_norm", "kv_norm", "mlp_norm", "final_norm"):
        w[n] = small_w[n]

    loss, grad_x, g = _local_step(x.reshape(-1, D_MODEL), loss_target.reshape(-1, D_MODEL), w)

    gp = _pack_grads(g)
    from_sibling = _send_half_to_sibling(gp)
    s32, s16 = _sum_over_cores(gp, from_sibling, cq)
    from_chips, smalls = _exchange_chips(s16, _pack_small(g, loss))
    total = _exchange_halves(_sum_over_chips(s32, from_chips, cq))
    small_tot = _sum_small(smalls)
    loss = small_tot[LOSS_ROW, 0]

    grad, delta, new_m, new_v = {}, {}, {}, {}
    for n, _ in PACK:
        o, r = PACK_OFF[n]
        shp = wsh[n].shape
        two_d = (-1, shp[-1])
        grad[n] = total[o:o + r].reshape(shp)
        d, m2, v2 = _adamw(wsh[n].reshape(two_d), grad[n].reshape(two_d), msh[n].reshape(two_d), vsh[n].reshape(two_d),
                           "adamw_" + n)
        delta[n], new_m[n], new_v[n] = d.reshape(shp), m2.reshape(shp), v2.reshape(shp)
    items = []
    for n, row, r, wd in SMALL:
        gs = small_tot[row:row + r, :wd]
        if n in ("hgrn_norm", "hgrn_lb_logits"):
            gs = lax.dynamic_slice(gs, (0, 256 * chip), (r, 256))
        grad[n] = gs.reshape(wsh[n].shape)
        items.append((small_w[n], gs, msh[n].reshape(gs.shape), vsh[n].reshape(gs.shape)))
    for (n, _, _, _), (d, m2, v2) in zip(SMALL, _adamw_small(items)):
        shp = wsh[n].shape
        delta[n], new_m[n], new_v[n] = d.reshape(shp), m2.reshape(shp), v2.reshape(shp)

    return (loss, grad_x.reshape(x.shape), *[grad[n] for n in WEIGHTS], *[delta[n] for n in WEIGHTS],
            *[new_m[n] for n in WEIGHTS], *[new_v[n] for n in WEIGHTS])
```

```python
import functools

import jax
import jax.numpy as jnp
from jax import lax
from jax.experimental import pallas as pl
from jax.experimental.pallas import tpu as pltpu

f32, bf16 = jnp.float32, jnp.bfloat16
HI = lax.Precision.HIGHEST
MESH = pl.DeviceIdType.MESH

D_MODEL = 1024
D_FF = 4096
EPS = 1e-6
HGRN_HEADS, HGRN_DK, HGRN_CHUNK, HGRN_SUB = 8, 128, 64, 16
MLA_HEADS, MLA_NOPE, MLA_ROPE, MLA_V = 16, 128, 64, 128
MLA_Q_LORA, MLA_KV_LORA = 256, 256
ROPE_THETA = 10000.0
ATT_SCALE = (MLA_NOPE + MLA_ROPE) ** -0.5
EXP_CLAMP = 80.0

ADAM_LR, ADAM_B1, ADAM_B2, ADAM_EPS, ADAM_WD, ADAM_STEP = 0.001, 0.9, 0.999, 1e-08, 0.01, 10

V7X_VMEM_BYTES = 64 * 1024 * 1024
VMEM_LIMIT = V7X_VMEM_BYTES - 8 * 1024 * 1024
LANES = 128

PACK = (("hgrn_w_q", 256), ("hgrn_w_f", 256), ("hgrn_w_i", 256), ("hgrn_w_g", 256), ("hgrn_w_o", 256),
        ("mla_w_dq", 64), ("mla_w_uq", 192), ("mla_w_o", 512), ("kv_w_dkv", 80), ("kv_w_uk", 128),
        ("kv_w_uv", 128), ("mlp_w_up", 2048), ("mlp_w_down", 2048))
PACK_ROWS = sum(r for _, r in PACK)
PACK_PAD = 16
ROWS = PACK_ROWS + PACK_PAD
HALF = ROWS // 2
PACK_OFF = {}
_o = 0
for _n, _r in PACK:
    PACK_OFF[_n] = (_o, _r)
    _o += _r

WEIGHTS = ("hgrn_norm", "hgrn_w_q", "hgrn_w_f", "hgrn_w_i", "hgrn_w_g", "hgrn_g_norm", "hgrn_w_o", "hgrn_lb_logits",
           "mla_norm", "mla_w_dq", "mla_q_norm", "mla_w_uq", "mla_w_o", "kv_in_norm", "kv_w_dkv", "kv_norm", "kv_w_uk",
           "kv_w_uv", "mlp_norm", "mlp_w_up", "mlp_w_down", "final_norm")
SMALL = (("hgrn_norm", 0, 1, 1024), ("hgrn_lb_logits", 1, 2, 1024), ("hgrn_g_norm", 3, 1, 128),
         ("mla_norm", 4, 1, 1024), ("mla_q_norm", 5, 1, 256), ("kv_in_norm", 6, 1, 1024), ("kv_norm", 7, 1, 256),
         ("mlp_norm", 8, 2, 1024), ("final_norm", 10, 1, 1024))
SMALL_ROWS = 16


def _pc(body, *, name, out_shape, grid=None, in_specs=None, out_specs=None, scratch=(), sem=None, grid_spec=None,
        aliases=None):
    params = pltpu.CompilerParams(dimension_semantics=sem, vmem_limit_bytes=VMEM_LIMIT)
    if grid_spec is not None:
        return pl.pallas_call(body, name=name, out_shape=out_shape, grid_spec=grid_spec, compiler_params=params,
                              interpret=False)
    kw = {k: v for k, v in (("grid", grid), ("in_specs", in_specs), ("out_specs", out_specs),
                            ("input_output_aliases", aliases)) if v is not None}
    return pl.pallas_call(body, name=name, out_shape=out_shape, scratch_shapes=list(scratch), compiler_params=params,
                          interpret=False, **kw)


def _sds(shape, dtype):
    return jax.ShapeDtypeStruct(tuple(shape), dtype)


def _mm(a, b, *, name, ta=False, tb=False, outs=(f32,), epi=None, extras=(), tm=1024, tn=1024, tk=512):
    m, k = (a.shape[1], a.shape[0]) if ta else a.shape
    n = b.shape[0] if tb else b.shape[1]
    tm, tn, tk = min(tm, m), min(tn, n), min(tk, k)
    assert m % tm == 0 and n % tn == 0 and k % tk == 0, (name, m, n, k)
    nk = k // tk
    a_spec = pl.BlockSpec((tk, tm), lambda i, j, kk: (kk, i)) if ta else pl.BlockSpec((tm, tk), lambda i, j, kk: (i, kk))
    b_spec = pl.BlockSpec((tn, tk), lambda i, j, kk: (j, kk)) if tb else pl.BlockSpec((tk, tn), lambda i, j, kk: (kk, j))
    e_specs = [pl.BlockSpec((tm, tn), lambda i, j, kk: (i, j)) if e.shape[1] == n else
               pl.BlockSpec((tm, e.shape[1]), lambda i, j, kk: (i, 0)) for e in extras]
    dn = (((0 if ta else 1,), (1 if tb else 0,)), ((), ()))
    n_e, n_o = len(extras), len(outs)

    def body(*refs):
        a_ref, b_ref = refs[0], refs[1]
        e_refs = refs[2:2 + n_e]
        o_refs = refs[2 + n_e:2 + n_e + n_o]
        acc = refs[-1]
        kk = pl.program_id(2)

        @pl.when(kk == 0)
        def _():
            acc[...] = jnp.zeros_like(acc)

        acc[...] += lax.dot_general(a_ref[...].astype(bf16), b_ref[...].astype(bf16), dn, preferred_element_type=f32)

        @pl.when(kk == nk - 1)
        def _():
            r = acc[...]
            res = epi(r, *[e[...] for e in e_refs]) if epi is not None else (r,)
            for o, v in zip(o_refs, res):
                o[...] = v.astype(o.dtype)

    out = _pc(body, name=name, grid=(m // tm, n // tn, nk),
              in_specs=[a_spec, b_spec] + e_specs,
              out_specs=[pl.BlockSpec((tm, tn), lambda i, j, kk: (i, j)) for _ in outs],
              out_shape=[_sds((m, n), dt) for dt in outs],
              scratch=[pltpu.VMEM((tm, tn), f32)],
              sem=("parallel", "parallel", "arbitrary"))(a, b, *extras)
    return out[0] if n_o == 1 else out


def _rw(fn, rows, bcast, outs, accs=(), *, name, tr=256):
    t = rows[0].shape[0]
    tr = min(tr, t)
    assert t % tr == 0
    n_r, n_b, n_o, n_a = len(rows), len(bcast), len(outs), len(accs)

    def body(*refs):
        r_refs = refs[:n_r]
        b_refs = refs[n_r:n_r + n_b]
        o_refs = refs[n_r + n_b:n_r + n_b + n_o]
        a_refs = refs[n_r + n_b + n_o:]
        res = fn(*[r[...] for r in r_refs], *[b[...] for b in b_refs])
        for o, v in zip(o_refs, res[:n_o]):
            o[...] = v.astype(o.dtype)
        i = pl.program_id(0)
        for a_ref, v in zip(a_refs, res[n_o:]):
            @pl.when(i == 0)
            def _(a_ref=a_ref):
                a_ref[...] = jnp.zeros_like(a_ref)
            a_ref[...] += v

    in_specs = [pl.BlockSpec((tr, r.shape[1]), lambda i: (i, 0)) for r in rows]
    in_specs += [pl.BlockSpec(b.shape, lambda i: (0, 0)) for b in bcast]
    out_specs = [pl.BlockSpec((tr, w), lambda i: (i, 0)) for w, _ in outs]
    out_specs += [pl.BlockSpec(s, lambda i: (0, 0)) for s in accs]
    out_shape = [_sds((t, w), dt) for w, dt in outs] + [_sds(s, f32) for s in accs]
    res = _pc(body, name=name, grid=(t // tr,), in_specs=in_specs, out_specs=out_specs, out_shape=out_shape,
              sem=("arbitrary",))(*rows, *bcast)
    return res


def _rms(x, gain):
    return x * lax.rsqrt(jnp.mean(x * x, axis=-1, keepdims=True) + EPS) * gain


def _rms_bwd(x, gain, dy):
    _, vjp = jax.vjp(_rms, x, gain)
    return vjp(dy)


def _lower_bound(lbl):
    l0, l1 = lbl[0:1, :], lbl[1:2, :]
    mx = jnp.maximum(l0, l1)
    e0, e1 = jnp.exp(l0 - mx), jnp.exp(l1 - mx)
    return e0 / (e0 + e1)


def _gates(qpre, fpre, lbl):
    lb = _lower_bound(lbl)
    q = jax.nn.silu(qpre)
    forget = lb + (1.0 - lb) * jax.nn.sigmoid(fpre)
    return q, 1.0 - forget, jnp.log(forget)


def _head_norm_gate(o, gpre, gn):
    return _rms(o, gn) * jax.nn.silu(gpre)


def _swap_halves(x):
    w = x.shape[1]
    lane = lax.broadcasted_iota(jnp.int32, x.shape, 1)
    return jnp.where((lane % MLA_ROPE) < MLA_ROPE // 2, pltpu.roll(x, w - MLA_ROPE // 2, 1),
                     pltpu.roll(x, MLA_ROPE // 2, 1))


def _tile_lanes(tab, w):
    return tab if w == tab.shape[1] else jnp.concatenate([tab] * (w // tab.shape[1]), axis=1)


def _rope(x, cos, sgn_sin, sign=1.0):
    w = x.shape[1]
    return x * _tile_lanes(cos, w) + sign * _swap_halves(x) * _tile_lanes(sgn_sin, w)


def _bd(a, b, ca, cb):
    return lax.dot_general(a.astype(bf16), b.astype(bf16), (((ca,), (cb,)), ((), ())), preferred_element_type=f32)


@jax.custom_vjp
def _dot_nn(a, b):
    return _bd(a, b, 1, 0)


@jax.custom_vjp
def _dot_nt(a, b):
    return _bd(a, b, 1, 1)


@jax.custom_vjp
def _dot_tn(a, b):
    return _bd(a, b, 0, 0)


_dot_nn.defvjp(lambda a, b: (_bd(a, b, 1, 0), (a, b)), lambda r, g: (_bd(g, r[1], 1, 1), _bd(r[0], g, 0, 0)))
_dot_nt.defvjp(lambda a, b: (_bd(a, b, 1, 1), (a, b)), lambda r, g: (_bd(g, r[1], 1, 0), _bd(g, r[0], 0, 0)))
_dot_tn.defvjp(lambda a, b: (_bd(a, b, 0, 0), (a, b)), lambda r, g: (_bd(r[1], g, 1, 1), _bd(r[0], g, 1, 0)))


def _hdot(c, g):
    return jnp.dot(c, g, precision=HI, preferred_element_type=f32)


def _gla_consts():
    c, s = HGRN_CHUNK, HGRN_SUB
    row = lax.broadcasted_iota(jnp.int32, (c, c), 0)
    col = lax.broadcasted_iota(jnp.int32, (c, c), 1)
    incl = (col <= row).astype(f32)
    masks = []
    for i in range(c // s):
        n = s * (i + 1)
        mr = lax.broadcasted_iota(jnp.int32, (s, n), 0) + s * i
        mc = lax.broadcasted_iota(jnp.int32, (s, n), 1)
        masks.append(mc <= mr)
    return incl, masks


def _gla_chunk(consts, dots, q, k, v, g, st):
    incl, masks = consts
    dot_nn, dot_nt, dot_tn = dots
    c, s = HGRN_CHUNK, HGRN_SUB
    b = _hdot(incl, g)
    b_last = b[c - 1:c, :]
    o_inter = dot_nt(q * jnp.exp(b), st)
    st_new = st * jnp.exp(b_last) + dot_tn(v, k * jnp.exp(b_last - b))
    intra = []
    for i in range(c // s):
        n = s * (i + 1)
        ref = b[s * i - 1:s * i, :] if i else jnp.zeros_like(b_last)
        qt = q[s * i:n] * jnp.exp(b[s * i:n] - ref)
        kt = k[:n] * jnp.exp(jnp.minimum(ref - b[:n], EXP_CLAMP))
        sc = jnp.where(masks[i], dot_nt(qt, kt), 0.0)
        intra.append(dot_nn(sc, v[:n]))
    return o_inter + jnp.concatenate(intra, axis=0), st_new


_PLAIN_DOTS = (lambda a, b: _bd(a, b, 1, 0), lambda a, b: _bd(a, b, 1, 1), lambda a, b: _bd(a, b, 0, 0))
_VJP_DOTS = (_dot_nn, _dot_nt, _dot_tn)


def _head_cols(h):
    return slice(HGRN_DK * h, HGRN_DK * (h + 1))


def _gla_fwd(q, k, p4, g):
    t = q.shape[0]
    nc = t // HGRN_CHUNK

    def body(q_ref, k_ref, v_ref, g_ref, o_ref, s_ref, st):
        @pl.when(pl.program_id(0) == 0)
        def _():
            st[...] = jnp.zeros_like(st)

        consts = _gla_consts()
        for h in range(HGRN_HEADS):
            cols = _head_cols(h)
            s_in = st[h]
            s_ref[0, h] = s_in
            o, st_new = _gla_chunk(consts, _PLAIN_DOTS, q_ref[:, cols], k_ref[:, cols], v_ref[:, cols], g_ref[:, cols], s_in)
            o_ref[:, cols] = o
            st[h] = st_new

    blk = lambda off: pl.BlockSpec((HGRN_CHUNK, D_MODEL), lambda c: (c, off))
    state_shape = (HGRN_HEADS, HGRN_DK, HGRN_DK)
    return _pc(body, name="gla_fwd", grid=(nc,),
               in_specs=[blk(0), blk(0), blk(2), blk(0)],
               out_specs=[blk(0), pl.BlockSpec((1,) + state_shape, lambda c: (c, 0, 0, 0))],
               out_shape=[_sds((t, D_MODEL), f32), _sds((nc,) + state_shape, f32)],
               scratch=[pltpu.VMEM(state_shape, f32)], sem=("arbitrary",))(q, k, p4, g)


def _gla_bwd(q, k, p4, g, states, do):
    t = q.shape[0]
    nc = t // HGRN_CHUNK

    def body(q_ref, k_ref, v_ref, g_ref, s_ref, do_ref, dq_ref, dk_ref, dv_ref, dg_ref, dst):
        @pl.when(pl.program_id(0) == 0)
        def _():
            dst[...] = jnp.zeros_like(dst)

        consts = _gla_consts()
        fn = lambda qq, kk, vv, gg, ss: _gla_chunk(consts, _VJP_DOTS, qq, kk, vv, gg, ss)
        for h in range(HGRN_HEADS):
            cols = _head_cols(h)
            _, vjp = jax.vjp(fn, q_ref[:, cols], k_ref[:, cols], v_ref[:, cols], g_ref[:, cols], s_ref[0, h])
            dq, dk, dv, dg, ds = vjp((do_ref[:, cols], dst[h]))
            dq_ref[:, cols] = dq
            dk_ref[:, cols] = dk
            dv_ref[:, cols] = dv
            dg_ref[:, cols] = dg
            dst[h] = ds

    blk = lambda off: pl.BlockSpec((HGRN_CHUNK, D_MODEL), lambda c: (nc - 1 - c, off))
    state_shape = (HGRN_HEADS, HGRN_DK, HGRN_DK)
    return _pc(body, name="gla_bwd", grid=(nc,),
               in_specs=[blk(0), blk(0), blk(2), blk(0),
                         pl.BlockSpec((1,) + state_shape, lambda c: (nc - 1 - c, 0, 0, 0)), blk(0)],
               out_specs=[blk(0)] * 4, out_shape=[_sds((t, D_MODEL), f32)] * 4,
               scratch=[pltpu.VMEM(state_shape, f32)], sem=("arbitrary",))(q, k, p4, g, states, do)


ATT_FWD_TQ, ATT_FWD_TK = 1024, 1024
ATT_BWD_TQ, ATT_BWD_TK = 1024, 512
ATT_QK = 2 * LANES
NEG = -1e30


def _pair_masks(shape):
    lane = lax.broadcasted_iota(jnp.int32, shape, 1)
    return lane < MLA_ROPE, lane >= MLA_ROPE


def _causal(shape, row0, col0):
    row = row0 + lax.broadcasted_iota(jnp.int32, shape, 0)
    col = col0 + lax.broadcasted_iota(jnp.int32, shape, 1)
    return col <= row


def _qk_cols(e):
    return slice(ATT_QK * e, ATT_QK * (e + 1))


def _v_cols(e):
    return slice(MLA_V * e, MLA_V * (e + 1))


def _attn_fwd(qc, kc, v):
    t = qc.shape[0]
    tq, tk = min(ATT_FWD_TQ, t), min(ATT_FWD_TK, t)
    npair = MLA_HEADS // 2

    def body(q_ref, k_ref, v_ref, o_ref, lse_ref):
        i = pl.program_id(1)
        n_full = (i * tq + 1) // tk
        nkv = (i * tq + tq + tk - 1) // tk
        q = [q_ref[:, _qk_cols(e)] for e in range(2)]

        def step(j, carry, masked):
            ks = pl.ds(pl.multiple_of(j * tk, tk), tk)
            ok = _causal((tq, tk), i * tq, j * tk) if masked else None
            new = []
            for e in range(2):
                m, l, acc = carry[e]
                s = _bd(q[e], k_ref[ks, _qk_cols(e)], 1, 1)
                if masked:
                    s = jnp.where(ok, s, NEG)
                m_new = jnp.maximum(m, jnp.max(s, axis=-1, keepdims=True))
                p = jnp.exp(s - m_new)
                alpha = jnp.exp(m - m_new)
                l = alpha * l + jnp.sum(p, axis=-1, keepdims=True)
                acc = alpha * acc + _bd(p, v_ref[ks, _v_cols(e)], 1, 0)
                new.append((m_new, l, acc))
            return tuple(new)

        one = (jnp.full((tq, 1), NEG, f32), jnp.zeros((tq, 1), f32), jnp.zeros((tq, MLA_V), f32))
        carry = lax.fori_loop(0, n_full, functools.partial(step, masked=False), (one, one))
        carry = lax.fori_loop(n_full, nkv, functools.partial(step, masked=True), carry)
        o_ref[...] = jnp.concatenate([acc / l for _, l, acc in carry], axis=1).astype(o_ref.dtype)
        lo, _ = _pair_masks((tq, LANES))
        lse_ref[...] = jnp.where(lo, *[m + jnp.log(l) for m, l, _ in carry])

    return _pc(body, name="attn_fwd", grid=(npair, t // tq),
               in_specs=[pl.BlockSpec((tq, 2 * ATT_QK), lambda p, i: (i, p)),
                         pl.BlockSpec((t, 2 * ATT_QK), lambda p, i: (0, p)),
                         pl.BlockSpec((t, 2 * MLA_V), lambda p, i: (0, p))],
               out_specs=[pl.BlockSpec((tq, 2 * MLA_V), lambda p, i: (i, p)),
                          pl.BlockSpec((tq, LANES), lambda p, i: (i, p))],
               out_shape=[_sds((t, MLA_HEADS * MLA_V), bf16), _sds((t, npair * LANES), f32)],
               sem=("arbitrary", "arbitrary"))(qc, kc, v)


def _attn_bwd(qc, kc, v, do, lse, delta):
    t = qc.shape[0]
    tq, tk = min(ATT_BWD_TQ, t), min(ATT_BWD_TK, t)
    npair = MLA_HEADS // 2
    nq = t // tq

    def body(q_ref, do_ref, lse_ref, dl_ref, k_ref, v_ref, dq_ref, dk_ref, dv_ref):
        j = pl.program_id(1)

        @pl.when(j == 0)
        def _():
            dq_ref[...] = jnp.zeros_like(dq_ref)

        k = [k_ref[:, _qk_cols(e)] for e in range(2)]
        vv = [v_ref[:, _v_cols(e)] for e in range(2)]

        def step(i, carry, masked):
            qs = pl.ds(pl.multiple_of(i * tq, tq), tq)
            ok = _causal((tq, tk), i * tq, j * tk) if masked else None
            lse2, dl2 = lse_ref[qs, :], dl_ref[qs, :]
            new = []
            for e in range(2):
                dk, dv = carry[e]
                q_e, do_e = q_ref[qs, _qk_cols(e)], do_ref[qs, _v_cols(e)]
                p = jnp.exp(_bd(q_e, k[e], 1, 1) - lse2[:, MLA_ROPE * e:MLA_ROPE * e + 1])
                if masked:
                    p = jnp.where(ok, p, 0.0)
                dv = dv + _bd(p, do_e, 0, 0)
                dp = _bd(do_e, vv[e], 1, 1)
                ds = (p * (dp - dl2[:, MLA_ROPE * e:MLA_ROPE * e + 1])).astype(bf16)
                dk = dk + _bd(ds, q_e, 0, 0)
                dq_ref[qs, _qk_cols(e)] += _bd(ds, k[e], 1, 0)
                new.append((dk, dv))
            return tuple(new)

        one = (jnp.zeros((tk, ATT_QK), f32), jnp.zeros((tk, MLA_V), f32))
        i_full = jnp.minimum((j * tk + tk + tq - 2) // tq, nq)
        carry = lax.fori_loop((j * tk) // tq, i_full, functools.partial(step, masked=True), (one, one))
        carry = lax.fori_loop(i_full, nq, functools.partial(step, masked=False), carry)
        for e in range(2):
            dk_ref[:, _qk_cols(e)] = carry[e][0].astype(dk_ref.dtype)
            dv_ref[:, _v_cols(e)] = carry[e][1].astype(dv_ref.dtype)

    res = lambda w: pl.BlockSpec((t, w), lambda p, j: (0, p))
    blk = lambda w: pl.BlockSpec((tk, w), lambda p, j: (j, p))
    return _pc(body, name="attn_bwd", grid=(npair, t // tk),
               in_specs=[res(2 * ATT_QK), res(2 * MLA_V), res(LANES), res(LANES), blk(2 * ATT_QK), blk(2 * MLA_V)],
               out_specs=[res(2 * ATT_QK), blk(2 * ATT_QK), blk(2 * MLA_V)],
               out_shape=[_sds((t, MLA_HEADS * ATT_QK), f32), _sds((t, MLA_HEADS * ATT_QK), bf16),
                          _sds((t, MLA_HEADS * MLA_V), bf16)],
               sem=("arbitrary", "arbitrary"))(qc, do, lse, delta, kc, v)


def _rope_tables(t):
    half = MLA_ROPE // 2
    inv_freq = ROPE_THETA ** (-jnp.arange(half, dtype=f32) / half)
    ang = jnp.arange(t, dtype=f32)[:, None] * inv_freq[None, :]
    cos, sin = jnp.cos(ang), jnp.sin(ang)
    cos128, sin128 = jnp.concatenate([cos, cos] * 2, axis=1), jnp.concatenate([-sin, sin] * 2, axis=1)
    one, zero = jnp.ones((t, MLA_NOPE), f32), jnp.zeros((t, MLA_NOPE), f32)
    return cos128, sin128, jnp.concatenate([one, cos128], axis=1), jnp.concatenate([zero, sin128], axis=1)


def _relu2_epi(u):
    r = jnp.maximum(u, 0.0)
    return u, r * r


def _add_epi(r, res):
    return (r + res,)


def _drelu2_epi(da, u):
    return (da * 2.0 * jnp.maximum(u.astype(f32), 0.0),)


def _mlp_fwd(h, gain, w_up, w_down, tag):
    xm = _rw(lambda x, g: (_rms(x, g),), [h], [gain], [(D_MODEL, bf16)], name=f"mlp{tag}_norm")[0]
    u, a = _mm(xm, w_up, name=f"mlp{tag}_up", outs=(bf16, bf16), epi=_relu2_epi)
    h_out = _mm(a, w_down, name=f"mlp{tag}_down", epi=_add_epi, extras=(h,))
    return h_out, (xm, u, a)


def _mlp_bwd(dh, h, gain, w_up, w_down, saved, tag):
    xm, u, a = saved
    du = _mm(dh, w_down, tb=True, name=f"mlp{tag}_dact", outs=(bf16,), epi=_drelu2_epi, extras=(u,))
    d_down = _mm(a, dh, ta=True, name=f"mlp{tag}_dwdown")
    d_up = _mm(xm, du, ta=True, name=f"mlp{tag}_dwup")
    dxm = _mm(du, w_up, tb=True, name=f"mlp{tag}_dxm")

    def fn(x, dy, dres, g):
        dx, dg = _rms_bwd(x, g, dy)
        return dx + dres, dg

    dh_in, d_gain = _rw(fn, [h, dxm, dh], [gain], [(D_MODEL, f32)], [(1, D_MODEL)], name=f"mlp{tag}_dnorm")
    return dh_in, d_gain, d_up, d_down


def _local_step(x, target, w):
    t = x.shape[0]
    cos, sgn_sin, cos_qk, sin_qk = _rope_tables(t)
    grads = {}

    xn0 = _rw(lambda xx, g: (_rms(xx, g),), [x], [w["hgrn_norm"]], [(D_MODEL, bf16)], name="hgrn_norm")[0]
    p4 = _mm(xn0, w["hgrn_w4"], name="hgrn_proj")

    def gates_fn(p, lbl):
        return _gates(p[:, :D_MODEL], p[:, D_MODEL:2 * D_MODEL], lbl)

    q, k, g = _rw(gates_fn, [p4], [w["hgrn_lb_logits"]], [(D_MODEL, f32)] * 3, name="hgrn_gates")
    o, states = _gla_fwd(q, k, p4, g)

    def hn_fn(oo, p, gn):
        ys = [_head_norm_gate(oo[:, LANES * h:LANES * (h + 1)], p[:, 3 * D_MODEL + LANES * h:3 * D_MODEL + LANES * (h + 1)], gn)
              for h in range(HGRN_HEADS)]
        return (jnp.concatenate(ys, axis=1),)

    y = _rw(hn_fn, [o, p4], [w["hgrn_g_norm"]], [(D_MODEL, bf16)], name="hgrn_headnorm")[0]
    h1 = _mm(y, w["hgrn_w_o"], name="hgrn_out", epi=_add_epi, extras=(x,))
    h2, mlp0 = _mlp_fwd(h1, w["mlp_norm"][0:1], w["mlp_w_up"][0], w["mlp_w_down"][0], 0)

    hk, xn1 = _rw(lambda hh, g1, g2: (_rms(hh, g1), _rms(hh, g2)), [h2], [w["kv_in_norm"], w["mla_norm"]],
                  [(D_MODEL, bf16)] * 2, name="kv_mla_norm")
    ckr = _mm(hk, w["kv_w_dkv"], name="kv_down")

    def ckv_fn(c, cs, sn, g):
        kr = _rope(c[:, MLA_KV_LORA:], cs, sn)
        return _rms(c[:, :MLA_KV_LORA], g), jnp.concatenate([jnp.zeros_like(kr), kr], axis=1)

    c_kv, kr_head = _rw(ckv_fn, [ckr, cos, sgn_sin], [w["kv_norm"]], [(MLA_KV_LORA, bf16), (ATT_QK, f32)],
                        name="kv_norm_rope")
    kc = _mm(c_kv, w["kv_w_kcat"], name="kv_up_k", outs=(bf16,), extras=(kr_head,),
             epi=lambda r, kr: (r + _tile_lanes(kr, r.shape[1]),))
    v_att = _mm(c_kv, w["kv_w_uv"], name="kv_up_v", outs=(bf16,))
    cq0 = _mm(xn1, w["mla_w_dq"], name="q_down")
    c_q = _rw(lambda c, g: (_rms(c, g),), [cq0], [w["mla_q_norm"]], [(MLA_Q_LORA, bf16)], name="q_norm")[0]
    qc = _mm(c_q, w["mla_w_qcat"], name="q_up", outs=(bf16,), extras=(cos_qk, sin_qk),
             epi=lambda r, cs, sn: (_rope(r, cs, sn) * ATT_SCALE,))
    o_att, lse = _attn_fwd(qc, kc, v_att)
    h3 = _mm(o_att, w["mla_w_o"], name="mla_out", epi=_add_epi, extras=(h2,))
    h4, mlp1 = _mlp_fwd(h3, w["mlp_norm"][1:2], w["mlp_w_up"][1], w["mlp_w_down"][1], 1)

    def loss_fn(hh, tgt, gain):
        def f(a, b):
            e = _rms(a, b) - tgt
            return 0.5 * jnp.sum(jnp.sum(e * e, axis=-1, keepdims=True) / D_MODEL, axis=0, keepdims=True)
        val, vjp = jax.vjp(f, hh, gain)
        dh, dg = vjp(jnp.ones((1, 1), f32))
        return dh, jnp.broadcast_to(val, (1, LANES)), dg

    dh4, loss_acc, grads["final_norm"] = _rw(loss_fn, [h4, target], [w["final_norm"]], [(D_MODEL, f32)],
                                             [(1, LANES), (1, D_MODEL)], name="loss")
    loss = loss_acc[0, 0]

    dh3, g_n1, g_up1, g_dn1 = _mlp_bwd(dh4, h3, w["mlp_norm"][1:2], w["mlp_w_up"][1], w["mlp_w_down"][1], mlp1, 1)
    do_att = _mm(dh3, w["mla_w_o"], tb=True, name="mla_dout", outs=(bf16,))
    grads["mla_w_o"] = _mm(o_att, dh3, ta=True, name="mla_dwo")

    def delta_fn(a, b):
        prod = a.astype(f32) * b.astype(f32)
        outs = []
        for p in range(MLA_HEADS // 2):
            d0 = jnp.sum(prod[:, 2 * p * LANES:(2 * p + 1) * LANES], axis=-1, keepdims=True)
            d1 = jnp.sum(prod[:, (2 * p + 1) * LANES:(2 * p + 2) * LANES], axis=-1, keepdims=True)
            lo, _ = _pair_masks((a.shape[0], LANES))
            outs.append(jnp.where(lo, d0, d1))
        return (jnp.concatenate(outs, axis=1),)

    delta = _rw(delta_fn, [do_att, o_att], [], [(MLA_HEADS // 2 * LANES, f32)], name="attn_delta")[0]
    dqc, dkc, dv = _attn_bwd(qc, kc, v_att, do_att, lse, delta)
    dqf = _rw(lambda a, cs, sn: (_rope(a, cs, sn, -1.0) * ATT_SCALE,), [dqc, cos_qk, sin_qk], [],
              [(MLA_HEADS * ATT_QK, bf16)], name="dq_rope")[0]
    dc_q = _mm(dqf, w["mla_w_qcat"], tb=True, name="q_up_dx")
    grads["mla_w_qcat"] = _mm(c_q, dqf, ta=True, name="q_up_dw")

    def dqn_fn(c, dy, g):
        return _rms_bwd(c, g, dy)

    dcq0, grads["mla_q_norm"] = _rw(dqn_fn, [cq0, dc_q], [w["mla_q_norm"]], [(MLA_Q_LORA, bf16)], [(1, MLA_Q_LORA)],
                                    name="q_dnorm")
    dxn1 = _mm(dcq0, w["mla_w_dq"], tb=True, name="q_down_dx")
    grads["mla_w_dq"] = _mm(xn1, dcq0, ta=True, name="q_down_dw")

    dc_kv = _mm(dkc, w["kv_w_kcat"], tb=True, name="kv_up_dx_k")
    dc_kv = _mm(dv, w["kv_w_uv"], tb=True, name="kv_up_dx_v", epi=_add_epi, extras=(dc_kv,))
    grads["kv_w_kcat"] = _mm(c_kv, dkc, ta=True, name="kv_up_dw_k")
    grads["kv_w_uv"] = _mm(c_kv, dv, ta=True, name="kv_up_dw_v")

    def dckr_fn(c, dc, dk_heads, cs, sn, g):
        tot = dk_heads[:, LANES:ATT_QK].astype(f32)
        for h in range(1, MLA_HEADS):
            tot = tot + dk_heads[:, ATT_QK * h + LANES:ATT_QK * (h + 1)].astype(f32)
        lo, _ = _pair_masks(tot.shape)
        dkr = jnp.where(lo, _rope(tot, cs, sn, -1.0), 0.0)
        dcc, dg = _rms_bwd(c[:, :MLA_KV_LORA], g, dc)
        return jnp.concatenate([dcc, dkr], axis=1), dg

    dckr, grads["kv_norm"] = _rw(dckr_fn, [ckr, dc_kv, dkc, cos, sgn_sin], [w["kv_norm"]],
                                 [(MLA_KV_LORA + LANES, bf16)], [(1, MLA_KV_LORA)], name="kv_dnorm_rope")
    dhk = _mm(dckr, w["kv_w_dkv"], tb=True, name="kv_down_dx")
    grads["kv_w_dkv"] = _mm(hk, dckr, ta=True, name="kv_down_dw")

    def dh2_fn(hh, d1, d2, dres, g1, g2):
        a, ga = _rms_bwd(hh, g1, d1)
        b, gb = _rms_bwd(hh, g2, d2)
        return a + b + dres, ga, gb

    dh2, grads["kv_in_norm"], grads["mla_norm"] = _rw(dh2_fn, [h2, dhk, dxn1, dh3], [w["kv_in_norm"], w["mla_norm"]],
                                                      [(D_MODEL, f32)], [(1, D_MODEL)] * 2, name="kv_mla_dnorm")

    dh1, g_n0, g_up0, g_dn0 = _mlp_bwd(dh2, h1, w["mlp_norm"][0:1], w["mlp_w_up"][0], w["mlp_w_down"][0], mlp0, 0)
    grads["mlp_norm"] = jnp.concatenate([g_n0, g_n1], axis=0)
    grads["mlp_w_up"] = (g_up0, g_up1)
    grads["mlp_w_down"] = (g_dn0, g_dn1)
    dy = _mm(dh1, w["hgrn_w_o"], tb=True, name="hgrn_dout")
    grads["hgrn_w_o"] = _mm(y, dh1, ta=True, name="hgrn_dwo")

    def dhn_fn(oo, p, dyy, gn):
        dos, dgs, dgn = [], [], jnp.zeros_like(gn)
        for h in range(HGRN_HEADS):
            cols = slice(LANES * h, LANES * (h + 1))
            _, vjp = jax.vjp(_head_norm_gate, oo[:, cols], p[:, 3 * D_MODEL + LANES * h:3 * D_MODEL + LANES * (h + 1)], gn)
            a, b, c = vjp(dyy[:, cols])
            dos.append(a)
            dgs.append(b)
            dgn = dgn + c
        return jnp.concatenate(dos, axis=1), jnp.concatenate(dgs, axis=1), dgn

    do, dgate, grads["hgrn_g_norm"] = _rw(dhn_fn, [o, p4, dy], [w["hgrn_g_norm"]], [(D_MODEL, f32)] * 2, [(1, HGRN_DK)],
                                          name="hgrn_dheadnorm")
    dq, dk, dv_h, dg = _gla_bwd(q, k, p4, g, states, do)

    def dgates_fn(p, dqq, dkk, dgg, dvv, dgt, lbl):
        _, vjp = jax.vjp(_gates, p[:, :D_MODEL], p[:, D_MODEL:2 * D_MODEL], lbl)
        dqp, dfp, dlbl = vjp((dqq, dkk, dgg))
        return jnp.concatenate([dqp, dfp, dvv, dgt], axis=1), dlbl

    dp4, grads["hgrn_lb_logits"] = _rw(dgates_fn, [p4, dq, dk, dg, dv_h, dgate], [w["hgrn_lb_logits"]],
                                       [(4 * D_MODEL, bf16)], [(2, D_MODEL)], name="hgrn_dgates")
    dxn0 = _mm(dp4, w["hgrn_w4"], tb=True, name="hgrn_proj_dx")
    grads["hgrn_w4"] = _mm(xn0, dp4, ta=True, name="hgrn_proj_dw")

    def dx_fn(xx, dyy, dres, gn):
        dxx, dgn = _rms_bwd(xx, gn, dyy)
        return dxx + dres, dgn

    grad_x, grads["hgrn_norm"] = _rw(dx_fn, [x, dxn0, dh1], [w["hgrn_norm"]], [(D_MODEL, f32)], [(1, D_MODEL)],
                                     name="hgrn_dnorm")
    return loss, grad_x, grads


HBM = pl.BlockSpec(memory_space=pltpu.HBM)


def _me():
    return lax.axis_index("x"), lax.axis_index("y"), lax.axis_index("c")


def _flip(x, y, f):
    return (1 - x if f & 1 else x), (1 - y if f & 2 else y)


def _rcopy(src, dst, sems, k, dev):
    return pltpu.make_async_remote_copy(src_ref=src, dst_ref=dst, send_sem=sems.at[0, k], recv_sem=sems.at[1, k],
                                        device_id=dev, device_id_type=MESH)


def _all_gather_weights(wp, sv):
    def body(wp_ref, sv_ref, base_ref, out_ref, svs_ref, sems, local_sem):
        x, y, c = _me()
        half = pl.ds(pl.multiple_of(c * HALF, 16), HALF)
        other = pl.ds(pl.multiple_of((1 - c) * HALF, 16), HALF)
        mine = [pltpu.make_async_copy(sv_ref, svs_ref.at[2 * x + y], local_sem)]
        for cp in mine:
            cp.start()
        sends = []
        for f in (1, 2, 3):
            px, py = _flip(x, y, f)
            sends.append(_rcopy(wp_ref.at[half], out_ref.at[2 * x + y, half], sems, f - 1, (px, py, c)))
            sends.append(_rcopy(sv_ref, svs_ref.at[2 * x + y], sems, 5 + f, (px, py, c)))
        for cp in sends:
            cp.start()
        for f in (1, 2, 3):
            px, py = _flip(x, y, f)
            landed = out_ref.at[2 * px + py, half]
            _rcopy(landed, landed, sems, f - 1, (px, py, c)).wait_recv()
            sends.append(_rcopy(landed, landed, sems, 2 + f, (x, y, 1 - c)))
            sends[-1].start()
        for f in (1, 2, 3):
            px, py = _flip(x, y, f)
            theirs = out_ref.at[2 * px + py, other]
            _rcopy(theirs, theirs, sems, 2 + f, (x, y, 1 - c)).wait_recv()
            _rcopy(sv_ref, svs_ref.at[2 * px + py], sems, 5 + f, (px, py, c)).wait_recv()
        for cp in sends:
            cp.wait_send()
        for cp in mine:
            cp.wait()

    base = jnp.broadcast_to(wp[None], (4, ROWS, D_MODEL))
    return _pc(body, name="weights_all_gather", in_specs=[HBM, HBM, HBM], out_specs=[HBM, HBM],
               out_shape=[_sds((4, ROWS, D_MODEL), bf16), _sds((4, 8, 256), f32)], aliases={2: 0},
               scratch=[pltpu.SemaphoreType.DMA((2, 9)), pltpu.SemaphoreType.DMA])(wp, sv, base)


def _send_half_to_sibling(gp):
    def body(gp_ref, out_ref, sems):
        x, y, c = _me()
        other = pl.ds(pl.multiple_of((1 - c) * HALF, 8), HALF)
        cp = _rcopy(gp_ref.at[:, other], out_ref, sems, 0, (x, y, 1 - c))
        cp.start()
        cp.wait()

    return _pc(body, name="grads_to_sibling", in_specs=[HBM], out_specs=HBM, out_shape=_sds((4, HALF, D_MODEL), f32),
               scratch=[pltpu.SemaphoreType.DMA((2, 1))])(gp)


def _exchange_chips(sb, small):
    def body(sb_ref, small_ref, out_ref, smalls_ref, sems, local_sem):
        x, y, c = _me()
        me = 4 * x + 2 * y + c
        mine = pltpu.make_async_copy(small_ref, smalls_ref.at[me], local_sem)
        mine.start()
        sends = []
        for f in (1, 2, 3):
            px, py = _flip(x, y, f)
            sends.append(_rcopy(sb_ref.at[2 * px + py], out_ref.at[f - 1], sems, f - 1, (px, py, c)))
            sends[-1].start()
        for f in range(1, 8):
            px, py = _flip(x, y, f)
            pc = 1 - c if f & 4 else c
            sends.append(_rcopy(small_ref, smalls_ref.at[me], sems, 2 + f, (px, py, pc)))
            sends[-1].start()
        for f in (1, 2, 3):
            _rcopy(sb_ref.at[0], out_ref.at[f - 1], sems, f - 1, (x, y, c)).wait_recv()
        for f in range(1, 8):
            px, py = _flip(x, y, f)
            pc = 1 - c if f & 4 else c
            _rcopy(small_ref, smalls_ref.at[4 * px + 2 * py + pc], sems, 2 + f, (x, y, c)).wait_recv()
        for cp in sends:
            cp.wait_send()
        mine.wait()

    return _pc(body, name="grads_exchange_chips", in_specs=[HBM, HBM], out_specs=[HBM, HBM],
               out_shape=[_sds((3, HALF, D_MODEL), bf16), _sds((8, SMALL_ROWS, D_MODEL), f32)],
               scratch=[pltpu.SemaphoreType.DMA((2, 10)), pltpu.SemaphoreType.DMA])(sb, small)


def _exchange_halves(tot):
    def body(tot_ref, out_ref, sems):
        x, y, c = _me()
        half = pl.ds(pl.multiple_of(c * HALF, 8), HALF)
        cp = _rcopy(tot_ref.at[half], out_ref.at[half], sems, 0, (x, y, 1 - c))
        cp.start()
        cp.wait()

    return _pc(body, name="grads_exchange_halves", in_specs=[HBM], out_specs=HBM, out_shape=_sds((ROWS, D_MODEL), f32),
               aliases={0: 0}, scratch=[pltpu.SemaphoreType.DMA((2, 1))])(tot)


SUM_TR = 464


def _sum_over_cores(gp, recv, cq):
    nb = HALF // SUM_TR

    def body(cq_ref, g_ref, r_ref, o32_ref, o16_ref):
        s = g_ref[...] + r_ref[...]
        o32_ref[...] = s
        o16_ref[...] = s.astype(bf16)

    spec = pl.BlockSpec((1, SUM_TR, D_MODEL), lambda b, i, cq_ref: (b, i, 0))
    gs = pltpu.PrefetchScalarGridSpec(
        num_scalar_prefetch=1, grid=(4, nb),
        in_specs=[pl.BlockSpec((1, SUM_TR, D_MODEL), lambda b, i, cq_ref: (b, cq_ref[0] * nb + i, 0)), spec],
        out_specs=[spec, spec])
    return _pc(body, name="grads_sum_cores", grid_spec=gs, sem=("arbitrary", "arbitrary"),
               out_shape=[_sds((4, HALF, D_MODEL), f32), _sds((4, HALF, D_MODEL), bf16)])(cq, gp, recv)


def _sum_over_chips(s32, recv, cq):
    nb = HALF // SUM_TR

    def body(cq_ref, own_ref, r_ref, o_ref):
        o_ref[...] = ((own_ref[0] + r_ref[0].astype(f32)) + r_ref[1].astype(f32)) + r_ref[2].astype(f32)

    gs = pltpu.PrefetchScalarGridSpec(
        num_scalar_prefetch=1, grid=(nb,),
        in_specs=[pl.BlockSpec((1, SUM_TR, D_MODEL), lambda i, cq_ref: (cq_ref[1], i, 0)),
                  pl.BlockSpec((3, SUM_TR, D_MODEL), lambda i, cq_ref: (0, i, 0))],
        out_specs=pl.BlockSpec((SUM_TR, D_MODEL), lambda i, cq_ref: (cq_ref[0] * nb + i, 0)))
    return _pc(body, name="grads_sum_chips", grid_spec=gs, sem=("arbitrary",),
               out_shape=_sds((ROWS, D_MODEL), f32))(cq, s32, recv)


def _sum_small(smalls):
    def body(s_ref, o_ref):
        tot = s_ref[0]
        for d in range(1, 8):
            tot = tot + s_ref[d]
        o_ref[...] = tot

    return _pc(body, name="small_sum", out_shape=_sds((SMALL_ROWS, D_MODEL), f32))(smalls)


def _adamw_math(w, g, m, v):
    m = ADAM_B1 * m + (1.0 - ADAM_B1) * g
    v = ADAM_B2 * v + (1.0 - ADAM_B2) * jnp.square(g)
    m_hat = m / (1.0 - ADAM_B1 ** ADAM_STEP)
    v_hat = v / (1.0 - ADAM_B2 ** ADAM_STEP)
    delta = -ADAM_LR * (m_hat / (jnp.sqrt(v_hat) + ADAM_EPS) + ADAM_WD * w)
    return delta, m, v


def _adamw(w, g, m, v, name):
    cols = w.shape[1]
    return _rw(_adamw_math, [w, g, m, v], [], [(cols, f32)] * 3, name=name, tr=256)


def _adamw_small(items):
    n = len(items)

    def body(*refs):
        ins, outs = refs[:4 * n], refs[4 * n:]
        for i in range(n):
            res = _adamw_math(*[r[...] for r in ins[4 * i:4 * i + 4]])
            for o, val in zip(outs[3 * i:3 * i + 3], res):
                o[...] = val

    flat = [a for it in items for a in it]
    out_shape = [_sds(it[0].shape, f32) for it in items for _ in range(3)]
    res = _pc(body, name="adamw_small", out_shape=out_shape)(*flat)
    return [tuple(res[3 * i:3 * i + 3]) for i in range(n)]


def _pack_shards(sh, dtype):
    parts = [sh[n].reshape(-1, D_MODEL).astype(dtype) for n, _ in PACK]
    parts.append(jnp.zeros((PACK_PAD, D_MODEL), dtype))
    return jnp.concatenate(parts, axis=0)


def _unpack_full(g4):
    def rows(name):
        o, r = PACK_OFF[name]
        return g4[:, o:o + r]

    w = {}
    hg = g4[:, 0:1024].reshape(4, 4, 256, D_MODEL)
    w["hgrn_w4"] = hg.transpose(0, 2, 1, 3).reshape(D_MODEL, 4 * D_MODEL)
    w["hgrn_w_o"] = rows("hgrn_w_o").reshape(D_MODEL, D_MODEL)
    w["mla_w_dq"] = rows("mla_w_dq").reshape(D_MODEL, MLA_Q_LORA)
    uq = rows("mla_w_uq").reshape(4, MLA_Q_LORA, 768).transpose(1, 0, 2).reshape(MLA_Q_LORA, MLA_HEADS, MLA_NOPE + MLA_ROPE)
    w["mla_w_qcat"] = jnp.pad(uq, ((0, 0), (0, 0), (0, ATT_QK - MLA_NOPE - MLA_ROPE))).reshape(MLA_Q_LORA, MLA_HEADS * ATT_QK)
    w["mla_w_o"] = rows("mla_w_o").reshape(MLA_HEADS * MLA_V, D_MODEL)
    dkv = rows("kv_w_dkv").reshape(D_MODEL, MLA_KV_LORA + MLA_ROPE)
    w["kv_w_dkv"] = jnp.pad(dkv, ((0, 0), (0, LANES - MLA_ROPE)))
    uk = rows("kv_w_uk").reshape(4, MLA_KV_LORA, 512).transpose(1, 0, 2).reshape(MLA_KV_LORA, MLA_HEADS, MLA_NOPE)
    w["kv_w_kcat"] = jnp.pad(uk, ((0, 0), (0, 0), (0, ATT_QK - MLA_NOPE))).reshape(MLA_KV_LORA, MLA_HEADS * ATT_QK)
    w["kv_w_uv"] = rows("kv_w_uv").reshape(4, MLA_KV_LORA, 512).transpose(1, 0, 2).reshape(MLA_KV_LORA, MLA_HEADS * MLA_V)
    w["mlp_w_up"] = rows("mlp_w_up").reshape(4, 2, D_MODEL, 1024).transpose(1, 2, 0, 3).reshape(2, D_MODEL, D_FF)
    w["mlp_w_down"] = rows("mlp_w_down").reshape(4, 2, 1024, D_MODEL).transpose(1, 0, 2, 3).reshape(2, D_FF, D_MODEL)
    return w


def _pack_grads(g):
    parts = [g["hgrn_w4"].reshape(4, 256, 4, D_MODEL).transpose(0, 2, 1, 3).reshape(4, 1024, D_MODEL),
             g["hgrn_w_o"].reshape(4, 256, D_MODEL),
             g["mla_w_dq"].reshape(4, 64, D_MODEL)]
    uq = g["mla_w_qcat"].reshape(MLA_Q_LORA, MLA_HEADS, ATT_QK)[:, :, :MLA_NOPE + MLA_ROPE]
    parts.append(uq.reshape(MLA_Q_LORA, 4, 768).transpose(1, 0, 2).reshape(4, 192, D_MODEL))
    parts.append(g["mla_w_o"].reshape(4, 512, D_MODEL))
    parts.append(g["kv_w_dkv"][:, :MLA_KV_LORA + MLA_ROPE].reshape(4, 80, D_MODEL))
    uk = g["kv_w_kcat"].reshape(MLA_KV_LORA, MLA_HEADS, ATT_QK)[:, :, :MLA_NOPE]
    parts.append(uk.reshape(MLA_KV_LORA, 4, 512).transpose(1, 0, 2).reshape(4, 128, D_MODEL))
    parts.append(g["kv_w_uv"].reshape(MLA_KV_LORA, 4, 512).transpose(1, 0, 2).reshape(4, 128, D_MODEL))
    up = jnp.stack(g["mlp_w_up"])
    parts.append(up.reshape(2, D_MODEL, 4, 1024).transpose(2, 0, 1, 3).reshape(4, 2048, D_MODEL))
    dn = jnp.stack(g["mlp_w_down"])
    parts.append(dn.reshape(2, 4, 1024, D_MODEL).transpose(1, 0, 2, 3).reshape(4, 2048, D_MODEL))
    parts.append(jnp.zeros((4, PACK_PAD, D_MODEL), f32))
    return jnp.concatenate(parts, axis=1)


LOSS_ROW = 11


def _pack_small(g, loss):
    rows = []
    for name, _, r, wd in SMALL:
        a = g[name].reshape(r, wd)
        rows.append(jnp.pad(a, ((0, 0), (0, D_MODEL - wd))) if wd < D_MODEL else a)
    assert sum(r for _, _, r, _ in SMALL) == LOSS_ROW
    rows.append(jnp.full((1, D_MODEL), loss, f32))
    rows.append(jnp.zeros((SMALL_ROWS - LOSS_ROW - 1, D_MODEL), f32))
    return jnp.concatenate(rows, axis=0)


def kernel(x, hgrn_norm, hgrn_w_q, hgrn_w_f, hgrn_w_i, hgrn_w_g, hgrn_g_norm, hgrn_w_o, hgrn_lb_logits, mla_norm, mla_w_dq, mla_q_norm, mla_w_uq, mla_w_o, kv_in_norm, kv_w_dkv, kv_norm, kv_w_uk, kv_w_uv, mlp_norm, mlp_w_up, mlp_w_down, final_norm, loss_target, m_hgrn_norm, m_hgrn_w_q, m_hgrn_w_f, m_hgrn_w_i, m_hgrn_w_g, m_hgrn_g_norm, m_hgrn_w_o, m_hgrn_lb_logits, m_mla_norm, m_mla_w_dq, m_mla_q_norm, m_mla_w_uq, m_mla_w_o, m_kv_in_norm, m_kv_w_dkv, m_kv_norm, m_kv_w_uk, m_kv_w_uv, m_mlp_norm, m_mlp_w_up, m_mlp_w_down, m_final_norm, v_hgrn_norm, v_hgrn_w_q, v_hgrn_w_f, v_hgrn_w_i, v_hgrn_w_g, v_hgrn_g_norm, v_hgrn_w_o, v_hgrn_lb_logits, v_mla_norm, v_mla_w_dq, v_mla_q_norm, v_mla_w_uq, v_mla_w_o, v_kv_in_norm, v_kv_w_dkv, v_kv_norm, v_kv_w_uk, v_kv_w_uv, v_mlp_norm, v_mlp_w_up, v_mlp_w_down, v_final_norm):
    given = dict(locals())
    wsh = {n: given[n] for n in WEIGHTS}
    msh = {n: given["m_" + n] for n in WEIGHTS}
    vsh = {n: given["v_" + n] for n in WEIGHTS}
    xi, yi, ci = _me()
    chip = 2 * xi + yi
    cq = jnp.stack([ci, chip]).astype(jnp.int32)

    small_w = {n: wsh[n].reshape(r, -1) for n, _, r, _ in SMALL}
    sv = jnp.concatenate([small_w["hgrn_norm"], small_w["hgrn_lb_logits"], jnp.zeros((5, 256), f32)], axis=0)
    g4, sv4 = _all_gather_weights(_pack_shards(wsh, bf16), sv)
    w = _unpack_full(g4)
    w["hgrn_norm"] = sv4[:, 0, :].reshape(1, D_MODEL)
    w["hgrn_lb_logits"] = sv4[:, 1:3, :].transpose(1, 0, 2).reshape(2, D_MODEL)
    for n in ("hgrn_g_norm", "mla_norm", "mla_q_norm", "kv_in_norm", "kv_norm", "mlp_norm", "final_norm"):
        w[n] = small_w[n]

    loss, grad_x, g = _local_step(x.reshape(-1, D_MODEL), loss_target.reshape(-1, D_MODEL), w)

    gp = _pack_grads(g)
    from_sibling = _send_half_to_sibling(gp)
    s32, s16 = _sum_over_cores(gp, from_sibling, cq)
    from_chips, smalls = _exchange_chips(s16, _pack_small(g, loss))
    total = _exchange_halves(_sum_over_chips(s32, from_chips, cq))
    small_tot = _sum_small(smalls)
    loss = small_tot[LOSS_ROW, 0]

    grad, delta, new_m, new_v = {}, {}, {}, {}
    for n, _ in PACK:
        o, r = PACK_OFF[n]
        shp = wsh[n].shape
        two_d = (-1, shp[-1])
        grad[n] = total[o:o + r].reshape(shp)
        d, m2, v2 = _adamw(wsh[n].reshape(two_d), grad[n].reshape(two_d), msh[n].reshape(two_d), vsh[n].reshape(two_d),
                           "adamw_" + n)
        delta[n], new_m[n], new_v[n] = d.reshape(shp), m2.reshape(shp), v2.reshape(shp)
    items = []
    for n, row, r, wd in SMALL:
        gs = small_tot[row:row + r, :wd]
        if n in ("hgrn_norm", "hgrn_lb_logits"):
            gs = lax.dynamic_slice(gs, (0, 256 * chip), (r, 256))
        grad[n] = gs.reshape(wsh[n].shape)
        items.append((small_w[n], gs, msh[n].reshape(gs.shape), vsh[n].reshape(gs.shape)))
    for (n, _, _, _), (d, m2, v2) in zip(SMALL, _adamw_small(items)):
        shp = wsh[n].shape
        delta[n], new_m[n], new_v[n] = d.reshape(shp), m2.reshape(shp), v2.reshape(shp)

    return (loss, grad_x.reshape(x.shape), *[grad[n] for n in WEIGHTS], *[delta[n] for n in WEIGHTS],
            *[new_m[n] for n in WEIGHTS], *[new_v[n] for n in WEIGHTS])
```

```python
import functools

import jax
import jax.numpy as jnp
from jax import lax
from jax.experimental import pallas as pl
from jax.experimental.pallas import tpu as pltpu

f32, bf16 = jnp.float32, jnp.bfloat16
HI = lax.Precision.HIGHEST
MESH = pl.DeviceIdType.MESH

D_MODEL = 1024
D_FF = 4096
EPS = 1e-6
HGRN_HEADS, HGRN_DK, HGRN_CHUNK, HGRN_SUB = 8, 128, 64, 16
MLA_HEADS, MLA_NOPE, MLA_ROPE, MLA_V = 16, 128, 64, 128
MLA_Q_LORA, MLA_KV_LORA = 256, 256
ROPE_THETA = 10000.0
ATT_SCALE = (MLA_NOPE + MLA_ROPE) ** -0.5
EXP_CLAMP = 80.0

ADAM_LR, ADAM_B1, ADAM_B2, ADAM_EPS, ADAM_WD, ADAM_STEP = 0.001, 0.9, 0.999, 1e-08, 0.01, 10

V7X_VMEM_BYTES = 64 * 1024 * 1024
VMEM_LIMIT = V7X_VMEM_BYTES - 8 * 1024 * 1024
LANES = 128

PACK_PAD = 16
W_EARLY = (("hgrn_w_q", None, 256), ("hgrn_w_f", None, 256), ("hgrn_w_i", None, 256), ("hgrn_w_g", None, 256),
           ("hgrn_w_o", None, 256), ("mlp_w_up", 0, 1024), ("mlp_w_down", 0, 1024))
W_LATE = (("mla_w_dq", None, 64), ("mla_w_uq", None, 192), ("mla_w_o", None, 512), ("kv_w_dkv", None, 80),
          ("kv_w_uk", None, 128), ("kv_w_uv", None, 128), ("mlp_w_up", 1, 1024), ("mlp_w_down", 1, 1024))
G_LATE = (("hgrn_w_q", 256), ("hgrn_w_f", 256), ("hgrn_w_i", 256), ("hgrn_w_g", 256))
G_EARLY = (("hgrn_w_o", 256), ("mla_w_dq", 64), ("mla_w_uq", 192), ("mla_w_o", 512), ("kv_w_dkv", 80),
           ("kv_w_uk", 128), ("kv_w_uv", 128), ("mlp_w_up", 2048), ("mlp_w_down", 2048))


def _offsets(layout):
    out, o = {}, 0
    for entry in layout:
        out[entry[:-1] if len(entry) == 3 else entry[0]] = (o, entry[-1])
        o += entry[-1]
    return out, o


W_EARLY_OFF, W_EARLY_ROWS = _offsets(W_EARLY)
W_LATE_OFF, _w_late = _offsets(W_LATE)
W_LATE_ROWS = _w_late + PACK_PAD
G_LATE_OFF, G_LATE_ROWS = _offsets(G_LATE)
G_EARLY_OFF, _g_early = _offsets(G_EARLY)
G_EARLY_ROWS = _g_early + PACK_PAD
assert all(r % 32 == 0 for r in (W_EARLY_ROWS, W_LATE_ROWS, G_LATE_ROWS, G_EARLY_ROWS))

WEIGHTS = ("hgrn_norm", "hgrn_w_q", "hgrn_w_f", "hgrn_w_i", "hgrn_w_g", "hgrn_g_norm", "hgrn_w_o", "hgrn_lb_logits",
           "mla_norm", "mla_w_dq", "mla_q_norm", "mla_w_uq", "mla_w_o", "kv_in_norm", "kv_w_dkv", "kv_norm", "kv_w_uk",
           "kv_w_uv", "mlp_norm", "mlp_w_up", "mlp_w_down", "final_norm")
SMALL = (("hgrn_norm", 0, 1, 1024), ("hgrn_lb_logits", 1, 2, 1024), ("hgrn_g_norm", 3, 1, 128),
         ("mla_norm", 4, 1, 1024), ("mla_q_norm", 5, 1, 256), ("kv_in_norm", 6, 1, 1024), ("kv_norm", 7, 1, 256),
         ("mlp_norm", 8, 2, 1024), ("final_norm", 10, 1, 1024))
SMALL_ROWS = 16


def _pc(body, *, name, out_shape, grid=None, in_specs=None, out_specs=None, scratch=(), sem=None, grid_spec=None,
        aliases=None):
    params = pltpu.CompilerParams(dimension_semantics=sem, vmem_limit_bytes=VMEM_LIMIT)
    if grid_spec is not None:
        return pl.pallas_call(body, name=name, out_shape=out_shape, grid_spec=grid_spec, compiler_params=params,
                              interpret=False)
    kw = {k: v for k, v in (("grid", grid), ("in_specs", in_specs), ("out_specs", out_specs),
                            ("input_output_aliases", aliases)) if v is not None}
    return pl.pallas_call(body, name=name, out_shape=out_shape, scratch_shapes=list(scratch), compiler_params=params,
                          interpret=False, **kw)


def _sds(shape, dtype):
    return jax.ShapeDtypeStruct(tuple(shape), dtype)


def _mm(a, b, *, name, ta=False, tb=False, outs=(f32,), epi=None, extras=(), tm=1024, tn=1024, tk=512):
    m, k = (a.shape[1], a.shape[0]) if ta else a.shape
    n = b.shape[0] if tb else b.shape[1]
    tm, tn, tk = min(tm, m), min(tn, n), min(tk, k)
    assert m % tm == 0 and n % tn == 0 and k % tk == 0, (name, m, n, k)
    nk = k // tk
    a_spec = pl.BlockSpec((tk, tm), lambda i, j, kk: (kk, i)) if ta else pl.BlockSpec((tm, tk), lambda i, j, kk: (i, kk))
    b_spec = pl.BlockSpec((tn, tk), lambda i, j, kk: (j, kk)) if tb else pl.BlockSpec((tk, tn), lambda i, j, kk: (kk, j))
    e_specs = [pl.BlockSpec((tm, tn), lambda i, j, kk: (i, j)) if e.shape[1] == n else
               pl.BlockSpec((tm, e.shape[1]), lambda i, j, kk: (i, 0)) for e in extras]
    dn = (((0 if ta else 1,), (1 if tb else 0,)), ((), ()))
    n_e, n_o = len(extras), len(outs)

    def body(*refs):
        a_ref, b_ref = refs[0], refs[1]
        e_refs = refs[2:2 + n_e]
        o_refs = refs[2 + n_e:2 + n_e + n_o]
        acc = refs[-1]
        kk = pl.program_id(2)

        @pl.when(kk == 0)
        def _():
            acc[...] = jnp.zeros_like(acc)

        acc[...] += lax.dot_general(a_ref[...].astype(bf16), b_ref[...].astype(bf16), dn, preferred_element_type=f32)

        @pl.when(kk == nk - 1)
        def _():
            r = acc[...]
            res = epi(r, *[e[...] for e in e_refs]) if epi is not None else (r,)
            for o, v in zip(o_refs, res):
                o[...] = v.astype(o.dtype)

    out = _pc(body, name=name, grid=(m // tm, n // tn, nk),
              in_specs=[a_spec, b_spec] + e_specs,
              out_specs=[pl.BlockSpec((tm, tn), lambda i, j, kk: (i, j)) for _ in outs],
              out_shape=[_sds((m, n), dt) for dt in outs],
              scratch=[pltpu.VMEM((tm, tn), f32)],
              sem=("parallel", "parallel", "arbitrary"))(a, b, *extras)
    return out[0] if n_o == 1 else out


def _rw(fn, rows, bcast, outs, accs=(), *, name, tr=256):
    t = rows[0].shape[0]
    tr = min(tr, t)
    assert t % tr == 0
    n_r, n_b, n_o, n_a = len(rows), len(bcast), len(outs), len(accs)

    def body(*refs):
        r_refs = refs[:n_r]
        b_refs = refs[n_r:n_r + n_b]
        o_refs = refs[n_r + n_b:n_r + n_b + n_o]
        a_refs = refs[n_r + n_b + n_o:]
        res = fn(*[r[...] for r in r_refs], *[b[...] for b in b_refs])
        for o, v in zip(o_refs, res[:n_o]):
            o[...] = v.astype(o.dtype)
        i = pl.program_id(0)
        for a_ref, v in zip(a_refs, res[n_o:]):
            @pl.when(i == 0)
            def _(a_ref=a_ref):
                a_ref[...] = jnp.zeros_like(a_ref)
            a_ref[...] += v

    in_specs = [pl.BlockSpec((tr, r.shape[1]), lambda i: (i, 0)) for r in rows]
    in_specs += [pl.BlockSpec(b.shape, lambda i: (0, 0)) for b in bcast]
    out_specs = [pl.BlockSpec((tr, w), lambda i: (i, 0)) for w, _ in outs]
    out_specs += [pl.BlockSpec(s, lambda i: (0, 0)) for s in accs]
    out_shape = [_sds((t, w), dt) for w, dt in outs] + [_sds(s, f32) for s in accs]
    res = _pc(body, name=name, grid=(t // tr,), in_specs=in_specs, out_specs=out_specs, out_shape=out_shape,
              sem=("arbitrary",))(*rows, *bcast)
    return res


def _rms(x, gain):
    return x * lax.rsqrt(jnp.mean(x * x, axis=-1, keepdims=True) + EPS) * gain


def _rms_bwd(x, gain, dy):
    _, vjp = jax.vjp(_rms, x, gain)
    return vjp(dy)


def _lower_bound(lbl):
    l0, l1 = lbl[0:1, :], lbl[1:2, :]
    mx = jnp.maximum(l0, l1)
    e0, e1 = jnp.exp(l0 - mx), jnp.exp(l1 - mx)
    return e0 / (e0 + e1)


def _gates(qpre, fpre, lbl):
    lb = _lower_bound(lbl)
    q = jax.nn.silu(qpre)
    forget = lb + (1.0 - lb) * jax.nn.sigmoid(fpre)
    return q, 1.0 - forget, jnp.log(forget)


def _head_norm_gate(o, gpre, gn):
    return _rms(o, gn) * jax.nn.silu(gpre)


def _swap_halves(x):
    w = x.shape[1]
    lane = lax.broadcasted_iota(jnp.int32, x.shape, 1)
    return jnp.where((lane % MLA_ROPE) < MLA_ROPE // 2, pltpu.roll(x, w - MLA_ROPE // 2, 1),
                     pltpu.roll(x, MLA_ROPE // 2, 1))


def _tile_lanes(tab, w):
    return tab if w == tab.shape[1] else jnp.concatenate([tab] * (w // tab.shape[1]), axis=1)


def _rope(x, cos, sgn_sin, sign=1.0):
    w = x.shape[1]
    return x * _tile_lanes(cos, w) + sign * _swap_halves(x) * _tile_lanes(sgn_sin, w)


def _bd(a, b, ca, cb):
    return lax.dot_general(a.astype(bf16), b.astype(bf16), (((ca,), (cb,)), ((), ())), preferred_element_type=f32)


@jax.custom_vjp
def _dot_nn(a, b):
    return _bd(a, b, 1, 0)


@jax.custom_vjp
def _dot_nt(a, b):
    return _bd(a, b, 1, 1)


@jax.custom_vjp
def _dot_tn(a, b):
    return _bd(a, b, 0, 0)


_dot_nn.defvjp(lambda a, b: (_bd(a, b, 1, 0), (a, b)), lambda r, g: (_bd(g, r[1], 1, 1), _bd(r[0], g, 0, 0)))
_dot_nt.defvjp(lambda a, b: (_bd(a, b, 1, 1), (a, b)), lambda r, g: (_bd(g, r[1], 1, 0), _bd(g, r[0], 0, 0)))
_dot_tn.defvjp(lambda a, b: (_bd(a, b, 0, 0), (a, b)), lambda r, g: (_bd(r[1], g, 1, 1), _bd(r[0], g, 1, 0)))


def _hdot(c, g):
    return jnp.dot(c, g, precision=HI, preferred_element_type=f32)


def _gla_consts():
    c, s = HGRN_CHUNK, HGRN_SUB
    row = lax.broadcasted_iota(jnp.int32, (c, c), 0)
    col = lax.broadcasted_iota(jnp.int32, (c, c), 1)
    incl = (col <= row).astype(f32)
    masks = []
    for i in range(c // s):
        n = s * (i + 1)
        mr = lax.broadcasted_iota(jnp.int32, (s, n), 0) + s * i
        mc = lax.broadcasted_iota(jnp.int32, (s, n), 1)
        masks.append(mc <= mr)
    return incl, masks


def _gla_chunk(consts, dots, q, k, v, g, st):
    incl, masks = consts
    dot_nn, dot_nt, dot_tn = dots
    c, s = HGRN_CHUNK, HGRN_SUB
    b = _hdot(incl, g)
    b_last = b[c - 1:c, :]
    o_inter = dot_nt(q * jnp.exp(b), st)
    st_new = st * jnp.exp(b_last) + dot_tn(v, k * jnp.exp(b_last - b))
    intra = []
    for i in range(c // s):
        n = s * (i + 1)
        ref = b[s * i - 1:s * i, :] if i else jnp.zeros_like(b_last)
        qt = q[s * i:n] * jnp.exp(b[s * i:n] - ref)
        kt = k[:n] * jnp.exp(jnp.minimum(ref - b[:n], EXP_CLAMP))
        sc = jnp.where(masks[i], dot_nt(qt, kt), 0.0)
        intra.append(dot_nn(sc, v[:n]))
    return o_inter + jnp.concatenate(intra, axis=0), st_new


_PLAIN_DOTS = (lambda a, b: _bd(a, b, 1, 0), lambda a, b: _bd(a, b, 1, 1), lambda a, b: _bd(a, b, 0, 0))
_VJP_DOTS = (_dot_nn, _dot_nt, _dot_tn)


def _head_cols(h):
    return slice(HGRN_DK * h, HGRN_DK * (h + 1))


def _gla_fwd(q, k, p4, g, gather=None):
    t = q.shape[0]
    nc = t // HGRN_CHUNK

    def body(q_ref, k_ref, v_ref, g_ref, *rest):
        if gather is None:
            o_ref, s_ref, st = rest
        else:
            wp_ref, _, o_ref, s_ref, gathered_ref, st, sems = rest

        @pl.when(pl.program_id(0) == 0)
        def _():
            st[...] = jnp.zeros_like(st)
            if gather is not None:
                _gather_start(wp_ref, gathered_ref, sems)

        if gather is not None:
            @pl.when(pl.program_id(0) == nc - 1)
            def _():
                _gather_finish(wp_ref, gathered_ref, sems)

        consts = _gla_consts()
        for h in range(HGRN_HEADS):
            cols = _head_cols(h)
            s_in = st[h]
            s_ref[0, h] = s_in
            o, st_new = _gla_chunk(consts, _PLAIN_DOTS, q_ref[:, cols], k_ref[:, cols], v_ref[:, cols], g_ref[:, cols], s_in)
            o_ref[:, cols] = o
            st[h] = st_new

    blk = lambda off: pl.BlockSpec((HGRN_CHUNK, D_MODEL), lambda c: (c, off))
    state_shape = (HGRN_HEADS, HGRN_DK, HGRN_DK)
    in_specs = [blk(0), blk(0), blk(2), blk(0)]
    out_specs = [blk(0), pl.BlockSpec((1,) + state_shape, lambda c: (c, 0, 0, 0))]
    out_shape = [_sds((t, D_MODEL), f32), _sds((nc,) + state_shape, f32)]
    scratch = [pltpu.VMEM(state_shape, f32)]
    if gather is None:
        return _pc(body, name="gla_fwd", grid=(nc,), in_specs=in_specs, out_specs=out_specs, out_shape=out_shape,
                   scratch=scratch, sem=("arbitrary",))(q, k, p4, g)
    return _pc(body, name="gla_fwd_gather", grid=(nc,), in_specs=in_specs + [HBM, HBM], out_specs=out_specs + [HBM],
               out_shape=out_shape + [_sds((4,) + gather.shape, bf16)], aliases={5: 2},
               scratch=scratch + [pltpu.SemaphoreType.DMA((2, 6))], sem=("arbitrary",))(
                   q, k, p4, g, gather, _gather_base(gather))


def _gla_bwd(q, k, p4, g, states, do, exchange=None):
    t = q.shape[0]
    nc = t // HGRN_CHUNK

    def body(q_ref, k_ref, v_ref, g_ref, s_ref, do_ref, *rest):
        if exchange is None:
            dq_ref, dk_ref, dv_ref, dg_ref, dst = rest
        else:
            sb_ref, dq_ref, dk_ref, dv_ref, dg_ref, recv_ref, dst, sems = rest

        @pl.when(pl.program_id(0) == 0)
        def _():
            dst[...] = jnp.zeros_like(dst)
            if exchange is not None:
                _chips_start(sb_ref, recv_ref, sems)

        if exchange is not None:
            @pl.when(pl.program_id(0) == nc - 1)
            def _():
                _chips_finish(sb_ref, recv_ref, sems)

        consts = _gla_consts()
        fn = lambda qq, kk, vv, gg, ss: _gla_chunk(consts, _VJP_DOTS, qq, kk, vv, gg, ss)
        for h in range(HGRN_HEADS):
            cols = _head_cols(h)
            _, vjp = jax.vjp(fn, q_ref[:, cols], k_ref[:, cols], v_ref[:, cols], g_ref[:, cols], s_ref[0, h])
            dq, dk, dv, dg, ds = vjp((do_ref[:, cols], dst[h]))
            dq_ref[:, cols] = dq
            dk_ref[:, cols] = dk
            dv_ref[:, cols] = dv
            dg_ref[:, cols] = dg
            dst[h] = ds

    blk = lambda off: pl.BlockSpec((HGRN_CHUNK, D_MODEL), lambda c: (nc - 1 - c, off))
    state_shape = (HGRN_HEADS, HGRN_DK, HGRN_DK)
    in_specs = [blk(0), blk(0), blk(2), blk(0), pl.BlockSpec((1,) + state_shape, lambda c: (nc - 1 - c, 0, 0, 0)), blk(0)]
    out_shape = [_sds((t, D_MODEL), f32)] * 4
    scratch = [pltpu.VMEM(state_shape, f32)]
    if exchange is None:
        return _pc(body, name="gla_bwd", grid=(nc,), in_specs=in_specs, out_specs=[blk(0)] * 4, out_shape=out_shape,
                   scratch=scratch, sem=("arbitrary",))(q, k, p4, g, states, do)
    return _pc(body, name="gla_bwd_exchange", grid=(nc,), in_specs=in_specs + [HBM], out_specs=[blk(0)] * 4 + [HBM],
               out_shape=out_shape + [_sds((3,) + exchange.shape[1:], bf16)],
               scratch=scratch + [pltpu.SemaphoreType.DMA((2, 3))], sem=("arbitrary",))(q, k, p4, g, states, do, exchange)


ATT_FWD_TQ, ATT_FWD_TK = 1024, 1024
ATT_BWD_TQ, ATT_BWD_TK = 1024, 512
ATT_QK = 2 * LANES
NEG = -1e30


def _pair_masks(shape):
    lane = lax.broadcasted_iota(jnp.int32, shape, 1)
    return lane < MLA_ROPE, lane >= MLA_ROPE


def _causal(shape, row0, col0):
    row = row0 + lax.broadcasted_iota(jnp.int32, shape, 0)
    col = col0 + lax.broadcasted_iota(jnp.int32, shape, 1)
    return col <= row


def _qk_cols(e):
    return slice(ATT_QK * e, ATT_QK * (e + 1))


def _v_cols(e):
    return slice(MLA_V * e, MLA_V * (e + 1))


def _attn_fwd(qc, kc, v):
    t = qc.shape[0]
    tq, tk = min(ATT_FWD_TQ, t), min(ATT_FWD_TK, t)
    npair = MLA_HEADS // 2

    def body(q_ref, k_ref, v_ref, o_ref, lse_ref):
        i = pl.program_id(1)
        n_full = (i * tq + 1) // tk
        nkv = (i * tq + tq + tk - 1) // tk
        q = [q_ref[:, _qk_cols(e)] for e in range(2)]

        def step(j, carry, masked):
            ks = pl.ds(pl.multiple_of(j * tk, tk), tk)
            ok = _causal((tq, tk), i * tq, j * tk) if masked else None
            new = []
            for e in range(2):
                m, l, acc = carry[e]
                s = _bd(q[e], k_ref[ks, _qk_cols(e)], 1, 1)
                if masked:
                    s = jnp.where(ok, s, NEG)
                m_new = jnp.maximum(m, jnp.max(s, axis=-1, keepdims=True))
                p = jnp.exp(s - m_new)
                alpha = jnp.exp(m - m_new)
                l = alpha * l + jnp.sum(p, axis=-1, keepdims=True)
                acc = alpha * acc + _bd(p, v_ref[ks, _v_cols(e)], 1, 0)
                new.append((m_new, l, acc))
            return tuple(new)

        one = (jnp.full((tq, 1), NEG, f32), jnp.zeros((tq, 1), f32), jnp.zeros((tq, MLA_V), f32))
        carry = lax.fori_loop(0, n_full, functools.partial(step, masked=False), (one, one))
        carry = lax.fori_loop(n_full, nkv, functools.partial(step, masked=True), carry)
        o_ref[...] = jnp.concatenate([acc / l for _, l, acc in carry], axis=1).astype(o_ref.dtype)
        lo, _ = _pair_masks((tq, LANES))
        lse_ref[...] = jnp.where(lo, *[m + jnp.log(l) for m, l, _ in carry])

    return _pc(body, name="attn_fwd", grid=(npair, t // tq),
               in_specs=[pl.BlockSpec((tq, 2 * ATT_QK), lambda p, i: (i, p)),
                         pl.BlockSpec((t, 2 * ATT_QK), lambda p, i: (0, p)),
                         pl.BlockSpec((t, 2 * MLA_V), lambda p, i: (0, p))],
               out_specs=[pl.BlockSpec((tq, 2 * MLA_V), lambda p, i: (i, p)),
                          pl.BlockSpec((tq, LANES), lambda p, i: (i, p))],
               out_shape=[_sds((t, MLA_HEADS * MLA_V), bf16), _sds((t, npair * LANES), f32)],
               sem=("arbitrary", "arbitrary"))(qc, kc, v)


def _attn_bwd(qc, kc, v, do, lse, delta):
    t = qc.shape[0]
    tq, tk = min(ATT_BWD_TQ, t), min(ATT_BWD_TK, t)
    npair = MLA_HEADS // 2
    nq = t // tq

    def body(q_ref, do_ref, lse_ref, dl_ref, k_ref, v_ref, dq_ref, dk_ref, dv_ref):
        j = pl.program_id(1)

        @pl.when(j == 0)
        def _():
            dq_ref[...] = jnp.zeros_like(dq_ref)

        k = [k_ref[:, _qk_cols(e)] for e in range(2)]
        vv = [v_ref[:, _v_cols(e)] for e in range(2)]

        def step(i, carry, masked):
            qs = pl.ds(pl.multiple_of(i * tq, tq), tq)
            ok = _causal((tq, tk), i * tq, j * tk) if masked else None
            lse2, dl2 = lse_ref[qs, :], dl_ref[qs, :]
            new = []
            for e in range(2):
                dk, dv = carry[e]
                q_e, do_e = q_ref[qs, _qk_cols(e)], do_ref[qs, _v_cols(e)]
                p = jnp.exp(_bd(q_e, k[e], 1, 1) - lse2[:, MLA_ROPE * e:MLA_ROPE * e + 1])
                if masked:
                    p = jnp.where(ok, p, 0.0)
                dv = dv + _bd(p, do_e, 0, 0)
                dp = _bd(do_e, vv[e], 1, 1)
                ds = (p * (dp - dl2[:, MLA_ROPE * e:MLA_ROPE * e + 1])).astype(bf16)
                dk = dk + _bd(ds, q_e, 0, 0)
                dq_ref[qs, _qk_cols(e)] += _bd(ds, k[e], 1, 0)
                new.append((dk, dv))
            return tuple(new)

        one = (jnp.zeros((tk, ATT_QK), f32), jnp.zeros((tk, MLA_V), f32))
        i_full = jnp.minimum((j * tk + tk + tq - 2) // tq, nq)
        carry = lax.fori_loop((j * tk) // tq, i_full, functools.partial(step, masked=True), (one, one))
        carry = lax.fori_loop(i_full, nq, functools.partial(step, masked=False), carry)
        for e in range(2):
            dk_ref[:, _qk_cols(e)] = carry[e][0].astype(dk_ref.dtype)
            dv_ref[:, _v_cols(e)] = carry[e][1].astype(dv_ref.dtype)

    res = lambda w: pl.BlockSpec((t, w), lambda p, j: (0, p))
    blk = lambda w: pl.BlockSpec((tk, w), lambda p, j: (j, p))
    return _pc(body, name="attn_bwd", grid=(npair, t // tk),
               in_specs=[res(2 * ATT_QK), res(2 * MLA_V), res(LANES), res(LANES), blk(2 * ATT_QK), blk(2 * MLA_V)],
               out_specs=[res(2 * ATT_QK), blk(2 * ATT_QK), blk(2 * MLA_V)],
               out_shape=[_sds((t, MLA_HEADS * ATT_QK), f32), _sds((t, MLA_HEADS * ATT_QK), bf16),
                          _sds((t, MLA_HEADS * MLA_V), bf16)],
               sem=("arbitrary", "arbitrary"))(qc, do, lse, delta, kc, v)


def _rope_tables(t):
    half = MLA_ROPE // 2
    inv_freq = ROPE_THETA ** (-jnp.arange(half, dtype=f32) / half)
    ang = jnp.arange(t, dtype=f32)[:, None] * inv_freq[None, :]
    cos, sin = jnp.cos(ang), jnp.sin(ang)
    cos128, sin128 = jnp.concatenate([cos, cos] * 2, axis=1), jnp.concatenate([-sin, sin] * 2, axis=1)
    one, zero = jnp.ones((t, MLA_NOPE), f32), jnp.zeros((t, MLA_NOPE), f32)
    return cos128, sin128, jnp.concatenate([one, cos128], axis=1), jnp.concatenate([zero, sin128], axis=1)


def _relu2_epi(u):
    r = jnp.maximum(u, 0.0)
    return u, r * r


def _add_epi(r, res):
    return (r + res,)


def _drelu2_epi(da, u):
    return (da * 2.0 * jnp.maximum(u.astype(f32), 0.0),)


def _mlp_fwd(h, gain, w_up, w_down, tag):
    xm = _rw(lambda x, g: (_rms(x, g),), [h], [gain], [(D_MODEL, bf16)], name=f"mlp{tag}_norm")[0]
    u, a = _mm(xm, w_up, name=f"mlp{tag}_up", outs=(bf16, bf16), epi=_relu2_epi)
    h_out = _mm(a, w_down, name=f"mlp{tag}_down", epi=_add_epi, extras=(h,))
    return h_out, (xm, u, a)


def _mlp_bwd(dh, h, gain, w_up, w_down, saved, tag):
    xm, u, a = saved
    du = _mm(dh, w_down, tb=True, name=f"mlp{tag}_dact", outs=(bf16,), epi=_drelu2_epi, extras=(u,))
    d_down = _mm(a, dh, ta=True, name=f"mlp{tag}_dwdown")
    d_up = _mm(xm, du, ta=True, name=f"mlp{tag}_dwup")
    dxm = _mm(du, w_up, tb=True, name=f"mlp{tag}_dxm")

    def fn(x, dy, dres, g):
        dx, dg = _rms_bwd(x, g, dy)
        return dx + dres, dg

    dh_in, d_gain = _rw(fn, [h, dxm, dh], [gain], [(D_MODEL, f32)], [(1, D_MODEL)], name=f"mlp{tag}_dnorm")
    return dh_in, d_gain, d_up, d_down


def _local_step(x, target, w, comm=None):
    w = dict(w)
    t = x.shape[0]
    cos, sgn_sin, cos_qk, sin_qk = _rope_tables(t)
    grads = {}

    xn0 = _rw(lambda xx, g: (_rms(xx, g),), [x], [w["hgrn_norm"]], [(D_MODEL, bf16)], name="hgrn_norm")[0]
    p4 = _mm(xn0, w["hgrn_w4"], name="hgrn_proj")

    def gates_fn(p, lbl):
        return _gates(p[:, :D_MODEL], p[:, D_MODEL:2 * D_MODEL], lbl)

    q, k, g = _rw(gates_fn, [p4], [w["hgrn_lb_logits"]], [(D_MODEL, f32)] * 3, name="hgrn_gates")
    if comm is None:
        o, states = _gla_fwd(q, k, p4, g)
    else:
        o, states, gathered = _gla_fwd(q, k, p4, g, gather=comm.late_shard)
        w.update(comm.unpack_late(gathered))

    def hn_fn(oo, p, gn):
        ys = [_head_norm_gate(oo[:, LANES * h:LANES * (h + 1)], p[:, 3 * D_MODEL + LANES * h:3 * D_MODEL + LANES * (h + 1)], gn)
              for h in range(HGRN_HEADS)]
        return (jnp.concatenate(ys, axis=1),)

    y = _rw(hn_fn, [o, p4], [w["hgrn_g_norm"]], [(D_MODEL, bf16)], name="hgrn_headnorm")[0]
    h1 = _mm(y, w["hgrn_w_o"], name="hgrn_out", epi=_add_epi, extras=(x,))
    h2, mlp0 = _mlp_fwd(h1, w["mlp_norm"][0:1], w["mlp_w_up", 0], w["mlp_w_down", 0], 0)

    hk, xn1 = _rw(lambda hh, g1, g2: (_rms(hh, g1), _rms(hh, g2)), [h2], [w["kv_in_norm"], w["mla_norm"]],
                  [(D_MODEL, bf16)] * 2, name="kv_mla_norm")
    ckr = _mm(hk, w["kv_w_dkv"], name="kv_down")

    def ckv_fn(c, cs, sn, g):
        kr = _rope(c[:, MLA_KV_LORA:], cs, sn)
        return _rms(c[:, :MLA_KV_LORA], g), jnp.concatenate([jnp.zeros_like(kr), kr], axis=1)

    c_kv, kr_head = _rw(ckv_fn, [ckr, cos, sgn_sin], [w["kv_norm"]], [(MLA_KV_LORA, bf16), (ATT_QK, f32)],
                        name="kv_norm_rope")
    kc = _mm(c_kv, w["kv_w_kcat"], name="kv_up_k", outs=(bf16,), extras=(kr_head,),
             epi=lambda r, kr: (r + _tile_lanes(kr, r.shape[1]),))
    v_att = _mm(c_kv, w["kv_w_uv"], name="kv_up_v", outs=(bf16,))
    cq0 = _mm(xn1, w["mla_w_dq"], name="q_down")
    c_q = _rw(lambda c, g: (_rms(c, g),), [cq0], [w["mla_q_norm"]], [(MLA_Q_LORA, bf16)], name="q_norm")[0]
    qc = _mm(c_q, w["mla_w_qcat"], name="q_up", outs=(bf16,), extras=(cos_qk, sin_qk),
             epi=lambda r, cs, sn: (_rope(r, cs, sn) * ATT_SCALE,))
    o_att, lse = _attn_fwd(qc, kc, v_att)
    h3 = _mm(o_att, w["mla_w_o"], name="mla_out", epi=_add_epi, extras=(h2,))
    h4, mlp1 = _mlp_fwd(h3, w["mlp_norm"][1:2], w["mlp_w_up", 1], w["mlp_w_down", 1], 1)

    def loss_fn(hh, tgt, gain):
        def f(a, b):
            e = _rms(a, b) - tgt
            return 0.5 * jnp.sum(jnp.sum(e * e, axis=-1, keepdims=True) / D_MODEL, axis=0, keepdims=True)
        val, vjp = jax.vjp(f, hh, gain)
        dh, dg = vjp(jnp.ones((1, 1), f32))
        return dh, jnp.broadcast_to(val, (1, LANES)), dg

    dh4, loss_acc, grads["final_norm"] = _rw(loss_fn, [h4, target], [w["final_norm"]], [(D_MODEL, f32)],
                                             [(1, LANES), (1, D_MODEL)], name="loss")
    loss = loss_acc[0, 0]

    dh3, g_n1, g_up1, g_dn1 = _mlp_bwd(dh4, h3, w["mlp_norm"][1:2], w["mlp_w_up", 1], w["mlp_w_down", 1], mlp1, 1)
    do_att = _mm(dh3, w["mla_w_o"], tb=True, name="mla_dout", outs=(bf16,))
    grads["mla_w_o"] = _mm(o_att, dh3, ta=True, name="mla_dwo")

    def delta_fn(a, b):
        prod = a.astype(f32) * b.astype(f32)
        outs = []
        for p in range(MLA_HEADS // 2):
            d0 = jnp.sum(prod[:, 2 * p * LANES:(2 * p + 1) * LANES], axis=-1, keepdims=True)
            d1 = jnp.sum(prod[:, (2 * p + 1) * LANES:(2 * p + 2) * LANES], axis=-1, keepdims=True)
            lo, _ = _pair_masks((a.shape[0], LANES))
            outs.append(jnp.where(lo, d0, d1))
        return (jnp.concatenate(outs, axis=1),)

    delta = _rw(delta_fn, [do_att, o_att], [], [(MLA_HEADS // 2 * LANES, f32)], name="attn_delta")[0]
    dqc, dkc, dv = _attn_bwd(qc, kc, v_att, do_att, lse, delta)
    dqf = _rw(lambda a, cs, sn: (_rope(a, cs, sn, -1.0) * ATT_SCALE,), [dqc, cos_qk, sin_qk], [],
              [(MLA_HEADS * ATT_QK, bf16)], name="dq_rope")[0]
    dc_q = _mm(dqf, w["mla_w_qcat"], tb=True, name="q_up_dx")
    grads["mla_w_qcat"] = _mm(c_q, dqf, ta=True, name="q_up_dw")

    def dqn_fn(c, dy, g):
        return _rms_bwd(c, g, dy)

    dcq0, grads["mla_q_norm"] = _rw(dqn_fn, [cq0, dc_q], [w["mla_q_norm"]], [(MLA_Q_LORA, bf16)], [(1, MLA_Q_LORA)],
                                    name="q_dnorm")
    dxn1 = _mm(dcq0, w["mla_w_dq"], tb=True, name="q_down_dx")
    grads["mla_w_dq"] = _mm(xn1, dcq0, ta=True, name="q_down_dw")

    dc_kv = _mm(dkc, w["kv_w_kcat"], tb=True, name="kv_up_dx_k")
    dc_kv = _mm(dv, w["kv_w_uv"], tb=True, name="kv_up_dx_v", epi=_add_epi, extras=(dc_kv,))
    grads["kv_w_kcat"] = _mm(c_kv, dkc, ta=True, name="kv_up_dw_k")
    grads["kv_w_uv"] = _mm(c_kv, dv, ta=True, name="kv_up_dw_v")

    def dckr_fn(c, dc, dk_heads, cs, sn, g):
        tot = dk_heads[:, LANES:ATT_QK].astype(f32)
        for h in range(1, MLA_HEADS):
            tot = tot + dk_heads[:, ATT_QK * h + LANES:ATT_QK * (h + 1)].astype(f32)
        lo, _ = _pair_masks(tot.shape)
        dkr = jnp.where(lo, _rope(tot, cs, sn, -1.0), 0.0)
        dcc, dg = _rms_bwd(c[:, :MLA_KV_LORA], g, dc)
        return jnp.concatenate([dcc, dkr], axis=1), dg

    dckr, grads["kv_norm"] = _rw(dckr_fn, [ckr, dc_kv, dkc, cos, sgn_sin], [w["kv_norm"]],
                                 [(MLA_KV_LORA + LANES, bf16)], [(1, MLA_KV_LORA)], name="kv_dnorm_rope")
    dhk = _mm(dckr, w["kv_w_dkv"], tb=True, name="kv_down_dx")
    grads["kv_w_dkv"] = _mm(hk, dckr, ta=True, name="kv_down_dw")

    def dh2_fn(hh, d1, d2, dres, g1, g2):
        a, ga = _rms_bwd(hh, g1, d1)
        b, gb = _rms_bwd(hh, g2, d2)
        return a + b + dres, ga, gb

    dh2, grads["kv_in_norm"], grads["mla_norm"] = _rw(dh2_fn, [h2, dhk, dxn1, dh3], [w["kv_in_norm"], w["mla_norm"]],
                                                      [(D_MODEL, f32)], [(1, D_MODEL)] * 2, name="kv_mla_dnorm")

    dh1, g_n0, g_up0, g_dn0 = _mlp_bwd(dh2, h1, w["mlp_norm"][0:1], w["mlp_w_up", 0], w["mlp_w_down", 0], mlp0, 0)
    grads["mlp_norm"] = jnp.concatenate([g_n0, g_n1], axis=0)
    grads["mlp_w_up"] = (g_up0, g_up1)
    grads["mlp_w_down"] = (g_dn0, g_dn1)
    dy = _mm(dh1, w["hgrn_w_o"], tb=True, name="hgrn_dout")
    grads["hgrn_w_o"] = _mm(y, dh1, ta=True, name="hgrn_dwo")

    def dhn_fn(oo, p, dyy, gn):
        dos, dgs, dgn = [], [], jnp.zeros_like(gn)
        for h in range(HGRN_HEADS):
            cols = slice(LANES * h, LANES * (h + 1))
            _, vjp = jax.vjp(_head_norm_gate, oo[:, cols], p[:, 3 * D_MODEL + LANES * h:3 * D_MODEL + LANES * (h + 1)], gn)
            a, b, c = vjp(dyy[:, cols])
            dos.append(a)
            dgs.append(b)
            dgn = dgn + c
        return jnp.concatenate(dos, axis=1), jnp.concatenate(dgs, axis=1), dgn

    do, dgate, grads["hgrn_g_norm"] = _rw(dhn_fn, [o, p4, dy], [w["hgrn_g_norm"]], [(D_MODEL, f32)] * 2, [(1, HGRN_DK)],
                                          name="hgrn_dheadnorm")
    if comm is None:
        dq, dk, dv_h, dg = _gla_bwd(q, k, p4, g, states, do)
    else:
        dq, dk, dv_h, dg, comm.received_early = _gla_bwd(q, k, p4, g, states, do, exchange=comm.reduce_early(grads))

    def dgates_fn(p, dqq, dkk, dgg, dvv, dgt, lbl):
        _, vjp = jax.vjp(_gates, p[:, :D_MODEL], p[:, D_MODEL:2 * D_MODEL], lbl)
        dqp, dfp, dlbl = vjp((dqq, dkk, dgg))
        return jnp.concatenate([dqp, dfp, dvv, dgt], axis=1), dlbl

    dp4, grads["hgrn_lb_logits"] = _rw(dgates_fn, [p4, dq, dk, dg, dv_h, dgate], [w["hgrn_lb_logits"]],
                                       [(4 * D_MODEL, bf16)], [(2, D_MODEL)], name="hgrn_dgates")
    dxn0 = _mm(dp4, w["hgrn_w4"], tb=True, name="hgrn_proj_dx")
    grads["hgrn_w4"] = _mm(xn0, dp4, ta=True, name="hgrn_proj_dw")

    def dx_fn(xx, dyy, dres, gn):
        dxx, dgn = _rms_bwd(xx, gn, dyy)
        return dxx + dres, dgn

    grad_x, grads["hgrn_norm"] = _rw(dx_fn, [x, dxn0, dh1], [w["hgrn_norm"]], [(D_MODEL, f32)], [(1, D_MODEL)],
                                     name="hgrn_dnorm")
    return loss, grad_x, grads


HBM = pl.BlockSpec(memory_space=pltpu.HBM)


def _me():
    return lax.axis_index("x"), lax.axis_index("y"), lax.axis_index("c")


def _flip(x, y, f):
    return (1 - x if f & 1 else x), (1 - y if f & 2 else y)


def _rcopy(src, dst, sems, k, dev):
    return pltpu.make_async_remote_copy(src_ref=src, dst_ref=dst, send_sem=sems.at[0, k], recv_sem=sems.at[1, k],
                                        device_id=dev, device_id_type=MESH)


def _my_half(rows, c, mine=True):
    half = rows // 2
    return pl.ds(pl.multiple_of((c if mine else 1 - c) * half, 16), half)


def _gather_start(wp_ref, out_ref, sems):
    x, y, c = _me()
    half = _my_half(wp_ref.shape[0], c)
    for f in (1, 2, 3):
        px, py = _flip(x, y, f)
        _rcopy(wp_ref.at[half], out_ref.at[2 * x + y, half], sems, f - 1, (px, py, c)).start()


def _gather_finish(wp_ref, out_ref, sems):
    x, y, c = _me()
    half, other = _my_half(wp_ref.shape[0], c), _my_half(wp_ref.shape[0], c, mine=False)
    sends = []
    for f in (1, 2, 3):
        px, py = _flip(x, y, f)
        landed = out_ref.at[2 * px + py, half]
        _rcopy(landed, landed, sems, f - 1, (px, py, c)).wait_recv()
        sends.append(_rcopy(landed, landed, sems, 2 + f, (x, y, 1 - c)))
        sends[-1].start()
    for f in (1, 2, 3):
        px, py = _flip(x, y, f)
        theirs = out_ref.at[2 * px + py, other]
        _rcopy(theirs, theirs, sems, 2 + f, (x, y, 1 - c)).wait_recv()
        sends.append(_rcopy(wp_ref.at[half], out_ref.at[2 * x + y, half], sems, f - 1, (px, py, c)))
    for cp in sends:
        cp.wait_send()


def _gather_base(wp):
    return jnp.broadcast_to(wp[None], (4,) + wp.shape)


def _all_gather_weights(wp, sv):
    def body(wp_ref, sv_ref, base_ref, out_ref, svs_ref, sems, local_sem):
        x, y, c = _me()
        mine = pltpu.make_async_copy(sv_ref, svs_ref.at[2 * x + y], local_sem)
        mine.start()
        _gather_start(wp_ref, out_ref, sems)
        small = []
        for f in (1, 2, 3):
            px, py = _flip(x, y, f)
            small.append(_rcopy(sv_ref, svs_ref.at[2 * x + y], sems, 5 + f, (px, py, c)))
            small[-1].start()
        _gather_finish(wp_ref, out_ref, sems)
        for f in (1, 2, 3):
            px, py = _flip(x, y, f)
            _rcopy(sv_ref, svs_ref.at[2 * px + py], sems, 5 + f, (px, py, c)).wait_recv()
        for cp in small:
            cp.wait_send()
        mine.wait()

    return _pc(body, name="weights_all_gather", in_specs=[HBM, HBM, HBM], out_specs=[HBM, HBM],
               out_shape=[_sds((4,) + wp.shape, bf16), _sds((4, 8, 256), f32)], aliases={2: 0},
               scratch=[pltpu.SemaphoreType.DMA((2, 9)), pltpu.SemaphoreType.DMA])(wp, sv, _gather_base(wp))


def _send_half_to_sibling(gp, name):
    rows = gp.shape[1]

    def body(gp_ref, out_ref, sems):
        x, y, c = _me()
        cp = _rcopy(gp_ref.at[:, _my_half(rows, c, mine=False)], out_ref, sems, 0, (x, y, 1 - c))
        cp.start()
        cp.wait()

    return _pc(body, name=name, in_specs=[HBM], out_specs=HBM, out_shape=_sds((4, rows // 2, D_MODEL), f32),
               scratch=[pltpu.SemaphoreType.DMA((2, 1))])(gp)


def _chips_start(sb_ref, out_ref, sems):
    x, y, c = _me()
    for f in (1, 2, 3):
        px, py = _flip(x, y, f)
        _rcopy(sb_ref.at[2 * px + py], out_ref.at[f - 1], sems, f - 1, (px, py, c)).start()


def _chips_finish(sb_ref, out_ref, sems):
    x, y, c = _me()
    for f in (1, 2, 3):
        _rcopy(sb_ref.at[0], out_ref.at[f - 1], sems, f - 1, (x, y, c)).wait_recv()
    for f in (1, 2, 3):
        px, py = _flip(x, y, f)
        _rcopy(sb_ref.at[2 * px + py], out_ref.at[f - 1], sems, f - 1, (px, py, c)).wait_send()


def _exchange_chips(sb, small):
    def body(sb_ref, small_ref, out_ref, smalls_ref, sems, local_sem):
        x, y, c = _me()
        me = 4 * x + 2 * y + c
        mine = pltpu.make_async_copy(small_ref, smalls_ref.at[me], local_sem)
        mine.start()
        _chips_start(sb_ref, out_ref, sems)
        sends = []
        for f in range(1, 8):
            px, py = _flip(x, y, f)
            pc = 1 - c if f & 4 else c
            sends.append(_rcopy(small_ref, smalls_ref.at[me], sems, 2 + f, (px, py, pc)))
            sends[-1].start()
        _chips_finish(sb_ref, out_ref, sems)
        for f in range(1, 8):
            px, py = _flip(x, y, f)
            pc = 1 - c if f & 4 else c
            _rcopy(small_ref, smalls_ref.at[4 * px + 2 * py + pc], sems, 2 + f, (x, y, c)).wait_recv()
        for cp in sends:
            cp.wait_send()
        mine.wait()

    return _pc(body, name="grads_exchange_chips", in_specs=[HBM, HBM], out_specs=[HBM, HBM],
               out_shape=[_sds((3,) + sb.shape[1:], bf16), _sds((8, SMALL_ROWS, D_MODEL), f32)],
               scratch=[pltpu.SemaphoreType.DMA((2, 10)), pltpu.SemaphoreType.DMA])(sb, small)


def _exchange_halves(tot, name):
    rows = tot.shape[0]

    def body(tot_ref, out_ref, sems):
        x, y, c = _me()
        half = _my_half(rows, c)
        cp = _rcopy(tot_ref.at[half], out_ref.at[half], sems, 0, (x, y, 1 - c))
        cp.start()
        cp.wait()

    return _pc(body, name=name, in_specs=[HBM], out_specs=HBM, out_shape=_sds((rows, D_MODEL), f32),
               aliases={0: 0}, scratch=[pltpu.SemaphoreType.DMA((2, 1))])(tot)


def _sum_rows(half):
    return max(r for r in range(16, 513, 16) if half % r == 0)


def _sum_over_cores(gp, recv, cq, name):
    half = recv.shape[1]
    tr = _sum_rows(half)
    nb = half // tr

    def body(cq_ref, g_ref, r_ref, o32_ref, o16_ref):
        s = g_ref[...] + r_ref[...]
        o32_ref[...] = s
        o16_ref[...] = s.astype(bf16)

    spec = pl.BlockSpec((1, tr, D_MODEL), lambda b, i, cq_ref: (b, i, 0))
    gs = pltpu.PrefetchScalarGridSpec(
        num_scalar_prefetch=1, grid=(4, nb),
        in_specs=[pl.BlockSpec((1, tr, D_MODEL), lambda b, i, cq_ref: (b, cq_ref[0] * nb + i, 0)), spec],
        out_specs=[spec, spec])
    return _pc(body, name=name, grid_spec=gs, sem=("arbitrary", "arbitrary"),
               out_shape=[_sds((4, half, D_MODEL), f32), _sds((4, half, D_MODEL), bf16)])(cq, gp, recv)


def _sum_over_chips(s32, recv, cq, name):
    half = recv.shape[1]
    tr = _sum_rows(half)
    nb = half // tr

    def body(cq_ref, own_ref, r_ref, o_ref):
        o_ref[...] = ((own_ref[0] + r_ref[0].astype(f32)) + r_ref[1].astype(f32)) + r_ref[2].astype(f32)

    gs = pltpu.PrefetchScalarGridSpec(
        num_scalar_prefetch=1, grid=(nb,),
        in_specs=[pl.BlockSpec((1, tr, D_MODEL), lambda i, cq_ref: (cq_ref[1], i, 0)),
                  pl.BlockSpec((3, tr, D_MODEL), lambda i, cq_ref: (0, i, 0))],
        out_specs=pl.BlockSpec((tr, D_MODEL), lambda i, cq_ref: (cq_ref[0] * nb + i, 0)))
    return _pc(body, name=name, grid_spec=gs, sem=("arbitrary",),
               out_shape=_sds((2 * half, D_MODEL), f32))(cq, s32, recv)


def _sum_small(smalls):
    def body(s_ref, o_ref):
        tot = s_ref[0]
        for d in range(1, 8):
            tot = tot + s_ref[d]
        o_ref[...] = tot

    return _pc(body, name="small_sum", out_shape=_sds((SMALL_ROWS, D_MODEL), f32))(smalls)


def _adamw_math(w, g, m, v):
    m = ADAM_B1 * m + (1.0 - ADAM_B1) * g
    v = ADAM_B2 * v + (1.0 - ADAM_B2) * jnp.square(g)
    m_hat = m / (1.0 - ADAM_B1 ** ADAM_STEP)
    v_hat = v / (1.0 - ADAM_B2 ** ADAM_STEP)
    delta = -ADAM_LR * (m_hat / (jnp.sqrt(v_hat) + ADAM_EPS) + ADAM_WD * w)
    return delta, m, v


def _adamw(w, g, m, v, name):
    cols = w.shape[1]
    return _rw(_adamw_math, [w, g, m, v], [], [(cols, f32)] * 3, name=name, tr=256)


def _adamw_small(items):
    n = len(items)

    def body(*refs):
        ins, outs = refs[:4 * n], refs[4 * n:]
        for i in range(n):
            res = _adamw_math(*[r[...] for r in ins[4 * i:4 * i + 4]])
            for o, val in zip(outs[3 * i:3 * i + 3], res):
                o[...] = val

    flat = [a for it in items for a in it]
    out_shape = [_sds(it[0].shape, f32) for it in items for _ in range(3)]
    res = _pc(body, name="adamw_small", out_shape=out_shape)(*flat)
    return [tuple(res[3 * i:3 * i + 3]) for i in range(n)]


def _pack_shards(sh, layout, pad):
    parts = [(sh[n] if layer is None else sh[n][layer]).reshape(-1, D_MODEL).astype(bf16) for n, layer, _ in layout]
    if pad:
        parts.append(jnp.zeros((pad, D_MODEL), bf16))
    return jnp.concatenate(parts, axis=0)


def _mlp_full(g4, off, layer):
    o, r = off["mlp_w_up", layer]
    up = g4[:, o:o + r].transpose(1, 0, 2).reshape(D_MODEL, D_FF)
    o, r = off["mlp_w_down", layer]
    return {("mlp_w_up", layer): up, ("mlp_w_down", layer): g4[:, o:o + r].reshape(D_FF, D_MODEL)}


def _unpack_early(g4):
    w = _mlp_full(g4, W_EARLY_OFF, 0)
    hg = g4[:, 0:1024].reshape(4, 4, 256, D_MODEL)
    w["hgrn_w4"] = hg.transpose(0, 2, 1, 3).reshape(D_MODEL, 4 * D_MODEL)
    o, r = W_EARLY_OFF["hgrn_w_o", None]
    w["hgrn_w_o"] = g4[:, o:o + r].reshape(D_MODEL, D_MODEL)
    return w


def _unpack_late(g4):
    def rows(name):
        o, r = W_LATE_OFF[name, None]
        return g4[:, o:o + r]

    w = _mlp_full(g4, W_LATE_OFF, 1)
    w["mla_w_dq"] = rows("mla_w_dq").reshape(D_MODEL, MLA_Q_LORA)
    uq = rows("mla_w_uq").reshape(4, MLA_Q_LORA, 768).transpose(1, 0, 2).reshape(MLA_Q_LORA, MLA_HEADS, MLA_NOPE + MLA_ROPE)
    w["mla_w_qcat"] = jnp.pad(uq, ((0, 0), (0, 0), (0, ATT_QK - MLA_NOPE - MLA_ROPE))).reshape(MLA_Q_LORA, MLA_HEADS * ATT_QK)
    w["mla_w_o"] = rows("mla_w_o").reshape(MLA_HEADS * MLA_V, D_MODEL)
    dkv = rows("kv_w_dkv").reshape(D_MODEL, MLA_KV_LORA + MLA_ROPE)
    w["kv_w_dkv"] = jnp.pad(dkv, ((0, 0), (0, LANES - MLA_ROPE)))
    uk = rows("kv_w_uk").reshape(4, MLA_KV_LORA, 512).transpose(1, 0, 2).reshape(MLA_KV_LORA, MLA_HEADS, MLA_NOPE)
    w["kv_w_kcat"] = jnp.pad(uk, ((0, 0), (0, 0), (0, ATT_QK - MLA_NOPE))).reshape(MLA_KV_LORA, MLA_HEADS * ATT_QK)
    w["kv_w_uv"] = rows("kv_w_uv").reshape(4, MLA_KV_LORA, 512).transpose(1, 0, 2).reshape(MLA_KV_LORA, MLA_HEADS * MLA_V)
    return w


def _pack_grads_late(g):
    return g["hgrn_w4"].reshape(4, 256, 4, D_MODEL).transpose(0, 2, 1, 3).reshape(4, G_LATE_ROWS, D_MODEL)


def _pack_grads_early(g):
    parts = [g["hgrn_w_o"].reshape(4, 256, D_MODEL),
             g["mla_w_dq"].reshape(4, 64, D_MODEL)]
    uq = g["mla_w_qcat"].reshape(MLA_Q_LORA, MLA_HEADS, ATT_QK)[:, :, :MLA_NOPE + MLA_ROPE]
    parts.append(uq.reshape(MLA_Q_LORA, 4, 768).transpose(1, 0, 2).reshape(4, 192, D_MODEL))
    parts.append(g["mla_w_o"].reshape(4, 512, D_MODEL))
    parts.append(g["kv_w_dkv"][:, :MLA_KV_LORA + MLA_ROPE].reshape(4, 80, D_MODEL))
    uk = g["kv_w_kcat"].reshape(MLA_KV_LORA, MLA_HEADS, ATT_QK)[:, :, :MLA_NOPE]
    parts.append(uk.reshape(MLA_KV_LORA, 4, 512).transpose(1, 0, 2).reshape(4, 128, D_MODEL))
    parts.append(g["kv_w_uv"].reshape(MLA_KV_LORA, 4, 512).transpose(1, 0, 2).reshape(4, 128, D_MODEL))
    up = jnp.stack(g["mlp_w_up"])
    parts.append(up.reshape(2, D_MODEL, 4, 1024).transpose(2, 0, 1, 3).reshape(4, 2048, D_MODEL))
    dn = jnp.stack(g["mlp_w_down"])
    parts.append(dn.reshape(2, 4, 1024, D_MODEL).transpose(1, 0, 2, 3).reshape(4, 2048, D_MODEL))
    parts.append(jnp.zeros((4, PACK_PAD, D_MODEL), f32))
    return jnp.concatenate(parts, axis=1)


LOSS_ROW = 11


def _pack_small(g, loss):
    rows = []
    for name, _, r, wd in SMALL:
        a = g[name].reshape(r, wd)
        rows.append(jnp.pad(a, ((0, 0), (0, D_MODEL - wd))) if wd < D_MODEL else a)
    assert sum(r for _, _, r, _ in SMALL) == LOSS_ROW
    rows.append(jnp.full((1, D_MODEL), loss, f32))
    rows.append(jnp.zeros((SMALL_ROWS - LOSS_ROW - 1, D_MODEL), f32))
    return jnp.concatenate(rows, axis=0)


def kernel(x, hgrn_norm, hgrn_w_q, hgrn_w_f, hgrn_w_i, hgrn_w_g, hgrn_g_norm, hgrn_w_o, hgrn_lb_logits, mla_norm, mla_w_dq, mla_q_norm, mla_w_uq, mla_w_o, kv_in_norm, kv_w_dkv, kv_norm, kv_w_uk, kv_w_uv, mlp_norm, mlp_w_up, mlp_w_down, final_norm, loss_target, m_hgrn_norm, m_hgrn_w_q, m_hgrn_w_f, m_hgrn_w_i, m_hgrn_w_g, m_hgrn_g_norm, m_hgrn_w_o, m_hgrn_lb_logits, m_mla_norm, m_mla_w_dq, m_mla_q_norm, m_mla_w_uq, m_mla_w_o, m_kv_in_norm, m_kv_w_dkv, m_kv_norm, m_kv_w_uk, m_kv_w_uv, m_mlp_norm, m_mlp_w_up, m_mlp_w_down, m_final_norm, v_hgrn_norm, v_hgrn_w_q, v_hgrn_w_f, v_hgrn_w_i, v_hgrn_w_g, v_hgrn_g_norm, v_hgrn_w_o, v_hgrn_lb_logits, v_mla_norm, v_mla_w_dq, v_mla_q_norm, v_mla_w_uq, v_mla_w_o, v_kv_in_norm, v_kv_w_dkv, v_kv_norm, v_kv_w_uk, v_kv_w_uv, v_mlp_norm, v_mlp_w_up, v_mlp_w_down, v_final_norm):
    given = dict(locals())
    wsh = {n: given[n] for n in WEIGHTS}
    msh = {n: given["m_" + n] for n in WEIGHTS}
    vsh = {n: given["v_" + n] for n in WEIGHTS}
    xi, yi, ci = _me()
    chip = 2 * xi + yi
    cq = jnp.stack([ci, chip]).astype(jnp.int32)

    small_w = {n: wsh[n].reshape(r, -1) for n, _, r, _ in SMALL}
    sv = jnp.concatenate([small_w["hgrn_norm"], small_w["hgrn_lb_logits"], jnp.zeros((5, 256), f32)], axis=0)
    g4, sv4 = _all_gather_weights(_pack_shards(wsh, W_EARLY, 0), sv)
    w = _unpack_early(g4)
    w["hgrn_norm"] = sv4[:, 0, :].reshape(1, D_MODEL)
    w["hgrn_lb_logits"] = sv4[:, 1:3, :].transpose(1, 0, 2).reshape(2, D_MODEL)
    for n in ("hgrn_g_norm", "mla_norm", "mla_q_norm", "kv_in_norm", "kv_norm", "mlp_norm", "final_norm"):
        w[n] = small_w[n]

    class Comm:
        late_shard = _pack_shards(wsh, W_LATE, PACK_PAD)
        unpack_late = staticmethod(_unpack_late)
        received_early = None

        @staticmethod
        def reduce_early(grads):
            gp = _pack_grads_early(grads)
            Comm.s32_early, s16 = _sum_over_cores(gp, _send_half_to_sibling(gp, "grads_to_sibling_early"), cq,
                                                  "grads_sum_cores_early")
            return s16

    loss, grad_x, g = _local_step(x.reshape(-1, D_MODEL), loss_target.reshape(-1, D_MODEL), w, Comm)

    tot_early = _exchange_halves(_sum_over_chips(Comm.s32_early, Comm.received_early, cq, "grads_sum_chips_early"),
                                 "grads_exchange_halves_early")
    gp = _pack_grads_late(g)
    s32, s16 = _sum_over_cores(gp, _send_half_to_sibling(gp, "grads_to_sibling_late"), cq, "grads_sum_cores_late")
    from_chips, smalls = _exchange_chips(s16, _pack_small(g, loss))
    tot_late = _exchange_halves(_sum_over_chips(s32, from_chips, cq, "grads_sum_chips_late"), "grads_exchange_halves_late")
    small_tot = _sum_small(smalls)
    loss = small_tot[LOSS_ROW, 0]

    grad, delta, new_m, new_v = {}, {}, {}, {}
    where = [(n, tot_late, G_LATE_OFF[n]) for n, _ in G_LATE] + [(n, tot_early, G_EARLY_OFF[n]) for n, _ in G_EARLY]
    for n, total, (o, r) in where:
        shp = wsh[n].shape
        two_d = (-1, shp[-1])
        grad[n] = total[o:o + r].reshape(shp)
        d, m2, v2 = _adamw(wsh[n].reshape(two_d), grad[n].reshape(two_d), msh[n].reshape(two_d), vsh[n].reshape(two_d),
                           "adamw_" + n)
        delta[n], new_m[n], new_v[n] = d.reshape(shp), m2.reshape(shp), v2.reshape(shp)
    items = []
    for n, row, r, wd in SMALL:
        gs = small_tot[row:row + r, :wd]
        if n in ("hgrn_norm", "hgrn_lb_logits"):
            gs = lax.dynamic_slice(gs, (0, 256 * chip), (r, 256))
        grad[n] = gs.reshape(wsh[n].shape)
        items.append((small_w[n], gs, msh[n].reshape(gs.shape), vsh[n].reshape(gs.shape)))
    for (n, _, _, _), (d, m2, v2) in zip(SMALL, _adamw_small(items)):
        shp = wsh[n].shape
        delta[n], new_m[n], new_v[n] = d.reshape(shp), m2.reshape(shp), v2.reshape(shp)

    return (loss, grad_x.reshape(x.shape), *[grad[n] for n in WEIGHTS], *[delta[n] for n in WEIGHTS],
            *[new_m[n] for n in WEIGHTS], *[new_v[n] for n in WEIGHTS])
```

```python
import functools

import jax
import jax.numpy as jnp
from jax import lax
from jax.experimental import pallas as pl
from jax.experimental.pallas import tpu as pltpu

f32, bf16 = jnp.float32, jnp.bfloat16
HI = lax.Precision.HIGHEST
MESH = pl.DeviceIdType.MESH

D_MODEL = 1024
D_FF = 4096
EPS = 1e-6
HGRN_HEADS, HGRN_DK, HGRN_CHUNK, HGRN_SUB = 8, 128, 64, 16
MLA_HEADS, MLA_NOPE, MLA_ROPE, MLA_V = 16, 128, 64, 128
MLA_Q_LORA, MLA_KV_LORA = 256, 256
ROPE_THETA = 10000.0
ATT_SCALE = (MLA_NOPE + MLA_ROPE) ** -0.5
EXP_CLAMP = 80.0

ADAM_LR, ADAM_B1, ADAM_B2, ADAM_EPS, ADAM_WD, ADAM_STEP = 0.001, 0.9, 0.999, 1e-08, 0.01, 10

V7X_VMEM_BYTES = 64 * 1024 * 1024
VMEM_LIMIT = V7X_VMEM_BYTES - 8 * 1024 * 1024
LANES = 128

PACK_PAD = 16
W_EARLY = (("hgrn_w_q", None, 256), ("hgrn_w_f", None, 256), ("hgrn_w_i", None, 256), ("hgrn_w_g", None, 256),
           ("hgrn_w_o", None, 256), ("mlp_w_up", 0, 1024), ("mlp_w_down", 0, 1024))
W_LATE = (("mla_w_dq", None, 64), ("mla_w_uq", None, 192), ("mla_w_o", None, 512), ("kv_w_dkv", None, 80),
          ("kv_w_uk", None, 128), ("kv_w_uv", None, 128), ("mlp_w_up", 1, 1024), ("mlp_w_down", 1, 1024))
G_LATE = (("hgrn_w_q", 256), ("hgrn_w_f", 256), ("hgrn_w_i", 256), ("hgrn_w_g", 256))
G_EARLY = (("hgrn_w_o", 256), ("mla_w_dq", 64), ("mla_w_uq", 192), ("mla_w_o", 512), ("kv_w_dkv", 80),
           ("kv_w_uk", 128), ("kv_w_uv", 128), ("mlp_w_up", 2048), ("mlp_w_down", 2048))


def _offsets(layout):
    out, o = {}, 0
    for entry in layout:
        out[entry[:-1] if len(entry) == 3 else entry[0]] = (o, entry[-1])
        o += entry[-1]
    return out, o


W_EARLY_OFF, W_EARLY_ROWS = _offsets(W_EARLY)
W_LATE_OFF, _w_late = _offsets(W_LATE)
W_LATE_ROWS = _w_late + PACK_PAD
G_LATE_OFF, G_LATE_ROWS = _offsets(G_LATE)
G_EARLY_OFF, _g_early = _offsets(G_EARLY)
G_EARLY_ROWS = _g_early + PACK_PAD
assert all(r % 32 == 0 for r in (W_EARLY_ROWS, W_LATE_ROWS, G_LATE_ROWS, G_EARLY_ROWS))

WEIGHTS = ("hgrn_norm", "hgrn_w_q", "hgrn_w_f", "hgrn_w_i", "hgrn_w_g", "hgrn_g_norm", "hgrn_w_o", "hgrn_lb_logits",
           "mla_norm", "mla_w_dq", "mla_q_norm", "mla_w_uq", "mla_w_o", "kv_in_norm", "kv_w_dkv", "kv_norm", "kv_w_uk",
           "kv_w_uv", "mlp_norm", "mlp_w_up", "mlp_w_down", "final_norm")
SMALL = (("hgrn_norm", 0, 1, 1024), ("hgrn_lb_logits", 1, 2, 1024), ("hgrn_g_norm", 3, 1, 128),
         ("mla_norm", 4, 1, 1024), ("mla_q_norm", 5, 1, 256), ("kv_in_norm", 6, 1, 1024), ("kv_norm", 7, 1, 256),
         ("mlp_norm", 8, 2, 1024), ("final_norm", 10, 1, 1024))
SMALL_ROWS = 16


def _pc(body, *, name, out_shape, grid=None, in_specs=None, out_specs=None, scratch=(), sem=None, grid_spec=None,
        aliases=None):
    params = pltpu.CompilerParams(dimension_semantics=sem, vmem_limit_bytes=VMEM_LIMIT)
    if grid_spec is not None:
        return pl.pallas_call(body, name=name, out_shape=out_shape, grid_spec=grid_spec, compiler_params=params,
                              interpret=False)
    kw = {k: v for k, v in (("grid", grid), ("in_specs", in_specs), ("out_specs", out_specs),
                            ("input_output_aliases", aliases)) if v is not None}
    return pl.pallas_call(body, name=name, out_shape=out_shape, scratch_shapes=list(scratch), compiler_params=params,
                          interpret=False, **kw)


def _sds(shape, dtype):
    return jax.ShapeDtypeStruct(tuple(shape), dtype)


def _mm(a, b, *, name, ta=False, tb=False, outs=(f32,), epi=None, extras=(), tm=1024, tn=1024, tk=512):
    m, k = (a.shape[1], a.shape[0]) if ta else a.shape
    n = b.shape[0] if tb else b.shape[1]
    tm, tn, tk = min(tm, m), min(tn, n), min(tk, k)
    assert m % tm == 0 and n % tn == 0 and k % tk == 0, (name, m, n, k)
    nk = k // tk
    a_spec = pl.BlockSpec((tk, tm), lambda i, j, kk: (kk, i)) if ta else pl.BlockSpec((tm, tk), lambda i, j, kk: (i, kk))
    b_spec = pl.BlockSpec((tn, tk), lambda i, j, kk: (j, kk)) if tb else pl.BlockSpec((tk, tn), lambda i, j, kk: (kk, j))
    e_specs = [pl.BlockSpec((tm, tn), lambda i, j, kk: (i, j)) if e.shape[1] == n else
               pl.BlockSpec((tm, e.shape[1]), lambda i, j, kk: (i, 0)) for e in extras]
    dn = (((0 if ta else 1,), (1 if tb else 0,)), ((), ()))
    n_e, n_o = len(extras), len(outs)

    def body(*refs):
        a_ref, b_ref = refs[0], refs[1]
        e_refs = refs[2:2 + n_e]
        o_refs = refs[2 + n_e:2 + n_e + n_o]
        acc = refs[-1]
        kk = pl.program_id(2)

        @pl.when(kk == 0)
        def _():
            acc[...] = jnp.zeros_like(acc)

        acc[...] += lax.dot_general(a_ref[...].astype(bf16), b_ref[...].astype(bf16), dn, preferred_element_type=f32)

        @pl.when(kk == nk - 1)
        def _():
            r = acc[...]
            res = epi(r, *[e[...] for e in e_refs]) if epi is not None else (r,)
            for o, v in zip(o_refs, res):
                o[...] = v.astype(o.dtype)

    out = _pc(body, name=name, grid=(m // tm, n // tn, nk),
              in_specs=[a_spec, b_spec] + e_specs,
              out_specs=[pl.BlockSpec((tm, tn), lambda i, j, kk: (i, j)) for _ in outs],
              out_shape=[_sds((m, n), dt) for dt in outs],
              scratch=[pltpu.VMEM((tm, tn), f32)],
              sem=("parallel", "parallel", "arbitrary"))(a, b, *extras)
    return out[0] if n_o == 1 else out


def _rw(fn, rows, bcast, outs, accs=(), *, name, tr=256):
    t = rows[0].shape[0]
    tr = min(tr, t)
    assert t % tr == 0
    n_r, n_b, n_o, n_a = len(rows), len(bcast), len(outs), len(accs)

    def body(*refs):
        r_refs = refs[:n_r]
        b_refs = refs[n_r:n_r + n_b]
        o_refs = refs[n_r + n_b:n_r + n_b + n_o]
        a_refs = refs[n_r + n_b + n_o:]
        res = fn(*[r[...] for r in r_refs], *[b[...] for b in b_refs])
        for o, v in zip(o_refs, res[:n_o]):
            o[...] = v.astype(o.dtype)
        i = pl.program_id(0)
        for a_ref, v in zip(a_refs, res[n_o:]):
            @pl.when(i == 0)
            def _(a_ref=a_ref):
                a_ref[...] = jnp.zeros_like(a_ref)
            a_ref[...] += v

    in_specs = [pl.BlockSpec((tr, r.shape[1]), lambda i: (i, 0)) for r in rows]
    in_specs += [pl.BlockSpec(b.shape, lambda i: (0, 0)) for b in bcast]
    out_specs = [pl.BlockSpec((tr, w), lambda i: (i, 0)) for w, _ in outs]
    out_specs += [pl.BlockSpec(s, lambda i: (0, 0)) for s in accs]
    out_shape = [_sds((t, w), dt) for w, dt in outs] + [_sds(s, f32) for s in accs]
    res = _pc(body, name=name, grid=(t // tr,), in_specs=in_specs, out_specs=out_specs, out_shape=out_shape,
              sem=("arbitrary",))(*rows, *bcast)
    return res


def _rms(x, gain):
    return x * lax.rsqrt(jnp.mean(x * x, axis=-1, keepdims=True) + EPS) * gain


def _rms_bwd(x, gain, dy):
    _, vjp = jax.vjp(_rms, x, gain)
    return vjp(dy)


def _lower_bound(lbl):
    l0, l1 = lbl[0:1, :], lbl[1:2, :]
    mx = jnp.maximum(l0, l1)
    e0, e1 = jnp.exp(l0 - mx), jnp.exp(l1 - mx)
    return e0 / (e0 + e1)


def _gates(qpre, fpre, lbl):
    lb = _lower_bound(lbl)
    q = jax.nn.silu(qpre)
    forget = lb + (1.0 - lb) * jax.nn.sigmoid(fpre)
    return q, 1.0 - forget, jnp.log(forget)


def _head_norm_gate(o, gpre, gn):
    return _rms(o, gn) * jax.nn.silu(gpre)


def _swap_halves(x):
    w = x.shape[1]
    lane = lax.broadcasted_iota(jnp.int32, x.shape, 1)
    return jnp.where((lane % MLA_ROPE) < MLA_ROPE // 2, pltpu.roll(x, w - MLA_ROPE // 2, 1),
                     pltpu.roll(x, MLA_ROPE // 2, 1))


def _tile_lanes(tab, w):
    return tab if w == tab.shape[1] else jnp.concatenate([tab] * (w // tab.shape[1]), axis=1)


def _rope(x, cos, sgn_sin, sign=1.0):
    w = x.shape[1]
    return x * _tile_lanes(cos, w) + sign * _swap_halves(x) * _tile_lanes(sgn_sin, w)


def _bd(a, b, ca, cb):
    return lax.dot_general(a.astype(bf16), b.astype(bf16), (((ca,), (cb,)), ((), ())), preferred_element_type=f32)


@jax.custom_vjp
def _dot_nn(a, b):
    return _bd(a, b, 1, 0)


@jax.custom_vjp
def _dot_nt(a, b):
    return _bd(a, b, 1, 1)


@jax.custom_vjp
def _dot_tn(a, b):
    return _bd(a, b, 0, 0)


_dot_nn.defvjp(lambda a, b: (_bd(a, b, 1, 0), (a, b)), lambda r, g: (_bd(g, r[1], 1, 1), _bd(r[0], g, 0, 0)))
_dot_nt.defvjp(lambda a, b: (_bd(a, b, 1, 1), (a, b)), lambda r, g: (_bd(g, r[1], 1, 0), _bd(g, r[0], 0, 0)))
_dot_tn.defvjp(lambda a, b: (_bd(a, b, 0, 0), (a, b)), lambda r, g: (_bd(r[1], g, 1, 1), _bd(r[0], g, 1, 0)))


def _scan_rows(x, reverse):
    n = x.shape[0]
    row = lax.broadcasted_iota(jnp.int32, x.shape, 0)
    s = 1
    while s < n:
        if reverse:
            x = x + jnp.where(row < n - s, pltpu.roll(x, n - s, 0), 0.0)
        else:
            x = x + jnp.where(row >= s, pltpu.roll(x, s, 0), 0.0)
        s *= 2
    return x


@jax.custom_vjp
def _cumsum_rows(g):
    return _scan_rows(g, False)


_cumsum_rows.defvjp(lambda g: (_scan_rows(g, False), None), lambda _, ct: (_scan_rows(ct, True),))

HGRN_PAIRS = HGRN_HEADS // 2
HGRN_PAIR = 2 * HGRN_DK
GLA_STATE = (HGRN_PAIRS, HGRN_PAIR, HGRN_PAIR)


def _gla_consts():
    s = HGRN_SUB
    r = lax.broadcasted_iota(jnp.int32, (HGRN_PAIR, HGRN_PAIR), 0)
    c = lax.broadcasted_iota(jnp.int32, (HGRN_PAIR, HGRN_PAIR), 1)
    pair_mask = (r < HGRN_DK) == (c < HGRN_DK)
    masks = []
    for i in range(HGRN_CHUNK // s):
        n = s * (i + 1)
        row = lax.broadcasted_iota(jnp.int32, (HGRN_HEADS * s, HGRN_HEADS * n), 0)
        col = lax.broadcasted_iota(jnp.int32, (HGRN_HEADS * s, HGRN_HEADS * n), 1)
        col_head = sum((col >= m * n).astype(jnp.int32) for m in range(1, HGRN_HEADS))
        masks.append((col_head == row // s) & (col - col_head * n <= s * i + row % s))
    return pair_mask, masks


def _heads_to_rows(x):
    return jnp.concatenate([x[:, HGRN_DK * h:HGRN_DK * (h + 1)] for h in range(HGRN_HEADS)], axis=0)


def _gla_chunk(consts, dots, q, k, v, g, st):
    pair_mask, masks = consts
    dot_nn, dot_nt, dot_tn = dots
    c, s = HGRN_CHUNK, HGRN_SUB
    b = _cumsum_rows(g)
    b_last = b[c - 1:c, :]
    q_in, k_out = q * jnp.exp(b), k * jnp.exp(b_last - b)
    o_inter, st_new = [], []
    for p in range(HGRN_PAIRS):
        cols = slice(HGRN_PAIR * p, HGRN_PAIR * (p + 1))
        o_inter.append(dot_nt(q_in[:, cols], st[p]))
        st_new.append(st[p] * jnp.exp(b_last[:, cols]) + jnp.where(pair_mask, dot_tn(v[:, cols], k_out[:, cols]), 0.0))
    intra = []
    for i in range(c // s):
        n = s * (i + 1)
        ref = b[s * i - 1:s * i, :] if i else jnp.zeros_like(b_last)
        qt = _heads_to_rows(q[s * i:n] * jnp.exp(b[s * i:n] - ref))
        kt = _heads_to_rows(k[:n] * jnp.exp(jnp.minimum(ref - b[:n], EXP_CLAMP)))
        sc = jnp.where(masks[i], dot_nt(qt, kt), 0.0)
        oi = dot_nn(sc, _heads_to_rows(v[:n]))
        intra.append(jnp.concatenate([oi[s * h:s * (h + 1)] for h in range(HGRN_HEADS)], axis=1))
    return jnp.concatenate(o_inter, axis=1) + jnp.concatenate(intra, axis=0), st_new


_PLAIN_DOTS = (lambda a, b: _bd(a, b, 1, 0), lambda a, b: _bd(a, b, 1, 1), lambda a, b: _bd(a, b, 0, 0))
_VJP_DOTS = (_dot_nn, _dot_nt, _dot_tn)


def _gla_fwd(q, k, p4, g, gather=None):
    t = q.shape[0]
    nc = t // HGRN_CHUNK

    def body(q_ref, k_ref, v_ref, g_ref, *rest):
        if gather is None:
            o_ref, s_ref, st = rest
        else:
            wp_ref, _, o_ref, s_ref, gathered_ref, st, sems = rest

        @pl.when(pl.program_id(0) == 0)
        def _():
            st[...] = jnp.zeros_like(st)
            if gather is not None:
                _gather_start(wp_ref, gathered_ref, sems)

        if gather is not None:
            @pl.when(pl.program_id(0) == nc - 1)
            def _():
                _gather_finish(wp_ref, gathered_ref, sems)

        s_in = [st[p] for p in range(HGRN_PAIRS)]
        o_ref[...], st_new = _gla_chunk(_gla_consts(), _PLAIN_DOTS, q_ref[...], k_ref[...], v_ref[...], g_ref[...], s_in)
        for p in range(HGRN_PAIRS):
            s_ref[0, p] = s_in[p]
            st[p] = st_new[p]

    blk = lambda off: pl.BlockSpec((HGRN_CHUNK, D_MODEL), lambda c: (c, off))
    state_shape = GLA_STATE
    in_specs = [blk(0), blk(0), blk(2), blk(0)]
    out_specs = [blk(0), pl.BlockSpec((1,) + state_shape, lambda c: (c, 0, 0, 0))]
    out_shape = [_sds((t, D_MODEL), f32), _sds((nc,) + state_shape, f32)]
    scratch = [pltpu.VMEM(state_shape, f32)]
    if gather is None:
        return _pc(body, name="gla_fwd", grid=(nc,), in_specs=in_specs, out_specs=out_specs, out_shape=out_shape,
                   scratch=scratch, sem=("arbitrary",))(q, k, p4, g)
    return _pc(body, name="gla_fwd_gather", grid=(nc,), in_specs=in_specs + [HBM, HBM], out_specs=out_specs + [HBM],
               out_shape=out_shape + [_sds((4,) + gather.shape, bf16)], aliases={5: 2},
               scratch=scratch + [pltpu.SemaphoreType.DMA((2, 6))], sem=("arbitrary",))(
                   q, k, p4, g, gather, _gather_base(gather))


def _gla_bwd(q, k, p4, g, states, do, exchange=None):
    t = q.shape[0]
    nc = t // HGRN_CHUNK

    def body(q_ref, k_ref, v_ref, g_ref, s_ref, do_ref, *rest):
        if exchange is None:
            dq_ref, dk_ref, dv_ref, dg_ref, dst = rest
        else:
            sb_ref, dq_ref, dk_ref, dv_ref, dg_ref, recv_ref, dst, sems = rest

        @pl.when(pl.program_id(0) == 0)
        def _():
            dst[...] = jnp.zeros_like(dst)
            if exchange is not None:
                _chips_start(sb_ref, recv_ref, sems)

        if exchange is not None:
            @pl.when(pl.program_id(0) == nc - 1)
            def _():
                _chips_finish(sb_ref, recv_ref, sems)

        consts = _gla_consts()
        fn = lambda qq, kk, vv, gg, ss: _gla_chunk(consts, _VJP_DOTS, qq, kk, vv, gg, ss)
        pairs = range(HGRN_PAIRS)
        _, vjp = jax.vjp(fn, q_ref[...], k_ref[...], v_ref[...], g_ref[...], [s_ref[0, p] for p in pairs])
        dq_ref[...], dk_ref[...], dv_ref[...], dg_ref[...], ds = vjp((do_ref[...], [dst[p] for p in pairs]))
        for p in pairs:
            dst[p] = ds[p]

    blk = lambda off: pl.BlockSpec((HGRN_CHUNK, D_MODEL), lambda c: (nc - 1 - c, off))
    state_shape = GLA_STATE
    in_specs = [blk(0), blk(0), blk(2), blk(0), pl.BlockSpec((1,) + state_shape, lambda c: (nc - 1 - c, 0, 0, 0)), blk(0)]
    out_shape = [_sds((t, D_MODEL), f32)] * 4
    scratch = [pltpu.VMEM(state_shape, f32)]
    if exchange is None:
        return _pc(body, name="gla_bwd", grid=(nc,), in_specs=in_specs, out_specs=[blk(0)] * 4, out_shape=out_shape,
                   scratch=scratch, sem=("arbitrary",))(q, k, p4, g, states, do)
    return _pc(body, name="gla_bwd_exchange", grid=(nc,), in_specs=in_specs + [HBM], out_specs=[blk(0)] * 4 + [HBM],
               out_shape=out_shape + [_sds((3,) + exchange.shape[1:], bf16)],
               scratch=scratch + [pltpu.SemaphoreType.DMA((2, 3))], sem=("arbitrary",))(q, k, p4, g, states, do, exchange)


ATT_FWD_TQ, ATT_FWD_TK = 1024, 1024
ATT_BWD_TQ, ATT_BWD_TK = 1024, 512
ATT_QK = 2 * LANES
NEG = -1e30


def _pair_masks(shape):
    lane = lax.broadcasted_iota(jnp.int32, shape, 1)
    return lane < MLA_ROPE, lane >= MLA_ROPE


def _causal(shape, row0, col0):
    row = row0 + lax.broadcasted_iota(jnp.int32, shape, 0)
    col = col0 + lax.broadcasted_iota(jnp.int32, shape, 1)
    return col <= row


def _qk_cols(e):
    return slice(ATT_QK * e, ATT_QK * (e + 1))


def _v_cols(e):
    return slice(MLA_V * e, MLA_V * (e + 1))


def _attn_fwd(qc, kc, v):
    t = qc.shape[0]
    tq, tk = min(ATT_FWD_TQ, t), min(ATT_FWD_TK, t)
    npair = MLA_HEADS // 2

    def body(q_ref, k_ref, v_ref, o_ref, lse_ref):
        i = pl.program_id(1)
        n_full = (i * tq + 1) // tk
        nkv = (i * tq + tq + tk - 1) // tk
        q = [q_ref[:, _qk_cols(e)] for e in range(2)]

        def step(j, carry, masked):
            ks = pl.ds(pl.multiple_of(j * tk, tk), tk)
            ok = _causal((tq, tk), i * tq, j * tk) if masked else None
            new = []
            for e in range(2):
                m, l, acc = carry[e]
                s = _bd(q[e], k_ref[ks, _qk_cols(e)], 1, 1)
                if masked:
                    s = jnp.where(ok, s, NEG)
                m_new = jnp.maximum(m, jnp.max(s, axis=-1, keepdims=True))
                p = jnp.exp(s - m_new)
                alpha = jnp.exp(m - m_new)
                l = alpha * l + jnp.sum(p, axis=-1, keepdims=True)
                acc = alpha * acc + _bd(p, v_ref[ks, _v_cols(e)], 1, 0)
                new.append((m_new, l, acc))
            return tuple(new)

        one = (jnp.full((tq, 1), NEG, f32), jnp.zeros((tq, 1), f32), jnp.zeros((tq, MLA_V), f32))
        carry = lax.fori_loop(0, n_full, functools.partial(step, masked=False), (one, one))
        carry = lax.fori_loop(n_full, nkv, functools.partial(step, masked=True), carry)
        o_ref[...] = jnp.concatenate([acc / l for _, l, acc in carry], axis=1).astype(o_ref.dtype)
        lo, _ = _pair_masks((tq, LANES))
        lse_ref[...] = jnp.where(lo, *[m + jnp.log(l) for m, l, _ in carry])

    return _pc(body, name="attn_fwd", grid=(npair, t // tq),
               in_specs=[pl.BlockSpec((tq, 2 * ATT_QK), lambda p, i: (i, p)),
                         pl.BlockSpec((t, 2 * ATT_QK), lambda p, i: (0, p)),
                         pl.BlockSpec((t, 2 * MLA_V), lambda p, i: (0, p))],
               out_specs=[pl.BlockSpec((tq, 2 * MLA_V), lambda p, i: (i, p)),
                          pl.BlockSpec((tq, LANES), lambda p, i: (i, p))],
               out_shape=[_sds((t, MLA_HEADS * MLA_V), bf16), _sds((t, npair * LANES), f32)],
               sem=("arbitrary", "arbitrary"))(qc, kc, v)


def _attn_bwd(qc, kc, v, do, lse, delta):
    t = qc.shape[0]
    tq, tk = min(ATT_BWD_TQ, t), min(ATT_BWD_TK, t)
    npair = MLA_HEADS // 2
    nq = t // tq

    def body(q_ref, do_ref, lse_ref, dl_ref, k_ref, v_ref, dq_ref, dk_ref, dv_ref):
        j = pl.program_id(1)

        @pl.when(j == 0)
        def _():
            dq_ref[...] = jnp.zeros_like(dq_ref)

        k = [k_ref[:, _qk_cols(e)] for e in range(2)]
        vv = [v_ref[:, _v_cols(e)] for e in range(2)]

        def step(i, carry, masked):
            qs = pl.ds(pl.multiple_of(i * tq, tq), tq)
            ok = _causal((tq, tk), i * tq, j * tk) if masked else None
            lse2, dl2 = lse_ref[qs, :], dl_ref[qs, :]
            new = []
            for e in range(2):
                dk, dv = carry[e]
                q_e, do_e = q_ref[qs, _qk_cols(e)], do_ref[qs, _v_cols(e)]
                p = jnp.exp(_bd(q_e, k[e], 1, 1) - lse2[:, MLA_ROPE * e:MLA_ROPE * e + 1])
                if masked:
                    p = jnp.where(ok, p, 0.0)
                dv = dv + _bd(p, do_e, 0, 0)
                dp = _bd(do_e, vv[e], 1, 1)
                ds = (p * (dp - dl2[:, MLA_ROPE * e:MLA_ROPE * e + 1])).astype(bf16)
                dk = dk + _bd(ds, q_e, 0, 0)
                dq_ref[qs, _qk_cols(e)] += _bd(ds, k[e], 1, 0)
                new.append((dk, dv))
            return tuple(new)

        one = (jnp.zeros((tk, ATT_QK), f32), jnp.zeros((tk, MLA_V), f32))
        i_full = jnp.minimum((j * tk + tk + tq - 2) // tq, nq)
        carry = lax.fori_loop((j * tk) // tq, i_full, functools.partial(step, masked=True), (one, one))
        carry = lax.fori_loop(i_full, nq, functools.partial(step, masked=False), carry)
        for e in range(2):
            dk_ref[:, _qk_cols(e)] = carry[e][0].astype(dk_ref.dtype)
            dv_ref[:, _v_cols(e)] = carry[e][1].astype(dv_ref.dtype)

    res = lambda w: pl.BlockSpec((t, w), lambda p, j: (0, p))
    blk = lambda w: pl.BlockSpec((tk, w), lambda p, j: (j, p))
    return _pc(body, name="attn_bwd", grid=(npair, t // tk),
               in_specs=[res(2 * ATT_QK), res(2 * MLA_V), res(LANES), res(LANES), blk(2 * ATT_QK), blk(2 * MLA_V)],
               out_specs=[res(2 * ATT_QK), blk(2 * ATT_QK), blk(2 * MLA_V)],
               out_shape=[_sds((t, MLA_HEADS * ATT_QK), f32), _sds((t, MLA_HEADS * ATT_QK), bf16),
                          _sds((t, MLA_HEADS * MLA_V), bf16)],
               sem=("arbitrary", "arbitrary"))(qc, do, lse, delta, kc, v)


def _rope_tables(t):
    half = MLA_ROPE // 2
    inv_freq = ROPE_THETA ** (-jnp.arange(half, dtype=f32) / half)
    ang = jnp.arange(t, dtype=f32)[:, None] * inv_freq[None, :]
    cos, sin = jnp.cos(ang), jnp.sin(ang)
    cos128, sin128 = jnp.concatenate([cos, cos] * 2, axis=1), jnp.concatenate([-sin, sin] * 2, axis=1)
    one, zero = jnp.ones((t, MLA_NOPE), f32), jnp.zeros((t, MLA_NOPE), f32)
    return cos128, sin128, jnp.concatenate([one, cos128], axis=1), jnp.concatenate([zero, sin128], axis=1)


def _relu2_epi(u):
    r = jnp.maximum(u, 0.0)
    return u, r * r


def _add_epi(r, res):
    return (r + res,)


def _drelu2_epi(da, u):
    return (da * 2.0 * jnp.maximum(u.astype(f32), 0.0),)


def _mlp_fwd(h, gain, w_up, w_down, tag):
    xm = _rw(lambda x, g: (_rms(x, g),), [h], [gain], [(D_MODEL, bf16)], name=f"mlp{tag}_norm")[0]
    u, a = _mm(xm, w_up, name=f"mlp{tag}_up", outs=(bf16, bf16), epi=_relu2_epi)
    h_out = _mm(a, w_down, name=f"mlp{tag}_down", epi=_add_epi, extras=(h,))
    return h_out, (xm, u, a)


def _mlp_bwd(dh, h, gain, w_up, w_down, saved, tag):
    xm, u, a = saved
    du = _mm(dh, w_down, tb=True, name=f"mlp{tag}_dact", outs=(bf16,), epi=_drelu2_epi, extras=(u,))
    d_down = _mm(a, dh, ta=True, name=f"mlp{tag}_dwdown")
    d_up = _mm(xm, du, ta=True, name=f"mlp{tag}_dwup")
    dxm = _mm(du, w_up, tb=True, name=f"mlp{tag}_dxm")

    def fn(x, dy, dres, g):
        dx, dg = _rms_bwd(x, g, dy)
        return dx + dres, dg

    dh_in, d_gain = _rw(fn, [h, dxm, dh], [gain], [(D_MODEL, f32)], [(1, D_MODEL)], name=f"mlp{tag}_dnorm")
    return dh_in, d_gain, d_up, d_down


def _local_step(x, target, w, comm=None):
    w = dict(w)
    t = x.shape[0]
    cos, sgn_sin, cos_qk, sin_qk = _rope_tables(t)
    grads = {}

    xn0 = _rw(lambda xx, g: (_rms(xx, g),), [x], [w["hgrn_norm"]], [(D_MODEL, bf16)], name="hgrn_norm")[0]
    p4 = _mm(xn0, w["hgrn_w4"], name="hgrn_proj")

    def gates_fn(p, lbl):
        return _gates(p[:, :D_MODEL], p[:, D_MODEL:2 * D_MODEL], lbl)

    q, k, g = _rw(gates_fn, [p4], [w["hgrn_lb_logits"]], [(D_MODEL, f32)] * 3, name="hgrn_gates")
    if comm is None:
        o, states = _gla_fwd(q, k, p4, g)
    else:
        o, states, gathered = _gla_fwd(q, k, p4, g, gather=comm.late_shard)
        w.update(comm.unpack_late(gathered))

    def hn_fn(oo, p, gn):
        ys = [_head_norm_gate(oo[:, LANES * h:LANES * (h + 1)], p[:, 3 * D_MODEL + LANES * h:3 * D_MODEL + LANES * (h + 1)], gn)
              for h in range(HGRN_HEADS)]
        return (jnp.concatenate(ys, axis=1),)

    y = _rw(hn_fn, [o, p4], [w["hgrn_g_norm"]], [(D_MODEL, bf16)], name="hgrn_headnorm")[0]
    h1 = _mm(y, w["hgrn_w_o"], name="hgrn_out", epi=_add_epi, extras=(x,))
    h2, mlp0 = _mlp_fwd(h1, w["mlp_norm"][0:1], w["mlp_w_up", 0], w["mlp_w_down", 0], 0)

    hk, xn1 = _rw(lambda hh, g1, g2: (_rms(hh, g1), _rms(hh, g2)), [h2], [w["kv_in_norm"], w["mla_norm"]],
                  [(D_MODEL, bf16)] * 2, name="kv_mla_norm")
    ckr = _mm(hk, w["kv_w_dkv"], name="kv_down")

    def ckv_fn(c, cs, sn, g):
        kr = _rope(c[:, MLA_KV_LORA:], cs, sn)
        return _rms(c[:, :MLA_KV_LORA], g), jnp.concatenate([jnp.zeros_like(kr), kr], axis=1)

    c_kv, kr_head = _rw(ckv_fn, [ckr, cos, sgn_sin], [w["kv_norm"]], [(MLA_KV_LORA, bf16), (ATT_QK, f32)],
                        name="kv_norm_rope")
    kc = _mm(c_kv, w["kv_w_kcat"], name="kv_up_k", outs=(bf16,), extras=(kr_head,),
             epi=lambda r, kr: (r + _tile_lanes(kr, r.shape[1]),))
    v_att = _mm(c_kv, w["kv_w_uv"], name="kv_up_v", outs=(bf16,))
    cq0 = _mm(xn1, w["mla_w_dq"], name="q_down")
    c_q = _rw(lambda c, g: (_rms(c, g),), [cq0], [w["mla_q_norm"]], [(MLA_Q_LORA, bf16)], name="q_norm")[0]
    qc = _mm(c_q, w["mla_w_qcat"], name="q_up", outs=(bf16,), extras=(cos_qk, sin_qk),
             epi=lambda r, cs, sn: (_rope(r, cs, sn) * ATT_SCALE,))
    o_att, lse = _attn_fwd(qc, kc, v_att)
    h3 = _mm(o_att, w["mla_w_o"], name="mla_out", epi=_add_epi, extras=(h2,))
    h4, mlp1 = _mlp_fwd(h3, w["mlp_norm"][1:2], w["mlp_w_up", 1], w["mlp_w_down", 1], 1)

    def loss_fn(hh, tgt, gain):
        def f(a, b):
            e = _rms(a, b) - tgt
            return 0.5 * jnp.sum(jnp.sum(e * e, axis=-1, keepdims=True) / D_MODEL, axis=0, keepdims=True)
        val, vjp = jax.vjp(f, hh, gain)
        dh, dg = vjp(jnp.ones((1, 1), f32))
        return dh, jnp.broadcast_to(val, (1, LANES)), dg

    dh4, loss_acc, grads["final_norm"] = _rw(loss_fn, [h4, target], [w["final_norm"]], [(D_MODEL, f32)],
                                             [(1, LANES), (1, D_MODEL)], name="loss")
    loss = loss_acc[0, 0]

    dh3, g_n1, g_up1, g_dn1 = _mlp_bwd(dh4, h3, w["mlp_norm"][1:2], w["mlp_w_up", 1], w["mlp_w_down", 1], mlp1, 1)
    do_att = _mm(dh3, w["mla_w_o"], tb=True, name="mla_dout", outs=(bf16,))
    grads["mla_w_o"] = _mm(o_att, dh3, ta=True, name="mla_dwo")

    def delta_fn(a, b):
        prod = a.astype(f32) * b.astype(f32)
        outs = []
        for p in range(MLA_HEADS // 2):
            d0 = jnp.sum(prod[:, 2 * p * LANES:(2 * p + 1) * LANES], axis=-1, keepdims=True)
            d1 = jnp.sum(prod[:, (2 * p + 1) * LANES:(2 * p + 2) * LANES], axis=-1, keepdims=True)
            lo, _ = _pair_masks((a.shape[0], LANES))
            outs.append(jnp.where(lo, d0, d1))
        return (jnp.concatenate(outs, axis=1),)

    delta = _rw(delta_fn, [do_att, o_att], [], [(MLA_HEADS // 2 * LANES, f32)], name="attn_delta")[0]
    dqc, dkc, dv = _attn_bwd(qc, kc, v_att, do_att, lse, delta)
    dqf = _rw(lambda a, cs, sn: (_rope(a, cs, sn, -1.0) * ATT_SCALE,), [dqc, cos_qk, sin_qk], [],
              [(MLA_HEADS * ATT_QK, bf16)], name="dq_rope")[0]
    dc_q = _mm(dqf, w["mla_w_qcat"], tb=True, name="q_up_dx")
    grads["mla_w_qcat"] = _mm(c_q, dqf, ta=True, name="q_up_dw")

    def dqn_fn(c, dy, g):
        return _rms_bwd(c, g, dy)

    dcq0, grads["mla_q_norm"] = _rw(dqn_fn, [cq0, dc_q], [w["mla_q_norm"]], [(MLA_Q_LORA, bf16)], [(1, MLA_Q_LORA)],
                                    name="q_dnorm")
    dxn1 = _mm(dcq0, w["mla_w_dq"], tb=True, name="q_down_dx")
    grads["mla_w_dq"] = _mm(xn1, dcq0, ta=True, name="q_down_dw")

    dc_kv = _mm(dkc, w["kv_w_kcat"], tb=True, name="kv_up_dx_k")
    dc_kv = _mm(dv, w["kv_w_uv"], tb=True, name="kv_up_dx_v", epi=_add_epi, extras=(dc_kv,))
    grads["kv_w_kcat"] = _mm(c_kv, dkc, ta=True, name="kv_up_dw_k")
    grads["kv_w_uv"] = _mm(c_kv, dv, ta=True, name="kv_up_dw_v")

    def dckr_fn(c, dc, dk_heads, cs, sn, g):
        tot = dk_heads[:, LANES:ATT_QK].astype(f32)
        for h in range(1, MLA_HEADS):
            tot = tot + dk_heads[:, ATT_QK * h + LANES:ATT_QK * (h + 1)].astype(f32)
        lo, _ = _pair_masks(tot.shape)
        dkr = jnp.where(lo, _rope(tot, cs, sn, -1.0), 0.0)
        dcc, dg = _rms_bwd(c[:, :MLA_KV_LORA], g, dc)
        return jnp.concatenate([dcc, dkr], axis=1), dg

    dckr, grads["kv_norm"] = _rw(dckr_fn, [ckr, dc_kv, dkc, cos, sgn_sin], [w["kv_norm"]],
                                 [(MLA_KV_LORA + LANES, bf16)], [(1, MLA_KV_LORA)], name="kv_dnorm_rope")
    dhk = _mm(dckr, w["kv_w_dkv"], tb=True, name="kv_down_dx")
    grads["kv_w_dkv"] = _mm(hk, dckr, ta=True, name="kv_down_dw")

    def dh2_fn(hh, d1, d2, dres, g1, g2):
        a, ga = _rms_bwd(hh, g1, d1)
        b, gb = _rms_bwd(hh, g2, d2)
        return a + b + dres, ga, gb

    dh2, grads["kv_in_norm"], grads["mla_norm"] = _rw(dh2_fn, [h2, dhk, dxn1, dh3], [w["kv_in_norm"], w["mla_norm"]],
                                                      [(D_MODEL, f32)], [(1, D_MODEL)] * 2, name="kv_mla_dnorm")

    dh1, g_n0, g_up0, g_dn0 = _mlp_bwd(dh2, h1, w["mlp_norm"][0:1], w["mlp_w_up", 0], w["mlp_w_down", 0], mlp0, 0)
    grads["mlp_norm"] = jnp.concatenate([g_n0, g_n1], axis=0)
    grads["mlp_w_up"] = (g_up0, g_up1)
    grads["mlp_w_down"] = (g_dn0, g_dn1)
    dy = _mm(dh1, w["hgrn_w_o"], tb=True, name="hgrn_dout")
    grads["hgrn_w_o"] = _mm(y, dh1, ta=True, name="hgrn_dwo")

    def dhn_fn(oo, p, dyy, gn):
        dos, dgs, dgn = [], [], jnp.zeros_like(gn)
        for h in range(HGRN_HEADS):
            cols = slice(LANES * h, LANES * (h + 1))
            _, vjp = jax.vjp(_head_norm_gate, oo[:, cols], p[:, 3 * D_MODEL + LANES * h:3 * D_MODEL + LANES * (h + 1)], gn)
            a, b, c = vjp(dyy[:, cols])
            dos.append(a)
            dgs.append(b)
            dgn = dgn + c
        return jnp.concatenate(dos, axis=1), jnp.concatenate(dgs, axis=1), dgn

    do, dgate, grads["hgrn_g_norm"] = _rw(dhn_fn, [o, p4, dy], [w["hgrn_g_norm"]], [(D_MODEL, f32)] * 2, [(1, HGRN_DK)],
                                          name="hgrn_dheadnorm")
    if comm is None:
        dq, dk, dv_h, dg = _gla_bwd(q, k, p4, g, states, do)
    else:
        dq, dk, dv_h, dg, comm.received_early = _gla_bwd(q, k, p4, g, states, do, exchange=comm.reduce_early(grads))

    def dgates_fn(p, dqq, dkk, dgg, dvv, dgt, lbl):
        _, vjp = jax.vjp(_gates, p[:, :D_MODEL], p[:, D_MODEL:2 * D_MODEL], lbl)
        dqp, dfp, dlbl = vjp((dqq, dkk, dgg))
        return jnp.concatenate([dqp, dfp, dvv, dgt], axis=1), dlbl

    dp4, grads["hgrn_lb_logits"] = _rw(dgates_fn, [p4, dq, dk, dg, dv_h, dgate], [w["hgrn_lb_logits"]],
                                       [(4 * D_MODEL, bf16)], [(2, D_MODEL)], name="hgrn_dgates")
    dxn0 = _mm(dp4, w["hgrn_w4"], tb=True, name="hgrn_proj_dx")
    grads["hgrn_w4"] = _mm(xn0, dp4, ta=True, name="hgrn_proj_dw")

    def dx_fn(xx, dyy, dres, gn):
        dxx, dgn = _rms_bwd(xx, gn, dyy)
        return dxx + dres, dgn

    grad_x, grads["hgrn_norm"] = _rw(dx_fn, [x, dxn0, dh1], [w["hgrn_norm"]], [(D_MODEL, f32)], [(1, D_MODEL)],
                                     name="hgrn_dnorm")
    return loss, grad_x, grads


HBM = pl.BlockSpec(memory_space=pltpu.HBM)


def _me():
    return lax.axis_index("x"), lax.axis_index("y"), lax.axis_index("c")


def _flip(x, y, f):
    return (1 - x if f & 1 else x), (1 - y if f & 2 else y)


def _rcopy(src, dst, sems, k, dev):
    return pltpu.make_async_remote_copy(src_ref=src, dst_ref=dst, send_sem=sems.at[0, k], recv_sem=sems.at[1, k],
                                        device_id=dev, device_id_type=MESH)


def _my_half(rows, c, mine=True):
    half = rows // 2
    return pl.ds(pl.multiple_of((c if mine else 1 - c) * half, 16), half)


def _gather_start(wp_ref, out_ref, sems):
    x, y, c = _me()
    half = _my_half(wp_ref.shape[0], c)
    for f in (1, 2, 3):
        px, py = _flip(x, y, f)
        _rcopy(wp_ref.at[half], out_ref.at[2 * x + y, half], sems, f - 1, (px, py, c)).start()


def _gather_finish(wp_ref, out_ref, sems):
    x, y, c = _me()
    half, other = _my_half(wp_ref.shape[0], c), _my_half(wp_ref.shape[0], c, mine=False)
    sends = []
    for f in (1, 2, 3):
        px, py = _flip(x, y, f)
        landed = out_ref.at[2 * px + py, half]
        _rcopy(landed, landed, sems, f - 1, (px, py, c)).wait_recv()
        sends.append(_rcopy(landed, landed, sems, 2 + f, (x, y, 1 - c)))
        sends[-1].start()
    for f in (1, 2, 3):
        px, py = _flip(x, y, f)
        theirs = out_ref.at[2 * px + py, other]
        _rcopy(theirs, theirs, sems, 2 + f, (x, y, 1 - c)).wait_recv()
        sends.append(_rcopy(wp_ref.at[half], out_ref.at[2 * x + y, half], sems, f - 1, (px, py, c)))
    for cp in sends:
        cp.wait_send()


def _gather_base(wp):
    return jnp.broadcast_to(wp[None], (4,) + wp.shape)


def _all_gather_weights(wp, sv):
    def body(wp_ref, sv_ref, base_ref, out_ref, svs_ref, sems, local_sem):
        x, y, c = _me()
        mine = pltpu.make_async_copy(sv_ref, svs_ref.at[2 * x + y], local_sem)
        mine.start()
        _gather_start(wp_ref, out_ref, sems)
        small = []
        for f in (1, 2, 3):
            px, py = _flip(x, y, f)
            small.append(_rcopy(sv_ref, svs_ref.at[2 * x + y], sems, 5 + f, (px, py, c)))
            small[-1].start()
        _gather_finish(wp_ref, out_ref, sems)
        for f in (1, 2, 3):
            px, py = _flip(x, y, f)
            _rcopy(sv_ref, svs_ref.at[2 * px + py], sems, 5 + f, (px, py, c)).wait_recv()
        for cp in small:
            cp.wait_send()
        mine.wait()

    return _pc(body, name="weights_all_gather", in_specs=[HBM, HBM, HBM], out_specs=[HBM, HBM],
               out_shape=[_sds((4,) + wp.shape, bf16), _sds((4, 8, 256), f32)], aliases={2: 0},
               scratch=[pltpu.SemaphoreType.DMA((2, 9)), pltpu.SemaphoreType.DMA])(wp, sv, _gather_base(wp))


def _send_half_to_sibling(gp, name):
    rows = gp.shape[1]

    def body(gp_ref, out_ref, sems):
        x, y, c = _me()
        cp = _rcopy(gp_ref.at[:, _my_half(rows, c, mine=False)], out_ref, sems, 0, (x, y, 1 - c))
        cp.start()
        cp.wait()

    return _pc(body, name=name, in_specs=[HBM], out_specs=HBM, out_shape=_sds((4, rows // 2, D_MODEL), f32),
               scratch=[pltpu.SemaphoreType.DMA((2, 1))])(gp)


def _chips_start(sb_ref, out_ref, sems):
    x, y, c = _me()
    for f in (1, 2, 3):
        px, py = _flip(x, y, f)
        _rcopy(sb_ref.at[2 * px + py], out_ref.at[f - 1], sems, f - 1, (px, py, c)).start()


def _chips_finish(sb_ref, out_ref, sems):
    x, y, c = _me()
    for f in (1, 2, 3):
        _rcopy(sb_ref.at[0], out_ref.at[f - 1], sems, f - 1, (x, y, c)).wait_recv()
    for f in (1, 2, 3):
        px, py = _flip(x, y, f)
        _rcopy(sb_ref.at[2 * px + py], out_ref.at[f - 1], sems, f - 1, (px, py, c)).wait_send()


def _exchange_chips(sb, small):
    def body(sb_ref, small_ref, out_ref, smalls_ref, sems, local_sem):
        x, y, c = _me()
        me = 4 * x + 2 * y + c
        mine = pltpu.make_async_copy(small_ref, smalls_ref.at[me], local_sem)
        mine.start()
        _chips_start(sb_ref, out_ref, sems)
        sends = []
        for f in range(1, 8):
            px, py = _flip(x, y, f)
            pc = 1 - c if f & 4 else c
            sends.append(_rcopy(small_ref, smalls_ref.at[me], sems, 2 + f, (px, py, pc)))
            sends[-1].start()
        _chips_finish(sb_ref, out_ref, sems)
        for f in range(1, 8):
            px, py = _flip(x, y, f)
            pc = 1 - c if f & 4 else c
            _rcopy(small_ref, smalls_ref.at[4 * px + 2 * py + pc], sems, 2 + f, (x, y, c)).wait_recv()
        for cp in sends:
            cp.wait_send()
        mine.wait()

    return _pc(body, name="grads_exchange_chips", in_specs=[HBM, HBM], out_specs=[HBM, HBM],
               out_shape=[_sds((3,) + sb.shape[1:], bf16), _sds((8, SMALL_ROWS, D_MODEL), f32)],
               scratch=[pltpu.SemaphoreType.DMA((2, 10)), pltpu.SemaphoreType.DMA])(sb, small)


def _exchange_halves(tot, name):
    rows = tot.shape[0]

    def body(tot_ref, out_ref, sems):
        x, y, c = _me()
        half = _my_half(rows, c)
        cp = _rcopy(tot_ref.at[half], out_ref.at[half], sems, 0, (x, y, 1 - c))
        cp.start()
        cp.wait()

    return _pc(body, name=name, in_specs=[HBM], out_specs=HBM, out_shape=_sds((rows, D_MODEL), f32),
               aliases={0: 0}, scratch=[pltpu.SemaphoreType.DMA((2, 1))])(tot)


def _sum_rows(half):
    return max(r for r in range(16, 513, 16) if half % r == 0)


def _sum_over_cores(gp, recv, cq, name):
    half = recv.shape[1]
    tr = _sum_rows(half)
    nb = half // tr

    def body(cq_ref, g_ref, r_ref, o32_ref, o16_ref):
        s = g_ref[...] + r_ref[...]
        o32_ref[...] = s
        o16_ref[...] = s.astype(bf16)

    spec = pl.BlockSpec((1, tr, D_MODEL), lambda b, i, cq_ref: (b, i, 0))
    gs = pltpu.PrefetchScalarGridSpec(
        num_scalar_prefetch=1, grid=(4, nb),
        in_specs=[pl.BlockSpec((1, tr, D_MODEL), lambda b, i, cq_ref: (b, cq_ref[0] * nb + i, 0)), spec],
        out_specs=[spec, spec])
    return _pc(body, name=name, grid_spec=gs, sem=("arbitrary", "arbitrary"),
               out_shape=[_sds((4, half, D_MODEL), f32), _sds((4, half, D_MODEL), bf16)])(cq, gp, recv)


def _sum_over_chips(s32, recv, cq, name):
    half = recv.shape[1]
    tr = _sum_rows(half)
    nb = half // tr

    def body(cq_ref, own_ref, r_ref, o_ref):
        o_ref[...] = ((own_ref[0] + r_ref[0].astype(f32)) + r_ref[1].astype(f32)) + r_ref[2].astype(f32)

    gs = pltpu.PrefetchScalarGridSpec(
        num_scalar_prefetch=1, grid=(nb,),
        in_specs=[pl.BlockSpec((1, tr, D_MODEL), lambda i, cq_ref: (cq_ref[1], i, 0)),
                  pl.BlockSpec((3, tr, D_MODEL), lambda i, cq_ref: (0, i, 0))],
        out_specs=pl.BlockSpec((tr, D_MODEL), lambda i, cq_ref: (cq_ref[0] * nb + i, 0)))
    return _pc(body, name=name, grid_spec=gs, sem=("arbitrary",),
               out_shape=_sds((2 * half, D_MODEL), f32))(cq, s32, recv)


def _sum_small(smalls):
    def body(s_ref, o_ref):
        tot = s_ref[0]
        for d in range(1, 8):
            tot = tot + s_ref[d]
        o_ref[...] = tot

    return _pc(body, name="small_sum", out_shape=_sds((SMALL_ROWS, D_MODEL), f32))(smalls)


def _adamw_math(w, g, m, v):
    m = ADAM_B1 * m + (1.0 - ADAM_B1) * g
    v = ADAM_B2 * v + (1.0 - ADAM_B2) * jnp.square(g)
    m_hat = m / (1.0 - ADAM_B1 ** ADAM_STEP)
    v_hat = v / (1.0 - ADAM_B2 ** ADAM_STEP)
    delta = -ADAM_LR * (m_hat / (jnp.sqrt(v_hat) + ADAM_EPS) + ADAM_WD * w)
    return delta, m, v


def _adamw(w, g, m, v, name):
    cols = w.shape[1]
    return _rw(_adamw_math, [w, g, m, v], [], [(cols, f32)] * 3, name=name, tr=256)


def _adamw_small(items):
    n = len(items)

    def body(*refs):
        ins, outs = refs[:4 * n], refs[4 * n:]
        for i in range(n):
            res = _adamw_math(*[r[...] for r in ins[4 * i:4 * i + 4]])
            for o, val in zip(outs[3 * i:3 * i + 3], res):
                o[...] = val

    flat = [a for it in items for a in it]
    out_shape = [_sds(it[0].shape, f32) for it in items for _ in range(3)]
    res = _pc(body, name="adamw_small", out_shape=out_shape)(*flat)
    return [tuple(res[3 * i:3 * i + 3]) for i in range(n)]


def _pack_shards(sh, layout, pad):
    parts = [(sh[n] if layer is None else sh[n][layer]).reshape(-1, D_MODEL).astype(bf16) for n, layer, _ in layout]
    if pad:
        parts.append(jnp.zeros((pad, D_MODEL), bf16))
    return jnp.concatenate(parts, axis=0)


def _mlp_full(g4, off, layer):
    o, r = off["mlp_w_up", layer]
    up = g4[:, o:o + r].transpose(1, 0, 2).reshape(D_MODEL, D_FF)
    o, r = off["mlp_w_down", layer]
    return {("mlp_w_up", layer): up, ("mlp_w_down", layer): g4[:, o:o + r].reshape(D_FF, D_MODEL)}


def _unpack_early(g4):
    w = _mlp_full(g4, W_EARLY_OFF, 0)
    hg = g4[:, 0:1024].reshape(4, 4, 256, D_MODEL)
    w["hgrn_w4"] = hg.transpose(0, 2, 1, 3).reshape(D_MODEL, 4 * D_MODEL)
    o, r = W_EARLY_OFF["hgrn_w_o", None]
    w["hgrn_w_o"] = g4[:, o:o + r].reshape(D_MODEL, D_MODEL)
    return w


def _unpack_late(g4):
    def rows(name):
        o, r = W_LATE_OFF[name, None]
        return g4[:, o:o + r]

    w = _mlp_full(g4, W_LATE_OFF, 1)
    w["mla_w_dq"] = rows("mla_w_dq").reshape(D_MODEL, MLA_Q_LORA)
    uq = rows("mla_w_uq").reshape(4, MLA_Q_LORA, 768).transpose(1, 0, 2).reshape(MLA_Q_LORA, MLA_HEADS, MLA_NOPE + MLA_ROPE)
    w["mla_w_qcat"] = jnp.pad(uq, ((0, 0), (0, 0), (0, ATT_QK - MLA_NOPE - MLA_ROPE))).reshape(MLA_Q_LORA, MLA_HEADS * ATT_QK)
    w["mla_w_o"] = rows("mla_w_o").reshape(MLA_HEADS * MLA_V, D_MODEL)
    dkv = rows("kv_w_dkv").reshape(D_MODEL, MLA_KV_LORA + MLA_ROPE)
    w["kv_w_dkv"] = jnp.pad(dkv, ((0, 0), (0, LANES - MLA_ROPE)))
    uk = rows("kv_w_uk").reshape(4, MLA_KV_LORA, 512).transpose(1, 0, 2).reshape(MLA_KV_LORA, MLA_HEADS, MLA_NOPE)
    w["kv_w_kcat"] = jnp.pad(uk, ((0, 0), (0, 0), (0, ATT_QK - MLA_NOPE))).reshape(MLA_KV_LORA, MLA_HEADS * ATT_QK)
    w["kv_w_uv"] = rows("kv_w_uv").reshape(4, MLA_KV_LORA, 512).transpose(1, 0, 2).reshape(MLA_KV_LORA, MLA_HEADS * MLA_V)
    return w


def _pack_grads_late(g):
    return g["hgrn_w4"].reshape(4, 256, 4, D_MODEL).transpose(0, 2, 1, 3).reshape(4, G_LATE_ROWS, D_MODEL)


def _pack_grads_early(g):
    parts = [g["hgrn_w_o"].reshape(4, 256, D_MODEL),
             g["mla_w_dq"].reshape(4, 64, D_MODEL)]
    uq = g["mla_w_qcat"].reshape(MLA_Q_LORA, MLA_HEADS, ATT_QK)[:, :, :MLA_NOPE + MLA_ROPE]
    parts.append(uq.reshape(MLA_Q_LORA, 4, 768).transpose(1, 0, 2).reshape(4, 192, D_MODEL))
    parts.append(g["mla_w_o"].reshape(4, 512, D_MODEL))
    parts.append(g["kv_w_dkv"][:, :MLA_KV_LORA + MLA_ROPE].reshape(4, 80, D_MODEL))
    uk = g["kv_w_kcat"].reshape(MLA_KV_LORA, MLA_HEADS, ATT_QK)[:, :, :MLA_NOPE]
    parts.append(uk.reshape(MLA_KV_LORA, 4, 512).transpose(1, 0, 2).reshape(4, 128, D_MODEL))
    parts.append(g["kv_w_uv"].reshape(MLA_KV_LORA, 4, 512).transpose(1, 0, 2).reshape(4, 128, D_MODEL))
    parts += [up.reshape(D_MODEL, 4, 1024).transpose(1, 0, 2) for up in g["mlp_w_up"]]
    parts += [dn.reshape(4, 1024, D_MODEL) for dn in g["mlp_w_down"]]
    parts.append(jnp.zeros((4, PACK_PAD, D_MODEL), f32))
    return jnp.concatenate(parts, axis=1)


LOSS_ROW = 11


def _pack_small(g, loss):
    rows = []
    for name, _, r, wd in SMALL:
        a = g[name].reshape(r, wd)
        rows.append(jnp.pad(a, ((0, 0), (0, D_MODEL - wd))) if wd < D_MODEL else a)
    assert sum(r for _, _, r, _ in SMALL) == LOSS_ROW
    rows.append(jnp.full((1, D_MODEL), loss, f32))
    rows.append(jnp.zeros((SMALL_ROWS - LOSS_ROW - 1, D_MODEL), f32))
    return jnp.concatenate(rows, axis=0)


def kernel(x, hgrn_norm, hgrn_w_q, hgrn_w_f, hgrn_w_i, hgrn_w_g, hgrn_g_norm, hgrn_w_o, hgrn_lb_logits, mla_norm, mla_w_dq, mla_q_norm, mla_w_uq, mla_w_o, kv_in_norm, kv_w_dkv, kv_norm, kv_w_uk, kv_w_uv, mlp_norm, mlp_w_up, mlp_w_down, final_norm, loss_target, m_hgrn_norm, m_hgrn_w_q, m_hgrn_w_f, m_hgrn_w_i, m_hgrn_w_g, m_hgrn_g_norm, m_hgrn_w_o, m_hgrn_lb_logits, m_mla_norm, m_mla_w_dq, m_mla_q_norm, m_mla_w_uq, m_mla_w_o, m_kv_in_norm, m_kv_w_dkv, m_kv_norm, m_kv_w_uk, m_kv_w_uv, m_mlp_norm, m_mlp_w_up, m_mlp_w_down, m_final_norm, v_hgrn_norm, v_hgrn_w_q, v_hgrn_w_f, v_hgrn_w_i, v_hgrn_w_g, v_hgrn_g_norm, v_hgrn_w_o, v_hgrn_lb_logits, v_mla_norm, v_mla_w_dq, v_mla_q_norm, v_mla_w_uq, v_mla_w_o, v_kv_in_norm, v_kv_w_dkv, v_kv_norm, v_kv_w_uk, v_kv_w_uv, v_mlp_norm, v_mlp_w_up, v_mlp_w_down, v_final_norm):
    given = dict(locals())
    wsh = {n: given[n] for n in WEIGHTS}
    msh = {n: given["m_" + n] for n in WEIGHTS}
    vsh = {n: given["v_" + n] for n in WEIGHTS}
    xi, yi, ci = _me()
    chip = 2 * xi + yi
    cq = jnp.stack([ci, chip]).astype(jnp.int32)

    small_w = {n: wsh[n].reshape(r, -1) for n, _, r, _ in SMALL}
    sv = jnp.concatenate([small_w["hgrn_norm"], small_w["hgrn_lb_logits"], jnp.zeros((5, 256), f32)], axis=0)
    g4, sv4 = _all_gather_weights(_pack_shards(wsh, W_EARLY, 0), sv)
    w = _unpack_early(g4)
    w["hgrn_norm"] = sv4[:, 0, :].reshape(1, D_MODEL)
    w["hgrn_lb_logits"] = sv4[:, 1:3, :].transpose(1, 0, 2).reshape(2, D_MODEL)
    for n in ("hgrn_g_norm", "mla_norm", "mla_q_norm", "kv_in_norm", "kv_norm", "mlp_norm", "final_norm"):
        w[n] = small_w[n]

    class Comm:
        late_shard = _pack_shards(wsh, W_LATE, PACK_PAD)
        unpack_late = staticmethod(_unpack_late)
        received_early = None

        @staticmethod
        def reduce_early(grads):
            gp = _pack_grads_early(grads)
            Comm.s32_early, s16 = _sum_over_cores(gp, _send_half_to_sibling(gp, "grads_to_sibling_early"), cq,
                                                  "grads_sum_cores_early")
            return s16

    loss, grad_x, g = _local_step(x.reshape(-1, D_MODEL), loss_target.reshape(-1, D_MODEL), w, Comm)

    tot_early = _exchange_halves(_sum_over_chips(Comm.s32_early, Comm.received_early, cq, "grads_sum_chips_early"),
                                 "grads_exchange_halves_early")
    gp = _pack_grads_late(g)
    s32, s16 = _sum_over_cores(gp, _send_half_to_sibling(gp, "grads_to_sibling_late"), cq, "grads_sum_cores_late")
    from_chips, smalls = _exchange_chips(s16, _pack_small(g, loss))
    tot_late = _exchange_halves(_sum_over_chips(s32, from_chips, cq, "grads_sum_chips_late"), "grads_exchange_halves_late")
    small_tot = _sum_small(smalls)
    loss = small_tot[LOSS_ROW, 0]

    grad, delta, new_m, new_v = {}, {}, {}, {}
    where = [(n, tot_late, G_LATE_OFF[n]) for n, _ in G_LATE] + [(n, tot_early, G_EARLY_OFF[n]) for n, _ in G_EARLY]
    for n, total, (o, r) in where:
        shp = wsh[n].shape
        two_d = (-1, shp[-1])
        grad[n] = total[o:o + r].reshape(shp)
        d, m2, v2 = _adamw(wsh[n].reshape(two_d), grad[n].reshape(two_d), msh[n].reshape(two_d), vsh[n].reshape(two_d),
                           "adamw_" + n)
        delta[n], new_m[n], new_v[n] = d.reshape(shp), m2.reshape(shp), v2.reshape(shp)
    items = []
    for n, row, r, wd in SMALL:
        gs = small_tot[row:row + r, :wd]
        if n in ("hgrn_norm", "hgrn_lb_logits"):
            gs = lax.dynamic_slice(gs, (0, 256 * chip), (r, 256))
        grad[n] = gs.reshape(wsh[n].shape)
        items.append((small_w[n], gs, msh[n].reshape(gs.shape), vsh[n].reshape(gs.shape)))
    for (n, _, _, _), (d, m2, v2) in zip(SMALL, _adamw_small(items)):
        shp = wsh[n].shape
        delta[n], new_m[n], new_v[n] = d.reshape(shp), m2.reshape(shp), v2.reshape(shp)

    return (loss, grad_x.reshape(x.shape), *[grad[n] for n in WEIGHTS], *[delta[n] for n in WEIGHTS],
            *[new_m[n] for n in WEIGHTS], *[new_v[n] for n in WEIGHTS])
```

```python
import functools

import jax
import jax.numpy as jnp
from jax import lax
from jax.experimental import pallas as pl
from jax.experimental.pallas import tpu as pltpu

f32, bf16 = jnp.float32, jnp.bfloat16
HI = lax.Precision.HIGHEST
MESH = pl.DeviceIdType.MESH

D_MODEL = 1024
D_FF = 4096
EPS = 1e-6
HGRN_HEADS, HGRN_DK, HGRN_CHUNK, HGRN_SUB = 8, 128, 64, 16
MLA_HEADS, MLA_NOPE, MLA_ROPE, MLA_V = 16, 128, 64, 128
MLA_Q_LORA, MLA_KV_LORA = 256, 256
ROPE_THETA = 10000.0
ATT_SCALE = (MLA_NOPE + MLA_ROPE) ** -0.5
EXP_CLAMP = 80.0

ADAM_LR, ADAM_B1, ADAM_B2, ADAM_EPS, ADAM_WD, ADAM_STEP = 0.001, 0.9, 0.999, 1e-08, 0.01, 10

V7X_VMEM_BYTES = 64 * 1024 * 1024
VMEM_LIMIT = V7X_VMEM_BYTES - 8 * 1024 * 1024
LANES = 128

PACK_PAD = 16
W_EARLY = (("hgrn_w_q", None, 256), ("hgrn_w_f", None, 256), ("hgrn_w_i", None, 256), ("hgrn_w_g", None, 256),
           ("hgrn_w_o", None, 256), ("mlp_w_up", 0, 1024), ("mlp_w_down", 0, 1024))
W_LATE = (("mla_w_dq", None, 64), ("mla_w_uq", None, 192), ("mla_w_o", None, 512), ("kv_w_dkv", None, 80),
          ("kv_w_uk", None, 128), ("kv_w_uv", None, 128), ("mlp_w_up", 1, 1024), ("mlp_w_down", 1, 1024))
G_LATE = (("hgrn_w_q", 256), ("hgrn_w_f", 256), ("hgrn_w_i", 256), ("hgrn_w_g", 256))
G_EARLY = (("hgrn_w_o", 256), ("mla_w_dq", 64), ("mla_w_uq", 192), ("mla_w_o", 512), ("kv_w_dkv", 80),
           ("kv_w_uk", 128), ("kv_w_uv", 128), ("mlp_w_up", 2048), ("mlp_w_down", 2048))


def _offsets(layout):
    out, o = {}, 0
    for entry in layout:
        out[entry[:-1] if len(entry) == 3 else entry[0]] = (o, entry[-1])
        o += entry[-1]
    return out, o


W_EARLY_OFF, W_EARLY_ROWS = _offsets(W_EARLY)
W_LATE_OFF, _w_late = _offsets(W_LATE)
W_LATE_ROWS = _w_late + PACK_PAD
G_LATE_OFF, G_LATE_ROWS = _offsets(G_LATE)
G_EARLY_OFF, _g_early = _offsets(G_EARLY)
G_EARLY_ROWS = _g_early + PACK_PAD
assert all(r % 32 == 0 for r in (W_EARLY_ROWS, W_LATE_ROWS, G_LATE_ROWS, G_EARLY_ROWS))

WEIGHTS = ("hgrn_norm", "hgrn_w_q", "hgrn_w_f", "hgrn_w_i", "hgrn_w_g", "hgrn_g_norm", "hgrn_w_o", "hgrn_lb_logits",
           "mla_norm", "mla_w_dq", "mla_q_norm", "mla_w_uq", "mla_w_o", "kv_in_norm", "kv_w_dkv", "kv_norm", "kv_w_uk",
           "kv_w_uv", "mlp_norm", "mlp_w_up", "mlp_w_down", "final_norm")
SMALL = (("hgrn_norm", 0, 1, 1024), ("hgrn_lb_logits", 1, 2, 1024), ("hgrn_g_norm", 3, 1, 128),
         ("mla_norm", 4, 1, 1024), ("mla_q_norm", 5, 1, 256), ("kv_in_norm", 6, 1, 1024), ("kv_norm", 7, 1, 256),
         ("mlp_norm", 8, 2, 1024), ("final_norm", 10, 1, 1024))
SMALL_ROWS = 16


def _pc(body, *, name, out_shape, grid=None, in_specs=None, out_specs=None, scratch=(), sem=None, grid_spec=None,
        aliases=None):
    params = pltpu.CompilerParams(dimension_semantics=sem, vmem_limit_bytes=VMEM_LIMIT)
    if grid_spec is not None:
        return pl.pallas_call(body, name=name, out_shape=out_shape, grid_spec=grid_spec, compiler_params=params,
                              interpret=False)
    kw = {k: v for k, v in (("grid", grid), ("in_specs", in_specs), ("out_specs", out_specs),
                            ("input_output_aliases", aliases)) if v is not None}
    return pl.pallas_call(body, name=name, out_shape=out_shape, scratch_shapes=list(scratch), compiler_params=params,
                          interpret=False, **kw)


def _sds(shape, dtype):
    return jax.ShapeDtypeStruct(tuple(shape), dtype)


def _mm(a, b, *, name, ta=False, tb=False, outs=(f32,), epi=None, extras=(), tm=1024, tn=1024, tk=512):
    m, k = (a.shape[1], a.shape[0]) if ta else a.shape
    n = b.shape[0] if tb else b.shape[1]
    tm, tn, tk = min(tm, m), min(tn, n), min(tk, k)
    assert m % tm == 0 and n % tn == 0 and k % tk == 0, (name, m, n, k)
    nk = k // tk
    a_spec = pl.BlockSpec((tk, tm), lambda i, j, kk: (kk, i)) if ta else pl.BlockSpec((tm, tk), lambda i, j, kk: (i, kk))
    b_spec = pl.BlockSpec((tn, tk), lambda i, j, kk: (j, kk)) if tb else pl.BlockSpec((tk, tn), lambda i, j, kk: (kk, j))
    e_specs = [pl.BlockSpec((tm, tn), lambda i, j, kk: (i, j)) if e.shape[1] == n else
               pl.BlockSpec((tm, e.shape[1]), lambda i, j, kk: (i, 0)) for e in extras]
    dn = (((0 if ta else 1,), (1 if tb else 0,)), ((), ()))
    n_e, n_o = len(extras), len(outs)

    def finish(r, e_refs, o_refs):
        res = epi(r, *[e[...] for e in e_refs]) if epi is not None else (r,)
        for o, v in zip(o_refs, res):
            o[...] = v.astype(o.dtype)

    def body(*refs):
        a_ref, b_ref = refs[0], refs[1]
        e_refs = refs[2:2 + n_e]
        o_refs = refs[2 + n_e:2 + n_e + n_o]
        prod = lax.dot_general(a_ref[...].astype(bf16), b_ref[...].astype(bf16), dn, preferred_element_type=f32)
        if nk == 1:
            finish(prod, e_refs, o_refs)
            return
        acc = refs[-1]
        kk = pl.program_id(2)

        @pl.when(kk == 0)
        def _():
            acc[...] = prod

        @pl.when(kk > 0)
        def _():
            acc[...] += prod

        @pl.when(kk == nk - 1)
        def _():
            finish(acc[...], e_refs, o_refs)

    out = _pc(body, name=name, grid=(m // tm, n // tn, nk),
              in_specs=[a_spec, b_spec] + e_specs,
              out_specs=[pl.BlockSpec((tm, tn), lambda i, j, kk: (i, j)) for _ in outs],
              out_shape=[_sds((m, n), dt) for dt in outs],
              scratch=[pltpu.VMEM((tm, tn), f32)] if nk > 1 else [],
              sem=("parallel", "parallel", "arbitrary"))(a, b, *extras)
    return out[0] if n_o == 1 else out


def _rw(fn, rows, bcast, outs, accs=(), *, name, tr=256):
    t = rows[0].shape[0]
    tr = min(tr, t)
    assert t % tr == 0
    n_r, n_b, n_o, n_a = len(rows), len(bcast), len(outs), len(accs)

    def body(*refs):
        r_refs = refs[:n_r]
        b_refs = refs[n_r:n_r + n_b]
        o_refs = refs[n_r + n_b:n_r + n_b + n_o]
        a_refs = refs[n_r + n_b + n_o:]
        res = fn(*[r[...] for r in r_refs], *[b[...] for b in b_refs])
        for o, v in zip(o_refs, res[:n_o]):
            o[...] = v.astype(o.dtype)
        i = pl.program_id(0)
        for a_ref, v in zip(a_refs, res[n_o:]):
            @pl.when(i == 0)
            def _(a_ref=a_ref):
                a_ref[...] = jnp.zeros_like(a_ref)
            a_ref[...] += v

    in_specs = [pl.BlockSpec((tr, r.shape[1]), lambda i: (i, 0)) for r in rows]
    in_specs += [pl.BlockSpec(b.shape, lambda i: (0, 0)) for b in bcast]
    out_specs = [pl.BlockSpec((tr, w), lambda i: (i, 0)) for w, _ in outs]
    out_specs += [pl.BlockSpec(s, lambda i: (0, 0)) for s in accs]
    out_shape = [_sds((t, w), dt) for w, dt in outs] + [_sds(s, f32) for s in accs]
    res = _pc(body, name=name, grid=(t // tr,), in_specs=in_specs, out_specs=out_specs, out_shape=out_shape,
              sem=("arbitrary",))(*rows, *bcast)
    return res


def _rms(x, gain):
    return x * lax.rsqrt(jnp.mean(x * x, axis=-1, keepdims=True) + EPS) * gain


def _rms_bwd(x, gain, dy):
    _, vjp = jax.vjp(_rms, x, gain)
    return vjp(dy)


def _lower_bound(lbl):
    l0, l1 = lbl[0:1, :], lbl[1:2, :]
    mx = jnp.maximum(l0, l1)
    e0, e1 = jnp.exp(l0 - mx), jnp.exp(l1 - mx)
    return e0 / (e0 + e1)


def _gates(qpre, fpre, lbl):
    lb = _lower_bound(lbl)
    q = jax.nn.silu(qpre)
    forget = lb + (1.0 - lb) * jax.nn.sigmoid(fpre)
    return q, 1.0 - forget, jnp.log(forget)


def _head_norm_gate(o, gpre, gn):
    return _rms(o, gn) * jax.nn.silu(gpre)


def _swap_halves(x):
    w = x.shape[1]
    lane = lax.broadcasted_iota(jnp.int32, x.shape, 1)
    return jnp.where((lane % MLA_ROPE) < MLA_ROPE // 2, pltpu.roll(x, w - MLA_ROPE // 2, 1),
                     pltpu.roll(x, MLA_ROPE // 2, 1))


def _tile_lanes(tab, w):
    return tab if w == tab.shape[1] else jnp.concatenate([tab] * (w // tab.shape[1]), axis=1)


def _rope(x, cos, sgn_sin, sign=1.0):
    w = x.shape[1]
    return x * _tile_lanes(cos, w) + sign * _swap_halves(x) * _tile_lanes(sgn_sin, w)


def _bd(a, b, ca, cb):
    return lax.dot_general(a.astype(bf16), b.astype(bf16), (((ca,), (cb,)), ((), ())), preferred_element_type=f32)


@jax.custom_vjp
def _dot_nn(a, b):
    return _bd(a, b, 1, 0)


@jax.custom_vjp
def _dot_nt(a, b):
    return _bd(a, b, 1, 1)


@jax.custom_vjp
def _dot_tn(a, b):
    return _bd(a, b, 0, 0)


_dot_nn.defvjp(lambda a, b: (_bd(a, b, 1, 0), (a, b)), lambda r, g: (_bd(g, r[1], 1, 1), _bd(r[0], g, 0, 0)))
_dot_nt.defvjp(lambda a, b: (_bd(a, b, 1, 1), (a, b)), lambda r, g: (_bd(g, r[1], 1, 0), _bd(g, r[0], 0, 0)))
_dot_tn.defvjp(lambda a, b: (_bd(a, b, 0, 0), (a, b)), lambda r, g: (_bd(r[1], g, 1, 1), _bd(r[0], g, 1, 0)))


def _scan_rows(x, reverse):
    n = x.shape[0]
    row = lax.broadcasted_iota(jnp.int32, x.shape, 0)
    s = 1
    while s < n:
        if reverse:
            x = x + jnp.where(row < n - s, pltpu.roll(x, n - s, 0), 0.0)
        else:
            x = x + jnp.where(row >= s, pltpu.roll(x, s, 0), 0.0)
        s *= 2
    return x


@jax.custom_vjp
def _cumsum_rows(g):
    return _scan_rows(g, False)


_cumsum_rows.defvjp(lambda g: (_scan_rows(g, False), None), lambda _, ct: (_scan_rows(ct, True),))

HGRN_PAIRS = HGRN_HEADS // 2
HGRN_PAIR = 2 * HGRN_DK
GLA_STATE = (HGRN_PAIRS, HGRN_PAIR, HGRN_PAIR)


def _gla_consts():
    s = HGRN_SUB
    r = lax.broadcasted_iota(jnp.int32, (HGRN_PAIR, HGRN_PAIR), 0)
    c = lax.broadcasted_iota(jnp.int32, (HGRN_PAIR, HGRN_PAIR), 1)
    pair_mask = (r < HGRN_DK) == (c < HGRN_DK)
    masks = []
    for i in range(HGRN_CHUNK // s):
        n = s * (i + 1)
        row = lax.broadcasted_iota(jnp.int32, (HGRN_HEADS * s, HGRN_HEADS * n), 0)
        col = lax.broadcasted_iota(jnp.int32, (HGRN_HEADS * s, HGRN_HEADS * n), 1)
        col_head = sum((col >= m * n).astype(jnp.int32) for m in range(1, HGRN_HEADS))
        masks.append((col_head == row // s) & (col - col_head * n <= s * i + row % s))
    return pair_mask, masks


def _heads_to_rows(x):
    return jnp.concatenate([x[:, HGRN_DK * h:HGRN_DK * (h + 1)] for h in range(HGRN_HEADS)], axis=0)


def _gla_chunk(consts, dots, q, k, v, g, st):
    pair_mask, masks = consts
    dot_nn, dot_nt, dot_tn = dots
    c, s = HGRN_CHUNK, HGRN_SUB
    b = _cumsum_rows(g)
    b_last = b[c - 1:c, :]
    q_in, k_out = q * jnp.exp(b), k * jnp.exp(b_last - b)
    o_inter, st_new = [], []
    for p in range(HGRN_PAIRS):
        cols = slice(HGRN_PAIR * p, HGRN_PAIR * (p + 1))
        o_inter.append(dot_nt(q_in[:, cols], st[p]))
        st_new.append(st[p] * jnp.exp(b_last[:, cols]) + jnp.where(pair_mask, dot_tn(v[:, cols], k_out[:, cols]), 0.0))
    intra = []
    for i in range(c // s):
        n = s * (i + 1)
        ref = b[s * i - 1:s * i, :] if i else jnp.zeros_like(b_last)
        qt = _heads_to_rows(q[s * i:n] * jnp.exp(b[s * i:n] - ref))
        kt = _heads_to_rows(k[:n] * jnp.exp(jnp.minimum(ref - b[:n], EXP_CLAMP)))
        sc = jnp.where(masks[i], dot_nt(qt, kt), 0.0)
        oi = dot_nn(sc, _heads_to_rows(v[:n]))
        intra.append(jnp.concatenate([oi[s * h:s * (h + 1)] for h in range(HGRN_HEADS)], axis=1))
    return jnp.concatenate(o_inter, axis=1) + jnp.concatenate(intra, axis=0), st_new


_PLAIN_DOTS = (lambda a, b: _bd(a, b, 1, 0), lambda a, b: _bd(a, b, 1, 1), lambda a, b: _bd(a, b, 0, 0))
_VJP_DOTS = (_dot_nn, _dot_nt, _dot_tn)


def _gla_fwd(q, k, p4, g, gather=None):
    t = q.shape[0]
    nc = t // HGRN_CHUNK

    def body(q_ref, k_ref, v_ref, g_ref, *rest):
        if gather is None:
            o_ref, s_ref, st = rest
        else:
            wp_ref, _, o_ref, s_ref, gathered_ref, st, sems = rest

        @pl.when(pl.program_id(0) == 0)
        def _():
            st[...] = jnp.zeros_like(st)
            if gather is not None:
                _gather_start(wp_ref, gathered_ref, sems)

        if gather is not None:
            @pl.when(pl.program_id(0) == nc - 1)
            def _():
                _gather_finish(wp_ref, gathered_ref, sems)

        s_in = [st[p] for p in range(HGRN_PAIRS)]
        o_ref[...], st_new = _gla_chunk(_gla_consts(), _PLAIN_DOTS, q_ref[...], k_ref[...], v_ref[...], g_ref[...], s_in)
        for p in range(HGRN_PAIRS):
            s_ref[0, p] = s_in[p]
            st[p] = st_new[p]

    blk = lambda off: pl.BlockSpec((HGRN_CHUNK, D_MODEL), lambda c: (c, off))
    state_shape = GLA_STATE
    in_specs = [blk(0), blk(0), blk(2), blk(0)]
    out_specs = [blk(0), pl.BlockSpec((1,) + state_shape, lambda c: (c, 0, 0, 0))]
    out_shape = [_sds((t, D_MODEL), f32), _sds((nc,) + state_shape, f32)]
    scratch = [pltpu.VMEM(state_shape, f32)]
    if gather is None:
        return _pc(body, name="gla_fwd", grid=(nc,), in_specs=in_specs, out_specs=out_specs, out_shape=out_shape,
                   scratch=scratch, sem=("arbitrary",))(q, k, p4, g)
    return _pc(body, name="gla_fwd_gather", grid=(nc,), in_specs=in_specs + [HBM, HBM], out_specs=out_specs + [HBM],
               out_shape=out_shape + [_sds((4,) + gather.shape, bf16)], aliases={5: 2},
               scratch=scratch + [pltpu.SemaphoreType.DMA((2, 6))], sem=("arbitrary",))(
                   q, k, p4, g, gather, _gather_base(gather))


def _gla_bwd(q, k, p4, g, states, do, exchange=None):
    t = q.shape[0]
    nc = t // HGRN_CHUNK

    def body(q_ref, k_ref, v_ref, g_ref, s_ref, do_ref, *rest):
        if exchange is None:
            dq_ref, dk_ref, dv_ref, dg_ref, dst = rest
        else:
            sb_ref, dq_ref, dk_ref, dv_ref, dg_ref, recv_ref, dst, sems = rest

        @pl.when(pl.program_id(0) == 0)
        def _():
            dst[...] = jnp.zeros_like(dst)
            if exchange is not None:
                _chips_start(sb_ref, recv_ref, sems)

        if exchange is not None:
            @pl.when(pl.program_id(0) == nc - 1)
            def _():
                _chips_finish(sb_ref, recv_ref, sems)

        consts = _gla_consts()
        fn = lambda qq, kk, vv, gg, ss: _gla_chunk(consts, _VJP_DOTS, qq, kk, vv, gg, ss)
        pairs = range(HGRN_PAIRS)
        _, vjp = jax.vjp(fn, q_ref[...], k_ref[...], v_ref[...], g_ref[...], [s_ref[0, p] for p in pairs])
        dq_ref[...], dk_ref[...], dv_ref[...], dg_ref[...], ds = vjp((do_ref[...], [dst[p] for p in pairs]))
        for p in pairs:
            dst[p] = ds[p]

    blk = lambda off: pl.BlockSpec((HGRN_CHUNK, D_MODEL), lambda c: (nc - 1 - c, off))
    state_shape = GLA_STATE
    in_specs = [blk(0), blk(0), blk(2), blk(0), pl.BlockSpec((1,) + state_shape, lambda c: (nc - 1 - c, 0, 0, 0)), blk(0)]
    out_shape = [_sds((t, D_MODEL), f32)] * 4
    scratch = [pltpu.VMEM(state_shape, f32)]
    if exchange is None:
        return _pc(body, name="gla_bwd", grid=(nc,), in_specs=in_specs, out_specs=[blk(0)] * 4, out_shape=out_shape,
                   scratch=scratch, sem=("arbitrary",))(q, k, p4, g, states, do)
    return _pc(body, name="gla_bwd_exchange", grid=(nc,), in_specs=in_specs + [HBM], out_specs=[blk(0)] * 4 + [HBM],
               out_shape=out_shape + [_sds((3,) + exchange.shape[1:], bf16)],
               scratch=scratch + [pltpu.SemaphoreType.DMA((2, 3))], sem=("arbitrary",))(q, k, p4, g, states, do, exchange)


ATT_FWD_TQ, ATT_FWD_TK = 1024, 1024
ATT_BWD_TQ, ATT_BWD_TK = 1024, 512
ATT_QK = 2 * LANES
NEG = -1e30


def _pair_masks(shape):
    lane = lax.broadcasted_iota(jnp.int32, shape, 1)
    return lane < MLA_ROPE, lane >= MLA_ROPE


def _causal(shape, row0, col0):
    row = row0 + lax.broadcasted_iota(jnp.int32, shape, 0)
    col = col0 + lax.broadcasted_iota(jnp.int32, shape, 1)
    return col <= row


def _qk_cols(e):
    return slice(ATT_QK * e, ATT_QK * (e + 1))


def _v_cols(e):
    return slice(MLA_V * e, MLA_V * (e + 1))


def _attn_fwd(qc, kc, v):
    t = qc.shape[0]
    tq, tk = min(ATT_FWD_TQ, t), min(ATT_FWD_TK, t)
    npair = MLA_HEADS // 2

    def body(q_ref, k_ref, v_ref, o_ref, lse_ref):
        i = pl.program_id(1)
        n_full = (i * tq + 1) // tk
        nkv = (i * tq + tq + tk - 1) // tk
        q = [q_ref[:, _qk_cols(e)] for e in range(2)]

        def step(j, carry, masked):
            ks = pl.ds(pl.multiple_of(j * tk, tk), tk)
            ok = _causal((tq, tk), i * tq, j * tk) if masked else None
            new = []
            for e in range(2):
                m, l, acc = carry[e]
                s = _bd(q[e], k_ref[ks, _qk_cols(e)], 1, 1)
                if masked:
                    s = jnp.where(ok, s, NEG)
                m_new = jnp.maximum(m, jnp.max(s, axis=-1, keepdims=True))
                p = jnp.exp(s - m_new)
                alpha = jnp.exp(m - m_new)
                l = alpha * l + jnp.sum(p, axis=-1, keepdims=True)
                acc = alpha * acc + _bd(p, v_ref[ks, _v_cols(e)], 1, 0)
                new.append((m_new, l, acc))
            return tuple(new)

        one = (jnp.full((tq, 1), NEG, f32), jnp.zeros((tq, 1), f32), jnp.zeros((tq, MLA_V), f32))
        carry = lax.fori_loop(0, n_full, functools.partial(step, masked=False), (one, one))
        carry = lax.fori_loop(n_full, nkv, functools.partial(step, masked=True), carry)
        o_ref[...] = jnp.concatenate([acc / l for _, l, acc in carry], axis=1).astype(o_ref.dtype)
        lo, _ = _pair_masks((tq, LANES))
        lse_ref[...] = jnp.where(lo, *[m + jnp.log(l) for m, l, _ in carry])

    return _pc(body, name="attn_fwd", grid=(npair, t // tq),
               in_specs=[pl.BlockSpec((tq, 2 * ATT_QK), lambda p, i: (i, p)),
                         pl.BlockSpec((t, 2 * ATT_QK), lambda p, i: (0, p)),
                         pl.BlockSpec((t, 2 * MLA_V), lambda p, i: (0, p))],
               out_specs=[pl.BlockSpec((tq, 2 * MLA_V), lambda p, i: (i, p)),
                          pl.BlockSpec((tq, LANES), lambda p, i: (i, p))],
               out_shape=[_sds((t, MLA_HEADS * MLA_V), bf16), _sds((t, npair * LANES), f32)],
               sem=("arbitrary", "arbitrary"))(qc, kc, v)


def _attn_bwd(qc, kc, v, do, lse, delta):
    t = qc.shape[0]
    tq, tk = min(ATT_BWD_TQ, t), min(ATT_BWD_TK, t)
    npair = MLA_HEADS // 2
    nq = t // tq

    def body(q_ref, do_ref, lse_ref, dl_ref, k_ref, v_ref, dq_ref, dk_ref, dv_ref):
        j = pl.program_id(1)

        @pl.when(j == 0)
        def _():
            dq_ref[...] = jnp.zeros_like(dq_ref)

        k = [k_ref[:, _qk_cols(e)] for e in range(2)]
        vv = [v_ref[:, _v_cols(e)] for e in range(2)]

        def step(i, carry, masked):
            qs = pl.ds(pl.multiple_of(i * tq, tq), tq)
            ok = _causal((tq, tk), i * tq, j * tk) if masked else None
            lse2, dl2 = lse_ref[qs, :], dl_ref[qs, :]
            new = []
            for e in range(2):
                dk, dv = carry[e]
                q_e, do_e = q_ref[qs, _qk_cols(e)], do_ref[qs, _v_cols(e)]
                p = jnp.exp(_bd(q_e, k[e], 1, 1) - lse2[:, MLA_ROPE * e:MLA_ROPE * e + 1])
                if masked:
                    p = jnp.where(ok, p, 0.0)
                dv = dv + _bd(p, do_e, 0, 0)
                dp = _bd(do_e, vv[e], 1, 1)
                ds = (p * (dp - dl2[:, MLA_ROPE * e:MLA_ROPE * e + 1])).astype(bf16)
                dk = dk + _bd(ds, q_e, 0, 0)
                dq_ref[qs, _qk_cols(e)] += _bd(ds, k[e], 1, 0)
                new.append((dk, dv))
            return tuple(new)

        one = (jnp.zeros((tk, ATT_QK), f32), jnp.zeros((tk, MLA_V), f32))
        i_full = jnp.minimum((j * tk + tk + tq - 2) // tq, nq)
        carry = lax.fori_loop((j * tk) // tq, i_full, functools.partial(step, masked=True), (one, one))
        carry = lax.fori_loop(i_full, nq, functools.partial(step, masked=False), carry)
        for e in range(2):
            dk_ref[:, _qk_cols(e)] = carry[e][0].astype(dk_ref.dtype)
            dv_ref[:, _v_cols(e)] = carry[e][1].astype(dv_ref.dtype)

    res = lambda w: pl.BlockSpec((t, w), lambda p, j: (0, p))
    blk = lambda w: pl.BlockSpec((tk, w), lambda p, j: (j, p))
    return _pc(body, name="attn_bwd", grid=(npair, t // tk),
               in_specs=[res(2 * ATT_QK), res(2 * MLA_V), res(LANES), res(LANES), blk(2 * ATT_QK), blk(2 * MLA_V)],
               out_specs=[res(2 * ATT_QK), blk(2 * ATT_QK), blk(2 * MLA_V)],
               out_shape=[_sds((t, MLA_HEADS * ATT_QK), f32), _sds((t, MLA_HEADS * ATT_QK), bf16),
                          _sds((t, MLA_HEADS * MLA_V), bf16)],
               sem=("arbitrary", "arbitrary"))(qc, do, lse, delta, kc, v)


def _rope_tables(t):
    half = MLA_ROPE // 2
    inv_freq = ROPE_THETA ** (-jnp.arange(half, dtype=f32) / half)
    ang = jnp.arange(t, dtype=f32)[:, None] * inv_freq[None, :]
    cos, sin = jnp.cos(ang), jnp.sin(ang)
    cos128, sin128 = jnp.concatenate([cos, cos] * 2, axis=1), jnp.concatenate([-sin, sin] * 2, axis=1)
    one, zero = jnp.ones((t, MLA_NOPE), f32), jnp.zeros((t, MLA_NOPE), f32)
    return cos128, sin128, jnp.concatenate([one, cos128], axis=1), jnp.concatenate([zero, sin128], axis=1)


MLP_TK = (512, 1024)


def _relu2_epi(u):
    r = jnp.maximum(u, 0.0)
    return u, r * r


def _add_epi(r, res):
    return (r + res,)


def _drelu2_epi(da, u):
    return (da * 2.0 * jnp.maximum(u.astype(f32), 0.0),)


def _mlp_fwd(h, gain, w_up, w_down, tag):
    tk = MLP_TK[tag]
    xm = _rw(lambda x, g: (_rms(x, g),), [h], [gain], [(D_MODEL, bf16)], name=f"mlp{tag}_norm")[0]
    u, a = _mm(xm, w_up, name=f"mlp{tag}_up", outs=(bf16, bf16), epi=_relu2_epi, tk=tk)
    h_out = _mm(a, w_down, name=f"mlp{tag}_down", epi=_add_epi, extras=(h,), tk=tk)
    return h_out, (xm, u, a)


def _mlp_bwd(dh, h, gain, w_up, w_down, saved, tag):
    tk = MLP_TK[tag]
    xm, u, a = saved
    du = _mm(dh, w_down, tb=True, name=f"mlp{tag}_dact", outs=(bf16,), epi=_drelu2_epi, extras=(u,), tk=tk)
    d_down = _mm(a, dh, ta=True, name=f"mlp{tag}_dwdown", tk=tk)
    d_up = _mm(xm, du, ta=True, name=f"mlp{tag}_dwup", tk=tk)
    dxm = _mm(du, w_up, tb=True, name=f"mlp{tag}_dxm", tk=tk)

    def fn(x, dy, dres, g):
        dx, dg = _rms_bwd(x, g, dy)
        return dx + dres, dg

    dh_in, d_gain = _rw(fn, [h, dxm, dh], [gain], [(D_MODEL, f32)], [(1, D_MODEL)], name=f"mlp{tag}_dnorm")
    return dh_in, d_gain, d_up, d_down


def _local_step(x, target, w, comm=None):
    w = dict(w)
    t = x.shape[0]
    cos, sgn_sin, cos_qk, sin_qk = _rope_tables(t)
    grads = {}

    xn0 = _rw(lambda xx, g: (_rms(xx, g),), [x], [w["hgrn_norm"]], [(D_MODEL, bf16)], name="hgrn_norm")[0]
    p4 = _mm(xn0, w["hgrn_w4"], name="hgrn_proj")

    def gates_fn(p, lbl):
        return _gates(p[:, :D_MODEL], p[:, D_MODEL:2 * D_MODEL], lbl)

    q, k, g = _rw(gates_fn, [p4], [w["hgrn_lb_logits"]], [(D_MODEL, f32)] * 3, name="hgrn_gates")
    if comm is None:
        o, states = _gla_fwd(q, k, p4, g)
    else:
        o, states, gathered = _gla_fwd(q, k, p4, g, gather=comm.late_shard)
        w.update(comm.unpack_late(gathered))

    def hn_fn(oo, p, gn):
        ys = [_head_norm_gate(oo[:, LANES * h:LANES * (h + 1)], p[:, 3 * D_MODEL + LANES * h:3 * D_MODEL + LANES * (h + 1)], gn)
              for h in range(HGRN_HEADS)]
        return (jnp.concatenate(ys, axis=1),)

    y = _rw(hn_fn, [o, p4], [w["hgrn_g_norm"]], [(D_MODEL, bf16)], name="hgrn_headnorm")[0]
    h1 = _mm(y, w["hgrn_w_o"], name="hgrn_out", epi=_add_epi, extras=(x,))
    h2, mlp0 = _mlp_fwd(h1, w["mlp_norm"][0:1], w["mlp_w_up", 0], w["mlp_w_down", 0], 0)

    hk, xn1 = _rw(lambda hh, g1, g2: (_rms(hh, g1), _rms(hh, g2)), [h2], [w["kv_in_norm"], w["mla_norm"]],
                  [(D_MODEL, bf16)] * 2, name="kv_mla_norm")
    ckr = _mm(hk, w["kv_w_dkv"], name="kv_down")

    def ckv_fn(c, cs, sn, g):
        kr = _rope(c[:, MLA_KV_LORA:], cs, sn)
        return _rms(c[:, :MLA_KV_LORA], g), jnp.concatenate([jnp.zeros_like(kr), kr], axis=1)

    c_kv, kr_head = _rw(ckv_fn, [ckr, cos, sgn_sin], [w["kv_norm"]], [(MLA_KV_LORA, bf16), (ATT_QK, f32)],
                        name="kv_norm_rope")
    kc = _mm(c_kv, w["kv_w_kcat"], name="kv_up_k", outs=(bf16,), extras=(kr_head,),
             epi=lambda r, kr: (r + _tile_lanes(kr, r.shape[1]),))
    v_att = _mm(c_kv, w["kv_w_uv"], name="kv_up_v", outs=(bf16,))
    cq0 = _mm(xn1, w["mla_w_dq"], name="q_down")
    c_q = _rw(lambda c, g: (_rms(c, g),), [cq0], [w["mla_q_norm"]], [(MLA_Q_LORA, bf16)], name="q_norm")[0]
    qc = _mm(c_q, w["mla_w_qcat"], name="q_up", outs=(bf16,), extras=(cos_qk, sin_qk),
             epi=lambda r, cs, sn: (_rope(r, cs, sn) * ATT_SCALE,))
    o_att, lse = _attn_fwd(qc, kc, v_att)
    h3 = _mm(o_att, w["mla_w_o"], name="mla_out", epi=_add_epi, extras=(h2,))
    h4, mlp1 = _mlp_fwd(h3, w["mlp_norm"][1:2], w["mlp_w_up", 1], w["mlp_w_down", 1], 1)

    def loss_fn(hh, tgt, gain):
        def f(a, b):
            e = _rms(a, b) - tgt
            return 0.5 * jnp.sum(jnp.sum(e * e, axis=-1, keepdims=True) / D_MODEL, axis=0, keepdims=True)
        val, vjp = jax.vjp(f, hh, gain)
        dh, dg = vjp(jnp.ones((1, 1), f32))
        return dh, jnp.broadcast_to(val, (1, LANES)), dg

    dh4, loss_acc, grads["final_norm"] = _rw(loss_fn, [h4, target], [w["final_norm"]], [(D_MODEL, f32)],
                                             [(1, LANES), (1, D_MODEL)], name="loss")
    loss = loss_acc[0, 0]

    dh3, g_n1, g_up1, g_dn1 = _mlp_bwd(dh4, h3, w["mlp_norm"][1:2], w["mlp_w_up", 1], w["mlp_w_down", 1], mlp1, 1)
    do_att = _mm(dh3, w["mla_w_o"], tb=True, name="mla_dout", outs=(bf16,))
    grads["mla_w_o"] = _mm(o_att, dh3, ta=True, name="mla_dwo")

    def delta_fn(a, b):
        prod = a.astype(f32) * b.astype(f32)
        outs = []
        for p in range(MLA_HEADS // 2):
            d0 = jnp.sum(prod[:, 2 * p * LANES:(2 * p + 1) * LANES], axis=-1, keepdims=True)
            d1 = jnp.sum(prod[:, (2 * p + 1) * LANES:(2 * p + 2) * LANES], axis=-1, keepdims=True)
            lo, _ = _pair_masks((a.shape[0], LANES))
            outs.append(jnp.where(lo, d0, d1))
        return (jnp.concatenate(outs, axis=1),)

    delta = _rw(delta_fn, [do_att, o_att], [], [(MLA_HEADS // 2 * LANES, f32)], name="attn_delta")[0]
    dqc, dkc, dv = _attn_bwd(qc, kc, v_att, do_att, lse, delta)
    dqf = _rw(lambda a, cs, sn: (_rope(a, cs, sn, -1.0) * ATT_SCALE,), [dqc, cos_qk, sin_qk], [],
              [(MLA_HEADS * ATT_QK, bf16)], name="dq_rope")[0]
    dc_q = _mm(dqf, w["mla_w_qcat"], tb=True, name="q_up_dx")
    grads["mla_w_qcat"] = _mm(c_q, dqf, ta=True, name="q_up_dw")

    def dqn_fn(c, dy, g):
        return _rms_bwd(c, g, dy)

    dcq0, grads["mla_q_norm"] = _rw(dqn_fn, [cq0, dc_q], [w["mla_q_norm"]], [(MLA_Q_LORA, bf16)], [(1, MLA_Q_LORA)],
                                    name="q_dnorm")
    dxn1 = _mm(dcq0, w["mla_w_dq"], tb=True, name="q_down_dx")
    grads["mla_w_dq"] = _mm(xn1, dcq0, ta=True, name="q_down_dw")

    dc_kv = _mm(dkc, w["kv_w_kcat"], tb=True, name="kv_up_dx_k")
    dc_kv = _mm(dv, w["kv_w_uv"], tb=True, name="kv_up_dx_v", epi=_add_epi, extras=(dc_kv,))
    grads["kv_w_kcat"] = _mm(c_kv, dkc, ta=True, name="kv_up_dw_k")
    grads["kv_w_uv"] = _mm(c_kv, dv, ta=True, name="kv_up_dw_v")

    def dckr_fn(c, dc, dk_heads, cs, sn, g):
        tot = dk_heads[:, LANES:ATT_QK].astype(f32)
        for h in range(1, MLA_HEADS):
            tot = tot + dk_heads[:, ATT_QK * h + LANES:ATT_QK * (h + 1)].astype(f32)
        lo, _ = _pair_masks(tot.shape)
        dkr = jnp.where(lo, _rope(tot, cs, sn, -1.0), 0.0)
        dcc, dg = _rms_bwd(c[:, :MLA_KV_LORA], g, dc)
        return jnp.concatenate([dcc, dkr], axis=1), dg

    dckr, grads["kv_norm"] = _rw(dckr_fn, [ckr, dc_kv, dkc, cos, sgn_sin], [w["kv_norm"]],
                                 [(MLA_KV_LORA + LANES, bf16)], [(1, MLA_KV_LORA)], name="kv_dnorm_rope")
    dhk = _mm(dckr, w["kv_w_dkv"], tb=True, name="kv_down_dx")
    grads["kv_w_dkv"] = _mm(hk, dckr, ta=True, name="kv_down_dw")

    def dh2_fn(hh, d1, d2, dres, g1, g2):
        a, ga = _rms_bwd(hh, g1, d1)
        b, gb = _rms_bwd(hh, g2, d2)
        return a + b + dres, ga, gb

    dh2, grads["kv_in_norm"], grads["mla_norm"] = _rw(dh2_fn, [h2, dhk, dxn1, dh3], [w["kv_in_norm"], w["mla_norm"]],
                                                      [(D_MODEL, f32)], [(1, D_MODEL)] * 2, name="kv_mla_dnorm")

    dh1, g_n0, g_up0, g_dn0 = _mlp_bwd(dh2, h1, w["mlp_norm"][0:1], w["mlp_w_up", 0], w["mlp_w_down", 0], mlp0, 0)
    grads["mlp_norm"] = jnp.concatenate([g_n0, g_n1], axis=0)
    grads["mlp_w_up"] = (g_up0, g_up1)
    grads["mlp_w_down"] = (g_dn0, g_dn1)
    dy = _mm(dh1, w["hgrn_w_o"], tb=True, name="hgrn_dout")
    grads["hgrn_w_o"] = _mm(y, dh1, ta=True, name="hgrn_dwo")

    def dhn_fn(oo, p, dyy, gn):
        dos, dgs, dgn = [], [], jnp.zeros_like(gn)
        for h in range(HGRN_HEADS):
            cols = slice(LANES * h, LANES * (h + 1))
            _, vjp = jax.vjp(_head_norm_gate, oo[:, cols], p[:, 3 * D_MODEL + LANES * h:3 * D_MODEL + LANES * (h + 1)], gn)
            a, b, c = vjp(dyy[:, cols])
            dos.append(a)
            dgs.append(b)
            dgn = dgn + c
        return jnp.concatenate(dos, axis=1), jnp.concatenate(dgs, axis=1), dgn

    do, dgate, grads["hgrn_g_norm"] = _rw(dhn_fn, [o, p4, dy], [w["hgrn_g_norm"]], [(D_MODEL, f32)] * 2, [(1, HGRN_DK)],
                                          name="hgrn_dheadnorm")
    if comm is None:
        dq, dk, dv_h, dg = _gla_bwd(q, k, p4, g, states, do)
    else:
        dq, dk, dv_h, dg, comm.received_early = _gla_bwd(q, k, p4, g, states, do, exchange=comm.reduce_early(grads))

    def dgates_fn(p, dqq, dkk, dgg, dvv, dgt, lbl):
        _, vjp = jax.vjp(_gates, p[:, :D_MODEL], p[:, D_MODEL:2 * D_MODEL], lbl)
        dqp, dfp, dlbl = vjp((dqq, dkk, dgg))
        return jnp.concatenate([dqp, dfp, dvv, dgt], axis=1), dlbl

    dp4, grads["hgrn_lb_logits"] = _rw(dgates_fn, [p4, dq, dk, dg, dv_h, dgate], [w["hgrn_lb_logits"]],
                                       [(4 * D_MODEL, bf16)], [(2, D_MODEL)], name="hgrn_dgates")
    dxn0 = _mm(dp4, w["hgrn_w4"], tb=True, name="hgrn_proj_dx", tk=2048)
    grads["hgrn_w4"] = _mm(xn0, dp4, ta=True, name="hgrn_proj_dw", tk=2048)

    def dx_fn(xx, dyy, dres, gn):
        dxx, dgn = _rms_bwd(xx, gn, dyy)
        return dxx + dres, dgn

    grad_x, grads["hgrn_norm"] = _rw(dx_fn, [x, dxn0, dh1], [w["hgrn_norm"]], [(D_MODEL, f32)], [(1, D_MODEL)],
                                     name="hgrn_dnorm")
    return loss, grad_x, grads


HBM = pl.BlockSpec(memory_space=pltpu.HBM)


def _me():
    return lax.axis_index("x"), lax.axis_index("y"), lax.axis_index("c")


def _flip(x, y, f):
    return (1 - x if f & 1 else x), (1 - y if f & 2 else y)


def _rcopy(src, dst, sems, k, dev):
    return pltpu.make_async_remote_copy(src_ref=src, dst_ref=dst, send_sem=sems.at[0, k], recv_sem=sems.at[1, k],
                                        device_id=dev, device_id_type=MESH)


def _my_half(rows, c, mine=True):
    half = rows // 2
    return pl.ds(pl.multiple_of((c if mine else 1 - c) * half, 16), half)


def _gather_start(wp_ref, out_ref, sems):
    x, y, c = _me()
    half = _my_half(wp_ref.shape[0], c)
    for f in (1, 2, 3):
        px, py = _flip(x, y, f)
        _rcopy(wp_ref.at[half], out_ref.at[2 * x + y, half], sems, f - 1, (px, py, c)).start()


def _gather_finish(wp_ref, out_ref, sems):
    x, y, c = _me()
    half, other = _my_half(wp_ref.shape[0], c), _my_half(wp_ref.shape[0], c, mine=False)
    sends = []
    for f in (1, 2, 3):
        px, py = _flip(x, y, f)
        landed = out_ref.at[2 * px + py, half]
        _rcopy(landed, landed, sems, f - 1, (px, py, c)).wait_recv()
        sends.append(_rcopy(landed, landed, sems, 2 + f, (x, y, 1 - c)))
        sends[-1].start()
    for f in (1, 2, 3):
        px, py = _flip(x, y, f)
        theirs = out_ref.at[2 * px + py, other]
        _rcopy(theirs, theirs, sems, 2 + f, (x, y, 1 - c)).wait_recv()
        sends.append(_rcopy(wp_ref.at[half], out_ref.at[2 * x + y, half], sems, f - 1, (px, py, c)))
    for cp in sends:
        cp.wait_send()


def _gather_base(wp):
    return jnp.broadcast_to(wp[None], (4,) + wp.shape)


def _all_gather_weights(wp, sv):
    def body(wp_ref, sv_ref, base_ref, out_ref, svs_ref, sems, local_sem):
        x, y, c = _me()
        mine = pltpu.make_async_copy(sv_ref, svs_ref.at[2 * x + y], local_sem)
        mine.start()
        _gather_start(wp_ref, out_ref, sems)
        small = []
        for f in (1, 2, 3):
            px, py = _flip(x, y, f)
            small.append(_rcopy(sv_ref, svs_ref.at[2 * x + y], sems, 5 + f, (px, py, c)))
            small[-1].start()
        _gather_finish(wp_ref, out_ref, sems)
        for f in (1, 2, 3):
            px, py = _flip(x, y, f)
            _rcopy(sv_ref, svs_ref.at[2 * px + py], sems, 5 + f, (px, py, c)).wait_recv()
        for cp in small:
            cp.wait_send()
        mine.wait()

    return _pc(body, name="weights_all_gather", in_specs=[HBM, HBM, HBM], out_specs=[HBM, HBM],
               out_shape=[_sds((4,) + wp.shape, bf16), _sds((4, 8, 256), f32)], aliases={2: 0},
               scratch=[pltpu.SemaphoreType.DMA((2, 9)), pltpu.SemaphoreType.DMA])(wp, sv, _gather_base(wp))


def _send_half_to_sibling(gp, name):
    rows = gp.shape[1]

    def body(gp_ref, out_ref, sems):
        x, y, c = _me()
        cp = _rcopy(gp_ref.at[:, _my_half(rows, c, mine=False)], out_ref, sems, 0, (x, y, 1 - c))
        cp.start()
        cp.wait()

    return _pc(body, name=name, in_specs=[HBM], out_specs=HBM, out_shape=_sds((4, rows // 2, D_MODEL), f32),
               scratch=[pltpu.SemaphoreType.DMA((2, 1))])(gp)


def _chips_start(sb_ref, out_ref, sems):
    x, y, c = _me()
    for f in (1, 2, 3):
        px, py = _flip(x, y, f)
        _rcopy(sb_ref.at[2 * px + py], out_ref.at[f - 1], sems, f - 1, (px, py, c)).start()


def _chips_finish(sb_ref, out_ref, sems):
    x, y, c = _me()
    for f in (1, 2, 3):
        _rcopy(sb_ref.at[0], out_ref.at[f - 1], sems, f - 1, (x, y, c)).wait_recv()
    for f in (1, 2, 3):
        px, py = _flip(x, y, f)
        _rcopy(sb_ref.at[2 * px + py], out_ref.at[f - 1], sems, f - 1, (px, py, c)).wait_send()


def _exchange_chips(sb, small):
    def body(sb_ref, small_ref, out_ref, smalls_ref, sems, local_sem):
        x, y, c = _me()
        me = 4 * x + 2 * y + c
        mine = pltpu.make_async_copy(small_ref, smalls_ref.at[me], local_sem)
        mine.start()
        _chips_start(sb_ref, out_ref, sems)
        sends = []
        for f in range(1, 8):
            px, py = _flip(x, y, f)
            pc = 1 - c if f & 4 else c
            sends.append(_rcopy(small_ref, smalls_ref.at[me], sems, 2 + f, (px, py, pc)))
            sends[-1].start()
        _chips_finish(sb_ref, out_ref, sems)
        for f in range(1, 8):
            px, py = _flip(x, y, f)
            pc = 1 - c if f & 4 else c
            _rcopy(small_ref, smalls_ref.at[4 * px + 2 * py + pc], sems, 2 + f, (x, y, c)).wait_recv()
        for cp in sends:
            cp.wait_send()
        mine.wait()

    return _pc(body, name="grads_exchange_chips", in_specs=[HBM, HBM], out_specs=[HBM, HBM],
               out_shape=[_sds((3,) + sb.shape[1:], bf16), _sds((8, SMALL_ROWS, D_MODEL), f32)],
               scratch=[pltpu.SemaphoreType.DMA((2, 10)), pltpu.SemaphoreType.DMA])(sb, small)


def _exchange_halves(tot, name):
    rows = tot.shape[0]

    def body(tot_ref, out_ref, sems):
        x, y, c = _me()
        half = _my_half(rows, c)
        cp = _rcopy(tot_ref.at[half], out_ref.at[half], sems, 0, (x, y, 1 - c))
        cp.start()
        cp.wait()

    return _pc(body, name=name, in_specs=[HBM], out_specs=HBM, out_shape=_sds((rows, D_MODEL), f32),
               aliases={0: 0}, scratch=[pltpu.SemaphoreType.DMA((2, 1))])(tot)


def _sum_rows(half):
    return max(r for r in range(16, 513, 16) if half % r == 0)


def _sum_over_cores(gp, recv, cq, name):
    half = recv.shape[1]
    tr = _sum_rows(half)
    nb = half // tr

    def body(cq_ref, g_ref, r_ref, o32_ref, o16_ref):
        s = g_ref[...] + r_ref[...]
        o32_ref[...] = s
        o16_ref[...] = s.astype(bf16)

    spec = pl.BlockSpec((1, tr, D_MODEL), lambda b, i, cq_ref: (b, i, 0))
    gs = pltpu.PrefetchScalarGridSpec(
        num_scalar_prefetch=1, grid=(4, nb),
        in_specs=[pl.BlockSpec((1, tr, D_MODEL), lambda b, i, cq_ref: (b, cq_ref[0] * nb + i, 0)), spec],
        out_specs=[spec, spec])
    return _pc(body, name=name, grid_spec=gs, sem=("arbitrary", "arbitrary"),
               out_shape=[_sds((4, half, D_MODEL), f32), _sds((4, half, D_MODEL), bf16)])(cq, gp, recv)


def _sum_over_chips(s32, recv, cq, name):
    half = recv.shape[1]
    tr = _sum_rows(half)
    nb = half // tr

    def body(cq_ref, own_ref, r_ref, o_ref):
        o_ref[...] = ((own_ref[0] + r_ref[0].astype(f32)) + r_ref[1].astype(f32)) + r_ref[2].astype(f32)

    gs = pltpu.PrefetchScalarGridSpec(
        num_scalar_prefetch=1, grid=(nb,),
        in_specs=[pl.BlockSpec((1, tr, D_MODEL), lambda i, cq_ref: (cq_ref[1], i, 0)),
                  pl.BlockSpec((3, tr, D_MODEL), lambda i, cq_ref: (0, i, 0))],
        out_specs=pl.BlockSpec((tr, D_MODEL), lambda i, cq_ref: (cq_ref[0] * nb + i, 0)))
    return _pc(body, name=name, grid_spec=gs, sem=("arbitrary",),
               out_shape=_sds((2 * half, D_MODEL), f32))(cq, s32, recv)


def _sum_small(smalls):
    def body(s_ref, o_ref):
        tot = s_ref[0]
        for d in range(1, 8):
            tot = tot + s_ref[d]
        o_ref[...] = tot

    return _pc(body, name="small_sum", out_shape=_sds((SMALL_ROWS, D_MODEL), f32))(smalls)


def _adamw_math(w, g, m, v):
    m = ADAM_B1 * m + (1.0 - ADAM_B1) * g
    v = ADAM_B2 * v + (1.0 - ADAM_B2) * jnp.square(g)
    m_hat = m / (1.0 - ADAM_B1 ** ADAM_STEP)
    v_hat = v / (1.0 - ADAM_B2 ** ADAM_STEP)
    delta = -ADAM_LR * (m_hat / (jnp.sqrt(v_hat) + ADAM_EPS) + ADAM_WD * w)
    return delta, m, v


def _adamw(w, g, m, v, name):
    cols = w.shape[1]
    return _rw(_adamw_math, [w, g, m, v], [], [(cols, f32)] * 3, name=name, tr=256)


def _adamw_small(items):
    n = len(items)

    def body(*refs):
        ins, outs = refs[:4 * n], refs[4 * n:]
        for i in range(n):
            res = _adamw_math(*[r[...] for r in ins[4 * i:4 * i + 4]])
            for o, val in zip(outs[3 * i:3 * i + 3], res):
                o[...] = val

    flat = [a for it in items for a in it]
    out_shape = [_sds(it[0].shape, f32) for it in items for _ in range(3)]
    res = _pc(body, name="adamw_small", out_shape=out_shape)(*flat)
    return [tuple(res[3 * i:3 * i + 3]) for i in range(n)]


def _pack_shards(sh, layout, pad):
    parts = [(sh[n] if layer is None else sh[n][layer]).reshape(-1, D_MODEL).astype(bf16) for n, layer, _ in layout]
    if pad:
        parts.append(jnp.zeros((pad, D_MODEL), bf16))
    return jnp.concatenate(parts, axis=0)


def _mlp_full(g4, off, layer):
    o, r = off["mlp_w_up", layer]
    up = g4[:, o:o + r].transpose(1, 0, 2).reshape(D_MODEL, D_FF)
    o, r = off["mlp_w_down", layer]
    return {("mlp_w_up", layer): up, ("mlp_w_down", layer): g4[:, o:o + r].reshape(D_FF, D_MODEL)}


def _unpack_early(g4):
    w = _mlp_full(g4, W_EARLY_OFF, 0)
    hg = g4[:, 0:1024].reshape(4, 4, 256, D_MODEL)
    w["hgrn_w4"] = hg.transpose(0, 2, 1, 3).reshape(D_MODEL, 4 * D_MODEL)
    o, r = W_EARLY_OFF["hgrn_w_o", None]
    w["hgrn_w_o"] = g4[:, o:o + r].reshape(D_MODEL, D_MODEL)
    return w


def _unpack_late(g4):
    def rows(name):
        o, r = W_LATE_OFF[name, None]
        return g4[:, o:o + r]

    w = _mlp_full(g4, W_LATE_OFF, 1)
    w["mla_w_dq"] = rows("mla_w_dq").reshape(D_MODEL, MLA_Q_LORA)
    uq = rows("mla_w_uq").reshape(4, MLA_Q_LORA, 768).transpose(1, 0, 2).reshape(MLA_Q_LORA, MLA_HEADS, MLA_NOPE + MLA_ROPE)
    w["mla_w_qcat"] = jnp.pad(uq, ((0, 0), (0, 0), (0, ATT_QK - MLA_NOPE - MLA_ROPE))).reshape(MLA_Q_LORA, MLA_HEADS * ATT_QK)
    w["mla_w_o"] = rows("mla_w_o").reshape(MLA_HEADS * MLA_V, D_MODEL)
    dkv = rows("kv_w_dkv").reshape(D_MODEL, MLA_KV_LORA + MLA_ROPE)
    w["kv_w_dkv"] = jnp.pad(dkv, ((0, 0), (0, LANES - MLA_ROPE)))
    uk = rows("kv_w_uk").reshape(4, MLA_KV_LORA, 512).transpose(1, 0, 2).reshape(MLA_KV_LORA, MLA_HEADS, MLA_NOPE)
    w["kv_w_kcat"] = jnp.pad(uk, ((0, 0), (0, 0), (0, ATT_QK - MLA_NOPE))).reshape(MLA_KV_LORA, MLA_HEADS * ATT_QK)
    w["kv_w_uv"] = rows("kv_w_uv").reshape(4, MLA_KV_LORA, 512).transpose(1, 0, 2).reshape(MLA_KV_LORA, MLA_HEADS * MLA_V)
    return w


def _pack_grads_late(g):
    return g["hgrn_w4"].reshape(4, 256, 4, D_MODEL).transpose(0, 2, 1, 3).reshape(4, G_LATE_ROWS, D_MODEL)


def _pack_grads_early(g):
    parts = [g["hgrn_w_o"].reshape(4, 256, D_MODEL),
             g["mla_w_dq"].reshape(4, 64, D_MODEL)]
    uq = g["mla_w_qcat"].reshape(MLA_Q_LORA, MLA_HEADS, ATT_QK)[:, :, :MLA_NOPE + MLA_ROPE]
    parts.append(uq.reshape(MLA_Q_LORA, 4, 768).transpose(1, 0, 2).reshape(4, 192, D_MODEL))
    parts.append(g["mla_w_o"].reshape(4, 512, D_MODEL))
    parts.append(g["kv_w_dkv"][:, :MLA_KV_LORA + MLA_ROPE].reshape(4, 80, D_MODEL))
    uk = g["kv_w_kcat"].reshape(MLA_KV_LORA, MLA_HEADS, ATT_QK)[:, :, :MLA_NOPE]
    parts.append(uk.reshape(MLA_KV_LORA, 4, 512).transpose(1, 0, 2).reshape(4, 128, D_MODEL))
    parts.append(g["kv_w_uv"].reshape(MLA_KV_LORA, 4, 512).transpose(1, 0, 2).reshape(4, 128, D_MODEL))
    parts += [up.reshape(D_MODEL, 4, 1024).transpose(1, 0, 2) for up in g["mlp_w_up"]]
    parts += [dn.reshape(4, 1024, D_MODEL) for dn in g["mlp_w_down"]]
    parts.append(jnp.zeros((4, PACK_PAD, D_MODEL), f32))
    return jnp.concatenate(parts, axis=1)


LOSS_ROW = 11


def _pack_small(g, loss):
    rows = []
    for name, _, r, wd in SMALL:
        a = g[name].reshape(r, wd)
        rows.append(jnp.pad(a, ((0, 0), (0, D_MODEL - wd))) if wd < D_MODEL else a)
    assert sum(r for _, _, r, _ in SMALL) == LOSS_ROW
    rows.append(jnp.full((1, D_MODEL), loss, f32))
    rows.append(jnp.zeros((SMALL_ROWS - LOSS_ROW - 1, D_MODEL), f32))
    return jnp.concatenate(rows, axis=0)


def kernel(x, hgrn_norm, hgrn_w_q, hgrn_w_f, hgrn_w_i, hgrn_w_g, hgrn_g_norm, hgrn_w_o, hgrn_lb_logits, mla_norm, mla_w_dq, mla_q_norm, mla_w_uq, mla_w_o, kv_in_norm, kv_w_dkv, kv_norm, kv_w_uk, kv_w_uv, mlp_norm, mlp_w_up, mlp_w_down, final_norm, loss_target, m_hgrn_norm, m_hgrn_w_q, m_hgrn_w_f, m_hgrn_w_i, m_hgrn_w_g, m_hgrn_g_norm, m_hgrn_w_o, m_hgrn_lb_logits, m_mla_norm, m_mla_w_dq, m_mla_q_norm, m_mla_w_uq, m_mla_w_o, m_kv_in_norm, m_kv_w_dkv, m_kv_norm, m_kv_w_uk, m_kv_w_uv, m_mlp_norm, m_mlp_w_up, m_mlp_w_down, m_final_norm, v_hgrn_norm, v_hgrn_w_q, v_hgrn_w_f, v_hgrn_w_i, v_hgrn_w_g, v_hgrn_g_norm, v_hgrn_w_o, v_hgrn_lb_logits, v_mla_norm, v_mla_w_dq, v_mla_q_norm, v_mla_w_uq, v_mla_w_o, v_kv_in_norm, v_kv_w_dkv, v_kv_norm, v_kv_w_uk, v_kv_w_uv, v_mlp_norm, v_mlp_w_up, v_mlp_w_down, v_final_norm):
    given = dict(locals())
    wsh = {n: given[n] for n in WEIGHTS}
    msh = {n: given["m_" + n] for n in WEIGHTS}
    vsh = {n: given["v_" + n] for n in WEIGHTS}
    xi, yi, ci = _me()
    chip = 2 * xi + yi
    cq = jnp.stack([ci, chip]).astype(jnp.int32)

    small_w = {n: wsh[n].reshape(r, -1) for n, _, r, _ in SMALL}
    sv = jnp.concatenate([small_w["hgrn_norm"], small_w["hgrn_lb_logits"], jnp.zeros((5, 256), f32)], axis=0)
    g4, sv4 = _all_gather_weights(_pack_shards(wsh, W_EARLY, 0), sv)
    w = _unpack_early(g4)
    w["hgrn_norm"] = sv4[:, 0, :].reshape(1, D_MODEL)
    w["hgrn_lb_logits"] = sv4[:, 1:3, :].transpose(1, 0, 2).reshape(2, D_MODEL)
    for n in ("hgrn_g_norm", "mla_norm", "mla_q_norm", "kv_in_norm", "kv_norm", "mlp_norm", "final_norm"):
        w[n] = small_w[n]

    class Comm:
        late_shard = _pack_shards(wsh, W_LATE, PACK_PAD)
        unpack_late = staticmethod(_unpack_late)
        received_early = None

        @staticmethod
        def reduce_early(grads):
            gp = _pack_grads_early(grads)
            Comm.s32_early, s16 = _sum_over_cores(gp, _send_half_to_sibling(gp, "grads_to_sibling_early"), cq,
                                                  "grads_sum_cores_early")
            return s16

    loss, grad_x, g = _local_step(x.reshape(-1, D_MODEL), loss_target.reshape(-1, D_MODEL), w, Comm)

    tot_early = _exchange_halves(_sum_over_chips(Comm.s32_early, Comm.received_early, cq, "grads_sum_chips_early"),
                                 "grads_exchange_halves_early")
    gp = _pack_grads_late(g)
    s32, s16 = _sum_over_cores(gp, _send_half_to_sibling(gp, "grads_to_sibling_late"), cq, "grads_sum_cores_late")
    from_chips, smalls = _exchange_chips(s16, _pack_small(g, loss))
    tot_late = _exchange_halves(_sum_over_chips(s32, from_chips, cq, "grads_sum_chips_late"), "grads_exchange_halves_late")
    small_tot = _sum_small(smalls)
    loss = small_tot[LOSS_ROW, 0]

    grad, delta, new_m, new_v = {}, {}, {}, {}
    where = [(n, tot_late, G_LATE_OFF[n]) for n, _ in G_LATE] + [(n, tot_early, G_EARLY_OFF[n]) for n, _ in G_EARLY]
    for n, total, (o, r) in where:
        shp = wsh[n].shape
        two_d = (-1, shp[-1])
        grad[n] = total[o:o + r].reshape(shp)
        d, m2, v2 = _adamw(wsh[n].reshape(two_d), grad[n].reshape(two_d), msh[n].reshape(two_d), vsh[n].reshape(two_d),
                           "adamw_" + n)
        delta[n], new_m[n], new_v[n] = d.reshape(shp), m2.reshape(shp), v2.reshape(shp)
    items = []
    for n, row, r, wd in SMALL:
        gs = small_tot[row:row + r, :wd]
        if n in ("hgrn_norm", "hgrn_lb_logits"):
            gs = lax.dynamic_slice(gs, (0, 256 * chip), (r, 256))
        grad[n] = gs.reshape(wsh[n].shape)
        items.append((small_w[n], gs, msh[n].reshape(gs.shape), vsh[n].reshape(gs.shape)))
    for (n, _, _, _), (d, m2, v2) in zip(SMALL, _adamw_small(items)):
        shp = wsh[n].shape
        delta[n], new_m[n], new_v[n] = d.reshape(shp), m2.reshape(shp), v2.reshape(shp)

    return (loss, grad_x.reshape(x.shape), *[grad[n] for n in WEIGHTS], *[delta[n] for n in WEIGHTS],
            *[new_m[n] for n in WEIGHTS], *[new_v[n] for n in WEIGHTS])
```

```python
import functools

import jax
import jax.numpy as jnp
from jax import lax
from jax.experimental import pallas as pl
from jax.experimental.pallas import tpu as pltpu

f32, bf16 = jnp.float32, jnp.bfloat16
HI = lax.Precision.HIGHEST
MESH = pl.DeviceIdType.MESH

D_MODEL = 1024
D_FF = 4096
EPS = 1e-6
HGRN_HEADS, HGRN_DK, HGRN_CHUNK, HGRN_SUB = 8, 128, 64, 16
MLA_HEADS, MLA_NOPE, MLA_ROPE, MLA_V = 16, 128, 64, 128
MLA_Q_LORA, MLA_KV_LORA = 256, 256
ROPE_THETA = 10000.0
ATT_SCALE = (MLA_NOPE + MLA_ROPE) ** -0.5
EXP_CLAMP = 80.0

ADAM_LR, ADAM_B1, ADAM_B2, ADAM_EPS, ADAM_WD, ADAM_STEP = 0.001, 0.9, 0.999, 1e-08, 0.01, 10

V7X_VMEM_BYTES = 64 * 1024 * 1024
VMEM_LIMIT = V7X_VMEM_BYTES - 8 * 1024 * 1024
LANES = 128

PACK_PAD = 16
W_EARLY = (("hgrn_w_q", None, 256), ("hgrn_w_f", None, 256), ("hgrn_w_i", None, 256), ("hgrn_w_g", None, 256),
           ("hgrn_w_o", None, 256), ("mlp_w_up", 0, 1024), ("mlp_w_down", 0, 1024))
W_LATE = (("mla_w_dq", None, 64), ("mla_w_uq", None, 192), ("mla_w_o", None, 512), ("kv_w_dkv", None, 80),
          ("kv_w_uk", None, 128), ("kv_w_uv", None, 128), ("mlp_w_up", 1, 1024), ("mlp_w_down", 1, 1024))
G_LATE = (("hgrn_w_q", 256), ("hgrn_w_f", 256), ("hgrn_w_i", 256), ("hgrn_w_g", 256))
G_EARLY = (("hgrn_w_o", 256), ("mla_w_dq", 64), ("mla_w_uq", 192), ("mla_w_o", 512), ("kv_w_dkv", 80),
           ("kv_w_uk", 128), ("kv_w_uv", 128), ("mlp_w_up", 2048), ("mlp_w_down", 2048))


def _offsets(layout):
    out, o = {}, 0
    for entry in layout:
        out[entry[:-1] if len(entry) == 3 else entry[0]] = (o, entry[-1])
        o += entry[-1]
    return out, o


W_EARLY_OFF, W_EARLY_ROWS = _offsets(W_EARLY)
W_LATE_OFF, _w_late = _offsets(W_LATE)
W_LATE_ROWS = _w_late + PACK_PAD
G_LATE_OFF, G_LATE_ROWS = _offsets(G_LATE)
G_EARLY_OFF, _g_early = _offsets(G_EARLY)
G_EARLY_ROWS = _g_early + PACK_PAD
assert all(r % 32 == 0 for r in (W_EARLY_ROWS, W_LATE_ROWS, G_LATE_ROWS, G_EARLY_ROWS))

WEIGHTS = ("hgrn_norm", "hgrn_w_q", "hgrn_w_f", "hgrn_w_i", "hgrn_w_g", "hgrn_g_norm", "hgrn_w_o", "hgrn_lb_logits",
           "mla_norm", "mla_w_dq", "mla_q_norm", "mla_w_uq", "mla_w_o", "kv_in_norm", "kv_w_dkv", "kv_norm", "kv_w_uk",
           "kv_w_uv", "mlp_norm", "mlp_w_up", "mlp_w_down", "final_norm")
SMALL = (("hgrn_norm", 0, 1, 1024), ("hgrn_lb_logits", 1, 2, 1024), ("hgrn_g_norm", 3, 1, 128),
         ("mla_norm", 4, 1, 1024), ("mla_q_norm", 5, 1, 256), ("kv_in_norm", 6, 1, 1024), ("kv_norm", 7, 1, 256),
         ("mlp_norm", 8, 2, 1024), ("final_norm", 10, 1, 1024))
SMALL_ROWS = 16


def _pc(body, *, name, out_shape, grid=None, in_specs=None, out_specs=None, scratch=(), sem=None, grid_spec=None,
        aliases=None):
    params = pltpu.CompilerParams(dimension_semantics=sem, vmem_limit_bytes=VMEM_LIMIT)
    if grid_spec is not None:
        return pl.pallas_call(body, name=name, out_shape=out_shape, grid_spec=grid_spec, compiler_params=params,
                              interpret=False)
    kw = {k: v for k, v in (("grid", grid), ("in_specs", in_specs), ("out_specs", out_specs),
                            ("input_output_aliases", aliases)) if v is not None}
    return pl.pallas_call(body, name=name, out_shape=out_shape, scratch_shapes=list(scratch), compiler_params=params,
                          interpret=False, **kw)


def _sds(shape, dtype):
    return jax.ShapeDtypeStruct(tuple(shape), dtype)


def _mm(a, b, *, name, ta=False, tb=False, outs=(f32,), epi=None, extras=(), tm=1024, tn=1024, tk=4096):
    m, k = (a.shape[1], a.shape[0]) if ta else a.shape
    n = b.shape[0] if tb else b.shape[1]
    tm, tn, tk = min(tm, m), min(tn, n), min(tk, k)
    assert m % tm == 0 and n % tn == 0 and k % tk == 0, (name, m, n, k)
    nk = k // tk
    a_spec = pl.BlockSpec((tk, tm), lambda i, j, kk: (kk, i)) if ta else pl.BlockSpec((tm, tk), lambda i, j, kk: (i, kk))
    b_spec = pl.BlockSpec((tn, tk), lambda i, j, kk: (j, kk)) if tb else pl.BlockSpec((tk, tn), lambda i, j, kk: (kk, j))
    e_specs = [pl.BlockSpec((tm, tn), lambda i, j, kk: (i, j)) if e.shape[1] == n else
               pl.BlockSpec((tm, e.shape[1]), lambda i, j, kk: (i, 0)) for e in extras]
    dn = (((0 if ta else 1,), (1 if tb else 0,)), ((), ()))
    n_e, n_o = len(extras), len(outs)

    def finish(r, e_refs, o_refs):
        res = epi(r, *[e[...] for e in e_refs]) if epi is not None else (r,)
        for o, v in zip(o_refs, res):
            o[...] = v.astype(o.dtype)

    def body(*refs):
        a_ref, b_ref = refs[0], refs[1]
        e_refs = refs[2:2 + n_e]
        o_refs = refs[2 + n_e:2 + n_e + n_o]
        prod = lax.dot_general(a_ref[...].astype(bf16), b_ref[...].astype(bf16), dn, preferred_element_type=f32)
        if nk == 1:
            finish(prod, e_refs, o_refs)
            return
        acc = refs[-1]
        kk = pl.program_id(2)

        @pl.when(kk == 0)
        def _():
            acc[...] = jnp.zeros_like(acc)

        acc[...] += prod

        @pl.when(kk == nk - 1)
        def _():
            finish(acc[...], e_refs, o_refs)

    out = _pc(body, name=name, grid=(m // tm, n // tn, nk),
              in_specs=[a_spec, b_spec] + e_specs,
              out_specs=[pl.BlockSpec((tm, tn), lambda i, j, kk: (i, j)) for _ in outs],
              out_shape=[_sds((m, n), dt) for dt in outs],
              scratch=[pltpu.VMEM((tm, tn), f32)] if nk > 1 else [],
              sem=("parallel", "parallel", "arbitrary"))(a, b, *extras)
    return out[0] if n_o == 1 else out


def _rw(fn, rows, bcast, outs, accs=(), *, name, tr=256):
    t = rows[0].shape[0]
    tr = min(tr, t)
    assert t % tr == 0
    n_r, n_b, n_o, n_a = len(rows), len(bcast), len(outs), len(accs)

    def body(*refs):
        r_refs = refs[:n_r]
        b_refs = refs[n_r:n_r + n_b]
        o_refs = refs[n_r + n_b:n_r + n_b + n_o]
        a_refs = refs[n_r + n_b + n_o:]
        res = fn(*[r[...] for r in r_refs], *[b[...] for b in b_refs])
        for o, v in zip(o_refs, res[:n_o]):
            o[...] = v.astype(o.dtype)
        i = pl.program_id(0)
        for a_ref, v in zip(a_refs, res[n_o:]):
            @pl.when(i == 0)
            def _(a_ref=a_ref):
                a_ref[...] = jnp.zeros_like(a_ref)
            a_ref[...] += v

    in_specs = [pl.BlockSpec((tr, r.shape[1]), lambda i: (i, 0)) for r in rows]
    in_specs += [pl.BlockSpec(b.shape, lambda i: (0, 0)) for b in bcast]
    out_specs = [pl.BlockSpec((tr, w), lambda i: (i, 0)) for w, _ in outs]
    out_specs += [pl.BlockSpec(s, lambda i: (0, 0)) for s in accs]
    out_shape = [_sds((t, w), dt) for w, dt in outs] + [_sds(s, f32) for s in accs]
    res = _pc(body, name=name, grid=(t // tr,), in_specs=in_specs, out_specs=out_specs, out_shape=out_shape,
              sem=("arbitrary",))(*rows, *bcast)
    return res


def _rms(x, gain):
    return x * lax.rsqrt(jnp.mean(x * x, axis=-1, keepdims=True) + EPS) * gain


def _rms_bwd(x, gain, dy):
    _, vjp = jax.vjp(_rms, x, gain)
    return vjp(dy)


def _lower_bound(lbl):
    l0, l1 = lbl[0:1, :], lbl[1:2, :]
    mx = jnp.maximum(l0, l1)
    e0, e1 = jnp.exp(l0 - mx), jnp.exp(l1 - mx)
    return e0 / (e0 + e1)


def _gates(qpre, fpre, lbl):
    lb = _lower_bound(lbl)
    q = jax.nn.silu(qpre)
    forget = lb + (1.0 - lb) * jax.nn.sigmoid(fpre)
    return q, 1.0 - forget, jnp.log(forget)


def _head_norm_gate(o, gpre, gn):
    return _rms(o, gn) * jax.nn.silu(gpre)


def _swap_halves(x):
    w = x.shape[1]
    lane = lax.broadcasted_iota(jnp.int32, x.shape, 1)
    return jnp.where((lane % MLA_ROPE) < MLA_ROPE // 2, pltpu.roll(x, w - MLA_ROPE // 2, 1),
                     pltpu.roll(x, MLA_ROPE // 2, 1))


def _tile_lanes(tab, w):
    return tab if w == tab.shape[1] else jnp.concatenate([tab] * (w // tab.shape[1]), axis=1)


def _rope(x, cos, sgn_sin, sign=1.0):
    w = x.shape[1]
    return x * _tile_lanes(cos, w) + sign * _swap_halves(x) * _tile_lanes(sgn_sin, w)


def _bd(a, b, ca, cb):
    return lax.dot_general(a.astype(bf16), b.astype(bf16), (((ca,), (cb,)), ((), ())), preferred_element_type=f32)


@jax.custom_vjp
def _dot_nn(a, b):
    return _bd(a, b, 1, 0)


@jax.custom_vjp
def _dot_nt(a, b):
    return _bd(a, b, 1, 1)


@jax.custom_vjp
def _dot_tn(a, b):
    return _bd(a, b, 0, 0)


_dot_nn.defvjp(lambda a, b: (_bd(a, b, 1, 0), (a, b)), lambda r, g: (_bd(g, r[1], 1, 1), _bd(r[0], g, 0, 0)))
_dot_nt.defvjp(lambda a, b: (_bd(a, b, 1, 1), (a, b)), lambda r, g: (_bd(g, r[1], 1, 0), _bd(g, r[0], 0, 0)))
_dot_tn.defvjp(lambda a, b: (_bd(a, b, 0, 0), (a, b)), lambda r, g: (_bd(r[1], g, 1, 1), _bd(r[0], g, 1, 0)))


def _scan_rows(x, reverse):
    n = x.shape[0]
    row = lax.broadcasted_iota(jnp.int32, x.shape, 0)
    s = 1
    while s < n:
        if reverse:
            x = x + jnp.where(row < n - s, pltpu.roll(x, n - s, 0), 0.0)
        else:
            x = x + jnp.where(row >= s, pltpu.roll(x, s, 0), 0.0)
        s *= 2
    return x


@jax.custom_vjp
def _cumsum_rows(g):
    return _scan_rows(g, False)


_cumsum_rows.defvjp(lambda g: (_scan_rows(g, False), None), lambda _, ct: (_scan_rows(ct, True),))

HGRN_PAIRS = HGRN_HEADS // 2
HGRN_PAIR = 2 * HGRN_DK
GLA_STATE = (HGRN_PAIRS, HGRN_PAIR, HGRN_PAIR)


def _gla_consts():
    s = HGRN_SUB
    r = lax.broadcasted_iota(jnp.int32, (HGRN_PAIR, HGRN_PAIR), 0)
    c = lax.broadcasted_iota(jnp.int32, (HGRN_PAIR, HGRN_PAIR), 1)
    pair_mask = (r < HGRN_DK) == (c < HGRN_DK)
    masks = []
    for i in range(HGRN_CHUNK // s):
        n = s * (i + 1)
        row = lax.broadcasted_iota(jnp.int32, (HGRN_HEADS * s, HGRN_HEADS * n), 0)
        col = lax.broadcasted_iota(jnp.int32, (HGRN_HEADS * s, HGRN_HEADS * n), 1)
        col_head = sum((col >= m * n).astype(jnp.int32) for m in range(1, HGRN_HEADS))
        masks.append((col_head == row // s) & (col - col_head * n <= s * i + row % s))
    return pair_mask, masks


def _heads_to_rows(x):
    return jnp.concatenate([x[:, HGRN_DK * h:HGRN_DK * (h + 1)] for h in range(HGRN_HEADS)], axis=0)


def _gla_chunk(consts, dots, q, k, v, g, st):
    pair_mask, masks = consts
    dot_nn, dot_nt, dot_tn = dots
    c, s = HGRN_CHUNK, HGRN_SUB
    b = _cumsum_rows(g)
    b_last = b[c - 1:c, :]
    q_in, k_out = q * jnp.exp(b), k * jnp.exp(b_last - b)
    o_inter, st_new = [], []
    for p in range(HGRN_PAIRS):
        cols = slice(HGRN_PAIR * p, HGRN_PAIR * (p + 1))
        o_inter.append(dot_nt(q_in[:, cols], st[p]))
        st_new.append(st[p] * jnp.exp(b_last[:, cols]) + jnp.where(pair_mask, dot_tn(v[:, cols], k_out[:, cols]), 0.0))
    intra = []
    for i in range(c // s):
        n = s * (i + 1)
        ref = b[s * i - 1:s * i, :] if i else jnp.zeros_like(b_last)
        qt = _heads_to_rows(q[s * i:n] * jnp.exp(b[s * i:n] - ref))
        kt = _heads_to_rows(k[:n] * jnp.exp(jnp.minimum(ref - b[:n], EXP_CLAMP)))
        sc = jnp.where(masks[i], dot_nt(qt, kt), 0.0)
        oi = dot_nn(sc, _heads_to_rows(v[:n]))
        intra.append(jnp.concatenate([oi[s * h:s * (h + 1)] for h in range(HGRN_HEADS)], axis=1))
    return jnp.concatenate(o_inter, axis=1) + jnp.concatenate(intra, axis=0), st_new


_PLAIN_DOTS = (lambda a, b: _bd(a, b, 1, 0), lambda a, b: _bd(a, b, 1, 1), lambda a, b: _bd(a, b, 0, 0))
_VJP_DOTS = (_dot_nn, _dot_nt, _dot_tn)


def _gla_fwd(q, k, p4, g, gather=None):
    t = q.shape[0]
    nc = t // HGRN_CHUNK

    def body(q_ref, k_ref, v_ref, g_ref, *rest):
        if gather is None:
            o_ref, s_ref, st = rest
        else:
            wp_ref, _, o_ref, s_ref, gathered_ref, st, sems = rest

        @pl.when(pl.program_id(0) == 0)
        def _():
            st[...] = jnp.zeros_like(st)
            if gather is not None:
                _gather_start(wp_ref, gathered_ref, sems)

        if gather is not None:
            @pl.when(pl.program_id(0) == nc - 1)
            def _():
                _gather_finish(wp_ref, gathered_ref, sems)

        s_in = [st[p] for p in range(HGRN_PAIRS)]
        o_ref[...], st_new = _gla_chunk(_gla_consts(), _PLAIN_DOTS, q_ref[...], k_ref[...], v_ref[...], g_ref[...], s_in)
        for p in range(HGRN_PAIRS):
            s_ref[0, p] = s_in[p]
            st[p] = st_new[p]

    blk = lambda off: pl.BlockSpec((HGRN_CHUNK, D_MODEL), lambda c: (c, off))
    state_shape = GLA_STATE
    in_specs = [blk(0), blk(0), blk(2), blk(0)]
    out_specs = [blk(0), pl.BlockSpec((1,) + state_shape, lambda c: (c, 0, 0, 0))]
    out_shape = [_sds((t, D_MODEL), f32), _sds((nc,) + state_shape, f32)]
    scratch = [pltpu.VMEM(state_shape, f32)]
    if gather is None:
        return _pc(body, name="gla_fwd", grid=(nc,), in_specs=in_specs, out_specs=out_specs, out_shape=out_shape,
                   scratch=scratch, sem=("arbitrary",))(q, k, p4, g)
    return _pc(body, name="gla_fwd_gather", grid=(nc,), in_specs=in_specs + [HBM, HBM], out_specs=out_specs + [HBM],
               out_shape=out_shape + [_sds((4,) + gather.shape, bf16)], aliases={5: 2},
               scratch=scratch + [pltpu.SemaphoreType.DMA((2, 6))], sem=("arbitrary",))(
                   q, k, p4, g, gather, _gather_base(gather))


def _gla_bwd(q, k, p4, g, states, do, exchange=None):
    t = q.shape[0]
    nc = t // HGRN_CHUNK

    def body(q_ref, k_ref, v_ref, g_ref, s_ref, do_ref, *rest):
        if exchange is None:
            dq_ref, dk_ref, dv_ref, dg_ref, dst = rest
        else:
            sb_ref, dq_ref, dk_ref, dv_ref, dg_ref, recv_ref, dst, sems = rest

        @pl.when(pl.program_id(0) == 0)
        def _():
            dst[...] = jnp.zeros_like(dst)
            if exchange is not None:
                _chips_start(sb_ref, recv_ref, sems)

        if exchange is not None:
            @pl.when(pl.program_id(0) == nc - 1)
            def _():
                _chips_finish(sb_ref, recv_ref, sems)

        consts = _gla_consts()
        fn = lambda qq, kk, vv, gg, ss: _gla_chunk(consts, _VJP_DOTS, qq, kk, vv, gg, ss)
        pairs = range(HGRN_PAIRS)
        _, vjp = jax.vjp(fn, q_ref[...], k_ref[...], v_ref[...], g_ref[...], [s_ref[0, p] for p in pairs])
        dq_ref[...], dk_ref[...], dv_ref[...], dg_ref[...], ds = vjp((do_ref[...], [dst[p] for p in pairs]))
        for p in pairs:
            dst[p] = ds[p]

    blk = lambda off: pl.BlockSpec((HGRN_CHUNK, D_MODEL), lambda c: (nc - 1 - c, off))
    state_shape = GLA_STATE
    in_specs = [blk(0), blk(0), blk(2), blk(0), pl.BlockSpec((1,) + state_shape, lambda c: (nc - 1 - c, 0, 0, 0)), blk(0)]
    out_shape = [_sds((t, D_MODEL), f32)] * 4
    scratch = [pltpu.VMEM(state_shape, f32)]
    if exchange is None:
        return _pc(body, name="gla_bwd", grid=(nc,), in_specs=in_specs, out_specs=[blk(0)] * 4, out_shape=out_shape,
                   scratch=scratch, sem=("arbitrary",))(q, k, p4, g, states, do)
    return _pc(body, name="gla_bwd_exchange", grid=(nc,), in_specs=in_specs + [HBM], out_specs=[blk(0)] * 4 + [HBM],
               out_shape=out_shape + [_sds((3,) + exchange.shape[1:], bf16)],
               scratch=scratch + [pltpu.SemaphoreType.DMA((2, 3))], sem=("arbitrary",))(q, k, p4, g, states, do, exchange)


ATT_FWD_TQ, ATT_FWD_TK = 1024, 1024
ATT_BWD_TQ, ATT_BWD_TK = 1024, 512
ATT_QK = 2 * LANES
NEG = -1e30


def _pair_masks(shape):
    lane = lax.broadcasted_iota(jnp.int32, shape, 1)
    return lane < MLA_ROPE, lane >= MLA_ROPE


def _causal(shape, row0, col0):
    row = row0 + lax.broadcasted_iota(jnp.int32, shape, 0)
    col = col0 + lax.broadcasted_iota(jnp.int32, shape, 1)
    return col <= row


def _qk_cols(e):
    return slice(ATT_QK * e, ATT_QK * (e + 1))


def _v_cols(e):
    return slice(MLA_V * e, MLA_V * (e + 1))


def _attn_fwd(qc, kc, v):
    t = qc.shape[0]
    tq, tk = min(ATT_FWD_TQ, t), min(ATT_FWD_TK, t)
    npair = MLA_HEADS // 2

    def body(q_ref, k_ref, v_ref, o_ref, lse_ref):
        i = pl.program_id(1)
        n_full = (i * tq + 1) // tk
        nkv = (i * tq + tq + tk - 1) // tk
        q = [q_ref[:, _qk_cols(e)] for e in range(2)]

        def step(j, carry, masked):
            ks = pl.ds(pl.multiple_of(j * tk, tk), tk)
            ok = _causal((tq, tk), i * tq, j * tk) if masked else None
            new = []
            for e in range(2):
                m, l, acc = carry[e]
                s = _bd(q[e], k_ref[ks, _qk_cols(e)], 1, 1)
                if masked:
                    s = jnp.where(ok, s, NEG)
                m_new = jnp.maximum(m, jnp.max(s, axis=-1, keepdims=True))
                p = jnp.exp(s - m_new)
                alpha = jnp.exp(m - m_new)
                l = alpha * l + jnp.sum(p, axis=-1, keepdims=True)
                acc = alpha * acc + _bd(p, v_ref[ks, _v_cols(e)], 1, 0)
                new.append((m_new, l, acc))
            return tuple(new)

        one = (jnp.full((tq, 1), NEG, f32), jnp.zeros((tq, 1), f32), jnp.zeros((tq, MLA_V), f32))
        carry = lax.fori_loop(0, n_full, functools.partial(step, masked=False), (one, one))
        carry = lax.fori_loop(n_full, nkv, functools.partial(step, masked=True), carry)
        o_ref[...] = jnp.concatenate([acc / l for _, l, acc in carry], axis=1).astype(o_ref.dtype)
        lo, _ = _pair_masks((tq, LANES))
        lse_ref[...] = jnp.where(lo, *[m + jnp.log(l) for m, l, _ in carry])

    return _pc(body, name="attn_fwd", grid=(npair, t // tq),
               in_specs=[pl.BlockSpec((tq, 2 * ATT_QK), lambda p, i: (i, p)),
                         pl.BlockSpec((t, 2 * ATT_QK), lambda p, i: (0, p)),
                         pl.BlockSpec((t, 2 * MLA_V), lambda p, i: (0, p))],
               out_specs=[pl.BlockSpec((tq, 2 * MLA_V), lambda p, i: (i, p)),
                          pl.BlockSpec((tq, LANES), lambda p, i: (i, p))],
               out_shape=[_sds((t, MLA_HEADS * MLA_V), bf16), _sds((t, npair * LANES), f32)],
               sem=("arbitrary", "arbitrary"))(qc, kc, v)


def _attn_bwd(qc, kc, v, do, lse, delta):
    t = qc.shape[0]
    tq, tk = min(ATT_BWD_TQ, t), min(ATT_BWD_TK, t)
    npair = MLA_HEADS // 2
    nq = t // tq

    def body(q_ref, do_ref, lse_ref, dl_ref, k_ref, v_ref, dq_ref, dk_ref, dv_ref):
        j = pl.program_id(1)

        @pl.when(j == 0)
        def _():
            dq_ref[...] = jnp.zeros_like(dq_ref)

        k = [k_ref[:, _qk_cols(e)] for e in range(2)]
        vv = [v_ref[:, _v_cols(e)] for e in range(2)]

        def step(i, carry, masked):
            qs = pl.ds(pl.multiple_of(i * tq, tq), tq)
            ok = _causal((tq, tk), i * tq, j * tk) if masked else None
            lse2, dl2 = lse_ref[qs, :], dl_ref[qs, :]
            new = []
            for e in range(2):
                dk, dv = carry[e]
                q_e, do_e = q_ref[qs, _qk_cols(e)], do_ref[qs, _v_cols(e)]
                p = jnp.exp(_bd(q_e, k[e], 1, 1) - lse2[:, MLA_ROPE * e:MLA_ROPE * e + 1])
                if masked:
                    p = jnp.where(ok, p, 0.0)
                dv = dv + _bd(p, do_e, 0, 0)
                dp = _bd(do_e, vv[e], 1, 1)
                ds = (p * (dp - dl2[:, MLA_ROPE * e:MLA_ROPE * e + 1])).astype(bf16)
                dk = dk + _bd(ds, q_e, 0, 0)
                dq_ref[qs, _qk_cols(e)] += _bd(ds, k[e], 1, 0)
                new.append((dk, dv))
            return tuple(new)

        one = (jnp.zeros((tk, ATT_QK), f32), jnp.zeros((tk, MLA_V), f32))
        i_full = jnp.minimum((j * tk + tk + tq - 2) // tq, nq)
        carry = lax.fori_loop((j * tk) // tq, i_full, functools.partial(step, masked=True), (one, one))
        carry = lax.fori_loop(i_full, nq, functools.partial(step, masked=False), carry)
        for e in range(2):
            dk_ref[:, _qk_cols(e)] = carry[e][0].astype(dk_ref.dtype)
            dv_ref[:, _v_cols(e)] = carry[e][1].astype(dv_ref.dtype)

    res = lambda w: pl.BlockSpec((t, w), lambda p, j: (0, p))
    blk = lambda w: pl.BlockSpec((tk, w), lambda p, j: (j, p))
    return _pc(body, name="attn_bwd", grid=(npair, t // tk),
               in_specs=[res(2 * ATT_QK), res(2 * MLA_V), res(LANES), res(LANES), blk(2 * ATT_QK), blk(2 * MLA_V)],
               out_specs=[res(2 * ATT_QK), blk(2 * ATT_QK), blk(2 * MLA_V)],
               out_shape=[_sds((t, MLA_HEADS * ATT_QK), f32), _sds((t, MLA_HEADS * ATT_QK), bf16),
                          _sds((t, MLA_HEADS * MLA_V), bf16)],
               sem=("arbitrary", "arbitrary"))(qc, do, lse, delta, kc, v)


def _rope_tables(t):
    half = MLA_ROPE // 2
    inv_freq = ROPE_THETA ** (-jnp.arange(half, dtype=f32) / half)
    ang = jnp.arange(t, dtype=f32)[:, None] * inv_freq[None, :]
    cos, sin = jnp.cos(ang), jnp.sin(ang)
    cos128, sin128 = jnp.concatenate([cos, cos] * 2, axis=1), jnp.concatenate([-sin, sin] * 2, axis=1)
    one, zero = jnp.ones((t, MLA_NOPE), f32), jnp.zeros((t, MLA_NOPE), f32)
    return cos128, sin128, jnp.concatenate([one, cos128], axis=1), jnp.concatenate([zero, sin128], axis=1)


def _relu2_epi(u):
    r = jnp.maximum(u, 0.0)
    return u, r * r


def _add_epi(r, res):
    return (r + res,)


def _drelu2_epi(da, u):
    return (da * 2.0 * jnp.maximum(u.astype(f32), 0.0),)


def _mlp_fwd(h, gain, w_up, w_down, tag):
    xm = _rw(lambda x, g: (_rms(x, g),), [h], [gain], [(D_MODEL, bf16)], name=f"mlp{tag}_norm")[0]
    u, a = _mm(xm, w_up, name=f"mlp{tag}_up", outs=(bf16, bf16), epi=_relu2_epi)
    h_out = _mm(a, w_down, name=f"mlp{tag}_down", epi=_add_epi, extras=(h,))
    return h_out, (xm, u, a)


def _mlp_bwd(dh, dh16, h, gain, w_up, w_down, saved, tag):
    xm, u, a = saved
    du = _mm(dh16, w_down, tb=True, name=f"mlp{tag}_dact", outs=(bf16,), epi=_drelu2_epi, extras=(u,))
    d_down = _mm(a, dh16, ta=True, name=f"mlp{tag}_dwdown")
    d_up = _mm(xm, du, ta=True, name=f"mlp{tag}_dwup")
    dxm = _mm(du, w_up, tb=True, name=f"mlp{tag}_dxm")

    def fn(x, dy, dres, g):
        dx, dg = _rms_bwd(x, g, dy)
        return dx + dres, dx + dres, dg

    dh_in, dh_in16, d_gain = _rw(fn, [h, dxm, dh], [gain], [(D_MODEL, f32), (D_MODEL, bf16)], [(1, D_MODEL)],
                                 name=f"mlp{tag}_dnorm")
    return dh_in, dh_in16, d_gain, d_up, d_down


def _local_step(x, target, w, comm=None):
    w = dict(w)
    t = x.shape[0]
    cos, sgn_sin, cos_qk, sin_qk = _rope_tables(t)
    grads = {}

    xn0 = _rw(lambda xx, g: (_rms(xx, g),), [x], [w["hgrn_norm"]], [(D_MODEL, bf16)], name="hgrn_norm")[0]
    p4 = _mm(xn0, w["hgrn_w4"], name="hgrn_proj", tn=2048)

    def gates_fn(p, lbl):
        return _gates(p[:, :D_MODEL], p[:, D_MODEL:2 * D_MODEL], lbl)

    q, k, g = _rw(gates_fn, [p4], [w["hgrn_lb_logits"]], [(D_MODEL, f32)] * 3, name="hgrn_gates")
    if comm is None:
        o, states = _gla_fwd(q, k, p4, g)
    else:
        o, states, gathered = _gla_fwd(q, k, p4, g, gather=comm.late_shard)
        w.update(comm.unpack_late(gathered))

    def hn_fn(oo, p, gn):
        ys = [_head_norm_gate(oo[:, LANES * h:LANES * (h + 1)], p[:, 3 * D_MODEL + LANES * h:3 * D_MODEL + LANES * (h + 1)], gn)
              for h in range(HGRN_HEADS)]
        return (jnp.concatenate(ys, axis=1),)

    y = _rw(hn_fn, [o, p4], [w["hgrn_g_norm"]], [(D_MODEL, bf16)], name="hgrn_headnorm")[0]
    h1 = _mm(y, w["hgrn_w_o"], name="hgrn_out", epi=_add_epi, extras=(x,))
    h2, mlp0 = _mlp_fwd(h1, w["mlp_norm"][0:1], w["mlp_w_up", 0], w["mlp_w_down", 0], 0)

    hk, xn1 = _rw(lambda hh, g1, g2: (_rms(hh, g1), _rms(hh, g2)), [h2], [w["kv_in_norm"], w["mla_norm"]],
                  [(D_MODEL, bf16)] * 2, name="kv_mla_norm")
    ckr = _mm(hk, w["kv_w_dkv"], name="kv_down")

    def ckv_fn(c, cs, sn, g):
        kr = _rope(c[:, MLA_KV_LORA:], cs, sn)
        return _rms(c[:, :MLA_KV_LORA], g), jnp.concatenate([jnp.zeros_like(kr), kr], axis=1)

    c_kv, kr_head = _rw(ckv_fn, [ckr, cos, sgn_sin], [w["kv_norm"]], [(MLA_KV_LORA, bf16), (ATT_QK, f32)],
                        name="kv_norm_rope")
    kc = _mm(c_kv, w["kv_w_kcat"], name="kv_up_k", outs=(bf16,), extras=(kr_head,),
             epi=lambda r, kr: (r + _tile_lanes(kr, r.shape[1]),))
    v_att = _mm(c_kv, w["kv_w_uv"], name="kv_up_v", outs=(bf16,))
    cq0 = _mm(xn1, w["mla_w_dq"], name="q_down")
    c_q = _rw(lambda c, g: (_rms(c, g),), [cq0], [w["mla_q_norm"]], [(MLA_Q_LORA, bf16)], name="q_norm")[0]
    qc = _mm(c_q, w["mla_w_qcat"], name="q_up", outs=(bf16,), extras=(cos_qk, sin_qk),
             epi=lambda r, cs, sn: (_rope(r, cs, sn) * ATT_SCALE,))
    o_att, lse = _attn_fwd(qc, kc, v_att)
    h3 = _mm(o_att, w["mla_w_o"], name="mla_out", epi=_add_epi, extras=(h2,))
    h4, mlp1 = _mlp_fwd(h3, w["mlp_norm"][1:2], w["mlp_w_up", 1], w["mlp_w_down", 1], 1)

    def loss_fn(hh, tgt, gain):
        def f(a, b):
            e = _rms(a, b) - tgt
            return 0.5 * jnp.sum(jnp.sum(e * e, axis=-1, keepdims=True) / D_MODEL, axis=0, keepdims=True)
        val, vjp = jax.vjp(f, hh, gain)
        dh, dg = vjp(jnp.ones((1, 1), f32))
        return dh, dh, jnp.broadcast_to(val, (1, LANES)), dg

    dh4, dh4_16, loss_acc, grads["final_norm"] = _rw(loss_fn, [h4, target], [w["final_norm"]],
                                                     [(D_MODEL, f32), (D_MODEL, bf16)], [(1, LANES), (1, D_MODEL)],
                                                     name="loss")
    loss = loss_acc[0, 0]

    dh3, dh3_16, g_n1, g_up1, g_dn1 = _mlp_bwd(dh4, dh4_16, h3, w["mlp_norm"][1:2], w["mlp_w_up", 1],
                                               w["mlp_w_down", 1], mlp1, 1)
    do_att = _mm(dh3_16, w["mla_w_o"], tb=True, name="mla_dout", outs=(bf16,))
    grads["mla_w_o"] = _mm(o_att, dh3_16, ta=True, name="mla_dwo")

    def delta_fn(a, b):
        prod = a.astype(f32) * b.astype(f32)
        outs = []
        for p in range(MLA_HEADS // 2):
            d0 = jnp.sum(prod[:, 2 * p * LANES:(2 * p + 1) * LANES], axis=-1, keepdims=True)
            d1 = jnp.sum(prod[:, (2 * p + 1) * LANES:(2 * p + 2) * LANES], axis=-1, keepdims=True)
            lo, _ = _pair_masks((a.shape[0], LANES))
            outs.append(jnp.where(lo, d0, d1))
        return (jnp.concatenate(outs, axis=1),)

    delta = _rw(delta_fn, [do_att, o_att], [], [(MLA_HEADS // 2 * LANES, f32)], name="attn_delta")[0]
    dqc, dkc, dv = _attn_bwd(qc, kc, v_att, do_att, lse, delta)
    dqf = _rw(lambda a, cs, sn: (_rope(a, cs, sn, -1.0) * ATT_SCALE,), [dqc, cos_qk, sin_qk], [],
              [(MLA_HEADS * ATT_QK, bf16)], name="dq_rope")[0]
    dc_q = _mm(dqf, w["mla_w_qcat"], tb=True, name="q_up_dx")
    grads["mla_w_qcat"] = _mm(c_q, dqf, ta=True, name="q_up_dw")

    def dqn_fn(c, dy, g):
        return _rms_bwd(c, g, dy)

    dcq0, grads["mla_q_norm"] = _rw(dqn_fn, [cq0, dc_q], [w["mla_q_norm"]], [(MLA_Q_LORA, bf16)], [(1, MLA_Q_LORA)],
                                    name="q_dnorm")
    dxn1 = _mm(dcq0, w["mla_w_dq"], tb=True, name="q_down_dx")
    grads["mla_w_dq"] = _mm(xn1, dcq0, ta=True, name="q_down_dw")

    dc_kv = _mm(dkc, w["kv_w_kcat"], tb=True, name="kv_up_dx_k")
    dc_kv = _mm(dv, w["kv_w_uv"], tb=True, name="kv_up_dx_v", epi=_add_epi, extras=(dc_kv,))
    grads["kv_w_kcat"] = _mm(c_kv, dkc, ta=True, name="kv_up_dw_k")
    grads["kv_w_uv"] = _mm(c_kv, dv, ta=True, name="kv_up_dw_v")

    def dckr_fn(c, dc, dk_heads, cs, sn, g):
        tot = dk_heads[:, LANES:ATT_QK].astype(f32)
        for h in range(1, MLA_HEADS):
            tot = tot + dk_heads[:, ATT_QK * h + LANES:ATT_QK * (h + 1)].astype(f32)
        lo, _ = _pair_masks(tot.shape)
        dkr = jnp.where(lo, _rope(tot, cs, sn, -1.0), 0.0)
        dcc, dg = _rms_bwd(c[:, :MLA_KV_LORA], g, dc)
        return jnp.concatenate([dcc, dkr], axis=1), dg

    dckr, grads["kv_norm"] = _rw(dckr_fn, [ckr, dc_kv, dkc, cos, sgn_sin], [w["kv_norm"]],
                                 [(MLA_KV_LORA + LANES, bf16)], [(1, MLA_KV_LORA)], name="kv_dnorm_rope")
    dhk = _mm(dckr, w["kv_w_dkv"], tb=True, name="kv_down_dx")
    grads["kv_w_dkv"] = _mm(hk, dckr, ta=True, name="kv_down_dw")

    def dh2_fn(hh, d1, d2, dres, g1, g2):
        a, ga = _rms_bwd(hh, g1, d1)
        b, gb = _rms_bwd(hh, g2, d2)
        return a + b + dres, a + b + dres, ga, gb

    dh2, dh2_16, grads["kv_in_norm"], grads["mla_norm"] = _rw(
        dh2_fn, [h2, dhk, dxn1, dh3], [w["kv_in_norm"], w["mla_norm"]], [(D_MODEL, f32), (D_MODEL, bf16)],
        [(1, D_MODEL)] * 2, name="kv_mla_dnorm")

    dh1, dh1_16, g_n0, g_up0, g_dn0 = _mlp_bwd(dh2, dh2_16, h1, w["mlp_norm"][0:1], w["mlp_w_up", 0],
                                               w["mlp_w_down", 0], mlp0, 0)
    grads["mlp_norm"] = jnp.concatenate([g_n0, g_n1], axis=0)
    grads["mlp_w_up"] = (g_up0, g_up1)
    grads["mlp_w_down"] = (g_dn0, g_dn1)
    dy = _mm(dh1_16, w["hgrn_w_o"], tb=True, name="hgrn_dout")
    grads["hgrn_w_o"] = _mm(y, dh1_16, ta=True, name="hgrn_dwo")

    def dhn_fn(oo, p, dyy, gn):
        dos, dgs, dgn = [], [], jnp.zeros_like(gn)
        for h in range(HGRN_HEADS):
            cols = slice(LANES * h, LANES * (h + 1))
            _, vjp = jax.vjp(_head_norm_gate, oo[:, cols], p[:, 3 * D_MODEL + LANES * h:3 * D_MODEL + LANES * (h + 1)], gn)
            a, b, c = vjp(dyy[:, cols])
            dos.append(a)
            dgs.append(b)
            dgn = dgn + c
        return jnp.concatenate(dos, axis=1), jnp.concatenate(dgs, axis=1), dgn

    do, dgate, grads["hgrn_g_norm"] = _rw(dhn_fn, [o, p4, dy], [w["hgrn_g_norm"]], [(D_MODEL, f32)] * 2, [(1, HGRN_DK)],
                                          name="hgrn_dheadnorm")
    if comm is None:
        dq, dk, dv_h, dg = _gla_bwd(q, k, p4, g, states, do)
    else:
        dq, dk, dv_h, dg, comm.received_early = _gla_bwd(q, k, p4, g, states, do, exchange=comm.reduce_early(grads))

    def dgates_fn(p, dqq, dkk, dgg, dvv, dgt, lbl):
        _, vjp = jax.vjp(_gates, p[:, :D_MODEL], p[:, D_MODEL:2 * D_MODEL], lbl)
        dqp, dfp, dlbl = vjp((dqq, dkk, dgg))
        return jnp.concatenate([dqp, dfp, dvv, dgt], axis=1), dlbl

    dp4, grads["hgrn_lb_logits"] = _rw(dgates_fn, [p4, dq, dk, dg, dv_h, dgate], [w["hgrn_lb_logits"]],
                                       [(4 * D_MODEL, bf16)], [(2, D_MODEL)], name="hgrn_dgates")
    dxn0 = _mm(dp4, w["hgrn_w4"], tb=True, name="hgrn_proj_dx")
    grads["hgrn_w4"] = _mm(xn0, dp4, ta=True, name="hgrn_proj_dw")

    def dx_fn(xx, dyy, dres, gn):
        dxx, dgn = _rms_bwd(xx, gn, dyy)
        return dxx + dres, dgn

    grad_x, grads["hgrn_norm"] = _rw(dx_fn, [x, dxn0, dh1], [w["hgrn_norm"]], [(D_MODEL, f32)], [(1, D_MODEL)],
                                     name="hgrn_dnorm")
    return loss, grad_x, grads


HBM = pl.BlockSpec(memory_space=pltpu.HBM)


def _me():
    return lax.axis_index("x"), lax.axis_index("y"), lax.axis_index("c")


def _flip(x, y, f):
    return (1 - x if f & 1 else x), (1 - y if f & 2 else y)


def _rcopy(src, dst, sems, k, dev):
    return pltpu.make_async_remote_copy(src_ref=src, dst_ref=dst, send_sem=sems.at[0, k], recv_sem=sems.at[1, k],
                                        device_id=dev, device_id_type=MESH)


def _my_half(rows, c, mine=True):
    half = rows // 2
    return pl.ds(pl.multiple_of((c if mine else 1 - c) * half, 16), half)


def _gather_start(wp_ref, out_ref, sems):
    x, y, c = _me()
    half = _my_half(wp_ref.shape[0], c)
    for f in (1, 2, 3):
        px, py = _flip(x, y, f)
        _rcopy(wp_ref.at[half], out_ref.at[2 * x + y, half], sems, f - 1, (px, py, c)).start()


def _gather_finish(wp_ref, out_ref, sems):
    x, y, c = _me()
    half, other = _my_half(wp_ref.shape[0], c), _my_half(wp_ref.shape[0], c, mine=False)
    sends = []
    for f in (1, 2, 3):
        px, py = _flip(x, y, f)
        landed = out_ref.at[2 * px + py, half]
        _rcopy(landed, landed, sems, f - 1, (px, py, c)).wait_recv()
        sends.append(_rcopy(landed, landed, sems, 2 + f, (x, y, 1 - c)))
        sends[-1].start()
    for f in (1, 2, 3):
        px, py = _flip(x, y, f)
        theirs = out_ref.at[2 * px + py, other]
        _rcopy(theirs, theirs, sems, 2 + f, (x, y, 1 - c)).wait_recv()
        sends.append(_rcopy(wp_ref.at[half], out_ref.at[2 * x + y, half], sems, f - 1, (px, py, c)))
    for cp in sends:
        cp.wait_send()


def _gather_base(wp):
    return jnp.broadcast_to(wp[None], (4,) + wp.shape)


def _all_gather_weights(wp, sv):
    def body(wp_ref, sv_ref, base_ref, out_ref, svs_ref, sems, local_sem):
        x, y, c = _me()
        mine = pltpu.make_async_copy(sv_ref, svs_ref.at[2 * x + y], local_sem)
        mine.start()
        _gather_start(wp_ref, out_ref, sems)
        small = []
        for f in (1, 2, 3):
            px, py = _flip(x, y, f)
            small.append(_rcopy(sv_ref, svs_ref.at[2 * x + y], sems, 5 + f, (px, py, c)))
            small[-1].start()
        _gather_finish(wp_ref, out_ref, sems)
        for f in (1, 2, 3):
            px, py = _flip(x, y, f)
            _rcopy(sv_ref, svs_ref.at[2 * px + py], sems, 5 + f, (px, py, c)).wait_recv()
        for cp in small:
            cp.wait_send()
        mine.wait()

    return _pc(body, name="weights_all_gather", in_specs=[HBM, HBM, HBM], out_specs=[HBM, HBM],
               out_shape=[_sds((4,) + wp.shape, bf16), _sds((4, 8, 256), f32)], aliases={2: 0},
               scratch=[pltpu.SemaphoreType.DMA((2, 9)), pltpu.SemaphoreType.DMA])(wp, sv, _gather_base(wp))


def _send_half_to_sibling(gp, name):
    rows = gp.shape[1]

    def body(gp_ref, out_ref, sems):
        x, y, c = _me()
        cp = _rcopy(gp_ref.at[:, _my_half(rows, c, mine=False)], out_ref, sems, 0, (x, y, 1 - c))
        cp.start()
        cp.wait()

    return _pc(body, name=name, in_specs=[HBM], out_specs=HBM, out_shape=_sds((4, rows // 2, D_MODEL), f32),
               scratch=[pltpu.SemaphoreType.DMA((2, 1))])(gp)


def _chips_start(sb_ref, out_ref, sems):
    x, y, c = _me()
    for f in (1, 2, 3):
        px, py = _flip(x, y, f)
        _rcopy(sb_ref.at[2 * px + py], out_ref.at[f - 1], sems, f - 1, (px, py, c)).start()


def _chips_finish(sb_ref, out_ref, sems):
    x, y, c = _me()
    for f in (1, 2, 3):
        _rcopy(sb_ref.at[0], out_ref.at[f - 1], sems, f - 1, (x, y, c)).wait_recv()
    for f in (1, 2, 3):
        px, py = _flip(x, y, f)
        _rcopy(sb_ref.at[2 * px + py], out_ref.at[f - 1], sems, f - 1, (px, py, c)).wait_send()


def _exchange_chips(sb, small):
    def body(sb_ref, small_ref, out_ref, smalls_ref, sems, local_sem):
        x, y, c = _me()
        me = 4 * x + 2 * y + c
        mine = pltpu.make_async_copy(small_ref, smalls_ref.at[me], local_sem)
        mine.start()
        _chips_start(sb_ref, out_ref, sems)
        sends = []
        for f in range(1, 8):
            px, py = _flip(x, y, f)
            pc = 1 - c if f & 4 else c
            sends.append(_rcopy(small_ref, smalls_ref.at[me], sems, 2 + f, (px, py, pc)))
            sends[-1].start()
        _chips_finish(sb_ref, out_ref, sems)
        for f in range(1, 8):
            px, py = _flip(x, y, f)
            pc = 1 - c if f & 4 else c
            _rcopy(small_ref, smalls_ref.at[4 * px + 2 * py + pc], sems, 2 + f, (x, y, c)).wait_recv()
        for cp in sends:
            cp.wait_send()
        mine.wait()

    return _pc(body, name="grads_exchange_chips", in_specs=[HBM, HBM], out_specs=[HBM, HBM],
               out_shape=[_sds((3,) + sb.shape[1:], bf16), _sds((8, SMALL_ROWS, D_MODEL), f32)],
               scratch=[pltpu.SemaphoreType.DMA((2, 10)), pltpu.SemaphoreType.DMA])(sb, small)


def _exchange_halves(tot, name):
    rows = tot.shape[0]

    def body(tot_ref, out_ref, sems):
        x, y, c = _me()
        half = _my_half(rows, c)
        cp = _rcopy(tot_ref.at[half], out_ref.at[half], sems, 0, (x, y, 1 - c))
        cp.start()
        cp.wait()

    return _pc(body, name=name, in_specs=[HBM], out_specs=HBM, out_shape=_sds((rows, D_MODEL), f32),
               aliases={0: 0}, scratch=[pltpu.SemaphoreType.DMA((2, 1))])(tot)


def _sum_rows(half):
    return max(r for r in range(16, 513, 16) if half % r == 0)


def _sum_over_cores(gp, recv, cq, name):
    half = recv.shape[1]
    tr = _sum_rows(half)
    nb = half // tr

    def body(cq_ref, g_ref, r_ref, o32_ref, o16_ref):
        s = g_ref[...] + r_ref[...]
        o32_ref[...] = s
        o16_ref[...] = s.astype(bf16)

    spec = pl.BlockSpec((1, tr, D_MODEL), lambda b, i, cq_ref: (b, i, 0))
    gs = pltpu.PrefetchScalarGridSpec(
        num_scalar_prefetch=1, grid=(4, nb),
        in_specs=[pl.BlockSpec((1, tr, D_MODEL), lambda b, i, cq_ref: (b, cq_ref[0] * nb + i, 0)), spec],
        out_specs=[spec, spec])
    return _pc(body, name=name, grid_spec=gs, sem=("arbitrary", "arbitrary"),
               out_shape=[_sds((4, half, D_MODEL), f32), _sds((4, half, D_MODEL), bf16)])(cq, gp, recv)


def _sum_over_chips(s32, recv, cq, name):
    half = recv.shape[1]
    tr = _sum_rows(half)
    nb = half // tr

    def body(cq_ref, own_ref, r_ref, o_ref):
        o_ref[...] = ((own_ref[0] + r_ref[0].astype(f32)) + r_ref[1].astype(f32)) + r_ref[2].astype(f32)

    gs = pltpu.PrefetchScalarGridSpec(
        num_scalar_prefetch=1, grid=(nb,),
        in_specs=[pl.BlockSpec((1, tr, D_MODEL), lambda i, cq_ref: (cq_ref[1], i, 0)),
                  pl.BlockSpec((3, tr, D_MODEL), lambda i, cq_ref: (0, i, 0))],
        out_specs=pl.BlockSpec((tr, D_MODEL), lambda i, cq_ref: (cq_ref[0] * nb + i, 0)))
    return _pc(body, name=name, grid_spec=gs, sem=("arbitrary",),
               out_shape=_sds((2 * half, D_MODEL), f32))(cq, s32, recv)


def _sum_small(smalls):
    def body(s_ref, o_ref):
        tot = s_ref[0]
        for d in range(1, 8):
            tot = tot + s_ref[d]
        o_ref[...] = tot

    return _pc(body, name="small_sum", out_shape=_sds((SMALL_ROWS, D_MODEL), f32))(smalls)


def _adamw_math(w, g, m, v):
    m = ADAM_B1 * m + (1.0 - ADAM_B1) * g
    v = ADAM_B2 * v + (1.0 - ADAM_B2) * jnp.square(g)
    m_hat = m / (1.0 - ADAM_B1 ** ADAM_STEP)
    v_hat = v / (1.0 - ADAM_B2 ** ADAM_STEP)
    delta = -ADAM_LR * (m_hat / (jnp.sqrt(v_hat) + ADAM_EPS) + ADAM_WD * w)
    return delta, m, v


def _adamw(w, g, m, v, name):
    cols = w.shape[1]
    return _rw(_adamw_math, [w, g, m, v], [], [(cols, f32)] * 3, name=name, tr=256)


def _adamw_small(items):
    n = len(items)

    def body(*refs):
        ins, outs = refs[:4 * n], refs[4 * n:]
        for i in range(n):
            res = _adamw_math(*[r[...] for r in ins[4 * i:4 * i + 4]])
            for o, val in zip(outs[3 * i:3 * i + 3], res):
                o[...] = val

    flat = [a for it in items for a in it]
    out_shape = [_sds(it[0].shape, f32) for it in items for _ in range(3)]
    res = _pc(body, name="adamw_small", out_shape=out_shape)(*flat)
    return [tuple(res[3 * i:3 * i + 3]) for i in range(n)]


def _pack_shards(sh, layout, pad):
    parts = [(sh[n] if layer is None else sh[n][layer]).reshape(-1, D_MODEL).astype(bf16) for n, layer, _ in layout]
    if pad:
        parts.append(jnp.zeros((pad, D_MODEL), bf16))
    return jnp.concatenate(parts, axis=0)


def _mlp_full(g4, off, layer):
    o, r = off["mlp_w_up", layer]
    up = g4[:, o:o + r].transpose(1, 0, 2).reshape(D_MODEL, D_FF)
    o, r = off["mlp_w_down", layer]
    return {("mlp_w_up", layer): up, ("mlp_w_down", layer): g4[:, o:o + r].reshape(D_FF, D_MODEL)}


def _unpack_early(g4):
    w = _mlp_full(g4, W_EARLY_OFF, 0)
    hg = g4[:, 0:1024].reshape(4, 4, 256, D_MODEL)
    w["hgrn_w4"] = hg.transpose(0, 2, 1, 3).reshape(D_MODEL, 4 * D_MODEL)
    o, r = W_EARLY_OFF["hgrn_w_o", None]
    w["hgrn_w_o"] = g4[:, o:o + r].reshape(D_MODEL, D_MODEL)
    return w


def _unpack_late(g4):
    def rows(name):
        o, r = W_LATE_OFF[name, None]
        return g4[:, o:o + r]

    w = _mlp_full(g4, W_LATE_OFF, 1)
    w["mla_w_dq"] = rows("mla_w_dq").reshape(D_MODEL, MLA_Q_LORA)
    uq = rows("mla_w_uq").reshape(4, MLA_Q_LORA, 768).transpose(1, 0, 2).reshape(MLA_Q_LORA, MLA_HEADS, MLA_NOPE + MLA_ROPE)
    w["mla_w_qcat"] = jnp.pad(uq, ((0, 0), (0, 0), (0, ATT_QK - MLA_NOPE - MLA_ROPE))).reshape(MLA_Q_LORA, MLA_HEADS * ATT_QK)
    w["mla_w_o"] = rows("mla_w_o").reshape(MLA_HEADS * MLA_V, D_MODEL)
    dkv = rows("kv_w_dkv").reshape(D_MODEL, MLA_KV_LORA + MLA_ROPE)
    w["kv_w_dkv"] = jnp.pad(dkv, ((0, 0), (0, LANES - MLA_ROPE)))
    uk = rows("kv_w_uk").reshape(4, MLA_KV_LORA, 512).transpose(1, 0, 2).reshape(MLA_KV_LORA, MLA_HEADS, MLA_NOPE)
    w["kv_w_kcat"] = jnp.pad(uk, ((0, 0), (0, 0), (0, ATT_QK - MLA_NOPE))).reshape(MLA_KV_LORA, MLA_HEADS * ATT_QK)
    w["kv_w_uv"] = rows("kv_w_uv").reshape(4, MLA_KV_LORA, 512).transpose(1, 0, 2).reshape(MLA_KV_LORA, MLA_HEADS * MLA_V)
    return w


def _pack_grads_late(g):
    return g["hgrn_w4"].reshape(4, 256, 4, D_MODEL).transpose(0, 2, 1, 3).reshape(4, G_LATE_ROWS, D_MODEL)


def _pack_grads_early(g):
    parts = [g["hgrn_w_o"].reshape(4, 256, D_MODEL),
             g["mla_w_dq"].reshape(4, 64, D_MODEL)]
    uq = g["mla_w_qcat"].reshape(MLA_Q_LORA, MLA_HEADS, ATT_QK)[:, :, :MLA_NOPE + MLA_ROPE]
    parts.append(uq.reshape(MLA_Q_LORA, 4, 768).transpose(1, 0, 2).reshape(4, 192, D_MODEL))
    parts.append(g["mla_w_o"].reshape(4, 512, D_MODEL))
    parts.append(g["kv_w_dkv"][:, :MLA_KV_LORA + MLA_ROPE].reshape(4, 80, D_MODEL))
    uk = g["kv_w_kcat"].reshape(MLA_KV_LORA, MLA_HEADS, ATT_QK)[:, :, :MLA_NOPE]
    parts.append(uk.reshape(MLA_KV_LORA, 4, 512).transpose(1, 0, 2).reshape(4, 128, D_MODEL))
    parts.append(g["kv_w_uv"].reshape(MLA_KV_LORA, 4, 512).transpose(1, 0, 2).reshape(4, 128, D_MODEL))
    parts += [up.reshape(D_MODEL, 4, 1024).transpose(1, 0, 2) for up in g["mlp_w_up"]]
    parts += [dn.reshape(4, 1024, D_MODEL) for dn in g["mlp_w_down"]]
    parts.append(jnp.zeros((4, PACK_PAD, D_MODEL), f32))
    return jnp.concatenate(parts, axis=1)


LOSS_ROW = 11


def _pack_small(g, loss):
    rows = []
    for name, _, r, wd in SMALL:
        a = g[name].reshape(r, wd)
        rows.append(jnp.pad(a, ((0, 0), (0, D_MODEL - wd))) if wd < D_MODEL else a)
    assert sum(r for _, _, r, _ in SMALL) == LOSS_ROW
    rows.append(jnp.full((1, D_MODEL), loss, f32))
    rows.append(jnp.zeros((SMALL_ROWS - LOSS_ROW - 1, D_MODEL), f32))
    return jnp.concatenate(rows, axis=0)


def kernel(x, hgrn_norm, hgrn_w_q, hgrn_w_f, hgrn_w_i, hgrn_w_g, hgrn_g_norm, hgrn_w_o, hgrn_lb_logits, mla_norm, mla_w_dq, mla_q_norm, mla_w_uq, mla_w_o, kv_in_norm, kv_w_dkv, kv_norm, kv_w_uk, kv_w_uv, mlp_norm, mlp_w_up, mlp_w_down, final_norm, loss_target, m_hgrn_norm, m_hgrn_w_q, m_hgrn_w_f, m_hgrn_w_i, m_hgrn_w_g, m_hgrn_g_norm, m_hgrn_w_o, m_hgrn_lb_logits, m_mla_norm, m_mla_w_dq, m_mla_q_norm, m_mla_w_uq, m_mla_w_o, m_kv_in_norm, m_kv_w_dkv, m_kv_norm, m_kv_w_uk, m_kv_w_uv, m_mlp_norm, m_mlp_w_up, m_mlp_w_down, m_final_norm, v_hgrn_norm, v_hgrn_w_q, v_hgrn_w_f, v_hgrn_w_i, v_hgrn_w_g, v_hgrn_g_norm, v_hgrn_w_o, v_hgrn_lb_logits, v_mla_norm, v_mla_w_dq, v_mla_q_norm, v_mla_w_uq, v_mla_w_o, v_kv_in_norm, v_kv_w_dkv, v_kv_norm, v_kv_w_uk, v_kv_w_uv, v_mlp_norm, v_mlp_w_up, v_mlp_w_down, v_final_norm):
    given = dict(locals())
    wsh = {n: given[n] for n in WEIGHTS}
    msh = {n: given["m_" + n] for n in WEIGHTS}
    vsh = {n: given["v_" + n] for n in WEIGHTS}
    xi, yi, ci = _me()
    chip = 2 * xi + yi
    cq = jnp.stack([ci, chip]).astype(jnp.int32)

    small_w = {n: wsh[n].reshape(r, -1) for n, _, r, _ in SMALL}
    sv = jnp.concatenate([small_w["hgrn_norm"], small_w["hgrn_lb_logits"], jnp.zeros((5, 256), f32)], axis=0)
    g4, sv4 = _all_gather_weights(_pack_shards(wsh, W_EARLY, 0), sv)
    w = _unpack_early(g4)
    w["hgrn_norm"] = sv4[:, 0, :].reshape(1, D_MODEL)
    w["hgrn_lb_logits"] = sv4[:, 1:3, :].transpose(1, 0, 2).reshape(2, D_MODEL)
    for n in ("hgrn_g_norm", "mla_norm", "mla_q_norm", "kv_in_norm", "kv_norm", "mlp_norm", "final_norm"):
        w[n] = small_w[n]

    class Comm:
        late_shard = _pack_shards(wsh, W_LATE, PACK_PAD)
        unpack_late = staticmethod(_unpack_late)
        received_early = None

        @staticmethod
        def reduce_early(grads):
            gp = _pack_grads_early(grads)
            Comm.s32_early, s16 = _sum_over_cores(gp, _send_half_to_sibling(gp, "grads_to_sibling_early"), cq,
                                                  "grads_sum_cores_early")
            return s16

    loss, grad_x, g = _local_step(x.reshape(-1, D_MODEL), loss_target.reshape(-1, D_MODEL), w, Comm)

    tot_early = _exchange_halves(_sum_over_chips(Comm.s32_early, Comm.received_early, cq, "grads_sum_chips_early"),
                                 "grads_exchange_halves_early")
    gp = _pack_grads_late(g)
    s32, s16 = _sum_over_cores(gp, _send_half_to_sibling(gp, "grads_to_sibling_late"), cq, "grads_sum_cores_late")
    from_chips, smalls = _exchange_chips(s16, _pack_small(g, loss))
    tot_late = _exchange_halves(_sum_over_chips(s32, from_chips, cq, "grads_sum_chips_late"), "grads_exchange_halves_late")
    small_tot = _sum_small(smalls)
    loss = small_tot[LOSS_ROW, 0]

    grad, delta, new_m, new_v = {}, {}, {}, {}
    where = [(n, tot_late, G_LATE_OFF[n]) for n, _ in G_LATE] + [(n, tot_early, G_EARLY_OFF[n]) for n, _ in G_EARLY]
    for n, total, (o, r) in where:
        shp = wsh[n].shape
        two_d = (-1, shp[-1])
        grad[n] = total[o:o + r].reshape(shp)
        d, m2, v2 = _adamw(wsh[n].reshape(two_d), grad[n].reshape(two_d), msh[n].reshape(two_d), vsh[n].reshape(two_d),
                           "adamw_" + n)
        delta[n], new_m[n], new_v[n] = d.reshape(shp), m2.reshape(shp), v2.reshape(shp)
    items = []
    for n, row, r, wd in SMALL:
        gs = small_tot[row:row + r, :wd]
        if n in ("hgrn_norm", "hgrn_lb_logits"):
            gs = lax.dynamic_slice(gs, (0, 256 * chip), (r, 256))
        grad[n] = gs.reshape(wsh[n].shape)
        items.append((small_w[n], gs, msh[n].reshape(gs.shape), vsh[n].reshape(gs.shape)))
    for (n, _, _, _), (d, m2, v2) in zip(SMALL, _adamw_small(items)):
        shp = wsh[n].shape
        delta[n], new_m[n], new_v[n] = d.reshape(shp), m2.reshape(shp), v2.reshape(shp)

    return (loss, grad_x.reshape(x.shape), *[grad[n] for n in WEIGHTS], *[delta[n] for n in WEIGHTS],
            *[new_m[n] for n in WEIGHTS], *[new_v[n] for n in WEIGHTS])
```

```python
import functools

import jax
import jax.numpy as jnp
from jax import lax
from jax.experimental import pallas as pl
from jax.experimental.pallas import tpu as pltpu

f32, bf16 = jnp.float32, jnp.bfloat16
HI = lax.Precision.HIGHEST
MESH = pl.DeviceIdType.MESH

D_MODEL = 1024
D_FF = 4096
EPS = 1e-6
HGRN_HEADS, HGRN_DK, HGRN_CHUNK, HGRN_SUB = 8, 128, 64, 16
MLA_HEADS, MLA_NOPE, MLA_ROPE, MLA_V = 16, 128, 64, 128
MLA_Q_LORA, MLA_KV_LORA = 256, 256
ROPE_THETA = 10000.0
ATT_SCALE = (MLA_NOPE + MLA_ROPE) ** -0.5
EXP_CLAMP = 80.0

ADAM_LR, ADAM_B1, ADAM_B2, ADAM_EPS, ADAM_WD, ADAM_STEP = 0.001, 0.9, 0.999, 1e-08, 0.01, 10

V7X_VMEM_BYTES = 64 * 1024 * 1024
VMEM_LIMIT = V7X_VMEM_BYTES - 8 * 1024 * 1024
LANES = 128

PACK_PAD = 16
W_EARLY = (("hgrn_w_q", None, 256), ("hgrn_w_f", None, 256), ("hgrn_w_i", None, 256), ("hgrn_w_g", None, 256),
           ("hgrn_w_o", None, 256), ("mlp_w_up", 0, 1024), ("mlp_w_down", 0, 1024))
W_LATE = (("mla_w_dq", None, 64), ("mla_w_uq", None, 192), ("mla_w_o", None, 512), ("kv_w_dkv", None, 80),
          ("kv_w_uk", None, 128), ("kv_w_uv", None, 128), ("mlp_w_up", 1, 1024), ("mlp_w_down", 1, 1024))
G_LATE = (("hgrn_w_q", 256), ("hgrn_w_f", 256), ("hgrn_w_i", 256), ("hgrn_w_g", 256))
G_EARLY = (("hgrn_w_o", 256), ("mla_w_dq", 64), ("mla_w_uq", 192), ("mla_w_o", 512), ("kv_w_dkv", 80),
           ("kv_w_uk", 128), ("kv_w_uv", 128), ("mlp_w_up", 2048), ("mlp_w_down", 2048))


def _offsets(layout):
    out, o = {}, 0
    for entry in layout:
        out[entry[:-1] if len(entry) == 3 else entry[0]] = (o, entry[-1])
        o += entry[-1]
    return out, o


W_EARLY_OFF, W_EARLY_ROWS = _offsets(W_EARLY)
W_LATE_OFF, _w_late = _offsets(W_LATE)
W_LATE_ROWS = _w_late + PACK_PAD
G_LATE_OFF, G_LATE_ROWS = _offsets(G_LATE)
G_EARLY_OFF, _g_early = _offsets(G_EARLY)
G_EARLY_ROWS = _g_early + PACK_PAD
assert all(r % 32 == 0 for r in (W_EARLY_ROWS, W_LATE_ROWS, G_LATE_ROWS, G_EARLY_ROWS))

WEIGHTS = ("hgrn_norm", "hgrn_w_q", "hgrn_w_f", "hgrn_w_i", "hgrn_w_g", "hgrn_g_norm", "hgrn_w_o", "hgrn_lb_logits",
           "mla_norm", "mla_w_dq", "mla_q_norm", "mla_w_uq", "mla_w_o", "kv_in_norm", "kv_w_dkv", "kv_norm", "kv_w_uk",
           "kv_w_uv", "mlp_norm", "mlp_w_up", "mlp_w_down", "final_norm")
SMALL = (("hgrn_norm", 0, 1, 1024), ("hgrn_lb_logits", 1, 2, 1024), ("hgrn_g_norm", 3, 1, 128),
         ("mla_norm", 4, 1, 1024), ("mla_q_norm", 5, 1, 256), ("kv_in_norm", 6, 1, 1024), ("kv_norm", 7, 1, 256),
         ("mlp_norm", 8, 2, 1024), ("final_norm", 10, 1, 1024))
SMALL_ROWS = 16


def _pc(body, *, name, out_shape, grid=None, in_specs=None, out_specs=None, scratch=(), sem=None, grid_spec=None,
        aliases=None):
    params = pltpu.CompilerParams(dimension_semantics=sem, vmem_limit_bytes=VMEM_LIMIT)
    if grid_spec is not None:
        return pl.pallas_call(body, name=name, out_shape=out_shape, grid_spec=grid_spec, compiler_params=params,
                              interpret=False)
    kw = {k: v for k, v in (("grid", grid), ("in_specs", in_specs), ("out_specs", out_specs),
                            ("input_output_aliases", aliases)) if v is not None}
    return pl.pallas_call(body, name=name, out_shape=out_shape, scratch_shapes=list(scratch), compiler_params=params,
                          interpret=False, **kw)


def _sds(shape, dtype):
    return jax.ShapeDtypeStruct(tuple(shape), dtype)


def _mm(a, b, *, name, ta=False, tb=False, outs=(f32,), epi=None, extras=(), tm=1024, tn=1024, tk=4096):
    m, k = (a.shape[1], a.shape[0]) if ta else a.shape
    n = b.shape[0] if tb else b.shape[1]
    tm, tn, tk = min(tm, m), min(tn, n), min(tk, k)
    assert m % tm == 0 and n % tn == 0 and k % tk == 0, (name, m, n, k)
    nk = k // tk
    a_spec = pl.BlockSpec((tk, tm), lambda i, j, kk: (kk, i)) if ta else pl.BlockSpec((tm, tk), lambda i, j, kk: (i, kk))
    b_spec = pl.BlockSpec((tn, tk), lambda i, j, kk: (j, kk)) if tb else pl.BlockSpec((tk, tn), lambda i, j, kk: (kk, j))
    e_specs = [pl.BlockSpec((tm, tn), lambda i, j, kk: (i, j)) if e.shape[1] == n else
               pl.BlockSpec((tm, e.shape[1]), lambda i, j, kk: (i, 0)) for e in extras]
    dn = (((0 if ta else 1,), (1 if tb else 0,)), ((), ()))
    n_e, n_o = len(extras), len(outs)

    def finish(r, e_refs, o_refs):
        res = epi(r, *[e[...] for e in e_refs]) if epi is not None else (r,)
        for o, v in zip(o_refs, res):
            o[...] = v.astype(o.dtype)

    def body(*refs):
        a_ref, b_ref = refs[0], refs[1]
        e_refs = refs[2:2 + n_e]
        o_refs = refs[2 + n_e:2 + n_e + n_o]
        prod = lax.dot_general(a_ref[...].astype(bf16), b_ref[...].astype(bf16), dn, preferred_element_type=f32)
        if nk == 1:
            finish(prod, e_refs, o_refs)
            return
        acc = refs[-1]
        kk = pl.program_id(2)

        @pl.when(kk == 0)
        def _():
            acc[...] = jnp.zeros_like(acc)

        acc[...] += prod

        @pl.when(kk == nk - 1)
        def _():
            finish(acc[...], e_refs, o_refs)

    out = _pc(body, name=name, grid=(m // tm, n // tn, nk),
              in_specs=[a_spec, b_spec] + e_specs,
              out_specs=[pl.BlockSpec((tm, tn), lambda i, j, kk: (i, j)) for _ in outs],
              out_shape=[_sds((m, n), dt) for dt in outs],
              scratch=[pltpu.VMEM((tm, tn), f32)] if nk > 1 else [],
              sem=("parallel", "parallel", "arbitrary"))(a, b, *extras)
    return out[0] if n_o == 1 else out


def _rw(fn, rows, bcast, outs, accs=(), *, name, tr=256):
    t = rows[0].shape[0]
    tr = min(tr, t)
    assert t % tr == 0
    n_r, n_b, n_o, n_a = len(rows), len(bcast), len(outs), len(accs)

    def body(*refs):
        r_refs = refs[:n_r]
        b_refs = refs[n_r:n_r + n_b]
        o_refs = refs[n_r + n_b:n_r + n_b + n_o]
        a_refs = refs[n_r + n_b + n_o:]
        res = fn(*[r[...] for r in r_refs], *[b[...] for b in b_refs])
        for o, v in zip(o_refs, res[:n_o]):
            o[...] = v.astype(o.dtype)
        i = pl.program_id(0)
        for a_ref, v in zip(a_refs, res[n_o:]):
            @pl.when(i == 0)
            def _(a_ref=a_ref):
                a_ref[...] = jnp.zeros_like(a_ref)
            a_ref[...] += v

    in_specs = [pl.BlockSpec((tr, r.shape[1]), lambda i: (i, 0)) for r in rows]
    in_specs += [pl.BlockSpec(b.shape, lambda i: (0, 0)) for b in bcast]
    out_specs = [pl.BlockSpec((tr, w), lambda i: (i, 0)) for w, _ in outs]
    out_specs += [pl.BlockSpec(s, lambda i: (0, 0)) for s in accs]
    out_shape = [_sds((t, w), dt) for w, dt in outs] + [_sds(s, f32) for s in accs]
    res = _pc(body, name=name, grid=(t // tr,), in_specs=in_specs, out_specs=out_specs, out_shape=out_shape,
              sem=("arbitrary",))(*rows, *bcast)
    return res


def _rms(x, gain):
    return x * lax.rsqrt(jnp.mean(x * x, axis=-1, keepdims=True) + EPS) * gain


def _rms_bwd(x, gain, dy):
    _, vjp = jax.vjp(_rms, x, gain)
    return vjp(dy)


def _lower_bound(lbl):
    l0, l1 = lbl[0:1, :], lbl[1:2, :]
    mx = jnp.maximum(l0, l1)
    e0, e1 = jnp.exp(l0 - mx), jnp.exp(l1 - mx)
    return e0 / (e0 + e1)


def _gates(qpre, fpre, lbl):
    lb = _lower_bound(lbl)
    q = jax.nn.silu(qpre)
    forget = lb + (1.0 - lb) * jax.nn.sigmoid(fpre)
    return q, 1.0 - forget, jnp.log(forget)


def _head_norm_gate(o, gpre, gn):
    return _rms(o, gn) * jax.nn.silu(gpre)


def _swap_halves(x):
    w = x.shape[1]
    lane = lax.broadcasted_iota(jnp.int32, x.shape, 1)
    return jnp.where((lane % MLA_ROPE) < MLA_ROPE // 2, pltpu.roll(x, w - MLA_ROPE // 2, 1),
                     pltpu.roll(x, MLA_ROPE // 2, 1))


def _tile_lanes(tab, w):
    return tab if w == tab.shape[1] else jnp.concatenate([tab] * (w // tab.shape[1]), axis=1)


def _rope(x, cos, sgn_sin, sign=1.0):
    w = x.shape[1]
    return x * _tile_lanes(cos, w) + sign * _swap_halves(x) * _tile_lanes(sgn_sin, w)


def _bd(a, b, ca, cb):
    return lax.dot_general(a.astype(bf16), b.astype(bf16), (((ca,), (cb,)), ((), ())), preferred_element_type=f32)


@jax.custom_vjp
def _dot_nn(a, b):
    return _bd(a, b, 1, 0)


@jax.custom_vjp
def _dot_nt(a, b):
    return _bd(a, b, 1, 1)


@jax.custom_vjp
def _dot_tn(a, b):
    return _bd(a, b, 0, 0)


_dot_nn.defvjp(lambda a, b: (_bd(a, b, 1, 0), (a, b)), lambda r, g: (_bd(g, r[1], 1, 1), _bd(r[0], g, 0, 0)))
_dot_nt.defvjp(lambda a, b: (_bd(a, b, 1, 1), (a, b)), lambda r, g: (_bd(g, r[1], 1, 0), _bd(g, r[0], 0, 0)))
_dot_tn.defvjp(lambda a, b: (_bd(a, b, 0, 0), (a, b)), lambda r, g: (_bd(r[1], g, 1, 1), _bd(r[0], g, 1, 0)))


def _scan_rows(x, reverse):
    n = x.shape[0]
    row = lax.broadcasted_iota(jnp.int32, x.shape, 0)
    s = 1
    while s < n:
        if reverse:
            x = x + jnp.where(row < n - s, pltpu.roll(x, n - s, 0), 0.0)
        else:
            x = x + jnp.where(row >= s, pltpu.roll(x, s, 0), 0.0)
        s *= 2
    return x


@jax.custom_vjp
def _cumsum_rows(g):
    return _scan_rows(g, False)


_cumsum_rows.defvjp(lambda g: (_scan_rows(g, False), None), lambda _, ct: (_scan_rows(ct, True),))

HGRN_PAIRS = HGRN_HEADS // 2
HGRN_PAIR = 2 * HGRN_DK
GLA_STATE = (HGRN_PAIRS, HGRN_PAIR, HGRN_PAIR)


def _gla_consts():
    s = HGRN_SUB
    r = lax.broadcasted_iota(jnp.int32, (HGRN_PAIR, HGRN_PAIR), 0)
    c = lax.broadcasted_iota(jnp.int32, (HGRN_PAIR, HGRN_PAIR), 1)
    pair_mask = (r < HGRN_DK) == (c < HGRN_DK)
    masks = []
    for i in range(HGRN_CHUNK // s):
        n = s * (i + 1)
        row = lax.broadcasted_iota(jnp.int32, (HGRN_HEADS * s, HGRN_HEADS * n), 0)
        col = lax.broadcasted_iota(jnp.int32, (HGRN_HEADS * s, HGRN_HEADS * n), 1)
        col_head = sum((col >= m * n).astype(jnp.int32) for m in range(1, HGRN_HEADS))
        masks.append((col_head == row // s) & (col - col_head * n <= s * i + row % s))
    return pair_mask, masks


def _heads_to_rows(x):
    return jnp.concatenate([x[:, HGRN_DK * h:HGRN_DK * (h + 1)] for h in range(HGRN_HEADS)], axis=0)


def _gla_chunk(consts, dots, q, k, v, g, st):
    pair_mask, masks = consts
    dot_nn, dot_nt, dot_tn = dots
    c, s = HGRN_CHUNK, HGRN_SUB
    b = _cumsum_rows(g)
    b_last = b[c - 1:c, :]
    q_in, k_out = q * jnp.exp(b), k * jnp.exp(b_last - b)
    o_inter, st_new = [], []
    for p in range(HGRN_PAIRS):
        cols = slice(HGRN_PAIR * p, HGRN_PAIR * (p + 1))
        o_inter.append(dot_nt(q_in[:, cols], st[p]))
        st_new.append(st[p] * jnp.exp(b_last[:, cols]) + jnp.where(pair_mask, dot_tn(v[:, cols], k_out[:, cols]), 0.0))
    intra = []
    for i in range(c // s):
        n = s * (i + 1)
        ref = b[s * i - 1:s * i, :] if i else jnp.zeros_like(b_last)
        qt = _heads_to_rows(q[s * i:n] * jnp.exp(b[s * i:n] - ref))
        kt = _heads_to_rows(k[:n] * jnp.exp(jnp.minimum(ref - b[:n], EXP_CLAMP)))
        sc = jnp.where(masks[i], dot_nt(qt, kt), 0.0)
        oi = dot_nn(sc, _heads_to_rows(v[:n]))
        intra.append(jnp.concatenate([oi[s * h:s * (h + 1)] for h in range(HGRN_HEADS)], axis=1))
    return jnp.concatenate(o_inter, axis=1) + jnp.concatenate(intra, axis=0), st_new


_PLAIN_DOTS = (lambda a, b: _bd(a, b, 1, 0), lambda a, b: _bd(a, b, 1, 1), lambda a, b: _bd(a, b, 0, 0))
_VJP_DOTS = (_dot_nn, _dot_nt, _dot_tn)


def _hgrn_mix(consts, dots, qpre, fpre, v, gpre, lbl, gn, st):
    q, k, g = _gates(qpre, fpre, lbl)
    o, st_new = _gla_chunk(consts, dots, q, k, v, g, st)
    y = [_head_norm_gate(o[:, HGRN_DK * h:HGRN_DK * (h + 1)], gpre[:, HGRN_DK * h:HGRN_DK * (h + 1)], gn)
         for h in range(HGRN_HEADS)]
    return jnp.concatenate(y, axis=1), st_new


def _gla_fwd(p4, lbl, gn, gather=None):
    t = p4.shape[0]
    nc = t // HGRN_CHUNK

    def body(q_ref, k_ref, v_ref, g_ref, lbl_ref, gn_ref, *rest):
        if gather is None:
            o_ref, s_ref, st = rest
        else:
            wp_ref, _, o_ref, s_ref, gathered_ref, st, sems = rest

        @pl.when(pl.program_id(0) == 0)
        def _():
            st[...] = jnp.zeros_like(st)
            if gather is not None:
                _gather_start(wp_ref, gathered_ref, sems)

        if gather is not None:
            @pl.when(pl.program_id(0) == nc - 1)
            def _():
                _gather_finish(wp_ref, gathered_ref, sems)

        s_in = [st[p] for p in range(HGRN_PAIRS)]
        y, st_new = _hgrn_mix(_gla_consts(), _PLAIN_DOTS, q_ref[...], k_ref[...], v_ref[...], g_ref[...],
                              lbl_ref[...], gn_ref[...], s_in)
        o_ref[...] = y.astype(o_ref.dtype)
        for p in range(HGRN_PAIRS):
            s_ref[0, p] = s_in[p]
            st[p] = st_new[p]

    blk = lambda off: pl.BlockSpec((HGRN_CHUNK, D_MODEL), lambda c: (c, off))
    whole = lambda a: pl.BlockSpec(a.shape, lambda c: (0, 0))
    state_shape = GLA_STATE
    in_specs = [blk(0), blk(1), blk(2), blk(3), whole(lbl), whole(gn)]
    out_specs = [blk(0), pl.BlockSpec((1,) + state_shape, lambda c: (c, 0, 0, 0))]
    out_shape = [_sds((t, D_MODEL), bf16), _sds((nc,) + state_shape, f32)]
    scratch = [pltpu.VMEM(state_shape, f32)]
    if gather is None:
        return _pc(body, name="gla_fwd", grid=(nc,), in_specs=in_specs, out_specs=out_specs, out_shape=out_shape,
                   scratch=scratch, sem=("arbitrary",))(p4, p4, p4, p4, lbl, gn)
    return _pc(body, name="gla_fwd_gather", grid=(nc,), in_specs=in_specs + [HBM, HBM], out_specs=out_specs + [HBM],
               out_shape=out_shape + [_sds((4,) + gather.shape, bf16)], aliases={7: 2},
               scratch=scratch + [pltpu.SemaphoreType.DMA((2, 6))], sem=("arbitrary",))(
                   p4, p4, p4, p4, lbl, gn, gather, _gather_base(gather))


def _gla_bwd(p4, lbl, gn, states, dy, exchange=None):
    t = p4.shape[0]
    nc = t // HGRN_CHUNK

    def body(q_ref, k_ref, v_ref, g_ref, lbl_ref, gn_ref, s_ref, dy_ref, *rest):
        if exchange is None:
            dp_ref, dlbl_ref, dgn_ref, dst = rest
        else:
            sb_ref, dp_ref, dlbl_ref, dgn_ref, recv_ref, dst, sems = rest

        @pl.when(pl.program_id(0) == 0)
        def _():
            dst[...] = jnp.zeros_like(dst)
            dlbl_ref[...] = jnp.zeros_like(dlbl_ref)
            dgn_ref[...] = jnp.zeros_like(dgn_ref)
            if exchange is not None:
                _chips_start(sb_ref, recv_ref, sems)

        if exchange is not None:
            @pl.when(pl.program_id(0) == nc - 1)
            def _():
                _chips_finish(sb_ref, recv_ref, sems)

        consts = _gla_consts()
        fn = lambda *args: _hgrn_mix(consts, _VJP_DOTS, *args)
        pairs = range(HGRN_PAIRS)
        _, vjp = jax.vjp(fn, q_ref[...], k_ref[...], v_ref[...], g_ref[...], lbl_ref[...], gn_ref[...],
                         [s_ref[0, p] for p in pairs])
        *d_proj, dlbl, dgn, ds = vjp((dy_ref[...], [dst[p] for p in pairs]))
        for i, d in enumerate(d_proj):
            dp_ref[:, D_MODEL * i:D_MODEL * (i + 1)] = d.astype(dp_ref.dtype)
        dlbl_ref[...] += dlbl
        dgn_ref[...] += dgn
        for p in pairs:
            dst[p] = ds[p]

    blk = lambda off: pl.BlockSpec((HGRN_CHUNK, D_MODEL), lambda c: (nc - 1 - c, off))
    whole = lambda a: pl.BlockSpec(a.shape, lambda c: (0, 0))
    state_shape = GLA_STATE
    in_specs = [blk(0), blk(1), blk(2), blk(3), whole(lbl), whole(gn),
                pl.BlockSpec((1,) + state_shape, lambda c: (nc - 1 - c, 0, 0, 0)), blk(0)]
    out_specs = [pl.BlockSpec((HGRN_CHUNK, 4 * D_MODEL), lambda c: (nc - 1 - c, 0)), whole(lbl), whole(gn)]
    out_shape = [_sds((t, 4 * D_MODEL), bf16), _sds(lbl.shape, f32), _sds(gn.shape, f32)]
    scratch = [pltpu.VMEM(state_shape, f32)]
    if exchange is None:
        return _pc(body, name="gla_bwd", grid=(nc,), in_specs=in_specs, out_specs=out_specs, out_shape=out_shape,
                   scratch=scratch, sem=("arbitrary",))(p4, p4, p4, p4, lbl, gn, states, dy)
    return _pc(body, name="gla_bwd_exchange", grid=(nc,), in_specs=in_specs + [HBM], out_specs=out_specs + [HBM],
               out_shape=out_shape + [_sds((3,) + exchange.shape[1:], bf16)],
               scratch=scratch + [pltpu.SemaphoreType.DMA((2, 3))], sem=("arbitrary",))(
                   p4, p4, p4, p4, lbl, gn, states, dy, exchange)


ATT_FWD_TQ, ATT_FWD_TK = 1024, 1024
ATT_BWD_TQ, ATT_BWD_TK = 1024, 512
ATT_QK = 2 * LANES
NEG = -1e30


def _pair_masks(shape):
    lane = lax.broadcasted_iota(jnp.int32, shape, 1)
    return lane < MLA_ROPE, lane >= MLA_ROPE


def _causal(shape, row0, col0):
    row = row0 + lax.broadcasted_iota(jnp.int32, shape, 0)
    col = col0 + lax.broadcasted_iota(jnp.int32, shape, 1)
    return col <= row


def _qk_cols(e):
    return slice(ATT_QK * e, ATT_QK * (e + 1))


def _v_cols(e):
    return slice(MLA_V * e, MLA_V * (e + 1))


def _attn_fwd(qc, kc, v):
    t = qc.shape[0]
    tq, tk = min(ATT_FWD_TQ, t), min(ATT_FWD_TK, t)
    npair = MLA_HEADS // 2

    def body(q_ref, k_ref, v_ref, o_ref, lse_ref):
        i = pl.program_id(1)
        n_full = (i * tq + 1) // tk
        nkv = (i * tq + tq + tk - 1) // tk
        q = [q_ref[:, _qk_cols(e)] for e in range(2)]

        def step(j, carry, masked):
            ks = pl.ds(pl.multiple_of(j * tk, tk), tk)
            ok = _causal((tq, tk), i * tq, j * tk) if masked else None
            new = []
            for e in range(2):
                m, l, acc = carry[e]
                s = _bd(q[e], k_ref[ks, _qk_cols(e)], 1, 1)
                if masked:
                    s = jnp.where(ok, s, NEG)
                m_new = jnp.maximum(m, jnp.max(s, axis=-1, keepdims=True))
                p = jnp.exp(s - m_new)
                alpha = jnp.exp(m - m_new)
                l = alpha * l + jnp.sum(p, axis=-1, keepdims=True)
                acc = alpha * acc + _bd(p, v_ref[ks, _v_cols(e)], 1, 0)
                new.append((m_new, l, acc))
            return tuple(new)

        one = (jnp.full((tq, 1), NEG, f32), jnp.zeros((tq, 1), f32), jnp.zeros((tq, MLA_V), f32))
        carry = lax.fori_loop(0, n_full, functools.partial(step, masked=False), (one, one))
        carry = lax.fori_loop(n_full, nkv, functools.partial(step, masked=True), carry)
        o_ref[...] = jnp.concatenate([acc / l for _, l, acc in carry], axis=1).astype(o_ref.dtype)
        lo, _ = _pair_masks((tq, LANES))
        lse_ref[...] = jnp.where(lo, *[m + jnp.log(l) for m, l, _ in carry])

    return _pc(body, name="attn_fwd", grid=(npair, t // tq),
               in_specs=[pl.BlockSpec((tq, 2 * ATT_QK), lambda p, i: (i, p)),
                         pl.BlockSpec((t, 2 * ATT_QK), lambda p, i: (0, p)),
                         pl.BlockSpec((t, 2 * MLA_V), lambda p, i: (0, p))],
               out_specs=[pl.BlockSpec((tq, 2 * MLA_V), lambda p, i: (i, p)),
                          pl.BlockSpec((tq, LANES), lambda p, i: (i, p))],
               out_shape=[_sds((t, MLA_HEADS * MLA_V), bf16), _sds((t, npair * LANES), f32)],
               sem=("arbitrary", "arbitrary"))(qc, kc, v)


def _attn_bwd(qc, kc, v, do, lse, delta):
    t = qc.shape[0]
    tq, tk = min(ATT_BWD_TQ, t), min(ATT_BWD_TK, t)
    npair = MLA_HEADS // 2
    nq = t // tq

    def body(q_ref, do_ref, lse_ref, dl_ref, k_ref, v_ref, dq_ref, dk_ref, dv_ref):
        j = pl.program_id(1)

        @pl.when(j == 0)
        def _():
            dq_ref[...] = jnp.zeros_like(dq_ref)

        k = [k_ref[:, _qk_cols(e)] for e in range(2)]
        vv = [v_ref[:, _v_cols(e)] for e in range(2)]

        def step(i, carry, masked):
            qs = pl.ds(pl.multiple_of(i * tq, tq), tq)
            ok = _causal((tq, tk), i * tq, j * tk) if masked else None
            lse2, dl2 = lse_ref[qs, :], dl_ref[qs, :]
            new = []
            for e in range(2):
                dk, dv = carry[e]
                q_e, do_e = q_ref[qs, _qk_cols(e)], do_ref[qs, _v_cols(e)]
                p = jnp.exp(_bd(q_e, k[e], 1, 1) - lse2[:, MLA_ROPE * e:MLA_ROPE * e + 1])
                if masked:
                    p = jnp.where(ok, p, 0.0)
                dv = dv + _bd(p, do_e, 0, 0)
                dp = _bd(do_e, vv[e], 1, 1)
                ds = (p * (dp - dl2[:, MLA_ROPE * e:MLA_ROPE * e + 1])).astype(bf16)
                dk = dk + _bd(ds, q_e, 0, 0)
                dq_ref[qs, _qk_cols(e)] += _bd(ds, k[e], 1, 0)
                new.append((dk, dv))
            return tuple(new)

        one = (jnp.zeros((tk, ATT_QK), f32), jnp.zeros((tk, MLA_V), f32))
        i_full = jnp.minimum((j * tk + tk + tq - 2) // tq, nq)
        carry = lax.fori_loop((j * tk) // tq, i_full, functools.partial(step, masked=True), (one, one))
        carry = lax.fori_loop(i_full, nq, functools.partial(step, masked=False), carry)
        for e in range(2):
            dk_ref[:, _qk_cols(e)] = carry[e][0].astype(dk_ref.dtype)
            dv_ref[:, _v_cols(e)] = carry[e][1].astype(dv_ref.dtype)

    res = lambda w: pl.BlockSpec((t, w), lambda p, j: (0, p))
    blk = lambda w: pl.BlockSpec((tk, w), lambda p, j: (j, p))
    return _pc(body, name="attn_bwd", grid=(npair, t // tk),
               in_specs=[res(2 * ATT_QK), res(2 * MLA_V), res(LANES), res(LANES), blk(2 * ATT_QK), blk(2 * MLA_V)],
               out_specs=[res(2 * ATT_QK), blk(2 * ATT_QK), blk(2 * MLA_V)],
               out_shape=[_sds((t, MLA_HEADS * ATT_QK), f32), _sds((t, MLA_HEADS * ATT_QK), bf16),
                          _sds((t, MLA_HEADS * MLA_V), bf16)],
               sem=("arbitrary", "arbitrary"))(qc, do, lse, delta, kc, v)


def _rope_tables(t):
    half = MLA_ROPE // 2
    inv_freq = ROPE_THETA ** (-jnp.arange(half, dtype=f32) / half)
    ang = jnp.arange(t, dtype=f32)[:, None] * inv_freq[None, :]
    cos, sin = jnp.cos(ang), jnp.sin(ang)
    cos128, sin128 = jnp.concatenate([cos, cos] * 2, axis=1), jnp.concatenate([-sin, sin] * 2, axis=1)
    one, zero = jnp.ones((t, MLA_NOPE), f32), jnp.zeros((t, MLA_NOPE), f32)
    return cos128, sin128, jnp.concatenate([one, cos128], axis=1), jnp.concatenate([zero, sin128], axis=1)


def _relu2_epi(u):
    r = jnp.maximum(u, 0.0)
    return u, r * r


def _add_epi(r, res):
    return (r + res,)


def _drelu2_epi(da, u):
    return (da * 2.0 * jnp.maximum(u.astype(f32), 0.0),)


def _mlp_fwd(h, gain, w_up, w_down, tag):
    xm = _rw(lambda x, g: (_rms(x, g),), [h], [gain], [(D_MODEL, bf16)], name=f"mlp{tag}_norm")[0]
    u, a = _mm(xm, w_up, name=f"mlp{tag}_up", outs=(bf16, bf16), epi=_relu2_epi)
    h_out = _mm(a, w_down, name=f"mlp{tag}_down", epi=_add_epi, extras=(h,))
    return h_out, (xm, u, a)


def _mlp_bwd(dh, dh16, h, gain, w_up, w_down, saved, tag):
    xm, u, a = saved
    du = _mm(dh16, w_down, tb=True, name=f"mlp{tag}_dact", outs=(bf16,), epi=_drelu2_epi, extras=(u,))
    d_down = _mm(a, dh16, ta=True, name=f"mlp{tag}_dwdown")
    d_up = _mm(xm, du, ta=True, name=f"mlp{tag}_dwup")
    dxm = _mm(du, w_up, tb=True, name=f"mlp{tag}_dxm")

    def fn(x, dy, dres, g):
        dx, dg = _rms_bwd(x, g, dy)
        return dx + dres, dx + dres, dg

    dh_in, dh_in16, d_gain = _rw(fn, [h, dxm, dh], [gain], [(D_MODEL, f32), (D_MODEL, bf16)], [(1, D_MODEL)],
                                 name=f"mlp{tag}_dnorm")
    return dh_in, dh_in16, d_gain, d_up, d_down


def _local_step(x, target, w, comm=None):
    w = dict(w)
    t = x.shape[0]
    cos, sgn_sin, cos_qk, sin_qk = _rope_tables(t)
    grads = {}

    xn0 = _rw(lambda xx, g: (_rms(xx, g),), [x], [w["hgrn_norm"]], [(D_MODEL, bf16)], name="hgrn_norm")[0]
    p4 = _mm(xn0, w["hgrn_w4"], name="hgrn_proj", tn=2048)

    if comm is None:
        y, states = _gla_fwd(p4, w["hgrn_lb_logits"], w["hgrn_g_norm"])
    else:
        y, states, gathered = _gla_fwd(p4, w["hgrn_lb_logits"], w["hgrn_g_norm"], gather=comm.late_shard)
        w.update(comm.unpack_late(gathered))
    h1 = _mm(y, w["hgrn_w_o"], name="hgrn_out", epi=_add_epi, extras=(x,))
    h2, mlp0 = _mlp_fwd(h1, w["mlp_norm"][0:1], w["mlp_w_up", 0], w["mlp_w_down", 0], 0)

    hk, xn1 = _rw(lambda hh, g1, g2: (_rms(hh, g1), _rms(hh, g2)), [h2], [w["kv_in_norm"], w["mla_norm"]],
                  [(D_MODEL, bf16)] * 2, name="kv_mla_norm")
    ckr = _mm(hk, w["kv_w_dkv"], name="kv_down")

    def ckv_fn(c, cs, sn, g):
        kr = _rope(c[:, MLA_KV_LORA:], cs, sn)
        return _rms(c[:, :MLA_KV_LORA], g), jnp.concatenate([jnp.zeros_like(kr), kr], axis=1)

    c_kv, kr_head = _rw(ckv_fn, [ckr, cos, sgn_sin], [w["kv_norm"]], [(MLA_KV_LORA, bf16), (ATT_QK, f32)],
                        name="kv_norm_rope")
    kc = _mm(c_kv, w["kv_w_kcat"], name="kv_up_k", outs=(bf16,), extras=(kr_head,),
             epi=lambda r, kr: (r + _tile_lanes(kr, r.shape[1]),))
    v_att = _mm(c_kv, w["kv_w_uv"], name="kv_up_v", outs=(bf16,))
    cq0 = _mm(xn1, w["mla_w_dq"], name="q_down")
    c_q = _rw(lambda c, g: (_rms(c, g),), [cq0], [w["mla_q_norm"]], [(MLA_Q_LORA, bf16)], name="q_norm")[0]
    qc = _mm(c_q, w["mla_w_qcat"], name="q_up", outs=(bf16,), extras=(cos_qk, sin_qk),
             epi=lambda r, cs, sn: (_rope(r, cs, sn) * ATT_SCALE,))
    o_att, lse = _attn_fwd(qc, kc, v_att)
    h3 = _mm(o_att, w["mla_w_o"], name="mla_out", epi=_add_epi, extras=(h2,))
    h4, mlp1 = _mlp_fwd(h3, w["mlp_norm"][1:2], w["mlp_w_up", 1], w["mlp_w_down", 1], 1)

    def loss_fn(hh, tgt, gain):
        def f(a, b):
            e = _rms(a, b) - tgt
            return 0.5 * jnp.sum(jnp.sum(e * e, axis=-1, keepdims=True) / D_MODEL, axis=0, keepdims=True)
        val, vjp = jax.vjp(f, hh, gain)
        dh, dg = vjp(jnp.ones((1, 1), f32))
        return dh, dh, jnp.broadcast_to(val, (1, LANES)), dg

    dh4, dh4_16, loss_acc, grads["final_norm"] = _rw(loss_fn, [h4, target], [w["final_norm"]],
                                                     [(D_MODEL, f32), (D_MODEL, bf16)], [(1, LANES), (1, D_MODEL)],
                                                     name="loss")
    loss = loss_acc[0, 0]

    dh3, dh3_16, g_n1, g_up1, g_dn1 = _mlp_bwd(dh4, dh4_16, h3, w["mlp_norm"][1:2], w["mlp_w_up", 1],
                                               w["mlp_w_down", 1], mlp1, 1)
    do_att = _mm(dh3_16, w["mla_w_o"], tb=True, name="mla_dout", outs=(bf16,))
    grads["mla_w_o"] = _mm(o_att, dh3_16, ta=True, name="mla_dwo")

    def delta_fn(a, b):
        prod = a.astype(f32) * b.astype(f32)
        outs = []
        for p in range(MLA_HEADS // 2):
            d0 = jnp.sum(prod[:, 2 * p * LANES:(2 * p + 1) * LANES], axis=-1, keepdims=True)
            d1 = jnp.sum(prod[:, (2 * p + 1) * LANES:(2 * p + 2) * LANES], axis=-1, keepdims=True)
            lo, _ = _pair_masks((a.shape[0], LANES))
            outs.append(jnp.where(lo, d0, d1))
        return (jnp.concatenate(outs, axis=1),)

    delta = _rw(delta_fn, [do_att, o_att], [], [(MLA_HEADS // 2 * LANES, f32)], name="attn_delta")[0]
    dqc, dkc, dv = _attn_bwd(qc, kc, v_att, do_att, lse, delta)
    dqf = _rw(lambda a, cs, sn: (_rope(a, cs, sn, -1.0) * ATT_SCALE,), [dqc, cos_qk, sin_qk], [],
              [(MLA_HEADS * ATT_QK, bf16)], name="dq_rope")[0]
    dc_q = _mm(dqf, w["mla_w_qcat"], tb=True, name="q_up_dx")
    grads["mla_w_qcat"] = _mm(c_q, dqf, ta=True, name="q_up_dw")

    def dqn_fn(c, dy, g):
        return _rms_bwd(c, g, dy)

    dcq0, grads["mla_q_norm"] = _rw(dqn_fn, [cq0, dc_q], [w["mla_q_norm"]], [(MLA_Q_LORA, bf16)], [(1, MLA_Q_LORA)],
                                    name="q_dnorm")
    dxn1 = _mm(dcq0, w["mla_w_dq"], tb=True, name="q_down_dx")
    grads["mla_w_dq"] = _mm(xn1, dcq0, ta=True, name="q_down_dw")

    dc_kv = _mm(dkc, w["kv_w_kcat"], tb=True, name="kv_up_dx_k")
    dc_kv = _mm(dv, w["kv_w_uv"], tb=True, name="kv_up_dx_v", epi=_add_epi, extras=(dc_kv,))
    grads["kv_w_kcat"] = _mm(c_kv, dkc, ta=True, name="kv_up_dw_k")
    grads["kv_w_uv"] = _mm(c_kv, dv, ta=True, name="kv_up_dw_v")

    def dckr_fn(c, dc, dk_heads, cs, sn, g):
        tot = dk_heads[:, LANES:ATT_QK].astype(f32)
        for h in range(1, MLA_HEADS):
            tot = tot + dk_heads[:, ATT_QK * h + LANES:ATT_QK * (h + 1)].astype(f32)
        lo, _ = _pair_masks(tot.shape)
        dkr = jnp.where(lo, _rope(tot, cs, sn, -1.0), 0.0)
        dcc, dg = _rms_bwd(c[:, :MLA_KV_LORA], g, dc)
        return jnp.concatenate([dcc, dkr], axis=1), dg

    dckr, grads["kv_norm"] = _rw(dckr_fn, [ckr, dc_kv, dkc, cos, sgn_sin], [w["kv_norm"]],
                                 [(MLA_KV_LORA + LANES, bf16)], [(1, MLA_KV_LORA)], name="kv_dnorm_rope")
    dhk = _mm(dckr, w["kv_w_dkv"], tb=True, name="kv_down_dx")
    grads["kv_w_dkv"] = _mm(hk, dckr, ta=True, name="kv_down_dw")

    def dh2_fn(hh, d1, d2, dres, g1, g2):
        a, ga = _rms_bwd(hh, g1, d1)
        b, gb = _rms_bwd(hh, g2, d2)
        return a + b + dres, a + b + dres, ga, gb

    dh2, dh2_16, grads["kv_in_norm"], grads["mla_norm"] = _rw(
        dh2_fn, [h2, dhk, dxn1, dh3], [w["kv_in_norm"], w["mla_norm"]], [(D_MODEL, f32), (D_MODEL, bf16)],
        [(1, D_MODEL)] * 2, name="kv_mla_dnorm")

    dh1, dh1_16, g_n0, g_up0, g_dn0 = _mlp_bwd(dh2, dh2_16, h1, w["mlp_norm"][0:1], w["mlp_w_up", 0],
                                               w["mlp_w_down", 0], mlp0, 0)
    grads["mlp_norm"] = jnp.concatenate([g_n0, g_n1], axis=0)
    grads["mlp_w_up"] = (g_up0, g_up1)
    grads["mlp_w_down"] = (g_dn0, g_dn1)
    dy = _mm(dh1_16, w["hgrn_w_o"], tb=True, name="hgrn_dout")
    grads["hgrn_w_o"] = _mm(y, dh1_16, ta=True, name="hgrn_dwo")

    gla_args = (p4, w["hgrn_lb_logits"], w["hgrn_g_norm"], states, dy)
    if comm is None:
        dp4, grads["hgrn_lb_logits"], grads["hgrn_g_norm"] = _gla_bwd(*gla_args)
    else:
        dp4, grads["hgrn_lb_logits"], grads["hgrn_g_norm"], comm.received_early = _gla_bwd(
            *gla_args, exchange=comm.reduce_early(grads))
    dxn0 = _mm(dp4, w["hgrn_w4"], tb=True, name="hgrn_proj_dx")
    grads["hgrn_w4"] = _mm(xn0, dp4, ta=True, name="hgrn_proj_dw")

    def dx_fn(xx, dyy, dres, gn):
        dxx, dgn = _rms_bwd(xx, gn, dyy)
        return dxx + dres, dgn

    grad_x, grads["hgrn_norm"] = _rw(dx_fn, [x, dxn0, dh1], [w["hgrn_norm"]], [(D_MODEL, f32)], [(1, D_MODEL)],
                                     name="hgrn_dnorm")
    return loss, grad_x, grads


HBM = pl.BlockSpec(memory_space=pltpu.HBM)


def _me():
    return lax.axis_index("x"), lax.axis_index("y"), lax.axis_index("c")


def _flip(x, y, f):
    return (1 - x if f & 1 else x), (1 - y if f & 2 else y)


def _rcopy(src, dst, sems, k, dev):
    return pltpu.make_async_remote_copy(src_ref=src, dst_ref=dst, send_sem=sems.at[0, k], recv_sem=sems.at[1, k],
                                        device_id=dev, device_id_type=MESH)


def _my_half(rows, c, mine=True):
    half = rows // 2
    return pl.ds(pl.multiple_of((c if mine else 1 - c) * half, 16), half)


def _gather_start(wp_ref, out_ref, sems):
    x, y, c = _me()
    half = _my_half(wp_ref.shape[0], c)
    for f in (1, 2, 3):
        px, py = _flip(x, y, f)
        _rcopy(wp_ref.at[half], out_ref.at[2 * x + y, half], sems, f - 1, (px, py, c)).start()


def _gather_finish(wp_ref, out_ref, sems):
    x, y, c = _me()
    half, other = _my_half(wp_ref.shape[0], c), _my_half(wp_ref.shape[0], c, mine=False)
    sends = []
    for f in (1, 2, 3):
        px, py = _flip(x, y, f)
        landed = out_ref.at[2 * px + py, half]
        _rcopy(landed, landed, sems, f - 1, (px, py, c)).wait_recv()
        sends.append(_rcopy(landed, landed, sems, 2 + f, (x, y, 1 - c)))
        sends[-1].start()
    for f in (1, 2, 3):
        px, py = _flip(x, y, f)
        theirs = out_ref.at[2 * px + py, other]
        _rcopy(theirs, theirs, sems, 2 + f, (x, y, 1 - c)).wait_recv()
        sends.append(_rcopy(wp_ref.at[half], out_ref.at[2 * x + y, half], sems, f - 1, (px, py, c)))
    for cp in sends:
        cp.wait_send()


def _gather_base(wp):
    return jnp.broadcast_to(wp[None], (4,) + wp.shape)


def _all_gather_weights(wp, sv):
    def body(wp_ref, sv_ref, base_ref, out_ref, svs_ref, sems, local_sem):
        x, y, c = _me()
        mine = pltpu.make_async_copy(sv_ref, svs_ref.at[2 * x + y], local_sem)
        mine.start()
        _gather_start(wp_ref, out_ref, sems)
        small = []
        for f in (1, 2, 3):
            px, py = _flip(x, y, f)
            small.append(_rcopy(sv_ref, svs_ref.at[2 * x + y], sems, 5 + f, (px, py, c)))
            small[-1].start()
        _gather_finish(wp_ref, out_ref, sems)
        for f in (1, 2, 3):
            px, py = _flip(x, y, f)
            _rcopy(sv_ref, svs_ref.at[2 * px + py], sems, 5 + f, (px, py, c)).wait_recv()
        for cp in small:
            cp.wait_send()
        mine.wait()

    return _pc(body, name="weights_all_gather", in_specs=[HBM, HBM, HBM], out_specs=[HBM, HBM],
               out_shape=[_sds((4,) + wp.shape, bf16), _sds((4, 8, 256), f32)], aliases={2: 0},
               scratch=[pltpu.SemaphoreType.DMA((2, 9)), pltpu.SemaphoreType.DMA])(wp, sv, _gather_base(wp))


def _send_half_to_sibling(gp, name):
    rows = gp.shape[1]

    def body(gp_ref, out_ref, sems):
        x, y, c = _me()
        cp = _rcopy(gp_ref.at[:, _my_half(rows, c, mine=False)], out_ref, sems, 0, (x, y, 1 - c))
        cp.start()
        cp.wait()

    return _pc(body, name=name, in_specs=[HBM], out_specs=HBM, out_shape=_sds((4, rows // 2, D_MODEL), f32),
               scratch=[pltpu.SemaphoreType.DMA((2, 1))])(gp)


def _chips_start(sb_ref, out_ref, sems):
    x, y, c = _me()
    for f in (1, 2, 3):
        px, py = _flip(x, y, f)
        _rcopy(sb_ref.at[2 * px + py], out_ref.at[f - 1], sems, f - 1, (px, py, c)).start()


def _chips_finish(sb_ref, out_ref, sems):
    x, y, c = _me()
    for f in (1, 2, 3):
        _rcopy(sb_ref.at[0], out_ref.at[f - 1], sems, f - 1, (x, y, c)).wait_recv()
    for f in (1, 2, 3):
        px, py = _flip(x, y, f)
        _rcopy(sb_ref.at[2 * px + py], out_ref.at[f - 1], sems, f - 1, (px, py, c)).wait_send()


def _exchange_chips(sb, small):
    def body(sb_ref, small_ref, out_ref, smalls_ref, sems, local_sem):
        x, y, c = _me()
        me = 4 * x + 2 * y + c
        mine = pltpu.make_async_copy(small_ref, smalls_ref.at[me], local_sem)
        mine.start()
        _chips_start(sb_ref, out_ref, sems)
        sends = []
        for f in range(1, 8):
            px, py = _flip(x, y, f)
            pc = 1 - c if f & 4 else c
            sends.append(_rcopy(small_ref, smalls_ref.at[me], sems, 2 + f, (px, py, pc)))
            sends[-1].start()
        _chips_finish(sb_ref, out_ref, sems)
        for f in range(1, 8):
            px, py = _flip(x, y, f)
            pc = 1 - c if f & 4 else c
            _rcopy(small_ref, smalls_ref.at[4 * px + 2 * py + pc], sems, 2 + f, (x, y, c)).wait_recv()
        for cp in sends:
            cp.wait_send()
        mine.wait()

    return _pc(body, name="grads_exchange_chips", in_specs=[HBM, HBM], out_specs=[HBM, HBM],
               out_shape=[_sds((3,) + sb.shape[1:], bf16), _sds((8, SMALL_ROWS, D_MODEL), f32)],
               scratch=[pltpu.SemaphoreType.DMA((2, 10)), pltpu.SemaphoreType.DMA])(sb, small)


def _exchange_halves(tot, name):
    rows = tot.shape[0]

    def body(tot_ref, out_ref, sems):
        x, y, c = _me()
        half = _my_half(rows, c)
        cp = _rcopy(tot_ref.at[half], out_ref.at[half], sems, 0, (x, y, 1 - c))
        cp.start()
        cp.wait()

    return _pc(body, name=name, in_specs=[HBM], out_specs=HBM, out_shape=_sds((rows, D_MODEL), f32),
               aliases={0: 0}, scratch=[pltpu.SemaphoreType.DMA((2, 1))])(tot)


def _sum_rows(half):
    return max(r for r in range(16, 513, 16) if half % r == 0)


def _sum_over_cores(gp, recv, cq, name):
    half = recv.shape[1]
    tr = _sum_rows(half)
    nb = half // tr

    def body(cq_ref, g_ref, r_ref, o32_ref, o16_ref):
        s = g_ref[...] + r_ref[...]
        o32_ref[...] = s
        o16_ref[...] = s.astype(bf16)

    spec = pl.BlockSpec((1, tr, D_MODEL), lambda b, i, cq_ref: (b, i, 0))
    gs = pltpu.PrefetchScalarGridSpec(
        num_scalar_prefetch=1, grid=(4, nb),
        in_specs=[pl.BlockSpec((1, tr, D_MODEL), lambda b, i, cq_ref: (b, cq_ref[0] * nb + i, 0)), spec],
        out_specs=[spec, spec])
    return _pc(body, name=name, grid_spec=gs, sem=("arbitrary", "arbitrary"),
               out_shape=[_sds((4, half, D_MODEL), f32), _sds((4, half, D_MODEL), bf16)])(cq, gp, recv)


def _sum_over_chips(s32, recv, cq, name):
    half = recv.shape[1]
    tr = _sum_rows(half)
    nb = half // tr

    def body(cq_ref, own_ref, r_ref, o_ref):
        o_ref[...] = ((own_ref[0] + r_ref[0].astype(f32)) + r_ref[1].astype(f32)) + r_ref[2].astype(f32)

    gs = pltpu.PrefetchScalarGridSpec(
        num_scalar_prefetch=1, grid=(nb,),
        in_specs=[pl.BlockSpec((1, tr, D_MODEL), lambda i, cq_ref: (cq_ref[1], i, 0)),
                  pl.BlockSpec((3, tr, D_MODEL), lambda i, cq_ref: (0, i, 0))],
        out_specs=pl.BlockSpec((tr, D_MODEL), lambda i, cq_ref: (cq_ref[0] * nb + i, 0)))
    return _pc(body, name=name, grid_spec=gs, sem=("arbitrary",),
               out_shape=_sds((2 * half, D_MODEL), f32))(cq, s32, recv)


def _sum_small(smalls):
    def body(s_ref, o_ref):
        tot = s_ref[0]
        for d in range(1, 8):
            tot = tot + s_ref[d]
        o_ref[...] = tot

    return _pc(body, name="small_sum", out_shape=_sds((SMALL_ROWS, D_MODEL), f32))(smalls)


def _adamw_math(w, g, m, v):
    m = ADAM_B1 * m + (1.0 - ADAM_B1) * g
    v = ADAM_B2 * v + (1.0 - ADAM_B2) * jnp.square(g)
    m_hat = m / (1.0 - ADAM_B1 ** ADAM_STEP)
    v_hat = v / (1.0 - ADAM_B2 ** ADAM_STEP)
    delta = -ADAM_LR * (m_hat / (jnp.sqrt(v_hat) + ADAM_EPS) + ADAM_WD * w)
    return delta, m, v


def _adamw(w, g, m, v, name):
    cols = w.shape[1]
    return _rw(_adamw_math, [w, g, m, v], [], [(cols, f32)] * 3, name=name, tr=256)


def _adamw_small(items):
    n = len(items)

    def body(*refs):
        ins, outs = refs[:4 * n], refs[4 * n:]
        for i in range(n):
            res = _adamw_math(*[r[...] for r in ins[4 * i:4 * i + 4]])
            for o, val in zip(outs[3 * i:3 * i + 3], res):
                o[...] = val

    flat = [a for it in items for a in it]
    out_shape = [_sds(it[0].shape, f32) for it in items for _ in range(3)]
    res = _pc(body, name="adamw_small", out_shape=out_shape)(*flat)
    return [tuple(res[3 * i:3 * i + 3]) for i in range(n)]


def _pack_shards(sh, layout, pad):
    parts = [(sh[n] if layer is None else sh[n][layer]).reshape(-1, D_MODEL).astype(bf16) for n, layer, _ in layout]
    if pad:
        parts.append(jnp.zeros((pad, D_MODEL), bf16))
    return jnp.concatenate(parts, axis=0)


def _mlp_full(g4, off, layer):
    o, r = off["mlp_w_up", layer]
    up = g4[:, o:o + r].transpose(1, 0, 2).reshape(D_MODEL, D_FF)
    o, r = off["mlp_w_down", layer]
    return {("mlp_w_up", layer): up, ("mlp_w_down", layer): g4[:, o:o + r].reshape(D_FF, D_MODEL)}


def _unpack_early(g4):
    w = _mlp_full(g4, W_EARLY_OFF, 0)
    hg = g4[:, 0:1024].reshape(4, 4, 256, D_MODEL)
    w["hgrn_w4"] = hg.transpose(0, 2, 1, 3).reshape(D_MODEL, 4 * D_MODEL)
    o, r = W_EARLY_OFF["hgrn_w_o", None]
    w["hgrn_w_o"] = g4[:, o:o + r].reshape(D_MODEL, D_MODEL)
    return w


def _unpack_late(g4):
    def rows(name):
        o, r = W_LATE_OFF[name, None]
        return g4[:, o:o + r]

    w = _mlp_full(g4, W_LATE_OFF, 1)
    w["mla_w_dq"] = rows("mla_w_dq").reshape(D_MODEL, MLA_Q_LORA)
    uq = rows("mla_w_uq").reshape(4, MLA_Q_LORA, 768).transpose(1, 0, 2).reshape(MLA_Q_LORA, MLA_HEADS, MLA_NOPE + MLA_ROPE)
    w["mla_w_qcat"] = jnp.pad(uq, ((0, 0), (0, 0), (0, ATT_QK - MLA_NOPE - MLA_ROPE))).reshape(MLA_Q_LORA, MLA_HEADS * ATT_QK)
    w["mla_w_o"] = rows("mla_w_o").reshape(MLA_HEADS * MLA_V, D_MODEL)
    dkv = rows("kv_w_dkv").reshape(D_MODEL, MLA_KV_LORA + MLA_ROPE)
    w["kv_w_dkv"] = jnp.pad(dkv, ((0, 0), (0, LANES - MLA_ROPE)))
    uk = rows("kv_w_uk").reshape(4, MLA_KV_LORA, 512).transpose(1, 0, 2).reshape(MLA_KV_LORA, MLA_HEADS, MLA_NOPE)
    w["kv_w_kcat"] = jnp.pad(uk, ((0, 0), (0, 0), (0, ATT_QK - MLA_NOPE))).reshape(MLA_KV_LORA, MLA_HEADS * ATT_QK)
    w["kv_w_uv"] = rows("kv_w_uv").reshape(4, MLA_KV_LORA, 512).transpose(1, 0, 2).reshape(MLA_KV_LORA, MLA_HEADS * MLA_V)
    return w


def _pack_grads_late(g):
    return g["hgrn_w4"].reshape(4, 256, 4, D_MODEL).transpose(0, 2, 1, 3).reshape(4, G_LATE_ROWS, D_MODEL)


def _pack_grads_early(g):
    parts = [g["hgrn_w_o"].reshape(4, 256, D_MODEL),
             g["mla_w_dq"].reshape(4, 64, D_MODEL)]
    uq = g["mla_w_qcat"].reshape(MLA_Q_LORA, MLA_HEADS, ATT_QK)[:, :, :MLA_NOPE + MLA_ROPE]
    parts.append(uq.reshape(MLA_Q_LORA, 4, 768).transpose(1, 0, 2).reshape(4, 192, D_MODEL))
    parts.append(g["mla_w_o"].reshape(4, 512, D_MODEL))
    parts.append(g["kv_w_dkv"][:, :MLA_KV_LORA + MLA_ROPE].reshape(4, 80, D_MODEL))
    uk = g["kv_w_kcat"].reshape(MLA_KV_LORA, MLA_HEADS, ATT_QK)[:, :, :MLA_NOPE]
    parts.append(uk.reshape(MLA_KV_LORA, 4, 512).transpose(1, 0, 2).reshape(4, 128, D_MODEL))
    parts.append(g["kv_w_uv"].reshape(MLA_KV_LORA, 4, 512).transpose(1, 0, 2).reshape(4, 128, D_MODEL))
    parts += [up.reshape(D_MODEL, 4, 1024).transpose(1, 0, 2) for up in g["mlp_w_up"]]
    parts += [dn.reshape(4, 1024, D_MODEL) for dn in g["mlp_w_down"]]
    parts.append(jnp.zeros((4, PACK_PAD, D_MODEL), f32))
    return jnp.concatenate(parts, axis=1)


LOSS_ROW = 11


def _pack_small(g, loss):
    rows = []
    for name, _, r, wd in SMALL:
        a = g[name].reshape(r, wd)
        rows.append(jnp.pad(a, ((0, 0), (0, D_MODEL - wd))) if wd < D_MODEL else a)
    assert sum(r for _, _, r, _ in SMALL) == LOSS_ROW
    rows.append(jnp.full((1, D_MODEL), loss, f32))
    rows.append(jnp.zeros((SMALL_ROWS - LOSS_ROW - 1, D_MODEL), f32))
    return jnp.concatenate(rows, axis=0)


def kernel(x, hgrn_norm, hgrn_w_q, hgrn_w_f, hgrn_w_i, hgrn_w_g, hgrn_g_norm, hgrn_w_o, hgrn_lb_logits, mla_norm, mla_w_dq, mla_q_norm, mla_w_uq, mla_w_o, kv_in_norm, kv_w_dkv, kv_norm, kv_w_uk, kv_w_uv, mlp_norm, mlp_w_up, mlp_w_down, final_norm, loss_target, m_hgrn_norm, m_hgrn_w_q, m_hgrn_w_f, m_hgrn_w_i, m_hgrn_w_g, m_hgrn_g_norm, m_hgrn_w_o, m_hgrn_lb_logits, m_mla_norm, m_mla_w_dq, m_mla_q_norm, m_mla_w_uq, m_mla_w_o, m_kv_in_norm, m_kv_w_dkv, m_kv_norm, m_kv_w_uk, m_kv_w_uv, m_mlp_norm, m_mlp_w_up, m_mlp_w_down, m_final_norm, v_hgrn_norm, v_hgrn_w_q, v_hgrn_w_f, v_hgrn_w_i, v_hgrn_w_g, v_hgrn_g_norm, v_hgrn_w_o, v_hgrn_lb_logits, v_mla_norm, v_mla_w_dq, v_mla_q_norm, v_mla_w_uq, v_mla_w_o, v_kv_in_norm, v_kv_w_dkv, v_kv_norm, v_kv_w_uk, v_kv_w_uv, v_mlp_norm, v_mlp_w_up, v_mlp_w_down, v_final_norm):
    given = dict(locals())
    wsh = {n: given[n] for n in WEIGHTS}
    msh = {n: given["m_" + n] for n in WEIGHTS}
    vsh = {n: given["v_" + n] for n in WEIGHTS}
    xi, yi, ci = _me()
    chip = 2 * xi + yi
    cq = jnp.stack([ci, chip]).astype(jnp.int32)

    small_w = {n: wsh[n].reshape(r, -1) for n, _, r, _ in SMALL}
    sv = jnp.concatenate([small_w["hgrn_norm"], small_w["hgrn_lb_logits"], jnp.zeros((5, 256), f32)], axis=0)
    g4, sv4 = _all_gather_weights(_pack_shards(wsh, W_EARLY, 0), sv)
    w = _unpack_early(g4)
    w["hgrn_norm"] = sv4[:, 0, :].reshape(1, D_MODEL)
    w["hgrn_lb_logits"] = sv4[:, 1:3, :].transpose(1, 0, 2).reshape(2, D_MODEL)
    for n in ("hgrn_g_norm", "mla_norm", "mla_q_norm", "kv_in_norm", "kv_norm", "mlp_norm", "final_norm"):
        w[n] = small_w[n]

    class Comm:
        late_shard = _pack_shards(wsh, W_LATE, PACK_PAD)
        unpack_late = staticmethod(_unpack_late)
        received_early = None

        @staticmethod
        def reduce_early(grads):
            gp = _pack_grads_early(grads)
            Comm.s32_early, s16 = _sum_over_cores(gp, _send_half_to_sibling(gp, "grads_to_sibling_early"), cq,
                                                  "grads_sum_cores_early")
            return s16

    loss, grad_x, g = _local_step(x.reshape(-1, D_MODEL), loss_target.reshape(-1, D_MODEL), w, Comm)

    tot_early = _exchange_halves(_sum_over_chips(Comm.s32_early, Comm.received_early, cq, "grads_sum_chips_early"),
                                 "grads_exchange_halves_early")
    gp = _pack_grads_late(g)
    s32, s16 = _sum_over_cores(gp, _send_half_to_sibling(gp, "grads_to_sibling_late"), cq, "grads_sum_cores_late")
    from_chips, smalls = _exchange_chips(s16, _pack_small(g, loss))
    tot_late = _exchange_halves(_sum_over_chips(s32, from_chips, cq, "grads_sum_chips_late"), "grads_exchange_halves_late")
    small_tot = _sum_small(smalls)
    loss = small_tot[LOSS_ROW, 0]

    grad, delta, new_m, new_v = {}, {}, {}, {}
    where = [(n, tot_late, G_LATE_OFF[n]) for n, _ in G_LATE] + [(n, tot_early, G_EARLY_OFF[n]) for n, _ in G_EARLY]
    for n, total, (o, r) in where:
        shp = wsh[n].shape
        two_d = (-1, shp[-1])
        grad[n] = total[o:o + r].reshape(shp)
        d, m2, v2 = _adamw(wsh[n].reshape(two_d), grad[n].reshape(two_d), msh[n].reshape(two_d), vsh[n].reshape(two_d),
                           "adamw_" + n)
        delta[n], new_m[n], new_v[n] = d.reshape(shp), m2.reshape(shp), v2.reshape(shp)
    items = []
    for n, row, r, wd in SMALL:
        gs = small_tot[row:row + r, :wd]
        if n in ("hgrn_norm", "hgrn_lb_logits"):
            gs = lax.dynamic_slice(gs, (0, 256 * chip), (r, 256))
        grad[n] = gs.reshape(wsh[n].shape)
        items.append((small_w[n], gs, msh[n].reshape(gs.shape), vsh[n].reshape(gs.shape)))
    for (n, _, _, _), (d, m2, v2) in zip(SMALL, _adamw_small(items)):
        shp = wsh[n].shape
        delta[n], new_m[n], new_v[n] = d.reshape(shp), m2.reshape(shp), v2.reshape(shp)

    return (loss, grad_x.reshape(x.shape), *[grad[n] for n in WEIGHTS], *[delta[n] for n in WEIGHTS],
            *[new_m[n] for n in WEIGHTS], *[new_v[n] for n in WEIGHTS])
```

```python
import functools

import jax
import jax.numpy as jnp
from jax import lax
from jax.experimental import pallas as pl
from jax.experimental.pallas import tpu as pltpu

f32, bf16 = jnp.float32, jnp.bfloat16
HI = lax.Precision.HIGHEST
MESH = pl.DeviceIdType.MESH

D_MODEL = 1024
D_FF = 4096
EPS = 1e-6
HGRN_HEADS, HGRN_DK, HGRN_CHUNK, HGRN_SUB = 8, 128, 64, 16
MLA_HEADS, MLA_NOPE, MLA_ROPE, MLA_V = 16, 128, 64, 128
MLA_Q_LORA, MLA_KV_LORA = 256, 256
ROPE_THETA = 10000.0
ATT_SCALE = (MLA_NOPE + MLA_ROPE) ** -0.5
EXP_CLAMP = 80.0

ADAM_LR, ADAM_B1, ADAM_B2, ADAM_EPS, ADAM_WD, ADAM_STEP = 0.001, 0.9, 0.999, 1e-08, 0.01, 10

V7X_VMEM_BYTES = 64 * 1024 * 1024
VMEM_LIMIT = V7X_VMEM_BYTES - 8 * 1024 * 1024
LANES = 128

PACK_PAD = 16
W_EARLY = (("hgrn_w_q", None, 256), ("hgrn_w_f", None, 256), ("hgrn_w_i", None, 256), ("hgrn_w_g", None, 256),
           ("hgrn_w_o", None, 256), ("mlp_w_up", 0, 1024), ("mlp_w_down", 0, 1024))
W_LATE = (("mla_w_dq", None, 64), ("mla_w_uq", None, 192), ("mla_w_o", None, 512), ("kv_w_dkv", None, 80),
          ("kv_w_uk", None, 128), ("kv_w_uv", None, 128), ("mlp_w_up", 1, 1024), ("mlp_w_down", 1, 1024))
G_LATE = (("hgrn_w_q", 256), ("hgrn_w_f", 256), ("hgrn_w_i", 256), ("hgrn_w_g", 256))
G_EARLY = (("hgrn_w_o", 256), ("mla_w_dq", 64), ("mla_w_uq", 192), ("mla_w_o", 512), ("kv_w_dkv", 80),
           ("kv_w_uk", 128), ("kv_w_uv", 128), ("mlp_w_up", 2048), ("mlp_w_down", 2048))


def _offsets(layout):
    out, o = {}, 0
    for entry in layout:
        out[entry[:-1] if len(entry) == 3 else entry[0]] = (o, entry[-1])
        o += entry[-1]
    return out, o


W_EARLY_OFF, W_EARLY_ROWS = _offsets(W_EARLY)
W_LATE_OFF, _w_late = _offsets(W_LATE)
W_LATE_ROWS = _w_late + PACK_PAD
G_LATE_OFF, G_LATE_ROWS = _offsets(G_LATE)
G_EARLY_OFF, _g_early = _offsets(G_EARLY)
G_EARLY_ROWS = _g_early + PACK_PAD
assert all(r % 32 == 0 for r in (W_EARLY_ROWS, W_LATE_ROWS, G_LATE_ROWS, G_EARLY_ROWS))

WEIGHTS = ("hgrn_norm", "hgrn_w_q", "hgrn_w_f", "hgrn_w_i", "hgrn_w_g", "hgrn_g_norm", "hgrn_w_o", "hgrn_lb_logits",
           "mla_norm", "mla_w_dq", "mla_q_norm", "mla_w_uq", "mla_w_o", "kv_in_norm", "kv_w_dkv", "kv_norm", "kv_w_uk",
           "kv_w_uv", "mlp_norm", "mlp_w_up", "mlp_w_down", "final_norm")
SMALL = (("hgrn_norm", 0, 1, 1024), ("hgrn_lb_logits", 1, 2, 1024), ("hgrn_g_norm", 3, 1, 128),
         ("mla_norm", 4, 1, 1024), ("mla_q_norm", 5, 1, 256), ("kv_in_norm", 6, 1, 1024), ("kv_norm", 7, 1, 256),
         ("mlp_norm", 8, 2, 1024), ("final_norm", 10, 1, 1024))
SMALL_ROWS = 16


def _pc(body, *, name, out_shape, grid=None, in_specs=None, out_specs=None, scratch=(), sem=None, grid_spec=None,
        aliases=None):
    params = pltpu.CompilerParams(dimension_semantics=sem, vmem_limit_bytes=VMEM_LIMIT)
    if grid_spec is not None:
        return pl.pallas_call(body, name=name, out_shape=out_shape, grid_spec=grid_spec, compiler_params=params,
                              interpret=False)
    kw = {k: v for k, v in (("grid", grid), ("in_specs", in_specs), ("out_specs", out_specs),
                            ("input_output_aliases", aliases)) if v is not None}
    return pl.pallas_call(body, name=name, out_shape=out_shape, scratch_shapes=list(scratch), compiler_params=params,
                          interpret=False, **kw)


def _sds(shape, dtype):
    return jax.ShapeDtypeStruct(tuple(shape), dtype)


def _mm(a, b, *, name, ta=False, tb=False, outs=(f32,), epi=None, extras=(), tm=1024, tn=1024, tk=4096):
    m, k = (a.shape[1], a.shape[0]) if ta else a.shape
    n = b.shape[0] if tb else b.shape[1]
    tm, tn, tk = min(tm, m), min(tn, n), min(tk, k)
    assert m % tm == 0 and n % tn == 0 and k % tk == 0, (name, m, n, k)
    nk = k // tk
    a_spec = pl.BlockSpec((tk, tm), lambda i, j, kk: (kk, i)) if ta else pl.BlockSpec((tm, tk), lambda i, j, kk: (i, kk))
    b_spec = pl.BlockSpec((tn, tk), lambda i, j, kk: (j, kk)) if tb else pl.BlockSpec((tk, tn), lambda i, j, kk: (kk, j))
    e_specs = [pl.BlockSpec((tm, tn), lambda i, j, kk: (i, j)) if e.shape[1] == n else
               pl.BlockSpec((tm, e.shape[1]), lambda i, j, kk: (i, 0)) for e in extras]
    dn = (((0 if ta else 1,), (1 if tb else 0,)), ((), ()))
    n_e, n_o = len(extras), len(outs)

    def finish(r, e_refs, o_refs):
        res = epi(r, *[e[...] for e in e_refs]) if epi is not None else (r,)
        for o, v in zip(o_refs, res):
            o[...] = v.astype(o.dtype)

    def body(*refs):
        a_ref, b_ref = refs[0], refs[1]
        e_refs = refs[2:2 + n_e]
        o_refs = refs[2 + n_e:2 + n_e + n_o]
        prod = lax.dot_general(a_ref[...].astype(bf16), b_ref[...].astype(bf16), dn, preferred_element_type=f32)
        if nk == 1:
            finish(prod, e_refs, o_refs)
            return
        acc = refs[-1]
        kk = pl.program_id(2)

        @pl.when(kk == 0)
        def _():
            acc[...] = jnp.zeros_like(acc)

        acc[...] += prod

        @pl.when(kk == nk - 1)
        def _():
            finish(acc[...], e_refs, o_refs)

    out = _pc(body, name=name, grid=(m // tm, n // tn, nk),
              in_specs=[a_spec, b_spec] + e_specs,
              out_specs=[pl.BlockSpec((tm, tn), lambda i, j, kk: (i, j)) for _ in outs],
              out_shape=[_sds((m, n), dt) for dt in outs],
              scratch=[pltpu.VMEM((tm, tn), f32)] if nk > 1 else [],
              sem=("parallel", "parallel", "arbitrary"))(a, b, *extras)
    return out[0] if n_o == 1 else out


def _wgrad(a, b, name):
    return _mm(a, b, ta=True, name=name, outs=(bf16,))


def _rw(fn, rows, bcast, outs, accs=(), *, name, tr=256):
    t = rows[0].shape[0]
    tr = min(tr, t)
    assert t % tr == 0
    n_r, n_b, n_o, n_a = len(rows), len(bcast), len(outs), len(accs)

    def body(*refs):
        r_refs = refs[:n_r]
        b_refs = refs[n_r:n_r + n_b]
        o_refs = refs[n_r + n_b:n_r + n_b + n_o]
        a_refs = refs[n_r + n_b + n_o:]
        res = fn(*[r[...] for r in r_refs], *[b[...] for b in b_refs])
        for o, v in zip(o_refs, res[:n_o]):
            o[...] = v.astype(o.dtype)
        i = pl.program_id(0)
        for a_ref, v in zip(a_refs, res[n_o:]):
            @pl.when(i == 0)
            def _(a_ref=a_ref):
                a_ref[...] = jnp.zeros_like(a_ref)
            a_ref[...] += v

    in_specs = [pl.BlockSpec((tr, r.shape[1]), lambda i: (i, 0)) for r in rows]
    in_specs += [pl.BlockSpec(b.shape, lambda i: (0, 0)) for b in bcast]
    out_specs = [pl.BlockSpec((tr, w), lambda i: (i, 0)) for w, _ in outs]
    out_specs += [pl.BlockSpec(s, lambda i: (0, 0)) for s in accs]
    out_shape = [_sds((t, w), dt) for w, dt in outs] + [_sds(s, f32) for s in accs]
    res = _pc(body, name=name, grid=(t // tr,), in_specs=in_specs, out_specs=out_specs, out_shape=out_shape,
              sem=("arbitrary",))(*rows, *bcast)
    return res


def _rms(x, gain):
    return x * lax.rsqrt(jnp.mean(x * x, axis=-1, keepdims=True) + EPS) * gain


def _rms_bwd(x, gain, dy):
    _, vjp = jax.vjp(_rms, x, gain)
    return vjp(dy)


def _lower_bound(lbl):
    l0, l1 = lbl[0:1, :], lbl[1:2, :]
    mx = jnp.maximum(l0, l1)
    e0, e1 = jnp.exp(l0 - mx), jnp.exp(l1 - mx)
    return e0 / (e0 + e1)


def _gates(qpre, fpre, lbl):
    lb = _lower_bound(lbl)
    q = jax.nn.silu(qpre)
    forget = lb + (1.0 - lb) * jax.nn.sigmoid(fpre)
    return q, 1.0 - forget, jnp.log(forget)


def _head_norm_gate(o, gpre, gn):
    return _rms(o, gn) * jax.nn.silu(gpre)


def _swap_halves(x):
    w = x.shape[1]
    lane = lax.broadcasted_iota(jnp.int32, x.shape, 1)
    return jnp.where((lane % MLA_ROPE) < MLA_ROPE // 2, pltpu.roll(x, w - MLA_ROPE // 2, 1),
                     pltpu.roll(x, MLA_ROPE // 2, 1))


def _tile_lanes(tab, w):
    return tab if w == tab.shape[1] else jnp.concatenate([tab] * (w // tab.shape[1]), axis=1)


def _rope(x, cos, sgn_sin, sign=1.0):
    w = x.shape[1]
    return x * _tile_lanes(cos, w) + sign * _swap_halves(x) * _tile_lanes(sgn_sin, w)


def _rope_heads(x, cos, sgn_sin, sign, scale):
    parts = []
    for h in range(x.shape[1] // (2 * LANES)):
        parts.append(x[:, 2 * LANES * h:2 * LANES * h + LANES] * scale)
        parts.append(_rope(x[:, 2 * LANES * h + LANES:2 * LANES * (h + 1)], cos, sgn_sin, sign) * scale)
    return jnp.concatenate(parts, axis=1)


def _bd(a, b, ca, cb):
    return lax.dot_general(a.astype(bf16), b.astype(bf16), (((ca,), (cb,)), ((), ())), preferred_element_type=f32)


@jax.custom_vjp
def _dot_nn(a, b):
    return _bd(a, b, 1, 0)


@jax.custom_vjp
def _dot_nt(a, b):
    return _bd(a, b, 1, 1)


@jax.custom_vjp
def _dot_tn(a, b):
    return _bd(a, b, 0, 0)


_dot_nn.defvjp(lambda a, b: (_bd(a, b, 1, 0), (a, b)), lambda r, g: (_bd(g, r[1], 1, 1), _bd(r[0], g, 0, 0)))
_dot_nt.defvjp(lambda a, b: (_bd(a, b, 1, 1), (a, b)), lambda r, g: (_bd(g, r[1], 1, 0), _bd(g, r[0], 0, 0)))
_dot_tn.defvjp(lambda a, b: (_bd(a, b, 0, 0), (a, b)), lambda r, g: (_bd(r[1], g, 1, 1), _bd(r[0], g, 1, 0)))


def _scan_rows(x, reverse):
    n = x.shape[0]
    row = lax.broadcasted_iota(jnp.int32, x.shape, 0)
    s = 1
    while s < n:
        if reverse:
            x = x + jnp.where(row < n - s, pltpu.roll(x, n - s, 0), 0.0)
        else:
            x = x + jnp.where(row >= s, pltpu.roll(x, s, 0), 0.0)
        s *= 2
    return x


@jax.custom_vjp
def _cumsum_rows(g):
    return _scan_rows(g, False)


_cumsum_rows.defvjp(lambda g: (_scan_rows(g, False), None), lambda _, ct: (_scan_rows(ct, True),))

HGRN_PAIRS = HGRN_HEADS // 2
HGRN_PAIR = 2 * HGRN_DK
GLA_STATE = (HGRN_PAIRS, HGRN_PAIR, HGRN_PAIR)


def _gla_consts():
    s = HGRN_SUB
    r = lax.broadcasted_iota(jnp.int32, (HGRN_PAIR, HGRN_PAIR), 0)
    c = lax.broadcasted_iota(jnp.int32, (HGRN_PAIR, HGRN_PAIR), 1)
    pair_mask = (r < HGRN_DK) == (c < HGRN_DK)
    masks = []
    for i in range(HGRN_CHUNK // s):
        n = s * (i + 1)
        row = lax.broadcasted_iota(jnp.int32, (HGRN_HEADS * s, HGRN_HEADS * n), 0)
        col = lax.broadcasted_iota(jnp.int32, (HGRN_HEADS * s, HGRN_HEADS * n), 1)
        col_head = sum((col >= m * n).astype(jnp.int32) for m in range(1, HGRN_HEADS))
        masks.append((col_head == row // s) & (col - col_head * n <= s * i + row % s))
    return pair_mask, masks


def _heads_to_rows(x):
    return jnp.concatenate([x[:, HGRN_DK * h:HGRN_DK * (h + 1)] for h in range(HGRN_HEADS)], axis=0)


def _gla_chunk(consts, dots, q, k, v, g, st):
    pair_mask, masks = consts
    dot_nn, dot_nt, dot_tn = dots
    c, s = HGRN_CHUNK, HGRN_SUB
    b = _cumsum_rows(g)
    b_last = b[c - 1:c, :]
    q_in, k_out = q * jnp.exp(b), k * jnp.exp(b_last - b)
    o_inter, st_new = [], []
    for p in range(HGRN_PAIRS):
        cols = slice(HGRN_PAIR * p, HGRN_PAIR * (p + 1))
        o_inter.append(dot_nt(q_in[:, cols], st[p]))
        st_new.append(st[p] * jnp.exp(b_last[:, cols]) + jnp.where(pair_mask, dot_tn(v[:, cols], k_out[:, cols]), 0.0))
    intra = []
    for i in range(c // s):
        n = s * (i + 1)
        ref = b[s * i - 1:s * i, :] if i else jnp.zeros_like(b_last)
        qt = _heads_to_rows(q[s * i:n] * jnp.exp(b[s * i:n] - ref))
        kt = _heads_to_rows(k[:n] * jnp.exp(jnp.minimum(ref - b[:n], EXP_CLAMP)))
        sc = jnp.where(masks[i], dot_nt(qt, kt), 0.0)
        oi = dot_nn(sc, _heads_to_rows(v[:n]))
        intra.append(jnp.concatenate([oi[s * h:s * (h + 1)] for h in range(HGRN_HEADS)], axis=1))
    return jnp.concatenate(o_inter, axis=1) + jnp.concatenate(intra, axis=0), st_new


_PLAIN_DOTS = (lambda a, b: _bd(a, b, 1, 0), lambda a, b: _bd(a, b, 1, 1), lambda a, b: _bd(a, b, 0, 0))
_VJP_DOTS = (_dot_nn, _dot_nt, _dot_tn)


def _hgrn_mix(consts, dots, qpre, fpre, v, gpre, lbl, gn, st):
    q, k, g = _gates(qpre, fpre, lbl)
    o, st_new = _gla_chunk(consts, dots, q, k, v, g, st)
    y = [_head_norm_gate(o[:, HGRN_DK * h:HGRN_DK * (h + 1)], gpre[:, HGRN_DK * h:HGRN_DK * (h + 1)], gn)
         for h in range(HGRN_HEADS)]
    return jnp.concatenate(y, axis=1), st_new


def _gla_fwd(p4, lbl, gn, gather=None):
    t = p4.shape[0]
    nc = t // HGRN_CHUNK

    def body(q_ref, k_ref, v_ref, g_ref, lbl_ref, gn_ref, *rest):
        if gather is None:
            o_ref, s_ref, st = rest
        else:
            wp_ref, _, o_ref, s_ref, gathered_ref, st, sems = rest

        @pl.when(pl.program_id(0) == 0)
        def _():
            st[...] = jnp.zeros_like(st)
            if gather is not None:
                _gather_start(wp_ref, gathered_ref, sems)

        if gather is not None:
            @pl.when(pl.program_id(0) == nc - 1)
            def _():
                _gather_finish(wp_ref, gathered_ref, sems)

        s_in = [st[p] for p in range(HGRN_PAIRS)]
        y, st_new = _hgrn_mix(_gla_consts(), _PLAIN_DOTS, q_ref[...], k_ref[...], v_ref[...], g_ref[...],
                              lbl_ref[...], gn_ref[...], s_in)
        o_ref[...] = y.astype(o_ref.dtype)
        for p in range(HGRN_PAIRS):
            s_ref[0, p] = s_in[p]
            st[p] = st_new[p]

    blk = lambda off: pl.BlockSpec((HGRN_CHUNK, D_MODEL), lambda c: (c, off))
    whole = lambda a: pl.BlockSpec(a.shape, lambda c: (0, 0))
    state_shape = GLA_STATE
    in_specs = [blk(0), blk(1), blk(2), blk(3), whole(lbl), whole(gn)]
    out_specs = [blk(0), pl.BlockSpec((1,) + state_shape, lambda c: (c, 0, 0, 0))]
    out_shape = [_sds((t, D_MODEL), bf16), _sds((nc,) + state_shape, f32)]
    scratch = [pltpu.VMEM(state_shape, f32)]
    if gather is None:
        return _pc(body, name="gla_fwd", grid=(nc,), in_specs=in_specs, out_specs=out_specs, out_shape=out_shape,
                   scratch=scratch, sem=("arbitrary",))(p4, p4, p4, p4, lbl, gn)
    return _pc(body, name="gla_fwd_gather", grid=(nc,), in_specs=in_specs + [HBM, HBM], out_specs=out_specs + [HBM],
               out_shape=out_shape + [_sds((4,) + gather.shape, bf16)], aliases={7: 2},
               scratch=scratch + [pltpu.SemaphoreType.DMA((2, 6))], sem=("arbitrary",))(
                   p4, p4, p4, p4, lbl, gn, gather, _gather_base(gather))


def _gla_bwd(p4, lbl, gn, states, dy, exchange=None):
    t = p4.shape[0]
    nc = t // HGRN_CHUNK

    def body(q_ref, k_ref, v_ref, g_ref, lbl_ref, gn_ref, s_ref, dy_ref, *rest):
        if exchange is None:
            dp_ref, dlbl_ref, dgn_ref, dst = rest
        else:
            sb_ref, dp_ref, dlbl_ref, dgn_ref, recv_ref, dst, sems = rest

        @pl.when(pl.program_id(0) == 0)
        def _():
            dst[...] = jnp.zeros_like(dst)
            dlbl_ref[...] = jnp.zeros_like(dlbl_ref)
            dgn_ref[...] = jnp.zeros_like(dgn_ref)
            if exchange is not None:
                _chips_start(sb_ref, recv_ref, sems)

        if exchange is not None:
            @pl.when(pl.program_id(0) == nc - 1)
            def _():
                _chips_finish(sb_ref, recv_ref, sems)

        consts = _gla_consts()
        fn = lambda *args: _hgrn_mix(consts, _VJP_DOTS, *args)
        pairs = range(HGRN_PAIRS)
        _, vjp = jax.vjp(fn, q_ref[...], k_ref[...], v_ref[...], g_ref[...], lbl_ref[...], gn_ref[...],
                         [s_ref[0, p] for p in pairs])
        *d_proj, dlbl, dgn, ds = vjp((dy_ref[...], [dst[p] for p in pairs]))
        for i, d in enumerate(d_proj):
            dp_ref[:, D_MODEL * i:D_MODEL * (i + 1)] = d.astype(dp_ref.dtype)
        dlbl_ref[...] += dlbl
        dgn_ref[...] += dgn
        for p in pairs:
            dst[p] = ds[p]

    blk = lambda off: pl.BlockSpec((HGRN_CHUNK, D_MODEL), lambda c: (nc - 1 - c, off))
    whole = lambda a: pl.BlockSpec(a.shape, lambda c: (0, 0))
    state_shape = GLA_STATE
    in_specs = [blk(0), blk(1), blk(2), blk(3), whole(lbl), whole(gn),
                pl.BlockSpec((1,) + state_shape, lambda c: (nc - 1 - c, 0, 0, 0)), blk(0)]
    out_specs = [pl.BlockSpec((HGRN_CHUNK, 4 * D_MODEL), lambda c: (nc - 1 - c, 0)), whole(lbl), whole(gn)]
    out_shape = [_sds((t, 4 * D_MODEL), bf16), _sds(lbl.shape, f32), _sds(gn.shape, f32)]
    scratch = [pltpu.VMEM(state_shape, f32)]
    if exchange is None:
        return _pc(body, name="gla_bwd", grid=(nc,), in_specs=in_specs, out_specs=out_specs, out_shape=out_shape,
                   scratch=scratch, sem=("arbitrary",))(p4, p4, p4, p4, lbl, gn, states, dy)
    return _pc(body, name="gla_bwd_exchange", grid=(nc,), in_specs=in_specs + [HBM], out_specs=out_specs + [HBM],
               out_shape=out_shape + [_sds((3,) + exchange.shape[1:], bf16)],
               scratch=scratch + [pltpu.SemaphoreType.DMA((2, 3))], sem=("arbitrary",))(
                   p4, p4, p4, p4, lbl, gn, states, dy, exchange)


ATT_FWD_TQ, ATT_FWD_TK = 1024, 1024
ATT_BWD_TQ, ATT_BWD_TK = 1024, 512
ATT_QK = 2 * LANES
NEG = -1e30


def _pair_masks(shape):
    lane = lax.broadcasted_iota(jnp.int32, shape, 1)
    return lane < MLA_ROPE, lane >= MLA_ROPE


def _causal(shape, row0, col0):
    row = row0 + lax.broadcasted_iota(jnp.int32, shape, 0)
    col = col0 + lax.broadcasted_iota(jnp.int32, shape, 1)
    return col <= row


def _qk_cols(e):
    return slice(ATT_QK * e, ATT_QK * (e + 1))


def _v_cols(e):
    return slice(MLA_V * e, MLA_V * (e + 1))


def _attn_fwd(qc, kc, v):
    t = qc.shape[0]
    tq, tk = min(ATT_FWD_TQ, t), min(ATT_FWD_TK, t)
    npair = MLA_HEADS // 2

    def body(q_ref, k_ref, v_ref, o_ref, lse_ref):
        i = pl.program_id(1)
        n_full = (i * tq + 1) // tk
        nkv = (i * tq + tq + tk - 1) // tk
        q = [q_ref[:, _qk_cols(e)] for e in range(2)]

        def step(j, carry, masked):
            ks = pl.ds(pl.multiple_of(j * tk, tk), tk)
            ok = _causal((tq, tk), i * tq, j * tk) if masked else None
            new = []
            for e in range(2):
                m, l, acc = carry[e]
                s = _bd(q[e], k_ref[ks, _qk_cols(e)], 1, 1)
                if masked:
                    s = jnp.where(ok, s, NEG)
                m_new = jnp.maximum(m, jnp.max(s, axis=-1, keepdims=True))
                p = jnp.exp(s - m_new)
                alpha = jnp.exp(m - m_new)
                l = alpha * l + jnp.sum(p, axis=-1, keepdims=True)
                acc = alpha * acc + _bd(p, v_ref[ks, _v_cols(e)], 1, 0)
                new.append((m_new, l, acc))
            return tuple(new)

        one = (jnp.full((tq, 1), NEG, f32), jnp.zeros((tq, 1), f32), jnp.zeros((tq, MLA_V), f32))
        carry = lax.fori_loop(0, n_full, functools.partial(step, masked=False), (one, one))
        carry = lax.fori_loop(n_full, nkv, functools.partial(step, masked=True), carry)
        o_ref[...] = jnp.concatenate([acc / l for _, l, acc in carry], axis=1).astype(o_ref.dtype)
        lo, _ = _pair_masks((tq, LANES))
        lse_ref[...] = jnp.where(lo, *[m + jnp.log(l) for m, l, _ in carry])

    return _pc(body, name="attn_fwd", grid=(npair, t // tq),
               in_specs=[pl.BlockSpec((tq, 2 * ATT_QK), lambda p, i: (i, p)),
                         pl.BlockSpec((t, 2 * ATT_QK), lambda p, i: (0, p)),
                         pl.BlockSpec((t, 2 * MLA_V), lambda p, i: (0, p))],
               out_specs=[pl.BlockSpec((tq, 2 * MLA_V), lambda p, i: (i, p)),
                          pl.BlockSpec((tq, LANES), lambda p, i: (i, p))],
               out_shape=[_sds((t, MLA_HEADS * MLA_V), bf16), _sds((t, npair * LANES), f32)],
               sem=("arbitrary", "arbitrary"))(qc, kc, v)


def _attn_bwd(qc, kc, v, do, lse, delta):
    t = qc.shape[0]
    tq, tk = min(ATT_BWD_TQ, t), min(ATT_BWD_TK, t)
    npair = MLA_HEADS // 2
    nq = t // tq

    def body(q_ref, do_ref, lse_ref, dl_ref, k_ref, v_ref, dq_ref, dk_ref, dv_ref):
        j = pl.program_id(1)

        @pl.when(j == 0)
        def _():
            dq_ref[...] = jnp.zeros_like(dq_ref)

        k = [k_ref[:, _qk_cols(e)] for e in range(2)]
        vv = [v_ref[:, _v_cols(e)] for e in range(2)]

        def step(i, carry, masked):
            qs = pl.ds(pl.multiple_of(i * tq, tq), tq)
            ok = _causal((tq, tk), i * tq, j * tk) if masked else None
            lse2, dl2 = lse_ref[qs, :], dl_ref[qs, :]
            new = []
            for e in range(2):
                dk, dv = carry[e]
                q_e, do_e = q_ref[qs, _qk_cols(e)], do_ref[qs, _v_cols(e)]
                p = jnp.exp(_bd(q_e, k[e], 1, 1) - lse2[:, MLA_ROPE * e:MLA_ROPE * e + 1])
                if masked:
                    p = jnp.where(ok, p, 0.0)
                dv = dv + _bd(p, do_e, 0, 0)
                dp = _bd(do_e, vv[e], 1, 1)
                ds = (p * (dp - dl2[:, MLA_ROPE * e:MLA_ROPE * e + 1])).astype(bf16)
                dk = dk + _bd(ds, q_e, 0, 0)
                dq_ref[qs, _qk_cols(e)] += _bd(ds, k[e], 1, 0)
                new.append((dk, dv))
            return tuple(new)

        one = (jnp.zeros((tk, ATT_QK), f32), jnp.zeros((tk, MLA_V), f32))
        i_full = jnp.minimum((j * tk + tk + tq - 2) // tq, nq)
        carry = lax.fori_loop((j * tk) // tq, i_full, functools.partial(step, masked=True), (one, one))
        carry = lax.fori_loop(i_full, nq, functools.partial(step, masked=False), carry)
        for e in range(2):
            dk_ref[:, _qk_cols(e)] = carry[e][0].astype(dk_ref.dtype)
            dv_ref[:, _v_cols(e)] = carry[e][1].astype(dv_ref.dtype)

    res = lambda w: pl.BlockSpec((t, w), lambda p, j: (0, p))
    blk = lambda w: pl.BlockSpec((tk, w), lambda p, j: (j, p))
    return _pc(body, name="attn_bwd", grid=(npair, t // tk),
               in_specs=[res(2 * ATT_QK), res(2 * MLA_V), res(LANES), res(LANES), blk(2 * ATT_QK), blk(2 * MLA_V)],
               out_specs=[res(2 * ATT_QK), blk(2 * ATT_QK), blk(2 * MLA_V)],
               out_shape=[_sds((t, MLA_HEADS * ATT_QK), f32), _sds((t, MLA_HEADS * ATT_QK), bf16),
                          _sds((t, MLA_HEADS * MLA_V), bf16)],
               sem=("arbitrary", "arbitrary"))(qc, do, lse, delta, kc, v)


def _rope_tables(t):
    half = MLA_ROPE // 2
    inv_freq = ROPE_THETA ** (-jnp.arange(half, dtype=f32) / half)
    ang = jnp.arange(t, dtype=f32)[:, None] * inv_freq[None, :]
    cos, sin = jnp.cos(ang), jnp.sin(ang)
    return jnp.concatenate([cos, cos] * 2, axis=1), jnp.concatenate([-sin, sin] * 2, axis=1)


def _relu2_epi(u):
    r = jnp.maximum(u, 0.0)
    return u, r * r


def _add_epi(r, res):
    return (r + res,)


def _drelu2_epi(da, u):
    return (da * 2.0 * jnp.maximum(u.astype(f32), 0.0),)


def _mlp_fwd(h, gain, w_up, w_down, tag):
    xm = _rw(lambda x, g: (_rms(x, g),), [h], [gain], [(D_MODEL, bf16)], name=f"mlp{tag}_norm")[0]
    u, a = _mm(xm, w_up, name=f"mlp{tag}_up", outs=(bf16, bf16), epi=_relu2_epi)
    h_out = _mm(a, w_down, name=f"mlp{tag}_down", epi=_add_epi, extras=(h,))
    return h_out, (xm, u, a)


def _mlp_bwd(dh, dh16, h, gain, w_up, w_down, saved, tag):
    xm, u, a = saved
    du = _mm(dh16, w_down, tb=True, name=f"mlp{tag}_dact", outs=(bf16,), epi=_drelu2_epi, extras=(u,))
    d_down = _wgrad(a, dh16, f"mlp{tag}_dwdown")
    d_up = _wgrad(xm, du, f"mlp{tag}_dwup")
    dxm = _mm(du, w_up, tb=True, name=f"mlp{tag}_dxm")

    def fn(x, dy, dres, g):
        dx, dg = _rms_bwd(x, g, dy)
        return dx + dres, dx + dres, dg

    dh_in, dh_in16, d_gain = _rw(fn, [h, dxm, dh], [gain], [(D_MODEL, f32), (D_MODEL, bf16)], [(1, D_MODEL)],
                                 name=f"mlp{tag}_dnorm")
    return dh_in, dh_in16, d_gain, d_up, d_down


def _local_step(x, target, w, comm=None):
    w = dict(w)
    t = x.shape[0]
    cos, sgn_sin = _rope_tables(t)
    grads = {}

    xn0 = _rw(lambda xx, g: (_rms(xx, g),), [x], [w["hgrn_norm"]], [(D_MODEL, bf16)], name="hgrn_norm")[0]
    p4 = _mm(xn0, w["hgrn_w4"], name="hgrn_proj", tn=2048)

    if comm is None:
        y, states = _gla_fwd(p4, w["hgrn_lb_logits"], w["hgrn_g_norm"])
    else:
        y, states, gathered = _gla_fwd(p4, w["hgrn_lb_logits"], w["hgrn_g_norm"], gather=comm.late_shard)
        w.update(comm.unpack_late(gathered))
    h1 = _mm(y, w["hgrn_w_o"], name="hgrn_out", epi=_add_epi, extras=(x,))
    h2, mlp0 = _mlp_fwd(h1, w["mlp_norm"][0:1], w["mlp_w_up", 0], w["mlp_w_down", 0], 0)

    hk, xn1 = _rw(lambda hh, g1, g2: (_rms(hh, g1), _rms(hh, g2)), [h2], [w["kv_in_norm"], w["mla_norm"]],
                  [(D_MODEL, bf16)] * 2, name="kv_mla_norm")
    ckr = _mm(hk, w["kv_w_dkv"], name="kv_down")

    def ckv_fn(c, cs, sn, g):
        kr = _rope(c[:, MLA_KV_LORA:], cs, sn)
        return _rms(c[:, :MLA_KV_LORA], g), jnp.concatenate([jnp.zeros_like(kr), kr], axis=1)

    c_kv, kr_head = _rw(ckv_fn, [ckr, cos, sgn_sin], [w["kv_norm"]], [(MLA_KV_LORA, bf16), (ATT_QK, f32)],
                        name="kv_norm_rope")
    kc = _mm(c_kv, w["kv_w_kcat"], name="kv_up_k", outs=(bf16,), extras=(kr_head,),
             epi=lambda r, kr: (r + _tile_lanes(kr, r.shape[1]),))
    v_att = _mm(c_kv, w["kv_w_uv"], name="kv_up_v", outs=(bf16,))
    cq0 = _mm(xn1, w["mla_w_dq"], name="q_down")
    c_q = _rw(lambda c, g: (_rms(c, g),), [cq0], [w["mla_q_norm"]], [(MLA_Q_LORA, bf16)], name="q_norm")[0]
    qc = _mm(c_q, w["mla_w_qcat"], name="q_up", outs=(bf16,), extras=(cos, sgn_sin),
             epi=lambda r, cs, sn: (_rope_heads(r, cs, sn, 1.0, ATT_SCALE),))
    o_att, lse = _attn_fwd(qc, kc, v_att)
    h3 = _mm(o_att, w["mla_w_o"], name="mla_out", epi=_add_epi, extras=(h2,))
    h4, mlp1 = _mlp_fwd(h3, w["mlp_norm"][1:2], w["mlp_w_up", 1], w["mlp_w_down", 1], 1)

    def loss_fn(hh, tgt, gain):
        def f(a, b):
            e = _rms(a, b) - tgt
            return 0.5 * jnp.sum(jnp.sum(e * e, axis=-1, keepdims=True) / D_MODEL, axis=0, keepdims=True)
        val, vjp = jax.vjp(f, hh, gain)
        dh, dg = vjp(jnp.ones((1, 1), f32))
        return dh, dh, jnp.broadcast_to(val, (1, LANES)), dg

    dh4, dh4_16, loss_acc, grads["final_norm"] = _rw(loss_fn, [h4, target], [w["final_norm"]],
                                                     [(D_MODEL, f32), (D_MODEL, bf16)], [(1, LANES), (1, D_MODEL)],
                                                     name="loss")
    loss = loss_acc[0, 0]

    dh3, dh3_16, g_n1, g_up1, g_dn1 = _mlp_bwd(dh4, dh4_16, h3, w["mlp_norm"][1:2], w["mlp_w_up", 1],
                                               w["mlp_w_down", 1], mlp1, 1)
    do_att = _mm(dh3_16, w["mla_w_o"], tb=True, name="mla_dout", outs=(bf16,))
    grads["mla_w_o"] = _wgrad(o_att, dh3_16, "mla_dwo")

    def delta_fn(a, b):
        prod = a.astype(f32) * b.astype(f32)
        outs = []
        for p in range(MLA_HEADS // 2):
            d0 = jnp.sum(prod[:, 2 * p * LANES:(2 * p + 1) * LANES], axis=-1, keepdims=True)
            d1 = jnp.sum(prod[:, (2 * p + 1) * LANES:(2 * p + 2) * LANES], axis=-1, keepdims=True)
            lo, _ = _pair_masks((a.shape[0], LANES))
            outs.append(jnp.where(lo, d0, d1))
        return (jnp.concatenate(outs, axis=1),)

    delta = _rw(delta_fn, [do_att, o_att], [], [(MLA_HEADS // 2 * LANES, f32)], name="attn_delta")[0]
    dqc, dkc, dv = _attn_bwd(qc, kc, v_att, do_att, lse, delta)
    dqf = _rw(lambda a, cs, sn: (_rope_heads(a, cs, sn, -1.0, ATT_SCALE),), [dqc, cos, sgn_sin], [],
              [(MLA_HEADS * ATT_QK, bf16)], name="dq_rope")[0]
    dc_q = _mm(dqf, w["mla_w_qcat"], tb=True, name="q_up_dx")
    grads["mla_w_qcat"] = _wgrad(c_q, dqf, "q_up_dw")

    def dqn_fn(c, dy, g):
        return _rms_bwd(c, g, dy)

    dcq0, grads["mla_q_norm"] = _rw(dqn_fn, [cq0, dc_q], [w["mla_q_norm"]], [(MLA_Q_LORA, bf16)], [(1, MLA_Q_LORA)],
                                    name="q_dnorm")
    dxn1 = _mm(dcq0, w["mla_w_dq"], tb=True, name="q_down_dx")
    grads["mla_w_dq"] = _wgrad(xn1, dcq0, "q_down_dw")

    dc_kv = _mm(dkc, w["kv_w_kcat"], tb=True, name="kv_up_dx_k")
    dc_kv = _mm(dv, w["kv_w_uv"], tb=True, name="kv_up_dx_v", epi=_add_epi, extras=(dc_kv,))
    grads["kv_w_kcat"] = _wgrad(c_kv, dkc, "kv_up_dw_k")
    grads["kv_w_uv"] = _wgrad(c_kv, dv, "kv_up_dw_v")

    def dckr_fn(c, dc, dk_heads, cs, sn, g):
        tot = dk_heads[:, LANES:ATT_QK].astype(f32)
        for h in range(1, MLA_HEADS):
            tot = tot + dk_heads[:, ATT_QK * h + LANES:ATT_QK * (h + 1)].astype(f32)
        lo, _ = _pair_masks(tot.shape)
        dkr = jnp.where(lo, _rope(tot, cs, sn, -1.0), 0.0)
        dcc, dg = _rms_bwd(c[:, :MLA_KV_LORA], g, dc)
        return jnp.concatenate([dcc, dkr], axis=1), dg

    dckr, grads["kv_norm"] = _rw(dckr_fn, [ckr, dc_kv, dkc, cos, sgn_sin], [w["kv_norm"]],
                                 [(MLA_KV_LORA + LANES, bf16)], [(1, MLA_KV_LORA)], name="kv_dnorm_rope")
    dhk = _mm(dckr, w["kv_w_dkv"], tb=True, name="kv_down_dx")
    grads["kv_w_dkv"] = _wgrad(hk, dckr, "kv_down_dw")

    def dh2_fn(hh, d1, d2, dres, g1, g2):
        a, ga = _rms_bwd(hh, g1, d1)
        b, gb = _rms_bwd(hh, g2, d2)
        return a + b + dres, a + b + dres, ga, gb

    dh2, dh2_16, grads["kv_in_norm"], grads["mla_norm"] = _rw(
        dh2_fn, [h2, dhk, dxn1, dh3], [w["kv_in_norm"], w["mla_norm"]], [(D_MODEL, f32), (D_MODEL, bf16)],
        [(1, D_MODEL)] * 2, name="kv_mla_dnorm")

    dh1, dh1_16, g_n0, g_up0, g_dn0 = _mlp_bwd(dh2, dh2_16, h1, w["mlp_norm"][0:1], w["mlp_w_up", 0],
                                               w["mlp_w_down", 0], mlp0, 0)
    grads["mlp_norm"] = jnp.concatenate([g_n0, g_n1], axis=0)
    grads["mlp_w_up"] = (g_up0, g_up1)
    grads["mlp_w_down"] = (g_dn0, g_dn1)
    dy = _mm(dh1_16, w["hgrn_w_o"], tb=True, name="hgrn_dout")
    grads["hgrn_w_o"] = _wgrad(y, dh1_16, "hgrn_dwo")

    gla_args = (p4, w["hgrn_lb_logits"], w["hgrn_g_norm"], states, dy)
    if comm is None:
        dp4, grads["hgrn_lb_logits"], grads["hgrn_g_norm"] = _gla_bwd(*gla_args)
    else:
        dp4, grads["hgrn_lb_logits"], grads["hgrn_g_norm"], comm.received_early = _gla_bwd(
            *gla_args, exchange=comm.reduce_early(grads))
    dxn0 = _mm(dp4, w["hgrn_w4"], tb=True, name="hgrn_proj_dx")
    grads["hgrn_w4"] = _mm(xn0, dp4, ta=True, name="hgrn_proj_dw")

    def dx_fn(xx, dyy, dres, gn):
        dxx, dgn = _rms_bwd(xx, gn, dyy)
        return dxx + dres, dgn

    grad_x, grads["hgrn_norm"] = _rw(dx_fn, [x, dxn0, dh1], [w["hgrn_norm"]], [(D_MODEL, f32)], [(1, D_MODEL)],
                                     name="hgrn_dnorm")
    return loss, grad_x, grads


HBM = pl.BlockSpec(memory_space=pltpu.HBM)


def _me():
    return lax.axis_index("x"), lax.axis_index("y"), lax.axis_index("c")


def _flip(x, y, f):
    return (1 - x if f & 1 else x), (1 - y if f & 2 else y)


def _rcopy(src, dst, sems, k, dev):
    return pltpu.make_async_remote_copy(src_ref=src, dst_ref=dst, send_sem=sems.at[0, k], recv_sem=sems.at[1, k],
                                        device_id=dev, device_id_type=MESH)


def _my_half(rows, c, mine=True):
    half = rows // 2
    return pl.ds(pl.multiple_of((c if mine else 1 - c) * half, 16), half)


def _gather_start(wp_ref, out_ref, sems):
    x, y, c = _me()
    half = _my_half(wp_ref.shape[0], c)
    for f in (1, 2, 3):
        px, py = _flip(x, y, f)
        _rcopy(wp_ref.at[half], out_ref.at[2 * x + y, half], sems, f - 1, (px, py, c)).start()


def _gather_finish(wp_ref, out_ref, sems):
    x, y, c = _me()
    half, other = _my_half(wp_ref.shape[0], c), _my_half(wp_ref.shape[0], c, mine=False)
    sends = []
    for f in (1, 2, 3):
        px, py = _flip(x, y, f)
        landed = out_ref.at[2 * px + py, half]
        _rcopy(landed, landed, sems, f - 1, (px, py, c)).wait_recv()
        sends.append(_rcopy(landed, landed, sems, 2 + f, (x, y, 1 - c)))
        sends[-1].start()
    for f in (1, 2, 3):
        px, py = _flip(x, y, f)
        theirs = out_ref.at[2 * px + py, other]
        _rcopy(theirs, theirs, sems, 2 + f, (x, y, 1 - c)).wait_recv()
        sends.append(_rcopy(wp_ref.at[half], out_ref.at[2 * x + y, half], sems, f - 1, (px, py, c)))
    for cp in sends:
        cp.wait_send()


def _gather_base(wp):
    return jnp.broadcast_to(wp[None], (4,) + wp.shape)


def _all_gather_weights(wp, sv):
    def body(wp_ref, sv_ref, base_ref, out_ref, svs_ref, sems, local_sem):
        x, y, c = _me()
        mine = pltpu.make_async_copy(sv_ref, svs_ref.at[2 * x + y], local_sem)
        mine.start()
        _gather_start(wp_ref, out_ref, sems)
        small = []
        for f in (1, 2, 3):
            px, py = _flip(x, y, f)
            small.append(_rcopy(sv_ref, svs_ref.at[2 * x + y], sems, 5 + f, (px, py, c)))
            small[-1].start()
        _gather_finish(wp_ref, out_ref, sems)
        for f in (1, 2, 3):
            px, py = _flip(x, y, f)
            _rcopy(sv_ref, svs_ref.at[2 * px + py], sems, 5 + f, (px, py, c)).wait_recv()
        for cp in small:
            cp.wait_send()
        mine.wait()

    return _pc(body, name="weights_all_gather", in_specs=[HBM, HBM, HBM], out_specs=[HBM, HBM],
               out_shape=[_sds((4,) + wp.shape, bf16), _sds((4, 8, 256), f32)], aliases={2: 0},
               scratch=[pltpu.SemaphoreType.DMA((2, 9)), pltpu.SemaphoreType.DMA])(wp, sv, _gather_base(wp))


def _send_half_to_sibling(gp, name):
    rows = gp.shape[1]

    def body(gp_ref, out_ref, sems):
        x, y, c = _me()
        cp = _rcopy(gp_ref.at[:, _my_half(rows, c, mine=False)], out_ref, sems, 0, (x, y, 1 - c))
        cp.start()
        cp.wait()

    return _pc(body, name=name, in_specs=[HBM], out_specs=HBM, out_shape=_sds((4, rows // 2, D_MODEL), gp.dtype),
               scratch=[pltpu.SemaphoreType.DMA((2, 1))])(gp)


def _chips_start(sb_ref, out_ref, sems):
    x, y, c = _me()
    for f in (1, 2, 3):
        px, py = _flip(x, y, f)
        _rcopy(sb_ref.at[2 * px + py], out_ref.at[f - 1], sems, f - 1, (px, py, c)).start()


def _chips_finish(sb_ref, out_ref, sems):
    x, y, c = _me()
    for f in (1, 2, 3):
        _rcopy(sb_ref.at[0], out_ref.at[f - 1], sems, f - 1, (x, y, c)).wait_recv()
    for f in (1, 2, 3):
        px, py = _flip(x, y, f)
        _rcopy(sb_ref.at[2 * px + py], out_ref.at[f - 1], sems, f - 1, (px, py, c)).wait_send()


def _exchange_chips(sb, small):
    def body(sb_ref, small_ref, out_ref, smalls_ref, sems, local_sem):
        x, y, c = _me()
        me = 4 * x + 2 * y + c
        mine = pltpu.make_async_copy(small_ref, smalls_ref.at[me], local_sem)
        mine.start()
        _chips_start(sb_ref, out_ref, sems)
        sends = []
        for f in range(1, 8):
            px, py = _flip(x, y, f)
            pc = 1 - c if f & 4 else c
            sends.append(_rcopy(small_ref, smalls_ref.at[me], sems, 2 + f, (px, py, pc)))
            sends[-1].start()
        _chips_finish(sb_ref, out_ref, sems)
        for f in range(1, 8):
            px, py = _flip(x, y, f)
            pc = 1 - c if f & 4 else c
            _rcopy(small_ref, smalls_ref.at[4 * px + 2 * py + pc], sems, 2 + f, (x, y, c)).wait_recv()
        for cp in sends:
            cp.wait_send()
        mine.wait()

    return _pc(body, name="grads_exchange_chips", in_specs=[HBM, HBM], out_specs=[HBM, HBM],
               out_shape=[_sds((3,) + sb.shape[1:], bf16), _sds((8, SMALL_ROWS, D_MODEL), f32)],
               scratch=[pltpu.SemaphoreType.DMA((2, 10)), pltpu.SemaphoreType.DMA])(sb, small)


def _exchange_halves(tot, name):
    rows = tot.shape[0]

    def body(tot_ref, out_ref, sems):
        x, y, c = _me()
        half = _my_half(rows, c)
        cp = _rcopy(tot_ref.at[half], out_ref.at[half], sems, 0, (x, y, 1 - c))
        cp.start()
        cp.wait()

    return _pc(body, name=name, in_specs=[HBM], out_specs=HBM, out_shape=_sds((rows, D_MODEL), f32),
               aliases={0: 0}, scratch=[pltpu.SemaphoreType.DMA((2, 1))])(tot)


def _sum_rows(half):
    return max(r for r in range(16, 513, 16) if half % r == 0)


def _sum_over_cores(gp, recv, cq, name):
    half = recv.shape[1]
    tr = _sum_rows(half)
    nb = half // tr

    def body(cq_ref, g_ref, r_ref, o32_ref, o16_ref):
        s = g_ref[...].astype(f32) + r_ref[...].astype(f32)
        o32_ref[...] = s
        o16_ref[...] = s.astype(bf16)

    spec = pl.BlockSpec((1, tr, D_MODEL), lambda b, i, cq_ref: (b, i, 0))
    gs = pltpu.PrefetchScalarGridSpec(
        num_scalar_prefetch=1, grid=(4, nb),
        in_specs=[pl.BlockSpec((1, tr, D_MODEL), lambda b, i, cq_ref: (b, cq_ref[0] * nb + i, 0)), spec],
        out_specs=[spec, spec])
    return _pc(body, name=name, grid_spec=gs, sem=("arbitrary", "arbitrary"),
               out_shape=[_sds((4, half, D_MODEL), f32), _sds((4, half, D_MODEL), bf16)])(cq, gp, recv)


def _sum_over_chips(s32, recv, cq, name):
    half = recv.shape[1]
    tr = _sum_rows(half)
    nb = half // tr

    def body(cq_ref, own_ref, r_ref, o_ref):
        o_ref[...] = ((own_ref[0] + r_ref[0].astype(f32)) + r_ref[1].astype(f32)) + r_ref[2].astype(f32)

    gs = pltpu.PrefetchScalarGridSpec(
        num_scalar_prefetch=1, grid=(nb,),
        in_specs=[pl.BlockSpec((1, tr, D_MODEL), lambda i, cq_ref: (cq_ref[1], i, 0)),
                  pl.BlockSpec((3, tr, D_MODEL), lambda i, cq_ref: (0, i, 0))],
        out_specs=pl.BlockSpec((tr, D_MODEL), lambda i, cq_ref: (cq_ref[0] * nb + i, 0)))
    return _pc(body, name=name, grid_spec=gs, sem=("arbitrary",),
               out_shape=_sds((2 * half, D_MODEL), f32))(cq, s32, recv)


def _sum_small(smalls):
    def body(s_ref, o_ref):
        tot = s_ref[0]
        for d in range(1, 8):
            tot = tot + s_ref[d]
        o_ref[...] = tot

    return _pc(body, name="small_sum", out_shape=_sds((SMALL_ROWS, D_MODEL), f32))(smalls)


def _adamw_math(w, g, m, v):
    m = ADAM_B1 * m + (1.0 - ADAM_B1) * g
    v = ADAM_B2 * v + (1.0 - ADAM_B2) * jnp.square(g)
    m_hat = m / (1.0 - ADAM_B1 ** ADAM_STEP)
    v_hat = v / (1.0 - ADAM_B2 ** ADAM_STEP)
    delta = -ADAM_LR * (m_hat / (jnp.sqrt(v_hat) + ADAM_EPS) + ADAM_WD * w)
    return delta, m, v


def _adamw(w, g, m, v, name):
    cols = w.shape[1]
    return _rw(_adamw_math, [w, g, m, v], [], [(cols, f32)] * 3, name=name, tr=256)


def _adamw_small(items):
    n = len(items)

    def body(*refs):
        ins, outs = refs[:4 * n], refs[4 * n:]
        for i in range(n):
            res = _adamw_math(*[r[...] for r in ins[4 * i:4 * i + 4]])
            for o, val in zip(outs[3 * i:3 * i + 3], res):
                o[...] = val

    flat = [a for it in items for a in it]
    out_shape = [_sds(it[0].shape, f32) for it in items for _ in range(3)]
    res = _pc(body, name="adamw_small", out_shape=out_shape)(*flat)
    return [tuple(res[3 * i:3 * i + 3]) for i in range(n)]


def _pack_shards(sh, layout, pad):
    parts = [(sh[n] if layer is None else sh[n][layer]).reshape(-1, D_MODEL).astype(bf16) for n, layer, _ in layout]
    if pad:
        parts.append(jnp.zeros((pad, D_MODEL), bf16))
    return jnp.concatenate(parts, axis=0)


def _mlp_full(g4, off, layer):
    o, r = off["mlp_w_up", layer]
    up = g4[:, o:o + r].transpose(1, 0, 2).reshape(D_MODEL, D_FF)
    o, r = off["mlp_w_down", layer]
    return {("mlp_w_up", layer): up, ("mlp_w_down", layer): g4[:, o:o + r].reshape(D_FF, D_MODEL)}


def _unpack_early(g4):
    w = _mlp_full(g4, W_EARLY_OFF, 0)
    hg = g4[:, 0:1024].reshape(4, 4, 256, D_MODEL)
    w["hgrn_w4"] = hg.transpose(0, 2, 1, 3).reshape(D_MODEL, 4 * D_MODEL)
    o, r = W_EARLY_OFF["hgrn_w_o", None]
    w["hgrn_w_o"] = g4[:, o:o + r].reshape(D_MODEL, D_MODEL)
    return w


def _unpack_late(g4):
    def rows(name):
        o, r = W_LATE_OFF[name, None]
        return g4[:, o:o + r]

    w = _mlp_full(g4, W_LATE_OFF, 1)
    w["mla_w_dq"] = rows("mla_w_dq").reshape(D_MODEL, MLA_Q_LORA)
    uq = rows("mla_w_uq").reshape(4, MLA_Q_LORA, 768).transpose(1, 0, 2).reshape(MLA_Q_LORA, MLA_HEADS, MLA_NOPE + MLA_ROPE)
    w["mla_w_qcat"] = jnp.pad(uq, ((0, 0), (0, 0), (0, ATT_QK - MLA_NOPE - MLA_ROPE))).reshape(MLA_Q_LORA, MLA_HEADS * ATT_QK)
    w["mla_w_o"] = rows("mla_w_o").reshape(MLA_HEADS * MLA_V, D_MODEL)
    dkv = rows("kv_w_dkv").reshape(D_MODEL, MLA_KV_LORA + MLA_ROPE)
    w["kv_w_dkv"] = jnp.pad(dkv, ((0, 0), (0, LANES - MLA_ROPE)))
    uk = rows("kv_w_uk").reshape(4, MLA_KV_LORA, 512).transpose(1, 0, 2).reshape(MLA_KV_LORA, MLA_HEADS, MLA_NOPE)
    w["kv_w_kcat"] = jnp.pad(uk, ((0, 0), (0, 0), (0, ATT_QK - MLA_NOPE))).reshape(MLA_KV_LORA, MLA_HEADS * ATT_QK)
    w["kv_w_uv"] = rows("kv_w_uv").reshape(4, MLA_KV_LORA, 512).transpose(1, 0, 2).reshape(MLA_KV_LORA, MLA_HEADS * MLA_V)
    return w


def _pack_grads_late(g):
    return g["hgrn_w4"].reshape(4, 256, 4, D_MODEL).transpose(0, 2, 1, 3).reshape(4, G_LATE_ROWS, D_MODEL)


def _pack_grads_early(g):
    parts = [g["hgrn_w_o"].reshape(4, 256, D_MODEL),
             g["mla_w_dq"].reshape(4, 64, D_MODEL)]
    uq = g["mla_w_qcat"].reshape(MLA_Q_LORA, MLA_HEADS, ATT_QK)[:, :, :MLA_NOPE + MLA_ROPE]
    parts.append(uq.reshape(MLA_Q_LORA, 4, 768).transpose(1, 0, 2).reshape(4, 192, D_MODEL))
    parts.append(g["mla_w_o"].reshape(4, 512, D_MODEL))
    parts.append(g["kv_w_dkv"][:, :MLA_KV_LORA + MLA_ROPE].reshape(4, 80, D_MODEL))
    uk = g["kv_w_kcat"].reshape(MLA_KV_LORA, MLA_HEADS, ATT_QK)[:, :, :MLA_NOPE]
    parts.append(uk.reshape(MLA_KV_LORA, 4, 512).transpose(1, 0, 2).reshape(4, 128, D_MODEL))
    parts.append(g["kv_w_uv"].reshape(MLA_KV_LORA, 4, 512).transpose(1, 0, 2).reshape(4, 128, D_MODEL))
    parts += [up.reshape(D_MODEL, 4, 1024).transpose(1, 0, 2) for up in g["mlp_w_up"]]
    parts += [dn.reshape(4, 1024, D_MODEL) for dn in g["mlp_w_down"]]
    parts.append(jnp.zeros((4, PACK_PAD, D_MODEL), bf16))
    return jnp.concatenate(parts, axis=1)


LOSS_ROW = 11


def _pack_small(g, loss):
    rows = []
    for name, _, r, wd in SMALL:
        a = g[name].reshape(r, wd)
        rows.append(jnp.pad(a, ((0, 0), (0, D_MODEL - wd))) if wd < D_MODEL else a)
    assert sum(r for _, _, r, _ in SMALL) == LOSS_ROW
    rows.append(jnp.full((1, D_MODEL), loss, f32))
    rows.append(jnp.zeros((SMALL_ROWS - LOSS_ROW - 1, D_MODEL), f32))
    return jnp.concatenate(rows, axis=0)


def kernel(x, hgrn_norm, hgrn_w_q, hgrn_w_f, hgrn_w_i, hgrn_w_g, hgrn_g_norm, hgrn_w_o, hgrn_lb_logits, mla_norm, mla_w_dq, mla_q_norm, mla_w_uq, mla_w_o, kv_in_norm, kv_w_dkv, kv_norm, kv_w_uk, kv_w_uv, mlp_norm, mlp_w_up, mlp_w_down, final_norm, loss_target, m_hgrn_norm, m_hgrn_w_q, m_hgrn_w_f, m_hgrn_w_i, m_hgrn_w_g, m_hgrn_g_norm, m_hgrn_w_o, m_hgrn_lb_logits, m_mla_norm, m_mla_w_dq, m_mla_q_norm, m_mla_w_uq, m_mla_w_o, m_kv_in_norm, m_kv_w_dkv, m_kv_norm, m_kv_w_uk, m_kv_w_uv, m_mlp_norm, m_mlp_w_up, m_mlp_w_down, m_final_norm, v_hgrn_norm, v_hgrn_w_q, v_hgrn_w_f, v_hgrn_w_i, v_hgrn_w_g, v_hgrn_g_norm, v_hgrn_w_o, v_hgrn_lb_logits, v_mla_norm, v_mla_w_dq, v_mla_q_norm, v_mla_w_uq, v_mla_w_o, v_kv_in_norm, v_kv_w_dkv, v_kv_norm, v_kv_w_uk, v_kv_w_uv, v_mlp_norm, v_mlp_w_up, v_mlp_w_down, v_final_norm):
    given = dict(locals())
    wsh = {n: given[n] for n in WEIGHTS}
    msh = {n: given["m_" + n] for n in WEIGHTS}
    vsh = {n: given["v_" + n] for n in WEIGHTS}
    xi, yi, ci = _me()
    chip = 2 * xi + yi
    cq = jnp.stack([ci, chip]).astype(jnp.int32)

    small_w = {n: wsh[n].reshape(r, -1) for n, _, r, _ in SMALL}
    sv = jnp.concatenate([small_w["hgrn_norm"], small_w["hgrn_lb_logits"], jnp.zeros((5, 256), f32)], axis=0)
    g4, sv4 = _all_gather_weights(_pack_shards(wsh, W_EARLY, 0), sv)
    w = _unpack_early(g4)
    w["hgrn_norm"] = sv4[:, 0, :].reshape(1, D_MODEL)
    w["hgrn_lb_logits"] = sv4[:, 1:3, :].transpose(1, 0, 2).reshape(2, D_MODEL)
    for n in ("hgrn_g_norm", "mla_norm", "mla_q_norm", "kv_in_norm", "kv_norm", "mlp_norm", "final_norm"):
        w[n] = small_w[n]

    class Comm:
        late_shard = _pack_shards(wsh, W_LATE, PACK_PAD)
        unpack_late = staticmethod(_unpack_late)
        received_early = None

        @staticmethod
        def reduce_early(grads):
            gp = _pack_grads_early(grads)
            Comm.s32_early, s16 = _sum_over_cores(gp, _send_half_to_sibling(gp, "grads_to_sibling_early"), cq,
                                                  "grads_sum_cores_early")
            return s16

    loss, grad_x, g = _local_step(x.reshape(-1, D_MODEL), loss_target.reshape(-1, D_MODEL), w, Comm)

    tot_early = _exchange_halves(_sum_over_chips(Comm.s32_early, Comm.received_early, cq, "grads_sum_chips_early"),
                                 "grads_exchange_halves_early")
    gp = _pack_grads_late(g)
    s32, s16 = _sum_over_cores(gp, _send_half_to_sibling(gp, "grads_to_sibling_late"), cq, "grads_sum_cores_late")
    from_chips, smalls = _exchange_chips(s16, _pack_small(g, loss))
    tot_late = _exchange_halves(_sum_over_chips(s32, from_chips, cq, "grads_sum_chips_late"), "grads_exchange_halves_late")
    small_tot = _sum_small(smalls)
    loss = small_tot[LOSS_ROW, 0]

    grad, delta, new_m, new_v = {}, {}, {}, {}
    where = [(n, tot_late, G_LATE_OFF[n]) for n, _ in G_LATE] + [(n, tot_early, G_EARLY_OFF[n]) for n, _ in G_EARLY]
    for n, total, (o, r) in where:
        shp = wsh[n].shape
        two_d = (-1, shp[-1])
        grad[n] = total[o:o + r].reshape(shp)
        d, m2, v2 = _adamw(wsh[n].reshape(two_d), grad[n].reshape(two_d), msh[n].reshape(two_d), vsh[n].reshape(two_d),
                           "adamw_" + n)
        delta[n], new_m[n], new_v[n] = d.reshape(shp), m2.reshape(shp), v2.reshape(shp)
    items = []
    for n, row, r, wd in SMALL:
        gs = small_tot[row:row + r, :wd]
        if n in ("hgrn_norm", "hgrn_lb_logits"):
            gs = lax.dynamic_slice(gs, (0, 256 * chip), (r, 256))
        grad[n] = gs.reshape(wsh[n].shape)
        items.append((small_w[n], gs, msh[n].reshape(gs.shape), vsh[n].reshape(gs.shape)))
    for (n, _, _, _), (d, m2, v2) in zip(SMALL, _adamw_small(items)):
        shp = wsh[n].shape
        delta[n], new_m[n], new_v[n] = d.reshape(shp), m2.reshape(shp), v2.reshape(shp)

    return (loss, grad_x.reshape(x.shape), *[grad[n] for n in WEIGHTS], *[delta[n] for n in WEIGHTS],
            *[new_m[n] for n in WEIGHTS], *[new_v[n] for n in WEIGHTS])
```

```python
import functools

import jax
import jax.numpy as jnp
from jax import lax
from jax.experimental import pallas as pl
from jax.experimental.pallas import tpu as pltpu

f32, bf16 = jnp.float32, jnp.bfloat16
HI = lax.Precision.HIGHEST
MESH = pl.DeviceIdType.MESH

D_MODEL = 1024
D_FF = 4096
EPS = 1e-6
HGRN_HEADS, HGRN_DK, HGRN_CHUNK, HGRN_SUB = 8, 128, 64, 16
MLA_HEADS, MLA_NOPE, MLA_ROPE, MLA_V = 16, 128, 64, 128
MLA_Q_LORA, MLA_KV_LORA = 256, 256
ROPE_THETA = 10000.0
ATT_SCALE = (MLA_NOPE + MLA_ROPE) ** -0.5
EXP_CLAMP = 80.0

ADAM_LR, ADAM_B1, ADAM_B2, ADAM_EPS, ADAM_WD, ADAM_STEP = 0.001, 0.9, 0.999, 1e-08, 0.01, 10

V7X_VMEM_BYTES = 64 * 1024 * 1024
VMEM_LIMIT = V7X_VMEM_BYTES - 8 * 1024 * 1024
LANES = 128

PACK_PAD = 16
W_EARLY = (("hgrn_w_q", None, 256), ("hgrn_w_f", None, 256), ("hgrn_w_i", None, 256), ("hgrn_w_g", None, 256),
           ("hgrn_w_o", None, 256), ("mlp_w_up", 0, 1024), ("mlp_w_down", 0, 1024))
W_MID = (("mla_w_dq", None, 64), ("mla_w_uq", None, 192), ("mla_w_o", None, 512), ("kv_w_dkv", None, 80),
         ("kv_w_uk", None, 128), ("kv_w_uv", None, 128))
W_LAST = (("mlp_w_up", 1, 1024), ("mlp_w_down", 1, 1024))
G_LATE = (("hgrn_w_q", None, 256), ("hgrn_w_f", None, 256), ("hgrn_w_i", None, 256), ("hgrn_w_g", None, 256))
G_ATTN = (("mla_w_o", None, 512), ("mlp_w_up", 1, 1024), ("mlp_w_down", 1, 1024))
G_GLA = (("hgrn_w_o", None, 256), ("mla_w_dq", None, 64), ("mla_w_uq", None, 192), ("kv_w_dkv", None, 80),
         ("kv_w_uk", None, 128), ("kv_w_uv", None, 128), ("mlp_w_up", 0, 1024), ("mlp_w_down", 0, 1024))
PADDED = (W_MID, G_GLA)


def _offsets(layout):
    out, o = {}, 0
    for name, layer, rows in layout:
        out[name, layer] = (o, rows)
        o += rows
    return out, o + (PACK_PAD if layout in PADDED else 0)


W_EARLY_OFF, W_EARLY_ROWS = _offsets(W_EARLY)
W_MID_OFF, W_MID_ROWS = _offsets(W_MID)
W_LAST_OFF, W_LAST_ROWS = _offsets(W_LAST)
G_LATE_OFF, G_LATE_ROWS = _offsets(G_LATE)
G_ATTN_OFF, G_ATTN_ROWS = _offsets(G_ATTN)
G_GLA_OFF, G_GLA_ROWS = _offsets(G_GLA)
assert all(r % 32 == 0 for r in (W_EARLY_ROWS, W_MID_ROWS, W_LAST_ROWS, G_LATE_ROWS, G_ATTN_ROWS, G_GLA_ROWS))

WEIGHTS = ("hgrn_norm", "hgrn_w_q", "hgrn_w_f", "hgrn_w_i", "hgrn_w_g", "hgrn_g_norm", "hgrn_w_o", "hgrn_lb_logits",
           "mla_norm", "mla_w_dq", "mla_q_norm", "mla_w_uq", "mla_w_o", "kv_in_norm", "kv_w_dkv", "kv_norm", "kv_w_uk",
           "kv_w_uv", "mlp_norm", "mlp_w_up", "mlp_w_down", "final_norm")
SMALL = (("hgrn_norm", 0, 1, 1024), ("hgrn_lb_logits", 1, 2, 1024), ("hgrn_g_norm", 3, 1, 128),
         ("mla_norm", 4, 1, 1024), ("mla_q_norm", 5, 1, 256), ("kv_in_norm", 6, 1, 1024), ("kv_norm", 7, 1, 256),
         ("mlp_norm", 8, 2, 1024), ("final_norm", 10, 1, 1024))
SMALL_ROWS = 16


def _pc(body, *, name, out_shape, grid=None, in_specs=None, out_specs=None, scratch=(), sem=None, grid_spec=None,
        aliases=None):
    params = pltpu.CompilerParams(dimension_semantics=sem, vmem_limit_bytes=VMEM_LIMIT)
    if grid_spec is not None:
        return pl.pallas_call(body, name=name, out_shape=out_shape, grid_spec=grid_spec, compiler_params=params,
                              interpret=False)
    kw = {k: v for k, v in (("grid", grid), ("in_specs", in_specs), ("out_specs", out_specs),
                            ("input_output_aliases", aliases)) if v is not None}
    return pl.pallas_call(body, name=name, out_shape=out_shape, scratch_shapes=list(scratch), compiler_params=params,
                          interpret=False, **kw)


def _sds(shape, dtype):
    return jax.ShapeDtypeStruct(tuple(shape), dtype)


def _mm(a, b, *, name, ta=False, tb=False, outs=(f32,), epi=None, extras=(), tm=1024, tn=1024, tk=4096):
    m, k = (a.shape[1], a.shape[0]) if ta else a.shape
    n = b.shape[0] if tb else b.shape[1]
    tm, tn, tk = min(tm, m), min(tn, n), min(tk, k)
    assert m % tm == 0 and n % tn == 0 and k % tk == 0, (name, m, n, k)
    nk = k // tk
    a_spec = pl.BlockSpec((tk, tm), lambda i, j, kk: (kk, i)) if ta else pl.BlockSpec((tm, tk), lambda i, j, kk: (i, kk))
    b_spec = pl.BlockSpec((tn, tk), lambda i, j, kk: (j, kk)) if tb else pl.BlockSpec((tk, tn), lambda i, j, kk: (kk, j))
    e_specs = [pl.BlockSpec((tm, tn), lambda i, j, kk: (i, j)) if e.shape[1] == n else
               pl.BlockSpec((tm, e.shape[1]), lambda i, j, kk: (i, 0)) for e in extras]
    dn = (((0 if ta else 1,), (1 if tb else 0,)), ((), ()))
    n_e, n_o = len(extras), len(outs)

    def finish(r, e_refs, o_refs):
        res = epi(r, *[e[...] for e in e_refs]) if epi is not None else (r,)
        for o, v in zip(o_refs, res):
            o[...] = v.astype(o.dtype)

    def body(*refs):
        a_ref, b_ref = refs[0], refs[1]
        e_refs = refs[2:2 + n_e]
        o_refs = refs[2 + n_e:2 + n_e + n_o]
        prod = lax.dot_general(a_ref[...].astype(bf16), b_ref[...].astype(bf16), dn, preferred_element_type=f32)
        if nk == 1:
            finish(prod, e_refs, o_refs)
            return
        acc = refs[-1]
        kk = pl.program_id(2)

        @pl.when(kk == 0)
        def _():
            acc[...] = jnp.zeros_like(acc)

        acc[...] += prod

        @pl.when(kk == nk - 1)
        def _():
            finish(acc[...], e_refs, o_refs)

    out = _pc(body, name=name, grid=(m // tm, n // tn, nk),
              in_specs=[a_spec, b_spec] + e_specs,
              out_specs=[pl.BlockSpec((tm, tn), lambda i, j, kk: (i, j)) for _ in outs],
              out_shape=[_sds((m, n), dt) for dt in outs],
              scratch=[pltpu.VMEM((tm, tn), f32)] if nk > 1 else [],
              sem=("parallel", "parallel", "arbitrary"))(a, b, *extras)
    return out[0] if n_o == 1 else out


def _wgrad(a, b, name):
    return _mm(a, b, ta=True, name=name, outs=(bf16,))


def _rw(fn, rows, bcast, outs, accs=(), *, name, tr=256):
    t = rows[0].shape[0]
    tr = min(tr, t)
    assert t % tr == 0
    n_r, n_b, n_o, n_a = len(rows), len(bcast), len(outs), len(accs)

    def body(*refs):
        r_refs = refs[:n_r]
        b_refs = refs[n_r:n_r + n_b]
        o_refs = refs[n_r + n_b:n_r + n_b + n_o]
        a_refs = refs[n_r + n_b + n_o:]
        res = fn(*[r[...] for r in r_refs], *[b[...] for b in b_refs])
        for o, v in zip(o_refs, res[:n_o]):
            o[...] = v.astype(o.dtype)
        i = pl.program_id(0)
        for a_ref, v in zip(a_refs, res[n_o:]):
            @pl.when(i == 0)
            def _(a_ref=a_ref):
                a_ref[...] = jnp.zeros_like(a_ref)
            a_ref[...] += v

    in_specs = [pl.BlockSpec((tr, r.shape[1]), lambda i: (i, 0)) for r in rows]
    in_specs += [pl.BlockSpec(b.shape, lambda i: (0, 0)) for b in bcast]
    out_specs = [pl.BlockSpec((tr, w), lambda i: (i, 0)) for w, _ in outs]
    out_specs += [pl.BlockSpec(s, lambda i: (0, 0)) for s in accs]
    out_shape = [_sds((t, w), dt) for w, dt in outs] + [_sds(s, f32) for s in accs]
    res = _pc(body, name=name, grid=(t // tr,), in_specs=in_specs, out_specs=out_specs, out_shape=out_shape,
              sem=("arbitrary",))(*rows, *bcast)
    return res


def _rms(x, gain):
    return x * lax.rsqrt(jnp.mean(x * x, axis=-1, keepdims=True) + EPS) * gain


def _rms_bwd(x, gain, dy):
    _, vjp = jax.vjp(_rms, x, gain)
    return vjp(dy)


def _lower_bound(lbl):
    l0, l1 = lbl[0:1, :], lbl[1:2, :]
    mx = jnp.maximum(l0, l1)
    e0, e1 = jnp.exp(l0 - mx), jnp.exp(l1 - mx)
    return e0 / (e0 + e1)


def _gates(qpre, fpre, lbl):
    lb = _lower_bound(lbl)
    q = jax.nn.silu(qpre)
    forget = lb + (1.0 - lb) * jax.nn.sigmoid(fpre)
    return q, 1.0 - forget, jnp.log(forget)


def _head_norm_gate(o, gpre, gn):
    return _rms(o, gn) * jax.nn.silu(gpre)


def _swap_halves(x):
    w = x.shape[1]
    lane = lax.broadcasted_iota(jnp.int32, x.shape, 1)
    return jnp.where((lane % MLA_ROPE) < MLA_ROPE // 2, pltpu.roll(x, w - MLA_ROPE // 2, 1),
                     pltpu.roll(x, MLA_ROPE // 2, 1))


def _tile_lanes(tab, w):
    return tab if w == tab.shape[1] else jnp.concatenate([tab] * (w // tab.shape[1]), axis=1)


def _rope(x, cos, sgn_sin, sign=1.0):
    w = x.shape[1]
    return x * _tile_lanes(cos, w) + sign * _swap_halves(x) * _tile_lanes(sgn_sin, w)


def _rope_heads(x, cos, sgn_sin, sign, scale):
    parts = []
    for h in range(x.shape[1] // (2 * LANES)):
        parts.append(x[:, 2 * LANES * h:2 * LANES * h + LANES] * scale)
        parts.append(_rope(x[:, 2 * LANES * h + LANES:2 * LANES * (h + 1)], cos, sgn_sin, sign) * scale)
    return jnp.concatenate(parts, axis=1)


def _bd(a, b, ca, cb):
    return lax.dot_general(a.astype(bf16), b.astype(bf16), (((ca,), (cb,)), ((), ())), preferred_element_type=f32)


@jax.custom_vjp
def _dot_nn(a, b):
    return _bd(a, b, 1, 0)


@jax.custom_vjp
def _dot_nt(a, b):
    return _bd(a, b, 1, 1)


@jax.custom_vjp
def _dot_tn(a, b):
    return _bd(a, b, 0, 0)


_dot_nn.defvjp(lambda a, b: (_bd(a, b, 1, 0), (a, b)), lambda r, g: (_bd(g, r[1], 1, 1), _bd(r[0], g, 0, 0)))
_dot_nt.defvjp(lambda a, b: (_bd(a, b, 1, 1), (a, b)), lambda r, g: (_bd(g, r[1], 1, 0), _bd(g, r[0], 0, 0)))
_dot_tn.defvjp(lambda a, b: (_bd(a, b, 0, 0), (a, b)), lambda r, g: (_bd(r[1], g, 1, 1), _bd(r[0], g, 1, 0)))


def _scan_rows(x, reverse):
    n = x.shape[0]
    row = lax.broadcasted_iota(jnp.int32, x.shape, 0)
    s = 1
    while s < n:
        if reverse:
            x = x + jnp.where(row < n - s, pltpu.roll(x, n - s, 0), 0.0)
        else:
            x = x + jnp.where(row >= s, pltpu.roll(x, s, 0), 0.0)
        s *= 2
    return x


@jax.custom_vjp
def _cumsum_rows(g):
    return _scan_rows(g, False)


_cumsum_rows.defvjp(lambda g: (_scan_rows(g, False), None), lambda _, ct: (_scan_rows(ct, True),))

HGRN_PAIRS = HGRN_HEADS // 2
HGRN_PAIR = 2 * HGRN_DK
GLA_STATE = (HGRN_PAIRS, HGRN_PAIR, HGRN_PAIR)


def _gla_consts():
    s = HGRN_SUB
    r = lax.broadcasted_iota(jnp.int32, (HGRN_PAIR, HGRN_PAIR), 0)
    c = lax.broadcasted_iota(jnp.int32, (HGRN_PAIR, HGRN_PAIR), 1)
    pair_mask = (r < HGRN_DK) == (c < HGRN_DK)
    masks = []
    for i in range(HGRN_CHUNK // s):
        n = s * (i + 1)
        row = lax.broadcasted_iota(jnp.int32, (HGRN_HEADS * s, HGRN_HEADS * n), 0)
        col = lax.broadcasted_iota(jnp.int32, (HGRN_HEADS * s, HGRN_HEADS * n), 1)
        col_head = sum((col >= m * n).astype(jnp.int32) for m in range(1, HGRN_HEADS))
        masks.append((col_head == row // s) & (col - col_head * n <= s * i + row % s))
    return pair_mask, masks


def _heads_to_rows(x):
    return jnp.concatenate([x[:, HGRN_DK * h:HGRN_DK * (h + 1)] for h in range(HGRN_HEADS)], axis=0)


def _gla_chunk(consts, dots, q, k, v, g, st):
    pair_mask, masks = consts
    dot_nn, dot_nt, dot_tn = dots
    c, s = HGRN_CHUNK, HGRN_SUB
    b = _cumsum_rows(g)
    b_last = b[c - 1:c, :]
    q_in, k_out = q * jnp.exp(b), k * jnp.exp(b_last - b)
    o_inter, st_new = [], []
    for p in range(HGRN_PAIRS):
        cols = slice(HGRN_PAIR * p, HGRN_PAIR * (p + 1))
        o_inter.append(dot_nt(q_in[:, cols], st[p]))
        st_new.append(st[p] * jnp.exp(b_last[:, cols]) + jnp.where(pair_mask, dot_tn(v[:, cols], k_out[:, cols]), 0.0))
    intra = []
    for i in range(c // s):
        n = s * (i + 1)
        ref = b[s * i - 1:s * i, :] if i else jnp.zeros_like(b_last)
        qt = _heads_to_rows(q[s * i:n] * jnp.exp(b[s * i:n] - ref))
        kt = _heads_to_rows(k[:n] * jnp.exp(jnp.minimum(ref - b[:n], EXP_CLAMP)))
        sc = jnp.where(masks[i], dot_nt(qt, kt), 0.0)
        oi = dot_nn(sc, _heads_to_rows(v[:n]))
        intra.append(jnp.concatenate([oi[s * h:s * (h + 1)] for h in range(HGRN_HEADS)], axis=1))
    return jnp.concatenate(o_inter, axis=1) + jnp.concatenate(intra, axis=0), st_new


_PLAIN_DOTS = (lambda a, b: _bd(a, b, 1, 0), lambda a, b: _bd(a, b, 1, 1), lambda a, b: _bd(a, b, 0, 0))
_VJP_DOTS = (_dot_nn, _dot_nt, _dot_tn)


def _hgrn_mix(consts, dots, qpre, fpre, v, gpre, lbl, gn, st):
    q, k, g = _gates(qpre, fpre, lbl)
    o, st_new = _gla_chunk(consts, dots, q, k, v, g, st)
    y = [_head_norm_gate(o[:, HGRN_DK * h:HGRN_DK * (h + 1)], gpre[:, HGRN_DK * h:HGRN_DK * (h + 1)], gn)
         for h in range(HGRN_HEADS)]
    return jnp.concatenate(y, axis=1), st_new


def _gla_fwd(p4, lbl, gn, gather=None):
    t = p4.shape[0]
    nc = t // HGRN_CHUNK

    def body(q_ref, k_ref, v_ref, g_ref, lbl_ref, gn_ref, *rest):
        if gather is None:
            o_ref, s_ref, st = rest
        else:
            wp_ref, _, o_ref, s_ref, gathered_ref, st, sems = rest

        @pl.when(pl.program_id(0) == 0)
        def _():
            st[...] = jnp.zeros_like(st)
            if gather is not None:
                _gather_start(wp_ref, gathered_ref, sems)

        if gather is not None:
            @pl.when(pl.program_id(0) == nc - 1)
            def _():
                _gather_finish(wp_ref, gathered_ref, sems)

        s_in = [st[p] for p in range(HGRN_PAIRS)]
        y, st_new = _hgrn_mix(_gla_consts(), _PLAIN_DOTS, q_ref[...], k_ref[...], v_ref[...], g_ref[...],
                              lbl_ref[...], gn_ref[...], s_in)
        o_ref[...] = y.astype(o_ref.dtype)
        for p in range(HGRN_PAIRS):
            s_ref[0, p] = s_in[p]
            st[p] = st_new[p]

    blk = lambda off: pl.BlockSpec((HGRN_CHUNK, D_MODEL), lambda c: (c, off))
    whole = lambda a: pl.BlockSpec(a.shape, lambda c: (0, 0))
    state_shape = GLA_STATE
    in_specs = [blk(0), blk(1), blk(2), blk(3), whole(lbl), whole(gn)]
    out_specs = [blk(0), pl.BlockSpec((1,) + state_shape, lambda c: (c, 0, 0, 0))]
    out_shape = [_sds((t, D_MODEL), bf16), _sds((nc,) + state_shape, f32)]
    scratch = [pltpu.VMEM(state_shape, f32)]
    if gather is None:
        return _pc(body, name="gla_fwd", grid=(nc,), in_specs=in_specs, out_specs=out_specs, out_shape=out_shape,
                   scratch=scratch, sem=("arbitrary",))(p4, p4, p4, p4, lbl, gn)
    return _pc(body, name="gla_fwd_gather", grid=(nc,), in_specs=in_specs + [HBM, HBM], out_specs=out_specs + [HBM],
               out_shape=out_shape + [_sds((4,) + gather.shape, bf16)], aliases={7: 2},
               scratch=scratch + [pltpu.SemaphoreType.DMA((2, 6))], sem=("arbitrary",))(
                   p4, p4, p4, p4, lbl, gn, gather, _gather_base(gather))


def _gla_bwd(p4, lbl, gn, states, dy, exchange=None):
    t = p4.shape[0]
    nc = t // HGRN_CHUNK

    def body(q_ref, k_ref, v_ref, g_ref, lbl_ref, gn_ref, s_ref, dy_ref, *rest):
        if exchange is None:
            dp_ref, dlbl_ref, dgn_ref, dst = rest
        else:
            sb_ref, dp_ref, dlbl_ref, dgn_ref, recv_ref, dst, sems = rest

        @pl.when(pl.program_id(0) == 0)
        def _():
            dst[...] = jnp.zeros_like(dst)
            dlbl_ref[...] = jnp.zeros_like(dlbl_ref)
            dgn_ref[...] = jnp.zeros_like(dgn_ref)
            if exchange is not None:
                _chips_start(sb_ref, recv_ref, sems)

        if exchange is not None:
            @pl.when(pl.program_id(0) == nc - 1)
            def _():
                _chips_finish(sb_ref, recv_ref, sems)

        consts = _gla_consts()
        fn = lambda *args: _hgrn_mix(consts, _VJP_DOTS, *args)
        pairs = range(HGRN_PAIRS)
        _, vjp = jax.vjp(fn, q_ref[...], k_ref[...], v_ref[...], g_ref[...], lbl_ref[...], gn_ref[...],
                         [s_ref[0, p] for p in pairs])
        *d_proj, dlbl, dgn, ds = vjp((dy_ref[...], [dst[p] for p in pairs]))
        for i, d in enumerate(d_proj):
            dp_ref[:, D_MODEL * i:D_MODEL * (i + 1)] = d.astype(dp_ref.dtype)
        dlbl_ref[...] += dlbl
        dgn_ref[...] += dgn
        for p in pairs:
            dst[p] = ds[p]

    blk = lambda off: pl.BlockSpec((HGRN_CHUNK, D_MODEL), lambda c: (nc - 1 - c, off))
    whole = lambda a: pl.BlockSpec(a.shape, lambda c: (0, 0))
    state_shape = GLA_STATE
    in_specs = [blk(0), blk(1), blk(2), blk(3), whole(lbl), whole(gn),
                pl.BlockSpec((1,) + state_shape, lambda c: (nc - 1 - c, 0, 0, 0)), blk(0)]
    out_specs = [pl.BlockSpec((HGRN_CHUNK, 4 * D_MODEL), lambda c: (nc - 1 - c, 0)), whole(lbl), whole(gn)]
    out_shape = [_sds((t, 4 * D_MODEL), bf16), _sds(lbl.shape, f32), _sds(gn.shape, f32)]
    scratch = [pltpu.VMEM(state_shape, f32)]
    if exchange is None:
        return _pc(body, name="gla_bwd", grid=(nc,), in_specs=in_specs, out_specs=out_specs, out_shape=out_shape,
                   scratch=scratch, sem=("arbitrary",))(p4, p4, p4, p4, lbl, gn, states, dy)
    return _pc(body, name="gla_bwd_exchange", grid=(nc,), in_specs=in_specs + [HBM], out_specs=out_specs + [HBM],
               out_shape=out_shape + [_sds((3,) + exchange.shape[1:], bf16)],
               scratch=scratch + [pltpu.SemaphoreType.DMA((2, 3))], sem=("arbitrary",))(
                   p4, p4, p4, p4, lbl, gn, states, dy, exchange)


ATT_FWD_TQ, ATT_FWD_TK = 1024, 1024
ATT_BWD_TQ, ATT_BWD_TK = 1024, 512
ATT_QK = 2 * LANES
NEG = -1e30


def _pair_masks(shape):
    lane = lax.broadcasted_iota(jnp.int32, shape, 1)
    return lane < MLA_ROPE, lane >= MLA_ROPE


def _causal(shape, row0, col0):
    row = row0 + lax.broadcasted_iota(jnp.int32, shape, 0)
    col = col0 + lax.broadcasted_iota(jnp.int32, shape, 1)
    return col <= row


def _qk_cols(e):
    return slice(ATT_QK * e, ATT_QK * (e + 1))


def _v_cols(e):
    return slice(MLA_V * e, MLA_V * (e + 1))


def _first_last_step(n0, n1):
    p, i = pl.program_id(0), pl.program_id(1)
    return (p == 0) & (i == 0), (p == n0 - 1) & (i == n1 - 1)


def _attn_fwd(qc, kc, v, gather=None):
    t = qc.shape[0]
    tq, tk = min(ATT_FWD_TQ, t), min(ATT_FWD_TK, t)
    npair = MLA_HEADS // 2

    def body(q_ref, k_ref, v_ref, *rest):
        if gather is None:
            o_ref, lse_ref = rest
        else:
            wp_ref, _, o_ref, lse_ref, gathered_ref, sems = rest
            first, last = _first_last_step(npair, t // tq)
            pl.when(first)(lambda: _gather_start(wp_ref, gathered_ref, sems))
            pl.when(last)(lambda: _gather_finish(wp_ref, gathered_ref, sems))
        i = pl.program_id(1)
        n_full = (i * tq + 1) // tk
        nkv = (i * tq + tq + tk - 1) // tk
        q = [q_ref[:, _qk_cols(e)] for e in range(2)]

        def step(j, carry, masked):
            ks = pl.ds(pl.multiple_of(j * tk, tk), tk)
            ok = _causal((tq, tk), i * tq, j * tk) if masked else None
            new = []
            for e in range(2):
                m, l, acc = carry[e]
                s = _bd(q[e], k_ref[ks, _qk_cols(e)], 1, 1)
                if masked:
                    s = jnp.where(ok, s, NEG)
                m_new = jnp.maximum(m, jnp.max(s, axis=-1, keepdims=True))
                p = jnp.exp(s - m_new)
                alpha = jnp.exp(m - m_new)
                l = alpha * l + jnp.sum(p, axis=-1, keepdims=True)
                acc = alpha * acc + _bd(p, v_ref[ks, _v_cols(e)], 1, 0)
                new.append((m_new, l, acc))
            return tuple(new)

        one = (jnp.full((tq, 1), NEG, f32), jnp.zeros((tq, 1), f32), jnp.zeros((tq, MLA_V), f32))
        carry = lax.fori_loop(0, n_full, functools.partial(step, masked=False), (one, one))
        carry = lax.fori_loop(n_full, nkv, functools.partial(step, masked=True), carry)
        o_ref[...] = jnp.concatenate([acc / l for _, l, acc in carry], axis=1).astype(o_ref.dtype)
        lo, _ = _pair_masks((tq, LANES))
        lse_ref[...] = jnp.where(lo, *[m + jnp.log(l) for m, l, _ in carry])

    in_specs = [pl.BlockSpec((tq, 2 * ATT_QK), lambda p, i: (i, p)),
                pl.BlockSpec((t, 2 * ATT_QK), lambda p, i: (0, p)),
                pl.BlockSpec((t, 2 * MLA_V), lambda p, i: (0, p))]
    out_specs = [pl.BlockSpec((tq, 2 * MLA_V), lambda p, i: (i, p)), pl.BlockSpec((tq, LANES), lambda p, i: (i, p))]
    out_shape = [_sds((t, MLA_HEADS * MLA_V), bf16), _sds((t, npair * LANES), f32)]
    if gather is None:
        return _pc(body, name="attn_fwd", grid=(npair, t // tq), in_specs=in_specs, out_specs=out_specs,
                   out_shape=out_shape, sem=("arbitrary", "arbitrary"))(qc, kc, v)
    return _pc(body, name="attn_fwd_gather", grid=(npair, t // tq), in_specs=in_specs + [HBM, HBM],
               out_specs=out_specs + [HBM], out_shape=out_shape + [_sds((4,) + gather.shape, bf16)], aliases={4: 2},
               scratch=[pltpu.SemaphoreType.DMA((2, 6))], sem=("arbitrary", "arbitrary"))(
                   qc, kc, v, gather, _gather_base(gather))


def _attn_bwd(qc, kc, v, do, lse, delta, exchange=None):
    t = qc.shape[0]
    tq, tk = min(ATT_BWD_TQ, t), min(ATT_BWD_TK, t)
    npair = MLA_HEADS // 2
    nq = t // tq

    def body(q_ref, do_ref, lse_ref, dl_ref, k_ref, v_ref, *rest):
        if exchange is None:
            dq_ref, dk_ref, dv_ref = rest
        else:
            sb_ref, dq_ref, dk_ref, dv_ref, recv_ref, sems = rest
            first, last = _first_last_step(npair, t // tk)
            pl.when(first)(lambda: _chips_start(sb_ref, recv_ref, sems))
            pl.when(last)(lambda: _chips_finish(sb_ref, recv_ref, sems))
        j = pl.program_id(1)

        @pl.when(j == 0)
        def _():
            dq_ref[...] = jnp.zeros_like(dq_ref)

        k = [k_ref[:, _qk_cols(e)] for e in range(2)]
        vv = [v_ref[:, _v_cols(e)] for e in range(2)]

        def step(i, carry, masked):
            qs = pl.ds(pl.multiple_of(i * tq, tq), tq)
            ok = _causal((tq, tk), i * tq, j * tk) if masked else None
            lse2, dl2 = lse_ref[qs, :], dl_ref[qs, :]
            new = []
            for e in range(2):
                dk, dv = carry[e]
                q_e, do_e = q_ref[qs, _qk_cols(e)], do_ref[qs, _v_cols(e)]
                p = jnp.exp(_bd(q_e, k[e], 1, 1) - lse2[:, MLA_ROPE * e:MLA_ROPE * e + 1])
                if masked:
                    p = jnp.where(ok, p, 0.0)
                dv = dv + _bd(p, do_e, 0, 0)
                dp = _bd(do_e, vv[e], 1, 1)
                ds = (p * (dp - dl2[:, MLA_ROPE * e:MLA_ROPE * e + 1])).astype(bf16)
                dk = dk + _bd(ds, q_e, 0, 0)
                dq_ref[qs, _qk_cols(e)] += _bd(ds, k[e], 1, 0)
                new.append((dk, dv))
            return tuple(new)

        one = (jnp.zeros((tk, ATT_QK), f32), jnp.zeros((tk, MLA_V), f32))
        i_full = jnp.minimum((j * tk + tk + tq - 2) // tq, nq)
        carry = lax.fori_loop((j * tk) // tq, i_full, functools.partial(step, masked=True), (one, one))
        carry = lax.fori_loop(i_full, nq, functools.partial(step, masked=False), carry)
        for e in range(2):
            dk_ref[:, _qk_cols(e)] = carry[e][0].astype(dk_ref.dtype)
            dv_ref[:, _v_cols(e)] = carry[e][1].astype(dv_ref.dtype)

    res = lambda w: pl.BlockSpec((t, w), lambda p, j: (0, p))
    blk = lambda w: pl.BlockSpec((tk, w), lambda p, j: (j, p))
    in_specs = [res(2 * ATT_QK), res(2 * MLA_V), res(LANES), res(LANES), blk(2 * ATT_QK), blk(2 * MLA_V)]
    out_specs = [res(2 * ATT_QK), blk(2 * ATT_QK), blk(2 * MLA_V)]
    out_shape = [_sds((t, MLA_HEADS * ATT_QK), f32), _sds((t, MLA_HEADS * ATT_QK), bf16), _sds((t, MLA_HEADS * MLA_V), bf16)]
    if exchange is None:
        return _pc(body, name="attn_bwd", grid=(npair, t // tk), in_specs=in_specs, out_specs=out_specs,
                   out_shape=out_shape, sem=("arbitrary", "arbitrary"))(qc, do, lse, delta, kc, v)
    return _pc(body, name="attn_bwd_exchange", grid=(npair, t // tk), in_specs=in_specs + [HBM],
               out_specs=out_specs + [HBM], out_shape=out_shape + [_sds((3,) + exchange.shape[1:], bf16)],
               scratch=[pltpu.SemaphoreType.DMA((2, 3))], sem=("arbitrary", "arbitrary"))(
                   qc, do, lse, delta, kc, v, exchange)


def _rope_tables(t):
    half = MLA_ROPE // 2
    inv_freq = ROPE_THETA ** (-jnp.arange(half, dtype=f32) / half)
    ang = jnp.arange(t, dtype=f32)[:, None] * inv_freq[None, :]
    cos, sin = jnp.cos(ang), jnp.sin(ang)
    return jnp.concatenate([cos, cos] * 2, axis=1), jnp.concatenate([-sin, sin] * 2, axis=1)


def _relu2_epi(u):
    r = jnp.maximum(u, 0.0)
    return u, r * r


def _add_epi(r, res):
    return (r + res,)


def _drelu2_epi(da, u):
    return (da * 2.0 * jnp.maximum(u.astype(f32), 0.0),)


def _mlp_fwd(h, gain, w_up, w_down, tag):
    xm = _rw(lambda x, g: (_rms(x, g),), [h], [gain], [(D_MODEL, bf16)], name=f"mlp{tag}_norm")[0]
    u, a = _mm(xm, w_up, name=f"mlp{tag}_up", outs=(bf16, bf16), epi=_relu2_epi)
    h_out = _mm(a, w_down, name=f"mlp{tag}_down", epi=_add_epi, extras=(h,))
    return h_out, (xm, u, a)


def _mlp_bwd(dh, dh16, h, gain, w_up, w_down, saved, tag):
    xm, u, a = saved
    du = _mm(dh16, w_down, tb=True, name=f"mlp{tag}_dact", outs=(bf16,), epi=_drelu2_epi, extras=(u,))
    d_down = _wgrad(a, dh16, f"mlp{tag}_dwdown")
    d_up = _wgrad(xm, du, f"mlp{tag}_dwup")
    dxm = _mm(du, w_up, tb=True, name=f"mlp{tag}_dxm")

    def fn(x, dy, dres, g):
        dx, dg = _rms_bwd(x, g, dy)
        return dx + dres, dx + dres, dg

    dh_in, dh_in16, d_gain = _rw(fn, [h, dxm, dh], [gain], [(D_MODEL, f32), (D_MODEL, bf16)], [(1, D_MODEL)],
                                 name=f"mlp{tag}_dnorm")
    return dh_in, dh_in16, d_gain, d_up, d_down


def _local_step(x, target, w, comm=None):
    w = dict(w)
    t = x.shape[0]
    cos, sgn_sin = _rope_tables(t)
    grads = {}

    xn0 = _rw(lambda xx, g: (_rms(xx, g),), [x], [w["hgrn_norm"]], [(D_MODEL, bf16)], name="hgrn_norm")[0]
    p4 = _mm(xn0, w["hgrn_w4"], name="hgrn_proj", tn=2048)

    if comm is None:
        y, states = _gla_fwd(p4, w["hgrn_lb_logits"], w["hgrn_g_norm"])
    else:
        y, states, gathered = _gla_fwd(p4, w["hgrn_lb_logits"], w["hgrn_g_norm"], gather=comm.shard["gla"])
        w.update(comm.unpack["gla"](gathered))
    h1 = _mm(y, w["hgrn_w_o"], name="hgrn_out", epi=_add_epi, extras=(x,))
    h2, mlp0 = _mlp_fwd(h1, w["mlp_norm"][0:1], w["mlp_w_up", 0], w["mlp_w_down", 0], 0)

    hk, xn1 = _rw(lambda hh, g1, g2: (_rms(hh, g1), _rms(hh, g2)), [h2], [w["kv_in_norm"], w["mla_norm"]],
                  [(D_MODEL, bf16)] * 2, name="kv_mla_norm")
    ckr = _mm(hk, w["kv_w_dkv"], name="kv_down")

    def ckv_fn(c, cs, sn, g):
        kr = _rope(c[:, MLA_KV_LORA:], cs, sn)
        return _rms(c[:, :MLA_KV_LORA], g), jnp.concatenate([jnp.zeros_like(kr), kr], axis=1)

    c_kv, kr_head = _rw(ckv_fn, [ckr, cos, sgn_sin], [w["kv_norm"]], [(MLA_KV_LORA, bf16), (ATT_QK, f32)],
                        name="kv_norm_rope")
    kc = _mm(c_kv, w["kv_w_kcat"], name="kv_up_k", outs=(bf16,), extras=(kr_head,),
             epi=lambda r, kr: (r + _tile_lanes(kr, r.shape[1]),))
    v_att = _mm(c_kv, w["kv_w_uv"], name="kv_up_v", outs=(bf16,))
    cq0 = _mm(xn1, w["mla_w_dq"], name="q_down")
    c_q = _rw(lambda c, g: (_rms(c, g),), [cq0], [w["mla_q_norm"]], [(MLA_Q_LORA, bf16)], name="q_norm")[0]
    qc = _mm(c_q, w["mla_w_qcat"], name="q_up", outs=(bf16,), extras=(cos, sgn_sin),
             epi=lambda r, cs, sn: (_rope_heads(r, cs, sn, 1.0, ATT_SCALE),))
    if comm is None:
        o_att, lse = _attn_fwd(qc, kc, v_att)
    else:
        o_att, lse, gathered = _attn_fwd(qc, kc, v_att, gather=comm.shard["attn"])
        w.update(comm.unpack["attn"](gathered))
    h3 = _mm(o_att, w["mla_w_o"], name="mla_out", epi=_add_epi, extras=(h2,))
    h4, mlp1 = _mlp_fwd(h3, w["mlp_norm"][1:2], w["mlp_w_up", 1], w["mlp_w_down", 1], 1)

    def loss_fn(hh, tgt, gain):
        def f(a, b):
            e = _rms(a, b) - tgt
            return 0.5 * jnp.sum(jnp.sum(e * e, axis=-1, keepdims=True) / D_MODEL, axis=0, keepdims=True)
        val, vjp = jax.vjp(f, hh, gain)
        dh, dg = vjp(jnp.ones((1, 1), f32))
        return dh, dh, jnp.broadcast_to(val, (1, LANES)), dg

    dh4, dh4_16, loss_acc, grads["final_norm"] = _rw(loss_fn, [h4, target], [w["final_norm"]],
                                                     [(D_MODEL, f32), (D_MODEL, bf16)], [(1, LANES), (1, D_MODEL)],
                                                     name="loss")
    loss = loss_acc[0, 0]

    dh3, dh3_16, g_n1, grads["mlp_w_up", 1], grads["mlp_w_down", 1] = _mlp_bwd(
        dh4, dh4_16, h3, w["mlp_norm"][1:2], w["mlp_w_up", 1], w["mlp_w_down", 1], mlp1, 1)
    do_att = _mm(dh3_16, w["mla_w_o"], tb=True, name="mla_dout", outs=(bf16,))
    grads["mla_w_o"] = _wgrad(o_att, dh3_16, "mla_dwo")

    def delta_fn(a, b):
        prod = a.astype(f32) * b.astype(f32)
        outs = []
        for p in range(MLA_HEADS // 2):
            d0 = jnp.sum(prod[:, 2 * p * LANES:(2 * p + 1) * LANES], axis=-1, keepdims=True)
            d1 = jnp.sum(prod[:, (2 * p + 1) * LANES:(2 * p + 2) * LANES], axis=-1, keepdims=True)
            lo, _ = _pair_masks((a.shape[0], LANES))
            outs.append(jnp.where(lo, d0, d1))
        return (jnp.concatenate(outs, axis=1),)

    delta = _rw(delta_fn, [do_att, o_att], [], [(MLA_HEADS // 2 * LANES, f32)], name="attn_delta")[0]
    if comm is None:
        dqc, dkc, dv = _attn_bwd(qc, kc, v_att, do_att, lse, delta)
    else:
        dqc, dkc, dv, comm.received["attn"] = _attn_bwd(qc, kc, v_att, do_att, lse, delta,
                                                        exchange=comm.reduce(grads, "attn"))
    dqf = _rw(lambda a, cs, sn: (_rope_heads(a, cs, sn, -1.0, ATT_SCALE),), [dqc, cos, sgn_sin], [],
              [(MLA_HEADS * ATT_QK, bf16)], name="dq_rope")[0]
    dc_q = _mm(dqf, w["mla_w_qcat"], tb=True, name="q_up_dx")
    grads["mla_w_qcat"] = _wgrad(c_q, dqf, "q_up_dw")

    def dqn_fn(c, dy, g):
        return _rms_bwd(c, g, dy)

    dcq0, grads["mla_q_norm"] = _rw(dqn_fn, [cq0, dc_q], [w["mla_q_norm"]], [(MLA_Q_LORA, bf16)], [(1, MLA_Q_LORA)],
                                    name="q_dnorm")
    dxn1 = _mm(dcq0, w["mla_w_dq"], tb=True, name="q_down_dx")
    grads["mla_w_dq"] = _wgrad(xn1, dcq0, "q_down_dw")

    dc_kv = _mm(dkc, w["kv_w_kcat"], tb=True, name="kv_up_dx_k")
    dc_kv = _mm(dv, w["kv_w_uv"], tb=True, name="kv_up_dx_v", epi=_add_epi, extras=(dc_kv,))
    grads["kv_w_kcat"] = _wgrad(c_kv, dkc, "kv_up_dw_k")
    grads["kv_w_uv"] = _wgrad(c_kv, dv, "kv_up_dw_v")

    def dckr_fn(c, dc, dk_heads, cs, sn, g):
        tot = dk_heads[:, LANES:ATT_QK].astype(f32)
        for h in range(1, MLA_HEADS):
            tot = tot + dk_heads[:, ATT_QK * h + LANES:ATT_QK * (h + 1)].astype(f32)
        lo, _ = _pair_masks(tot.shape)
        dkr = jnp.where(lo, _rope(tot, cs, sn, -1.0), 0.0)
        dcc, dg = _rms_bwd(c[:, :MLA_KV_LORA], g, dc)
        return jnp.concatenate([dcc, dkr], axis=1), dg

    dckr, grads["kv_norm"] = _rw(dckr_fn, [ckr, dc_kv, dkc, cos, sgn_sin], [w["kv_norm"]],
                                 [(MLA_KV_LORA + LANES, bf16)], [(1, MLA_KV_LORA)], name="kv_dnorm_rope")
    dhk = _mm(dckr, w["kv_w_dkv"], tb=True, name="kv_down_dx")
    grads["kv_w_dkv"] = _wgrad(hk, dckr, "kv_down_dw")

    def dh2_fn(hh, d1, d2, dres, g1, g2):
        a, ga = _rms_bwd(hh, g1, d1)
        b, gb = _rms_bwd(hh, g2, d2)
        return a + b + dres, a + b + dres, ga, gb

    dh2, dh2_16, grads["kv_in_norm"], grads["mla_norm"] = _rw(
        dh2_fn, [h2, dhk, dxn1, dh3], [w["kv_in_norm"], w["mla_norm"]], [(D_MODEL, f32), (D_MODEL, bf16)],
        [(1, D_MODEL)] * 2, name="kv_mla_dnorm")

    dh1, dh1_16, g_n0, grads["mlp_w_up", 0], grads["mlp_w_down", 0] = _mlp_bwd(
        dh2, dh2_16, h1, w["mlp_norm"][0:1], w["mlp_w_up", 0], w["mlp_w_down", 0], mlp0, 0)
    grads["mlp_norm"] = jnp.concatenate([g_n0, g_n1], axis=0)
    dy = _mm(dh1_16, w["hgrn_w_o"], tb=True, name="hgrn_dout")
    grads["hgrn_w_o"] = _wgrad(y, dh1_16, "hgrn_dwo")

    gla_args = (p4, w["hgrn_lb_logits"], w["hgrn_g_norm"], states, dy)
    if comm is None:
        dp4, grads["hgrn_lb_logits"], grads["hgrn_g_norm"] = _gla_bwd(*gla_args)
    else:
        dp4, grads["hgrn_lb_logits"], grads["hgrn_g_norm"], comm.received["gla"] = _gla_bwd(
            *gla_args, exchange=comm.reduce(grads, "gla"))
    dxn0 = _mm(dp4, w["hgrn_w4"], tb=True, name="hgrn_proj_dx")
    grads["hgrn_w4"] = _mm(xn0, dp4, ta=True, name="hgrn_proj_dw")

    def dx_fn(xx, dyy, dres, gn):
        dxx, dgn = _rms_bwd(xx, gn, dyy)
        return dxx + dres, dgn

    grad_x, grads["hgrn_norm"] = _rw(dx_fn, [x, dxn0, dh1], [w["hgrn_norm"]], [(D_MODEL, f32)], [(1, D_MODEL)],
                                     name="hgrn_dnorm")
    return loss, grad_x, grads


HBM = pl.BlockSpec(memory_space=pltpu.HBM)


def _me():
    return lax.axis_index("x"), lax.axis_index("y"), lax.axis_index("c")


def _flip(x, y, f):
    return (1 - x if f & 1 else x), (1 - y if f & 2 else y)


def _rcopy(src, dst, sems, k, dev):
    return pltpu.make_async_remote_copy(src_ref=src, dst_ref=dst, send_sem=sems.at[0, k], recv_sem=sems.at[1, k],
                                        device_id=dev, device_id_type=MESH)


def _my_half(rows, c, mine=True):
    half = rows // 2
    return pl.ds(pl.multiple_of((c if mine else 1 - c) * half, 16), half)


def _gather_start(wp_ref, out_ref, sems):
    x, y, c = _me()
    half = _my_half(wp_ref.shape[0], c)
    for f in (1, 2, 3):
        px, py = _flip(x, y, f)
        _rcopy(wp_ref.at[half], out_ref.at[2 * x + y, half], sems, f - 1, (px, py, c)).start()


def _gather_finish(wp_ref, out_ref, sems):
    x, y, c = _me()
    half, other = _my_half(wp_ref.shape[0], c), _my_half(wp_ref.shape[0], c, mine=False)
    sends = []
    for f in (1, 2, 3):
        px, py = _flip(x, y, f)
        landed = out_ref.at[2 * px + py, half]
        _rcopy(landed, landed, sems, f - 1, (px, py, c)).wait_recv()
        sends.append(_rcopy(landed, landed, sems, 2 + f, (x, y, 1 - c)))
        sends[-1].start()
    for f in (1, 2, 3):
        px, py = _flip(x, y, f)
        theirs = out_ref.at[2 * px + py, other]
        _rcopy(theirs, theirs, sems, 2 + f, (x, y, 1 - c)).wait_recv()
        sends.append(_rcopy(wp_ref.at[half], out_ref.at[2 * x + y, half], sems, f - 1, (px, py, c)))
    for cp in sends:
        cp.wait_send()


def _gather_base(wp):
    return jnp.broadcast_to(wp[None], (4,) + wp.shape)


def _all_gather_weights(wp, sv):
    def body(wp_ref, sv_ref, base_ref, out_ref, svs_ref, sems, local_sem):
        x, y, c = _me()
        mine = pltpu.make_async_copy(sv_ref, svs_ref.at[2 * x + y], local_sem)
        mine.start()
        _gather_start(wp_ref, out_ref, sems)
        small = []
        for f in (1, 2, 3):
            px, py = _flip(x, y, f)
            small.append(_rcopy(sv_ref, svs_ref.at[2 * x + y], sems, 5 + f, (px, py, c)))
            small[-1].start()
        _gather_finish(wp_ref, out_ref, sems)
        for f in (1, 2, 3):
            px, py = _flip(x, y, f)
            _rcopy(sv_ref, svs_ref.at[2 * px + py], sems, 5 + f, (px, py, c)).wait_recv()
        for cp in small:
            cp.wait_send()
        mine.wait()

    return _pc(body, name="weights_all_gather", in_specs=[HBM, HBM, HBM], out_specs=[HBM, HBM],
               out_shape=[_sds((4,) + wp.shape, bf16), _sds((4, 8, 256), f32)], aliases={2: 0},
               scratch=[pltpu.SemaphoreType.DMA((2, 9)), pltpu.SemaphoreType.DMA])(wp, sv, _gather_base(wp))


def _send_half_to_sibling(gp, name):
    rows = gp.shape[1]

    def body(gp_ref, out_ref, sems):
        x, y, c = _me()
        cp = _rcopy(gp_ref.at[:, _my_half(rows, c, mine=False)], out_ref, sems, 0, (x, y, 1 - c))
        cp.start()
        cp.wait()

    return _pc(body, name=name, in_specs=[HBM], out_specs=HBM, out_shape=_sds((4, rows // 2, D_MODEL), gp.dtype),
               scratch=[pltpu.SemaphoreType.DMA((2, 1))])(gp)


def _chips_start(sb_ref, out_ref, sems):
    x, y, c = _me()
    for f in (1, 2, 3):
        px, py = _flip(x, y, f)
        _rcopy(sb_ref.at[2 * px + py], out_ref.at[f - 1], sems, f - 1, (px, py, c)).start()


def _chips_finish(sb_ref, out_ref, sems):
    x, y, c = _me()
    for f in (1, 2, 3):
        _rcopy(sb_ref.at[0], out_ref.at[f - 1], sems, f - 1, (x, y, c)).wait_recv()
    for f in (1, 2, 3):
        px, py = _flip(x, y, f)
        _rcopy(sb_ref.at[2 * px + py], out_ref.at[f - 1], sems, f - 1, (px, py, c)).wait_send()


def _exchange_chips(sb, small):
    def body(sb_ref, small_ref, out_ref, smalls_ref, sems, local_sem):
        x, y, c = _me()
        me = 4 * x + 2 * y + c
        mine = pltpu.make_async_copy(small_ref, smalls_ref.at[me], local_sem)
        mine.start()
        _chips_start(sb_ref, out_ref, sems)
        sends = []
        for f in range(1, 8):
            px, py = _flip(x, y, f)
            pc = 1 - c if f & 4 else c
            sends.append(_rcopy(small_ref, smalls_ref.at[me], sems, 2 + f, (px, py, pc)))
            sends[-1].start()
        _chips_finish(sb_ref, out_ref, sems)
        for f in range(1, 8):
            px, py = _flip(x, y, f)
            pc = 1 - c if f & 4 else c
            _rcopy(small_ref, smalls_ref.at[4 * px + 2 * py + pc], sems, 2 + f, (x, y, c)).wait_recv()
        for cp in sends:
            cp.wait_send()
        mine.wait()

    return _pc(body, name="grads_exchange_chips", in_specs=[HBM, HBM], out_specs=[HBM, HBM],
               out_shape=[_sds((3,) + sb.shape[1:], bf16), _sds((8, SMALL_ROWS, D_MODEL), f32)],
               scratch=[pltpu.SemaphoreType.DMA((2, 10)), pltpu.SemaphoreType.DMA])(sb, small)


def _exchange_halves(tot, name):
    rows = tot.shape[0]

    def body(tot_ref, out_ref, sems):
        x, y, c = _me()
        half = _my_half(rows, c)
        cp = _rcopy(tot_ref.at[half], out_ref.at[half], sems, 0, (x, y, 1 - c))
        cp.start()
        cp.wait()

    return _pc(body, name=name, in_specs=[HBM], out_specs=HBM, out_shape=_sds((rows, D_MODEL), f32),
               aliases={0: 0}, scratch=[pltpu.SemaphoreType.DMA((2, 1))])(tot)


def _sum_rows(half):
    return max(r for r in range(16, 513, 16) if half % r == 0)


def _sum_over_cores(gp, recv, cq, name):
    half = recv.shape[1]
    tr = _sum_rows(half)
    nb = half // tr

    def body(cq_ref, g_ref, r_ref, o32_ref, o16_ref):
        s = g_ref[...].astype(f32) + r_ref[...].astype(f32)
        o32_ref[...] = s
        o16_ref[...] = s.astype(bf16)

    spec = pl.BlockSpec((1, tr, D_MODEL), lambda b, i, cq_ref: (b, i, 0))
    gs = pltpu.PrefetchScalarGridSpec(
        num_scalar_prefetch=1, grid=(4, nb),
        in_specs=[pl.BlockSpec((1, tr, D_MODEL), lambda b, i, cq_ref: (b, cq_ref[0] * nb + i, 0)), spec],
        out_specs=[spec, spec])
    return _pc(body, name=name, grid_spec=gs, sem=("arbitrary", "arbitrary"),
               out_shape=[_sds((4, half, D_MODEL), f32), _sds((4, half, D_MODEL), bf16)])(cq, gp, recv)


def _sum_over_chips(s32, recv, cq, name):
    half = recv.shape[1]
    tr = _sum_rows(half)
    nb = half // tr

    def body(cq_ref, own_ref, r_ref, o_ref):
        o_ref[...] = ((own_ref[0] + r_ref[0].astype(f32)) + r_ref[1].astype(f32)) + r_ref[2].astype(f32)

    gs = pltpu.PrefetchScalarGridSpec(
        num_scalar_prefetch=1, grid=(nb,),
        in_specs=[pl.BlockSpec((1, tr, D_MODEL), lambda i, cq_ref: (cq_ref[1], i, 0)),
                  pl.BlockSpec((3, tr, D_MODEL), lambda i, cq_ref: (0, i, 0))],
        out_specs=pl.BlockSpec((tr, D_MODEL), lambda i, cq_ref: (cq_ref[0] * nb + i, 0)))
    return _pc(body, name=name, grid_spec=gs, sem=("arbitrary",),
               out_shape=_sds((2 * half, D_MODEL), f32))(cq, s32, recv)


def _sum_small(smalls):
    def body(s_ref, o_ref):
        tot = s_ref[0]
        for d in range(1, 8):
            tot = tot + s_ref[d]
        o_ref[...] = tot

    return _pc(body, name="small_sum", out_shape=_sds((SMALL_ROWS, D_MODEL), f32))(smalls)


def _adamw_math(w, g, m, v):
    m = ADAM_B1 * m + (1.0 - ADAM_B1) * g
    v = ADAM_B2 * v + (1.0 - ADAM_B2) * jnp.square(g)
    m_hat = m / (1.0 - ADAM_B1 ** ADAM_STEP)
    v_hat = v / (1.0 - ADAM_B2 ** ADAM_STEP)
    delta = -ADAM_LR * (m_hat / (jnp.sqrt(v_hat) + ADAM_EPS) + ADAM_WD * w)
    return delta, m, v


def _adamw(w, g, m, v, name):
    cols = w.shape[1]
    return _rw(_adamw_math, [w, g, m, v], [], [(cols, f32)] * 3, name=name, tr=256)


def _adamw_small(items):
    n = len(items)

    def body(*refs):
        ins, outs = refs[:4 * n], refs[4 * n:]
        for i in range(n):
            res = _adamw_math(*[r[...] for r in ins[4 * i:4 * i + 4]])
            for o, val in zip(outs[3 * i:3 * i + 3], res):
                o[...] = val

    flat = [a for it in items for a in it]
    out_shape = [_sds(it[0].shape, f32) for it in items for _ in range(3)]
    res = _pc(body, name="adamw_small", out_shape=out_shape)(*flat)
    return [tuple(res[3 * i:3 * i + 3]) for i in range(n)]


def _pack_shards(sh, layout, pad):
    parts = [(sh[n] if layer is None else sh[n][layer]).reshape(-1, D_MODEL).astype(bf16) for n, layer, _ in layout]
    if pad:
        parts.append(jnp.zeros((pad, D_MODEL), bf16))
    return jnp.concatenate(parts, axis=0)


def _mlp_full(g4, off, layer):
    o, r = off["mlp_w_up", layer]
    up = g4[:, o:o + r].transpose(1, 0, 2).reshape(D_MODEL, D_FF)
    o, r = off["mlp_w_down", layer]
    return {("mlp_w_up", layer): up, ("mlp_w_down", layer): g4[:, o:o + r].reshape(D_FF, D_MODEL)}


def _unpack_early(g4):
    w = _mlp_full(g4, W_EARLY_OFF, 0)
    hg = g4[:, 0:1024].reshape(4, 4, 256, D_MODEL)
    w["hgrn_w4"] = hg.transpose(0, 2, 1, 3).reshape(D_MODEL, 4 * D_MODEL)
    o, r = W_EARLY_OFF["hgrn_w_o", None]
    w["hgrn_w_o"] = g4[:, o:o + r].reshape(D_MODEL, D_MODEL)
    return w


def _unpack_last(g4):
    return _mlp_full(g4, W_LAST_OFF, 1)


def _unpack_mid(g4):
    def rows(name):
        o, r = W_MID_OFF[name, None]
        return g4[:, o:o + r]

    w = {"mla_w_dq": rows("mla_w_dq").reshape(D_MODEL, MLA_Q_LORA)}
    uq = rows("mla_w_uq").reshape(4, MLA_Q_LORA, 768).transpose(1, 0, 2).reshape(MLA_Q_LORA, MLA_HEADS, MLA_NOPE + MLA_ROPE)
    w["mla_w_qcat"] = jnp.pad(uq, ((0, 0), (0, 0), (0, ATT_QK - MLA_NOPE - MLA_ROPE))).reshape(MLA_Q_LORA, MLA_HEADS * ATT_QK)
    w["mla_w_o"] = rows("mla_w_o").reshape(MLA_HEADS * MLA_V, D_MODEL)
    dkv = rows("kv_w_dkv").reshape(D_MODEL, MLA_KV_LORA + MLA_ROPE)
    w["kv_w_dkv"] = jnp.pad(dkv, ((0, 0), (0, LANES - MLA_ROPE)))
    uk = rows("kv_w_uk").reshape(4, MLA_KV_LORA, 512).transpose(1, 0, 2).reshape(MLA_KV_LORA, MLA_HEADS, MLA_NOPE)
    w["kv_w_kcat"] = jnp.pad(uk, ((0, 0), (0, 0), (0, ATT_QK - MLA_NOPE))).reshape(MLA_KV_LORA, MLA_HEADS * ATT_QK)
    w["kv_w_uv"] = rows("kv_w_uv").reshape(4, MLA_KV_LORA, 512).transpose(1, 0, 2).reshape(MLA_KV_LORA, MLA_HEADS * MLA_V)
    return w


def _pack_grads_late(g):
    return g["hgrn_w4"].reshape(4, 256, 4, D_MODEL).transpose(0, 2, 1, 3).reshape(4, G_LATE_ROWS, D_MODEL)


def _grad_rows(g, name, layer):
    if name in ("mlp_w_up", "mlp_w_down"):
        full = g[name, layer]
        return full.reshape(D_MODEL, 4, 1024).transpose(1, 0, 2) if name == "mlp_w_up" else full.reshape(4, 1024, D_MODEL)
    if name == "mla_w_uq":
        uq = g["mla_w_qcat"].reshape(MLA_Q_LORA, MLA_HEADS, ATT_QK)[:, :, :MLA_NOPE + MLA_ROPE]
        return uq.reshape(MLA_Q_LORA, 4, 768).transpose(1, 0, 2).reshape(4, 192, D_MODEL)
    if name == "kv_w_uk":
        uk = g["kv_w_kcat"].reshape(MLA_KV_LORA, MLA_HEADS, ATT_QK)[:, :, :MLA_NOPE]
        return uk.reshape(MLA_KV_LORA, 4, 512).transpose(1, 0, 2).reshape(4, 128, D_MODEL)
    if name == "kv_w_uv":
        return g[name].reshape(MLA_KV_LORA, 4, 512).transpose(1, 0, 2).reshape(4, 128, D_MODEL)
    if name == "kv_w_dkv":
        return g[name][:, :MLA_KV_LORA + MLA_ROPE].reshape(4, 80, D_MODEL)
    return g[name].reshape(4, -1, D_MODEL)


def _pack_grads(g, layout):
    parts = [_grad_rows(g, name, layer) for name, layer, _ in layout]
    if layout in PADDED:
        parts.append(jnp.zeros((4, PACK_PAD, D_MODEL), bf16))
    return jnp.concatenate(parts, axis=1)


LOSS_ROW = 11


def _pack_small(g, loss):
    rows = []
    for name, _, r, wd in SMALL:
        a = g[name].reshape(r, wd)
        rows.append(jnp.pad(a, ((0, 0), (0, D_MODEL - wd))) if wd < D_MODEL else a)
    assert sum(r for _, _, r, _ in SMALL) == LOSS_ROW
    rows.append(jnp.full((1, D_MODEL), loss, f32))
    rows.append(jnp.zeros((SMALL_ROWS - LOSS_ROW - 1, D_MODEL), f32))
    return jnp.concatenate(rows, axis=0)


def kernel(x, hgrn_norm, hgrn_w_q, hgrn_w_f, hgrn_w_i, hgrn_w_g, hgrn_g_norm, hgrn_w_o, hgrn_lb_logits, mla_norm, mla_w_dq, mla_q_norm, mla_w_uq, mla_w_o, kv_in_norm, kv_w_dkv, kv_norm, kv_w_uk, kv_w_uv, mlp_norm, mlp_w_up, mlp_w_down, final_norm, loss_target, m_hgrn_norm, m_hgrn_w_q, m_hgrn_w_f, m_hgrn_w_i, m_hgrn_w_g, m_hgrn_g_norm, m_hgrn_w_o, m_hgrn_lb_logits, m_mla_norm, m_mla_w_dq, m_mla_q_norm, m_mla_w_uq, m_mla_w_o, m_kv_in_norm, m_kv_w_dkv, m_kv_norm, m_kv_w_uk, m_kv_w_uv, m_mlp_norm, m_mlp_w_up, m_mlp_w_down, m_final_norm, v_hgrn_norm, v_hgrn_w_q, v_hgrn_w_f, v_hgrn_w_i, v_hgrn_w_g, v_hgrn_g_norm, v_hgrn_w_o, v_hgrn_lb_logits, v_mla_norm, v_mla_w_dq, v_mla_q_norm, v_mla_w_uq, v_mla_w_o, v_kv_in_norm, v_kv_w_dkv, v_kv_norm, v_kv_w_uk, v_kv_w_uv, v_mlp_norm, v_mlp_w_up, v_mlp_w_down, v_final_norm):
    given = dict(locals())
    wsh = {n: given[n] for n in WEIGHTS}
    msh = {n: given["m_" + n] for n in WEIGHTS}
    vsh = {n: given["v_" + n] for n in WEIGHTS}
    xi, yi, ci = _me()
    chip = 2 * xi + yi
    cq = jnp.stack([ci, chip]).astype(jnp.int32)

    small_w = {n: wsh[n].reshape(r, -1) for n, _, r, _ in SMALL}
    sv = jnp.concatenate([small_w["hgrn_norm"], small_w["hgrn_lb_logits"], jnp.zeros((5, 256), f32)], axis=0)
    g4, sv4 = _all_gather_weights(_pack_shards(wsh, W_EARLY, 0), sv)
    w = _unpack_early(g4)
    w["hgrn_norm"] = sv4[:, 0, :].reshape(1, D_MODEL)
    w["hgrn_lb_logits"] = sv4[:, 1:3, :].transpose(1, 0, 2).reshape(2, D_MODEL)
    for n in ("hgrn_g_norm", "mla_norm", "mla_q_norm", "kv_in_norm", "kv_norm", "mlp_norm", "final_norm"):
        w[n] = small_w[n]

    class Comm:
        shard = {"gla": _pack_shards(wsh, W_MID, PACK_PAD), "attn": _pack_shards(wsh, W_LAST, 0)}
        unpack = {"gla": _unpack_mid, "attn": _unpack_last}
        layout = {"gla": G_GLA, "attn": G_ATTN}
        received, s32 = {}, {}

        @staticmethod
        def reduce(grads, part):
            gp = _pack_grads(grads, Comm.layout[part])
            Comm.s32[part], s16 = _sum_over_cores(gp, _send_half_to_sibling(gp, "grads_to_sibling_" + part), cq,
                                                  "grads_sum_cores_" + part)
            return s16

    loss, grad_x, g = _local_step(x.reshape(-1, D_MODEL), loss_target.reshape(-1, D_MODEL), w, Comm)

    total = {part: _exchange_halves(_sum_over_chips(Comm.s32[part], Comm.received[part], cq, "grads_sum_chips_" + part),
                                    "grads_exchange_halves_" + part) for part in ("attn", "gla")}
    gp = _pack_grads_late(g)
    s32, s16 = _sum_over_cores(gp, _send_half_to_sibling(gp, "grads_to_sibling_late"), cq, "grads_sum_cores_late")
    from_chips, smalls = _exchange_chips(s16, _pack_small(g, loss))
    total["late"] = _exchange_halves(_sum_over_chips(s32, from_chips, cq, "grads_sum_chips_late"),
                                     "grads_exchange_halves_late")
    small_tot = _sum_small(smalls)
    loss = small_tot[LOSS_ROW, 0]

    where = {}
    for part, offsets in (("late", G_LATE_OFF), ("gla", G_GLA_OFF), ("attn", G_ATTN_OFF)):
        for (n, layer), (o, r) in offsets.items():
            where.setdefault(n, []).append(total[part][o:o + r])
    grad, delta, new_m, new_v = {}, {}, {}, {}
    for n, pieces in where.items():
        shp = wsh[n].shape
        two_d = (-1, shp[-1])
        grad[n] = (pieces[0] if len(pieces) == 1 else jnp.concatenate(pieces, axis=0)).reshape(shp)
        d, m2, v2 = _adamw(wsh[n].reshape(two_d), grad[n].reshape(two_d), msh[n].reshape(two_d), vsh[n].reshape(two_d),
                           "adamw_" + n)
        delta[n], new_m[n], new_v[n] = d.reshape(shp), m2.reshape(shp), v2.reshape(shp)
    items = []
    for n, row, r, wd in SMALL:
        gs = small_tot[row:row + r, :wd]
        if n in ("hgrn_norm", "hgrn_lb_logits"):
            gs = lax.dynamic_slice(gs, (0, 256 * chip), (r, 256))
        grad[n] = gs.reshape(wsh[n].shape)
        items.append((small_w[n], gs, msh[n].reshape(gs.shape), vsh[n].reshape(gs.shape)))
    for (n, _, _, _), (d, m2, v2) in zip(SMALL, _adamw_small(items)):
        shp = wsh[n].shape
        delta[n], new_m[n], new_v[n] = d.reshape(shp), m2.reshape(shp), v2.reshape(shp)

    return (loss, grad_x.reshape(x.shape), *[grad[n] for n in WEIGHTS], *[delta[n] for n in WEIGHTS],
            *[new_m[n] for n in WEIGHTS], *[new_v[n] for n in WEIGHTS])
```

```python
import functools

import jax
import jax.numpy as jnp
from jax import lax
from jax.experimental import pallas as pl
from jax.experimental.pallas import tpu as pltpu

f32, bf16 = jnp.float32, jnp.bfloat16
HI = lax.Precision.HIGHEST
MESH = pl.DeviceIdType.MESH

D_MODEL = 1024
D_FF = 4096
EPS = 1e-6
HGRN_HEADS, HGRN_DK, HGRN_CHUNK, HGRN_SUB = 8, 128, 64, 16
MLA_HEADS, MLA_NOPE, MLA_ROPE, MLA_V = 16, 128, 64, 128
MLA_Q_LORA, MLA_KV_LORA = 256, 256
ROPE_THETA = 10000.0
ATT_SCALE = (MLA_NOPE + MLA_ROPE) ** -0.5
EXP_CLAMP = 80.0

ADAM_LR, ADAM_B1, ADAM_B2, ADAM_EPS, ADAM_WD, ADAM_STEP = 0.001, 0.9, 0.999, 1e-08, 0.01, 10

V7X_VMEM_BYTES = 64 * 1024 * 1024
VMEM_LIMIT = V7X_VMEM_BYTES - 8 * 1024 * 1024
LANES = 128

PACK_PAD = 16
W_EARLY = (("hgrn_w_q", None, 256), ("hgrn_w_f", None, 256), ("hgrn_w_i", None, 256), ("hgrn_w_g", None, 256),
           ("hgrn_w_o", None, 256), ("mlp_w_up", 0, 1024), ("mlp_w_down", 0, 1024))
W_MID = (("mla_w_dq", None, 64), ("mla_w_uq", None, 192), ("mla_w_o", None, 512), ("kv_w_dkv", None, 80),
         ("kv_w_uk", None, 128), ("kv_w_uv", None, 128))
W_LAST = (("mlp_w_up", 1, 1024), ("mlp_w_down", 1, 1024))
G_LATE = (("hgrn_w_q", None, 256), ("hgrn_w_f", None, 256), ("hgrn_w_i", None, 256), ("hgrn_w_g", None, 256))
G_ATTN = (("mla_w_o", None, 512), ("mlp_w_up", 1, 1024), ("mlp_w_down", 1, 1024))
G_GLA = (("hgrn_w_o", None, 256), ("mla_w_dq", None, 64), ("mla_w_uq", None, 192), ("kv_w_dkv", None, 80),
         ("kv_w_uk", None, 128), ("kv_w_uv", None, 128), ("mlp_w_up", 0, 1024), ("mlp_w_down", 0, 1024))
PADDED = (W_MID, G_GLA)


def _offsets(layout):
    out, o = {}, 0
    for name, layer, rows in layout:
        out[name, layer] = (o, rows)
        o += rows
    return out, o + (PACK_PAD if layout in PADDED else 0)


W_EARLY_OFF, W_EARLY_ROWS = _offsets(W_EARLY)
W_MID_OFF, W_MID_ROWS = _offsets(W_MID)
W_LAST_OFF, W_LAST_ROWS = _offsets(W_LAST)
G_LATE_OFF, G_LATE_ROWS = _offsets(G_LATE)
G_ATTN_OFF, G_ATTN_ROWS = _offsets(G_ATTN)
G_GLA_OFF, G_GLA_ROWS = _offsets(G_GLA)
assert all(r % 32 == 0 for r in (W_EARLY_ROWS, W_MID_ROWS, W_LAST_ROWS, G_LATE_ROWS, G_ATTN_ROWS, G_GLA_ROWS))

WEIGHTS = ("hgrn_norm", "hgrn_w_q", "hgrn_w_f", "hgrn_w_i", "hgrn_w_g", "hgrn_g_norm", "hgrn_w_o", "hgrn_lb_logits",
           "mla_norm", "mla_w_dq", "mla_q_norm", "mla_w_uq", "mla_w_o", "kv_in_norm", "kv_w_dkv", "kv_norm", "kv_w_uk",
           "kv_w_uv", "mlp_norm", "mlp_w_up", "mlp_w_down", "final_norm")
SMALL = (("hgrn_norm", 0, 1, 1024), ("hgrn_lb_logits", 1, 2, 1024), ("hgrn_g_norm", 3, 1, 128),
         ("mla_norm", 4, 1, 1024), ("mla_q_norm", 5, 1, 256), ("kv_in_norm", 6, 1, 1024), ("kv_norm", 7, 1, 256),
         ("mlp_norm", 8, 2, 1024), ("final_norm", 10, 1, 1024))
SMALL_ROWS = 16


def _pc(body, *, name, out_shape, grid=None, in_specs=None, out_specs=None, scratch=(), sem=None, grid_spec=None,
        aliases=None):
    params = pltpu.CompilerParams(dimension_semantics=sem, vmem_limit_bytes=VMEM_LIMIT)
    if grid_spec is not None:
        return pl.pallas_call(body, name=name, out_shape=out_shape, grid_spec=grid_spec, compiler_params=params,
                              interpret=False)
    kw = {k: v for k, v in (("grid", grid), ("in_specs", in_specs), ("out_specs", out_specs),
                            ("input_output_aliases", aliases)) if v is not None}
    return pl.pallas_call(body, name=name, out_shape=out_shape, scratch_shapes=list(scratch), compiler_params=params,
                          interpret=False, **kw)


def _sds(shape, dtype):
    return jax.ShapeDtypeStruct(tuple(shape), dtype)


def _mm(a, b, *, name, ta=False, tb=False, outs=(f32,), epi=None, extras=(), accs=0, tm=1024, tn=1024, tk=4096):
    m, k = (a.shape[1], a.shape[0]) if ta else a.shape
    n = b.shape[0] if tb else b.shape[1]
    tm, tn, tk = min(tm, m), min(tn, n), min(tk, k)
    assert m % tm == 0 and n % tn == 0 and k % tk == 0, (name, m, n, k)
    nk = k // tk
    assert accs == 0 or (tn == n and nk == 1), name
    a_spec = pl.BlockSpec((tk, tm), lambda i, j, kk: (kk, i)) if ta else pl.BlockSpec((tm, tk), lambda i, j, kk: (i, kk))
    b_spec = pl.BlockSpec((tn, tk), lambda i, j, kk: (j, kk)) if tb else pl.BlockSpec((tk, tn), lambda i, j, kk: (kk, j))

    def extra_spec(e):
        if e.shape == (m, n):
            return pl.BlockSpec((tm, tn), lambda i, j, kk: (i, j))
        if e.shape[0] == m:
            return pl.BlockSpec((tm, e.shape[1]), lambda i, j, kk: (i, 0))
        return pl.BlockSpec((e.shape[0], tn), lambda i, j, kk: (0, j))

    e_specs = [extra_spec(e) for e in extras]
    dn = (((0 if ta else 1,), (1 if tb else 0,)), ((), ()))
    n_e, n_o = len(extras), len(outs)

    def finish(r, e_refs, o_refs):
        res = epi(r, *[e[...] for e in e_refs]) if epi is not None else (r,)
        for o, v in zip(o_refs[:n_o], res[:n_o]):
            o[...] = v.astype(o.dtype)
        for o, v in zip(o_refs[n_o:], res[n_o:]):
            @pl.when(pl.program_id(0) == 0)
            def _(o=o):
                o[...] = jnp.zeros_like(o)
            o[...] += v

    def body(*refs):
        a_ref, b_ref = refs[0], refs[1]
        e_refs = refs[2:2 + n_e]
        o_refs = refs[2 + n_e:2 + n_e + n_o + accs]
        prod = lax.dot_general(a_ref[...].astype(bf16), b_ref[...].astype(bf16), dn, preferred_element_type=f32)
        if nk == 1:
            finish(prod, e_refs, o_refs)
            return
        acc = refs[-1]
        kk = pl.program_id(2)

        @pl.when(kk == 0)
        def _():
            acc[...] = jnp.zeros_like(acc)

        acc[...] += prod

        @pl.when(kk == nk - 1)
        def _():
            finish(acc[...], e_refs, o_refs)

    out = _pc(body, name=name, grid=(m // tm, n // tn, nk),
              in_specs=[a_spec, b_spec] + e_specs,
              out_specs=[pl.BlockSpec((tm, tn), lambda i, j, kk: (i, j)) for _ in outs] +
                        [pl.BlockSpec((1, n), lambda i, j, kk: (0, 0))] * accs,
              out_shape=[_sds((m, n), dt) for dt in outs] + [_sds((1, n), f32)] * accs,
              scratch=[pltpu.VMEM((tm, tn), f32)] if nk > 1 else [],
              sem=("arbitrary" if accs else "parallel", "parallel", "arbitrary"))(a, b, *extras)
    return out[0] if n_o + accs == 1 else out


def _wgrad(a, b, name):
    return _mm(a, b, ta=True, name=name, outs=(bf16,))


def _rw(fn, rows, bcast, outs, accs=(), *, name, tr=256):
    t = rows[0].shape[0]
    tr = min(tr, t)
    assert t % tr == 0
    n_r, n_b, n_o, n_a = len(rows), len(bcast), len(outs), len(accs)

    def body(*refs):
        r_refs = refs[:n_r]
        b_refs = refs[n_r:n_r + n_b]
        o_refs = refs[n_r + n_b:n_r + n_b + n_o]
        a_refs = refs[n_r + n_b + n_o:]
        res = fn(*[r[...] for r in r_refs], *[b[...] for b in b_refs])
        for o, v in zip(o_refs, res[:n_o]):
            o[...] = v.astype(o.dtype)
        i = pl.program_id(0)
        for a_ref, v in zip(a_refs, res[n_o:]):
            @pl.when(i == 0)
            def _(a_ref=a_ref):
                a_ref[...] = jnp.zeros_like(a_ref)
            a_ref[...] += v

    in_specs = [pl.BlockSpec((tr, r.shape[1]), lambda i: (i, 0)) for r in rows]
    in_specs += [pl.BlockSpec(b.shape, lambda i: (0, 0)) for b in bcast]
    out_specs = [pl.BlockSpec((tr, w), lambda i: (i, 0)) for w, _ in outs]
    out_specs += [pl.BlockSpec(s, lambda i: (0, 0)) for s in accs]
    out_shape = [_sds((t, w), dt) for w, dt in outs] + [_sds(s, f32) for s in accs]
    res = _pc(body, name=name, grid=(t // tr,), in_specs=in_specs, out_specs=out_specs, out_shape=out_shape,
              sem=("arbitrary",))(*rows, *bcast)
    return res


def _rms(x, gain):
    return x * lax.rsqrt(jnp.mean(x * x, axis=-1, keepdims=True) + EPS) * gain


def _rms_bwd(x, gain, dy):
    _, vjp = jax.vjp(_rms, x, gain)
    return vjp(dy)


def _lower_bound(lbl):
    l0, l1 = lbl[0:1, :], lbl[1:2, :]
    mx = jnp.maximum(l0, l1)
    e0, e1 = jnp.exp(l0 - mx), jnp.exp(l1 - mx)
    return e0 / (e0 + e1)


def _gates(qpre, fpre, lbl):
    lb = _lower_bound(lbl)
    q = jax.nn.silu(qpre)
    forget = lb + (1.0 - lb) * jax.nn.sigmoid(fpre)
    return q, 1.0 - forget, jnp.log(forget)


def _head_norm_gate(o, gpre, gn):
    return _rms(o, gn) * jax.nn.silu(gpre)


def _swap_halves(x):
    w = x.shape[1]
    lane = lax.broadcasted_iota(jnp.int32, x.shape, 1)
    return jnp.where((lane % MLA_ROPE) < MLA_ROPE // 2, pltpu.roll(x, w - MLA_ROPE // 2, 1),
                     pltpu.roll(x, MLA_ROPE // 2, 1))


def _tile_lanes(tab, w):
    return tab if w == tab.shape[1] else jnp.concatenate([tab] * (w // tab.shape[1]), axis=1)


def _rope(x, cos, sgn_sin, sign=1.0):
    w = x.shape[1]
    return x * _tile_lanes(cos, w) + sign * _swap_halves(x) * _tile_lanes(sgn_sin, w)


def _rope_heads(x, cos, sgn_sin, sign, scale):
    parts = []
    for h in range(x.shape[1] // (2 * LANES)):
        parts.append(x[:, 2 * LANES * h:2 * LANES * h + LANES] * scale)
        parts.append(_rope(x[:, 2 * LANES * h + LANES:2 * LANES * (h + 1)], cos, sgn_sin, sign) * scale)
    return jnp.concatenate(parts, axis=1)


def _bd(a, b, ca, cb):
    return lax.dot_general(a.astype(bf16), b.astype(bf16), (((ca,), (cb,)), ((), ())), preferred_element_type=f32)


@jax.custom_vjp
def _dot_nn(a, b):
    return _bd(a, b, 1, 0)


@jax.custom_vjp
def _dot_nt(a, b):
    return _bd(a, b, 1, 1)


@jax.custom_vjp
def _dot_tn(a, b):
    return _bd(a, b, 0, 0)


_dot_nn.defvjp(lambda a, b: (_bd(a, b, 1, 0), (a, b)), lambda r, g: (_bd(g, r[1], 1, 1), _bd(r[0], g, 0, 0)))
_dot_nt.defvjp(lambda a, b: (_bd(a, b, 1, 1), (a, b)), lambda r, g: (_bd(g, r[1], 1, 0), _bd(g, r[0], 0, 0)))
_dot_tn.defvjp(lambda a, b: (_bd(a, b, 0, 0), (a, b)), lambda r, g: (_bd(r[1], g, 1, 1), _bd(r[0], g, 1, 0)))


def _scan_rows(x, reverse):
    n = x.shape[0]
    row = lax.broadcasted_iota(jnp.int32, x.shape, 0)
    s = 1
    while s < n:
        if reverse:
            x = x + jnp.where(row < n - s, pltpu.roll(x, n - s, 0), 0.0)
        else:
            x = x + jnp.where(row >= s, pltpu.roll(x, s, 0), 0.0)
        s *= 2
    return x


@jax.custom_vjp
def _cumsum_rows(g):
    return _scan_rows(g, False)


_cumsum_rows.defvjp(lambda g: (_scan_rows(g, False), None), lambda _, ct: (_scan_rows(ct, True),))

HGRN_PAIRS = HGRN_HEADS // 2
HGRN_PAIR = 2 * HGRN_DK
GLA_STATE = (HGRN_PAIRS, HGRN_PAIR, HGRN_PAIR)


def _gla_consts():
    s = HGRN_SUB
    r = lax.broadcasted_iota(jnp.int32, (HGRN_PAIR, HGRN_PAIR), 0)
    c = lax.broadcasted_iota(jnp.int32, (HGRN_PAIR, HGRN_PAIR), 1)
    pair_mask = (r < HGRN_DK) == (c < HGRN_DK)
    masks = []
    for i in range(HGRN_CHUNK // s):
        n = s * (i + 1)
        row = lax.broadcasted_iota(jnp.int32, (HGRN_HEADS * s, HGRN_HEADS * n), 0)
        col = lax.broadcasted_iota(jnp.int32, (HGRN_HEADS * s, HGRN_HEADS * n), 1)
        col_head = sum((col >= m * n).astype(jnp.int32) for m in range(1, HGRN_HEADS))
        masks.append((col_head == row // s) & (col - col_head * n <= s * i + row % s))
    return pair_mask, masks


def _heads_to_rows(x):
    return jnp.concatenate([x[:, HGRN_DK * h:HGRN_DK * (h + 1)] for h in range(HGRN_HEADS)], axis=0)


def _gla_chunk(consts, dots, q, k, v, g, st):
    pair_mask, masks = consts
    dot_nn, dot_nt, dot_tn = dots
    c, s = HGRN_CHUNK, HGRN_SUB
    b = _cumsum_rows(g)
    b_last = b[c - 1:c, :]
    q_in, k_out = q * jnp.exp(b), k * jnp.exp(b_last - b)
    o_inter, st_new = [], []
    for p in range(HGRN_PAIRS):
        cols = slice(HGRN_PAIR * p, HGRN_PAIR * (p + 1))
        o_inter.append(dot_nt(q_in[:, cols], st[p]))
        st_new.append(st[p] * jnp.exp(b_last[:, cols]) + jnp.where(pair_mask, dot_tn(v[:, cols], k_out[:, cols]), 0.0))
    intra = []
    for i in range(c // s):
        n = s * (i + 1)
        ref = b[s * i - 1:s * i, :] if i else jnp.zeros_like(b_last)
        qt = _heads_to_rows(q[s * i:n] * jnp.exp(b[s * i:n] - ref))
        kt = _heads_to_rows(k[:n] * jnp.exp(jnp.minimum(ref - b[:n], EXP_CLAMP)))
        sc = jnp.where(masks[i], dot_nt(qt, kt), 0.0)
        oi = dot_nn(sc, _heads_to_rows(v[:n]))
        intra.append(jnp.concatenate([oi[s * h:s * (h + 1)] for h in range(HGRN_HEADS)], axis=1))
    return jnp.concatenate(o_inter, axis=1) + jnp.concatenate(intra, axis=0), st_new


_PLAIN_DOTS = (lambda a, b: _bd(a, b, 1, 0), lambda a, b: _bd(a, b, 1, 1), lambda a, b: _bd(a, b, 0, 0))
_VJP_DOTS = (_dot_nn, _dot_nt, _dot_tn)


def _hgrn_mix(consts, dots, qpre, fpre, v, gpre, lbl, gn, st):
    q, k, g = _gates(qpre, fpre, lbl)
    o, st_new = _gla_chunk(consts, dots, q, k, v, g, st)
    y = [_head_norm_gate(o[:, HGRN_DK * h:HGRN_DK * (h + 1)], gpre[:, HGRN_DK * h:HGRN_DK * (h + 1)], gn)
         for h in range(HGRN_HEADS)]
    return jnp.concatenate(y, axis=1), st_new


def _gla_fwd(p4, lbl, gn, gather=None):
    t = p4.shape[0]
    nc = t // HGRN_CHUNK

    def body(q_ref, k_ref, v_ref, g_ref, lbl_ref, gn_ref, *rest):
        if gather is None:
            o_ref, s_ref, st = rest
        else:
            wp_ref, _, o_ref, s_ref, gathered_ref, st, sems = rest

        @pl.when(pl.program_id(0) == 0)
        def _():
            st[...] = jnp.zeros_like(st)
            if gather is not None:
                _gather_start(wp_ref, gathered_ref, sems)

        if gather is not None:
            @pl.when(pl.program_id(0) == nc - 1)
            def _():
                _gather_finish(wp_ref, gathered_ref, sems)

        s_in = [st[p] for p in range(HGRN_PAIRS)]
        y, st_new = _hgrn_mix(_gla_consts(), _PLAIN_DOTS, q_ref[...], k_ref[...], v_ref[...], g_ref[...],
                              lbl_ref[...], gn_ref[...], s_in)
        o_ref[...] = y.astype(o_ref.dtype)
        for p in range(HGRN_PAIRS):
            s_ref[0, p] = s_in[p]
            st[p] = st_new[p]

    blk = lambda off: pl.BlockSpec((HGRN_CHUNK, D_MODEL), lambda c: (c, off))
    whole = lambda a: pl.BlockSpec(a.shape, lambda c: (0, 0))
    state_shape = GLA_STATE
    in_specs = [blk(0), blk(1), blk(2), blk(3), whole(lbl), whole(gn)]
    out_specs = [blk(0), pl.BlockSpec((1,) + state_shape, lambda c: (c, 0, 0, 0))]
    out_shape = [_sds((t, D_MODEL), bf16), _sds((nc,) + state_shape, f32)]
    scratch = [pltpu.VMEM(state_shape, f32)]
    if gather is None:
        return _pc(body, name="gla_fwd", grid=(nc,), in_specs=in_specs, out_specs=out_specs, out_shape=out_shape,
                   scratch=scratch, sem=("arbitrary",))(p4, p4, p4, p4, lbl, gn)
    return _pc(body, name="gla_fwd_gather", grid=(nc,), in_specs=in_specs + [HBM, HBM], out_specs=out_specs + [HBM],
               out_shape=out_shape + [_sds((4,) + gather.shape, bf16)], aliases={7: 2},
               scratch=scratch + [pltpu.SemaphoreType.DMA((2, 6))], sem=("arbitrary",))(
                   p4, p4, p4, p4, lbl, gn, gather, _gather_base(gather))


def _gla_bwd(p4, lbl, gn, states, dy, exchange=None):
    t = p4.shape[0]
    nc = t // HGRN_CHUNK

    def body(q_ref, k_ref, v_ref, g_ref, lbl_ref, gn_ref, s_ref, dy_ref, *rest):
        if exchange is None:
            dp_ref, dlbl_ref, dgn_ref, dst = rest
        else:
            sb_ref, dp_ref, dlbl_ref, dgn_ref, recv_ref, dst, sems = rest

        @pl.when(pl.program_id(0) == 0)
        def _():
            dst[...] = jnp.zeros_like(dst)
            dlbl_ref[...] = jnp.zeros_like(dlbl_ref)
            dgn_ref[...] = jnp.zeros_like(dgn_ref)
            if exchange is not None:
                _chips_start(sb_ref, recv_ref, sems)

        if exchange is not None:
            @pl.when(pl.program_id(0) == nc - 1)
            def _():
                _chips_finish(sb_ref, recv_ref, sems)

        consts = _gla_consts()
        fn = lambda *args: _hgrn_mix(consts, _VJP_DOTS, *args)
        pairs = range(HGRN_PAIRS)
        _, vjp = jax.vjp(fn, q_ref[...], k_ref[...], v_ref[...], g_ref[...], lbl_ref[...], gn_ref[...],
                         [s_ref[0, p] for p in pairs])
        *d_proj, dlbl, dgn, ds = vjp((dy_ref[...], [dst[p] for p in pairs]))
        for i, d in enumerate(d_proj):
            dp_ref[:, D_MODEL * i:D_MODEL * (i + 1)] = d.astype(dp_ref.dtype)
        dlbl_ref[...] += dlbl
        dgn_ref[...] += dgn
        for p in pairs:
            dst[p] = ds[p]

    blk = lambda off: pl.BlockSpec((HGRN_CHUNK, D_MODEL), lambda c: (nc - 1 - c, off))
    whole = lambda a: pl.BlockSpec(a.shape, lambda c: (0, 0))
    state_shape = GLA_STATE
    in_specs = [blk(0), blk(1), blk(2), blk(3), whole(lbl), whole(gn),
                pl.BlockSpec((1,) + state_shape, lambda c: (nc - 1 - c, 0, 0, 0)), blk(0)]
    out_specs = [pl.BlockSpec((HGRN_CHUNK, 4 * D_MODEL), lambda c: (nc - 1 - c, 0)), whole(lbl), whole(gn)]
    out_shape = [_sds((t, 4 * D_MODEL), bf16), _sds(lbl.shape, f32), _sds(gn.shape, f32)]
    scratch = [pltpu.VMEM(state_shape, f32)]
    if exchange is None:
        return _pc(body, name="gla_bwd", grid=(nc,), in_specs=in_specs, out_specs=out_specs, out_shape=out_shape,
                   scratch=scratch, sem=("arbitrary",))(p4, p4, p4, p4, lbl, gn, states, dy)
    return _pc(body, name="gla_bwd_exchange", grid=(nc,), in_specs=in_specs + [HBM], out_specs=out_specs + [HBM],
               out_shape=out_shape + [_sds((3,) + exchange.shape[1:], bf16)],
               scratch=scratch + [pltpu.SemaphoreType.DMA((2, 3))], sem=("arbitrary",))(
                   p4, p4, p4, p4, lbl, gn, states, dy, exchange)


ATT_FWD_TQ, ATT_FWD_TK = 1024, 1024
ATT_BWD_TQ, ATT_BWD_TK = 1024, 512
ATT_QK = 2 * LANES
NEG = -1e30


def _pair_masks(shape):
    lane = lax.broadcasted_iota(jnp.int32, shape, 1)
    return lane < MLA_ROPE, lane >= MLA_ROPE


def _causal(shape, row0, col0):
    row = row0 + lax.broadcasted_iota(jnp.int32, shape, 0)
    col = col0 + lax.broadcasted_iota(jnp.int32, shape, 1)
    return col <= row


def _qk_cols(e):
    return slice(ATT_QK * e, ATT_QK * (e + 1))


def _v_cols(e):
    return slice(MLA_V * e, MLA_V * (e + 1))


def _first_last_step(n0, n1):
    p, i = pl.program_id(0), pl.program_id(1)
    return (p == 0) & (i == 0), (p == n0 - 1) & (i == n1 - 1)


def _attn_fwd(qc, kc, v, gather=None):
    t = qc.shape[0]
    tq, tk = min(ATT_FWD_TQ, t), min(ATT_FWD_TK, t)
    npair = MLA_HEADS // 2

    def body(q_ref, k_ref, v_ref, *rest):
        if gather is None:
            o_ref, lse_ref = rest
        else:
            wp_ref, _, o_ref, lse_ref, gathered_ref, sems = rest
            first, last = _first_last_step(npair, t // tq)
            pl.when(first)(lambda: _gather_start(wp_ref, gathered_ref, sems))
            pl.when(last)(lambda: _gather_finish(wp_ref, gathered_ref, sems))
        i = pl.program_id(1)
        n_full = (i * tq + 1) // tk
        nkv = (i * tq + tq + tk - 1) // tk
        q = [q_ref[:, _qk_cols(e)] for e in range(2)]

        def step(j, carry, masked):
            ks = pl.ds(pl.multiple_of(j * tk, tk), tk)
            ok = _causal((tq, tk), i * tq, j * tk) if masked else None
            new = []
            for e in range(2):
                m, l, acc = carry[e]
                s = _bd(q[e], k_ref[ks, _qk_cols(e)], 1, 1)
                if masked:
                    s = jnp.where(ok, s, NEG)
                m_new = jnp.maximum(m, jnp.max(s, axis=-1, keepdims=True))
                p = jnp.exp(s - m_new)
                alpha = jnp.exp(m - m_new)
                l = alpha * l + jnp.sum(p, axis=-1, keepdims=True)
                acc = alpha * acc + _bd(p, v_ref[ks, _v_cols(e)], 1, 0)
                new.append((m_new, l, acc))
            return tuple(new)

        one = (jnp.full((tq, 1), NEG, f32), jnp.zeros((tq, 1), f32), jnp.zeros((tq, MLA_V), f32))
        carry = lax.fori_loop(0, n_full, functools.partial(step, masked=False), (one, one))
        carry = lax.fori_loop(n_full, nkv, functools.partial(step, masked=True), carry)
        o_ref[...] = jnp.concatenate([acc / l for _, l, acc in carry], axis=1).astype(o_ref.dtype)
        lo, _ = _pair_masks((tq, LANES))
        lse_ref[...] = jnp.where(lo, *[m + jnp.log(l) for m, l, _ in carry])

    in_specs = [pl.BlockSpec((tq, 2 * ATT_QK), lambda p, i: (i, p)),
                pl.BlockSpec((t, 2 * ATT_QK), lambda p, i: (0, p)),
                pl.BlockSpec((t, 2 * MLA_V), lambda p, i: (0, p))]
    out_specs = [pl.BlockSpec((tq, 2 * MLA_V), lambda p, i: (i, p)), pl.BlockSpec((tq, LANES), lambda p, i: (i, p))]
    out_shape = [_sds((t, MLA_HEADS * MLA_V), bf16), _sds((t, npair * LANES), f32)]
    if gather is None:
        return _pc(body, name="attn_fwd", grid=(npair, t // tq), in_specs=in_specs, out_specs=out_specs,
                   out_shape=out_shape, sem=("arbitrary", "arbitrary"))(qc, kc, v)
    return _pc(body, name="attn_fwd_gather", grid=(npair, t // tq), in_specs=in_specs + [HBM, HBM],
               out_specs=out_specs + [HBM], out_shape=out_shape + [_sds((4,) + gather.shape, bf16)], aliases={4: 2},
               scratch=[pltpu.SemaphoreType.DMA((2, 6))], sem=("arbitrary", "arbitrary"))(
                   qc, kc, v, gather, _gather_base(gather))


def _attn_bwd(qc, kc, v, do, lse, delta, exchange=None):
    t = qc.shape[0]
    tq, tk = min(ATT_BWD_TQ, t), min(ATT_BWD_TK, t)
    npair = MLA_HEADS // 2
    nq = t // tq

    def body(q_ref, do_ref, lse_ref, dl_ref, k_ref, v_ref, *rest):
        if exchange is None:
            dq_ref, dk_ref, dv_ref = rest
        else:
            sb_ref, dq_ref, dk_ref, dv_ref, recv_ref, sems = rest
            first, last = _first_last_step(npair, t // tk)
            pl.when(first)(lambda: _chips_start(sb_ref, recv_ref, sems))
            pl.when(last)(lambda: _chips_finish(sb_ref, recv_ref, sems))
        j = pl.program_id(1)

        @pl.when(j == 0)
        def _():
            dq_ref[...] = jnp.zeros_like(dq_ref)

        k = [k_ref[:, _qk_cols(e)] for e in range(2)]
        vv = [v_ref[:, _v_cols(e)] for e in range(2)]

        def step(i, carry, masked):
            qs = pl.ds(pl.multiple_of(i * tq, tq), tq)
            ok = _causal((tq, tk), i * tq, j * tk) if masked else None
            lse2, dl2 = lse_ref[qs, :], dl_ref[qs, :]
            new = []
            for e in range(2):
                dk, dv = carry[e]
                q_e, do_e = q_ref[qs, _qk_cols(e)], do_ref[qs, _v_cols(e)]
                p = jnp.exp(_bd(q_e, k[e], 1, 1) - lse2[:, MLA_ROPE * e:MLA_ROPE * e + 1])
                if masked:
                    p = jnp.where(ok, p, 0.0)
                dv = dv + _bd(p, do_e, 0, 0)
                dp = _bd(do_e, vv[e], 1, 1)
                ds = (p * (dp - dl2[:, MLA_ROPE * e:MLA_ROPE * e + 1])).astype(bf16)
                dk = dk + _bd(ds, q_e, 0, 0)
                dq_ref[qs, _qk_cols(e)] += _bd(ds, k[e], 1, 0)
                new.append((dk, dv))
            return tuple(new)

        one = (jnp.zeros((tk, ATT_QK), f32), jnp.zeros((tk, MLA_V), f32))
        i_full = jnp.minimum((j * tk + tk + tq - 2) // tq, nq)
        carry = lax.fori_loop((j * tk) // tq, i_full, functools.partial(step, masked=True), (one, one))
        carry = lax.fori_loop(i_full, nq, functools.partial(step, masked=False), carry)
        for e in range(2):
            dk_ref[:, _qk_cols(e)] = carry[e][0].astype(dk_ref.dtype)
            dv_ref[:, _v_cols(e)] = carry[e][1].astype(dv_ref.dtype)

    res = lambda w: pl.BlockSpec((t, w), lambda p, j: (0, p))
    blk = lambda w: pl.BlockSpec((tk, w), lambda p, j: (j, p))
    in_specs = [res(2 * ATT_QK), res(2 * MLA_V), res(LANES), res(LANES), blk(2 * ATT_QK), blk(2 * MLA_V)]
    out_specs = [res(2 * ATT_QK), blk(2 * ATT_QK), blk(2 * MLA_V)]
    out_shape = [_sds((t, MLA_HEADS * ATT_QK), f32), _sds((t, MLA_HEADS * ATT_QK), bf16), _sds((t, MLA_HEADS * MLA_V), bf16)]
    if exchange is None:
        return _pc(body, name="attn_bwd", grid=(npair, t // tk), in_specs=in_specs, out_specs=out_specs,
                   out_shape=out_shape, sem=("arbitrary", "arbitrary"))(qc, do, lse, delta, kc, v)
    return _pc(body, name="attn_bwd_exchange", grid=(npair, t // tk), in_specs=in_specs + [HBM],
               out_specs=out_specs + [HBM], out_shape=out_shape + [_sds((3,) + exchange.shape[1:], bf16)],
               scratch=[pltpu.SemaphoreType.DMA((2, 3))], sem=("arbitrary", "arbitrary"))(
                   qc, do, lse, delta, kc, v, exchange)


def _rope_tables(t):
    half = MLA_ROPE // 2
    inv_freq = ROPE_THETA ** (-jnp.arange(half, dtype=f32) / half)
    ang = jnp.arange(t, dtype=f32)[:, None] * inv_freq[None, :]
    cos, sin = jnp.cos(ang), jnp.sin(ang)
    return jnp.concatenate([cos, cos] * 2, axis=1), jnp.concatenate([-sin, sin] * 2, axis=1)


def _relu2_epi(u):
    r = jnp.maximum(u, 0.0)
    return u, r * r


def _add_epi(r, res):
    return (r + res,)


def _drelu2_epi(da, u):
    return (da * 2.0 * jnp.maximum(u.astype(f32), 0.0),)


ROWWISE_EPI_TM = 512


def _residual_norms_epi(r, res, *gains):
    h = r + res
    return (h, *[_rms(h, g) for g in gains])


def _residual_out(a, w, h, gains, name):
    res = _mm(a, w, name=name, outs=(f32,) + (bf16,) * len(gains), epi=_residual_norms_epi, extras=(h, *gains),
              tm=ROWWISE_EPI_TM)
    return res if gains else [res]


def _dnorm_epi(dy, x, dres, gain):
    dx, dg = _rms_bwd(x, gain, dy)
    return dx + dres, dx + dres, dg


def _mlp_fwd(h, xm, w_up, w_down, tag, next_gains):
    u, a = _mm(xm, w_up, name=f"mlp{tag}_up", outs=(bf16, bf16), epi=_relu2_epi)
    h_out, *normed = _residual_out(a, w_down, h, next_gains, f"mlp{tag}_down")
    return h_out, normed, (xm, u, a)


def _mlp_bwd(dh, dh16, h, gain, w_up, w_down, saved, tag):
    xm, u, a = saved
    du = _mm(dh16, w_down, tb=True, name=f"mlp{tag}_dact", outs=(bf16,), epi=_drelu2_epi, extras=(u,))
    d_down = _wgrad(a, dh16, f"mlp{tag}_dwdown")
    d_up = _wgrad(xm, du, f"mlp{tag}_dwup")
    dh_in, dh_in16, d_gain = _mm(du, w_up, tb=True, name=f"mlp{tag}_dxm", outs=(f32, bf16), accs=1, epi=_dnorm_epi,
                                 extras=(h, dh, gain), tm=ROWWISE_EPI_TM)
    return dh_in, dh_in16, d_gain, d_up, d_down


def _local_step(x, target, w, comm=None):
    w = dict(w)
    t = x.shape[0]
    cos, sgn_sin = _rope_tables(t)
    grads = {}

    xn0 = _rw(lambda xx, g: (_rms(xx, g),), [x], [w["hgrn_norm"]], [(D_MODEL, bf16)], name="hgrn_norm")[0]
    p4 = _mm(xn0, w["hgrn_w4"], name="hgrn_proj", tn=2048)

    if comm is None:
        y, states = _gla_fwd(p4, w["hgrn_lb_logits"], w["hgrn_g_norm"])
    else:
        y, states, gathered = _gla_fwd(p4, w["hgrn_lb_logits"], w["hgrn_g_norm"], gather=comm.shard["gla"])
        w.update(comm.unpack["gla"](gathered))
    h1, xm0 = _residual_out(y, w["hgrn_w_o"], x, [w["mlp_norm"][0:1]], "hgrn_out")
    h2, (hk, xn1), mlp0 = _mlp_fwd(h1, xm0, w["mlp_w_up", 0], w["mlp_w_down", 0], 0, [w["kv_in_norm"], w["mla_norm"]])

    ckr = _mm(hk, w["kv_w_dkv"], name="kv_down")

    def ckv_fn(c, cs, sn, g):
        kr = _rope(c[:, MLA_KV_LORA:], cs, sn)
        return _rms(c[:, :MLA_KV_LORA], g), jnp.concatenate([jnp.zeros_like(kr), kr], axis=1)

    c_kv, kr_head = _rw(ckv_fn, [ckr, cos, sgn_sin], [w["kv_norm"]], [(MLA_KV_LORA, bf16), (ATT_QK, f32)],
                        name="kv_norm_rope")
    kc = _mm(c_kv, w["kv_w_kcat"], name="kv_up_k", outs=(bf16,), extras=(kr_head,),
             epi=lambda r, kr: (r + _tile_lanes(kr, r.shape[1]),))
    v_att = _mm(c_kv, w["kv_w_uv"], name="kv_up_v", outs=(bf16,))
    cq0 = _mm(xn1, w["mla_w_dq"], name="q_down")
    c_q = _rw(lambda c, g: (_rms(c, g),), [cq0], [w["mla_q_norm"]], [(MLA_Q_LORA, bf16)], name="q_norm")[0]
    qc = _mm(c_q, w["mla_w_qcat"], name="q_up", outs=(bf16,), extras=(cos, sgn_sin),
             epi=lambda r, cs, sn: (_rope_heads(r, cs, sn, 1.0, ATT_SCALE),))
    if comm is None:
        o_att, lse = _attn_fwd(qc, kc, v_att)
    else:
        o_att, lse, gathered = _attn_fwd(qc, kc, v_att, gather=comm.shard["attn"])
        w.update(comm.unpack["attn"](gathered))
    h3, xm1 = _residual_out(o_att, w["mla_w_o"], h2, [w["mlp_norm"][1:2]], "mla_out")
    h4, _, mlp1 = _mlp_fwd(h3, xm1, w["mlp_w_up", 1], w["mlp_w_down", 1], 1, [])

    def loss_fn(hh, tgt, gain):
        def f(a, b):
            e = _rms(a, b) - tgt
            return 0.5 * jnp.sum(jnp.sum(e * e, axis=-1, keepdims=True) / D_MODEL, axis=0, keepdims=True)
        val, vjp = jax.vjp(f, hh, gain)
        dh, dg = vjp(jnp.ones((1, 1), f32))
        return dh, dh, jnp.broadcast_to(val, (1, LANES)), dg

    dh4, dh4_16, loss_acc, grads["final_norm"] = _rw(loss_fn, [h4, target], [w["final_norm"]],
                                                     [(D_MODEL, f32), (D_MODEL, bf16)], [(1, LANES), (1, D_MODEL)],
                                                     name="loss")
    loss = loss_acc[0, 0]

    dh3, dh3_16, g_n1, grads["mlp_w_up", 1], grads["mlp_w_down", 1] = _mlp_bwd(
        dh4, dh4_16, h3, w["mlp_norm"][1:2], w["mlp_w_up", 1], w["mlp_w_down", 1], mlp1, 1)
    do_att = _mm(dh3_16, w["mla_w_o"], tb=True, name="mla_dout", outs=(bf16,))
    grads["mla_w_o"] = _wgrad(o_att, dh3_16, "mla_dwo")

    def delta_fn(a, b):
        prod = a.astype(f32) * b.astype(f32)
        outs = []
        for p in range(MLA_HEADS // 2):
            d0 = jnp.sum(prod[:, 2 * p * LANES:(2 * p + 1) * LANES], axis=-1, keepdims=True)
            d1 = jnp.sum(prod[:, (2 * p + 1) * LANES:(2 * p + 2) * LANES], axis=-1, keepdims=True)
            lo, _ = _pair_masks((a.shape[0], LANES))
            outs.append(jnp.where(lo, d0, d1))
        return (jnp.concatenate(outs, axis=1),)

    delta = _rw(delta_fn, [do_att, o_att], [], [(MLA_HEADS // 2 * LANES, f32)], name="attn_delta")[0]
    if comm is None:
        dqc, dkc, dv = _attn_bwd(qc, kc, v_att, do_att, lse, delta)
    else:
        dqc, dkc, dv, comm.received["attn"] = _attn_bwd(qc, kc, v_att, do_att, lse, delta,
                                                        exchange=comm.reduce(grads, "attn"))
    dqf = _rw(lambda a, cs, sn: (_rope_heads(a, cs, sn, -1.0, ATT_SCALE),), [dqc, cos, sgn_sin], [],
              [(MLA_HEADS * ATT_QK, bf16)], name="dq_rope")[0]
    dc_q = _mm(dqf, w["mla_w_qcat"], tb=True, name="q_up_dx")
    grads["mla_w_qcat"] = _wgrad(c_q, dqf, "q_up_dw")

    def dqn_fn(c, dy, g):
        return _rms_bwd(c, g, dy)

    dcq0, grads["mla_q_norm"] = _rw(dqn_fn, [cq0, dc_q], [w["mla_q_norm"]], [(MLA_Q_LORA, bf16)], [(1, MLA_Q_LORA)],
                                    name="q_dnorm")
    dxn1 = _mm(dcq0, w["mla_w_dq"], tb=True, name="q_down_dx")
    grads["mla_w_dq"] = _wgrad(xn1, dcq0, "q_down_dw")

    dc_kv = _mm(dkc, w["kv_w_kcat"], tb=True, name="kv_up_dx_k")
    dc_kv = _mm(dv, w["kv_w_uv"], tb=True, name="kv_up_dx_v", epi=_add_epi, extras=(dc_kv,))
    grads["kv_w_kcat"] = _wgrad(c_kv, dkc, "kv_up_dw_k")
    grads["kv_w_uv"] = _wgrad(c_kv, dv, "kv_up_dw_v")

    def dckr_fn(c, dc, dk_heads, cs, sn, g):
        tot = dk_heads[:, LANES:ATT_QK].astype(f32)
        for h in range(1, MLA_HEADS):
            tot = tot + dk_heads[:, ATT_QK * h + LANES:ATT_QK * (h + 1)].astype(f32)
        lo, _ = _pair_masks(tot.shape)
        dkr = jnp.where(lo, _rope(tot, cs, sn, -1.0), 0.0)
        dcc, dg = _rms_bwd(c[:, :MLA_KV_LORA], g, dc)
        return jnp.concatenate([dcc, dkr], axis=1), dg

    dckr, grads["kv_norm"] = _rw(dckr_fn, [ckr, dc_kv, dkc, cos, sgn_sin], [w["kv_norm"]],
                                 [(MLA_KV_LORA + LANES, bf16)], [(1, MLA_KV_LORA)], name="kv_dnorm_rope")
    dhk = _mm(dckr, w["kv_w_dkv"], tb=True, name="kv_down_dx")
    grads["kv_w_dkv"] = _wgrad(hk, dckr, "kv_down_dw")

    def dh2_fn(hh, d1, d2, dres, g1, g2):
        a, ga = _rms_bwd(hh, g1, d1)
        b, gb = _rms_bwd(hh, g2, d2)
        return a + b + dres, a + b + dres, ga, gb

    dh2, dh2_16, grads["kv_in_norm"], grads["mla_norm"] = _rw(
        dh2_fn, [h2, dhk, dxn1, dh3], [w["kv_in_norm"], w["mla_norm"]], [(D_MODEL, f32), (D_MODEL, bf16)],
        [(1, D_MODEL)] * 2, name="kv_mla_dnorm")

    dh1, dh1_16, g_n0, grads["mlp_w_up", 0], grads["mlp_w_down", 0] = _mlp_bwd(
        dh2, dh2_16, h1, w["mlp_norm"][0:1], w["mlp_w_up", 0], w["mlp_w_down", 0], mlp0, 0)
    grads["mlp_norm"] = jnp.concatenate([g_n0, g_n1], axis=0)
    dy = _mm(dh1_16, w["hgrn_w_o"], tb=True, name="hgrn_dout")
    grads["hgrn_w_o"] = _wgrad(y, dh1_16, "hgrn_dwo")

    gla_args = (p4, w["hgrn_lb_logits"], w["hgrn_g_norm"], states, dy)
    if comm is None:
        dp4, grads["hgrn_lb_logits"], grads["hgrn_g_norm"] = _gla_bwd(*gla_args)
    else:
        dp4, grads["hgrn_lb_logits"], grads["hgrn_g_norm"], comm.received["gla"] = _gla_bwd(
            *gla_args, exchange=comm.reduce(grads, "gla"))
    grads["hgrn_w4"] = _mm(xn0, dp4, ta=True, name="hgrn_proj_dw")
    grad_x, grads["hgrn_norm"] = _mm(dp4, w["hgrn_w4"], tb=True, name="hgrn_proj_dx", outs=(f32,), accs=1,
                                     epi=lambda *args: _dnorm_epi(*args)[1:], extras=(x, dh1, w["hgrn_norm"]),
                                     tm=ROWWISE_EPI_TM)
    return loss, grad_x, grads


HBM = pl.BlockSpec(memory_space=pltpu.HBM)


def _me():
    return lax.axis_index("x"), lax.axis_index("y"), lax.axis_index("c")


def _flip(x, y, f):
    return (1 - x if f & 1 else x), (1 - y if f & 2 else y)


def _rcopy(src, dst, sems, k, dev):
    return pltpu.make_async_remote_copy(src_ref=src, dst_ref=dst, send_sem=sems.at[0, k], recv_sem=sems.at[1, k],
                                        device_id=dev, device_id_type=MESH)


def _my_half(rows, c, mine=True):
    half = rows // 2
    return pl.ds(pl.multiple_of((c if mine else 1 - c) * half, 16), half)


def _gather_start(wp_ref, out_ref, sems):
    x, y, c = _me()
    half = _my_half(wp_ref.shape[0], c)
    for f in (1, 2, 3):
        px, py = _flip(x, y, f)
        _rcopy(wp_ref.at[half], out_ref.at[2 * x + y, half], sems, f - 1, (px, py, c)).start()


def _gather_finish(wp_ref, out_ref, sems):
    x, y, c = _me()
    half, other = _my_half(wp_ref.shape[0], c), _my_half(wp_ref.shape[0], c, mine=False)
    sends = []
    for f in (1, 2, 3):
        px, py = _flip(x, y, f)
        landed = out_ref.at[2 * px + py, half]
        _rcopy(landed, landed, sems, f - 1, (px, py, c)).wait_recv()
        sends.append(_rcopy(landed, landed, sems, 2 + f, (x, y, 1 - c)))
        sends[-1].start()
    for f in (1, 2, 3):
        px, py = _flip(x, y, f)
        theirs = out_ref.at[2 * px + py, other]
        _rcopy(theirs, theirs, sems, 2 + f, (x, y, 1 - c)).wait_recv()
        sends.append(_rcopy(wp_ref.at[half], out_ref.at[2 * x + y, half], sems, f - 1, (px, py, c)))
    for cp in sends:
        cp.wait_send()


def _gather_base(wp):
    return jnp.broadcast_to(wp[None], (4,) + wp.shape)


def _all_gather_weights(wp, sv):
    def body(wp_ref, sv_ref, base_ref, out_ref, svs_ref, sems, local_sem):
        x, y, c = _me()
        mine = pltpu.make_async_copy(sv_ref, svs_ref.at[2 * x + y], local_sem)
        mine.start()
        _gather_start(wp_ref, out_ref, sems)
        small = []
        for f in (1, 2, 3):
            px, py = _flip(x, y, f)
            small.append(_rcopy(sv_ref, svs_ref.at[2 * x + y], sems, 5 + f, (px, py, c)))
            small[-1].start()
        _gather_finish(wp_ref, out_ref, sems)
        for f in (1, 2, 3):
            px, py = _flip(x, y, f)
            _rcopy(sv_ref, svs_ref.at[2 * px + py], sems, 5 + f, (px, py, c)).wait_recv()
        for cp in small:
            cp.wait_send()
        mine.wait()

    return _pc(body, name="weights_all_gather", in_specs=[HBM, HBM, HBM], out_specs=[HBM, HBM],
               out_shape=[_sds((4,) + wp.shape, bf16), _sds((4, 8, 256), f32)], aliases={2: 0},
               scratch=[pltpu.SemaphoreType.DMA((2, 9)), pltpu.SemaphoreType.DMA])(wp, sv, _gather_base(wp))


def _send_half_to_sibling(gp, name):
    rows = gp.shape[1]

    def body(gp_ref, out_ref, sems):
        x, y, c = _me()
        cp = _rcopy(gp_ref.at[:, _my_half(rows, c, mine=False)], out_ref, sems, 0, (x, y, 1 - c))
        cp.start()
        cp.wait()

    return _pc(body, name=name, in_specs=[HBM], out_specs=HBM, out_shape=_sds((4, rows // 2, D_MODEL), gp.dtype),
               scratch=[pltpu.SemaphoreType.DMA((2, 1))])(gp)


def _chips_start(sb_ref, out_ref, sems):
    x, y, c = _me()
    for f in (1, 2, 3):
        px, py = _flip(x, y, f)
        _rcopy(sb_ref.at[2 * px + py], out_ref.at[f - 1], sems, f - 1, (px, py, c)).start()


def _chips_finish(sb_ref, out_ref, sems):
    x, y, c = _me()
    for f in (1, 2, 3):
        _rcopy(sb_ref.at[0], out_ref.at[f - 1], sems, f - 1, (x, y, c)).wait_recv()
    for f in (1, 2, 3):
        px, py = _flip(x, y, f)
        _rcopy(sb_ref.at[2 * px + py], out_ref.at[f - 1], sems, f - 1, (px, py, c)).wait_send()


def _exchange_chips(sb, small):
    def body(sb_ref, small_ref, out_ref, smalls_ref, sems, local_sem):
        x, y, c = _me()
        me = 4 * x + 2 * y + c
        mine = pltpu.make_async_copy(small_ref, smalls_ref.at[me], local_sem)
        mine.start()
        _chips_start(sb_ref, out_ref, sems)
        sends = []
        for f in range(1, 8):
            px, py = _flip(x, y, f)
            pc = 1 - c if f & 4 else c
            sends.append(_rcopy(small_ref, smalls_ref.at[me], sems, 2 + f, (px, py, pc)))
            sends[-1].start()
        _chips_finish(sb_ref, out_ref, sems)
        for f in range(1, 8):
            px, py = _flip(x, y, f)
            pc = 1 - c if f & 4 else c
            _rcopy(small_ref, smalls_ref.at[4 * px + 2 * py + pc], sems, 2 + f, (x, y, c)).wait_recv()
        for cp in sends:
            cp.wait_send()
        mine.wait()

    return _pc(body, name="grads_exchange_chips", in_specs=[HBM, HBM], out_specs=[HBM, HBM],
               out_shape=[_sds((3,) + sb.shape[1:], bf16), _sds((8, SMALL_ROWS, D_MODEL), f32)],
               scratch=[pltpu.SemaphoreType.DMA((2, 10)), pltpu.SemaphoreType.DMA])(sb, small)


def _exchange_halves(tot, name):
    rows = tot.shape[0]

    def body(tot_ref, out_ref, sems):
        x, y, c = _me()
        half = _my_half(rows, c)
        cp = _rcopy(tot_ref.at[half], out_ref.at[half], sems, 0, (x, y, 1 - c))
        cp.start()
        cp.wait()

    return _pc(body, name=name, in_specs=[HBM], out_specs=HBM, out_shape=_sds((rows, D_MODEL), f32),
               aliases={0: 0}, scratch=[pltpu.SemaphoreType.DMA((2, 1))])(tot)


def _sum_rows(half):
    return max(r for r in range(16, 513, 16) if half % r == 0)


def _sum_over_cores(gp, recv, cq, name):
    half = recv.shape[1]
    tr = _sum_rows(half)
    nb = half // tr

    def body(cq_ref, g_ref, r_ref, o32_ref, o16_ref):
        s = g_ref[...].astype(f32) + r_ref[...].astype(f32)
        o32_ref[...] = s
        o16_ref[...] = s.astype(bf16)

    spec = pl.BlockSpec((1, tr, D_MODEL), lambda b, i, cq_ref: (b, i, 0))
    gs = pltpu.PrefetchScalarGridSpec(
        num_scalar_prefetch=1, grid=(4, nb),
        in_specs=[pl.BlockSpec((1, tr, D_MODEL), lambda b, i, cq_ref: (b, cq_ref[0] * nb + i, 0)), spec],
        out_specs=[spec, spec])
    return _pc(body, name=name, grid_spec=gs, sem=("arbitrary", "arbitrary"),
               out_shape=[_sds((4, half, D_MODEL), f32), _sds((4, half, D_MODEL), bf16)])(cq, gp, recv)


def _sum_over_chips(s32, recv, cq, name):
    half = recv.shape[1]
    tr = _sum_rows(half)
    nb = half // tr

    def body(cq_ref, own_ref, r_ref, o_ref):
        o_ref[...] = ((own_ref[0] + r_ref[0].astype(f32)) + r_ref[1].astype(f32)) + r_ref[2].astype(f32)

    gs = pltpu.PrefetchScalarGridSpec(
        num_scalar_prefetch=1, grid=(nb,),
        in_specs=[pl.BlockSpec((1, tr, D_MODEL), lambda i, cq_ref: (cq_ref[1], i, 0)),
                  pl.BlockSpec((3, tr, D_MODEL), lambda i, cq_ref: (0, i, 0))],
        out_specs=pl.BlockSpec((tr, D_MODEL), lambda i, cq_ref: (cq_ref[0] * nb + i, 0)))
    return _pc(body, name=name, grid_spec=gs, sem=("arbitrary",),
               out_shape=_sds((2 * half, D_MODEL), f32))(cq, s32, recv)


def _sum_small(smalls):
    def body(s_ref, o_ref):
        tot = s_ref[0]
        for d in range(1, 8):
            tot = tot + s_ref[d]
        o_ref[...] = tot

    return _pc(body, name="small_sum", out_shape=_sds((SMALL_ROWS, D_MODEL), f32))(smalls)


def _adamw_math(w, g, m, v):
    m = ADAM_B1 * m + (1.0 - ADAM_B1) * g
    v = ADAM_B2 * v + (1.0 - ADAM_B2) * jnp.square(g)
    m_hat = m / (1.0 - ADAM_B1 ** ADAM_STEP)
    v_hat = v / (1.0 - ADAM_B2 ** ADAM_STEP)
    delta = -ADAM_LR * (m_hat / (jnp.sqrt(v_hat) + ADAM_EPS) + ADAM_WD * w)
    return delta, m, v


def _adamw(w, g, m, v, name):
    cols = w.shape[1]
    return _rw(_adamw_math, [w, g, m, v], [], [(cols, f32)] * 3, name=name, tr=256)


def _adamw_small(items):
    n = len(items)

    def body(*refs):
        ins, outs = refs[:4 * n], refs[4 * n:]
        for i in range(n):
            res = _adamw_math(*[r[...] for r in ins[4 * i:4 * i + 4]])
            for o, val in zip(outs[3 * i:3 * i + 3], res):
                o[...] = val

    flat = [a for it in items for a in it]
    out_shape = [_sds(it[0].shape, f32) for it in items for _ in range(3)]
    res = _pc(body, name="adamw_small", out_shape=out_shape)(*flat)
    return [tuple(res[3 * i:3 * i + 3]) for i in range(n)]


def _pack_shards(sh, layout, pad):
    parts = [(sh[n] if layer is None else sh[n][layer]).reshape(-1, D_MODEL).astype(bf16) for n, layer, _ in layout]
    if pad:
        parts.append(jnp.zeros((pad, D_MODEL), bf16))
    return jnp.concatenate(parts, axis=0)


def _mlp_full(g4, off, layer):
    o, r = off["mlp_w_up", layer]
    up = g4[:, o:o + r].transpose(1, 0, 2).reshape(D_MODEL, D_FF)
    o, r = off["mlp_w_down", layer]
    return {("mlp_w_up", layer): up, ("mlp_w_down", layer): g4[:, o:o + r].reshape(D_FF, D_MODEL)}


def _unpack_early(g4):
    w = _mlp_full(g4, W_EARLY_OFF, 0)
    hg = g4[:, 0:1024].reshape(4, 4, 256, D_MODEL)
    w["hgrn_w4"] = hg.transpose(0, 2, 1, 3).reshape(D_MODEL, 4 * D_MODEL)
    o, r = W_EARLY_OFF["hgrn_w_o", None]
    w["hgrn_w_o"] = g4[:, o:o + r].reshape(D_MODEL, D_MODEL)
    return w


def _unpack_last(g4):
    return _mlp_full(g4, W_LAST_OFF, 1)


def _unpack_mid(g4):
    def rows(name):
        o, r = W_MID_OFF[name, None]
        return g4[:, o:o + r]

    w = {"mla_w_dq": rows("mla_w_dq").reshape(D_MODEL, MLA_Q_LORA)}
    uq = rows("mla_w_uq").reshape(4, MLA_Q_LORA, 768).transpose(1, 0, 2).reshape(MLA_Q_LORA, MLA_HEADS, MLA_NOPE + MLA_ROPE)
    w["mla_w_qcat"] = jnp.pad(uq, ((0, 0), (0, 0), (0, ATT_QK - MLA_NOPE - MLA_ROPE))).reshape(MLA_Q_LORA, MLA_HEADS * ATT_QK)
    w["mla_w_o"] = rows("mla_w_o").reshape(MLA_HEADS * MLA_V, D_MODEL)
    dkv = rows("kv_w_dkv").reshape(D_MODEL, MLA_KV_LORA + MLA_ROPE)
    w["kv_w_dkv"] = jnp.pad(dkv, ((0, 0), (0, LANES - MLA_ROPE)))
    uk = rows("kv_w_uk").reshape(4, MLA_KV_LORA, 512).transpose(1, 0, 2).reshape(MLA_KV_LORA, MLA_HEADS, MLA_NOPE)
    w["kv_w_kcat"] = jnp.pad(uk, ((0, 0), (0, 0), (0, ATT_QK - MLA_NOPE))).reshape(MLA_KV_LORA, MLA_HEADS * ATT_QK)
    w["kv_w_uv"] = rows("kv_w_uv").reshape(4, MLA_KV_LORA, 512).transpose(1, 0, 2).reshape(MLA_KV_LORA, MLA_HEADS * MLA_V)
    return w


def _pack_grads_late(g):
    return g["hgrn_w4"].reshape(4, 256, 4, D_MODEL).transpose(0, 2, 1, 3).reshape(4, G_LATE_ROWS, D_MODEL)


def _grad_rows(g, name, layer):
    if name in ("mlp_w_up", "mlp_w_down"):
        full = g[name, layer]
        return full.reshape(D_MODEL, 4, 1024).transpose(1, 0, 2) if name == "mlp_w_up" else full.reshape(4, 1024, D_MODEL)
    if name == "mla_w_uq":
        uq = g["mla_w_qcat"].reshape(MLA_Q_LORA, MLA_HEADS, ATT_QK)[:, :, :MLA_NOPE + MLA_ROPE]
        return uq.reshape(MLA_Q_LORA, 4, 768).transpose(1, 0, 2).reshape(4, 192, D_MODEL)
    if name == "kv_w_uk":
        uk = g["kv_w_kcat"].reshape(MLA_KV_LORA, MLA_HEADS, ATT_QK)[:, :, :MLA_NOPE]
        return uk.reshape(MLA_KV_LORA, 4, 512).transpose(1, 0, 2).reshape(4, 128, D_MODEL)
    if name == "kv_w_uv":
        return g[name].reshape(MLA_KV_LORA, 4, 512).transpose(1, 0, 2).reshape(4, 128, D_MODEL)
    if name == "kv_w_dkv":
        return g[name][:, :MLA_KV_LORA + MLA_ROPE].reshape(4, 80, D_MODEL)
    return g[name].reshape(4, -1, D_MODEL)


def _pack_grads(g, layout):
    parts = [_grad_rows(g, name, layer) for name, layer, _ in layout]
    if layout in PADDED:
        parts.append(jnp.zeros((4, PACK_PAD, D_MODEL), bf16))
    return jnp.concatenate(parts, axis=1)


LOSS_ROW = 11


def _pack_small(g, loss):
    rows = []
    for name, _, r, wd in SMALL:
        a = g[name].reshape(r, wd)
        rows.append(jnp.pad(a, ((0, 0), (0, D_MODEL - wd))) if wd < D_MODEL else a)
    assert sum(r for _, _, r, _ in SMALL) == LOSS_ROW
    rows.append(jnp.full((1, D_MODEL), loss, f32))
    rows.append(jnp.zeros((SMALL_ROWS - LOSS_ROW - 1, D_MODEL), f32))
    return jnp.concatenate(rows, axis=0)


def kernel(x, hgrn_norm, hgrn_w_q, hgrn_w_f, hgrn_w_i, hgrn_w_g, hgrn_g_norm, hgrn_w_o, hgrn_lb_logits, mla_norm, mla_w_dq, mla_q_norm, mla_w_uq, mla_w_o, kv_in_norm, kv_w_dkv, kv_norm, kv_w_uk, kv_w_uv, mlp_norm, mlp_w_up, mlp_w_down, final_norm, loss_target, m_hgrn_norm, m_hgrn_w_q, m_hgrn_w_f, m_hgrn_w_i, m_hgrn_w_g, m_hgrn_g_norm, m_hgrn_w_o, m_hgrn_lb_logits, m_mla_norm, m_mla_w_dq, m_mla_q_norm, m_mla_w_uq, m_mla_w_o, m_kv_in_norm, m_kv_w_dkv, m_kv_norm, m_kv_w_uk, m_kv_w_uv, m_mlp_norm, m_mlp_w_up, m_mlp_w_down, m_final_norm, v_hgrn_norm, v_hgrn_w_q, v_hgrn_w_f, v_hgrn_w_i, v_hgrn_w_g, v_hgrn_g_norm, v_hgrn_w_o, v_hgrn_lb_logits, v_mla_norm, v_mla_w_dq, v_mla_q_norm, v_mla_w_uq, v_mla_w_o, v_kv_in_norm, v_kv_w_dkv, v_kv_norm, v_kv_w_uk, v_kv_w_uv, v_mlp_norm, v_mlp_w_up, v_mlp_w_down, v_final_norm):
    given = dict(locals())
    wsh = {n: given[n] for n in WEIGHTS}
    msh = {n: given["m_" + n] for n in WEIGHTS}
    vsh = {n: given["v_" + n] for n in WEIGHTS}
    xi, yi, ci = _me()
    chip = 2 * xi + yi
    cq = jnp.stack([ci, chip]).astype(jnp.int32)

    small_w = {n: wsh[n].reshape(r, -1) for n, _, r, _ in SMALL}
    sv = jnp.concatenate([small_w["hgrn_norm"], small_w["hgrn_lb_logits"], jnp.zeros((5, 256), f32)], axis=0)
    g4, sv4 = _all_gather_weights(_pack_shards(wsh, W_EARLY, 0), sv)
    w = _unpack_early(g4)
    w["hgrn_norm"] = sv4[:, 0, :].reshape(1, D_MODEL)
    w["hgrn_lb_logits"] = sv4[:, 1:3, :].transpose(1, 0, 2).reshape(2, D_MODEL)
    for n in ("hgrn_g_norm", "mla_norm", "mla_q_norm", "kv_in_norm", "kv_norm", "mlp_norm", "final_norm"):
        w[n] = small_w[n]

    class Comm:
        shard = {"gla": _pack_shards(wsh, W_MID, PACK_PAD), "attn": _pack_shards(wsh, W_LAST, 0)}
        unpack = {"gla": _unpack_mid, "attn": _unpack_last}
        layout = {"gla": G_GLA, "attn": G_ATTN}
        received, s32 = {}, {}

        @staticmethod
        def reduce(grads, part):
            gp = _pack_grads(grads, Comm.layout[part])
            Comm.s32[part], s16 = _sum_over_cores(gp, _send_half_to_sibling(gp, "grads_to_sibling_" + part), cq,
                                                  "grads_sum_cores_" + part)
            return s16

    loss, grad_x, g = _local_step(x.reshape(-1, D_MODEL), loss_target.reshape(-1, D_MODEL), w, Comm)

    total = {part: _exchange_halves(_sum_over_chips(Comm.s32[part], Comm.received[part], cq, "grads_sum_chips_" + part),
                                    "grads_exchange_halves_" + part) for part in ("attn", "gla")}
    gp = _pack_grads_late(g)
    s32, s16 = _sum_over_cores(gp, _send_half_to_sibling(gp, "grads_to_sibling_late"), cq, "grads_sum_cores_late")
    from_chips, smalls = _exchange_chips(s16, _pack_small(g, loss))
    total["late"] = _exchange_halves(_sum_over_chips(s32, from_chips, cq, "grads_sum_chips_late"),
                                     "grads_exchange_halves_late")
    small_tot = _sum_small(smalls)
    loss = small_tot[LOSS_ROW, 0]

    where = {}
    for part, offsets in (("late", G_LATE_OFF), ("gla", G_GLA_OFF), ("attn", G_ATTN_OFF)):
        for (n, layer), (o, r) in offsets.items():
            where.setdefault(n, []).append(total[part][o:o + r])
    grad, delta, new_m, new_v = {}, {}, {}, {}
    for n, pieces in where.items():
        shp = wsh[n].shape
        two_d = (-1, shp[-1])
        grad[n] = (pieces[0] if len(pieces) == 1 else jnp.concatenate(pieces, axis=0)).reshape(shp)
        d, m2, v2 = _adamw(wsh[n].reshape(two_d), grad[n].reshape(two_d), msh[n].reshape(two_d), vsh[n].reshape(two_d),
                           "adamw_" + n)
        delta[n], new_m[n], new_v[n] = d.reshape(shp), m2.reshape(shp), v2.reshape(shp)
    items = []
    for n, row, r, wd in SMALL:
        gs = small_tot[row:row + r, :wd]
        if n in ("hgrn_norm", "hgrn_lb_logits"):
            gs = lax.dynamic_slice(gs, (0, 256 * chip), (r, 256))
        grad[n] = gs.reshape(wsh[n].shape)
        items.append((small_w[n], gs, msh[n].reshape(gs.shape), vsh[n].reshape(gs.shape)))
    for (n, _, _, _), (d, m2, v2) in zip(SMALL, _adamw_small(items)):
        shp = wsh[n].shape
        delta[n], new_m[n], new_v[n] = d.reshape(shp), m2.reshape(shp), v2.reshape(shp)

    return (loss, grad_x.reshape(x.shape), *[grad[n] for n in WEIGHTS], *[delta[n] for n in WEIGHTS],
            *[new_m[n] for n in WEIGHTS], *[new_v[n] for n in WEIGHTS])
```

```python
import functools

import jax
import jax.numpy as jnp
from jax import lax
from jax.experimental import pallas as pl
from jax.experimental.pallas import tpu as pltpu

f32, bf16 = jnp.float32, jnp.bfloat16
HI = lax.Precision.HIGHEST
MESH = pl.DeviceIdType.MESH

D_MODEL = 1024
D_FF = 4096
EPS = 1e-6
HGRN_HEADS, HGRN_DK, HGRN_CHUNK, HGRN_SUB = 8, 128, 64, 16
MLA_HEADS, MLA_NOPE, MLA_ROPE, MLA_V = 16, 128, 64, 128
MLA_Q_LORA, MLA_KV_LORA = 256, 256
ROPE_THETA = 10000.0
ATT_SCALE = (MLA_NOPE + MLA_ROPE) ** -0.5
EXP_CLAMP = 80.0

ADAM_LR, ADAM_B1, ADAM_B2, ADAM_EPS, ADAM_WD, ADAM_STEP = 0.001, 0.9, 0.999, 1e-08, 0.01, 10

V7X_VMEM_BYTES = 64 * 1024 * 1024
VMEM_LIMIT = V7X_VMEM_BYTES - 8 * 1024 * 1024
LANES = 128

PACK_PAD = 16
W_EARLY = (("hgrn_w_q", None, 256), ("hgrn_w_f", None, 256), ("hgrn_w_i", None, 256), ("hgrn_w_g", None, 256),
           ("hgrn_w_o", None, 256), ("mlp_w_up", 0, 1024), ("mlp_w_down", 0, 1024))
W_MID = (("mla_w_dq", None, 64), ("mla_w_uq", None, 192), ("mla_w_o", None, 512), ("kv_w_dkv", None, 80),
         ("kv_w_uk", None, 128), ("kv_w_uv", None, 128))
W_LAST = (("mlp_w_up", 1, 1024), ("mlp_w_down", 1, 1024))
G_LATE = (("hgrn_w_q", None, 256), ("hgrn_w_f", None, 256), ("hgrn_w_i", None, 256), ("hgrn_w_g", None, 256))
G_ATTN = (("mla_w_o", None, 512), ("mlp_w_up", 1, 1024), ("mlp_w_down", 1, 1024))
G_GLA = (("hgrn_w_o", None, 256), ("mla_w_dq", None, 64), ("mla_w_uq", None, 192), ("kv_w_dkv", None, 80),
         ("kv_w_uk", None, 128), ("kv_w_uv", None, 128), ("mlp_w_up", 0, 1024), ("mlp_w_down", 0, 1024))
PADDED = (W_MID, G_GLA)


def _offsets(layout):
    out, o = {}, 0
    for name, layer, rows in layout:
        out[name, layer] = (o, rows)
        o += rows
    return out, o + (PACK_PAD if layout in PADDED else 0)


W_EARLY_OFF, W_EARLY_ROWS = _offsets(W_EARLY)
W_MID_OFF, W_MID_ROWS = _offsets(W_MID)
W_LAST_OFF, W_LAST_ROWS = _offsets(W_LAST)
G_LATE_OFF, G_LATE_ROWS = _offsets(G_LATE)
G_ATTN_OFF, G_ATTN_ROWS = _offsets(G_ATTN)
G_GLA_OFF, G_GLA_ROWS = _offsets(G_GLA)
assert all(r % 32 == 0 for r in (W_EARLY_ROWS, W_MID_ROWS, W_LAST_ROWS, G_LATE_ROWS, G_ATTN_ROWS, G_GLA_ROWS))

WEIGHTS = ("hgrn_norm", "hgrn_w_q", "hgrn_w_f", "hgrn_w_i", "hgrn_w_g", "hgrn_g_norm", "hgrn_w_o", "hgrn_lb_logits",
           "mla_norm", "mla_w_dq", "mla_q_norm", "mla_w_uq", "mla_w_o", "kv_in_norm", "kv_w_dkv", "kv_norm", "kv_w_uk",
           "kv_w_uv", "mlp_norm", "mlp_w_up", "mlp_w_down", "final_norm")
SMALL = (("hgrn_norm", 0, 1, 1024), ("hgrn_lb_logits", 1, 2, 1024), ("hgrn_g_norm", 3, 1, 128),
         ("mla_norm", 4, 1, 1024), ("mla_q_norm", 5, 1, 256), ("kv_in_norm", 6, 1, 1024), ("kv_norm", 7, 1, 256),
         ("mlp_norm", 8, 2, 1024), ("final_norm", 10, 1, 1024))
SMALL_ROWS = 16


def _pc(body, *, name, out_shape, grid=None, in_specs=None, out_specs=None, scratch=(), sem=None, grid_spec=None,
        aliases=None):
    params = pltpu.CompilerParams(dimension_semantics=sem, vmem_limit_bytes=VMEM_LIMIT)
    if grid_spec is not None:
        return pl.pallas_call(body, name=name, out_shape=out_shape, grid_spec=grid_spec, compiler_params=params,
                              interpret=False)
    kw = {k: v for k, v in (("grid", grid), ("in_specs", in_specs), ("out_specs", out_specs),
                            ("input_output_aliases", aliases)) if v is not None}
    return pl.pallas_call(body, name=name, out_shape=out_shape, scratch_shapes=list(scratch), compiler_params=params,
                          interpret=False, **kw)


def _sds(shape, dtype):
    return jax.ShapeDtypeStruct(tuple(shape), dtype)


def _mm(a, b, *, name, ta=False, tb=False, outs=(f32,), epi=None, extras=(), accs=0, tm=1024, tn=1024, tk=4096):
    m, k = (a.shape[1], a.shape[0]) if ta else a.shape
    n = b.shape[0] if tb else b.shape[1]
    tm, tn, tk = min(tm, m), min(tn, n), min(tk, k)
    assert m % tm == 0 and n % tn == 0 and k % tk == 0, (name, m, n, k)
    nk = k // tk
    assert accs == 0 or (tn == n and nk == 1), name
    a_spec = pl.BlockSpec((tk, tm), lambda i, j, kk: (kk, i)) if ta else pl.BlockSpec((tm, tk), lambda i, j, kk: (i, kk))
    b_spec = pl.BlockSpec((tn, tk), lambda i, j, kk: (j, kk)) if tb else pl.BlockSpec((tk, tn), lambda i, j, kk: (kk, j))

    def extra_spec(e):
        if e.shape == (m, n):
            return pl.BlockSpec((tm, tn), lambda i, j, kk: (i, j))
        if e.shape[0] == m:
            return pl.BlockSpec((tm, e.shape[1]), lambda i, j, kk: (i, 0))
        return pl.BlockSpec((e.shape[0], tn), lambda i, j, kk: (0, j))

    e_specs = [extra_spec(e) for e in extras]
    dn = (((0 if ta else 1,), (1 if tb else 0,)), ((), ()))
    n_e, n_o = len(extras), len(outs)

    def finish(r, e_refs, o_refs):
        res = epi(r, *[e[...] for e in e_refs]) if epi is not None else (r,)
        for o, v in zip(o_refs[:n_o], res[:n_o]):
            o[...] = v.astype(o.dtype)
        for o, v in zip(o_refs[n_o:], res[n_o:]):
            @pl.when(pl.program_id(0) == 0)
            def _(o=o):
                o[...] = jnp.zeros_like(o)
            o[...] += v

    def body(*refs):
        a_ref, b_ref = refs[0], refs[1]
        e_refs = refs[2:2 + n_e]
        o_refs = refs[2 + n_e:2 + n_e + n_o + accs]
        prod = lax.dot_general(a_ref[...].astype(bf16), b_ref[...].astype(bf16), dn, preferred_element_type=f32)
        if nk == 1:
            finish(prod, e_refs, o_refs)
            return
        acc = refs[-1]
        kk = pl.program_id(2)

        @pl.when(kk == 0)
        def _():
            acc[...] = jnp.zeros_like(acc)

        acc[...] += prod

        @pl.when(kk == nk - 1)
        def _():
            finish(acc[...], e_refs, o_refs)

    out = _pc(body, name=name, grid=(m // tm, n // tn, nk),
              in_specs=[a_spec, b_spec] + e_specs,
              out_specs=[pl.BlockSpec((tm, tn), lambda i, j, kk: (i, j)) for _ in outs] +
                        [pl.BlockSpec((1, n), lambda i, j, kk: (0, 0))] * accs,
              out_shape=[_sds((m, n), dt) for dt in outs] + [_sds((1, n), f32)] * accs,
              scratch=[pltpu.VMEM((tm, tn), f32)] if nk > 1 else [],
              sem=("arbitrary" if accs else "parallel", "parallel", "arbitrary"))(a, b, *extras)
    return out[0] if n_o + accs == 1 else out


def _wgrad(a, b, name):
    return _mm(a, b, ta=True, name=name, outs=(bf16,))


def _rw(fn, rows, bcast, outs, accs=(), *, name, tr=256):
    t = rows[0].shape[0]
    tr = min(tr, t)
    assert t % tr == 0
    n_r, n_b, n_o, n_a = len(rows), len(bcast), len(outs), len(accs)

    def body(*refs):
        r_refs = refs[:n_r]
        b_refs = refs[n_r:n_r + n_b]
        o_refs = refs[n_r + n_b:n_r + n_b + n_o]
        a_refs = refs[n_r + n_b + n_o:]
        res = fn(*[r[...] for r in r_refs], *[b[...] for b in b_refs])
        for o, v in zip(o_refs, res[:n_o]):
            o[...] = v.astype(o.dtype)
        i = pl.program_id(0)
        for a_ref, v in zip(a_refs, res[n_o:]):
            @pl.when(i == 0)
            def _(a_ref=a_ref):
                a_ref[...] = jnp.zeros_like(a_ref)
            a_ref[...] += v

    in_specs = [pl.BlockSpec((tr, r.shape[1]), lambda i: (i, 0)) for r in rows]
    in_specs += [pl.BlockSpec(b.shape, lambda i: (0, 0)) for b in bcast]
    out_specs = [pl.BlockSpec((tr, w), lambda i: (i, 0)) for w, _ in outs]
    out_specs += [pl.BlockSpec(s, lambda i: (0, 0)) for s in accs]
    out_shape = [_sds((t, w), dt) for w, dt in outs] + [_sds(s, f32) for s in accs]
    res = _pc(body, name=name, grid=(t // tr,), in_specs=in_specs, out_specs=out_specs, out_shape=out_shape,
              sem=("arbitrary",))(*rows, *bcast)
    return res


def _rms(x, gain):
    return x * lax.rsqrt(jnp.mean(x * x, axis=-1, keepdims=True) + EPS) * gain


def _rms_bwd(x, gain, dy):
    _, vjp = jax.vjp(_rms, x, gain)
    return vjp(dy)


def _lower_bound(lbl):
    l0, l1 = lbl[0:1, :], lbl[1:2, :]
    mx = jnp.maximum(l0, l1)
    e0, e1 = jnp.exp(l0 - mx), jnp.exp(l1 - mx)
    return e0 / (e0 + e1)


def _gates(qpre, fpre, lbl):
    lb = _lower_bound(lbl)
    q = jax.nn.silu(qpre)
    forget = lb + (1.0 - lb) * jax.nn.sigmoid(fpre)
    return q, 1.0 - forget, jnp.log(forget)


def _head_norm_gate(o, gpre, gn):
    return _rms(o, gn) * jax.nn.silu(gpre)


def _swap_halves(x):
    w = x.shape[1]
    lane = lax.broadcasted_iota(jnp.int32, x.shape, 1)
    return jnp.where((lane % MLA_ROPE) < MLA_ROPE // 2, pltpu.roll(x, w - MLA_ROPE // 2, 1),
                     pltpu.roll(x, MLA_ROPE // 2, 1))


def _tile_lanes(tab, w):
    return tab if w == tab.shape[1] else jnp.concatenate([tab] * (w // tab.shape[1]), axis=1)


def _rope(x, cos, sgn_sin, sign=1.0):
    w = x.shape[1]
    return x * _tile_lanes(cos, w) + sign * _swap_halves(x) * _tile_lanes(sgn_sin, w)


def _rope_heads(x, cos, sgn_sin, sign, scale):
    parts = []
    for h in range(x.shape[1] // (2 * LANES)):
        parts.append(x[:, 2 * LANES * h:2 * LANES * h + LANES] * scale)
        parts.append(_rope(x[:, 2 * LANES * h + LANES:2 * LANES * (h + 1)], cos, sgn_sin, sign) * scale)
    return jnp.concatenate(parts, axis=1)


def _bd(a, b, ca, cb):
    return lax.dot_general(a.astype(bf16), b.astype(bf16), (((ca,), (cb,)), ((), ())), preferred_element_type=f32)


@jax.custom_vjp
def _dot_nn(a, b):
    return _bd(a, b, 1, 0)


@jax.custom_vjp
def _dot_nt(a, b):
    return _bd(a, b, 1, 1)


@jax.custom_vjp
def _dot_tn(a, b):
    return _bd(a, b, 0, 0)


_dot_nn.defvjp(lambda a, b: (_bd(a, b, 1, 0), (a, b)), lambda r, g: (_bd(g, r[1], 1, 1), _bd(r[0], g, 0, 0)))
_dot_nt.defvjp(lambda a, b: (_bd(a, b, 1, 1), (a, b)), lambda r, g: (_bd(g, r[1], 1, 0), _bd(g, r[0], 0, 0)))
_dot_tn.defvjp(lambda a, b: (_bd(a, b, 0, 0), (a, b)), lambda r, g: (_bd(r[1], g, 1, 1), _bd(r[0], g, 1, 0)))


def _scan_rows(x, reverse):
    n = x.shape[0]
    row = lax.broadcasted_iota(jnp.int32, x.shape, 0)
    s = 1
    while s < n:
        if reverse:
            x = x + jnp.where(row < n - s, pltpu.roll(x, n - s, 0), 0.0)
        else:
            x = x + jnp.where(row >= s, pltpu.roll(x, s, 0), 0.0)
        s *= 2
    return x


@jax.custom_vjp
def _cumsum_rows(g):
    return _scan_rows(g, False)


_cumsum_rows.defvjp(lambda g: (_scan_rows(g, False), None), lambda _, ct: (_scan_rows(ct, True),))

HGRN_PAIRS = HGRN_HEADS // 2
HGRN_PAIR = 2 * HGRN_DK
GLA_STATE = (HGRN_PAIRS, HGRN_PAIR, HGRN_PAIR)


def _gla_consts():
    s = HGRN_SUB
    r = lax.broadcasted_iota(jnp.int32, (HGRN_PAIR, HGRN_PAIR), 0)
    c = lax.broadcasted_iota(jnp.int32, (HGRN_PAIR, HGRN_PAIR), 1)
    pair_mask = (r < HGRN_DK) == (c < HGRN_DK)
    masks = []
    for i in range(HGRN_CHUNK // s):
        n = s * (i + 1)
        row = lax.broadcasted_iota(jnp.int32, (HGRN_HEADS * s, HGRN_HEADS * n), 0)
        col = lax.broadcasted_iota(jnp.int32, (HGRN_HEADS * s, HGRN_HEADS * n), 1)
        col_head = sum((col >= m * n).astype(jnp.int32) for m in range(1, HGRN_HEADS))
        masks.append((col_head == row // s) & (col - col_head * n <= s * i + row % s))
    return pair_mask, masks


def _heads_to_rows(x):
    return jnp.concatenate([x[:, HGRN_DK * h:HGRN_DK * (h + 1)] for h in range(HGRN_HEADS)], axis=0)


def _gla_chunk(consts, dots, q, k, v, g, st):
    pair_mask, masks = consts
    dot_nn, dot_nt, dot_tn = dots
    c, s = HGRN_CHUNK, HGRN_SUB
    b = _cumsum_rows(g)
    b_last = b[c - 1:c, :]
    q_in, k_out = q * jnp.exp(b), k * jnp.exp(b_last - b)
    o_inter, st_new = [], []
    for p in range(HGRN_PAIRS):
        cols = slice(HGRN_PAIR * p, HGRN_PAIR * (p + 1))
        o_inter.append(dot_nt(q_in[:, cols], st[p]))
        st_new.append(st[p] * jnp.exp(b_last[:, cols]) + jnp.where(pair_mask, dot_tn(v[:, cols], k_out[:, cols]), 0.0))
    intra = []
    for i in range(c // s):
        n = s * (i + 1)
        ref = b[s * i - 1:s * i, :] if i else jnp.zeros_like(b_last)
        qt = _heads_to_rows(q[s * i:n] * jnp.exp(b[s * i:n] - ref))
        kt = _heads_to_rows(k[:n] * jnp.exp(jnp.minimum(ref - b[:n], EXP_CLAMP)))
        sc = jnp.where(masks[i], dot_nt(qt, kt), 0.0)
        oi = dot_nn(sc, _heads_to_rows(v[:n]))
        intra.append(jnp.concatenate([oi[s * h:s * (h + 1)] for h in range(HGRN_HEADS)], axis=1))
    return jnp.concatenate(o_inter, axis=1) + jnp.concatenate(intra, axis=0), st_new


_PLAIN_DOTS = (lambda a, b: _bd(a, b, 1, 0), lambda a, b: _bd(a, b, 1, 1), lambda a, b: _bd(a, b, 0, 0))
_VJP_DOTS = (_dot_nn, _dot_nt, _dot_tn)


def _hgrn_mix(consts, dots, qpre, fpre, v, gpre, lbl, gn, st):
    q, k, g = _gates(qpre, fpre, lbl)
    o, st_new = _gla_chunk(consts, dots, q, k, v, g, st)
    y = [_head_norm_gate(o[:, HGRN_DK * h:HGRN_DK * (h + 1)], gpre[:, HGRN_DK * h:HGRN_DK * (h + 1)], gn)
         for h in range(HGRN_HEADS)]
    return jnp.concatenate(y, axis=1), st_new


def _gla_fwd(p4, lbl, gn, gather=None):
    t = p4.shape[0]
    nc = t // HGRN_CHUNK

    def body(q_ref, k_ref, v_ref, g_ref, lbl_ref, gn_ref, *rest):
        if gather is None:
            o_ref, s_ref, st = rest
        else:
            wp_ref, _, o_ref, s_ref, gathered_ref, st, sems = rest

        @pl.when(pl.program_id(0) == 0)
        def _():
            st[...] = jnp.zeros_like(st)
            if gather is not None:
                _gather_start(wp_ref, gathered_ref, sems)

        if gather is not None:
            @pl.when(pl.program_id(0) == nc - 1)
            def _():
                _gather_finish(wp_ref, gathered_ref, sems)

        s_in = [st[p] for p in range(HGRN_PAIRS)]
        y, st_new = _hgrn_mix(_gla_consts(), _PLAIN_DOTS, q_ref[...], k_ref[...], v_ref[...], g_ref[...],
                              lbl_ref[...], gn_ref[...], s_in)
        o_ref[...] = y.astype(o_ref.dtype)
        for p in range(HGRN_PAIRS):
            s_ref[0, p] = s_in[p]
            st[p] = st_new[p]

    blk = lambda off: pl.BlockSpec((HGRN_CHUNK, D_MODEL), lambda c: (c, off))
    whole = lambda a: pl.BlockSpec(a.shape, lambda c: (0, 0))
    state_shape = GLA_STATE
    in_specs = [blk(0), blk(1), blk(2), blk(3), whole(lbl), whole(gn)]
    out_specs = [blk(0), pl.BlockSpec((1,) + state_shape, lambda c: (c, 0, 0, 0))]
    out_shape = [_sds((t, D_MODEL), bf16), _sds((nc,) + state_shape, f32)]
    scratch = [pltpu.VMEM(state_shape, f32)]
    if gather is None:
        return _pc(body, name="gla_fwd", grid=(nc,), in_specs=in_specs, out_specs=out_specs, out_shape=out_shape,
                   scratch=scratch, sem=("arbitrary",))(p4, p4, p4, p4, lbl, gn)
    return _pc(body, name="gla_fwd_gather", grid=(nc,), in_specs=in_specs + [HBM, HBM], out_specs=out_specs + [HBM],
               out_shape=out_shape + [_sds((4,) + gather.shape, bf16)], aliases={7: 2},
               scratch=scratch + [pltpu.SemaphoreType.DMA((2, 6))], sem=("arbitrary",))(
                   p4, p4, p4, p4, lbl, gn, gather, _gather_base(gather))


def _gla_bwd(p4, lbl, gn, states, dy, exchange=None):
    t = p4.shape[0]
    nc = t // HGRN_CHUNK

    def body(q_ref, k_ref, v_ref, g_ref, lbl_ref, gn_ref, s_ref, dy_ref, *rest):
        if exchange is None:
            dp_ref, dlbl_ref, dgn_ref, dst = rest
        else:
            sb_ref, dp_ref, dlbl_ref, dgn_ref, recv_ref, dst, sems = rest

        @pl.when(pl.program_id(0) == 0)
        def _():
            dst[...] = jnp.zeros_like(dst)
            dlbl_ref[...] = jnp.zeros_like(dlbl_ref)
            dgn_ref[...] = jnp.zeros_like(dgn_ref)
            if exchange is not None:
                _chips_start(sb_ref, recv_ref, sems)

        if exchange is not None:
            @pl.when(pl.program_id(0) == nc - 1)
            def _():
                _chips_finish(sb_ref, recv_ref, sems)

        consts = _gla_consts()
        fn = lambda *args: _hgrn_mix(consts, _VJP_DOTS, *args)
        pairs = range(HGRN_PAIRS)
        _, vjp = jax.vjp(fn, q_ref[...], k_ref[...], v_ref[...], g_ref[...], lbl_ref[...], gn_ref[...],
                         [s_ref[0, p] for p in pairs])
        *d_proj, dlbl, dgn, ds = vjp((dy_ref[...], [dst[p] for p in pairs]))
        for i, d in enumerate(d_proj):
            dp_ref[:, D_MODEL * i:D_MODEL * (i + 1)] = d.astype(dp_ref.dtype)
        dlbl_ref[...] += dlbl
        dgn_ref[...] += dgn
        for p in pairs:
            dst[p] = ds[p]

    blk = lambda off: pl.BlockSpec((HGRN_CHUNK, D_MODEL), lambda c: (nc - 1 - c, off))
    whole = lambda a: pl.BlockSpec(a.shape, lambda c: (0, 0))
    state_shape = GLA_STATE
    in_specs = [blk(0), blk(1), blk(2), blk(3), whole(lbl), whole(gn),
                pl.BlockSpec((1,) + state_shape, lambda c: (nc - 1 - c, 0, 0, 0)), blk(0)]
    out_specs = [pl.BlockSpec((HGRN_CHUNK, 4 * D_MODEL), lambda c: (nc - 1 - c, 0)), whole(lbl), whole(gn)]
    out_shape = [_sds((t, 4 * D_MODEL), bf16), _sds(lbl.shape, f32), _sds(gn.shape, f32)]
    scratch = [pltpu.VMEM(state_shape, f32)]
    if exchange is None:
        return _pc(body, name="gla_bwd", grid=(nc,), in_specs=in_specs, out_specs=out_specs, out_shape=out_shape,
                   scratch=scratch, sem=("arbitrary",))(p4, p4, p4, p4, lbl, gn, states, dy)
    return _pc(body, name="gla_bwd_exchange", grid=(nc,), in_specs=in_specs + [HBM], out_specs=out_specs + [HBM],
               out_shape=out_shape + [_sds((3,) + exchange.shape[1:], bf16)],
               scratch=scratch + [pltpu.SemaphoreType.DMA((2, 3))], sem=("arbitrary",))(
                   p4, p4, p4, p4, lbl, gn, states, dy, exchange)


ATT_FWD_TQ, ATT_FWD_TK = 1024, 1024
ATT_BWD_TQ, ATT_BWD_TK = 1024, 512
ATT_QK = 2 * LANES
NEG = -1e30


def _pair_masks(shape):
    lane = lax.broadcasted_iota(jnp.int32, shape, 1)
    return lane < MLA_ROPE, lane >= MLA_ROPE


def _causal(shape, row0, col0):
    row = row0 + lax.broadcasted_iota(jnp.int32, shape, 0)
    col = col0 + lax.broadcasted_iota(jnp.int32, shape, 1)
    return col <= row


def _qk_cols(e):
    return slice(ATT_QK * e, ATT_QK * (e + 1))


def _v_cols(e):
    return slice(MLA_V * e, MLA_V * (e + 1))


def _first_last_step(n0, n1):
    p, i = pl.program_id(0), pl.program_id(1)
    return (p == 0) & (i == 0), (p == n0 - 1) & (i == n1 - 1)


def _attn_fwd(qc, kc, v, gather=None):
    t = qc.shape[0]
    tq, tk = min(ATT_FWD_TQ, t), min(ATT_FWD_TK, t)
    assert tq == tk, "the diagonal block is split in the body on the premise of square blocks"
    npair = MLA_HEADS // 2

    def body(q_ref, k_ref, v_ref, *rest):
        if gather is None:
            o_ref, lse_ref = rest
        else:
            wp_ref, _, o_ref, lse_ref, gathered_ref, sems = rest
            first, last = _first_last_step(npair, t // tq)
            pl.when(first)(lambda: _gather_start(wp_ref, gathered_ref, sems))
            pl.when(last)(lambda: _gather_finish(wp_ref, gathered_ref, sems))
        i = pl.program_id(1)
        q = [q_ref[:, _qk_cols(e)] for e in range(2)]

        def update(state, q_rows, e, ks, ok):
            m, l, acc = state
            s = _bd(q_rows, k_ref[ks, _qk_cols(e)], 1, 1)
            if ok is not None:
                s = jnp.where(ok, s, NEG)
            m_new = jnp.maximum(m, jnp.max(s, axis=-1, keepdims=True))
            p = jnp.exp(s - m_new)
            alpha = jnp.exp(m - m_new)
            return m_new, alpha * l + jnp.sum(p, axis=-1, keepdims=True), alpha * acc + _bd(p, v_ref[ks, _v_cols(e)], 1, 0)

        def step(j, carry):
            ks = pl.ds(pl.multiple_of(j * tk, tk), tk)
            return tuple(update(carry[e], q[e], e, ks, None) for e in range(2))

        one = (jnp.full((tq, 1), NEG, f32), jnp.zeros((tq, 1), f32), jnp.zeros((tq, MLA_V), f32))
        carry = lax.fori_loop(0, i, step, (one, one))
        half = tq // 2
        outs, lses = [], []
        for e in range(2):
            top = update(tuple(a[:half] for a in carry[e]), q[e][:half], e,
                         pl.ds(pl.multiple_of(i * tk, tk), half), _causal((half, half), 0, 0))
            bottom = update(tuple(a[half:] for a in carry[e]), q[e][half:], e,
                            pl.ds(pl.multiple_of(i * tk, tk), tk), _causal((half, tk), half, 0))
            m, l, acc = (jnp.concatenate(ab, axis=0) for ab in zip(top, bottom))
            outs.append(acc / l)
            lses.append(m + jnp.log(l))
        o_ref[...] = jnp.concatenate(outs, axis=1).astype(o_ref.dtype)
        lo, _ = _pair_masks((tq, LANES))
        lse_ref[...] = jnp.where(lo, *lses)

    in_specs = [pl.BlockSpec((tq, 2 * ATT_QK), lambda p, i: (i, p)),
                pl.BlockSpec((t, 2 * ATT_QK), lambda p, i: (0, p)),
                pl.BlockSpec((t, 2 * MLA_V), lambda p, i: (0, p))]
    out_specs = [pl.BlockSpec((tq, 2 * MLA_V), lambda p, i: (i, p)), pl.BlockSpec((tq, LANES), lambda p, i: (i, p))]
    out_shape = [_sds((t, MLA_HEADS * MLA_V), bf16), _sds((t, npair * LANES), f32)]
    if gather is None:
        return _pc(body, name="attn_fwd", grid=(npair, t // tq), in_specs=in_specs, out_specs=out_specs,
                   out_shape=out_shape, sem=("arbitrary", "arbitrary"))(qc, kc, v)
    return _pc(body, name="attn_fwd_gather", grid=(npair, t // tq), in_specs=in_specs + [HBM, HBM],
               out_specs=out_specs + [HBM], out_shape=out_shape + [_sds((4,) + gather.shape, bf16)], aliases={4: 2},
               scratch=[pltpu.SemaphoreType.DMA((2, 6))], sem=("arbitrary", "arbitrary"))(
                   qc, kc, v, gather, _gather_base(gather))


def _attn_bwd(qc, kc, v, do, lse, delta, exchange=None):
    t = qc.shape[0]
    tq, tk = min(ATT_BWD_TQ, t), min(ATT_BWD_TK, t)
    npair = MLA_HEADS // 2
    nq = t // tq
    sub = tq // tk
    assert sub * tk == tq

    def body(q_ref, do_ref, lse_ref, dl_ref, k_ref, v_ref, *rest):
        if exchange is None:
            dq_ref, dk_ref, dv_ref = rest
        else:
            sb_ref, dq_ref, dk_ref, dv_ref, recv_ref, sems = rest
            first, last = _first_last_step(npair, t // tk)
            pl.when(first)(lambda: _chips_start(sb_ref, recv_ref, sems))
            pl.when(last)(lambda: _chips_finish(sb_ref, recv_ref, sems))
        j = pl.program_id(1)

        @pl.when(j == 0)
        def _():
            dq_ref[...] = jnp.zeros_like(dq_ref)

        k = [k_ref[:, _qk_cols(e)] for e in range(2)]
        vv = [v_ref[:, _v_cols(e)] for e in range(2)]

        def rows_step(carry, row0, rows, masked):
            qs = pl.ds(pl.multiple_of(row0, rows), rows)
            ok = _causal((rows, tk), row0, j * tk) if masked else None
            lse2, dl2 = lse_ref[qs, :], dl_ref[qs, :]
            new = []
            for e in range(2):
                dk, dv = carry[e]
                q_e, do_e = q_ref[qs, _qk_cols(e)], do_ref[qs, _v_cols(e)]
                p = jnp.exp(_bd(q_e, k[e], 1, 1) - lse2[:, MLA_ROPE * e:MLA_ROPE * e + 1])
                if masked:
                    p = jnp.where(ok, p, 0.0)
                dv = dv + _bd(p, do_e, 0, 0)
                dp = _bd(do_e, vv[e], 1, 1)
                ds = (p * (dp - dl2[:, MLA_ROPE * e:MLA_ROPE * e + 1])).astype(bf16)
                dk = dk + _bd(ds, q_e, 0, 0)
                dq_ref[qs, _qk_cols(e)] += _bd(ds, k[e], 1, 0)
                new.append((dk, dv))
            return tuple(new)

        one = (jnp.zeros((tk, ATT_QK), f32), jnp.zeros((tk, MLA_V), f32))
        i0 = (j * tk) // tq
        j_local = j - i0 * sub
        carry = (one, one)
        for r in range(sub):
            run = functools.partial(rows_step, row0=i0 * tq + r * tk, rows=tk, masked=True)
            carry = run(carry) if r == sub - 1 else lax.cond(r >= j_local, run, lambda c: c, carry)
        carry = lax.fori_loop(i0 + 1, nq, lambda i, c: rows_step(c, i * tq, tq, False), carry)
        for e in range(2):
            dk_ref[:, _qk_cols(e)] = carry[e][0].astype(dk_ref.dtype)
            dv_ref[:, _v_cols(e)] = carry[e][1].astype(dv_ref.dtype)

    res = lambda w: pl.BlockSpec((t, w), lambda p, j: (0, p))
    blk = lambda w: pl.BlockSpec((tk, w), lambda p, j: (j, p))
    in_specs = [res(2 * ATT_QK), res(2 * MLA_V), res(LANES), res(LANES), blk(2 * ATT_QK), blk(2 * MLA_V)]
    out_specs = [res(2 * ATT_QK), blk(2 * ATT_QK), blk(2 * MLA_V)]
    out_shape = [_sds((t, MLA_HEADS * ATT_QK), f32), _sds((t, MLA_HEADS * ATT_QK), bf16), _sds((t, MLA_HEADS * MLA_V), bf16)]
    if exchange is None:
        return _pc(body, name="attn_bwd", grid=(npair, t // tk), in_specs=in_specs, out_specs=out_specs,
                   out_shape=out_shape, sem=("arbitrary", "arbitrary"))(qc, do, lse, delta, kc, v)
    return _pc(body, name="attn_bwd_exchange", grid=(npair, t // tk), in_specs=in_specs + [HBM],
               out_specs=out_specs + [HBM], out_shape=out_shape + [_sds((3,) + exchange.shape[1:], bf16)],
               scratch=[pltpu.SemaphoreType.DMA((2, 3))], sem=("arbitrary", "arbitrary"))(
                   qc, do, lse, delta, kc, v, exchange)


def _rope_tables(t):
    half = MLA_ROPE // 2
    inv_freq = ROPE_THETA ** (-jnp.arange(half, dtype=f32) / half)
    ang = jnp.arange(t, dtype=f32)[:, None] * inv_freq[None, :]
    cos, sin = jnp.cos(ang), jnp.sin(ang)
    return jnp.concatenate([cos, cos] * 2, axis=1), jnp.concatenate([-sin, sin] * 2, axis=1)


def _relu2_epi(u):
    r = jnp.maximum(u, 0.0)
    return u, r * r


def _add_epi(r, res):
    return (r + res,)


def _drelu2_epi(da, u):
    return (da * 2.0 * jnp.maximum(u.astype(f32), 0.0),)


ROWWISE_EPI_TM = 512


def _residual_norms_epi(r, res, *gains):
    h = r + res
    return (h, *[_rms(h, g) for g in gains])


def _residual_out(a, w, h, gains, name):
    res = _mm(a, w, name=name, outs=(f32,) + (bf16,) * len(gains), epi=_residual_norms_epi, extras=(h, *gains),
              tm=ROWWISE_EPI_TM)
    return res if gains else [res]


def _dnorm_epi(dy, x, dres, gain):
    dx, dg = _rms_bwd(x, gain, dy)
    return dx + dres, dx + dres, dg


def _mlp_fwd(h, xm, w_up, w_down, tag, next_gains):
    u, a = _mm(xm, w_up, name=f"mlp{tag}_up", outs=(bf16, bf16), epi=_relu2_epi)
    h_out, *normed = _residual_out(a, w_down, h, next_gains, f"mlp{tag}_down")
    return h_out, normed, (xm, u, a)


def _mlp_bwd(dh, dh16, h, gain, w_up, w_down, saved, tag):
    xm, u, a = saved
    du = _mm(dh16, w_down, tb=True, name=f"mlp{tag}_dact", outs=(bf16,), epi=_drelu2_epi, extras=(u,))
    d_down = _wgrad(a, dh16, f"mlp{tag}_dwdown")
    d_up = _wgrad(xm, du, f"mlp{tag}_dwup")
    dh_in, dh_in16, d_gain = _mm(du, w_up, tb=True, name=f"mlp{tag}_dxm", outs=(f32, bf16), accs=1, epi=_dnorm_epi,
                                 extras=(h, dh, gain), tm=ROWWISE_EPI_TM)
    return dh_in, dh_in16, d_gain, d_up, d_down


def _local_step(x, target, w, comm=None):
    w = dict(w)
    t = x.shape[0]
    cos, sgn_sin = _rope_tables(t)
    grads = {}

    xn0 = _rw(lambda xx, g: (_rms(xx, g),), [x], [w["hgrn_norm"]], [(D_MODEL, bf16)], name="hgrn_norm")[0]
    p4 = _mm(xn0, w["hgrn_w4"], name="hgrn_proj", tn=2048)

    if comm is None:
        y, states = _gla_fwd(p4, w["hgrn_lb_logits"], w["hgrn_g_norm"])
    else:
        y, states, gathered = _gla_fwd(p4, w["hgrn_lb_logits"], w["hgrn_g_norm"], gather=comm.shard["gla"])
        w.update(comm.unpack["gla"](gathered))
    h1, xm0 = _residual_out(y, w["hgrn_w_o"], x, [w["mlp_norm"][0:1]], "hgrn_out")
    h2, (hk, xn1), mlp0 = _mlp_fwd(h1, xm0, w["mlp_w_up", 0], w["mlp_w_down", 0], 0, [w["kv_in_norm"], w["mla_norm"]])

    ckr = _mm(hk, w["kv_w_dkv"], name="kv_down")

    def ckv_fn(c, cs, sn, g):
        kr = _rope(c[:, MLA_KV_LORA:], cs, sn)
        return _rms(c[:, :MLA_KV_LORA], g), jnp.concatenate([jnp.zeros_like(kr), kr], axis=1)

    c_kv, kr_head = _rw(ckv_fn, [ckr, cos, sgn_sin], [w["kv_norm"]], [(MLA_KV_LORA, bf16), (ATT_QK, f32)],
                        name="kv_norm_rope")
    kc = _mm(c_kv, w["kv_w_kcat"], name="kv_up_k", outs=(bf16,), extras=(kr_head,),
             epi=lambda r, kr: (r + _tile_lanes(kr, r.shape[1]),))
    v_att = _mm(c_kv, w["kv_w_uv"], name="kv_up_v", outs=(bf16,))
    cq0 = _mm(xn1, w["mla_w_dq"], name="q_down")
    c_q = _rw(lambda c, g: (_rms(c, g),), [cq0], [w["mla_q_norm"]], [(MLA_Q_LORA, bf16)], name="q_norm")[0]
    qc = _mm(c_q, w["mla_w_qcat"], name="q_up", outs=(bf16,), extras=(cos, sgn_sin),
             epi=lambda r, cs, sn: (_rope_heads(r, cs, sn, 1.0, ATT_SCALE),))
    if comm is None:
        o_att, lse = _attn_fwd(qc, kc, v_att)
    else:
        o_att, lse, gathered = _attn_fwd(qc, kc, v_att, gather=comm.shard["attn"])
        w.update(comm.unpack["attn"](gathered))
    h3, xm1 = _residual_out(o_att, w["mla_w_o"], h2, [w["mlp_norm"][1:2]], "mla_out")
    u1, a1 = _mm(xm1, w["mlp_w_up", 1], name="mlp1_up", outs=(bf16, bf16), epi=_relu2_epi)
    mlp1 = (xm1, u1, a1)

    def loss_epi(r, res, tgt, gain):
        def f(a, b):
            e = _rms(a, b) - tgt
            return 0.5 * jnp.sum(jnp.sum(e * e, axis=-1, keepdims=True) / D_MODEL, axis=0, keepdims=True)
        val, vjp = jax.vjp(f, r + res, gain)
        dh, dg = vjp(jnp.ones((1, 1), f32))
        return dh, dh, jnp.broadcast_to(val, (1, D_MODEL)), dg

    dh4, dh4_16, loss_acc, grads["final_norm"] = _mm(
        a1, w["mlp_w_down", 1], name="mlp1_down_loss", outs=(f32, bf16), accs=2, epi=loss_epi,
        extras=(h3, target, w["final_norm"]), tm=ROWWISE_EPI_TM)
    loss = loss_acc[0, 0]

    dh3, dh3_16, g_n1, grads["mlp_w_up", 1], grads["mlp_w_down", 1] = _mlp_bwd(
        dh4, dh4_16, h3, w["mlp_norm"][1:2], w["mlp_w_up", 1], w["mlp_w_down", 1], mlp1, 1)
    do_att = _mm(dh3_16, w["mla_w_o"], tb=True, name="mla_dout", outs=(bf16,))
    grads["mla_w_o"] = _wgrad(o_att, dh3_16, "mla_dwo")

    def delta_fn(a, b):
        prod = a.astype(f32) * b.astype(f32)
        outs = []
        for p in range(MLA_HEADS // 2):
            d0 = jnp.sum(prod[:, 2 * p * LANES:(2 * p + 1) * LANES], axis=-1, keepdims=True)
            d1 = jnp.sum(prod[:, (2 * p + 1) * LANES:(2 * p + 2) * LANES], axis=-1, keepdims=True)
            lo, _ = _pair_masks((a.shape[0], LANES))
            outs.append(jnp.where(lo, d0, d1))
        return (jnp.concatenate(outs, axis=1),)

    delta = _rw(delta_fn, [do_att, o_att], [], [(MLA_HEADS // 2 * LANES, f32)], name="attn_delta")[0]
    if comm is None:
        dqc, dkc, dv = _attn_bwd(qc, kc, v_att, do_att, lse, delta)
    else:
        dqc, dkc, dv, comm.received["attn"] = _attn_bwd(qc, kc, v_att, do_att, lse, delta,
                                                        exchange=comm.reduce(grads, "attn"))
    dqf = _rw(lambda a, cs, sn: (_rope_heads(a, cs, sn, -1.0, ATT_SCALE),), [dqc, cos, sgn_sin], [],
              [(MLA_HEADS * ATT_QK, bf16)], name="dq_rope")[0]
    dc_q = _mm(dqf, w["mla_w_qcat"], tb=True, name="q_up_dx")
    grads["mla_w_qcat"] = _wgrad(c_q, dqf, "q_up_dw")

    def dqn_fn(c, dy, g):
        return _rms_bwd(c, g, dy)

    dcq0, grads["mla_q_norm"] = _rw(dqn_fn, [cq0, dc_q], [w["mla_q_norm"]], [(MLA_Q_LORA, bf16)], [(1, MLA_Q_LORA)],
                                    name="q_dnorm")
    dxn1 = _mm(dcq0, w["mla_w_dq"], tb=True, name="q_down_dx")
    grads["mla_w_dq"] = _wgrad(xn1, dcq0, "q_down_dw")

    dc_kv = _mm(dkc, w["kv_w_kcat"], tb=True, name="kv_up_dx_k")
    dc_kv = _mm(dv, w["kv_w_uv"], tb=True, name="kv_up_dx_v", epi=_add_epi, extras=(dc_kv,))
    grads["kv_w_kcat"] = _wgrad(c_kv, dkc, "kv_up_dw_k")
    grads["kv_w_uv"] = _wgrad(c_kv, dv, "kv_up_dw_v")

    def dckr_fn(c, dc, dk_heads, cs, sn, g):
        tot = dk_heads[:, LANES:ATT_QK].astype(f32)
        for h in range(1, MLA_HEADS):
            tot = tot + dk_heads[:, ATT_QK * h + LANES:ATT_QK * (h + 1)].astype(f32)
        lo, _ = _pair_masks(tot.shape)
        dkr = jnp.where(lo, _rope(tot, cs, sn, -1.0), 0.0)
        dcc, dg = _rms_bwd(c[:, :MLA_KV_LORA], g, dc)
        return jnp.concatenate([dcc, dkr], axis=1), dg

    dckr, grads["kv_norm"] = _rw(dckr_fn, [ckr, dc_kv, dkc, cos, sgn_sin], [w["kv_norm"]],
                                 [(MLA_KV_LORA + LANES, bf16)], [(1, MLA_KV_LORA)], name="kv_dnorm_rope")
    dhk = _mm(dckr, w["kv_w_dkv"], tb=True, name="kv_down_dx")
    grads["kv_w_dkv"] = _wgrad(hk, dckr, "kv_down_dw")

    def dh2_fn(hh, d1, d2, dres, g1, g2):
        a, ga = _rms_bwd(hh, g1, d1)
        b, gb = _rms_bwd(hh, g2, d2)
        return a + b + dres, a + b + dres, ga, gb

    dh2, dh2_16, grads["kv_in_norm"], grads["mla_norm"] = _rw(
        dh2_fn, [h2, dhk, dxn1, dh3], [w["kv_in_norm"], w["mla_norm"]], [(D_MODEL, f32), (D_MODEL, bf16)],
        [(1, D_MODEL)] * 2, name="kv_mla_dnorm")

    dh1, dh1_16, g_n0, grads["mlp_w_up", 0], grads["mlp_w_down", 0] = _mlp_bwd(
        dh2, dh2_16, h1, w["mlp_norm"][0:1], w["mlp_w_up", 0], w["mlp_w_down", 0], mlp0, 0)
    grads["mlp_norm"] = jnp.concatenate([g_n0, g_n1], axis=0)
    dy = _mm(dh1_16, w["hgrn_w_o"], tb=True, name="hgrn_dout")
    grads["hgrn_w_o"] = _wgrad(y, dh1_16, "hgrn_dwo")

    gla_args = (p4, w["hgrn_lb_logits"], w["hgrn_g_norm"], states, dy)
    if comm is None:
        dp4, grads["hgrn_lb_logits"], grads["hgrn_g_norm"] = _gla_bwd(*gla_args)
    else:
        dp4, grads["hgrn_lb_logits"], grads["hgrn_g_norm"], comm.received["gla"] = _gla_bwd(
            *gla_args, exchange=comm.reduce(grads, "gla"))
    grads["hgrn_w4"] = _mm(xn0, dp4, ta=True, name="hgrn_proj_dw")
    grad_x, grads["hgrn_norm"] = _mm(dp4, w["hgrn_w4"], tb=True, name="hgrn_proj_dx", outs=(f32,), accs=1,
                                     epi=lambda *args: _dnorm_epi(*args)[1:], extras=(x, dh1, w["hgrn_norm"]),
                                     tm=ROWWISE_EPI_TM)
    return loss, grad_x, grads


HBM = pl.BlockSpec(memory_space=pltpu.HBM)


def _me():
    return lax.axis_index("x"), lax.axis_index("y"), lax.axis_index("c")


def _flip(x, y, f):
    return (1 - x if f & 1 else x), (1 - y if f & 2 else y)


def _rcopy(src, dst, sems, k, dev):
    return pltpu.make_async_remote_copy(src_ref=src, dst_ref=dst, send_sem=sems.at[0, k], recv_sem=sems.at[1, k],
                                        device_id=dev, device_id_type=MESH)


def _my_half(rows, c, mine=True):
    half = rows // 2
    return pl.ds(pl.multiple_of((c if mine else 1 - c) * half, 16), half)


def _gather_start(wp_ref, out_ref, sems):
    x, y, c = _me()
    half = _my_half(wp_ref.shape[0], c)
    for f in (1, 2, 3):
        px, py = _flip(x, y, f)
        _rcopy(wp_ref.at[half], out_ref.at[2 * x + y, half], sems, f - 1, (px, py, c)).start()


def _gather_finish(wp_ref, out_ref, sems):
    x, y, c = _me()
    half, other = _my_half(wp_ref.shape[0], c), _my_half(wp_ref.shape[0], c, mine=False)
    sends = []
    for f in (1, 2, 3):
        px, py = _flip(x, y, f)
        landed = out_ref.at[2 * px + py, half]
        _rcopy(landed, landed, sems, f - 1, (px, py, c)).wait_recv()
        sends.append(_rcopy(landed, landed, sems, 2 + f, (x, y, 1 - c)))
        sends[-1].start()
    for f in (1, 2, 3):
        px, py = _flip(x, y, f)
        theirs = out_ref.at[2 * px + py, other]
        _rcopy(theirs, theirs, sems, 2 + f, (x, y, 1 - c)).wait_recv()
        sends.append(_rcopy(wp_ref.at[half], out_ref.at[2 * x + y, half], sems, f - 1, (px, py, c)))
    for cp in sends:
        cp.wait_send()


def _gather_base(wp):
    return jnp.broadcast_to(wp[None], (4,) + wp.shape)


def _all_gather_weights(wp, sv):
    def body(wp_ref, sv_ref, base_ref, out_ref, svs_ref, sems, local_sem):
        x, y, c = _me()
        mine = pltpu.make_async_copy(sv_ref, svs_ref.at[2 * x + y], local_sem)
        mine.start()
        _gather_start(wp_ref, out_ref, sems)
        small = []
        for f in (1, 2, 3):
            px, py = _flip(x, y, f)
            small.append(_rcopy(sv_ref, svs_ref.at[2 * x + y], sems, 5 + f, (px, py, c)))
            small[-1].start()
        _gather_finish(wp_ref, out_ref, sems)
        for f in (1, 2, 3):
            px, py = _flip(x, y, f)
            _rcopy(sv_ref, svs_ref.at[2 * px + py], sems, 5 + f, (px, py, c)).wait_recv()
        for cp in small:
            cp.wait_send()
        mine.wait()

    return _pc(body, name="weights_all_gather", in_specs=[HBM, HBM, HBM], out_specs=[HBM, HBM],
               out_shape=[_sds((4,) + wp.shape, bf16), _sds((4, 8, 256), f32)], aliases={2: 0},
               scratch=[pltpu.SemaphoreType.DMA((2, 9)), pltpu.SemaphoreType.DMA])(wp, sv, _gather_base(wp))


def _send_half_to_sibling(gp, name):
    rows = gp.shape[1]

    def body(gp_ref, out_ref, sems):
        x, y, c = _me()
        cp = _rcopy(gp_ref.at[:, _my_half(rows, c, mine=False)], out_ref, sems, 0, (x, y, 1 - c))
        cp.start()
        cp.wait()

    return _pc(body, name=name, in_specs=[HBM], out_specs=HBM, out_shape=_sds((4, rows // 2, D_MODEL), gp.dtype),
               scratch=[pltpu.SemaphoreType.DMA((2, 1))])(gp)


def _chips_start(sb_ref, out_ref, sems):
    x, y, c = _me()
    for f in (1, 2, 3):
        px, py = _flip(x, y, f)
        _rcopy(sb_ref.at[2 * px + py], out_ref.at[f - 1], sems, f - 1, (px, py, c)).start()


def _chips_finish(sb_ref, out_ref, sems):
    x, y, c = _me()
    for f in (1, 2, 3):
        _rcopy(sb_ref.at[0], out_ref.at[f - 1], sems, f - 1, (x, y, c)).wait_recv()
    for f in (1, 2, 3):
        px, py = _flip(x, y, f)
        _rcopy(sb_ref.at[2 * px + py], out_ref.at[f - 1], sems, f - 1, (px, py, c)).wait_send()


def _exchange_chips(sb, small):
    def body(sb_ref, small_ref, out_ref, smalls_ref, sems, local_sem):
        x, y, c = _me()
        me = 4 * x + 2 * y + c
        mine = pltpu.make_async_copy(small_ref, smalls_ref.at[me], local_sem)
        mine.start()
        _chips_start(sb_ref, out_ref, sems)
        sends = []
        for f in range(1, 8):
            px, py = _flip(x, y, f)
            pc = 1 - c if f & 4 else c
            sends.append(_rcopy(small_ref, smalls_ref.at[me], sems, 2 + f, (px, py, pc)))
            sends[-1].start()
        _chips_finish(sb_ref, out_ref, sems)
        for f in range(1, 8):
            px, py = _flip(x, y, f)
            pc = 1 - c if f & 4 else c
            _rcopy(small_ref, smalls_ref.at[4 * px + 2 * py + pc], sems, 2 + f, (x, y, c)).wait_recv()
        for cp in sends:
            cp.wait_send()
        mine.wait()

    return _pc(body, name="grads_exchange_chips", in_specs=[HBM, HBM], out_specs=[HBM, HBM],
               out_shape=[_sds((3,) + sb.shape[1:], bf16), _sds((8, SMALL_ROWS, D_MODEL), f32)],
               scratch=[pltpu.SemaphoreType.DMA((2, 10)), pltpu.SemaphoreType.DMA])(sb, small)


def _exchange_halves(tot, name):
    rows = tot.shape[0]

    def body(tot_ref, out_ref, sems):
        x, y, c = _me()
        half = _my_half(rows, c)
        cp = _rcopy(tot_ref.at[half], out_ref.at[half], sems, 0, (x, y, 1 - c))
        cp.start()
        cp.wait()

    return _pc(body, name=name, in_specs=[HBM], out_specs=HBM, out_shape=_sds((rows, D_MODEL), f32),
               aliases={0: 0}, scratch=[pltpu.SemaphoreType.DMA((2, 1))])(tot)


def _sum_rows(half):
    return max(r for r in range(16, 513, 16) if half % r == 0)


def _sum_over_cores(gp, recv, cq, name):
    half = recv.shape[1]
    tr = _sum_rows(half)
    nb = half // tr

    def body(cq_ref, g_ref, r_ref, o32_ref, o16_ref):
        s = g_ref[...].astype(f32) + r_ref[...].astype(f32)
        o32_ref[...] = s
        o16_ref[...] = s.astype(bf16)

    spec = pl.BlockSpec((1, tr, D_MODEL), lambda b, i, cq_ref: (b, i, 0))
    gs = pltpu.PrefetchScalarGridSpec(
        num_scalar_prefetch=1, grid=(4, nb),
        in_specs=[pl.BlockSpec((1, tr, D_MODEL), lambda b, i, cq_ref: (b, cq_ref[0] * nb + i, 0)), spec],
        out_specs=[spec, spec])
    return _pc(body, name=name, grid_spec=gs, sem=("arbitrary", "arbitrary"),
               out_shape=[_sds((4, half, D_MODEL), f32), _sds((4, half, D_MODEL), bf16)])(cq, gp, recv)


def _sum_over_chips(s32, recv, cq, name):
    half = recv.shape[1]
    tr = _sum_rows(half)
    nb = half // tr

    def body(cq_ref, own_ref, r_ref, o_ref):
        o_ref[...] = ((own_ref[0] + r_ref[0].astype(f32)) + r_ref[1].astype(f32)) + r_ref[2].astype(f32)

    gs = pltpu.PrefetchScalarGridSpec(
        num_scalar_prefetch=1, grid=(nb,),
        in_specs=[pl.BlockSpec((1, tr, D_MODEL), lambda i, cq_ref: (cq_ref[1], i, 0)),
                  pl.BlockSpec((3, tr, D_MODEL), lambda i, cq_ref: (0, i, 0))],
        out_specs=pl.BlockSpec((tr, D_MODEL), lambda i, cq_ref: (cq_ref[0] * nb + i, 0)))
    return _pc(body, name=name, grid_spec=gs, sem=("arbitrary",),
               out_shape=_sds((2 * half, D_MODEL), f32))(cq, s32, recv)


def _sum_small(smalls):
    def body(s_ref, o_ref):
        tot = s_ref[0]
        for d in range(1, 8):
            tot = tot + s_ref[d]
        o_ref[...] = tot

    return _pc(body, name="small_sum", out_shape=_sds((SMALL_ROWS, D_MODEL), f32))(smalls)


def _adamw_math(w, g, m, v):
    m = ADAM_B1 * m + (1.0 - ADAM_B1) * g
    v = ADAM_B2 * v + (1.0 - ADAM_B2) * jnp.square(g)
    m_hat = m / (1.0 - ADAM_B1 ** ADAM_STEP)
    v_hat = v / (1.0 - ADAM_B2 ** ADAM_STEP)
    delta = -ADAM_LR * (m_hat / (jnp.sqrt(v_hat) + ADAM_EPS) + ADAM_WD * w)
    return delta, m, v


def _adamw(w, g, m, v, name):
    cols = w.shape[1]
    return _rw(_adamw_math, [w, g, m, v], [], [(cols, f32)] * 3, name=name, tr=256)


def _adamw_small(items):
    n = len(items)

    def body(*refs):
        ins, outs = refs[:4 * n], refs[4 * n:]
        for i in range(n):
            res = _adamw_math(*[r[...] for r in ins[4 * i:4 * i + 4]])
            for o, val in zip(outs[3 * i:3 * i + 3], res):
                o[...] = val

    flat = [a for it in items for a in it]
    out_shape = [_sds(it[0].shape, f32) for it in items for _ in range(3)]
    res = _pc(body, name="adamw_small", out_shape=out_shape)(*flat)
    return [tuple(res[3 * i:3 * i + 3]) for i in range(n)]


def _pack_shards(sh, layout, pad):
    parts = [(sh[n] if layer is None else sh[n][layer]).reshape(-1, D_MODEL).astype(bf16) for n, layer, _ in layout]
    if pad:
        parts.append(jnp.zeros((pad, D_MODEL), bf16))
    return jnp.concatenate(parts, axis=0)


def _mlp_full(g4, off, layer):
    o, r = off["mlp_w_up", layer]
    up = g4[:, o:o + r].transpose(1, 0, 2).reshape(D_MODEL, D_FF)
    o, r = off["mlp_w_down", layer]
    return {("mlp_w_up", layer): up, ("mlp_w_down", layer): g4[:, o:o + r].reshape(D_FF, D_MODEL)}


def _unpack_early(g4):
    w = _mlp_full(g4, W_EARLY_OFF, 0)
    hg = g4[:, 0:1024].reshape(4, 4, 256, D_MODEL)
    w["hgrn_w4"] = hg.transpose(0, 2, 1, 3).reshape(D_MODEL, 4 * D_MODEL)
    o, r = W_EARLY_OFF["hgrn_w_o", None]
    w["hgrn_w_o"] = g4[:, o:o + r].reshape(D_MODEL, D_MODEL)
    return w


def _unpack_last(g4):
    return _mlp_full(g4, W_LAST_OFF, 1)


def _unpack_mid(g4):
    def rows(name):
        o, r = W_MID_OFF[name, None]
        return g4[:, o:o + r]

    w = {"mla_w_dq": rows("mla_w_dq").reshape(D_MODEL, MLA_Q_LORA)}
    uq = rows("mla_w_uq").reshape(4, MLA_Q_LORA, 768).transpose(1, 0, 2).reshape(MLA_Q_LORA, MLA_HEADS, MLA_NOPE + MLA_ROPE)
    w["mla_w_qcat"] = jnp.pad(uq, ((0, 0), (0, 0), (0, ATT_QK - MLA_NOPE - MLA_ROPE))).reshape(MLA_Q_LORA, MLA_HEADS * ATT_QK)
    w["mla_w_o"] = rows("mla_w_o").reshape(MLA_HEADS * MLA_V, D_MODEL)
    dkv = rows("kv_w_dkv").reshape(D_MODEL, MLA_KV_LORA + MLA_ROPE)
    w["kv_w_dkv"] = jnp.pad(dkv, ((0, 0), (0, LANES - MLA_ROPE)))
    uk = rows("kv_w_uk").reshape(4, MLA_KV_LORA, 512).transpose(1, 0, 2).reshape(MLA_KV_LORA, MLA_HEADS, MLA_NOPE)
    w["kv_w_kcat"] = jnp.pad(uk, ((0, 0), (0, 0), (0, ATT_QK - MLA_NOPE))).reshape(MLA_KV_LORA, MLA_HEADS * ATT_QK)
    w["kv_w_uv"] = rows("kv_w_uv").reshape(4, MLA_KV_LORA, 512).transpose(1, 0, 2).reshape(MLA_KV_LORA, MLA_HEADS * MLA_V)
    return w


def _pack_grads_late(g):
    return g["hgrn_w4"].reshape(4, 256, 4, D_MODEL).transpose(0, 2, 1, 3).reshape(4, G_LATE_ROWS, D_MODEL)


def _grad_rows(g, name, layer):
    if name in ("mlp_w_up", "mlp_w_down"):
        full = g[name, layer]
        return full.reshape(D_MODEL, 4, 1024).transpose(1, 0, 2) if name == "mlp_w_up" else full.reshape(4, 1024, D_MODEL)
    if name == "mla_w_uq":
        uq = g["mla_w_qcat"].reshape(MLA_Q_LORA, MLA_HEADS, ATT_QK)[:, :, :MLA_NOPE + MLA_ROPE]
        return uq.reshape(MLA_Q_LORA, 4, 768).transpose(1, 0, 2).reshape(4, 192, D_MODEL)
    if name == "kv_w_uk":
        uk = g["kv_w_kcat"].reshape(MLA_KV_LORA, MLA_HEADS, ATT_QK)[:, :, :MLA_NOPE]
        return uk.reshape(MLA_KV_LORA, 4, 512).transpose(1, 0, 2).reshape(4, 128, D_MODEL)
    if name == "kv_w_uv":
        return g[name].reshape(MLA_KV_LORA, 4, 512).transpose(1, 0, 2).reshape(4, 128, D_MODEL)
    if name == "kv_w_dkv":
        return g[name][:, :MLA_KV_LORA + MLA_ROPE].reshape(4, 80, D_MODEL)
    return g[name].reshape(4, -1, D_MODEL)


def _pack_grads(g, layout):
    parts = [_grad_rows(g, name, layer) for name, layer, _ in layout]
    if layout in PADDED:
        parts.append(jnp.zeros((4, PACK_PAD, D_MODEL), bf16))
    return jnp.concatenate(parts, axis=1)


LOSS_ROW = 11


def _pack_small(g, loss):
    rows = []
    for name, _, r, wd in SMALL:
        a = g[name].reshape(r, wd)
        rows.append(jnp.pad(a, ((0, 0), (0, D_MODEL - wd))) if wd < D_MODEL else a)
    assert sum(r for _, _, r, _ in SMALL) == LOSS_ROW
    rows.append(jnp.full((1, D_MODEL), loss, f32))
    rows.append(jnp.zeros((SMALL_ROWS - LOSS_ROW - 1, D_MODEL), f32))
    return jnp.concatenate(rows, axis=0)


def kernel(x, hgrn_norm, hgrn_w_q, hgrn_w_f, hgrn_w_i, hgrn_w_g, hgrn_g_norm, hgrn_w_o, hgrn_lb_logits, mla_norm, mla_w_dq, mla_q_norm, mla_w_uq, mla_w_o, kv_in_norm, kv_w_dkv, kv_norm, kv_w_uk, kv_w_uv, mlp_norm, mlp_w_up, mlp_w_down, final_norm, loss_target, m_hgrn_norm, m_hgrn_w_q, m_hgrn_w_f, m_hgrn_w_i, m_hgrn_w_g, m_hgrn_g_norm, m_hgrn_w_o, m_hgrn_lb_logits, m_mla_norm, m_mla_w_dq, m_mla_q_norm, m_mla_w_uq, m_mla_w_o, m_kv_in_norm, m_kv_w_dkv, m_kv_norm, m_kv_w_uk, m_kv_w_uv, m_mlp_norm, m_mlp_w_up, m_mlp_w_down, m_final_norm, v_hgrn_norm, v_hgrn_w_q, v_hgrn_w_f, v_hgrn_w_i, v_hgrn_w_g, v_hgrn_g_norm, v_hgrn_w_o, v_hgrn_lb_logits, v_mla_norm, v_mla_w_dq, v_mla_q_norm, v_mla_w_uq, v_mla_w_o, v_kv_in_norm, v_kv_w_dkv, v_kv_norm, v_kv_w_uk, v_kv_w_uv, v_mlp_norm, v_mlp_w_up, v_mlp_w_down, v_final_norm):
    given = dict(locals())
    wsh = {n: given[n] for n in WEIGHTS}
    msh = {n: given["m_" + n] for n in WEIGHTS}
    vsh = {n: given["v_" + n] for n in WEIGHTS}
    xi, yi, ci = _me()
    chip = 2 * xi + yi
    cq = jnp.stack([ci, chip]).astype(jnp.int32)

    small_w = {n: wsh[n].reshape(r, -1) for n, _, r, _ in SMALL}
    sv = jnp.concatenate([small_w["hgrn_norm"], small_w["hgrn_lb_logits"], jnp.zeros((5, 256), f32)], axis=0)
    g4, sv4 = _all_gather_weights(_pack_shards(wsh, W_EARLY, 0), sv)
    w = _unpack_early(g4)
    w["hgrn_norm"] = sv4[:, 0, :].reshape(1, D_MODEL)
    w["hgrn_lb_logits"] = sv4[:, 1:3, :].transpose(1, 0, 2).reshape(2, D_MODEL)
    for n in ("hgrn_g_norm", "mla_norm", "mla_q_norm", "kv_in_norm", "kv_norm", "mlp_norm", "final_norm"):
        w[n] = small_w[n]

    class Comm:
        shard = {"gla": _pack_shards(wsh, W_MID, PACK_PAD), "attn": _pack_shards(wsh, W_LAST, 0)}
        unpack = {"gla": _unpack_mid, "attn": _unpack_last}
        layout = {"gla": G_GLA, "attn": G_ATTN}
        received, s32 = {}, {}

        @staticmethod
        def reduce(grads, part):
            gp = _pack_grads(grads, Comm.layout[part])
            Comm.s32[part], s16 = _sum_over_cores(gp, _send_half_to_sibling(gp, "grads_to_sibling_" + part), cq,
                                                  "grads_sum_cores_" + part)
            return s16

    loss, grad_x, g = _local_step(x.reshape(-1, D_MODEL), loss_target.reshape(-1, D_MODEL), w, Comm)

    total = {part: _exchange_halves(_sum_over_chips(Comm.s32[part], Comm.received[part], cq, "grads_sum_chips_" + part),
                                    "grads_exchange_halves_" + part) for part in ("attn", "gla")}
    gp = _pack_grads_late(g)
    s32, s16 = _sum_over_cores(gp, _send_half_to_sibling(gp, "grads_to_sibling_late"), cq, "grads_sum_cores_late")
    from_chips, smalls = _exchange_chips(s16, _pack_small(g, loss))
    total["late"] = _exchange_halves(_sum_over_chips(s32, from_chips, cq, "grads_sum_chips_late"),
                                     "grads_exchange_halves_late")
    small_tot = _sum_small(smalls)
    loss = small_tot[LOSS_ROW, 0]

    where = {}
    for part, offsets in (("late", G_LATE_OFF), ("gla", G_GLA_OFF), ("attn", G_ATTN_OFF)):
        for (n, layer), (o, r) in offsets.items():
            where.setdefault(n, []).append(total[part][o:o + r])
    grad, delta, new_m, new_v = {}, {}, {}, {}
    for n, pieces in where.items():
        shp = wsh[n].shape
        two_d = (-1, shp[-1])
        grad[n] = (pieces[0] if len(pieces) == 1 else jnp.concatenate(pieces, axis=0)).reshape(shp)
        d, m2, v2 = _adamw(wsh[n].reshape(two_d), grad[n].reshape(two_d), msh[n].reshape(two_d), vsh[n].reshape(two_d),
                           "adamw_" + n)
        delta[n], new_m[n], new_v[n] = d.reshape(shp), m2.reshape(shp), v2.reshape(shp)
    items = []
    for n, row, r, wd in SMALL:
        gs = small_tot[row:row + r, :wd]
        if n in ("hgrn_norm", "hgrn_lb_logits"):
            gs = lax.dynamic_slice(gs, (0, 256 * chip), (r, 256))
        grad[n] = gs.reshape(wsh[n].shape)
        items.append((small_w[n], gs, msh[n].reshape(gs.shape), vsh[n].reshape(gs.shape)))
    for (n, _, _, _), (d, m2, v2) in zip(SMALL, _adamw_small(items)):
        shp = wsh[n].shape
        delta[n], new_m[n], new_v[n] = d.reshape(shp), m2.reshape(shp), v2.reshape(shp)

    return (loss, grad_x.reshape(x.shape), *[grad[n] for n in WEIGHTS], *[delta[n] for n in WEIGHTS],
            *[new_m[n] for n in WEIGHTS], *[new_v[n] for n in WEIGHTS])
```

```python
import functools

import jax
import jax.numpy as jnp
from jax import lax
from jax.experimental import pallas as pl
from jax.experimental.pallas import tpu as pltpu

f32, bf16 = jnp.float32, jnp.bfloat16
HI = lax.Precision.HIGHEST
MESH = pl.DeviceIdType.MESH

D_MODEL = 1024
D_FF = 4096
EPS = 1e-6
HGRN_HEADS, HGRN_DK, HGRN_CHUNK, HGRN_SUB = 8, 128, 64, 16
MLA_HEADS, MLA_NOPE, MLA_ROPE, MLA_V = 16, 128, 64, 128
MLA_Q_LORA, MLA_KV_LORA = 256, 256
ROPE_THETA = 10000.0
ATT_SCALE = (MLA_NOPE + MLA_ROPE) ** -0.5
EXP_CLAMP = 80.0

ADAM_LR, ADAM_B1, ADAM_B2, ADAM_EPS, ADAM_WD, ADAM_STEP = 0.001, 0.9, 0.999, 1e-08, 0.01, 10

V7X_VMEM_BYTES = 64 * 1024 * 1024
VMEM_LIMIT = V7X_VMEM_BYTES - 8 * 1024 * 1024
LANES = 128

PACK_PAD = 16
W_EARLY = (("hgrn_w_q", None, 256), ("hgrn_w_f", None, 256), ("hgrn_w_i", None, 256), ("hgrn_w_g", None, 256),
           ("hgrn_w_o", None, 256))
W_GLA = (("mlp_w_up", 0, 1024), ("mlp_w_down", 0, 1024))
W_MID = (("mla_w_dq", None, 64), ("mla_w_uq", None, 192), ("mla_w_o", None, 512), ("kv_w_dkv", None, 80),
         ("kv_w_uk", None, 128), ("kv_w_uv", None, 128))
W_LAST = (("mlp_w_up", 1, 1024), ("mlp_w_down", 1, 1024))
G_LATE = (("hgrn_w_q", None, 256), ("hgrn_w_f", None, 256), ("hgrn_w_i", None, 256), ("hgrn_w_g", None, 256))
G_ATTN = (("mla_w_o", None, 512), ("mlp_w_up", 1, 1024), ("mlp_w_down", 1, 1024))
G_GLA = (("hgrn_w_o", None, 256), ("mla_w_dq", None, 64), ("mla_w_uq", None, 192), ("kv_w_dkv", None, 80),
         ("kv_w_uk", None, 128), ("kv_w_uv", None, 128), ("mlp_w_up", 0, 1024), ("mlp_w_down", 0, 1024))
PADDED = (W_MID, G_GLA)


def _offsets(layout):
    out, o = {}, 0
    for name, layer, rows in layout:
        out[name, layer] = (o, rows)
        o += rows
    return out, o + (PACK_PAD if layout in PADDED else 0)


W_EARLY_OFF, W_EARLY_ROWS = _offsets(W_EARLY)
W_GLA_OFF, W_GLA_ROWS = _offsets(W_GLA)
W_MID_OFF, W_MID_ROWS = _offsets(W_MID)
W_LAST_OFF, W_LAST_ROWS = _offsets(W_LAST)
G_LATE_OFF, G_LATE_ROWS = _offsets(G_LATE)
G_ATTN_OFF, G_ATTN_ROWS = _offsets(G_ATTN)
G_GLA_OFF, G_GLA_ROWS = _offsets(G_GLA)
assert all(r % 32 == 0 for r in (W_EARLY_ROWS, W_MID_ROWS, W_LAST_ROWS, G_LATE_ROWS, G_ATTN_ROWS, G_GLA_ROWS))

WEIGHTS = ("hgrn_norm", "hgrn_w_q", "hgrn_w_f", "hgrn_w_i", "hgrn_w_g", "hgrn_g_norm", "hgrn_w_o", "hgrn_lb_logits",
           "mla_norm", "mla_w_dq", "mla_q_norm", "mla_w_uq", "mla_w_o", "kv_in_norm", "kv_w_dkv", "kv_norm", "kv_w_uk",
           "kv_w_uv", "mlp_norm", "mlp_w_up", "mlp_w_down", "final_norm")
SMALL = (("hgrn_norm", 0, 1, 1024), ("hgrn_lb_logits", 1, 2, 1024), ("hgrn_g_norm", 3, 1, 128),
         ("mla_norm", 4, 1, 1024), ("mla_q_norm", 5, 1, 256), ("kv_in_norm", 6, 1, 1024), ("kv_norm", 7, 1, 256),
         ("mlp_norm", 8, 2, 1024), ("final_norm", 10, 1, 1024))
SMALL_ROWS = 16


def _pc(body, *, name, out_shape, grid=None, in_specs=None, out_specs=None, scratch=(), sem=None, grid_spec=None,
        aliases=None):
    params = pltpu.CompilerParams(dimension_semantics=sem, vmem_limit_bytes=VMEM_LIMIT)
    if grid_spec is not None:
        return pl.pallas_call(body, name=name, out_shape=out_shape, grid_spec=grid_spec, compiler_params=params,
                              interpret=False)
    kw = {k: v for k, v in (("grid", grid), ("in_specs", in_specs), ("out_specs", out_specs),
                            ("input_output_aliases", aliases)) if v is not None}
    return pl.pallas_call(body, name=name, out_shape=out_shape, scratch_shapes=list(scratch), compiler_params=params,
                          interpret=False, **kw)


def _sds(shape, dtype):
    return jax.ShapeDtypeStruct(tuple(shape), dtype)


def _mm(a, b, *, name, ta=False, tb=False, outs=(f32,), epi=None, extras=(), accs=0, gather=None,
        tm=1024, tn=1024, tk=4096):
    m, k = (a.shape[1], a.shape[0]) if ta else a.shape
    n = b.shape[0] if tb else b.shape[1]
    tm, tn, tk = min(tm, m), min(tn, n), min(tk, k)
    assert m % tm == 0 and n % tn == 0 and k % tk == 0, (name, m, n, k)
    nk = k // tk
    assert accs == 0 or (tn == n and nk == 1), name
    a_spec = pl.BlockSpec((tk, tm), lambda i, j, kk: (kk, i)) if ta else pl.BlockSpec((tm, tk), lambda i, j, kk: (i, kk))
    b_spec = pl.BlockSpec((tn, tk), lambda i, j, kk: (j, kk)) if tb else pl.BlockSpec((tk, tn), lambda i, j, kk: (kk, j))

    def extra_spec(e):
        if e.shape == (m, n):
            return pl.BlockSpec((tm, tn), lambda i, j, kk: (i, j))
        if e.shape[0] == m:
            return pl.BlockSpec((tm, e.shape[1]), lambda i, j, kk: (i, 0))
        return pl.BlockSpec((e.shape[0], tn), lambda i, j, kk: (0, j))

    e_specs = [extra_spec(e) for e in extras]
    dn = (((0 if ta else 1,), (1 if tb else 0,)), ((), ()))
    n_e, n_o = len(extras), len(outs)

    def finish(r, e_refs, o_refs):
        res = epi(r, *[e[...] for e in e_refs]) if epi is not None else (r,)
        for o, v in zip(o_refs[:n_o], res[:n_o]):
            o[...] = v.astype(o.dtype)
        for o, v in zip(o_refs[n_o:], res[n_o:]):
            @pl.when(pl.program_id(0) == 0)
            def _(o=o):
                o[...] = jnp.zeros_like(o)
            o[...] += v

    grid = (m // tm, n // tn, nk)
    n_g = 0 if gather is None else 2

    def body(*refs):
        a_ref, b_ref = refs[0], refs[1]
        e_refs = refs[2:2 + n_e]
        o_refs = refs[2 + n_e + n_g:2 + n_e + n_g + n_o + accs]
        if gather is not None:
            wp_ref, gathered_ref, sems = refs[2 + n_e], refs[2 + n_e + n_g + n_o + accs], refs[-1]
            pid = [pl.program_id(d) for d in range(3)]
            pl.when((pid[0] == 0) & (pid[1] == 0) & (pid[2] == 0))(lambda: _gather_start(wp_ref, gathered_ref, sems))
            pl.when((pid[0] == grid[0] - 1) & (pid[1] == grid[1] - 1) & (pid[2] == grid[2] - 1))(
                lambda: _gather_finish(wp_ref, gathered_ref, sems))
        prod = lax.dot_general(a_ref[...].astype(bf16), b_ref[...].astype(bf16), dn, preferred_element_type=f32)
        if nk == 1:
            finish(prod, e_refs, o_refs)
            return
        acc = refs[2 + n_e + n_g + n_o + accs + n_g // 2]
        kk = pl.program_id(2)

        @pl.when(kk == 0)
        def _():
            acc[...] = jnp.zeros_like(acc)

        acc[...] += prod

        @pl.when(kk == nk - 1)
        def _():
            finish(acc[...], e_refs, o_refs)

    out_specs = ([pl.BlockSpec((tm, tn), lambda i, j, kk: (i, j)) for _ in outs] +
                 [pl.BlockSpec((1, n), lambda i, j, kk: (0, 0))] * accs)
    out_shape = [_sds((m, n), dt) for dt in outs] + [_sds((1, n), f32)] * accs
    scratch = [pltpu.VMEM((tm, tn), f32)] if nk > 1 else []
    if gather is None:
        out = _pc(body, name=name, grid=grid, in_specs=[a_spec, b_spec] + e_specs, out_specs=out_specs,
                  out_shape=out_shape, scratch=scratch,
                  sem=("arbitrary" if accs else "parallel", "parallel", "arbitrary"))(a, b, *extras)
    else:
        out = _pc(body, name=name + "_gather", grid=grid, in_specs=[a_spec, b_spec] + e_specs + [HBM, HBM],
                  out_specs=out_specs + [HBM], out_shape=out_shape + [_sds((4,) + gather.shape, bf16)],
                  aliases={2 + n_e + 1: n_o + accs}, scratch=scratch + [pltpu.SemaphoreType.DMA((2, 6))],
                  sem=("arbitrary",) * 3)(a, b, *extras, gather, _gather_base(gather))
    return out[0] if len(out) == 1 else out


def _wgrad(a, b, name):
    return _mm(a, b, ta=True, name=name, outs=(bf16,))


def _rw(fn, rows, bcast, outs, accs=(), *, name, tr=256):
    t = rows[0].shape[0]
    tr = min(tr, t)
    assert t % tr == 0
    n_r, n_b, n_o, n_a = len(rows), len(bcast), len(outs), len(accs)

    def body(*refs):
        r_refs = refs[:n_r]
        b_refs = refs[n_r:n_r + n_b]
        o_refs = refs[n_r + n_b:n_r + n_b + n_o]
        a_refs = refs[n_r + n_b + n_o:]
        res = fn(*[r[...] for r in r_refs], *[b[...] for b in b_refs])
        for o, v in zip(o_refs, res[:n_o]):
            o[...] = v.astype(o.dtype)
        i = pl.program_id(0)
        for a_ref, v in zip(a_refs, res[n_o:]):
            @pl.when(i == 0)
            def _(a_ref=a_ref):
                a_ref[...] = jnp.zeros_like(a_ref)
            a_ref[...] += v

    in_specs = [pl.BlockSpec((tr, r.shape[1]), lambda i: (i, 0)) for r in rows]
    in_specs += [pl.BlockSpec(b.shape, lambda i: (0, 0)) for b in bcast]
    out_specs = [pl.BlockSpec((tr, w), lambda i: (i, 0)) for w, _ in outs]
    out_specs += [pl.BlockSpec(s, lambda i: (0, 0)) for s in accs]
    out_shape = [_sds((t, w), dt) for w, dt in outs] + [_sds(s, f32) for s in accs]
    res = _pc(body, name=name, grid=(t // tr,), in_specs=in_specs, out_specs=out_specs, out_shape=out_shape,
              sem=("arbitrary",))(*rows, *bcast)
    return res


def _rms(x, gain):
    return x * lax.rsqrt(jnp.mean(x * x, axis=-1, keepdims=True) + EPS) * gain


def _rms_bwd(x, gain, dy):
    _, vjp = jax.vjp(_rms, x, gain)
    return vjp(dy)


def _lower_bound(lbl):
    l0, l1 = lbl[0:1, :], lbl[1:2, :]
    mx = jnp.maximum(l0, l1)
    e0, e1 = jnp.exp(l0 - mx), jnp.exp(l1 - mx)
    return e0 / (e0 + e1)


def _gates(qpre, fpre, lbl):
    lb = _lower_bound(lbl)
    q = jax.nn.silu(qpre)
    forget = lb + (1.0 - lb) * jax.nn.sigmoid(fpre)
    return q, 1.0 - forget, jnp.log(forget)


def _head_norm_gate(o, gpre, gn):
    return _rms(o, gn) * jax.nn.silu(gpre)


def _swap_halves(x):
    w = x.shape[1]
    lane = lax.broadcasted_iota(jnp.int32, x.shape, 1)
    return jnp.where((lane % MLA_ROPE) < MLA_ROPE // 2, pltpu.roll(x, w - MLA_ROPE // 2, 1),
                     pltpu.roll(x, MLA_ROPE // 2, 1))


def _tile_lanes(tab, w):
    return tab if w == tab.shape[1] else jnp.concatenate([tab] * (w // tab.shape[1]), axis=1)


def _rope(x, cos, sgn_sin, sign=1.0):
    w = x.shape[1]
    return x * _tile_lanes(cos, w) + sign * _swap_halves(x) * _tile_lanes(sgn_sin, w)


def _rope_heads(x, cos, sgn_sin, sign, scale):
    parts = []
    for h in range(x.shape[1] // (2 * LANES)):
        parts.append(x[:, 2 * LANES * h:2 * LANES * h + LANES] * scale)
        parts.append(_rope(x[:, 2 * LANES * h + LANES:2 * LANES * (h + 1)], cos, sgn_sin, sign) * scale)
    return jnp.concatenate(parts, axis=1)


def _bd(a, b, ca, cb):
    return lax.dot_general(a.astype(bf16), b.astype(bf16), (((ca,), (cb,)), ((), ())), preferred_element_type=f32)


@jax.custom_vjp
def _dot_nn(a, b):
    return _bd(a, b, 1, 0)


@jax.custom_vjp
def _dot_nt(a, b):
    return _bd(a, b, 1, 1)


@jax.custom_vjp
def _dot_tn(a, b):
    return _bd(a, b, 0, 0)


_dot_nn.defvjp(lambda a, b: (_bd(a, b, 1, 0), (a, b)), lambda r, g: (_bd(g, r[1], 1, 1), _bd(r[0], g, 0, 0)))
_dot_nt.defvjp(lambda a, b: (_bd(a, b, 1, 1), (a, b)), lambda r, g: (_bd(g, r[1], 1, 0), _bd(g, r[0], 0, 0)))
_dot_tn.defvjp(lambda a, b: (_bd(a, b, 0, 0), (a, b)), lambda r, g: (_bd(r[1], g, 1, 1), _bd(r[0], g, 1, 0)))


def _scan_rows(x, reverse):
    n = x.shape[0]
    row = lax.broadcasted_iota(jnp.int32, x.shape, 0)
    s = 1
    while s < n:
        if reverse:
            x = x + jnp.where(row < n - s, pltpu.roll(x, n - s, 0), 0.0)
        else:
            x = x + jnp.where(row >= s, pltpu.roll(x, s, 0), 0.0)
        s *= 2
    return x


@jax.custom_vjp
def _cumsum_rows(g):
    return _scan_rows(g, False)


_cumsum_rows.defvjp(lambda g: (_scan_rows(g, False), None), lambda _, ct: (_scan_rows(ct, True),))

HGRN_PAIRS = HGRN_HEADS // 2
HGRN_PAIR = 2 * HGRN_DK
GLA_STATE = (HGRN_PAIRS, HGRN_PAIR, HGRN_PAIR)


def _gla_consts():
    s = HGRN_SUB
    r = lax.broadcasted_iota(jnp.int32, (HGRN_PAIR, HGRN_PAIR), 0)
    c = lax.broadcasted_iota(jnp.int32, (HGRN_PAIR, HGRN_PAIR), 1)
    pair_mask = (r < HGRN_DK) == (c < HGRN_DK)
    masks = []
    for i in range(HGRN_CHUNK // s):
        n = s * (i + 1)
        row = lax.broadcasted_iota(jnp.int32, (HGRN_HEADS * s, HGRN_HEADS * n), 0)
        col = lax.broadcasted_iota(jnp.int32, (HGRN_HEADS * s, HGRN_HEADS * n), 1)
        col_head = sum((col >= m * n).astype(jnp.int32) for m in range(1, HGRN_HEADS))
        masks.append((col_head == row // s) & (col - col_head * n <= s * i + row % s))
    return pair_mask, masks


def _heads_to_rows(x):
    return jnp.concatenate([x[:, HGRN_DK * h:HGRN_DK * (h + 1)] for h in range(HGRN_HEADS)], axis=0)


def _gla_chunk(consts, dots, q, k, v, g, st):
    pair_mask, masks = consts
    dot_nn, dot_nt, dot_tn = dots
    c, s = HGRN_CHUNK, HGRN_SUB
    b = _cumsum_rows(g)
    b_last = b[c - 1:c, :]
    q_in, k_out = q * jnp.exp(b), k * jnp.exp(b_last - b)
    o_inter, st_new = [], []
    for p in range(HGRN_PAIRS):
        cols = slice(HGRN_PAIR * p, HGRN_PAIR * (p + 1))
        o_inter.append(dot_nt(q_in[:, cols], st[p]))
        st_new.append(st[p] * jnp.exp(b_last[:, cols]) + jnp.where(pair_mask, dot_tn(v[:, cols], k_out[:, cols]), 0.0))
    intra = []
    for i in range(c // s):
        n = s * (i + 1)
        ref = b[s * i - 1:s * i, :] if i else jnp.zeros_like(b_last)
        qt = _heads_to_rows(q[s * i:n] * jnp.exp(b[s * i:n] - ref))
        kt = _heads_to_rows(k[:n] * jnp.exp(jnp.minimum(ref - b[:n], EXP_CLAMP)))
        sc = jnp.where(masks[i], dot_nt(qt, kt), 0.0)
        oi = dot_nn(sc, _heads_to_rows(v[:n]))
        intra.append(jnp.concatenate([oi[s * h:s * (h + 1)] for h in range(HGRN_HEADS)], axis=1))
    return jnp.concatenate(o_inter, axis=1) + jnp.concatenate(intra, axis=0), st_new


_PLAIN_DOTS = (lambda a, b: _bd(a, b, 1, 0), lambda a, b: _bd(a, b, 1, 1), lambda a, b: _bd(a, b, 0, 0))
_VJP_DOTS = (_dot_nn, _dot_nt, _dot_tn)


def _hgrn_mix(consts, dots, qpre, fpre, v, gpre, lbl, gn, st):
    q, k, g = _gates(qpre, fpre, lbl)
    o, st_new = _gla_chunk(consts, dots, q, k, v, g, st)
    y = [_head_norm_gate(o[:, HGRN_DK * h:HGRN_DK * (h + 1)], gpre[:, HGRN_DK * h:HGRN_DK * (h + 1)], gn)
         for h in range(HGRN_HEADS)]
    return jnp.concatenate(y, axis=1), st_new


def _gla_fwd(p4, lbl, gn, gather=None):
    t = p4.shape[0]
    nc = t // HGRN_CHUNK

    def body(q_ref, k_ref, v_ref, g_ref, lbl_ref, gn_ref, *rest):
        if gather is None:
            o_ref, s_ref, st = rest
        else:
            wp_ref, _, o_ref, s_ref, gathered_ref, st, sems = rest

        @pl.when(pl.program_id(0) == 0)
        def _():
            st[...] = jnp.zeros_like(st)
            if gather is not None:
                _gather_start(wp_ref, gathered_ref, sems)

        if gather is not None:
            @pl.when(pl.program_id(0) == nc - 1)
            def _():
                _gather_finish(wp_ref, gathered_ref, sems)

        s_in = [st[p] for p in range(HGRN_PAIRS)]
        y, st_new = _hgrn_mix(_gla_consts(), _PLAIN_DOTS, q_ref[...], k_ref[...], v_ref[...], g_ref[...],
                              lbl_ref[...], gn_ref[...], s_in)
        o_ref[...] = y.astype(o_ref.dtype)
        for p in range(HGRN_PAIRS):
            s_ref[0, p] = s_in[p]
            st[p] = st_new[p]

    blk = lambda off: pl.BlockSpec((HGRN_CHUNK, D_MODEL), lambda c: (c, off))
    whole = lambda a: pl.BlockSpec(a.shape, lambda c: (0, 0))
    state_shape = GLA_STATE
    in_specs = [blk(0), blk(1), blk(2), blk(3), whole(lbl), whole(gn)]
    out_specs = [blk(0), pl.BlockSpec((1,) + state_shape, lambda c: (c, 0, 0, 0))]
    out_shape = [_sds((t, D_MODEL), bf16), _sds((nc,) + state_shape, f32)]
    scratch = [pltpu.VMEM(state_shape, f32)]
    if gather is None:
        return _pc(body, name="gla_fwd", grid=(nc,), in_specs=in_specs, out_specs=out_specs, out_shape=out_shape,
                   scratch=scratch, sem=("arbitrary",))(p4, p4, p4, p4, lbl, gn)
    return _pc(body, name="gla_fwd_gather", grid=(nc,), in_specs=in_specs + [HBM, HBM], out_specs=out_specs + [HBM],
               out_shape=out_shape + [_sds((4,) + gather.shape, bf16)], aliases={7: 2},
               scratch=scratch + [pltpu.SemaphoreType.DMA((2, 6))], sem=("arbitrary",))(
                   p4, p4, p4, p4, lbl, gn, gather, _gather_base(gather))


def _gla_bwd(p4, lbl, gn, states, dy, exchange=None):
    t = p4.shape[0]
    nc = t // HGRN_CHUNK

    def body(q_ref, k_ref, v_ref, g_ref, lbl_ref, gn_ref, s_ref, dy_ref, *rest):
        if exchange is None:
            dp_ref, dlbl_ref, dgn_ref, dst = rest
        else:
            sb_ref, dp_ref, dlbl_ref, dgn_ref, recv_ref, dst, sems = rest

        @pl.when(pl.program_id(0) == 0)
        def _():
            dst[...] = jnp.zeros_like(dst)
            dlbl_ref[...] = jnp.zeros_like(dlbl_ref)
            dgn_ref[...] = jnp.zeros_like(dgn_ref)
            if exchange is not None:
                _chips_start(sb_ref, recv_ref, sems)

        if exchange is not None:
            @pl.when(pl.program_id(0) == nc - 1)
            def _():
                _chips_finish(sb_ref, recv_ref, sems)

        consts = _gla_consts()
        fn = lambda *args: _hgrn_mix(consts, _VJP_DOTS, *args)
        pairs = range(HGRN_PAIRS)
        _, vjp = jax.vjp(fn, q_ref[...], k_ref[...], v_ref[...], g_ref[...], lbl_ref[...], gn_ref[...],
                         [s_ref[0, p] for p in pairs])
        *d_proj, dlbl, dgn, ds = vjp((dy_ref[...], [dst[p] for p in pairs]))
        for i, d in enumerate(d_proj):
            dp_ref[:, D_MODEL * i:D_MODEL * (i + 1)] = d.astype(dp_ref.dtype)
        dlbl_ref[...] += dlbl
        dgn_ref[...] += dgn
        for p in pairs:
            dst[p] = ds[p]

    blk = lambda off: pl.BlockSpec((HGRN_CHUNK, D_MODEL), lambda c: (nc - 1 - c, off))
    whole = lambda a: pl.BlockSpec(a.shape, lambda c: (0, 0))
    state_shape = GLA_STATE
    in_specs = [blk(0), blk(1), blk(2), blk(3), whole(lbl), whole(gn),
                pl.BlockSpec((1,) + state_shape, lambda c: (nc - 1 - c, 0, 0, 0)), blk(0)]
    out_specs = [pl.BlockSpec((HGRN_CHUNK, 4 * D_MODEL), lambda c: (nc - 1 - c, 0)), whole(lbl), whole(gn)]
    out_shape = [_sds((t, 4 * D_MODEL), bf16), _sds(lbl.shape, f32), _sds(gn.shape, f32)]
    scratch = [pltpu.VMEM(state_shape, f32)]
    if exchange is None:
        return _pc(body, name="gla_bwd", grid=(nc,), in_specs=in_specs, out_specs=out_specs, out_shape=out_shape,
                   scratch=scratch, sem=("arbitrary",))(p4, p4, p4, p4, lbl, gn, states, dy)
    return _pc(body, name="gla_bwd_exchange", grid=(nc,), in_specs=in_specs + [HBM], out_specs=out_specs + [HBM],
               out_shape=out_shape + [_sds((3,) + exchange.shape[1:], bf16)],
               scratch=scratch + [pltpu.SemaphoreType.DMA((2, 3))], sem=("arbitrary",))(
                   p4, p4, p4, p4, lbl, gn, states, dy, exchange)


ATT_FWD_TQ, ATT_FWD_TK = 1024, 1024
ATT_BWD_TQ, ATT_BWD_TK = 1024, 512
ATT_QK = 2 * LANES
NEG = -1e30


def _pair_masks(shape):
    lane = lax.broadcasted_iota(jnp.int32, shape, 1)
    return lane < MLA_ROPE, lane >= MLA_ROPE


def _causal(shape, row0, col0):
    row = row0 + lax.broadcasted_iota(jnp.int32, shape, 0)
    col = col0 + lax.broadcasted_iota(jnp.int32, shape, 1)
    return col <= row


def _qk_cols(e):
    return slice(ATT_QK * e, ATT_QK * (e + 1))


def _v_cols(e):
    return slice(MLA_V * e, MLA_V * (e + 1))


def _first_last_step(n0, n1):
    p, i = pl.program_id(0), pl.program_id(1)
    return (p == 0) & (i == 0), (p == n0 - 1) & (i == n1 - 1)


def _attn_fwd(qc, kc, v, gather=None):
    t = qc.shape[0]
    tq, tk = min(ATT_FWD_TQ, t), min(ATT_FWD_TK, t)
    assert tq == tk, "the diagonal block is split in the body on the premise of square blocks"
    npair = MLA_HEADS // 2

    def body(q_ref, k_ref, v_ref, *rest):
        if gather is None:
            o_ref, lse_ref = rest
        else:
            wp_ref, _, o_ref, lse_ref, gathered_ref, sems = rest
            first, last = _first_last_step(npair, t // tq)
            pl.when(first)(lambda: _gather_start(wp_ref, gathered_ref, sems))
            pl.when(last)(lambda: _gather_finish(wp_ref, gathered_ref, sems))
        i = pl.program_id(1)
        q = [q_ref[:, _qk_cols(e)] for e in range(2)]

        def update(state, q_rows, e, ks, ok):
            m, l, acc = state
            s = _bd(q_rows, k_ref[ks, _qk_cols(e)], 1, 1)
            if ok is not None:
                s = jnp.where(ok, s, NEG)
            m_new = jnp.maximum(m, jnp.max(s, axis=-1, keepdims=True))
            p = jnp.exp(s - m_new)
            alpha = jnp.exp(m - m_new)
            return m_new, alpha * l + jnp.sum(p, axis=-1, keepdims=True), alpha * acc + _bd(p, v_ref[ks, _v_cols(e)], 1, 0)

        def step(j, carry):
            ks = pl.ds(pl.multiple_of(j * tk, tk), tk)
            return tuple(update(carry[e], q[e], e, ks, None) for e in range(2))

        one = (jnp.full((tq, 1), NEG, f32), jnp.zeros((tq, 1), f32), jnp.zeros((tq, MLA_V), f32))
        carry = lax.fori_loop(0, i, step, (one, one))
        half = tq // 2
        outs, lses = [], []
        for e in range(2):
            top = update(tuple(a[:half] for a in carry[e]), q[e][:half], e,
                         pl.ds(pl.multiple_of(i * tk, tk), half), _causal((half, half), 0, 0))
            bottom = update(tuple(a[half:] for a in carry[e]), q[e][half:], e,
                            pl.ds(pl.multiple_of(i * tk, tk), tk), _causal((half, tk), half, 0))
            m, l, acc = (jnp.concatenate(ab, axis=0) for ab in zip(top, bottom))
            outs.append(acc / l)
            lses.append(m + jnp.log(l))
        o_ref[...] = jnp.concatenate(outs, axis=1).astype(o_ref.dtype)
        lo, _ = _pair_masks((tq, LANES))
        lse_ref[...] = jnp.where(lo, *lses)

    in_specs = [pl.BlockSpec((tq, 2 * ATT_QK), lambda p, i: (i, p)),
                pl.BlockSpec((t, 2 * ATT_QK), lambda p, i: (0, p)),
                pl.BlockSpec((t, 2 * MLA_V), lambda p, i: (0, p))]
    out_specs = [pl.BlockSpec((tq, 2 * MLA_V), lambda p, i: (i, p)), pl.BlockSpec((tq, LANES), lambda p, i: (i, p))]
    out_shape = [_sds((t, MLA_HEADS * MLA_V), bf16), _sds((t, npair * LANES), f32)]
    if gather is None:
        return _pc(body, name="attn_fwd", grid=(npair, t // tq), in_specs=in_specs, out_specs=out_specs,
                   out_shape=out_shape, sem=("arbitrary", "arbitrary"))(qc, kc, v)
    return _pc(body, name="attn_fwd_gather", grid=(npair, t // tq), in_specs=in_specs + [HBM, HBM],
               out_specs=out_specs + [HBM], out_shape=out_shape + [_sds((4,) + gather.shape, bf16)], aliases={4: 2},
               scratch=[pltpu.SemaphoreType.DMA((2, 6))], sem=("arbitrary", "arbitrary"))(
                   qc, kc, v, gather, _gather_base(gather))


def _attn_bwd(qc, kc, v, do, lse, delta, exchange=None):
    t = qc.shape[0]
    tq, tk = min(ATT_BWD_TQ, t), min(ATT_BWD_TK, t)
    npair = MLA_HEADS // 2
    nq = t // tq
    sub = tq // tk
    assert sub * tk == tq

    def body(q_ref, do_ref, lse_ref, dl_ref, k_ref, v_ref, *rest):
        if exchange is None:
            dq_ref, dk_ref, dv_ref = rest
        else:
            sb_ref, dq_ref, dk_ref, dv_ref, recv_ref, sems = rest
            first, last = _first_last_step(npair, t // tk)
            pl.when(first)(lambda: _chips_start(sb_ref, recv_ref, sems))
            pl.when(last)(lambda: _chips_finish(sb_ref, recv_ref, sems))
        j = pl.program_id(1)

        @pl.when(j == 0)
        def _():
            dq_ref[...] = jnp.zeros_like(dq_ref)

        k = [k_ref[:, _qk_cols(e)] for e in range(2)]
        vv = [v_ref[:, _v_cols(e)] for e in range(2)]

        def rows_step(carry, row0, rows, masked):
            qs = pl.ds(pl.multiple_of(row0, rows), rows)
            ok = _causal((rows, tk), row0, j * tk) if masked else None
            lse2, dl2 = lse_ref[qs, :], dl_ref[qs, :]
            new = []
            for e in range(2):
                dk, dv = carry[e]
                q_e, do_e = q_ref[qs, _qk_cols(e)], do_ref[qs, _v_cols(e)]
                p = jnp.exp(_bd(q_e, k[e], 1, 1) - lse2[:, MLA_ROPE * e:MLA_ROPE * e + 1])
                if masked:
                    p = jnp.where(ok, p, 0.0)
                dv = dv + _bd(p, do_e, 0, 0)
                dp = _bd(do_e, vv[e], 1, 1)
                ds = (p * (dp - dl2[:, MLA_ROPE * e:MLA_ROPE * e + 1])).astype(bf16)
                dk = dk + _bd(ds, q_e, 0, 0)
                dq_ref[qs, _qk_cols(e)] += _bd(ds, k[e], 1, 0)
                new.append((dk, dv))
            return tuple(new)

        one = (jnp.zeros((tk, ATT_QK), f32), jnp.zeros((tk, MLA_V), f32))
        i0 = (j * tk) // tq
        j_local = j - i0 * sub
        carry = (one, one)
        for r in range(sub):
            run = functools.partial(rows_step, row0=i0 * tq + r * tk, rows=tk, masked=True)
            carry = run(carry) if r == sub - 1 else lax.cond(r >= j_local, run, lambda c: c, carry)
        carry = lax.fori_loop(i0 + 1, nq, lambda i, c: rows_step(c, i * tq, tq, False), carry)
        for e in range(2):
            dk_ref[:, _qk_cols(e)] = carry[e][0].astype(dk_ref.dtype)
            dv_ref[:, _v_cols(e)] = carry[e][1].astype(dv_ref.dtype)

    res = lambda w: pl.BlockSpec((t, w), lambda p, j: (0, p))
    blk = lambda w: pl.BlockSpec((tk, w), lambda p, j: (j, p))
    in_specs = [res(2 * ATT_QK), res(2 * MLA_V), res(LANES), res(LANES), blk(2 * ATT_QK), blk(2 * MLA_V)]
    out_specs = [res(2 * ATT_QK), blk(2 * ATT_QK), blk(2 * MLA_V)]
    out_shape = [_sds((t, MLA_HEADS * ATT_QK), f32), _sds((t, MLA_HEADS * ATT_QK), bf16), _sds((t, MLA_HEADS * MLA_V), bf16)]
    if exchange is None:
        return _pc(body, name="attn_bwd", grid=(npair, t // tk), in_specs=in_specs, out_specs=out_specs,
                   out_shape=out_shape, sem=("arbitrary", "arbitrary"))(qc, do, lse, delta, kc, v)
    return _pc(body, name="attn_bwd_exchange", grid=(npair, t // tk), in_specs=in_specs + [HBM],
               out_specs=out_specs + [HBM], out_shape=out_shape + [_sds((3,) + exchange.shape[1:], bf16)],
               scratch=[pltpu.SemaphoreType.DMA((2, 3))], sem=("arbitrary", "arbitrary"))(
                   qc, do, lse, delta, kc, v, exchange)


def _rope_tables(t):
    half = MLA_ROPE // 2
    inv_freq = ROPE_THETA ** (-jnp.arange(half, dtype=f32) / half)
    ang = jnp.arange(t, dtype=f32)[:, None] * inv_freq[None, :]
    cos, sin = jnp.cos(ang), jnp.sin(ang)
    return jnp.concatenate([cos, cos] * 2, axis=1), jnp.concatenate([-sin, sin] * 2, axis=1)


def _relu2_epi(u):
    r = jnp.maximum(u, 0.0)
    return u, r * r


def _add_epi(r, res):
    return (r + res,)


def _drelu2_epi(da, u):
    return (da * 2.0 * jnp.maximum(u.astype(f32), 0.0),)


ROWWISE_EPI_TM = 512


def _residual_norms_epi(r, res, *gains):
    h = r + res
    return (h, *[_rms(h, g) for g in gains])


def _residual_out(a, w, h, gains, name):
    res = _mm(a, w, name=name, outs=(f32,) + (bf16,) * len(gains), epi=_residual_norms_epi, extras=(h, *gains),
              tm=ROWWISE_EPI_TM)
    return res if gains else [res]


def _dnorm_epi(dy, x, dres, gain):
    dx, dg = _rms_bwd(x, gain, dy)
    return dx + dres, dx + dres, dg


def _mlp_fwd(h, xm, w_up, w_down, tag, next_gains, gather=None):
    u, a, *gathered = _mm(xm, w_up, name=f"mlp{tag}_up", outs=(bf16, bf16), epi=_relu2_epi, gather=gather)
    h_out, *normed = _residual_out(a, w_down, h, next_gains, f"mlp{tag}_down")
    return h_out, normed, (xm, u, a), gathered


def _mlp_bwd(dh, dh16, h, gain, w_up, w_down, saved, tag):
    xm, u, a = saved
    du = _mm(dh16, w_down, tb=True, name=f"mlp{tag}_dact", outs=(bf16,), epi=_drelu2_epi, extras=(u,))
    d_down = _wgrad(a, dh16, f"mlp{tag}_dwdown")
    d_up = _wgrad(xm, du, f"mlp{tag}_dwup")
    dh_in, dh_in16, d_gain = _mm(du, w_up, tb=True, name=f"mlp{tag}_dxm", outs=(f32, bf16), accs=1, epi=_dnorm_epi,
                                 extras=(h, dh, gain), tm=ROWWISE_EPI_TM)
    return dh_in, dh_in16, d_gain, d_up, d_down


def _local_step(x, target, w, comm=None):
    w = dict(w)
    t = x.shape[0]
    cos, sgn_sin = _rope_tables(t)
    grads = {}

    xn0 = _rw(lambda xx, g: (_rms(xx, g),), [x], [w["hgrn_norm"]], [(D_MODEL, bf16)], name="hgrn_norm")[0]
    p4 = _mm(xn0, w["hgrn_w4"], name="hgrn_proj", tn=2048)

    if comm is None:
        y, states = _gla_fwd(p4, w["hgrn_lb_logits"], w["hgrn_g_norm"])
    else:
        y, states, gathered = _gla_fwd(p4, w["hgrn_lb_logits"], w["hgrn_g_norm"], gather=comm.shard["gla"])
        w.update(comm.unpack["gla"](gathered))
    h1, xm0 = _residual_out(y, w["hgrn_w_o"], x, [w["mlp_norm"][0:1]], "hgrn_out")
    h2, (hk, xn1), mlp0, gathered = _mlp_fwd(h1, xm0, w["mlp_w_up", 0], w["mlp_w_down", 0], 0,
                                             [w["kv_in_norm"], w["mla_norm"]],
                                             gather=None if comm is None else comm.shard["mlp0_up"])
    if comm is not None:
        w.update(comm.unpack["mlp0_up"](gathered[0]))

    ckr = _mm(hk, w["kv_w_dkv"], name="kv_down")

    def ckv_fn(c, cs, sn, g):
        kr = _rope(c[:, MLA_KV_LORA:], cs, sn)
        return _rms(c[:, :MLA_KV_LORA], g), jnp.concatenate([jnp.zeros_like(kr), kr], axis=1)

    c_kv, kr_head = _rw(ckv_fn, [ckr, cos, sgn_sin], [w["kv_norm"]], [(MLA_KV_LORA, bf16), (ATT_QK, f32)],
                        name="kv_norm_rope")
    kc = _mm(c_kv, w["kv_w_kcat"], name="kv_up_k", outs=(bf16,), extras=(kr_head,),
             epi=lambda r, kr: (r + _tile_lanes(kr, r.shape[1]),))
    v_att = _mm(c_kv, w["kv_w_uv"], name="kv_up_v", outs=(bf16,))
    cq0 = _mm(xn1, w["mla_w_dq"], name="q_down")
    c_q = _rw(lambda c, g: (_rms(c, g),), [cq0], [w["mla_q_norm"]], [(MLA_Q_LORA, bf16)], name="q_norm")[0]
    qc = _mm(c_q, w["mla_w_qcat"], name="q_up", outs=(bf16,), extras=(cos, sgn_sin),
             epi=lambda r, cs, sn: (_rope_heads(r, cs, sn, 1.0, ATT_SCALE),))
    if comm is None:
        o_att, lse = _attn_fwd(qc, kc, v_att)
    else:
        o_att, lse, gathered = _attn_fwd(qc, kc, v_att, gather=comm.shard["attn"])
        w.update(comm.unpack["attn"](gathered))
    h3, xm1 = _residual_out(o_att, w["mla_w_o"], h2, [w["mlp_norm"][1:2]], "mla_out")
    u1, a1 = _mm(xm1, w["mlp_w_up", 1], name="mlp1_up", outs=(bf16, bf16), epi=_relu2_epi)
    mlp1 = (xm1, u1, a1)

    def loss_epi(r, res, tgt, gain):
        def f(a, b):
            e = _rms(a, b) - tgt
            return 0.5 * jnp.sum(jnp.sum(e * e, axis=-1, keepdims=True) / D_MODEL, axis=0, keepdims=True)
        val, vjp = jax.vjp(f, r + res, gain)
        dh, dg = vjp(jnp.ones((1, 1), f32))
        return dh, dh, jnp.broadcast_to(val, (1, D_MODEL)), dg

    dh4, dh4_16, loss_acc, grads["final_norm"] = _mm(
        a1, w["mlp_w_down", 1], name="mlp1_down_loss", outs=(f32, bf16), accs=2, epi=loss_epi,
        extras=(h3, target, w["final_norm"]), tm=ROWWISE_EPI_TM)
    loss = loss_acc[0, 0]

    dh3, dh3_16, g_n1, grads["mlp_w_up", 1], grads["mlp_w_down", 1] = _mlp_bwd(
        dh4, dh4_16, h3, w["mlp_norm"][1:2], w["mlp_w_up", 1], w["mlp_w_down", 1], mlp1, 1)
    do_att = _mm(dh3_16, w["mla_w_o"], tb=True, name="mla_dout", outs=(bf16,))
    grads["mla_w_o"] = _wgrad(o_att, dh3_16, "mla_dwo")

    def delta_fn(a, b):
        prod = a.astype(f32) * b.astype(f32)
        outs = []
        for p in range(MLA_HEADS // 2):
            d0 = jnp.sum(prod[:, 2 * p * LANES:(2 * p + 1) * LANES], axis=-1, keepdims=True)
            d1 = jnp.sum(prod[:, (2 * p + 1) * LANES:(2 * p + 2) * LANES], axis=-1, keepdims=True)
            lo, _ = _pair_masks((a.shape[0], LANES))
            outs.append(jnp.where(lo, d0, d1))
        return (jnp.concatenate(outs, axis=1),)

    delta = _rw(delta_fn, [do_att, o_att], [], [(MLA_HEADS // 2 * LANES, f32)], name="attn_delta")[0]
    if comm is None:
        dqc, dkc, dv = _attn_bwd(qc, kc, v_att, do_att, lse, delta)
    else:
        dqc, dkc, dv, comm.received["attn"] = _attn_bwd(qc, kc, v_att, do_att, lse, delta,
                                                        exchange=comm.reduce(grads, "attn"))
    dqf = _rw(lambda a, cs, sn: (_rope_heads(a, cs, sn, -1.0, ATT_SCALE),), [dqc, cos, sgn_sin], [],
              [(MLA_HEADS * ATT_QK, bf16)], name="dq_rope")[0]
    dc_q = _mm(dqf, w["mla_w_qcat"], tb=True, name="q_up_dx")
    grads["mla_w_qcat"] = _wgrad(c_q, dqf, "q_up_dw")

    def dqn_fn(c, dy, g):
        return _rms_bwd(c, g, dy)

    dcq0, grads["mla_q_norm"] = _rw(dqn_fn, [cq0, dc_q], [w["mla_q_norm"]], [(MLA_Q_LORA, bf16)], [(1, MLA_Q_LORA)],
                                    name="q_dnorm")
    dxn1 = _mm(dcq0, w["mla_w_dq"], tb=True, name="q_down_dx")
    grads["mla_w_dq"] = _wgrad(xn1, dcq0, "q_down_dw")

    dc_kv = _mm(dkc, w["kv_w_kcat"], tb=True, name="kv_up_dx_k")
    dc_kv = _mm(dv, w["kv_w_uv"], tb=True, name="kv_up_dx_v", epi=_add_epi, extras=(dc_kv,))
    grads["kv_w_kcat"] = _wgrad(c_kv, dkc, "kv_up_dw_k")
    grads["kv_w_uv"] = _wgrad(c_kv, dv, "kv_up_dw_v")

    def dckr_fn(c, dc, dk_heads, cs, sn, g):
        tot = dk_heads[:, LANES:ATT_QK].astype(f32)
        for h in range(1, MLA_HEADS):
            tot = tot + dk_heads[:, ATT_QK * h + LANES:ATT_QK * (h + 1)].astype(f32)
        lo, _ = _pair_masks(tot.shape)
        dkr = jnp.where(lo, _rope(tot, cs, sn, -1.0), 0.0)
        dcc, dg = _rms_bwd(c[:, :MLA_KV_LORA], g, dc)
        return jnp.concatenate([dcc, dkr], axis=1), dg

    dckr, grads["kv_norm"] = _rw(dckr_fn, [ckr, dc_kv, dkc, cos, sgn_sin], [w["kv_norm"]],
                                 [(MLA_KV_LORA + LANES, bf16)], [(1, MLA_KV_LORA)], name="kv_dnorm_rope")
    dhk = _mm(dckr, w["kv_w_dkv"], tb=True, name="kv_down_dx")
    grads["kv_w_dkv"] = _wgrad(hk, dckr, "kv_down_dw")

    def dh2_fn(hh, d1, d2, dres, g1, g2):
        a, ga = _rms_bwd(hh, g1, d1)
        b, gb = _rms_bwd(hh, g2, d2)
        return a + b + dres, a + b + dres, ga, gb

    dh2, dh2_16, grads["kv_in_norm"], grads["mla_norm"] = _rw(
        dh2_fn, [h2, dhk, dxn1, dh3], [w["kv_in_norm"], w["mla_norm"]], [(D_MODEL, f32), (D_MODEL, bf16)],
        [(1, D_MODEL)] * 2, name="kv_mla_dnorm")

    dh1, dh1_16, g_n0, grads["mlp_w_up", 0], grads["mlp_w_down", 0] = _mlp_bwd(
        dh2, dh2_16, h1, w["mlp_norm"][0:1], w["mlp_w_up", 0], w["mlp_w_down", 0], mlp0, 0)
    grads["mlp_norm"] = jnp.concatenate([g_n0, g_n1], axis=0)
    dy = _mm(dh1_16, w["hgrn_w_o"], tb=True, name="hgrn_dout")
    grads["hgrn_w_o"] = _wgrad(y, dh1_16, "hgrn_dwo")

    gla_args = (p4, w["hgrn_lb_logits"], w["hgrn_g_norm"], states, dy)
    if comm is None:
        dp4, grads["hgrn_lb_logits"], grads["hgrn_g_norm"] = _gla_bwd(*gla_args)
    else:
        dp4, grads["hgrn_lb_logits"], grads["hgrn_g_norm"], comm.received["gla"] = _gla_bwd(
            *gla_args, exchange=comm.reduce(grads, "gla"))
    grads["hgrn_w4"] = _mm(xn0, dp4, ta=True, name="hgrn_proj_dw")
    grad_x, grads["hgrn_norm"] = _mm(dp4, w["hgrn_w4"], tb=True, name="hgrn_proj_dx", outs=(f32,), accs=1,
                                     epi=lambda *args: _dnorm_epi(*args)[1:], extras=(x, dh1, w["hgrn_norm"]),
                                     tm=ROWWISE_EPI_TM)
    return loss, grad_x, grads


HBM = pl.BlockSpec(memory_space=pltpu.HBM)


def _me():
    return lax.axis_index("x"), lax.axis_index("y"), lax.axis_index("c")


def _flip(x, y, f):
    return (1 - x if f & 1 else x), (1 - y if f & 2 else y)


def _rcopy(src, dst, sems, k, dev):
    return pltpu.make_async_remote_copy(src_ref=src, dst_ref=dst, send_sem=sems.at[0, k], recv_sem=sems.at[1, k],
                                        device_id=dev, device_id_type=MESH)


def _my_half(rows, c, mine=True):
    half = rows // 2
    return pl.ds(pl.multiple_of((c if mine else 1 - c) * half, 16), half)


def _gather_start(wp_ref, out_ref, sems):
    x, y, c = _me()
    half = _my_half(wp_ref.shape[0], c)
    for f in (1, 2, 3):
        px, py = _flip(x, y, f)
        _rcopy(wp_ref.at[half], out_ref.at[2 * x + y, half], sems, f - 1, (px, py, c)).start()


def _gather_finish(wp_ref, out_ref, sems):
    x, y, c = _me()
    half, other = _my_half(wp_ref.shape[0], c), _my_half(wp_ref.shape[0], c, mine=False)
    sends = []
    for f in (1, 2, 3):
        px, py = _flip(x, y, f)
        landed = out_ref.at[2 * px + py, half]
        _rcopy(landed, landed, sems, f - 1, (px, py, c)).wait_recv()
        sends.append(_rcopy(landed, landed, sems, 2 + f, (x, y, 1 - c)))
        sends[-1].start()
    for f in (1, 2, 3):
        px, py = _flip(x, y, f)
        theirs = out_ref.at[2 * px + py, other]
        _rcopy(theirs, theirs, sems, 2 + f, (x, y, 1 - c)).wait_recv()
        sends.append(_rcopy(wp_ref.at[half], out_ref.at[2 * x + y, half], sems, f - 1, (px, py, c)))
    for cp in sends:
        cp.wait_send()


def _gather_base(wp):
    return jnp.broadcast_to(wp[None], (4,) + wp.shape)


def _all_gather_weights(wp, sv):
    def body(wp_ref, sv_ref, base_ref, out_ref, svs_ref, sems, local_sem):
        x, y, c = _me()
        mine = pltpu.make_async_copy(sv_ref, svs_ref.at[2 * x + y], local_sem)
        mine.start()
        _gather_start(wp_ref, out_ref, sems)
        small = []
        for f in (1, 2, 3):
            px, py = _flip(x, y, f)
            small.append(_rcopy(sv_ref, svs_ref.at[2 * x + y], sems, 5 + f, (px, py, c)))
            small[-1].start()
        _gather_finish(wp_ref, out_ref, sems)
        for f in (1, 2, 3):
            px, py = _flip(x, y, f)
            _rcopy(sv_ref, svs_ref.at[2 * px + py], sems, 5 + f, (px, py, c)).wait_recv()
        for cp in small:
            cp.wait_send()
        mine.wait()

    return _pc(body, name="weights_all_gather", in_specs=[HBM, HBM, HBM], out_specs=[HBM, HBM],
               out_shape=[_sds((4,) + wp.shape, bf16), _sds((4, 8, 256), f32)], aliases={2: 0},
               scratch=[pltpu.SemaphoreType.DMA((2, 9)), pltpu.SemaphoreType.DMA])(wp, sv, _gather_base(wp))


def _send_half_to_sibling(gp, name):
    rows = gp.shape[1]

    def body(gp_ref, out_ref, sems):
        x, y, c = _me()
        cp = _rcopy(gp_ref.at[:, _my_half(rows, c, mine=False)], out_ref, sems, 0, (x, y, 1 - c))
        cp.start()
        cp.wait()

    return _pc(body, name=name, in_specs=[HBM], out_specs=HBM, out_shape=_sds((4, rows // 2, D_MODEL), gp.dtype),
               scratch=[pltpu.SemaphoreType.DMA((2, 1))])(gp)


def _chips_start(sb_ref, out_ref, sems):
    x, y, c = _me()
    for f in (1, 2, 3):
        px, py = _flip(x, y, f)
        _rcopy(sb_ref.at[2 * px + py], out_ref.at[f - 1], sems, f - 1, (px, py, c)).start()


def _chips_finish(sb_ref, out_ref, sems):
    x, y, c = _me()
    for f in (1, 2, 3):
        _rcopy(sb_ref.at[0], out_ref.at[f - 1], sems, f - 1, (x, y, c)).wait_recv()
    for f in (1, 2, 3):
        px, py = _flip(x, y, f)
        _rcopy(sb_ref.at[2 * px + py], out_ref.at[f - 1], sems, f - 1, (px, py, c)).wait_send()


def _exchange_chips(sb, small):
    def body(sb_ref, small_ref, out_ref, smalls_ref, sems, local_sem):
        x, y, c = _me()
        me = 4 * x + 2 * y + c
        mine = pltpu.make_async_copy(small_ref, smalls_ref.at[me], local_sem)
        mine.start()
        _chips_start(sb_ref, out_ref, sems)
        sends = []
        for f in range(1, 8):
            px, py = _flip(x, y, f)
            pc = 1 - c if f & 4 else c
            sends.append(_rcopy(small_ref, smalls_ref.at[me], sems, 2 + f, (px, py, pc)))
            sends[-1].start()
        _chips_finish(sb_ref, out_ref, sems)
        for f in range(1, 8):
            px, py = _flip(x, y, f)
            pc = 1 - c if f & 4 else c
            _rcopy(small_ref, smalls_ref.at[4 * px + 2 * py + pc], sems, 2 + f, (x, y, c)).wait_recv()
        for cp in sends:
            cp.wait_send()
        mine.wait()

    return _pc(body, name="grads_exchange_chips", in_specs=[HBM, HBM], out_specs=[HBM, HBM],
               out_shape=[_sds((3,) + sb.shape[1:], bf16), _sds((8, SMALL_ROWS, D_MODEL), f32)],
               scratch=[pltpu.SemaphoreType.DMA((2, 10)), pltpu.SemaphoreType.DMA])(sb, small)


def _exchange_halves(tot, name):
    rows = tot.shape[0]

    def body(tot_ref, out_ref, sems):
        x, y, c = _me()
        half = _my_half(rows, c)
        cp = _rcopy(tot_ref.at[half], out_ref.at[half], sems, 0, (x, y, 1 - c))
        cp.start()
        cp.wait()

    return _pc(body, name=name, in_specs=[HBM], out_specs=HBM, out_shape=_sds((rows, D_MODEL), f32),
               aliases={0: 0}, scratch=[pltpu.SemaphoreType.DMA((2, 1))])(tot)


def _sum_rows(half):
    return max(r for r in range(16, 513, 16) if half % r == 0)


def _sum_over_cores(gp, recv, cq, name):
    half = recv.shape[1]
    tr = _sum_rows(half)
    nb = half // tr

    def body(cq_ref, g_ref, r_ref, o32_ref, o16_ref):
        s = g_ref[...].astype(f32) + r_ref[...].astype(f32)
        o32_ref[...] = s
        o16_ref[...] = s.astype(bf16)

    spec = pl.BlockSpec((1, tr, D_MODEL), lambda b, i, cq_ref: (b, i, 0))
    gs = pltpu.PrefetchScalarGridSpec(
        num_scalar_prefetch=1, grid=(4, nb),
        in_specs=[pl.BlockSpec((1, tr, D_MODEL), lambda b, i, cq_ref: (b, cq_ref[0] * nb + i, 0)), spec],
        out_specs=[spec, spec])
    return _pc(body, name=name, grid_spec=gs, sem=("arbitrary", "arbitrary"),
               out_shape=[_sds((4, half, D_MODEL), f32), _sds((4, half, D_MODEL), bf16)])(cq, gp, recv)


def _sum_over_chips(s32, recv, cq, name):
    half = recv.shape[1]
    tr = _sum_rows(half)
    nb = half // tr

    def body(cq_ref, own_ref, r_ref, o_ref):
        o_ref[...] = ((own_ref[0] + r_ref[0].astype(f32)) + r_ref[1].astype(f32)) + r_ref[2].astype(f32)

    gs = pltpu.PrefetchScalarGridSpec(
        num_scalar_prefetch=1, grid=(nb,),
        in_specs=[pl.BlockSpec((1, tr, D_MODEL), lambda i, cq_ref: (cq_ref[1], i, 0)),
                  pl.BlockSpec((3, tr, D_MODEL), lambda i, cq_ref: (0, i, 0))],
        out_specs=pl.BlockSpec((tr, D_MODEL), lambda i, cq_ref: (cq_ref[0] * nb + i, 0)))
    return _pc(body, name=name, grid_spec=gs, sem=("arbitrary",),
               out_shape=_sds((2 * half, D_MODEL), f32))(cq, s32, recv)


def _sum_small(smalls):
    def body(s_ref, o_ref):
        tot = s_ref[0]
        for d in range(1, 8):
            tot = tot + s_ref[d]
        o_ref[...] = tot

    return _pc(body, name="small_sum", out_shape=_sds((SMALL_ROWS, D_MODEL), f32))(smalls)


def _adamw_math(w, g, m, v):
    m = ADAM_B1 * m + (1.0 - ADAM_B1) * g
    v = ADAM_B2 * v + (1.0 - ADAM_B2) * jnp.square(g)
    m_hat = m / (1.0 - ADAM_B1 ** ADAM_STEP)
    v_hat = v / (1.0 - ADAM_B2 ** ADAM_STEP)
    delta = -ADAM_LR * (m_hat / (jnp.sqrt(v_hat) + ADAM_EPS) + ADAM_WD * w)
    return delta, m, v


def _adamw(w, g, m, v, name):
    cols = w.shape[1]
    return _rw(_adamw_math, [w, g, m, v], [], [(cols, f32)] * 3, name=name, tr=256)


def _adamw_small(items):
    n = len(items)

    def body(*refs):
        ins, outs = refs[:4 * n], refs[4 * n:]
        for i in range(n):
            res = _adamw_math(*[r[...] for r in ins[4 * i:4 * i + 4]])
            for o, val in zip(outs[3 * i:3 * i + 3], res):
                o[...] = val

    flat = [a for it in items for a in it]
    out_shape = [_sds(it[0].shape, f32) for it in items for _ in range(3)]
    res = _pc(body, name="adamw_small", out_shape=out_shape)(*flat)
    return [tuple(res[3 * i:3 * i + 3]) for i in range(n)]


def _pack_shards(sh, layout, pad):
    parts = [(sh[n] if layer is None else sh[n][layer]).reshape(-1, D_MODEL).astype(bf16) for n, layer, _ in layout]
    if pad:
        parts.append(jnp.zeros((pad, D_MODEL), bf16))
    return jnp.concatenate(parts, axis=0)


def _mlp_full(g4, off, layer):
    o, r = off["mlp_w_up", layer]
    up = g4[:, o:o + r].transpose(1, 0, 2).reshape(D_MODEL, D_FF)
    o, r = off["mlp_w_down", layer]
    return {("mlp_w_up", layer): up, ("mlp_w_down", layer): g4[:, o:o + r].reshape(D_FF, D_MODEL)}


def _unpack_early(g4):
    hg = g4[:, 0:1024].reshape(4, 4, 256, D_MODEL)
    o, r = W_EARLY_OFF["hgrn_w_o", None]
    return {"hgrn_w4": hg.transpose(0, 2, 1, 3).reshape(D_MODEL, 4 * D_MODEL),
            "hgrn_w_o": g4[:, o:o + r].reshape(D_MODEL, D_MODEL)}


def _unpack_gla(g4):
    return _mlp_full(g4, W_GLA_OFF, 0)


def _unpack_last(g4):
    return _mlp_full(g4, W_LAST_OFF, 1)


def _unpack_mid(g4):
    def rows(name):
        o, r = W_MID_OFF[name, None]
        return g4[:, o:o + r]

    w = {"mla_w_dq": rows("mla_w_dq").reshape(D_MODEL, MLA_Q_LORA)}
    uq = rows("mla_w_uq").reshape(4, MLA_Q_LORA, 768).transpose(1, 0, 2).reshape(MLA_Q_LORA, MLA_HEADS, MLA_NOPE + MLA_ROPE)
    w["mla_w_qcat"] = jnp.pad(uq, ((0, 0), (0, 0), (0, ATT_QK - MLA_NOPE - MLA_ROPE))).reshape(MLA_Q_LORA, MLA_HEADS * ATT_QK)
    w["mla_w_o"] = rows("mla_w_o").reshape(MLA_HEADS * MLA_V, D_MODEL)
    dkv = rows("kv_w_dkv").reshape(D_MODEL, MLA_KV_LORA + MLA_ROPE)
    w["kv_w_dkv"] = jnp.pad(dkv, ((0, 0), (0, LANES - MLA_ROPE)))
    uk = rows("kv_w_uk").reshape(4, MLA_KV_LORA, 512).transpose(1, 0, 2).reshape(MLA_KV_LORA, MLA_HEADS, MLA_NOPE)
    w["kv_w_kcat"] = jnp.pad(uk, ((0, 0), (0, 0), (0, ATT_QK - MLA_NOPE))).reshape(MLA_KV_LORA, MLA_HEADS * ATT_QK)
    w["kv_w_uv"] = rows("kv_w_uv").reshape(4, MLA_KV_LORA, 512).transpose(1, 0, 2).reshape(MLA_KV_LORA, MLA_HEADS * MLA_V)
    return w


def _pack_grads_late(g):
    return g["hgrn_w4"].reshape(4, 256, 4, D_MODEL).transpose(0, 2, 1, 3).reshape(4, G_LATE_ROWS, D_MODEL)


def _grad_rows(g, name, layer):
    if name in ("mlp_w_up", "mlp_w_down"):
        full = g[name, layer]
        return full.reshape(D_MODEL, 4, 1024).transpose(1, 0, 2) if name == "mlp_w_up" else full.reshape(4, 1024, D_MODEL)
    if name == "mla_w_uq":
        uq = g["mla_w_qcat"].reshape(MLA_Q_LORA, MLA_HEADS, ATT_QK)[:, :, :MLA_NOPE + MLA_ROPE]
        return uq.reshape(MLA_Q_LORA, 4, 768).transpose(1, 0, 2).reshape(4, 192, D_MODEL)
    if name == "kv_w_uk":
        uk = g["kv_w_kcat"].reshape(MLA_KV_LORA, MLA_HEADS, ATT_QK)[:, :, :MLA_NOPE]
        return uk.reshape(MLA_KV_LORA, 4, 512).transpose(1, 0, 2).reshape(4, 128, D_MODEL)
    if name == "kv_w_uv":
        return g[name].reshape(MLA_KV_LORA, 4, 512).transpose(1, 0, 2).reshape(4, 128, D_MODEL)
    if name == "kv_w_dkv":
        return g[name][:, :MLA_KV_LORA + MLA_ROPE].reshape(4, 80, D_MODEL)
    return g[name].reshape(4, -1, D_MODEL)


def _pack_grads(g, layout):
    parts = [_grad_rows(g, name, layer) for name, layer, _ in layout]
    if layout in PADDED:
        parts.append(jnp.zeros((4, PACK_PAD, D_MODEL), bf16))
    return jnp.concatenate(parts, axis=1)


LOSS_ROW = 11


def _pack_small(g, loss):
    rows = []
    for name, _, r, wd in SMALL:
        a = g[name].reshape(r, wd)
        rows.append(jnp.pad(a, ((0, 0), (0, D_MODEL - wd))) if wd < D_MODEL else a)
    assert sum(r for _, _, r, _ in SMALL) == LOSS_ROW
    rows.append(jnp.full((1, D_MODEL), loss, f32))
    rows.append(jnp.zeros((SMALL_ROWS - LOSS_ROW - 1, D_MODEL), f32))
    return jnp.concatenate(rows, axis=0)


def kernel(x, hgrn_norm, hgrn_w_q, hgrn_w_f, hgrn_w_i, hgrn_w_g, hgrn_g_norm, hgrn_w_o, hgrn_lb_logits, mla_norm, mla_w_dq, mla_q_norm, mla_w_uq, mla_w_o, kv_in_norm, kv_w_dkv, kv_norm, kv_w_uk, kv_w_uv, mlp_norm, mlp_w_up, mlp_w_down, final_norm, loss_target, m_hgrn_norm, m_hgrn_w_q, m_hgrn_w_f, m_hgrn_w_i, m_hgrn_w_g, m_hgrn_g_norm, m_hgrn_w_o, m_hgrn_lb_logits, m_mla_norm, m_mla_w_dq, m_mla_q_norm, m_mla_w_uq, m_mla_w_o, m_kv_in_norm, m_kv_w_dkv, m_kv_norm, m_kv_w_uk, m_kv_w_uv, m_mlp_norm, m_mlp_w_up, m_mlp_w_down, m_final_norm, v_hgrn_norm, v_hgrn_w_q, v_hgrn_w_f, v_hgrn_w_i, v_hgrn_w_g, v_hgrn_g_norm, v_hgrn_w_o, v_hgrn_lb_logits, v_mla_norm, v_mla_w_dq, v_mla_q_norm, v_mla_w_uq, v_mla_w_o, v_kv_in_norm, v_kv_w_dkv, v_kv_norm, v_kv_w_uk, v_kv_w_uv, v_mlp_norm, v_mlp_w_up, v_mlp_w_down, v_final_norm):
    given = dict(locals())
    wsh = {n: given[n] for n in WEIGHTS}
    msh = {n: given["m_" + n] for n in WEIGHTS}
    vsh = {n: given["v_" + n] for n in WEIGHTS}
    xi, yi, ci = _me()
    chip = 2 * xi + yi
    cq = jnp.stack([ci, chip]).astype(jnp.int32)

    small_w = {n: wsh[n].reshape(r, -1) for n, _, r, _ in SMALL}
    sv = jnp.concatenate([small_w["hgrn_norm"], small_w["hgrn_lb_logits"], jnp.zeros((5, 256), f32)], axis=0)
    g4, sv4 = _all_gather_weights(_pack_shards(wsh, W_EARLY, 0), sv)
    w = _unpack_early(g4)
    w["hgrn_norm"] = sv4[:, 0, :].reshape(1, D_MODEL)
    w["hgrn_lb_logits"] = sv4[:, 1:3, :].transpose(1, 0, 2).reshape(2, D_MODEL)
    for n in ("hgrn_g_norm", "mla_norm", "mla_q_norm", "kv_in_norm", "kv_norm", "mlp_norm", "final_norm"):
        w[n] = small_w[n]

    class Comm:
        shard = {"gla": _pack_shards(wsh, W_GLA, 0), "mlp0_up": _pack_shards(wsh, W_MID, PACK_PAD),
                 "attn": _pack_shards(wsh, W_LAST, 0)}
        unpack = {"gla": _unpack_gla, "mlp0_up": _unpack_mid, "attn": _unpack_last}
        layout = {"gla": G_GLA, "attn": G_ATTN}
        received, s32 = {}, {}

        @staticmethod
        def reduce(grads, part):
            gp = _pack_grads(grads, Comm.layout[part])
            Comm.s32[part], s16 = _sum_over_cores(gp, _send_half_to_sibling(gp, "grads_to_sibling_" + part), cq,
                                                  "grads_sum_cores_" + part)
            return s16

    loss, grad_x, g = _local_step(x.reshape(-1, D_MODEL), loss_target.reshape(-1, D_MODEL), w, Comm)

    total = {part: _exchange_halves(_sum_over_chips(Comm.s32[part], Comm.received[part], cq, "grads_sum_chips_" + part),
                                    "grads_exchange_halves_" + part) for part in ("attn", "gla")}
    gp = _pack_grads_late(g)
    s32, s16 = _sum_over_cores(gp, _send_half_to_sibling(gp, "grads_to_sibling_late"), cq, "grads_sum_cores_late")
    from_chips, smalls = _exchange_chips(s16, _pack_small(g, loss))
    total["late"] = _exchange_halves(_sum_over_chips(s32, from_chips, cq, "grads_sum_chips_late"),
                                     "grads_exchange_halves_late")
    small_tot = _sum_small(smalls)
    loss = small_tot[LOSS_ROW, 0]

    where = {}
    for part, offsets in (("late", G_LATE_OFF), ("gla", G_GLA_OFF), ("attn", G_ATTN_OFF)):
        for (n, layer), (o, r) in offsets.items():
            where.setdefault(n, []).append(total[part][o:o + r])
    grad, delta, new_m, new_v = {}, {}, {}, {}
    for n, pieces in where.items():
        shp = wsh[n].shape
        two_d = (-1, shp[-1])
        grad[n] = (pieces[0] if len(pieces) == 1 else jnp.concatenate(pieces, axis=0)).reshape(shp)
        d, m2, v2 = _adamw(wsh[n].reshape(two_d), grad[n].reshape(two_d), msh[n].reshape(two_d), vsh[n].reshape(two_d),
                           "adamw_" + n)
        delta[n], new_m[n], new_v[n] = d.reshape(shp), m2.reshape(shp), v2.reshape(shp)
    items = []
    for n, row, r, wd in SMALL:
        gs = small_tot[row:row + r, :wd]
        if n in ("hgrn_norm", "hgrn_lb_logits"):
            gs = lax.dynamic_slice(gs, (0, 256 * chip), (r, 256))
        grad[n] = gs.reshape(wsh[n].shape)
        items.append((small_w[n], gs, msh[n].reshape(gs.shape), vsh[n].reshape(gs.shape)))
    for (n, _, _, _), (d, m2, v2) in zip(SMALL, _adamw_small(items)):
        shp = wsh[n].shape
        delta[n], new_m[n], new_v[n] = d.reshape(shp), m2.reshape(shp), v2.reshape(shp)

    return (loss, grad_x.reshape(x.shape), *[grad[n] for n in WEIGHTS], *[delta[n] for n in WEIGHTS],
            *[new_m[n] for n in WEIGHTS], *[new_v[n] for n in WEIGHTS])
```

```python
import functools

import jax
import jax.numpy as jnp
from jax import lax
from jax.experimental import pallas as pl
from jax.experimental.pallas import tpu as pltpu

f32, bf16 = jnp.float32, jnp.bfloat16
HI = lax.Precision.HIGHEST
MESH = pl.DeviceIdType.MESH

D_MODEL = 1024
D_FF = 4096
EPS = 1e-6
HGRN_HEADS, HGRN_DK, HGRN_CHUNK, HGRN_SUB = 8, 128, 64, 16
MLA_HEADS, MLA_NOPE, MLA_ROPE, MLA_V = 16, 128, 64, 128
MLA_Q_LORA, MLA_KV_LORA = 256, 256
ROPE_THETA = 10000.0
ATT_SCALE = (MLA_NOPE + MLA_ROPE) ** -0.5
EXP_CLAMP = 80.0

ADAM_LR, ADAM_B1, ADAM_B2, ADAM_EPS, ADAM_WD, ADAM_STEP = 0.001, 0.9, 0.999, 1e-08, 0.01, 10

V7X_VMEM_BYTES = 64 * 1024 * 1024
VMEM_LIMIT = V7X_VMEM_BYTES - 8 * 1024 * 1024
LANES = 128

PACK_PAD = 16
W_EARLY = (("hgrn_w_q", None, 256), ("hgrn_w_f", None, 256), ("hgrn_w_i", None, 256), ("hgrn_w_g", None, 256),
           ("hgrn_w_o", None, 256))
W_GLA = (("mlp_w_up", 0, 1024), ("mlp_w_down", 0, 1024))
W_MID = (("mla_w_dq", None, 64), ("mla_w_uq", None, 192), ("mla_w_o", None, 512), ("kv_w_dkv", None, 80),
         ("kv_w_uk", None, 128), ("kv_w_uv", None, 128))
W_LAST = (("mlp_w_up", 1, 1024), ("mlp_w_down", 1, 1024))
G_LATE = (("hgrn_w_q", None, 256), ("hgrn_w_f", None, 256), ("hgrn_w_i", None, 256), ("hgrn_w_g", None, 256))
G_ATTN = (("mla_w_o", None, 512), ("mlp_w_up", 1, 1024), ("mlp_w_down", 1, 1024))
G_GLA = (("hgrn_w_o", None, 256), ("mla_w_dq", None, 64), ("mla_w_uq", None, 192), ("kv_w_dkv", None, 80),
         ("kv_w_uk", None, 128), ("kv_w_uv", None, 128), ("mlp_w_up", 0, 1024), ("mlp_w_down", 0, 1024))
PADDED = (W_MID, G_GLA)


def _offsets(layout):
    out, o = {}, 0
    for name, layer, rows in layout:
        out[name, layer] = (o, rows)
        o += rows
    return out, o + (PACK_PAD if layout in PADDED else 0)


W_EARLY_OFF, W_EARLY_ROWS = _offsets(W_EARLY)
W_GLA_OFF, W_GLA_ROWS = _offsets(W_GLA)
W_MID_OFF, W_MID_ROWS = _offsets(W_MID)
W_LAST_OFF, W_LAST_ROWS = _offsets(W_LAST)
G_LATE_OFF, G_LATE_ROWS = _offsets(G_LATE)
G_ATTN_OFF, G_ATTN_ROWS = _offsets(G_ATTN)
G_GLA_OFF, G_GLA_ROWS = _offsets(G_GLA)
assert all(r % 32 == 0 for r in (W_EARLY_ROWS, W_MID_ROWS, W_LAST_ROWS, G_LATE_ROWS, G_ATTN_ROWS, G_GLA_ROWS))

WEIGHTS = ("hgrn_norm", "hgrn_w_q", "hgrn_w_f", "hgrn_w_i", "hgrn_w_g", "hgrn_g_norm", "hgrn_w_o", "hgrn_lb_logits",
           "mla_norm", "mla_w_dq", "mla_q_norm", "mla_w_uq", "mla_w_o", "kv_in_norm", "kv_w_dkv", "kv_norm", "kv_w_uk",
           "kv_w_uv", "mlp_norm", "mlp_w_up", "mlp_w_down", "final_norm")
SMALL = (("hgrn_norm", 0, 1, 1024), ("hgrn_lb_logits", 1, 2, 1024), ("hgrn_g_norm", 3, 1, 128),
         ("mla_norm", 4, 1, 1024), ("mla_q_norm", 5, 1, 256), ("kv_in_norm", 6, 1, 1024), ("kv_norm", 7, 1, 256),
         ("mlp_norm", 8, 2, 1024), ("final_norm", 10, 1, 1024))
SMALL_ROWS = 16


def _pc(body, *, name, out_shape, grid=None, in_specs=None, out_specs=None, scratch=(), sem=None, grid_spec=None,
        aliases=None):
    params = pltpu.CompilerParams(dimension_semantics=sem, vmem_limit_bytes=VMEM_LIMIT)
    if grid_spec is not None:
        return pl.pallas_call(body, name=name, out_shape=out_shape, grid_spec=grid_spec, compiler_params=params,
                              interpret=False)
    kw = {k: v for k, v in (("grid", grid), ("in_specs", in_specs), ("out_specs", out_specs),
                            ("input_output_aliases", aliases)) if v is not None}
    return pl.pallas_call(body, name=name, out_shape=out_shape, scratch_shapes=list(scratch), compiler_params=params,
                          interpret=False, **kw)


def _sds(shape, dtype):
    return jax.ShapeDtypeStruct(tuple(shape), dtype)


def _mm(a, b, *, name, ta=False, tb=False, outs=(f32,), epi=None, extras=(), accs=0, gather=None,
        tm=1024, tn=1024, tk=4096):
    m, k = (a.shape[1], a.shape[0]) if ta else a.shape
    n = b.shape[0] if tb else b.shape[1]
    tm, tn, tk = min(tm, m), min(tn, n), min(tk, k)
    assert m % tm == 0 and n % tn == 0 and k % tk == 0, (name, m, n, k)
    nk = k // tk
    assert accs == 0 or (tn == n and nk == 1), name
    a_spec = pl.BlockSpec((tk, tm), lambda i, j, kk: (kk, i)) if ta else pl.BlockSpec((tm, tk), lambda i, j, kk: (i, kk))
    b_spec = pl.BlockSpec((tn, tk), lambda i, j, kk: (j, kk)) if tb else pl.BlockSpec((tk, tn), lambda i, j, kk: (kk, j))

    def extra_spec(e):
        if e.shape == (m, n):
            return pl.BlockSpec((tm, tn), lambda i, j, kk: (i, j))
        if e.shape[0] == m:
            return pl.BlockSpec((tm, e.shape[1]), lambda i, j, kk: (i, 0))
        return pl.BlockSpec((e.shape[0], tn), lambda i, j, kk: (0, j))

    e_specs = [extra_spec(e) for e in extras]
    dn = (((0 if ta else 1,), (1 if tb else 0,)), ((), ()))
    n_e, n_o = len(extras), len(outs)

    def finish(r, e_refs, o_refs):
        res = epi(r, *[e[...] for e in e_refs]) if epi is not None else (r,)
        for o, v in zip(o_refs[:n_o], res[:n_o]):
            o[...] = v.astype(o.dtype)
        for o, v in zip(o_refs[n_o:], res[n_o:]):
            @pl.when(pl.program_id(0) == 0)
            def _(o=o):
                o[...] = jnp.zeros_like(o)
            o[...] += v

    grid = (m // tm, n // tn, nk)
    n_g = 0 if gather is None else 2

    def body(*refs):
        a_ref, b_ref = refs[0], refs[1]
        e_refs = refs[2:2 + n_e]
        o_refs = refs[2 + n_e + n_g:2 + n_e + n_g + n_o + accs]
        if gather is not None:
            wp_ref, gathered_ref, sems = refs[2 + n_e], refs[2 + n_e + n_g + n_o + accs], refs[-1]
            pid = [pl.program_id(d) for d in range(3)]
            pl.when((pid[0] == 0) & (pid[1] == 0) & (pid[2] == 0))(lambda: _gather_start(wp_ref, gathered_ref, sems))
            pl.when((pid[0] == grid[0] - 1) & (pid[1] == grid[1] - 1) & (pid[2] == grid[2] - 1))(
                lambda: _gather_finish(wp_ref, gathered_ref, sems))
        prod = lax.dot_general(a_ref[...].astype(bf16), b_ref[...].astype(bf16), dn, preferred_element_type=f32)
        if nk == 1:
            finish(prod, e_refs, o_refs)
            return
        acc = refs[2 + n_e + n_g + n_o + accs + n_g // 2]
        kk = pl.program_id(2)

        @pl.when(kk == 0)
        def _():
            acc[...] = jnp.zeros_like(acc)

        acc[...] += prod

        @pl.when(kk == nk - 1)
        def _():
            finish(acc[...], e_refs, o_refs)

    out_specs = ([pl.BlockSpec((tm, tn), lambda i, j, kk: (i, j)) for _ in outs] +
                 [pl.BlockSpec((1, n), lambda i, j, kk: (0, 0))] * accs)
    out_shape = [_sds((m, n), dt) for dt in outs] + [_sds((1, n), f32)] * accs
    scratch = [pltpu.VMEM((tm, tn), f32)] if nk > 1 else []
    if gather is None:
        out = _pc(body, name=name, grid=grid, in_specs=[a_spec, b_spec] + e_specs, out_specs=out_specs,
                  out_shape=out_shape, scratch=scratch,
                  sem=("arbitrary" if accs else "parallel", "parallel", "arbitrary"))(a, b, *extras)
    else:
        out = _pc(body, name=name + "_gather", grid=grid, in_specs=[a_spec, b_spec] + e_specs + [HBM, HBM],
                  out_specs=out_specs + [HBM], out_shape=out_shape + [_sds((4,) + gather.shape, bf16)],
                  aliases={2 + n_e + 1: n_o + accs}, scratch=scratch + [pltpu.SemaphoreType.DMA((2, 6))],
                  sem=("arbitrary",) * 3)(a, b, *extras, gather, _gather_base(gather))
    return out[0] if len(out) == 1 else out


def _wgrad(a, b, name):
    return _mm(a, b, ta=True, name=name, outs=(bf16,))


def _rw(fn, rows, bcast, outs, accs=(), *, name, tr=256):
    t = rows[0].shape[0]
    tr = min(tr, t)
    assert t % tr == 0
    n_r, n_b, n_o, n_a = len(rows), len(bcast), len(outs), len(accs)

    def body(*refs):
        r_refs = refs[:n_r]
        b_refs = refs[n_r:n_r + n_b]
        o_refs = refs[n_r + n_b:n_r + n_b + n_o]
        a_refs = refs[n_r + n_b + n_o:]
        res = fn(*[r[...] for r in r_refs], *[b[...] for b in b_refs])
        for o, v in zip(o_refs, res[:n_o]):
            o[...] = v.astype(o.dtype)
        i = pl.program_id(0)
        for a_ref, v in zip(a_refs, res[n_o:]):
            @pl.when(i == 0)
            def _(a_ref=a_ref):
                a_ref[...] = jnp.zeros_like(a_ref)
            a_ref[...] += v

    in_specs = [pl.BlockSpec((tr, r.shape[1]), lambda i: (i, 0)) for r in rows]
    in_specs += [pl.BlockSpec(b.shape, lambda i: (0, 0)) for b in bcast]
    out_specs = [pl.BlockSpec((tr, w), lambda i: (i, 0)) for w, _ in outs]
    out_specs += [pl.BlockSpec(s, lambda i: (0, 0)) for s in accs]
    out_shape = [_sds((t, w), dt) for w, dt in outs] + [_sds(s, f32) for s in accs]
    res = _pc(body, name=name, grid=(t // tr,), in_specs=in_specs, out_specs=out_specs, out_shape=out_shape,
              sem=("arbitrary",))(*rows, *bcast)
    return res


def _rms(x, gain):
    return x * lax.rsqrt(jnp.mean(x * x, axis=-1, keepdims=True) + EPS) * gain


def _rms_bwd(x, gain, dy):
    _, vjp = jax.vjp(_rms, x, gain)
    return vjp(dy)


def _lower_bound(lbl):
    l0, l1 = lbl[0:1, :], lbl[1:2, :]
    mx = jnp.maximum(l0, l1)
    e0, e1 = jnp.exp(l0 - mx), jnp.exp(l1 - mx)
    return e0 / (e0 + e1)


def _gates(qpre, fpre, lbl):
    lb = _lower_bound(lbl)
    q = jax.nn.silu(qpre)
    forget = lb + (1.0 - lb) * jax.nn.sigmoid(fpre)
    return q, 1.0 - forget, jnp.log(forget)


def _head_norm_gate(o, gpre, gn):
    return _rms(o, gn) * jax.nn.silu(gpre)


def _swap_halves(x):
    w = x.shape[1]
    lane = lax.broadcasted_iota(jnp.int32, x.shape, 1)
    return jnp.where((lane % MLA_ROPE) < MLA_ROPE // 2, pltpu.roll(x, w - MLA_ROPE // 2, 1),
                     pltpu.roll(x, MLA_ROPE // 2, 1))


def _tile_lanes(tab, w):
    return tab if w == tab.shape[1] else jnp.concatenate([tab] * (w // tab.shape[1]), axis=1)


def _rope(x, cos, sgn_sin, sign=1.0):
    w = x.shape[1]
    return x * _tile_lanes(cos, w) + sign * _swap_halves(x) * _tile_lanes(sgn_sin, w)


def _rope_heads(x, cos, sgn_sin, sign, scale):
    parts = []
    for h in range(x.shape[1] // (2 * LANES)):
        parts.append(x[:, 2 * LANES * h:2 * LANES * h + LANES] * scale)
        parts.append(_rope(x[:, 2 * LANES * h + LANES:2 * LANES * (h + 1)], cos, sgn_sin, sign) * scale)
    return jnp.concatenate(parts, axis=1)


def _bd(a, b, ca, cb):
    return lax.dot_general(a.astype(bf16), b.astype(bf16), (((ca,), (cb,)), ((), ())), preferred_element_type=f32)


@jax.custom_vjp
def _dot_nn(a, b):
    return _bd(a, b, 1, 0)


@jax.custom_vjp
def _dot_nt(a, b):
    return _bd(a, b, 1, 1)


@jax.custom_vjp
def _dot_tn(a, b):
    return _bd(a, b, 0, 0)


_dot_nn.defvjp(lambda a, b: (_bd(a, b, 1, 0), (a, b)), lambda r, g: (_bd(g, r[1], 1, 1), _bd(r[0], g, 0, 0)))
_dot_nt.defvjp(lambda a, b: (_bd(a, b, 1, 1), (a, b)), lambda r, g: (_bd(g, r[1], 1, 0), _bd(g, r[0], 0, 0)))
_dot_tn.defvjp(lambda a, b: (_bd(a, b, 0, 0), (a, b)), lambda r, g: (_bd(r[1], g, 1, 1), _bd(r[0], g, 1, 0)))


def _scan_rows(x, reverse):
    n = x.shape[0]
    row = lax.broadcasted_iota(jnp.int32, x.shape, 0)
    s = 1
    while s < n:
        if reverse:
            x = x + jnp.where(row < n - s, pltpu.roll(x, n - s, 0), 0.0)
        else:
            x = x + jnp.where(row >= s, pltpu.roll(x, s, 0), 0.0)
        s *= 2
    return x


@jax.custom_vjp
def _cumsum_rows(g):
    return _scan_rows(g, False)


_cumsum_rows.defvjp(lambda g: (_scan_rows(g, False), None), lambda _, ct: (_scan_rows(ct, True),))

HGRN_PAIRS = HGRN_HEADS // 2
HGRN_PAIR = 2 * HGRN_DK
GLA_STATE = (HGRN_PAIRS, HGRN_PAIR, HGRN_PAIR)


def _gla_consts():
    s = HGRN_SUB
    r = lax.broadcasted_iota(jnp.int32, (HGRN_PAIR, HGRN_PAIR), 0)
    c = lax.broadcasted_iota(jnp.int32, (HGRN_PAIR, HGRN_PAIR), 1)
    pair_mask = (r < HGRN_DK) == (c < HGRN_DK)
    masks = []
    for i in range(HGRN_CHUNK // s):
        n = s * (i + 1)
        row = lax.broadcasted_iota(jnp.int32, (HGRN_HEADS * s, HGRN_HEADS * n), 0)
        col = lax.broadcasted_iota(jnp.int32, (HGRN_HEADS * s, HGRN_HEADS * n), 1)
        col_head = sum((col >= m * n).astype(jnp.int32) for m in range(1, HGRN_HEADS))
        masks.append((col_head == row // s) & (col - col_head * n <= s * i + row % s))
    return pair_mask, masks


def _heads_to_rows(x):
    return jnp.concatenate([x[:, HGRN_DK * h:HGRN_DK * (h + 1)] for h in range(HGRN_HEADS)], axis=0)


def _gla_chunk(consts, dots, q, k, v, g, st):
    pair_mask, masks = consts
    dot_nn, dot_nt, dot_tn = dots
    c, s = HGRN_CHUNK, HGRN_SUB
    b = _cumsum_rows(g)
    b_last = b[c - 1:c, :]
    q_in, k_out = q * jnp.exp(b), k * jnp.exp(b_last - b)
    o_inter, st_new = [], []
    for p in range(HGRN_PAIRS):
        cols = slice(HGRN_PAIR * p, HGRN_PAIR * (p + 1))
        o_inter.append(dot_nt(q_in[:, cols], st[p]))
        st_new.append(st[p] * jnp.exp(b_last[:, cols]) + jnp.where(pair_mask, dot_tn(v[:, cols], k_out[:, cols]), 0.0))
    intra = []
    for i in range(c // s):
        n = s * (i + 1)
        ref = b[s * i - 1:s * i, :] if i else jnp.zeros_like(b_last)
        qt = _heads_to_rows(q[s * i:n] * jnp.exp(b[s * i:n] - ref))
        kt = _heads_to_rows(k[:n] * jnp.exp(jnp.minimum(ref - b[:n], EXP_CLAMP)))
        sc = jnp.where(masks[i], dot_nt(qt, kt), 0.0)
        oi = dot_nn(sc, _heads_to_rows(v[:n]))
        intra.append(jnp.concatenate([oi[s * h:s * (h + 1)] for h in range(HGRN_HEADS)], axis=1))
    return jnp.concatenate(o_inter, axis=1) + jnp.concatenate(intra, axis=0), st_new


_PLAIN_DOTS = (lambda a, b: _bd(a, b, 1, 0), lambda a, b: _bd(a, b, 1, 1), lambda a, b: _bd(a, b, 0, 0))
_VJP_DOTS = (_dot_nn, _dot_nt, _dot_tn)


def _hgrn_mix(consts, dots, qpre, fpre, v, gpre, lbl, gn, st):
    q, k, g = _gates(qpre, fpre, lbl)
    o, st_new = _gla_chunk(consts, dots, q, k, v, g, st)
    y = [_head_norm_gate(o[:, HGRN_DK * h:HGRN_DK * (h + 1)], gpre[:, HGRN_DK * h:HGRN_DK * (h + 1)], gn)
         for h in range(HGRN_HEADS)]
    return jnp.concatenate(y, axis=1), st_new


def _gla_fwd(p4, lbl, gn, gather=None):
    t = p4.shape[0]
    nc = t // HGRN_CHUNK

    def body(q_ref, k_ref, v_ref, g_ref, lbl_ref, gn_ref, *rest):
        if gather is None:
            o_ref, s_ref, st = rest
        else:
            wp_ref, _, o_ref, s_ref, gathered_ref, st, sems = rest

        @pl.when(pl.program_id(0) == 0)
        def _():
            st[...] = jnp.zeros_like(st)
            if gather is not None:
                _gather_start(wp_ref, gathered_ref, sems)

        if gather is not None:
            @pl.when(pl.program_id(0) == nc - 1)
            def _():
                _gather_finish(wp_ref, gathered_ref, sems)

        s_in = [st[p] for p in range(HGRN_PAIRS)]
        y, st_new = _hgrn_mix(_gla_consts(), _PLAIN_DOTS, q_ref[...], k_ref[...], v_ref[...], g_ref[...],
                              lbl_ref[...], gn_ref[...], s_in)
        o_ref[...] = y.astype(o_ref.dtype)
        for p in range(HGRN_PAIRS):
            s_ref[0, p] = s_in[p]
            st[p] = st_new[p]

    blk = lambda off: pl.BlockSpec((HGRN_CHUNK, D_MODEL), lambda c: (c, off))
    whole = lambda a: pl.BlockSpec(a.shape, lambda c: (0, 0))
    state_shape = GLA_STATE
    in_specs = [blk(0), blk(1), blk(2), blk(3), whole(lbl), whole(gn)]
    out_specs = [blk(0), pl.BlockSpec((1,) + state_shape, lambda c: (c, 0, 0, 0))]
    out_shape = [_sds((t, D_MODEL), bf16), _sds((nc,) + state_shape, f32)]
    scratch = [pltpu.VMEM(state_shape, f32)]
    if gather is None:
        return _pc(body, name="gla_fwd", grid=(nc,), in_specs=in_specs, out_specs=out_specs, out_shape=out_shape,
                   scratch=scratch, sem=("arbitrary",))(p4, p4, p4, p4, lbl, gn)
    return _pc(body, name="gla_fwd_gather", grid=(nc,), in_specs=in_specs + [HBM, HBM], out_specs=out_specs + [HBM],
               out_shape=out_shape + [_sds((4,) + gather.shape, bf16)], aliases={7: 2},
               scratch=scratch + [pltpu.SemaphoreType.DMA((2, 6))], sem=("arbitrary",))(
                   p4, p4, p4, p4, lbl, gn, gather, _gather_base(gather))


def _gla_bwd(p4, lbl, gn, states, dy, exchange=None):
    t = p4.shape[0]
    nc = t // HGRN_CHUNK

    def body(q_ref, k_ref, v_ref, g_ref, lbl_ref, gn_ref, s_ref, dy_ref, *rest):
        if exchange is None:
            dp_ref, dlbl_ref, dgn_ref, dst = rest
        else:
            sb_ref, dp_ref, dlbl_ref, dgn_ref, recv_ref, dst, sems = rest

        @pl.when(pl.program_id(0) == 0)
        def _():
            dst[...] = jnp.zeros_like(dst)
            dlbl_ref[...] = jnp.zeros_like(dlbl_ref)
            dgn_ref[...] = jnp.zeros_like(dgn_ref)
            if exchange is not None:
                _chips_start(sb_ref, recv_ref, sems)

        if exchange is not None:
            @pl.when(pl.program_id(0) == nc - 1)
            def _():
                _chips_finish(sb_ref, recv_ref, sems)

        consts = _gla_consts()
        fn = lambda *args: _hgrn_mix(consts, _VJP_DOTS, *args)
        pairs = range(HGRN_PAIRS)
        _, vjp = jax.vjp(fn, q_ref[...], k_ref[...], v_ref[...], g_ref[...], lbl_ref[...], gn_ref[...],
                         [s_ref[0, p] for p in pairs])
        *d_proj, dlbl, dgn, ds = vjp((dy_ref[...], [dst[p] for p in pairs]))
        for i, d in enumerate(d_proj):
            dp_ref[:, D_MODEL * i:D_MODEL * (i + 1)] = d.astype(dp_ref.dtype)
        dlbl_ref[...] += dlbl
        dgn_ref[...] += dgn
        for p in pairs:
            dst[p] = ds[p]

    blk = lambda off: pl.BlockSpec((HGRN_CHUNK, D_MODEL), lambda c: (nc - 1 - c, off))
    whole = lambda a: pl.BlockSpec(a.shape, lambda c: (0, 0))
    state_shape = GLA_STATE
    in_specs = [blk(0), blk(1), blk(2), blk(3), whole(lbl), whole(gn),
                pl.BlockSpec((1,) + state_shape, lambda c: (nc - 1 - c, 0, 0, 0)), blk(0)]
    out_specs = [pl.BlockSpec((HGRN_CHUNK, 4 * D_MODEL), lambda c: (nc - 1 - c, 0)), whole(lbl), whole(gn)]
    out_shape = [_sds((t, 4 * D_MODEL), bf16), _sds(lbl.shape, f32), _sds(gn.shape, f32)]
    scratch = [pltpu.VMEM(state_shape, f32)]
    if exchange is None:
        return _pc(body, name="gla_bwd", grid=(nc,), in_specs=in_specs, out_specs=out_specs, out_shape=out_shape,
                   scratch=scratch, sem=("arbitrary",))(p4, p4, p4, p4, lbl, gn, states, dy)
    return _pc(body, name="gla_bwd_exchange", grid=(nc,), in_specs=in_specs + [HBM], out_specs=out_specs + [HBM],
               out_shape=out_shape + [_sds((3,) + exchange.shape[1:], bf16)],
               scratch=scratch + [pltpu.SemaphoreType.DMA((2, 3))], sem=("arbitrary",))(
                   p4, p4, p4, p4, lbl, gn, states, dy, exchange)


ATT_FWD_TQ, ATT_FWD_TK = 1024, 1024
ATT_BWD_TQ, ATT_BWD_TK = 1024, 512
ATT_QK = 2 * LANES
NEG = -1e30


def _pair_masks(shape):
    lane = lax.broadcasted_iota(jnp.int32, shape, 1)
    return lane < MLA_ROPE, lane >= MLA_ROPE


def _causal(shape, row0, col0):
    row = row0 + lax.broadcasted_iota(jnp.int32, shape, 0)
    col = col0 + lax.broadcasted_iota(jnp.int32, shape, 1)
    return col <= row


def _qk_cols(e):
    return slice(ATT_QK * e, ATT_QK * (e + 1))


def _v_cols(e):
    return slice(MLA_V * e, MLA_V * (e + 1))


def _first_last_step(n0, n1):
    p, i = pl.program_id(0), pl.program_id(1)
    return (p == 0) & (i == 0), (p == n0 - 1) & (i == n1 - 1)


def _attn_fwd(qc, kc, v, gather=None):
    t = qc.shape[0]
    tq, tk = min(ATT_FWD_TQ, t), min(ATT_FWD_TK, t)
    assert tq == tk, "the diagonal block is split in the body on the premise of square blocks"
    npair = MLA_HEADS // 2

    def body(q_ref, k_ref, v_ref, *rest):
        if gather is None:
            o_ref, lse_ref = rest
        else:
            wp_ref, _, o_ref, lse_ref, gathered_ref, sems = rest
            first, last = _first_last_step(npair, t // tq)
            pl.when(first)(lambda: _gather_start(wp_ref, gathered_ref, sems))
            pl.when(last)(lambda: _gather_finish(wp_ref, gathered_ref, sems))
        i = pl.program_id(1)
        q = [q_ref[:, _qk_cols(e)] for e in range(2)]

        def update(state, q_rows, e, ks, ok):
            m, l, acc = state
            s = _bd(q_rows, k_ref[ks, _qk_cols(e)], 1, 1)
            if ok is not None:
                s = jnp.where(ok, s, NEG)
            m_new = jnp.maximum(m, jnp.max(s, axis=-1, keepdims=True))
            p = jnp.exp(s - m_new)
            alpha = jnp.exp(m - m_new)
            return m_new, alpha * l + jnp.sum(p, axis=-1, keepdims=True), alpha * acc + _bd(p, v_ref[ks, _v_cols(e)], 1, 0)

        def step(j, carry):
            ks = pl.ds(pl.multiple_of(j * tk, tk), tk)
            return tuple(update(carry[e], q[e], e, ks, None) for e in range(2))

        one = (jnp.full((tq, 1), NEG, f32), jnp.zeros((tq, 1), f32), jnp.zeros((tq, MLA_V), f32))
        carry = lax.fori_loop(0, i, step, (one, one))
        half = tq // 2
        outs, lses = [], []
        for e in range(2):
            top = update(tuple(a[:half] for a in carry[e]), q[e][:half], e,
                         pl.ds(pl.multiple_of(i * tk, tk), half), _causal((half, half), 0, 0))
            bottom = update(tuple(a[half:] for a in carry[e]), q[e][half:], e,
                            pl.ds(pl.multiple_of(i * tk, tk), tk), _causal((half, tk), half, 0))
            m, l, acc = (jnp.concatenate(ab, axis=0) for ab in zip(top, bottom))
            outs.append(acc / l)
            lses.append(m + jnp.log(l))
        o_ref[...] = jnp.concatenate(outs, axis=1).astype(o_ref.dtype)
        lo, _ = _pair_masks((tq, LANES))
        lse_ref[...] = jnp.where(lo, *lses)

    in_specs = [pl.BlockSpec((tq, 2 * ATT_QK), lambda p, i: (i, p)),
                pl.BlockSpec((t, 2 * ATT_QK), lambda p, i: (0, p)),
                pl.BlockSpec((t, 2 * MLA_V), lambda p, i: (0, p))]
    out_specs = [pl.BlockSpec((tq, 2 * MLA_V), lambda p, i: (i, p)), pl.BlockSpec((tq, LANES), lambda p, i: (i, p))]
    out_shape = [_sds((t, MLA_HEADS * MLA_V), bf16), _sds((t, npair * LANES), f32)]
    if gather is None:
        return _pc(body, name="attn_fwd", grid=(npair, t // tq), in_specs=in_specs, out_specs=out_specs,
                   out_shape=out_shape, sem=("arbitrary", "arbitrary"))(qc, kc, v)
    return _pc(body, name="attn_fwd_gather", grid=(npair, t // tq), in_specs=in_specs + [HBM, HBM],
               out_specs=out_specs + [HBM], out_shape=out_shape + [_sds((4,) + gather.shape, bf16)], aliases={4: 2},
               scratch=[pltpu.SemaphoreType.DMA((2, 6))], sem=("arbitrary", "arbitrary"))(
                   qc, kc, v, gather, _gather_base(gather))


def _attn_bwd(qc, kc, v, do, lse, delta, exchange=None):
    t = qc.shape[0]
    tq, tk = min(ATT_BWD_TQ, t), min(ATT_BWD_TK, t)
    npair = MLA_HEADS // 2
    nq = t // tq
    sub = tq // tk
    assert sub * tk == tq

    def body(q_ref, do_ref, lse_ref, dl_ref, k_ref, v_ref, *rest):
        if exchange is None:
            dq_ref, dk_ref, dv_ref = rest
        else:
            sb_ref, dq_ref, dk_ref, dv_ref, recv_ref, sems = rest
            first, last = _first_last_step(npair, t // tk)
            pl.when(first)(lambda: _chips_start(sb_ref, recv_ref, sems))
            pl.when(last)(lambda: _chips_finish(sb_ref, recv_ref, sems))
        j = pl.program_id(1)

        @pl.when(j == 0)
        def _():
            dq_ref[...] = jnp.zeros_like(dq_ref)

        k = [k_ref[:, _qk_cols(e)] for e in range(2)]
        vv = [v_ref[:, _v_cols(e)] for e in range(2)]

        def rows_step(carry, row0, rows, masked):
            qs = pl.ds(pl.multiple_of(row0, rows), rows)
            ok = _causal((rows, tk), row0, j * tk) if masked else None
            lse2, dl2 = lse_ref[qs, :], dl_ref[qs, :]
            new = []
            for e in range(2):
                dk, dv = carry[e]
                q_e, do_e = q_ref[qs, _qk_cols(e)], do_ref[qs, _v_cols(e)]
                p = jnp.exp(_bd(q_e, k[e], 1, 1) - lse2[:, MLA_ROPE * e:MLA_ROPE * e + 1])
                if masked:
                    p = jnp.where(ok, p, 0.0)
                dv = dv + _bd(p, do_e, 0, 0)
                dp = _bd(do_e, vv[e], 1, 1)
                ds = (p * (dp - dl2[:, MLA_ROPE * e:MLA_ROPE * e + 1])).astype(bf16)
                dk = dk + _bd(ds, q_e, 0, 0)
                dq_ref[qs, _qk_cols(e)] += _bd(ds, k[e], 1, 0)
                new.append((dk, dv))
            return tuple(new)

        one = (jnp.zeros((tk, ATT_QK), f32), jnp.zeros((tk, MLA_V), f32))
        i0 = (j * tk) // tq
        j_local = j - i0 * sub
        carry = (one, one)
        for r in range(sub):
            run = functools.partial(rows_step, row0=i0 * tq + r * tk, rows=tk, masked=True)
            carry = run(carry) if r == sub - 1 else lax.cond(r >= j_local, run, lambda c: c, carry)
        carry = lax.fori_loop(i0 + 1, nq, lambda i, c: rows_step(c, i * tq, tq, False), carry)
        for e in range(2):
            dk_ref[:, _qk_cols(e)] = carry[e][0].astype(dk_ref.dtype)
            dv_ref[:, _v_cols(e)] = carry[e][1].astype(dv_ref.dtype)

    res = lambda w: pl.BlockSpec((t, w), lambda p, j: (0, p))
    blk = lambda w: pl.BlockSpec((tk, w), lambda p, j: (j, p))
    in_specs = [res(2 * ATT_QK), res(2 * MLA_V), res(LANES), res(LANES), blk(2 * ATT_QK), blk(2 * MLA_V)]
    out_specs = [res(2 * ATT_QK), blk(2 * ATT_QK), blk(2 * MLA_V)]
    out_shape = [_sds((t, MLA_HEADS * ATT_QK), f32), _sds((t, MLA_HEADS * ATT_QK), bf16), _sds((t, MLA_HEADS * MLA_V), bf16)]
    if exchange is None:
        return _pc(body, name="attn_bwd", grid=(npair, t // tk), in_specs=in_specs, out_specs=out_specs,
                   out_shape=out_shape, sem=("arbitrary", "arbitrary"))(qc, do, lse, delta, kc, v)
    return _pc(body, name="attn_bwd_exchange", grid=(npair, t // tk), in_specs=in_specs + [HBM],
               out_specs=out_specs + [HBM], out_shape=out_shape + [_sds((3,) + exchange.shape[1:], bf16)],
               scratch=[pltpu.SemaphoreType.DMA((2, 3))], sem=("arbitrary", "arbitrary"))(
                   qc, do, lse, delta, kc, v, exchange)


def _rope_tables(t):
    half = MLA_ROPE // 2
    inv_freq = ROPE_THETA ** (-jnp.arange(half, dtype=f32) / half)
    ang = jnp.arange(t, dtype=f32)[:, None] * inv_freq[None, :]
    cos, sin = jnp.cos(ang), jnp.sin(ang)
    return jnp.concatenate([cos, cos] * 2, axis=1), jnp.concatenate([-sin, sin] * 2, axis=1)


def _relu2_epi(u):
    r = jnp.maximum(u, 0.0)
    return u, r * r


def _add_epi(r, res):
    return (r + res,)


def _drelu2_epi(da, u):
    return (da * 2.0 * jnp.maximum(u.astype(f32), 0.0),)


ROWWISE_EPI_TM = 512


def _residual_norms_epi(r, res, *gains):
    h = r + res
    return (h, *[_rms(h, g) for g in gains])


def _residual_out(a, w, h, gains, name):
    res = _mm(a, w, name=name, outs=(f32,) + (bf16,) * len(gains), epi=_residual_norms_epi, extras=(h, *gains),
              tm=ROWWISE_EPI_TM)
    return res if gains else [res]


def _dnorm_epi(dy, x, dres, gain):
    dx, dg = _rms_bwd(x, gain, dy)
    return dx + dres, dx + dres, dg


def _mlp_fwd(h, xm, w_up, w_down, tag, next_gains, gather=None):
    u, a, *gathered = _mm(xm, w_up, name=f"mlp{tag}_up", outs=(bf16, bf16), epi=_relu2_epi, gather=gather)
    h_out, *normed = _residual_out(a, w_down, h, next_gains, f"mlp{tag}_down")
    return h_out, normed, (xm, u, a), gathered


def _mlp_bwd(dh, dh16, h, gain, w_up, w_down, saved, tag):
    xm, u, a = saved
    du = _mm(dh16, w_down, tb=True, name=f"mlp{tag}_dact", outs=(bf16,), epi=_drelu2_epi, extras=(u,))
    d_down = _wgrad(a, dh16, f"mlp{tag}_dwdown")
    d_up = _wgrad(xm, du, f"mlp{tag}_dwup")
    dh_in, dh_in16, d_gain = _mm(du, w_up, tb=True, name=f"mlp{tag}_dxm", outs=(f32, bf16), accs=1, epi=_dnorm_epi,
                                 extras=(h, dh, gain), tm=ROWWISE_EPI_TM)
    return dh_in, dh_in16, d_gain, d_up, d_down


def _local_step(x, target, w, comm=None):
    w = dict(w)
    t = x.shape[0]
    cos, sgn_sin = _rope_tables(t)
    grads = {}

    xn0 = _rw(lambda xx, g: (_rms(xx, g),), [x], [w["hgrn_norm"]], [(D_MODEL, bf16)], name="hgrn_norm")[0]
    p4 = _mm(xn0, w["hgrn_w4"], name="hgrn_proj", tn=2048)

    if comm is None:
        y, states = _gla_fwd(p4, w["hgrn_lb_logits"], w["hgrn_g_norm"])
    else:
        y, states, gathered = _gla_fwd(p4, w["hgrn_lb_logits"], w["hgrn_g_norm"], gather=comm.shard["gla"])
        w.update(comm.unpack["gla"](gathered))
    h1, xm0 = _residual_out(y, w["hgrn_w_o"], x, [w["mlp_norm"][0:1]], "hgrn_out")
    h2, (hk, xn1), mlp0, gathered = _mlp_fwd(h1, xm0, w["mlp_w_up", 0], w["mlp_w_down", 0], 0,
                                             [w["kv_in_norm"], w["mla_norm"]],
                                             gather=None if comm is None else comm.shard["mlp0_up"])
    if comm is not None:
        w.update(comm.unpack["mlp0_up"](gathered[0]))

    ckr = _mm(hk, w["kv_w_dkv"], name="kv_down")

    def ckv_fn(c, cs, sn, g):
        kr = _rope(c[:, MLA_KV_LORA:], cs, sn)
        return _rms(c[:, :MLA_KV_LORA], g), jnp.concatenate([jnp.zeros_like(kr), kr], axis=1)

    c_kv, kr_head = _rw(ckv_fn, [ckr, cos, sgn_sin], [w["kv_norm"]], [(MLA_KV_LORA, bf16), (ATT_QK, f32)],
                        name="kv_norm_rope")
    kc = _mm(c_kv, w["kv_w_kcat"], name="kv_up_k", outs=(bf16,), extras=(kr_head,),
             epi=lambda r, kr: (r + _tile_lanes(kr, r.shape[1]),))
    v_att = _mm(c_kv, w["kv_w_uv"], name="kv_up_v", outs=(bf16,))
    cq0, c_q = _mm(xn1, w["mla_w_dq"], name="q_down", outs=(f32, bf16), extras=(w["mla_q_norm"],),
                   epi=lambda r, g: (r, _rms(r, g)))
    qc = _mm(c_q, w["mla_w_qcat"], name="q_up", outs=(bf16,), extras=(cos, sgn_sin),
             epi=lambda r, cs, sn: (_rope_heads(r, cs, sn, 1.0, ATT_SCALE),))
    if comm is None:
        o_att, lse = _attn_fwd(qc, kc, v_att)
    else:
        o_att, lse, gathered = _attn_fwd(qc, kc, v_att, gather=comm.shard["attn"])
        w.update(comm.unpack["attn"](gathered))
    h3, xm1 = _residual_out(o_att, w["mla_w_o"], h2, [w["mlp_norm"][1:2]], "mla_out")
    u1, a1 = _mm(xm1, w["mlp_w_up", 1], name="mlp1_up", outs=(bf16, bf16), epi=_relu2_epi)
    mlp1 = (xm1, u1, a1)

    def loss_epi(r, res, tgt, gain):
        def f(a, b):
            e = _rms(a, b) - tgt
            return 0.5 * jnp.sum(jnp.sum(e * e, axis=-1, keepdims=True) / D_MODEL, axis=0, keepdims=True)
        val, vjp = jax.vjp(f, r + res, gain)
        dh, dg = vjp(jnp.ones((1, 1), f32))
        return dh, dh, jnp.broadcast_to(val, (1, D_MODEL)), dg

    dh4, dh4_16, loss_acc, grads["final_norm"] = _mm(
        a1, w["mlp_w_down", 1], name="mlp1_down_loss", outs=(f32, bf16), accs=2, epi=loss_epi,
        extras=(h3, target, w["final_norm"]), tm=ROWWISE_EPI_TM)
    loss = loss_acc[0, 0]

    dh3, dh3_16, g_n1, grads["mlp_w_up", 1], grads["mlp_w_down", 1] = _mlp_bwd(
        dh4, dh4_16, h3, w["mlp_norm"][1:2], w["mlp_w_up", 1], w["mlp_w_down", 1], mlp1, 1)
    do_att = _mm(dh3_16, w["mla_w_o"], tb=True, name="mla_dout", outs=(bf16,))
    grads["mla_w_o"] = _wgrad(o_att, dh3_16, "mla_dwo")

    def delta_fn(a, b):
        prod = a.astype(f32) * b.astype(f32)
        outs = []
        for p in range(MLA_HEADS // 2):
            d0 = jnp.sum(prod[:, 2 * p * LANES:(2 * p + 1) * LANES], axis=-1, keepdims=True)
            d1 = jnp.sum(prod[:, (2 * p + 1) * LANES:(2 * p + 2) * LANES], axis=-1, keepdims=True)
            lo, _ = _pair_masks((a.shape[0], LANES))
            outs.append(jnp.where(lo, d0, d1))
        return (jnp.concatenate(outs, axis=1),)

    delta = _rw(delta_fn, [do_att, o_att], [], [(MLA_HEADS // 2 * LANES, f32)], name="attn_delta")[0]
    if comm is None:
        dqc, dkc, dv = _attn_bwd(qc, kc, v_att, do_att, lse, delta)
    else:
        dqc, dkc, dv, comm.received["attn"] = _attn_bwd(qc, kc, v_att, do_att, lse, delta,
                                                        exchange=comm.reduce(grads, "attn"))
    dqf = _rw(lambda a, cs, sn: (_rope_heads(a, cs, sn, -1.0, ATT_SCALE),), [dqc, cos, sgn_sin], [],
              [(MLA_HEADS * ATT_QK, bf16)], name="dq_rope")[0]
    dcq0, grads["mla_q_norm"] = _mm(dqf, w["mla_w_qcat"], tb=True, name="q_up_dx", outs=(bf16,), accs=1,
                                    extras=(cq0, w["mla_q_norm"]), epi=lambda dy, c, g: _rms_bwd(c, g, dy))
    grads["mla_w_qcat"] = _wgrad(c_q, dqf, "q_up_dw")
    dxn1 = _mm(dcq0, w["mla_w_dq"], tb=True, name="q_down_dx")
    grads["mla_w_dq"] = _wgrad(xn1, dcq0, "q_down_dw")

    dc_kv = _mm(dkc, w["kv_w_kcat"], tb=True, name="kv_up_dx_k")
    dc_kv = _mm(dv, w["kv_w_uv"], tb=True, name="kv_up_dx_v", epi=_add_epi, extras=(dc_kv,))
    grads["kv_w_kcat"] = _wgrad(c_kv, dkc, "kv_up_dw_k")
    grads["kv_w_uv"] = _wgrad(c_kv, dv, "kv_up_dw_v")

    def dckr_fn(c, dc, dk_heads, cs, sn, g):
        tot = dk_heads[:, LANES:ATT_QK].astype(f32)
        for h in range(1, MLA_HEADS):
            tot = tot + dk_heads[:, ATT_QK * h + LANES:ATT_QK * (h + 1)].astype(f32)
        lo, _ = _pair_masks(tot.shape)
        dkr = jnp.where(lo, _rope(tot, cs, sn, -1.0), 0.0)
        dcc, dg = _rms_bwd(c[:, :MLA_KV_LORA], g, dc)
        return jnp.concatenate([dcc, dkr], axis=1), dg

    dckr, grads["kv_norm"] = _rw(dckr_fn, [ckr, dc_kv, dkc, cos, sgn_sin], [w["kv_norm"]],
                                 [(MLA_KV_LORA + LANES, bf16)], [(1, MLA_KV_LORA)], name="kv_dnorm_rope")
    grads["kv_w_dkv"] = _wgrad(hk, dckr, "kv_down_dw")

    def dh2_epi(d1, hh, d2, dres, g1, g2):
        a, ga = _rms_bwd(hh, g1, d1)
        b, gb = _rms_bwd(hh, g2, d2)
        return a + b + dres, a + b + dres, ga, gb

    dh2, dh2_16, grads["kv_in_norm"], grads["mla_norm"] = _mm(
        dckr, w["kv_w_dkv"], tb=True, name="kv_down_dx", outs=(f32, bf16), accs=2, epi=dh2_epi,
        extras=(h2, dxn1, dh3, w["kv_in_norm"], w["mla_norm"]), tm=ROWWISE_EPI_TM)

    dh1, dh1_16, g_n0, grads["mlp_w_up", 0], grads["mlp_w_down", 0] = _mlp_bwd(
        dh2, dh2_16, h1, w["mlp_norm"][0:1], w["mlp_w_up", 0], w["mlp_w_down", 0], mlp0, 0)
    grads["mlp_norm"] = jnp.concatenate([g_n0, g_n1], axis=0)
    dy = _mm(dh1_16, w["hgrn_w_o"], tb=True, name="hgrn_dout")
    grads["hgrn_w_o"] = _wgrad(y, dh1_16, "hgrn_dwo")

    gla_args = (p4, w["hgrn_lb_logits"], w["hgrn_g_norm"], states, dy)
    if comm is None:
        dp4, grads["hgrn_lb_logits"], grads["hgrn_g_norm"] = _gla_bwd(*gla_args)
    else:
        dp4, grads["hgrn_lb_logits"], grads["hgrn_g_norm"], comm.received["gla"] = _gla_bwd(
            *gla_args, exchange=comm.reduce(grads, "gla"))
    grads["hgrn_w4"] = _mm(xn0, dp4, ta=True, name="hgrn_proj_dw")
    grad_x, grads["hgrn_norm"] = _mm(dp4, w["hgrn_w4"], tb=True, name="hgrn_proj_dx", outs=(f32,), accs=1,
                                     epi=lambda *args: _dnorm_epi(*args)[1:], extras=(x, dh1, w["hgrn_norm"]),
                                     tm=ROWWISE_EPI_TM)
    return loss, grad_x, grads


HBM = pl.BlockSpec(memory_space=pltpu.HBM)


def _me():
    return lax.axis_index("x"), lax.axis_index("y"), lax.axis_index("c")


def _flip(x, y, f):
    return (1 - x if f & 1 else x), (1 - y if f & 2 else y)


def _rcopy(src, dst, sems, k, dev):
    return pltpu.make_async_remote_copy(src_ref=src, dst_ref=dst, send_sem=sems.at[0, k], recv_sem=sems.at[1, k],
                                        device_id=dev, device_id_type=MESH)


def _my_half(rows, c, mine=True):
    half = rows // 2
    return pl.ds(pl.multiple_of((c if mine else 1 - c) * half, 16), half)


def _gather_start(wp_ref, out_ref, sems):
    x, y, c = _me()
    half = _my_half(wp_ref.shape[0], c)
    for f in (1, 2, 3):
        px, py = _flip(x, y, f)
        _rcopy(wp_ref.at[half], out_ref.at[2 * x + y, half], sems, f - 1, (px, py, c)).start()


def _gather_finish(wp_ref, out_ref, sems):
    x, y, c = _me()
    half, other = _my_half(wp_ref.shape[0], c), _my_half(wp_ref.shape[0], c, mine=False)
    sends = []
    for f in (1, 2, 3):
        px, py = _flip(x, y, f)
        landed = out_ref.at[2 * px + py, half]
        _rcopy(landed, landed, sems, f - 1, (px, py, c)).wait_recv()
        sends.append(_rcopy(landed, landed, sems, 2 + f, (x, y, 1 - c)))
        sends[-1].start()
    for f in (1, 2, 3):
        px, py = _flip(x, y, f)
        theirs = out_ref.at[2 * px + py, other]
        _rcopy(theirs, theirs, sems, 2 + f, (x, y, 1 - c)).wait_recv()
        sends.append(_rcopy(wp_ref.at[half], out_ref.at[2 * x + y, half], sems, f - 1, (px, py, c)))
    for cp in sends:
        cp.wait_send()


def _gather_base(wp):
    return jnp.broadcast_to(wp[None], (4,) + wp.shape)


def _all_gather_weights(wp, sv):
    def body(wp_ref, sv_ref, base_ref, out_ref, svs_ref, sems, local_sem):
        x, y, c = _me()
        mine = pltpu.make_async_copy(sv_ref, svs_ref.at[2 * x + y], local_sem)
        mine.start()
        _gather_start(wp_ref, out_ref, sems)
        small = []
        for f in (1, 2, 3):
            px, py = _flip(x, y, f)
            small.append(_rcopy(sv_ref, svs_ref.at[2 * x + y], sems, 5 + f, (px, py, c)))
            small[-1].start()
        _gather_finish(wp_ref, out_ref, sems)
        for f in (1, 2, 3):
            px, py = _flip(x, y, f)
            _rcopy(sv_ref, svs_ref.at[2 * px + py], sems, 5 + f, (px, py, c)).wait_recv()
        for cp in small:
            cp.wait_send()
        mine.wait()

    return _pc(body, name="weights_all_gather", in_specs=[HBM, HBM, HBM], out_specs=[HBM, HBM],
               out_shape=[_sds((4,) + wp.shape, bf16), _sds((4, 8, 256), f32)], aliases={2: 0},
               scratch=[pltpu.SemaphoreType.DMA((2, 9)), pltpu.SemaphoreType.DMA])(wp, sv, _gather_base(wp))


def _send_half_to_sibling(gp, name):
    rows = gp.shape[1]

    def body(gp_ref, out_ref, sems):
        x, y, c = _me()
        cp = _rcopy(gp_ref.at[:, _my_half(rows, c, mine=False)], out_ref, sems, 0, (x, y, 1 - c))
        cp.start()
        cp.wait()

    return _pc(body, name=name, in_specs=[HBM], out_specs=HBM, out_shape=_sds((4, rows // 2, D_MODEL), gp.dtype),
               scratch=[pltpu.SemaphoreType.DMA((2, 1))])(gp)


def _chips_start(sb_ref, out_ref, sems):
    x, y, c = _me()
    for f in (1, 2, 3):
        px, py = _flip(x, y, f)
        _rcopy(sb_ref.at[2 * px + py], out_ref.at[f - 1], sems, f - 1, (px, py, c)).start()


def _chips_finish(sb_ref, out_ref, sems):
    x, y, c = _me()
    for f in (1, 2, 3):
        _rcopy(sb_ref.at[0], out_ref.at[f - 1], sems, f - 1, (x, y, c)).wait_recv()
    for f in (1, 2, 3):
        px, py = _flip(x, y, f)
        _rcopy(sb_ref.at[2 * px + py], out_ref.at[f - 1], sems, f - 1, (px, py, c)).wait_send()


def _exchange_chips(sb, small):
    def body(sb_ref, small_ref, out_ref, smalls_ref, sems, local_sem):
        x, y, c = _me()
        me = 4 * x + 2 * y + c
        mine = pltpu.make_async_copy(small_ref, smalls_ref.at[me], local_sem)
        mine.start()
        _chips_start(sb_ref, out_ref, sems)
        sends = []
        for f in range(1, 8):
            px, py = _flip(x, y, f)
            pc = 1 - c if f & 4 else c
            sends.append(_rcopy(small_ref, smalls_ref.at[me], sems, 2 + f, (px, py, pc)))
            sends[-1].start()
        _chips_finish(sb_ref, out_ref, sems)
        for f in range(1, 8):
            px, py = _flip(x, y, f)
            pc = 1 - c if f & 4 else c
            _rcopy(small_ref, smalls_ref.at[4 * px + 2 * py + pc], sems, 2 + f, (x, y, c)).wait_recv()
        for cp in sends:
            cp.wait_send()
        mine.wait()

    return _pc(body, name="grads_exchange_chips", in_specs=[HBM, HBM], out_specs=[HBM, HBM],
               out_shape=[_sds((3,) + sb.shape[1:], bf16), _sds((8, SMALL_ROWS, D_MODEL), f32)],
               scratch=[pltpu.SemaphoreType.DMA((2, 10)), pltpu.SemaphoreType.DMA])(sb, small)


def _exchange_halves(tot, name):
    rows = tot.shape[0]

    def body(tot_ref, out_ref, sems):
        x, y, c = _me()
        half = _my_half(rows, c)
        cp = _rcopy(tot_ref.at[half], out_ref.at[half], sems, 0, (x, y, 1 - c))
        cp.start()
        cp.wait()

    return _pc(body, name=name, in_specs=[HBM], out_specs=HBM, out_shape=_sds((rows, D_MODEL), f32),
               aliases={0: 0}, scratch=[pltpu.SemaphoreType.DMA((2, 1))])(tot)


def _sum_rows(half):
    return max(r for r in range(16, 513, 16) if half % r == 0)


def _sum_over_cores(gp, recv, cq, name):
    half = recv.shape[1]
    tr = _sum_rows(half)
    nb = half // tr

    def body(cq_ref, g_ref, r_ref, o32_ref, o16_ref):
        s = g_ref[...].astype(f32) + r_ref[...].astype(f32)
        o32_ref[...] = s
        o16_ref[...] = s.astype(bf16)

    spec = pl.BlockSpec((1, tr, D_MODEL), lambda b, i, cq_ref: (b, i, 0))
    gs = pltpu.PrefetchScalarGridSpec(
        num_scalar_prefetch=1, grid=(4, nb),
        in_specs=[pl.BlockSpec((1, tr, D_MODEL), lambda b, i, cq_ref: (b, cq_ref[0] * nb + i, 0)), spec],
        out_specs=[spec, spec])
    return _pc(body, name=name, grid_spec=gs, sem=("arbitrary", "arbitrary"),
               out_shape=[_sds((4, half, D_MODEL), f32), _sds((4, half, D_MODEL), bf16)])(cq, gp, recv)


def _sum_over_chips(s32, recv, cq, name):
    half = recv.shape[1]
    tr = _sum_rows(half)
    nb = half // tr

    def body(cq_ref, own_ref, r_ref, o_ref):
        o_ref[...] = ((own_ref[0] + r_ref[0].astype(f32)) + r_ref[1].astype(f32)) + r_ref[2].astype(f32)

    gs = pltpu.PrefetchScalarGridSpec(
        num_scalar_prefetch=1, grid=(nb,),
        in_specs=[pl.BlockSpec((1, tr, D_MODEL), lambda i, cq_ref: (cq_ref[1], i, 0)),
                  pl.BlockSpec((3, tr, D_MODEL), lambda i, cq_ref: (0, i, 0))],
        out_specs=pl.BlockSpec((tr, D_MODEL), lambda i, cq_ref: (cq_ref[0] * nb + i, 0)))
    return _pc(body, name=name, grid_spec=gs, sem=("arbitrary",),
               out_shape=_sds((2 * half, D_MODEL), f32))(cq, s32, recv)


def _sum_small(smalls):
    def body(s_ref, o_ref):
        tot = s_ref[0]
        for d in range(1, 8):
            tot = tot + s_ref[d]
        o_ref[...] = tot

    return _pc(body, name="small_sum", out_shape=_sds((SMALL_ROWS, D_MODEL), f32))(smalls)


def _adamw_math(w, g, m, v):
    m = ADAM_B1 * m + (1.0 - ADAM_B1) * g
    v = ADAM_B2 * v + (1.0 - ADAM_B2) * jnp.square(g)
    m_hat = m / (1.0 - ADAM_B1 ** ADAM_STEP)
    v_hat = v / (1.0 - ADAM_B2 ** ADAM_STEP)
    delta = -ADAM_LR * (m_hat / (jnp.sqrt(v_hat) + ADAM_EPS) + ADAM_WD * w)
    return delta, m, v


def _adamw(w, g, m, v, name):
    cols = w.shape[1]
    return _rw(_adamw_math, [w, g, m, v], [], [(cols, f32)] * 3, name=name, tr=256)


def _adamw_small(items, name):
    n = len(items)

    def body(*refs):
        ins, outs = refs[:4 * n], refs[4 * n:]
        for i in range(n):
            res = _adamw_math(*[r[...] for r in ins[4 * i:4 * i + 4]])
            for o, val in zip(outs[3 * i:3 * i + 3], res):
                o[...] = val

    flat = [a for it in items for a in it]
    out_shape = [_sds(it[0].shape, f32) for it in items for _ in range(3)]
    res = _pc(body, name=name, out_shape=out_shape)(*flat)
    return [tuple(res[3 * i:3 * i + 3]) for i in range(n)]


def _pack_shards(sh, layout, pad):
    parts = [(sh[n] if layer is None else sh[n][layer]).reshape(-1, D_MODEL).astype(bf16) for n, layer, _ in layout]
    if pad:
        parts.append(jnp.zeros((pad, D_MODEL), bf16))
    return jnp.concatenate(parts, axis=0)


def _mlp_full(g4, off, layer):
    o, r = off["mlp_w_up", layer]
    up = g4[:, o:o + r].transpose(1, 0, 2).reshape(D_MODEL, D_FF)
    o, r = off["mlp_w_down", layer]
    return {("mlp_w_up", layer): up, ("mlp_w_down", layer): g4[:, o:o + r].reshape(D_FF, D_MODEL)}


def _unpack_early(g4):
    hg = g4[:, 0:1024].reshape(4, 4, 256, D_MODEL)
    o, r = W_EARLY_OFF["hgrn_w_o", None]
    return {"hgrn_w4": hg.transpose(0, 2, 1, 3).reshape(D_MODEL, 4 * D_MODEL),
            "hgrn_w_o": g4[:, o:o + r].reshape(D_MODEL, D_MODEL)}


def _unpack_gla(g4):
    return _mlp_full(g4, W_GLA_OFF, 0)


def _unpack_last(g4):
    return _mlp_full(g4, W_LAST_OFF, 1)


def _unpack_mid(g4):
    def rows(name):
        o, r = W_MID_OFF[name, None]
        return g4[:, o:o + r]

    w = {"mla_w_dq": rows("mla_w_dq").reshape(D_MODEL, MLA_Q_LORA)}
    uq = rows("mla_w_uq").reshape(4, MLA_Q_LORA, 768).transpose(1, 0, 2).reshape(MLA_Q_LORA, MLA_HEADS, MLA_NOPE + MLA_ROPE)
    w["mla_w_qcat"] = jnp.pad(uq, ((0, 0), (0, 0), (0, ATT_QK - MLA_NOPE - MLA_ROPE))).reshape(MLA_Q_LORA, MLA_HEADS * ATT_QK)
    w["mla_w_o"] = rows("mla_w_o").reshape(MLA_HEADS * MLA_V, D_MODEL)
    dkv = rows("kv_w_dkv").reshape(D_MODEL, MLA_KV_LORA + MLA_ROPE)
    w["kv_w_dkv"] = jnp.pad(dkv, ((0, 0), (0, LANES - MLA_ROPE)))
    uk = rows("kv_w_uk").reshape(4, MLA_KV_LORA, 512).transpose(1, 0, 2).reshape(MLA_KV_LORA, MLA_HEADS, MLA_NOPE)
    w["kv_w_kcat"] = jnp.pad(uk, ((0, 0), (0, 0), (0, ATT_QK - MLA_NOPE))).reshape(MLA_KV_LORA, MLA_HEADS * ATT_QK)
    w["kv_w_uv"] = rows("kv_w_uv").reshape(4, MLA_KV_LORA, 512).transpose(1, 0, 2).reshape(MLA_KV_LORA, MLA_HEADS * MLA_V)
    return w


def _pack_grads_late(g):
    return g["hgrn_w4"].reshape(4, 256, 4, D_MODEL).transpose(0, 2, 1, 3).reshape(4, G_LATE_ROWS, D_MODEL)


def _grad_rows(g, name, layer):
    if name in ("mlp_w_up", "mlp_w_down"):
        full = g[name, layer]
        return full.reshape(D_MODEL, 4, 1024).transpose(1, 0, 2) if name == "mlp_w_up" else full.reshape(4, 1024, D_MODEL)
    if name == "mla_w_uq":
        uq = g["mla_w_qcat"].reshape(MLA_Q_LORA, MLA_HEADS, ATT_QK)[:, :, :MLA_NOPE + MLA_ROPE]
        return uq.reshape(MLA_Q_LORA, 4, 768).transpose(1, 0, 2).reshape(4, 192, D_MODEL)
    if name == "kv_w_uk":
        uk = g["kv_w_kcat"].reshape(MLA_KV_LORA, MLA_HEADS, ATT_QK)[:, :, :MLA_NOPE]
        return uk.reshape(MLA_KV_LORA, 4, 512).transpose(1, 0, 2).reshape(4, 128, D_MODEL)
    if name == "kv_w_uv":
        return g[name].reshape(MLA_KV_LORA, 4, 512).transpose(1, 0, 2).reshape(4, 128, D_MODEL)
    if name == "kv_w_dkv":
        return g[name][:, :MLA_KV_LORA + MLA_ROPE].reshape(4, 80, D_MODEL)
    return g[name].reshape(4, -1, D_MODEL)


def _pack_grads(g, layout):
    parts = [_grad_rows(g, name, layer) for name, layer, _ in layout]
    if layout in PADDED:
        parts.append(jnp.zeros((4, PACK_PAD, D_MODEL), bf16))
    return jnp.concatenate(parts, axis=1)


LOSS_ROW = 11


def _pack_small(g, loss):
    rows = []
    for name, _, r, wd in SMALL:
        a = g[name].reshape(r, wd)
        rows.append(jnp.pad(a, ((0, 0), (0, D_MODEL - wd))) if wd < D_MODEL else a)
    assert sum(r for _, _, r, _ in SMALL) == LOSS_ROW
    rows.append(jnp.full((1, D_MODEL), loss, f32))
    rows.append(jnp.zeros((SMALL_ROWS - LOSS_ROW - 1, D_MODEL), f32))
    return jnp.concatenate(rows, axis=0)


def kernel(x, hgrn_norm, hgrn_w_q, hgrn_w_f, hgrn_w_i, hgrn_w_g, hgrn_g_norm, hgrn_w_o, hgrn_lb_logits, mla_norm, mla_w_dq, mla_q_norm, mla_w_uq, mla_w_o, kv_in_norm, kv_w_dkv, kv_norm, kv_w_uk, kv_w_uv, mlp_norm, mlp_w_up, mlp_w_down, final_norm, loss_target, m_hgrn_norm, m_hgrn_w_q, m_hgrn_w_f, m_hgrn_w_i, m_hgrn_w_g, m_hgrn_g_norm, m_hgrn_w_o, m_hgrn_lb_logits, m_mla_norm, m_mla_w_dq, m_mla_q_norm, m_mla_w_uq, m_mla_w_o, m_kv_in_norm, m_kv_w_dkv, m_kv_norm, m_kv_w_uk, m_kv_w_uv, m_mlp_norm, m_mlp_w_up, m_mlp_w_down, m_final_norm, v_hgrn_norm, v_hgrn_w_q, v_hgrn_w_f, v_hgrn_w_i, v_hgrn_w_g, v_hgrn_g_norm, v_hgrn_w_o, v_hgrn_lb_logits, v_mla_norm, v_mla_w_dq, v_mla_q_norm, v_mla_w_uq, v_mla_w_o, v_kv_in_norm, v_kv_w_dkv, v_kv_norm, v_kv_w_uk, v_kv_w_uv, v_mlp_norm, v_mlp_w_up, v_mlp_w_down, v_final_norm):
    given = dict(locals())
    wsh = {n: given[n] for n in WEIGHTS}
    msh = {n: given["m_" + n] for n in WEIGHTS}
    vsh = {n: given["v_" + n] for n in WEIGHTS}
    xi, yi, ci = _me()
    chip = 2 * xi + yi
    cq = jnp.stack([ci, chip]).astype(jnp.int32)

    small_w = {n: wsh[n].reshape(r, -1) for n, _, r, _ in SMALL}
    sv = jnp.concatenate([small_w["hgrn_norm"], small_w["hgrn_lb_logits"], jnp.zeros((5, 256), f32)], axis=0)
    g4, sv4 = _all_gather_weights(_pack_shards(wsh, W_EARLY, 0), sv)
    w = _unpack_early(g4)
    w["hgrn_norm"] = sv4[:, 0, :].reshape(1, D_MODEL)
    w["hgrn_lb_logits"] = sv4[:, 1:3, :].transpose(1, 0, 2).reshape(2, D_MODEL)
    for n in ("hgrn_g_norm", "mla_norm", "mla_q_norm", "kv_in_norm", "kv_norm", "mlp_norm", "final_norm"):
        w[n] = small_w[n]

    class Comm:
        shard = {"gla": _pack_shards(wsh, W_GLA, 0), "mlp0_up": _pack_shards(wsh, W_MID, PACK_PAD),
                 "attn": _pack_shards(wsh, W_LAST, 0)}
        unpack = {"gla": _unpack_gla, "mlp0_up": _unpack_mid, "attn": _unpack_last}
        layout = {"gla": G_GLA, "attn": G_ATTN}
        received, s32 = {}, {}

        @staticmethod
        def reduce(grads, part):
            gp = _pack_grads(grads, Comm.layout[part])
            Comm.s32[part], s16 = _sum_over_cores(gp, _send_half_to_sibling(gp, "grads_to_sibling_" + part), cq,
                                                  "grads_sum_cores_" + part)
            return s16

    loss, grad_x, g = _local_step(x.reshape(-1, D_MODEL), loss_target.reshape(-1, D_MODEL), w, Comm)

    total = {part: _exchange_halves(_sum_over_chips(Comm.s32[part], Comm.received[part], cq, "grads_sum_chips_" + part),
                                    "grads_exchange_halves_" + part) for part in ("attn", "gla")}
    gp = _pack_grads_late(g)
    s32, s16 = _sum_over_cores(gp, _send_half_to_sibling(gp, "grads_to_sibling_late"), cq, "grads_sum_cores_late")
    from_chips, smalls = _exchange_chips(s16, _pack_small(g, loss))
    total["late"] = _exchange_halves(_sum_over_chips(s32, from_chips, cq, "grads_sum_chips_late"),
                                     "grads_exchange_halves_late")
    small_tot = _sum_small(smalls)
    loss = small_tot[LOSS_ROW, 0]

    where = {}
    for part, offsets in (("late", G_LATE_OFF), ("gla", G_GLA_OFF), ("attn", G_ATTN_OFF)):
        for (n, layer), (o, r) in offsets.items():
            where.setdefault(n, []).append(total[part][o:o + r])
    grad, delta, new_m, new_v = {}, {}, {}, {}
    groups = {"hgrn": [], "mla_kv": []}
    for n, pieces in where.items():
        shp = wsh[n].shape
        two_d = (-1, shp[-1])
        grad[n] = (pieces[0] if len(pieces) == 1 else jnp.concatenate(pieces, axis=0)).reshape(shp)
        operands = (wsh[n].reshape(two_d), grad[n].reshape(two_d), msh[n].reshape(two_d), vsh[n].reshape(two_d))
        if n.startswith("mlp"):
            res = _adamw(*operands, "adamw_" + n)
            delta[n], new_m[n], new_v[n] = (a.reshape(shp) for a in res)
        else:
            groups["hgrn" if n.startswith("hgrn") else "mla_kv"].append((n, operands))
    for gname, members in groups.items():
        for (n, _), res in zip(members, _adamw_small([ops for _, ops in members], "adamw_" + gname)):
            delta[n], new_m[n], new_v[n] = (a.reshape(wsh[n].shape) for a in res)
    items = []
    for n, row, r, wd in SMALL:
        gs = small_tot[row:row + r, :wd]
        if n in ("hgrn_norm", "hgrn_lb_logits"):
            gs = lax.dynamic_slice(gs, (0, 256 * chip), (r, 256))
        grad[n] = gs.reshape(wsh[n].shape)
        items.append((small_w[n], gs, msh[n].reshape(gs.shape), vsh[n].reshape(gs.shape)))
    for (n, _, _, _), (d, m2, v2) in zip(SMALL, _adamw_small(items, "adamw_small")):
        shp = wsh[n].shape
        delta[n], new_m[n], new_v[n] = d.reshape(shp), m2.reshape(shp), v2.reshape(shp)

    return (loss, grad_x.reshape(x.shape), *[grad[n] for n in WEIGHTS], *[delta[n] for n in WEIGHTS],
            *[new_m[n] for n in WEIGHTS], *[new_v[n] for n in WEIGHTS])
```

```python
import functools

import jax
import jax.numpy as jnp
from jax import lax
from jax.experimental import pallas as pl
from jax.experimental.pallas import tpu as pltpu

f32, bf16 = jnp.float32, jnp.bfloat16
HI = lax.Precision.HIGHEST
MESH = pl.DeviceIdType.MESH

D_MODEL = 1024
D_FF = 4096
EPS = 1e-6
HGRN_HEADS, HGRN_DK, HGRN_CHUNK, HGRN_SUB = 8, 128, 64, 16
MLA_HEADS, MLA_NOPE, MLA_ROPE, MLA_V = 16, 128, 64, 128
MLA_Q_LORA, MLA_KV_LORA = 256, 256
ROPE_THETA = 10000.0
ATT_SCALE = (MLA_NOPE + MLA_ROPE) ** -0.5
EXP_CLAMP = 80.0

ADAM_LR, ADAM_B1, ADAM_B2, ADAM_EPS, ADAM_WD, ADAM_STEP = 0.001, 0.9, 0.999, 1e-08, 0.01, 10

V7X_VMEM_BYTES = 64 * 1024 * 1024
VMEM_LIMIT = V7X_VMEM_BYTES - 8 * 1024 * 1024
LANES = 128

PACK_PAD = 16
W_EARLY = (("hgrn_w_q", None, 256), ("hgrn_w_f", None, 256), ("hgrn_w_i", None, 256), ("hgrn_w_g", None, 256),
           ("hgrn_w_o", None, 256))
W_GLA = (("mlp_w_up", 0, 1024), ("mlp_w_down", 0, 1024))
W_MID = (("mla_w_dq", None, 64), ("mla_w_uq", None, 192), ("mla_w_o", None, 512), ("kv_w_dkv", None, 80),
         ("kv_w_uk", None, 128), ("kv_w_uv", None, 128))
W_LAST = (("mlp_w_up", 1, 1024), ("mlp_w_down", 1, 1024))
G_LATE = (("hgrn_w_q", None, 256), ("hgrn_w_f", None, 256), ("hgrn_w_i", None, 256), ("hgrn_w_g", None, 256))
G_ATTN = (("mla_w_o", None, 512), ("mlp_w_up", 1, 1024), ("mlp_w_down", 1, 1024))
G_GLA = (("hgrn_w_o", None, 256), ("mla_w_dq", None, 64), ("mla_w_uq", None, 192), ("kv_w_dkv", None, 80),
         ("kv_w_uk", None, 128), ("kv_w_uv", None, 128), ("mlp_w_up", 0, 1024), ("mlp_w_down", 0, 1024))
PADDED = (W_MID, G_GLA)


def _offsets(layout):
    out, o = {}, 0
    for name, layer, rows in layout:
        out[name, layer] = (o, rows)
        o += rows
    return out, o + (PACK_PAD if layout in PADDED else 0)


W_EARLY_OFF, W_EARLY_ROWS = _offsets(W_EARLY)
W_GLA_OFF, W_GLA_ROWS = _offsets(W_GLA)
W_MID_OFF, W_MID_ROWS = _offsets(W_MID)
W_LAST_OFF, W_LAST_ROWS = _offsets(W_LAST)
G_LATE_OFF, G_LATE_ROWS = _offsets(G_LATE)
G_ATTN_OFF, G_ATTN_ROWS = _offsets(G_ATTN)
G_GLA_OFF, G_GLA_ROWS = _offsets(G_GLA)
assert all(r % 32 == 0 for r in (W_EARLY_ROWS, W_MID_ROWS, W_LAST_ROWS, G_LATE_ROWS, G_ATTN_ROWS, G_GLA_ROWS))

WEIGHTS = ("hgrn_norm", "hgrn_w_q", "hgrn_w_f", "hgrn_w_i", "hgrn_w_g", "hgrn_g_norm", "hgrn_w_o", "hgrn_lb_logits",
           "mla_norm", "mla_w_dq", "mla_q_norm", "mla_w_uq", "mla_w_o", "kv_in_norm", "kv_w_dkv", "kv_norm", "kv_w_uk",
           "kv_w_uv", "mlp_norm", "mlp_w_up", "mlp_w_down", "final_norm")
SMALL = (("hgrn_norm", 0, 1, 1024), ("hgrn_lb_logits", 1, 2, 1024), ("hgrn_g_norm", 3, 1, 128),
         ("mla_norm", 4, 1, 1024), ("mla_q_norm", 5, 1, 256), ("kv_in_norm", 6, 1, 1024), ("kv_norm", 7, 1, 256),
         ("mlp_norm", 8, 2, 1024), ("final_norm", 10, 1, 1024))
SMALL_ROWS = 16


def _pc(body, *, name, out_shape, grid=None, in_specs=None, out_specs=None, scratch=(), sem=None, grid_spec=None,
        aliases=None):
    params = pltpu.CompilerParams(dimension_semantics=sem, vmem_limit_bytes=VMEM_LIMIT)
    if grid_spec is not None:
        return pl.pallas_call(body, name=name, out_shape=out_shape, grid_spec=grid_spec, compiler_params=params,
                              interpret=False)
    kw = {k: v for k, v in (("grid", grid), ("in_specs", in_specs), ("out_specs", out_specs),
                            ("input_output_aliases", aliases)) if v is not None}
    return pl.pallas_call(body, name=name, out_shape=out_shape, scratch_shapes=list(scratch), compiler_params=params,
                          interpret=False, **kw)


def _sds(shape, dtype):
    return jax.ShapeDtypeStruct(tuple(shape), dtype)


def _mm(a, b, *, name, ta=False, tb=False, outs=(f32,), epi=None, extras=(), accs=0, gather=None, exchange=None,
        tm=1024, tn=1024, tk=4096):
    m, k = (a.shape[1], a.shape[0]) if ta else a.shape
    n = b.shape[0] if tb else b.shape[1]
    tm, tn, tk = min(tm, m), min(tn, n), min(tk, k)
    assert m % tm == 0 and n % tn == 0 and k % tk == 0, (name, m, n, k)
    nk = k // tk
    assert accs == 0 or (tn == n and nk == 1), name
    a_spec = pl.BlockSpec((tk, tm), lambda i, j, kk: (kk, i)) if ta else pl.BlockSpec((tm, tk), lambda i, j, kk: (i, kk))
    b_spec = pl.BlockSpec((tn, tk), lambda i, j, kk: (j, kk)) if tb else pl.BlockSpec((tk, tn), lambda i, j, kk: (kk, j))

    def extra_spec(e):
        if e.shape == (m, n):
            return pl.BlockSpec((tm, tn), lambda i, j, kk: (i, j))
        if e.shape[0] == m:
            return pl.BlockSpec((tm, e.shape[1]), lambda i, j, kk: (i, 0))
        return pl.BlockSpec((e.shape[0], tn), lambda i, j, kk: (0, j))

    e_specs = [extra_spec(e) for e in extras]
    dn = (((0 if ta else 1,), (1 if tb else 0,)), ((), ()))
    n_e, n_o = len(extras), len(outs)

    def finish(r, e_refs, o_refs):
        res = epi(r, *[e[...] for e in e_refs]) if epi is not None else (r,)
        for o, v in zip(o_refs[:n_o], res[:n_o]):
            o[...] = v.astype(o.dtype)
        for o, v in zip(o_refs[n_o:], res[n_o:]):
            @pl.when(pl.program_id(0) == 0)
            def _(o=o):
                o[...] = jnp.zeros_like(o)
            o[...] += v

    grid = (m // tm, n // tn, nk)
    assert gather is None or exchange is None
    n_g = 2 if gather is not None else (1 if exchange is not None else 0)

    def body(*refs):
        a_ref, b_ref = refs[0], refs[1]
        e_refs = refs[2:2 + n_e]
        o_refs = refs[2 + n_e + n_g:2 + n_e + n_g + n_o + accs]
        if n_g:
            src_ref, dst_ref, sems = refs[2 + n_e], refs[2 + n_e + n_g + n_o + accs], refs[-1]
            start, fin = (_gather_start, _gather_finish) if gather is not None else (_chips_start, _chips_finish)
            pid = [pl.program_id(d) for d in range(3)]
            pl.when((pid[0] == 0) & (pid[1] == 0) & (pid[2] == 0))(lambda: start(src_ref, dst_ref, sems))
            pl.when((pid[0] == grid[0] - 1) & (pid[1] == grid[1] - 1) & (pid[2] == grid[2] - 1))(
                lambda: fin(src_ref, dst_ref, sems))
        prod = lax.dot_general(a_ref[...].astype(bf16), b_ref[...].astype(bf16), dn, preferred_element_type=f32)
        if nk == 1:
            finish(prod, e_refs, o_refs)
            return
        acc = refs[2 + n_e + n_g + n_o + accs + (1 if n_g else 0)]
        kk = pl.program_id(2)

        @pl.when(kk == 0)
        def _():
            acc[...] = jnp.zeros_like(acc)

        acc[...] += prod

        @pl.when(kk == nk - 1)
        def _():
            finish(acc[...], e_refs, o_refs)

    out_specs = ([pl.BlockSpec((tm, tn), lambda i, j, kk: (i, j)) for _ in outs] +
                 [pl.BlockSpec((1, n), lambda i, j, kk: (0, 0))] * accs)
    out_shape = [_sds((m, n), dt) for dt in outs] + [_sds((1, n), f32)] * accs
    scratch = [pltpu.VMEM((tm, tn), f32)] if nk > 1 else []
    if not n_g:
        out = _pc(body, name=name, grid=grid, in_specs=[a_spec, b_spec] + e_specs, out_specs=out_specs,
                  out_shape=out_shape, scratch=scratch,
                  sem=("arbitrary" if accs else "parallel", "parallel", "arbitrary"))(a, b, *extras)
    elif exchange is not None:
        out = _pc(body, name=name + "_exchange", grid=grid, in_specs=[a_spec, b_spec] + e_specs + [HBM],
                  out_specs=out_specs + [HBM], out_shape=out_shape + [_sds((3,) + exchange.shape[1:], bf16)],
                  scratch=scratch + [pltpu.SemaphoreType.DMA((2, 3))], sem=("arbitrary",) * 3)(a, b, *extras, exchange)
    else:
        out = _pc(body, name=name + "_gather", grid=grid, in_specs=[a_spec, b_spec] + e_specs + [HBM, HBM],
                  out_specs=out_specs + [HBM], out_shape=out_shape + [_sds((4,) + gather.shape, bf16)],
                  aliases={2 + n_e + 1: n_o + accs}, scratch=scratch + [pltpu.SemaphoreType.DMA((2, 6))],
                  sem=("arbitrary",) * 3)(a, b, *extras, gather, _gather_base(gather))
    return out[0] if len(out) == 1 else out


def _wgrad(a, b, name):
    return _mm(a, b, ta=True, name=name, outs=(bf16,))


def _rw(fn, rows, bcast, outs, accs=(), *, name, tr=256):
    t = rows[0].shape[0]
    tr = min(tr, t)
    assert t % tr == 0
    n_r, n_b, n_o, n_a = len(rows), len(bcast), len(outs), len(accs)

    def body(*refs):
        r_refs = refs[:n_r]
        b_refs = refs[n_r:n_r + n_b]
        o_refs = refs[n_r + n_b:n_r + n_b + n_o]
        a_refs = refs[n_r + n_b + n_o:]
        res = fn(*[r[...] for r in r_refs], *[b[...] for b in b_refs])
        for o, v in zip(o_refs, res[:n_o]):
            o[...] = v.astype(o.dtype)
        i = pl.program_id(0)
        for a_ref, v in zip(a_refs, res[n_o:]):
            @pl.when(i == 0)
            def _(a_ref=a_ref):
                a_ref[...] = jnp.zeros_like(a_ref)
            a_ref[...] += v

    in_specs = [pl.BlockSpec((tr, r.shape[1]), lambda i: (i, 0)) for r in rows]
    in_specs += [pl.BlockSpec(b.shape, lambda i: (0, 0)) for b in bcast]
    out_specs = [pl.BlockSpec((tr, w), lambda i: (i, 0)) for w, _ in outs]
    out_specs += [pl.BlockSpec(s, lambda i: (0, 0)) for s in accs]
    out_shape = [_sds((t, w), dt) for w, dt in outs] + [_sds(s, f32) for s in accs]
    res = _pc(body, name=name, grid=(t // tr,), in_specs=in_specs, out_specs=out_specs, out_shape=out_shape,
              sem=("arbitrary",))(*rows, *bcast)
    return res


def _rms(x, gain):
    return x * lax.rsqrt(jnp.mean(x * x, axis=-1, keepdims=True) + EPS) * gain


def _rms_bwd(x, gain, dy):
    _, vjp = jax.vjp(_rms, x, gain)
    return vjp(dy)


def _lower_bound(lbl):
    l0, l1 = lbl[0:1, :], lbl[1:2, :]
    mx = jnp.maximum(l0, l1)
    e0, e1 = jnp.exp(l0 - mx), jnp.exp(l1 - mx)
    return e0 / (e0 + e1)


def _gates(qpre, fpre, lbl):
    lb = _lower_bound(lbl)
    q = jax.nn.silu(qpre)
    forget = lb + (1.0 - lb) * jax.nn.sigmoid(fpre)
    return q, 1.0 - forget, jnp.log(forget)


def _head_norm_gate(o, gpre, gn):
    return _rms(o, gn) * jax.nn.silu(gpre)


def _swap_halves(x):
    w = x.shape[1]
    lane = lax.broadcasted_iota(jnp.int32, x.shape, 1)
    return jnp.where((lane % MLA_ROPE) < MLA_ROPE // 2, pltpu.roll(x, w - MLA_ROPE // 2, 1),
                     pltpu.roll(x, MLA_ROPE // 2, 1))


def _tile_lanes(tab, w):
    return tab if w == tab.shape[1] else jnp.concatenate([tab] * (w // tab.shape[1]), axis=1)


def _rope(x, cos, sgn_sin, sign=1.0):
    w = x.shape[1]
    return x * _tile_lanes(cos, w) + sign * _swap_halves(x) * _tile_lanes(sgn_sin, w)


def _rope_heads(x, cos, sgn_sin, sign, scale):
    parts = []
    for h in range(x.shape[1] // (2 * LANES)):
        parts.append(x[:, 2 * LANES * h:2 * LANES * h + LANES] * scale)
        parts.append(_rope(x[:, 2 * LANES * h + LANES:2 * LANES * (h + 1)], cos, sgn_sin, sign) * scale)
    return jnp.concatenate(parts, axis=1)


def _bd(a, b, ca, cb):
    return lax.dot_general(a.astype(bf16), b.astype(bf16), (((ca,), (cb,)), ((), ())), preferred_element_type=f32)


@jax.custom_vjp
def _dot_nn(a, b):
    return _bd(a, b, 1, 0)


@jax.custom_vjp
def _dot_nt(a, b):
    return _bd(a, b, 1, 1)


@jax.custom_vjp
def _dot_tn(a, b):
    return _bd(a, b, 0, 0)


_dot_nn.defvjp(lambda a, b: (_bd(a, b, 1, 0), (a, b)), lambda r, g: (_bd(g, r[1], 1, 1), _bd(r[0], g, 0, 0)))
_dot_nt.defvjp(lambda a, b: (_bd(a, b, 1, 1), (a, b)), lambda r, g: (_bd(g, r[1], 1, 0), _bd(g, r[0], 0, 0)))
_dot_tn.defvjp(lambda a, b: (_bd(a, b, 0, 0), (a, b)), lambda r, g: (_bd(r[1], g, 1, 1), _bd(r[0], g, 1, 0)))


def _scan_rows(x, reverse):
    n = x.shape[0]
    row = lax.broadcasted_iota(jnp.int32, x.shape, 0)
    s = 1
    while s < n:
        if reverse:
            x = x + jnp.where(row < n - s, pltpu.roll(x, n - s, 0), 0.0)
        else:
            x = x + jnp.where(row >= s, pltpu.roll(x, s, 0), 0.0)
        s *= 2
    return x


@jax.custom_vjp
def _cumsum_rows(g):
    return _scan_rows(g, False)


_cumsum_rows.defvjp(lambda g: (_scan_rows(g, False), None), lambda _, ct: (_scan_rows(ct, True),))

HGRN_PAIRS = HGRN_HEADS // 2
HGRN_PAIR = 2 * HGRN_DK
GLA_STATE = (HGRN_PAIRS, HGRN_PAIR, HGRN_PAIR)


def _gla_consts():
    s = HGRN_SUB
    r = lax.broadcasted_iota(jnp.int32, (HGRN_PAIR, HGRN_PAIR), 0)
    c = lax.broadcasted_iota(jnp.int32, (HGRN_PAIR, HGRN_PAIR), 1)
    pair_mask = (r < HGRN_DK) == (c < HGRN_DK)
    masks = []
    for i in range(HGRN_CHUNK // s):
        n = s * (i + 1)
        row = lax.broadcasted_iota(jnp.int32, (HGRN_HEADS * s, HGRN_HEADS * n), 0)
        col = lax.broadcasted_iota(jnp.int32, (HGRN_HEADS * s, HGRN_HEADS * n), 1)
        col_head = sum((col >= m * n).astype(jnp.int32) for m in range(1, HGRN_HEADS))
        masks.append((col_head == row // s) & (col - col_head * n <= s * i + row % s))
    return pair_mask, masks


def _heads_to_rows(x):
    return jnp.concatenate([x[:, HGRN_DK * h:HGRN_DK * (h + 1)] for h in range(HGRN_HEADS)], axis=0)


def _gla_chunk(consts, dots, q, k, v, g, st):
    pair_mask, masks = consts
    dot_nn, dot_nt, dot_tn = dots
    c, s = HGRN_CHUNK, HGRN_SUB
    b = _cumsum_rows(g)
    b_last = b[c - 1:c, :]
    q_in, k_out = q * jnp.exp(b), k * jnp.exp(b_last - b)
    o_inter, st_new = [], []
    for p in range(HGRN_PAIRS):
        cols = slice(HGRN_PAIR * p, HGRN_PAIR * (p + 1))
        o_inter.append(dot_nt(q_in[:, cols], st[p]))
        st_new.append(st[p] * jnp.exp(b_last[:, cols]) + jnp.where(pair_mask, dot_tn(v[:, cols], k_out[:, cols]), 0.0))
    intra = []
    for i in range(c // s):
        n = s * (i + 1)
        ref = b[s * i - 1:s * i, :] if i else jnp.zeros_like(b_last)
        qt = _heads_to_rows(q[s * i:n] * jnp.exp(b[s * i:n] - ref))
        kt = _heads_to_rows(k[:n] * jnp.exp(jnp.minimum(ref - b[:n], EXP_CLAMP)))
        sc = jnp.where(masks[i], dot_nt(qt, kt), 0.0)
        oi = dot_nn(sc, _heads_to_rows(v[:n]))
        intra.append(jnp.concatenate([oi[s * h:s * (h + 1)] for h in range(HGRN_HEADS)], axis=1))
    return jnp.concatenate(o_inter, axis=1) + jnp.concatenate(intra, axis=0), st_new


_PLAIN_DOTS = (lambda a, b: _bd(a, b, 1, 0), lambda a, b: _bd(a, b, 1, 1), lambda a, b: _bd(a, b, 0, 0))
_VJP_DOTS = (_dot_nn, _dot_nt, _dot_tn)


def _hgrn_mix(consts, dots, qpre, fpre, v, gpre, lbl, gn, st):
    q, k, g = _gates(qpre, fpre, lbl)
    o, st_new = _gla_chunk(consts, dots, q, k, v, g, st)
    y = [_head_norm_gate(o[:, HGRN_DK * h:HGRN_DK * (h + 1)], gpre[:, HGRN_DK * h:HGRN_DK * (h + 1)], gn)
         for h in range(HGRN_HEADS)]
    return jnp.concatenate(y, axis=1), st_new


def _gla_fwd(p4, lbl, gn, gather=None):
    t = p4.shape[0]
    nc = t // HGRN_CHUNK

    def body(q_ref, k_ref, v_ref, g_ref, lbl_ref, gn_ref, *rest):
        if gather is None:
            o_ref, s_ref, st = rest
        else:
            wp_ref, _, o_ref, s_ref, gathered_ref, st, sems = rest

        @pl.when(pl.program_id(0) == 0)
        def _():
            st[...] = jnp.zeros_like(st)
            if gather is not None:
                _gather_start(wp_ref, gathered_ref, sems)

        if gather is not None:
            @pl.when(pl.program_id(0) == nc - 1)
            def _():
                _gather_finish(wp_ref, gathered_ref, sems)

        s_in = [st[p] for p in range(HGRN_PAIRS)]
        y, st_new = _hgrn_mix(_gla_consts(), _PLAIN_DOTS, q_ref[...], k_ref[...], v_ref[...], g_ref[...],
                              lbl_ref[...], gn_ref[...], s_in)
        o_ref[...] = y.astype(o_ref.dtype)
        for p in range(HGRN_PAIRS):
            s_ref[0, p] = s_in[p]
            st[p] = st_new[p]

    blk = lambda off: pl.BlockSpec((HGRN_CHUNK, D_MODEL), lambda c: (c, off))
    whole = lambda a: pl.BlockSpec(a.shape, lambda c: (0, 0))
    state_shape = GLA_STATE
    in_specs = [blk(0), blk(1), blk(2), blk(3), whole(lbl), whole(gn)]
    out_specs = [blk(0), pl.BlockSpec((1,) + state_shape, lambda c: (c, 0, 0, 0))]
    out_shape = [_sds((t, D_MODEL), bf16), _sds((nc,) + state_shape, f32)]
    scratch = [pltpu.VMEM(state_shape, f32)]
    if gather is None:
        return _pc(body, name="gla_fwd", grid=(nc,), in_specs=in_specs, out_specs=out_specs, out_shape=out_shape,
                   scratch=scratch, sem=("arbitrary",))(p4, p4, p4, p4, lbl, gn)
    return _pc(body, name="gla_fwd_gather", grid=(nc,), in_specs=in_specs + [HBM, HBM], out_specs=out_specs + [HBM],
               out_shape=out_shape + [_sds((4,) + gather.shape, bf16)], aliases={7: 2},
               scratch=scratch + [pltpu.SemaphoreType.DMA((2, 6))], sem=("arbitrary",))(
                   p4, p4, p4, p4, lbl, gn, gather, _gather_base(gather))


def _gla_bwd(p4, lbl, gn, states, dy, exchange=None):
    t = p4.shape[0]
    nc = t // HGRN_CHUNK

    def body(q_ref, k_ref, v_ref, g_ref, lbl_ref, gn_ref, s_ref, dy_ref, *rest):
        if exchange is None:
            dp_ref, dlbl_ref, dgn_ref, dst = rest
        else:
            sb_ref, dp_ref, dlbl_ref, dgn_ref, recv_ref, dst, sems = rest

        @pl.when(pl.program_id(0) == 0)
        def _():
            dst[...] = jnp.zeros_like(dst)
            dlbl_ref[...] = jnp.zeros_like(dlbl_ref)
            dgn_ref[...] = jnp.zeros_like(dgn_ref)
            if exchange is not None:
                _chips_start(sb_ref, recv_ref, sems)

        if exchange is not None:
            @pl.when(pl.program_id(0) == nc - 1)
            def _():
                _chips_finish(sb_ref, recv_ref, sems)

        consts = _gla_consts()
        fn = lambda *args: _hgrn_mix(consts, _VJP_DOTS, *args)
        pairs = range(HGRN_PAIRS)
        _, vjp = jax.vjp(fn, q_ref[...], k_ref[...], v_ref[...], g_ref[...], lbl_ref[...], gn_ref[...],
                         [s_ref[0, p] for p in pairs])
        *d_proj, dlbl, dgn, ds = vjp((dy_ref[...], [dst[p] for p in pairs]))
        for i, d in enumerate(d_proj):
            dp_ref[:, D_MODEL * i:D_MODEL * (i + 1)] = d.astype(dp_ref.dtype)
        dlbl_ref[...] += dlbl
        dgn_ref[...] += dgn
        for p in pairs:
            dst[p] = ds[p]

    blk = lambda off: pl.BlockSpec((HGRN_CHUNK, D_MODEL), lambda c: (nc - 1 - c, off))
    whole = lambda a: pl.BlockSpec(a.shape, lambda c: (0, 0))
    state_shape = GLA_STATE
    in_specs = [blk(0), blk(1), blk(2), blk(3), whole(lbl), whole(gn),
                pl.BlockSpec((1,) + state_shape, lambda c: (nc - 1 - c, 0, 0, 0)), blk(0)]
    out_specs = [pl.BlockSpec((HGRN_CHUNK, 4 * D_MODEL), lambda c: (nc - 1 - c, 0)), whole(lbl), whole(gn)]
    out_shape = [_sds((t, 4 * D_MODEL), bf16), _sds(lbl.shape, f32), _sds(gn.shape, f32)]
    scratch = [pltpu.VMEM(state_shape, f32)]
    if exchange is None:
        return _pc(body, name="gla_bwd", grid=(nc,), in_specs=in_specs, out_specs=out_specs, out_shape=out_shape,
                   scratch=scratch, sem=("arbitrary",))(p4, p4, p4, p4, lbl, gn, states, dy)
    return _pc(body, name="gla_bwd_exchange", grid=(nc,), in_specs=in_specs + [HBM], out_specs=out_specs + [HBM],
               out_shape=out_shape + [_sds((3,) + exchange.shape[1:], bf16)],
               scratch=scratch + [pltpu.SemaphoreType.DMA((2, 3))], sem=("arbitrary",))(
                   p4, p4, p4, p4, lbl, gn, states, dy, exchange)


ATT_FWD_TQ, ATT_FWD_TK = 1024, 1024
ATT_BWD_TQ, ATT_BWD_TK = 1024, 512
ATT_QK = 2 * LANES
NEG = -1e30


def _pair_masks(shape):
    lane = lax.broadcasted_iota(jnp.int32, shape, 1)
    return lane < MLA_ROPE, lane >= MLA_ROPE


def _causal(shape, row0, col0):
    row = row0 + lax.broadcasted_iota(jnp.int32, shape, 0)
    col = col0 + lax.broadcasted_iota(jnp.int32, shape, 1)
    return col <= row


def _qk_cols(e):
    return slice(ATT_QK * e, ATT_QK * (e + 1))


def _v_cols(e):
    return slice(MLA_V * e, MLA_V * (e + 1))


def _first_last_step(n0, n1):
    p, i = pl.program_id(0), pl.program_id(1)
    return (p == 0) & (i == 0), (p == n0 - 1) & (i == n1 - 1)


def _attn_fwd(qc, kc, v, gather=None):
    t = qc.shape[0]
    tq, tk = min(ATT_FWD_TQ, t), min(ATT_FWD_TK, t)
    assert tq == tk, "the diagonal block is split in the body on the premise of square blocks"
    npair = MLA_HEADS // 2

    def body(q_ref, k_ref, v_ref, *rest):
        if gather is None:
            o_ref, lse_ref = rest
        else:
            wp_ref, _, o_ref, lse_ref, gathered_ref, sems = rest
            first, last = _first_last_step(npair, t // tq)
            pl.when(first)(lambda: _gather_start(wp_ref, gathered_ref, sems))
            pl.when(last)(lambda: _gather_finish(wp_ref, gathered_ref, sems))
        i = pl.program_id(1)
        q = [q_ref[:, _qk_cols(e)] for e in range(2)]

        def update(state, q_rows, e, ks, ok):
            m, l, acc = state
            s = _bd(q_rows, k_ref[ks, _qk_cols(e)], 1, 1)
            if ok is not None:
                s = jnp.where(ok, s, NEG)
            m_new = jnp.maximum(m, jnp.max(s, axis=-1, keepdims=True))
            p = jnp.exp(s - m_new)
            alpha = jnp.exp(m - m_new)
            return m_new, alpha * l + jnp.sum(p, axis=-1, keepdims=True), alpha * acc + _bd(p, v_ref[ks, _v_cols(e)], 1, 0)

        def step(j, carry):
            ks = pl.ds(pl.multiple_of(j * tk, tk), tk)
            return tuple(update(carry[e], q[e], e, ks, None) for e in range(2))

        one = (jnp.full((tq, 1), NEG, f32), jnp.zeros((tq, 1), f32), jnp.zeros((tq, MLA_V), f32))
        carry = lax.fori_loop(0, i, step, (one, one))
        half = tq // 2
        outs, lses = [], []
        for e in range(2):
            top = update(tuple(a[:half] for a in carry[e]), q[e][:half], e,
                         pl.ds(pl.multiple_of(i * tk, tk), half), _causal((half, half), 0, 0))
            bottom = update(tuple(a[half:] for a in carry[e]), q[e][half:], e,
                            pl.ds(pl.multiple_of(i * tk, tk), tk), _causal((half, tk), half, 0))
            m, l, acc = (jnp.concatenate(ab, axis=0) for ab in zip(top, bottom))
            outs.append(acc / l)
            lses.append(m + jnp.log(l))
        o_ref[...] = jnp.concatenate(outs, axis=1).astype(o_ref.dtype)
        lo, _ = _pair_masks((tq, LANES))
        lse_ref[...] = jnp.where(lo, *lses)

    in_specs = [pl.BlockSpec((tq, 2 * ATT_QK), lambda p, i: (i, p)),
                pl.BlockSpec((t, 2 * ATT_QK), lambda p, i: (0, p)),
                pl.BlockSpec((t, 2 * MLA_V), lambda p, i: (0, p))]
    out_specs = [pl.BlockSpec((tq, 2 * MLA_V), lambda p, i: (i, p)), pl.BlockSpec((tq, LANES), lambda p, i: (i, p))]
    out_shape = [_sds((t, MLA_HEADS * MLA_V), bf16), _sds((t, npair * LANES), f32)]
    if gather is None:
        return _pc(body, name="attn_fwd", grid=(npair, t // tq), in_specs=in_specs, out_specs=out_specs,
                   out_shape=out_shape, sem=("arbitrary", "arbitrary"))(qc, kc, v)
    return _pc(body, name="attn_fwd_gather", grid=(npair, t // tq), in_specs=in_specs + [HBM, HBM],
               out_specs=out_specs + [HBM], out_shape=out_shape + [_sds((4,) + gather.shape, bf16)], aliases={4: 2},
               scratch=[pltpu.SemaphoreType.DMA((2, 6))], sem=("arbitrary", "arbitrary"))(
                   qc, kc, v, gather, _gather_base(gather))


def _attn_bwd(qc, kc, v, do, lse, delta, exchange=None):
    t = qc.shape[0]
    tq, tk = min(ATT_BWD_TQ, t), min(ATT_BWD_TK, t)
    npair = MLA_HEADS // 2
    nq = t // tq
    sub = tq // tk
    assert sub * tk == tq

    def body(q_ref, do_ref, lse_ref, dl_ref, k_ref, v_ref, *rest):
        if exchange is None:
            dq_ref, dk_ref, dv_ref = rest
        else:
            sb_ref, dq_ref, dk_ref, dv_ref, recv_ref, sems = rest
            first, last = _first_last_step(npair, t // tk)
            pl.when(first)(lambda: _chips_start(sb_ref, recv_ref, sems))
            pl.when(last)(lambda: _chips_finish(sb_ref, recv_ref, sems))
        j = pl.program_id(1)

        @pl.when(j == 0)
        def _():
            dq_ref[...] = jnp.zeros_like(dq_ref)

        k = [k_ref[:, _qk_cols(e)] for e in range(2)]
        vv = [v_ref[:, _v_cols(e)] for e in range(2)]

        def rows_step(carry, row0, rows, masked):
            qs = pl.ds(pl.multiple_of(row0, rows), rows)
            ok = _causal((rows, tk), row0, j * tk) if masked else None
            lse2, dl2 = lse_ref[qs, :], dl_ref[qs, :]
            new = []
            for e in range(2):
                dk, dv = carry[e]
                q_e, do_e = q_ref[qs, _qk_cols(e)], do_ref[qs, _v_cols(e)]
                p = jnp.exp(_bd(q_e, k[e], 1, 1) - lse2[:, MLA_ROPE * e:MLA_ROPE * e + 1])
                if masked:
                    p = jnp.where(ok, p, 0.0)
                dv = dv + _bd(p, do_e, 0, 0)
                dp = _bd(do_e, vv[e], 1, 1)
                ds = (p * (dp - dl2[:, MLA_ROPE * e:MLA_ROPE * e + 1])).astype(bf16)
                dk = dk + _bd(ds, q_e, 0, 0)
                dq_ref[qs, _qk_cols(e)] += _bd(ds, k[e], 1, 0)
                new.append((dk, dv))
            return tuple(new)

        one = (jnp.zeros((tk, ATT_QK), f32), jnp.zeros((tk, MLA_V), f32))
        i0 = (j * tk) // tq
        j_local = j - i0 * sub
        carry = (one, one)
        for r in range(sub):
            run = functools.partial(rows_step, row0=i0 * tq + r * tk, rows=tk, masked=True)
            carry = run(carry) if r == sub - 1 else lax.cond(r >= j_local, run, lambda c: c, carry)
        carry = lax.fori_loop(i0 + 1, nq, lambda i, c: rows_step(c, i * tq, tq, False), carry)
        for e in range(2):
            dk_ref[:, _qk_cols(e)] = carry[e][0].astype(dk_ref.dtype)
            dv_ref[:, _v_cols(e)] = carry[e][1].astype(dv_ref.dtype)

    res = lambda w: pl.BlockSpec((t, w), lambda p, j: (0, p))
    blk = lambda w: pl.BlockSpec((tk, w), lambda p, j: (j, p))
    in_specs = [res(2 * ATT_QK), res(2 * MLA_V), res(LANES), res(LANES), blk(2 * ATT_QK), blk(2 * MLA_V)]
    out_specs = [res(2 * ATT_QK), blk(2 * ATT_QK), blk(2 * MLA_V)]
    out_shape = [_sds((t, MLA_HEADS * ATT_QK), f32), _sds((t, MLA_HEADS * ATT_QK), bf16), _sds((t, MLA_HEADS * MLA_V), bf16)]
    if exchange is None:
        return _pc(body, name="attn_bwd", grid=(npair, t // tk), in_specs=in_specs, out_specs=out_specs,
                   out_shape=out_shape, sem=("arbitrary", "arbitrary"))(qc, do, lse, delta, kc, v)
    return _pc(body, name="attn_bwd_exchange", grid=(npair, t // tk), in_specs=in_specs + [HBM],
               out_specs=out_specs + [HBM], out_shape=out_shape + [_sds((3,) + exchange.shape[1:], bf16)],
               scratch=[pltpu.SemaphoreType.DMA((2, 3))], sem=("arbitrary", "arbitrary"))(
                   qc, do, lse, delta, kc, v, exchange)


def _rope_tables(t):
    half = MLA_ROPE // 2
    inv_freq = ROPE_THETA ** (-jnp.arange(half, dtype=f32) / half)
    ang = jnp.arange(t, dtype=f32)[:, None] * inv_freq[None, :]
    cos, sin = jnp.cos(ang), jnp.sin(ang)
    return jnp.concatenate([cos, cos] * 2, axis=1), jnp.concatenate([-sin, sin] * 2, axis=1)


def _relu2_epi(u):
    r = jnp.maximum(u, 0.0)
    return u, r * r


def _add_epi(r, res):
    return (r + res,)


def _drelu2_epi(da, u):
    return (da * 2.0 * jnp.maximum(u.astype(f32), 0.0),)


ROWWISE_EPI_TM = 512


def _residual_norms_epi(r, res, *gains):
    h = r + res
    return (h, *[_rms(h, g) for g in gains])


def _residual_out(a, w, h, gains, name):
    res = _mm(a, w, name=name, outs=(f32,) + (bf16,) * len(gains), epi=_residual_norms_epi, extras=(h, *gains),
              tm=ROWWISE_EPI_TM)
    return res if gains else [res]


def _dnorm_epi(dy, x, dres, gain):
    dx, dg = _rms_bwd(x, gain, dy)
    return dx + dres, dx + dres, dg


def _mlp_fwd(h, xm, w_up, w_down, tag, next_gains, gather=None):
    u, a, *gathered = _mm(xm, w_up, name=f"mlp{tag}_up", outs=(bf16, bf16), epi=_relu2_epi, gather=gather)
    h_out, *normed = _residual_out(a, w_down, h, next_gains, f"mlp{tag}_down")
    return h_out, normed, (xm, u, a), gathered


def _mlp_bwd(dh, dh16, h, gain, w_up, w_down, saved, tag):
    xm, u, a = saved
    du = _mm(dh16, w_down, tb=True, name=f"mlp{tag}_dact", outs=(bf16,), epi=_drelu2_epi, extras=(u,))
    d_down = _wgrad(a, dh16, f"mlp{tag}_dwdown")
    d_up = _wgrad(xm, du, f"mlp{tag}_dwup")
    dh_in, dh_in16, d_gain = _mm(du, w_up, tb=True, name=f"mlp{tag}_dxm", outs=(f32, bf16), accs=1, epi=_dnorm_epi,
                                 extras=(h, dh, gain), tm=ROWWISE_EPI_TM)
    return dh_in, dh_in16, d_gain, d_up, d_down


def _local_step(x, target, w, comm=None):
    w = dict(w)
    t = x.shape[0]
    cos, sgn_sin = _rope_tables(t)
    grads = {}

    xn0 = _rw(lambda xx, g: (_rms(xx, g),), [x], [w["hgrn_norm"]], [(D_MODEL, bf16)], name="hgrn_norm")[0]
    p4 = _mm(xn0, w["hgrn_w4"], name="hgrn_proj", tn=2048)

    if comm is None:
        y, states = _gla_fwd(p4, w["hgrn_lb_logits"], w["hgrn_g_norm"])
    else:
        y, states, gathered = _gla_fwd(p4, w["hgrn_lb_logits"], w["hgrn_g_norm"], gather=comm.shard["gla"])
        w.update(comm.unpack["gla"](gathered))
    h1, xm0 = _residual_out(y, w["hgrn_w_o"], x, [w["mlp_norm"][0:1]], "hgrn_out")
    h2, (hk, xn1), mlp0, gathered = _mlp_fwd(h1, xm0, w["mlp_w_up", 0], w["mlp_w_down", 0], 0,
                                             [w["kv_in_norm"], w["mla_norm"]],
                                             gather=None if comm is None else comm.shard["mlp0_up"])
    if comm is not None:
        w.update(comm.unpack["mlp0_up"](gathered[0]))

    ckr = _mm(hk, w["kv_w_dkv"], name="kv_down")

    def ckv_fn(c, cs, sn, g):
        kr = _rope(c[:, MLA_KV_LORA:], cs, sn)
        return _rms(c[:, :MLA_KV_LORA], g), jnp.concatenate([jnp.zeros_like(kr), kr], axis=1)

    c_kv, kr_head = _rw(ckv_fn, [ckr, cos, sgn_sin], [w["kv_norm"]], [(MLA_KV_LORA, bf16), (ATT_QK, f32)],
                        name="kv_norm_rope")
    kc = _mm(c_kv, w["kv_w_kcat"], name="kv_up_k", outs=(bf16,), extras=(kr_head,),
             epi=lambda r, kr: (r + _tile_lanes(kr, r.shape[1]),))
    v_att = _mm(c_kv, w["kv_w_uv"], name="kv_up_v", outs=(bf16,))
    cq0, c_q = _mm(xn1, w["mla_w_dq"], name="q_down", outs=(f32, bf16), extras=(w["mla_q_norm"],),
                   epi=lambda r, g: (r, _rms(r, g)))
    qc = _mm(c_q, w["mla_w_qcat"], name="q_up", outs=(bf16,), extras=(cos, sgn_sin),
             epi=lambda r, cs, sn: (_rope_heads(r, cs, sn, 1.0, ATT_SCALE),))
    if comm is None:
        o_att, lse = _attn_fwd(qc, kc, v_att)
    else:
        o_att, lse, gathered = _attn_fwd(qc, kc, v_att, gather=comm.shard["attn"])
        w.update(comm.unpack["attn"](gathered))
    h3, xm1 = _residual_out(o_att, w["mla_w_o"], h2, [w["mlp_norm"][1:2]], "mla_out")
    u1, a1 = _mm(xm1, w["mlp_w_up", 1], name="mlp1_up", outs=(bf16, bf16), epi=_relu2_epi)
    mlp1 = (xm1, u1, a1)

    def loss_epi(r, res, tgt, gain):
        def f(a, b):
            e = _rms(a, b) - tgt
            return 0.5 * jnp.sum(jnp.sum(e * e, axis=-1, keepdims=True) / D_MODEL, axis=0, keepdims=True)
        val, vjp = jax.vjp(f, r + res, gain)
        dh, dg = vjp(jnp.ones((1, 1), f32))
        return dh, dh, jnp.broadcast_to(val, (1, D_MODEL)), dg

    dh4, dh4_16, loss_acc, grads["final_norm"] = _mm(
        a1, w["mlp_w_down", 1], name="mlp1_down_loss", outs=(f32, bf16), accs=2, epi=loss_epi,
        extras=(h3, target, w["final_norm"]), tm=ROWWISE_EPI_TM)
    loss = loss_acc[0, 0]

    dh3, dh3_16, g_n1, grads["mlp_w_up", 1], grads["mlp_w_down", 1] = _mlp_bwd(
        dh4, dh4_16, h3, w["mlp_norm"][1:2], w["mlp_w_up", 1], w["mlp_w_down", 1], mlp1, 1)
    do_att = _mm(dh3_16, w["mla_w_o"], tb=True, name="mla_dout", outs=(bf16,))
    grads["mla_w_o"] = _wgrad(o_att, dh3_16, "mla_dwo")

    def delta_fn(a, b):
        prod = a.astype(f32) * b.astype(f32)
        outs = []
        for p in range(MLA_HEADS // 2):
            d0 = jnp.sum(prod[:, 2 * p * LANES:(2 * p + 1) * LANES], axis=-1, keepdims=True)
            d1 = jnp.sum(prod[:, (2 * p + 1) * LANES:(2 * p + 2) * LANES], axis=-1, keepdims=True)
            lo, _ = _pair_masks((a.shape[0], LANES))
            outs.append(jnp.where(lo, d0, d1))
        return (jnp.concatenate(outs, axis=1),)

    delta = _rw(delta_fn, [do_att, o_att], [], [(MLA_HEADS // 2 * LANES, f32)], name="attn_delta")[0]
    if comm is None:
        dqc, dkc, dv = _attn_bwd(qc, kc, v_att, do_att, lse, delta)
    else:
        dqc, dkc, dv, comm.received["attn"] = _attn_bwd(qc, kc, v_att, do_att, lse, delta,
                                                        exchange=comm.reduce(grads, "attn"))
    dqf = _rw(lambda a, cs, sn: (_rope_heads(a, cs, sn, -1.0, ATT_SCALE),), [dqc, cos, sgn_sin], [],
              [(MLA_HEADS * ATT_QK, bf16)], name="dq_rope")[0]
    dcq0, grads["mla_q_norm"] = _mm(dqf, w["mla_w_qcat"], tb=True, name="q_up_dx", outs=(bf16,), accs=1,
                                    extras=(cq0, w["mla_q_norm"]), epi=lambda dy, c, g: _rms_bwd(c, g, dy))
    grads["mla_w_qcat"] = _wgrad(c_q, dqf, "q_up_dw")
    dxn1 = _mm(dcq0, w["mla_w_dq"], tb=True, name="q_down_dx")
    grads["mla_w_dq"] = _wgrad(xn1, dcq0, "q_down_dw")

    dc_kv = _mm(dkc, w["kv_w_kcat"], tb=True, name="kv_up_dx_k")
    dc_kv = _mm(dv, w["kv_w_uv"], tb=True, name="kv_up_dx_v", epi=_add_epi, extras=(dc_kv,))
    grads["kv_w_kcat"] = _wgrad(c_kv, dkc, "kv_up_dw_k")
    grads["kv_w_uv"] = _wgrad(c_kv, dv, "kv_up_dw_v")

    def dckr_fn(c, dc, dk_heads, cs, sn, g):
        tot = dk_heads[:, LANES:ATT_QK].astype(f32)
        for h in range(1, MLA_HEADS):
            tot = tot + dk_heads[:, ATT_QK * h + LANES:ATT_QK * (h + 1)].astype(f32)
        lo, _ = _pair_masks(tot.shape)
        dkr = jnp.where(lo, _rope(tot, cs, sn, -1.0), 0.0)
        dcc, dg = _rms_bwd(c[:, :MLA_KV_LORA], g, dc)
        return jnp.concatenate([dcc, dkr], axis=1), dg

    dckr, grads["kv_norm"] = _rw(dckr_fn, [ckr, dc_kv, dkc, cos, sgn_sin], [w["kv_norm"]],
                                 [(MLA_KV_LORA + LANES, bf16)], [(1, MLA_KV_LORA)], name="kv_dnorm_rope")
    grads["kv_w_dkv"] = _wgrad(hk, dckr, "kv_down_dw")

    def dh2_epi(d1, hh, d2, dres, g1, g2):
        a, ga = _rms_bwd(hh, g1, d1)
        b, gb = _rms_bwd(hh, g2, d2)
        return a + b + dres, a + b + dres, ga, gb

    dh2, dh2_16, grads["kv_in_norm"], grads["mla_norm"] = _mm(
        dckr, w["kv_w_dkv"], tb=True, name="kv_down_dx", outs=(f32, bf16), accs=2, epi=dh2_epi,
        extras=(h2, dxn1, dh3, w["kv_in_norm"], w["mla_norm"]), tm=ROWWISE_EPI_TM)

    dh1, dh1_16, g_n0, grads["mlp_w_up", 0], grads["mlp_w_down", 0] = _mlp_bwd(
        dh2, dh2_16, h1, w["mlp_norm"][0:1], w["mlp_w_up", 0], w["mlp_w_down", 0], mlp0, 0)
    grads["mlp_norm"] = jnp.concatenate([g_n0, g_n1], axis=0)
    dy = _mm(dh1_16, w["hgrn_w_o"], tb=True, name="hgrn_dout")
    grads["hgrn_w_o"] = _wgrad(y, dh1_16, "hgrn_dwo")

    gla_args = (p4, w["hgrn_lb_logits"], w["hgrn_g_norm"], states, dy)
    if comm is None:
        dp4, grads["hgrn_lb_logits"], grads["hgrn_g_norm"] = _gla_bwd(*gla_args)
    else:
        dp4, grads["hgrn_lb_logits"], grads["hgrn_g_norm"], comm.received["gla"] = _gla_bwd(
            *gla_args, exchange=comm.reduce(grads, "gla"))
    grads["hgrn_w4"] = _mm(xn0, dp4, ta=True, name="hgrn_proj_dw")
    grad_x, grads["hgrn_norm"], *received = _mm(
        dp4, w["hgrn_w4"], tb=True, name="hgrn_proj_dx", outs=(f32,), accs=1, epi=lambda *args: _dnorm_epi(*args)[1:],
        extras=(x, dh1, w["hgrn_norm"]), tm=ROWWISE_EPI_TM, exchange=None if comm is None else comm.reduce(grads, "late"))
    if comm is not None:
        comm.received["late"] = received[0]
    return loss, grad_x, grads


HBM = pl.BlockSpec(memory_space=pltpu.HBM)


def _me():
    return lax.axis_index("x"), lax.axis_index("y"), lax.axis_index("c")


def _flip(x, y, f):
    return (1 - x if f & 1 else x), (1 - y if f & 2 else y)


def _rcopy(src, dst, sems, k, dev):
    return pltpu.make_async_remote_copy(src_ref=src, dst_ref=dst, send_sem=sems.at[0, k], recv_sem=sems.at[1, k],
                                        device_id=dev, device_id_type=MESH)


def _my_half(rows, c, mine=True):
    half = rows // 2
    return pl.ds(pl.multiple_of((c if mine else 1 - c) * half, 16), half)


def _gather_start(wp_ref, out_ref, sems):
    x, y, c = _me()
    half = _my_half(wp_ref.shape[0], c)
    for f in (1, 2, 3):
        px, py = _flip(x, y, f)
        _rcopy(wp_ref.at[half], out_ref.at[2 * x + y, half], sems, f - 1, (px, py, c)).start()


def _gather_finish(wp_ref, out_ref, sems):
    x, y, c = _me()
    half, other = _my_half(wp_ref.shape[0], c), _my_half(wp_ref.shape[0], c, mine=False)
    sends = []
    for f in (1, 2, 3):
        px, py = _flip(x, y, f)
        landed = out_ref.at[2 * px + py, half]
        _rcopy(landed, landed, sems, f - 1, (px, py, c)).wait_recv()
        sends.append(_rcopy(landed, landed, sems, 2 + f, (x, y, 1 - c)))
        sends[-1].start()
    for f in (1, 2, 3):
        px, py = _flip(x, y, f)
        theirs = out_ref.at[2 * px + py, other]
        _rcopy(theirs, theirs, sems, 2 + f, (x, y, 1 - c)).wait_recv()
        sends.append(_rcopy(wp_ref.at[half], out_ref.at[2 * x + y, half], sems, f - 1, (px, py, c)))
    for cp in sends:
        cp.wait_send()


def _gather_base(wp):
    return jnp.broadcast_to(wp[None], (4,) + wp.shape)


def _all_gather_weights(wp, sv):
    def body(wp_ref, sv_ref, base_ref, out_ref, svs_ref, sems, local_sem):
        x, y, c = _me()
        mine = pltpu.make_async_copy(sv_ref, svs_ref.at[2 * x + y], local_sem)
        mine.start()
        _gather_start(wp_ref, out_ref, sems)
        small = []
        for f in (1, 2, 3):
            px, py = _flip(x, y, f)
            small.append(_rcopy(sv_ref, svs_ref.at[2 * x + y], sems, 5 + f, (px, py, c)))
            small[-1].start()
        _gather_finish(wp_ref, out_ref, sems)
        for f in (1, 2, 3):
            px, py = _flip(x, y, f)
            _rcopy(sv_ref, svs_ref.at[2 * px + py], sems, 5 + f, (px, py, c)).wait_recv()
        for cp in small:
            cp.wait_send()
        mine.wait()

    return _pc(body, name="weights_all_gather", in_specs=[HBM, HBM, HBM], out_specs=[HBM, HBM],
               out_shape=[_sds((4,) + wp.shape, bf16), _sds((4, 8, 256), f32)], aliases={2: 0},
               scratch=[pltpu.SemaphoreType.DMA((2, 9)), pltpu.SemaphoreType.DMA])(wp, sv, _gather_base(wp))


def _send_half_to_sibling(gp, name):
    rows = gp.shape[1]

    def body(gp_ref, out_ref, sems):
        x, y, c = _me()
        cp = _rcopy(gp_ref.at[:, _my_half(rows, c, mine=False)], out_ref, sems, 0, (x, y, 1 - c))
        cp.start()
        cp.wait()

    return _pc(body, name=name, in_specs=[HBM], out_specs=HBM, out_shape=_sds((4, rows // 2, D_MODEL), gp.dtype),
               scratch=[pltpu.SemaphoreType.DMA((2, 1))])(gp)


def _chips_start(sb_ref, out_ref, sems):
    x, y, c = _me()
    for f in (1, 2, 3):
        px, py = _flip(x, y, f)
        _rcopy(sb_ref.at[2 * px + py], out_ref.at[f - 1], sems, f - 1, (px, py, c)).start()


def _chips_finish(sb_ref, out_ref, sems):
    x, y, c = _me()
    for f in (1, 2, 3):
        _rcopy(sb_ref.at[0], out_ref.at[f - 1], sems, f - 1, (x, y, c)).wait_recv()
    for f in (1, 2, 3):
        px, py = _flip(x, y, f)
        _rcopy(sb_ref.at[2 * px + py], out_ref.at[f - 1], sems, f - 1, (px, py, c)).wait_send()


def _final_exchange(small, tots):
    n = len(tots)

    def body(small_ref, *refs):
        tot_refs, smalls_ref, out_refs, sems, local_sem = refs[:n], refs[n], refs[n + 1:2 * n + 1], refs[-2], refs[-1]
        x, y, c = _me()
        me = 4 * x + 2 * y + c
        mine = pltpu.make_async_copy(small_ref, smalls_ref.at[me], local_sem)
        mine.start()
        sends = []
        for f in range(1, 8):
            px, py = _flip(x, y, f)
            pc = 1 - c if f & 4 else c
            sends.append(_rcopy(small_ref, smalls_ref.at[me], sems, f - 1, (px, py, pc)))
        for i in range(n):
            half = _my_half(tot_refs[i].shape[0], c)
            sends.append(_rcopy(tot_refs[i].at[half], out_refs[i].at[half], sems, 7 + i, (x, y, 1 - c)))
        for cp in sends:
            cp.start()
        for f in range(1, 8):
            px, py = _flip(x, y, f)
            pc = 1 - c if f & 4 else c
            _rcopy(small_ref, smalls_ref.at[4 * px + 2 * py + pc], sems, f - 1, (x, y, c)).wait_recv()
        for i in range(n):
            theirs = out_refs[i].at[_my_half(tot_refs[i].shape[0], c, mine=False)]
            _rcopy(theirs, theirs, sems, 7 + i, (x, y, 1 - c)).wait_recv()
        for cp in sends:
            cp.wait_send()
        mine.wait()

    return _pc(body, name="final_exchange", in_specs=[HBM] * (1 + n), out_specs=[HBM] * (1 + n),
               out_shape=[_sds((8, SMALL_ROWS, D_MODEL), f32)] + [_sds(t.shape, f32) for t in tots],
               aliases={1 + i: 1 + i for i in range(n)},
               scratch=[pltpu.SemaphoreType.DMA((2, 7 + n)), pltpu.SemaphoreType.DMA])(small, *tots)


def _sum_rows(half):
    return max(r for r in range(16, 513, 16) if half % r == 0)


def _sum_over_cores(gp, recv, cq, name):
    half = recv.shape[1]
    tr = _sum_rows(half)
    nb = half // tr

    def body(cq_ref, g_ref, r_ref, o32_ref, o16_ref):
        s = g_ref[...].astype(f32) + r_ref[...].astype(f32)
        o32_ref[...] = s
        o16_ref[...] = s.astype(bf16)

    spec = pl.BlockSpec((1, tr, D_MODEL), lambda b, i, cq_ref: (b, i, 0))
    gs = pltpu.PrefetchScalarGridSpec(
        num_scalar_prefetch=1, grid=(4, nb),
        in_specs=[pl.BlockSpec((1, tr, D_MODEL), lambda b, i, cq_ref: (b, cq_ref[0] * nb + i, 0)), spec],
        out_specs=[spec, spec])
    return _pc(body, name=name, grid_spec=gs, sem=("arbitrary", "arbitrary"),
               out_shape=[_sds((4, half, D_MODEL), f32), _sds((4, half, D_MODEL), bf16)])(cq, gp, recv)


def _sum_over_chips(s32, recv, cq, name):
    half = recv.shape[1]
    tr = _sum_rows(half)
    nb = half // tr

    def body(cq_ref, own_ref, r_ref, o_ref):
        o_ref[...] = ((own_ref[0] + r_ref[0].astype(f32)) + r_ref[1].astype(f32)) + r_ref[2].astype(f32)

    gs = pltpu.PrefetchScalarGridSpec(
        num_scalar_prefetch=1, grid=(nb,),
        in_specs=[pl.BlockSpec((1, tr, D_MODEL), lambda i, cq_ref: (cq_ref[1], i, 0)),
                  pl.BlockSpec((3, tr, D_MODEL), lambda i, cq_ref: (0, i, 0))],
        out_specs=pl.BlockSpec((tr, D_MODEL), lambda i, cq_ref: (cq_ref[0] * nb + i, 0)))
    return _pc(body, name=name, grid_spec=gs, sem=("arbitrary",),
               out_shape=_sds((2 * half, D_MODEL), f32))(cq, s32, recv)


def _sum_small(smalls):
    def body(s_ref, o_ref):
        tot = s_ref[0]
        for d in range(1, 8):
            tot = tot + s_ref[d]
        o_ref[...] = tot

    return _pc(body, name="small_sum", out_shape=_sds((SMALL_ROWS, D_MODEL), f32))(smalls)


def _adamw_math(w, g, m, v):
    m = ADAM_B1 * m + (1.0 - ADAM_B1) * g
    v = ADAM_B2 * v + (1.0 - ADAM_B2) * jnp.square(g)
    m_hat = m / (1.0 - ADAM_B1 ** ADAM_STEP)
    v_hat = v / (1.0 - ADAM_B2 ** ADAM_STEP)
    delta = -ADAM_LR * (m_hat / (jnp.sqrt(v_hat) + ADAM_EPS) + ADAM_WD * w)
    return delta, m, v


def _adamw(w, g, m, v, name):
    cols = w.shape[1]
    return _rw(_adamw_math, [w, g, m, v], [], [(cols, f32)] * 3, name=name, tr=256)


def _adamw_small(items, name):
    n = len(items)

    def body(*refs):
        ins, outs = refs[:4 * n], refs[4 * n:]
        for i in range(n):
            res = _adamw_math(*[r[...] for r in ins[4 * i:4 * i + 4]])
            for o, val in zip(outs[3 * i:3 * i + 3], res):
                o[...] = val

    flat = [a for it in items for a in it]
    out_shape = [_sds(it[0].shape, f32) for it in items for _ in range(3)]
    res = _pc(body, name=name, out_shape=out_shape)(*flat)
    return [tuple(res[3 * i:3 * i + 3]) for i in range(n)]


def _pack_shards(sh, layout, pad):
    parts = [(sh[n] if layer is None else sh[n][layer]).reshape(-1, D_MODEL).astype(bf16) for n, layer, _ in layout]
    if pad:
        parts.append(jnp.zeros((pad, D_MODEL), bf16))
    return jnp.concatenate(parts, axis=0)


def _mlp_full(g4, off, layer):
    o, r = off["mlp_w_up", layer]
    up = g4[:, o:o + r].transpose(1, 0, 2).reshape(D_MODEL, D_FF)
    o, r = off["mlp_w_down", layer]
    return {("mlp_w_up", layer): up, ("mlp_w_down", layer): g4[:, o:o + r].reshape(D_FF, D_MODEL)}


def _unpack_early(g4):
    hg = g4[:, 0:1024].reshape(4, 4, 256, D_MODEL)
    o, r = W_EARLY_OFF["hgrn_w_o", None]
    return {"hgrn_w4": hg.transpose(0, 2, 1, 3).reshape(D_MODEL, 4 * D_MODEL),
            "hgrn_w_o": g4[:, o:o + r].reshape(D_MODEL, D_MODEL)}


def _unpack_gla(g4):
    return _mlp_full(g4, W_GLA_OFF, 0)


def _unpack_last(g4):
    return _mlp_full(g4, W_LAST_OFF, 1)


def _unpack_mid(g4):
    def rows(name):
        o, r = W_MID_OFF[name, None]
        return g4[:, o:o + r]

    w = {"mla_w_dq": rows("mla_w_dq").reshape(D_MODEL, MLA_Q_LORA)}
    uq = rows("mla_w_uq").reshape(4, MLA_Q_LORA, 768).transpose(1, 0, 2).reshape(MLA_Q_LORA, MLA_HEADS, MLA_NOPE + MLA_ROPE)
    w["mla_w_qcat"] = jnp.pad(uq, ((0, 0), (0, 0), (0, ATT_QK - MLA_NOPE - MLA_ROPE))).reshape(MLA_Q_LORA, MLA_HEADS * ATT_QK)
    w["mla_w_o"] = rows("mla_w_o").reshape(MLA_HEADS * MLA_V, D_MODEL)
    dkv = rows("kv_w_dkv").reshape(D_MODEL, MLA_KV_LORA + MLA_ROPE)
    w["kv_w_dkv"] = jnp.pad(dkv, ((0, 0), (0, LANES - MLA_ROPE)))
    uk = rows("kv_w_uk").reshape(4, MLA_KV_LORA, 512).transpose(1, 0, 2).reshape(MLA_KV_LORA, MLA_HEADS, MLA_NOPE)
    w["kv_w_kcat"] = jnp.pad(uk, ((0, 0), (0, 0), (0, ATT_QK - MLA_NOPE))).reshape(MLA_KV_LORA, MLA_HEADS * ATT_QK)
    w["kv_w_uv"] = rows("kv_w_uv").reshape(4, MLA_KV_LORA, 512).transpose(1, 0, 2).reshape(MLA_KV_LORA, MLA_HEADS * MLA_V)
    return w


def _pack_grads_late(g):
    return g["hgrn_w4"].reshape(4, 256, 4, D_MODEL).transpose(0, 2, 1, 3).reshape(4, G_LATE_ROWS, D_MODEL)


def _grad_rows(g, name, layer):
    if name in ("mlp_w_up", "mlp_w_down"):
        full = g[name, layer]
        return full.reshape(D_MODEL, 4, 1024).transpose(1, 0, 2) if name == "mlp_w_up" else full.reshape(4, 1024, D_MODEL)
    if name == "mla_w_uq":
        uq = g["mla_w_qcat"].reshape(MLA_Q_LORA, MLA_HEADS, ATT_QK)[:, :, :MLA_NOPE + MLA_ROPE]
        return uq.reshape(MLA_Q_LORA, 4, 768).transpose(1, 0, 2).reshape(4, 192, D_MODEL)
    if name == "kv_w_uk":
        uk = g["kv_w_kcat"].reshape(MLA_KV_LORA, MLA_HEADS, ATT_QK)[:, :, :MLA_NOPE]
        return uk.reshape(MLA_KV_LORA, 4, 512).transpose(1, 0, 2).reshape(4, 128, D_MODEL)
    if name == "kv_w_uv":
        return g[name].reshape(MLA_KV_LORA, 4, 512).transpose(1, 0, 2).reshape(4, 128, D_MODEL)
    if name == "kv_w_dkv":
        return g[name][:, :MLA_KV_LORA + MLA_ROPE].reshape(4, 80, D_MODEL)
    return g[name].reshape(4, -1, D_MODEL)


def _pack_grads(g, layout):
    parts = [_grad_rows(g, name, layer) for name, layer, _ in layout]
    if layout in PADDED:
        parts.append(jnp.zeros((4, PACK_PAD, D_MODEL), bf16))
    return jnp.concatenate(parts, axis=1)


LOSS_ROW = 11


def _pack_small(g, loss):
    rows = []
    for name, _, r, wd in SMALL:
        a = g[name].reshape(r, wd)
        rows.append(jnp.pad(a, ((0, 0), (0, D_MODEL - wd))) if wd < D_MODEL else a)
    assert sum(r for _, _, r, _ in SMALL) == LOSS_ROW
    rows.append(jnp.full((1, D_MODEL), loss, f32))
    rows.append(jnp.zeros((SMALL_ROWS - LOSS_ROW - 1, D_MODEL), f32))
    return jnp.concatenate(rows, axis=0)


def kernel(x, hgrn_norm, hgrn_w_q, hgrn_w_f, hgrn_w_i, hgrn_w_g, hgrn_g_norm, hgrn_w_o, hgrn_lb_logits, mla_norm, mla_w_dq, mla_q_norm, mla_w_uq, mla_w_o, kv_in_norm, kv_w_dkv, kv_norm, kv_w_uk, kv_w_uv, mlp_norm, mlp_w_up, mlp_w_down, final_norm, loss_target, m_hgrn_norm, m_hgrn_w_q, m_hgrn_w_f, m_hgrn_w_i, m_hgrn_w_g, m_hgrn_g_norm, m_hgrn_w_o, m_hgrn_lb_logits, m_mla_norm, m_mla_w_dq, m_mla_q_norm, m_mla_w_uq, m_mla_w_o, m_kv_in_norm, m_kv_w_dkv, m_kv_norm, m_kv_w_uk, m_kv_w_uv, m_mlp_norm, m_mlp_w_up, m_mlp_w_down, m_final_norm, v_hgrn_norm, v_hgrn_w_q, v_hgrn_w_f, v_hgrn_w_i, v_hgrn_w_g, v_hgrn_g_norm, v_hgrn_w_o, v_hgrn_lb_logits, v_mla_norm, v_mla_w_dq, v_mla_q_norm, v_mla_w_uq, v_mla_w_o, v_kv_in_norm, v_kv_w_dkv, v_kv_norm, v_kv_w_uk, v_kv_w_uv, v_mlp_norm, v_mlp_w_up, v_mlp_w_down, v_final_norm):
    given = dict(locals())
    wsh = {n: given[n] for n in WEIGHTS}
    msh = {n: given["m_" + n] for n in WEIGHTS}
    vsh = {n: given["v_" + n] for n in WEIGHTS}
    xi, yi, ci = _me()
    chip = 2 * xi + yi
    cq = jnp.stack([ci, chip]).astype(jnp.int32)

    small_w = {n: wsh[n].reshape(r, -1) for n, _, r, _ in SMALL}
    sv = jnp.concatenate([small_w["hgrn_norm"], small_w["hgrn_lb_logits"], jnp.zeros((5, 256), f32)], axis=0)
    g4, sv4 = _all_gather_weights(_pack_shards(wsh, W_EARLY, 0), sv)
    w = _unpack_early(g4)
    w["hgrn_norm"] = sv4[:, 0, :].reshape(1, D_MODEL)
    w["hgrn_lb_logits"] = sv4[:, 1:3, :].transpose(1, 0, 2).reshape(2, D_MODEL)
    for n in ("hgrn_g_norm", "mla_norm", "mla_q_norm", "kv_in_norm", "kv_norm", "mlp_norm", "final_norm"):
        w[n] = small_w[n]

    class Comm:
        shard = {"gla": _pack_shards(wsh, W_GLA, 0), "mlp0_up": _pack_shards(wsh, W_MID, PACK_PAD),
                 "attn": _pack_shards(wsh, W_LAST, 0)}
        unpack = {"gla": _unpack_gla, "mlp0_up": _unpack_mid, "attn": _unpack_last}
        layout = {"gla": G_GLA, "attn": G_ATTN}
        received, s32 = {}, {}

        @staticmethod
        def reduce(grads, part):
            gp = _pack_grads_late(grads) if part == "late" else _pack_grads(grads, Comm.layout[part])
            Comm.s32[part], s16 = _sum_over_cores(gp, _send_half_to_sibling(gp, "grads_to_sibling_" + part), cq,
                                                  "grads_sum_cores_" + part)
            return s16

    loss, grad_x, g = _local_step(x.reshape(-1, D_MODEL), loss_target.reshape(-1, D_MODEL), w, Comm)

    parts = ("late", "gla", "attn")
    halves = [_sum_over_chips(Comm.s32[p], Comm.received[p], cq, "grads_sum_chips_" + p) for p in parts]
    smalls, *totals = _final_exchange(_pack_small(g, loss), halves)
    total = dict(zip(parts, totals))
    small_tot = _sum_small(smalls)
    loss = small_tot[LOSS_ROW, 0]

    where = {}
    for part, offsets in (("late", G_LATE_OFF), ("gla", G_GLA_OFF), ("attn", G_ATTN_OFF)):
        for (n, layer), (o, r) in offsets.items():
            where.setdefault(n, []).append(total[part][o:o + r])
    grad, delta, new_m, new_v = {}, {}, {}, {}
    groups = {"hgrn": [], "mla_kv": []}
    for n, pieces in where.items():
        shp = wsh[n].shape
        two_d = (-1, shp[-1])
        grad[n] = (pieces[0] if len(pieces) == 1 else jnp.concatenate(pieces, axis=0)).reshape(shp)
        operands = (wsh[n].reshape(two_d), grad[n].reshape(two_d), msh[n].reshape(two_d), vsh[n].reshape(two_d))
        if n.startswith("mlp"):
            res = _adamw(*operands, "adamw_" + n)
            delta[n], new_m[n], new_v[n] = (a.reshape(shp) for a in res)
        else:
            groups["hgrn" if n.startswith("hgrn") else "mla_kv"].append((n, operands))
    for gname, members in groups.items():
        for (n, _), res in zip(members, _adamw_small([ops for _, ops in members], "adamw_" + gname)):
            delta[n], new_m[n], new_v[n] = (a.reshape(wsh[n].shape) for a in res)
    items = []
    for n, row, r, wd in SMALL:
        gs = small_tot[row:row + r, :wd]
        if n in ("hgrn_norm", "hgrn_lb_logits"):
            gs = lax.dynamic_slice(gs, (0, 256 * chip), (r, 256))
        grad[n] = gs.reshape(wsh[n].shape)
        items.append((small_w[n], gs, msh[n].reshape(gs.shape), vsh[n].reshape(gs.shape)))
    for (n, _, _, _), (d, m2, v2) in zip(SMALL, _adamw_small(items, "adamw_small")):
        shp = wsh[n].shape
        delta[n], new_m[n], new_v[n] = d.reshape(shp), m2.reshape(shp), v2.reshape(shp)

    return (loss, grad_x.reshape(x.shape), *[grad[n] for n in WEIGHTS], *[delta[n] for n in WEIGHTS],
            *[new_m[n] for n in WEIGHTS], *[new_v[n] for n in WEIGHTS])
```

```python
import functools

import jax
import jax.numpy as jnp
from jax import lax
from jax.experimental import pallas as pl
from jax.experimental.pallas import tpu as pltpu

f32, bf16 = jnp.float32, jnp.bfloat16
HI = lax.Precision.HIGHEST
MESH = pl.DeviceIdType.MESH

D_MODEL = 1024
D_FF = 4096
EPS = 1e-6
HGRN_HEADS, HGRN_DK, HGRN_CHUNK, HGRN_SUB = 8, 128, 64, 16
MLA_HEADS, MLA_NOPE, MLA_ROPE, MLA_V = 16, 128, 64, 128
MLA_Q_LORA, MLA_KV_LORA = 256, 256
ROPE_THETA = 10000.0
ATT_SCALE = (MLA_NOPE + MLA_ROPE) ** -0.5
EXP_CLAMP = 80.0

ADAM_LR, ADAM_B1, ADAM_B2, ADAM_EPS, ADAM_WD, ADAM_STEP = 0.001, 0.9, 0.999, 1e-08, 0.01, 10

V7X_VMEM_BYTES = 64 * 1024 * 1024
VMEM_LIMIT = V7X_VMEM_BYTES - 8 * 1024 * 1024
LANES = 128

PACK_PAD = 16
W_EARLY = (("hgrn_w_q", None, 256), ("hgrn_w_f", None, 256), ("hgrn_w_i", None, 256), ("hgrn_w_g", None, 256),
           ("hgrn_w_o", None, 256))
W_GLA = (("mlp_w_up", 0, 1024), ("mlp_w_down", 0, 1024))
W_MID = (("mla_w_dq", None, 64), ("mla_w_uq", None, 192), ("mla_w_o", None, 512), ("kv_w_dkv", None, 80),
         ("kv_w_uk", None, 128), ("kv_w_uv", None, 128))
W_LAST = (("mlp_w_up", 1, 1024), ("mlp_w_down", 1, 1024))
G_LATE = (("hgrn_w_q", None, 256), ("hgrn_w_f", None, 256), ("hgrn_w_i", None, 256), ("hgrn_w_g", None, 256))
G_ATTN = (("mla_w_o", None, 512), ("mlp_w_up", 1, 1024), ("mlp_w_down", 1, 1024))
G_GLA = (("hgrn_w_o", None, 256), ("mla_w_dq", None, 64), ("mla_w_uq", None, 192), ("kv_w_dkv", None, 80),
         ("kv_w_uk", None, 128), ("kv_w_uv", None, 128), ("mlp_w_up", 0, 1024), ("mlp_w_down", 0, 1024))
PADDED = (W_MID, G_GLA)


def _offsets(layout):
    out, o = {}, 0
    for name, layer, rows in layout:
        out[name, layer] = (o, rows)
        o += rows
    return out, o + (PACK_PAD if layout in PADDED else 0)


W_EARLY_OFF, W_EARLY_ROWS = _offsets(W_EARLY)
W_GLA_OFF, W_GLA_ROWS = _offsets(W_GLA)
W_MID_OFF, W_MID_ROWS = _offsets(W_MID)
W_LAST_OFF, W_LAST_ROWS = _offsets(W_LAST)
G_LATE_OFF, G_LATE_ROWS = _offsets(G_LATE)
G_ATTN_OFF, G_ATTN_ROWS = _offsets(G_ATTN)
G_GLA_OFF, G_GLA_ROWS = _offsets(G_GLA)
assert all(r % 32 == 0 for r in (W_EARLY_ROWS, W_MID_ROWS, W_LAST_ROWS, G_LATE_ROWS, G_ATTN_ROWS, G_GLA_ROWS))

WEIGHTS = ("hgrn_norm", "hgrn_w_q", "hgrn_w_f", "hgrn_w_i", "hgrn_w_g", "hgrn_g_norm", "hgrn_w_o", "hgrn_lb_logits",
           "mla_norm", "mla_w_dq", "mla_q_norm", "mla_w_uq", "mla_w_o", "kv_in_norm", "kv_w_dkv", "kv_norm", "kv_w_uk",
           "kv_w_uv", "mlp_norm", "mlp_w_up", "mlp_w_down", "final_norm")
SMALL = (("hgrn_norm", 0, 1, 1024), ("hgrn_lb_logits", 1, 2, 1024), ("hgrn_g_norm", 3, 1, 128),
         ("mla_norm", 4, 1, 1024), ("mla_q_norm", 5, 1, 256), ("kv_in_norm", 6, 1, 1024), ("kv_norm", 7, 1, 256),
         ("mlp_norm", 8, 2, 1024), ("final_norm", 10, 1, 1024))
SMALL_ROWS = 16


def _pc(body, *, name, out_shape, grid=None, in_specs=None, out_specs=None, scratch=(), sem=None, grid_spec=None,
        aliases=None):
    params = pltpu.CompilerParams(dimension_semantics=sem, vmem_limit_bytes=VMEM_LIMIT)
    if grid_spec is not None:
        return pl.pallas_call(body, name=name, out_shape=out_shape, grid_spec=grid_spec, compiler_params=params,
                              interpret=False)
    kw = {k: v for k, v in (("grid", grid), ("in_specs", in_specs), ("out_specs", out_specs),
                            ("input_output_aliases", aliases)) if v is not None}
    return pl.pallas_call(body, name=name, out_shape=out_shape, scratch_shapes=list(scratch), compiler_params=params,
                          interpret=False, **kw)


def _sds(shape, dtype):
    return jax.ShapeDtypeStruct(tuple(shape), dtype)


def _mm(a, b, *, name, ta=False, tb=False, outs=(f32,), epi=None, extras=(), accs=0, gather=None, exchange=None,
        tm=1024, tn=1024, tk=4096):
    m, k = (a.shape[1], a.shape[0]) if ta else a.shape
    n = b.shape[0] if tb else b.shape[1]
    tm, tn, tk = min(tm, m), min(tn, n), min(tk, k)
    assert m % tm == 0 and n % tn == 0 and k % tk == 0, (name, m, n, k)
    nk = k // tk
    assert accs == 0 or (tn == n and nk == 1), name
    a_spec = pl.BlockSpec((tk, tm), lambda i, j, kk: (kk, i)) if ta else pl.BlockSpec((tm, tk), lambda i, j, kk: (i, kk))
    b_spec = pl.BlockSpec((tn, tk), lambda i, j, kk: (j, kk)) if tb else pl.BlockSpec((tk, tn), lambda i, j, kk: (kk, j))

    def extra_spec(e):
        if e.shape == (m, n):
            return pl.BlockSpec((tm, tn), lambda i, j, kk: (i, j))
        if e.shape[0] == m:
            return pl.BlockSpec((tm, e.shape[1]), lambda i, j, kk: (i, 0))
        return pl.BlockSpec((e.shape[0], tn), lambda i, j, kk: (0, j))

    e_specs = [extra_spec(e) for e in extras]
    dn = (((0 if ta else 1,), (1 if tb else 0,)), ((), ()))
    n_e, n_o = len(extras), len(outs)

    def finish(r, e_refs, o_refs):
        res = epi(r, *[e[...] for e in e_refs]) if epi is not None else (r,)
        for o, v in zip(o_refs[:n_o], res[:n_o]):
            o[...] = v.astype(o.dtype)
        for o, v in zip(o_refs[n_o:], res[n_o:]):
            @pl.when(pl.program_id(0) == 0)
            def _(o=o):
                o[...] = jnp.zeros_like(o)
            o[...] += v

    grid = (m // tm, n // tn, nk)
    assert gather is None or exchange is None
    n_g = 2 if gather is not None else (1 if exchange is not None else 0)

    def body(*refs):
        a_ref, b_ref = refs[0], refs[1]
        e_refs = refs[2:2 + n_e]
        o_refs = refs[2 + n_e + n_g:2 + n_e + n_g + n_o + accs]
        if n_g:
            src_ref, dst_ref, sems = refs[2 + n_e], refs[2 + n_e + n_g + n_o + accs], refs[-1]
            start, fin = (_gather_start, _gather_finish) if gather is not None else (_chips_start, _chips_finish)
            pid = [pl.program_id(d) for d in range(3)]
            pl.when((pid[0] == 0) & (pid[1] == 0) & (pid[2] == 0))(lambda: start(src_ref, dst_ref, sems))
            pl.when((pid[0] == grid[0] - 1) & (pid[1] == grid[1] - 1) & (pid[2] == grid[2] - 1))(
                lambda: fin(src_ref, dst_ref, sems))
        prod = lax.dot_general(a_ref[...].astype(bf16), b_ref[...].astype(bf16), dn, preferred_element_type=f32)
        if nk == 1:
            finish(prod, e_refs, o_refs)
            return
        acc = refs[2 + n_e + n_g + n_o + accs + (1 if n_g else 0)]
        kk = pl.program_id(2)

        @pl.when(kk == 0)
        def _():
            acc[...] = jnp.zeros_like(acc)

        acc[...] += prod

        @pl.when(kk == nk - 1)
        def _():
            finish(acc[...], e_refs, o_refs)

    out_specs = ([pl.BlockSpec((tm, tn), lambda i, j, kk: (i, j)) for _ in outs] +
                 [pl.BlockSpec((1, n), lambda i, j, kk: (0, 0))] * accs)
    out_shape = [_sds((m, n), dt) for dt in outs] + [_sds((1, n), f32)] * accs
    scratch = [pltpu.VMEM((tm, tn), f32)] if nk > 1 else []
    if not n_g:
        out = _pc(body, name=name, grid=grid, in_specs=[a_spec, b_spec] + e_specs, out_specs=out_specs,
                  out_shape=out_shape, scratch=scratch,
                  sem=("arbitrary" if accs else "parallel", "parallel", "arbitrary"))(a, b, *extras)
    elif exchange is not None:
        out = _pc(body, name=name + "_exchange", grid=grid, in_specs=[a_spec, b_spec] + e_specs + [HBM],
                  out_specs=out_specs + [HBM], out_shape=out_shape + [_sds((3,) + exchange.shape[1:], bf16)],
                  scratch=scratch + [pltpu.SemaphoreType.DMA((2, 3))], sem=("arbitrary",) * 3)(a, b, *extras, exchange)
    else:
        out = _pc(body, name=name + "_gather", grid=grid, in_specs=[a_spec, b_spec] + e_specs + [HBM, HBM],
                  out_specs=out_specs + [HBM], out_shape=out_shape + [_sds((4,) + gather.shape, bf16)],
                  aliases={2 + n_e + 1: n_o + accs}, scratch=scratch + [pltpu.SemaphoreType.DMA((2, 6))],
                  sem=("arbitrary",) * 3)(a, b, *extras, gather, _gather_base(gather))
    return out[0] if len(out) == 1 else out


def _wgrad(a, b, name):
    return _mm(a, b, ta=True, name=name, outs=(bf16,))


def _rw(fn, rows, bcast, outs, accs=(), *, name, tr=256):
    t = rows[0].shape[0]
    tr = min(tr, t)
    assert t % tr == 0
    n_r, n_b, n_o, n_a = len(rows), len(bcast), len(outs), len(accs)

    def body(*refs):
        r_refs = refs[:n_r]
        b_refs = refs[n_r:n_r + n_b]
        o_refs = refs[n_r + n_b:n_r + n_b + n_o]
        a_refs = refs[n_r + n_b + n_o:]
        res = fn(*[r[...] for r in r_refs], *[b[...] for b in b_refs])
        for o, v in zip(o_refs, res[:n_o]):
            o[...] = v.astype(o.dtype)
        i = pl.program_id(0)
        for a_ref, v in zip(a_refs, res[n_o:]):
            @pl.when(i == 0)
            def _(a_ref=a_ref):
                a_ref[...] = jnp.zeros_like(a_ref)
            a_ref[...] += v

    in_specs = [pl.BlockSpec((tr, r.shape[1]), lambda i: (i, 0)) for r in rows]
    in_specs += [pl.BlockSpec(b.shape, lambda i: (0, 0)) for b in bcast]
    out_specs = [pl.BlockSpec((tr, w), lambda i: (i, 0)) for w, _ in outs]
    out_specs += [pl.BlockSpec(s, lambda i: (0, 0)) for s in accs]
    out_shape = [_sds((t, w), dt) for w, dt in outs] + [_sds(s, f32) for s in accs]
    res = _pc(body, name=name, grid=(t // tr,), in_specs=in_specs, out_specs=out_specs, out_shape=out_shape,
              sem=("arbitrary",))(*rows, *bcast)
    return res


def _rms(x, gain):
    return x * lax.rsqrt(jnp.mean(x * x, axis=-1, keepdims=True) + EPS) * gain


def _rms_bwd(x, gain, dy):
    _, vjp = jax.vjp(_rms, x, gain)
    return vjp(dy)


def _lower_bound(lbl):
    l0, l1 = lbl[0:1, :], lbl[1:2, :]
    mx = jnp.maximum(l0, l1)
    e0, e1 = jnp.exp(l0 - mx), jnp.exp(l1 - mx)
    return e0 / (e0 + e1)


def _gates(qpre, fpre, lbl):
    lb = _lower_bound(lbl)
    q = jax.nn.silu(qpre)
    forget = lb + (1.0 - lb) * jax.nn.sigmoid(fpre)
    return q, 1.0 - forget, jnp.log(forget)


def _head_norm_gate(o, gpre, gn):
    return _rms(o, gn) * jax.nn.silu(gpre)


def _swap_halves(x):
    w = x.shape[1]
    lane = lax.broadcasted_iota(jnp.int32, x.shape, 1)
    return jnp.where((lane % MLA_ROPE) < MLA_ROPE // 2, pltpu.roll(x, w - MLA_ROPE // 2, 1),
                     pltpu.roll(x, MLA_ROPE // 2, 1))


def _tile_lanes(tab, w):
    return tab if w == tab.shape[1] else jnp.concatenate([tab] * (w // tab.shape[1]), axis=1)


def _rope(x, cos, sgn_sin, sign=1.0):
    w = x.shape[1]
    return x * _tile_lanes(cos, w) + sign * _swap_halves(x) * _tile_lanes(sgn_sin, w)


def _rope_heads(x, cos, sgn_sin, sign, scale):
    parts = []
    for h in range(x.shape[1] // (2 * LANES)):
        parts.append(x[:, 2 * LANES * h:2 * LANES * h + LANES] * scale)
        parts.append(_rope(x[:, 2 * LANES * h + LANES:2 * LANES * (h + 1)], cos, sgn_sin, sign) * scale)
    return jnp.concatenate(parts, axis=1)


def _bd(a, b, ca, cb):
    return lax.dot_general(a.astype(bf16), b.astype(bf16), (((ca,), (cb,)), ((), ())), preferred_element_type=f32)


@jax.custom_vjp
def _dot_nn(a, b):
    return _bd(a, b, 1, 0)


@jax.custom_vjp
def _dot_nt(a, b):
    return _bd(a, b, 1, 1)


@jax.custom_vjp
def _dot_tn(a, b):
    return _bd(a, b, 0, 0)


_dot_nn.defvjp(lambda a, b: (_bd(a, b, 1, 0), (a, b)), lambda r, g: (_bd(g, r[1], 1, 1), _bd(r[0], g, 0, 0)))
_dot_nt.defvjp(lambda a, b: (_bd(a, b, 1, 1), (a, b)), lambda r, g: (_bd(g, r[1], 1, 0), _bd(g, r[0], 0, 0)))
_dot_tn.defvjp(lambda a, b: (_bd(a, b, 0, 0), (a, b)), lambda r, g: (_bd(r[1], g, 1, 1), _bd(r[0], g, 1, 0)))


def _scan_rows(x, reverse):
    n = x.shape[0]
    row = lax.broadcasted_iota(jnp.int32, x.shape, 0)
    s = 1
    while s < n:
        if reverse:
            x = x + jnp.where(row < n - s, pltpu.roll(x, n - s, 0), 0.0)
        else:
            x = x + jnp.where(row >= s, pltpu.roll(x, s, 0), 0.0)
        s *= 2
    return x


@jax.custom_vjp
def _cumsum_rows(g):
    return _scan_rows(g, False)


_cumsum_rows.defvjp(lambda g: (_scan_rows(g, False), None), lambda _, ct: (_scan_rows(ct, True),))

HGRN_PAIRS = HGRN_HEADS // 2
HGRN_PAIR = 2 * HGRN_DK
GLA_STATE = (HGRN_PAIRS, HGRN_PAIR, HGRN_PAIR)


def _gla_consts():
    s = HGRN_SUB
    r = lax.broadcasted_iota(jnp.int32, (HGRN_PAIR, HGRN_PAIR), 0)
    c = lax.broadcasted_iota(jnp.int32, (HGRN_PAIR, HGRN_PAIR), 1)
    pair_mask = (r < HGRN_DK) == (c < HGRN_DK)
    masks = []
    for i in range(HGRN_CHUNK // s):
        n = s * (i + 1)
        row = lax.broadcasted_iota(jnp.int32, (HGRN_HEADS * s, HGRN_HEADS * n), 0)
        col = lax.broadcasted_iota(jnp.int32, (HGRN_HEADS * s, HGRN_HEADS * n), 1)
        col_head = sum((col >= m * n).astype(jnp.int32) for m in range(1, HGRN_HEADS))
        masks.append((col_head == row // s) & (col - col_head * n <= s * i + row % s))
    return pair_mask, masks


def _heads_to_rows(x):
    return jnp.concatenate([x[:, HGRN_DK * h:HGRN_DK * (h + 1)] for h in range(HGRN_HEADS)], axis=0)


def _gla_chunk(consts, dots, q, k, v, g, st):
    pair_mask, masks = consts
    dot_nn, dot_nt, dot_tn = dots
    c, s = HGRN_CHUNK, HGRN_SUB
    b = _cumsum_rows(g)
    b_last = b[c - 1:c, :]
    q_in, k_out = q * jnp.exp(b), k * jnp.exp(b_last - b)
    o_inter, st_new = [], []
    for p in range(HGRN_PAIRS):
        cols = slice(HGRN_PAIR * p, HGRN_PAIR * (p + 1))
        o_inter.append(dot_nt(q_in[:, cols], st[p]))
        st_new.append(st[p] * jnp.exp(b_last[:, cols]) + jnp.where(pair_mask, dot_tn(v[:, cols], k_out[:, cols]), 0.0))
    intra = []
    for i in range(c // s):
        n = s * (i + 1)
        ref = b[s * i - 1:s * i, :] if i else jnp.zeros_like(b_last)
        qt = _heads_to_rows(q[s * i:n] * jnp.exp(b[s * i:n] - ref))
        kt = _heads_to_rows(k[:n] * jnp.exp(jnp.minimum(ref - b[:n], EXP_CLAMP)))
        sc = jnp.where(masks[i], dot_nt(qt, kt), 0.0)
        oi = dot_nn(sc, _heads_to_rows(v[:n]))
        intra.append(jnp.concatenate([oi[s * h:s * (h + 1)] for h in range(HGRN_HEADS)], axis=1))
    return jnp.concatenate(o_inter, axis=1) + jnp.concatenate(intra, axis=0), st_new


_PLAIN_DOTS = (lambda a, b: _bd(a, b, 1, 0), lambda a, b: _bd(a, b, 1, 1), lambda a, b: _bd(a, b, 0, 0))
_VJP_DOTS = (_dot_nn, _dot_nt, _dot_tn)


def _hgrn_mix(consts, dots, qpre, fpre, v, gpre, lbl, gn, st):
    q, k, g = _gates(qpre, fpre, lbl)
    o, st_new = _gla_chunk(consts, dots, q, k, v, g, st)
    y = [_head_norm_gate(o[:, HGRN_DK * h:HGRN_DK * (h + 1)], gpre[:, HGRN_DK * h:HGRN_DK * (h + 1)], gn)
         for h in range(HGRN_HEADS)]
    return jnp.concatenate(y, axis=1), st_new


def _gla_fwd(p4, lbl, gn, gather=None):
    t = p4.shape[0]
    nc = t // HGRN_CHUNK

    def body(q_ref, k_ref, v_ref, g_ref, lbl_ref, gn_ref, *rest):
        if gather is None:
            o_ref, s_ref, st = rest
        else:
            wp_ref, _, o_ref, s_ref, gathered_ref, st, sems = rest

        @pl.when(pl.program_id(0) == 0)
        def _():
            st[...] = jnp.zeros_like(st)
            if gather is not None:
                _gather_start(wp_ref, gathered_ref, sems)

        if gather is not None:
            @pl.when(pl.program_id(0) == nc - 1)
            def _():
                _gather_finish(wp_ref, gathered_ref, sems)

        s_in = [st[p] for p in range(HGRN_PAIRS)]
        y, st_new = _hgrn_mix(_gla_consts(), _PLAIN_DOTS, q_ref[...], k_ref[...], v_ref[...], g_ref[...],
                              lbl_ref[...], gn_ref[...], s_in)
        o_ref[...] = y.astype(o_ref.dtype)
        for p in range(HGRN_PAIRS):
            s_ref[0, p] = s_in[p]
            st[p] = st_new[p]

    blk = lambda off: pl.BlockSpec((HGRN_CHUNK, D_MODEL), lambda c: (c, off))
    whole = lambda a: pl.BlockSpec(a.shape, lambda c: (0, 0))
    state_shape = GLA_STATE
    in_specs = [blk(0), blk(1), blk(2), blk(3), whole(lbl), whole(gn)]
    out_specs = [blk(0), pl.BlockSpec((1,) + state_shape, lambda c: (c, 0, 0, 0))]
    out_shape = [_sds((t, D_MODEL), bf16), _sds((nc,) + state_shape, f32)]
    scratch = [pltpu.VMEM(state_shape, f32)]
    if gather is None:
        return _pc(body, name="gla_fwd", grid=(nc,), in_specs=in_specs, out_specs=out_specs, out_shape=out_shape,
                   scratch=scratch, sem=("arbitrary",))(p4, p4, p4, p4, lbl, gn)
    return _pc(body, name="gla_fwd_gather", grid=(nc,), in_specs=in_specs + [HBM, HBM], out_specs=out_specs + [HBM],
               out_shape=out_shape + [_sds((4,) + gather.shape, bf16)], aliases={7: 2},
               scratch=scratch + [pltpu.SemaphoreType.DMA((2, 6))], sem=("arbitrary",))(
                   p4, p4, p4, p4, lbl, gn, gather, _gather_base(gather))


def _gla_bwd(p4, lbl, gn, states, dy, exchange=None):
    t = p4.shape[0]
    nc = t // HGRN_CHUNK

    def body(q_ref, k_ref, v_ref, g_ref, lbl_ref, gn_ref, s_ref, dy_ref, *rest):
        if exchange is None:
            dp_ref, dlbl_ref, dgn_ref, dst = rest
        else:
            sb_ref, dp_ref, dlbl_ref, dgn_ref, recv_ref, dst, sems = rest

        @pl.when(pl.program_id(0) == 0)
        def _():
            dst[...] = jnp.zeros_like(dst)
            dlbl_ref[...] = jnp.zeros_like(dlbl_ref)
            dgn_ref[...] = jnp.zeros_like(dgn_ref)
            if exchange is not None:
                _chips_start(sb_ref, recv_ref, sems)

        if exchange is not None:
            @pl.when(pl.program_id(0) == nc - 1)
            def _():
                _chips_finish(sb_ref, recv_ref, sems)

        consts = _gla_consts()
        fn = lambda *args: _hgrn_mix(consts, _VJP_DOTS, *args)
        pairs = range(HGRN_PAIRS)
        _, vjp = jax.vjp(fn, q_ref[...], k_ref[...], v_ref[...], g_ref[...], lbl_ref[...], gn_ref[...],
                         [s_ref[0, p] for p in pairs])
        *d_proj, dlbl, dgn, ds = vjp((dy_ref[...], [dst[p] for p in pairs]))
        for i, d in enumerate(d_proj):
            dp_ref[:, D_MODEL * i:D_MODEL * (i + 1)] = d.astype(dp_ref.dtype)
        dlbl_ref[...] += dlbl
        dgn_ref[...] += dgn
        for p in pairs:
            dst[p] = ds[p]

    blk = lambda off: pl.BlockSpec((HGRN_CHUNK, D_MODEL), lambda c: (nc - 1 - c, off))
    whole = lambda a: pl.BlockSpec(a.shape, lambda c: (0, 0))
    state_shape = GLA_STATE
    in_specs = [blk(0), blk(1), blk(2), blk(3), whole(lbl), whole(gn),
                pl.BlockSpec((1,) + state_shape, lambda c: (nc - 1 - c, 0, 0, 0)), blk(0)]
    out_specs = [pl.BlockSpec((HGRN_CHUNK, 4 * D_MODEL), lambda c: (nc - 1 - c, 0)), whole(lbl), whole(gn)]
    out_shape = [_sds((t, 4 * D_MODEL), bf16), _sds(lbl.shape, f32), _sds(gn.shape, f32)]
    scratch = [pltpu.VMEM(state_shape, f32)]
    if exchange is None:
        return _pc(body, name="gla_bwd", grid=(nc,), in_specs=in_specs, out_specs=out_specs, out_shape=out_shape,
                   scratch=scratch, sem=("arbitrary",))(p4, p4, p4, p4, lbl, gn, states, dy)
    return _pc(body, name="gla_bwd_exchange", grid=(nc,), in_specs=in_specs + [HBM], out_specs=out_specs + [HBM],
               out_shape=out_shape + [_sds((3,) + exchange.shape[1:], bf16)],
               scratch=scratch + [pltpu.SemaphoreType.DMA((2, 3))], sem=("arbitrary",))(
                   p4, p4, p4, p4, lbl, gn, states, dy, exchange)


ATT_FWD_TQ, ATT_FWD_TK = 1024, 1024
ATT_BWD_TQ, ATT_BWD_TK = 1024, 512
ATT_QK = 2 * LANES
NEG = -1e30


def _pair_masks(shape):
    lane = lax.broadcasted_iota(jnp.int32, shape, 1)
    return lane < MLA_ROPE, lane >= MLA_ROPE


def _causal(shape, row0, col0):
    row = row0 + lax.broadcasted_iota(jnp.int32, shape, 0)
    col = col0 + lax.broadcasted_iota(jnp.int32, shape, 1)
    return col <= row


def _qk_cols(e):
    return slice(ATT_QK * e, ATT_QK * (e + 1))


def _v_cols(e):
    return slice(MLA_V * e, MLA_V * (e + 1))


def _first_last_step(n0, n1):
    p, i = pl.program_id(0), pl.program_id(1)
    return (p == 0) & (i == 0), (p == n0 - 1) & (i == n1 - 1)


def _attn_fwd(qc, kc, v, gather=None):
    t = qc.shape[0]
    tq, tk = min(ATT_FWD_TQ, t), min(ATT_FWD_TK, t)
    assert tq == tk, "the diagonal block is split in the body on the premise of square blocks"
    npair = MLA_HEADS // 2

    def body(q_ref, k_ref, v_ref, *rest):
        if gather is None:
            o_ref, lse_ref = rest
        else:
            wp_ref, _, o_ref, lse_ref, gathered_ref, sems = rest
            first, last = _first_last_step(npair, t // tq)
            pl.when(first)(lambda: _gather_start(wp_ref, gathered_ref, sems))
            pl.when(last)(lambda: _gather_finish(wp_ref, gathered_ref, sems))
        i = pl.program_id(1)
        q = [q_ref[:, _qk_cols(e)] for e in range(2)]

        def update(state, q_rows, e, ks, ok):
            m, l, acc = state
            s = _bd(q_rows, k_ref[ks, _qk_cols(e)], 1, 1)
            if ok is not None:
                s = jnp.where(ok, s, NEG)
            m_new = jnp.maximum(m, jnp.max(s, axis=-1, keepdims=True))
            p = jnp.exp(s - m_new)
            alpha = jnp.exp(m - m_new)
            return m_new, alpha * l + jnp.sum(p, axis=-1, keepdims=True), alpha * acc + _bd(p, v_ref[ks, _v_cols(e)], 1, 0)

        def step(j, carry):
            ks = pl.ds(pl.multiple_of(j * tk, tk), tk)
            return tuple(update(carry[e], q[e], e, ks, None) for e in range(2))

        one = (jnp.full((tq, 1), NEG, f32), jnp.zeros((tq, 1), f32), jnp.zeros((tq, MLA_V), f32))
        carry = lax.fori_loop(0, i, step, (one, one))
        half = tq // 2
        outs, lses = [], []
        for e in range(2):
            top = update(tuple(a[:half] for a in carry[e]), q[e][:half], e,
                         pl.ds(pl.multiple_of(i * tk, tk), half), _causal((half, half), 0, 0))
            bottom = update(tuple(a[half:] for a in carry[e]), q[e][half:], e,
                            pl.ds(pl.multiple_of(i * tk, tk), tk), _causal((half, tk), half, 0))
            m, l, acc = (jnp.concatenate(ab, axis=0) for ab in zip(top, bottom))
            outs.append(acc / l)
            lses.append(m + jnp.log(l))
        o_ref[...] = jnp.concatenate(outs, axis=1).astype(o_ref.dtype)
        lo, _ = _pair_masks((tq, LANES))
        lse_ref[...] = jnp.where(lo, *lses)

    in_specs = [pl.BlockSpec((tq, 2 * ATT_QK), lambda p, i: (i, p)),
                pl.BlockSpec((t, 2 * ATT_QK), lambda p, i: (0, p)),
                pl.BlockSpec((t, 2 * MLA_V), lambda p, i: (0, p))]
    out_specs = [pl.BlockSpec((tq, 2 * MLA_V), lambda p, i: (i, p)), pl.BlockSpec((tq, LANES), lambda p, i: (i, p))]
    out_shape = [_sds((t, MLA_HEADS * MLA_V), bf16), _sds((t, npair * LANES), f32)]
    if gather is None:
        return _pc(body, name="attn_fwd", grid=(npair, t // tq), in_specs=in_specs, out_specs=out_specs,
                   out_shape=out_shape, sem=("arbitrary", "arbitrary"))(qc, kc, v)
    return _pc(body, name="attn_fwd_gather", grid=(npair, t // tq), in_specs=in_specs + [HBM, HBM],
               out_specs=out_specs + [HBM], out_shape=out_shape + [_sds((4,) + gather.shape, bf16)], aliases={4: 2},
               scratch=[pltpu.SemaphoreType.DMA((2, 6))], sem=("arbitrary", "arbitrary"))(
                   qc, kc, v, gather, _gather_base(gather))


def _attn_bwd(qc, kc, v, do, o, lse, exchange=None):
    t = qc.shape[0]
    tq, tk = min(ATT_BWD_TQ, t), min(ATT_BWD_TK, t)
    npair = MLA_HEADS // 2
    nq = t // tq
    sub = tq // tk
    assert sub * tk == tq

    def body(q_ref, do_ref, o_ref, lse_ref, k_ref, v_ref, *rest):
        if exchange is None:
            dq_ref, dk_ref, dv_ref, dl_ref = rest
        else:
            sb_ref, dq_ref, dk_ref, dv_ref, recv_ref, dl_ref, sems = rest
            first, last = _first_last_step(npair, t // tk)
            pl.when(first)(lambda: _chips_start(sb_ref, recv_ref, sems))
            pl.when(last)(lambda: _chips_finish(sb_ref, recv_ref, sems))
        j = pl.program_id(1)

        @pl.when(j == 0)
        def _():
            dq_ref[...] = jnp.zeros_like(dq_ref)
            lo, _ = _pair_masks((tq, LANES))

            def fill_delta(i, carry):
                rows = pl.ds(pl.multiple_of(i * tq, tq), tq)
                prod = do_ref[rows, :].astype(f32) * o_ref[rows, :].astype(f32)
                dl_ref[rows, :] = jnp.where(lo, *[jnp.sum(prod[:, _v_cols(e)], axis=-1, keepdims=True) for e in range(2)])
                return carry

            lax.fori_loop(0, nq, fill_delta, 0)

        k = [k_ref[:, _qk_cols(e)] for e in range(2)]
        vv = [v_ref[:, _v_cols(e)] for e in range(2)]

        def rows_step(carry, row0, rows, masked):
            qs = pl.ds(pl.multiple_of(row0, rows), rows)
            ok = _causal((rows, tk), row0, j * tk) if masked else None
            lse2, dl2 = lse_ref[qs, :], dl_ref[qs, :]
            new = []
            for e in range(2):
                dk, dv = carry[e]
                q_e, do_e = q_ref[qs, _qk_cols(e)], do_ref[qs, _v_cols(e)]
                p = jnp.exp(_bd(q_e, k[e], 1, 1) - lse2[:, MLA_ROPE * e:MLA_ROPE * e + 1])
                if masked:
                    p = jnp.where(ok, p, 0.0)
                dv = dv + _bd(p, do_e, 0, 0)
                dp = _bd(do_e, vv[e], 1, 1)
                ds = (p * (dp - dl2[:, MLA_ROPE * e:MLA_ROPE * e + 1])).astype(bf16)
                dk = dk + _bd(ds, q_e, 0, 0)
                dq_ref[qs, _qk_cols(e)] += _bd(ds, k[e], 1, 0)
                new.append((dk, dv))
            return tuple(new)

        one = (jnp.zeros((tk, ATT_QK), f32), jnp.zeros((tk, MLA_V), f32))
        i0 = (j * tk) // tq
        j_local = j - i0 * sub
        carry = (one, one)
        for r in range(sub):
            run = functools.partial(rows_step, row0=i0 * tq + r * tk, rows=tk, masked=True)
            carry = run(carry) if r == sub - 1 else lax.cond(r >= j_local, run, lambda c: c, carry)
        carry = lax.fori_loop(i0 + 1, nq, lambda i, c: rows_step(c, i * tq, tq, False), carry)
        for e in range(2):
            dk_ref[:, _qk_cols(e)] = carry[e][0].astype(dk_ref.dtype)
            dv_ref[:, _v_cols(e)] = carry[e][1].astype(dv_ref.dtype)

    res = lambda w: pl.BlockSpec((t, w), lambda p, j: (0, p))
    blk = lambda w: pl.BlockSpec((tk, w), lambda p, j: (j, p))
    in_specs = [res(2 * ATT_QK), res(2 * MLA_V), res(2 * MLA_V), res(LANES), blk(2 * ATT_QK), blk(2 * MLA_V)]
    out_specs = [res(2 * ATT_QK), blk(2 * ATT_QK), blk(2 * MLA_V)]
    out_shape = [_sds((t, MLA_HEADS * ATT_QK), f32), _sds((t, MLA_HEADS * ATT_QK), bf16), _sds((t, MLA_HEADS * MLA_V), bf16)]
    scratch = [pltpu.VMEM((t, LANES), f32)]
    if exchange is None:
        return _pc(body, name="attn_bwd", grid=(npair, t // tk), in_specs=in_specs, out_specs=out_specs,
                   out_shape=out_shape, scratch=scratch, sem=("arbitrary", "arbitrary"))(qc, do, o, lse, kc, v)
    return _pc(body, name="attn_bwd_exchange", grid=(npair, t // tk), in_specs=in_specs + [HBM],
               out_specs=out_specs + [HBM], out_shape=out_shape + [_sds((3,) + exchange.shape[1:], bf16)],
               scratch=scratch + [pltpu.SemaphoreType.DMA((2, 3))], sem=("arbitrary", "arbitrary"))(
                   qc, do, o, lse, kc, v, exchange)


def _rope_tables(t):
    half = MLA_ROPE // 2
    inv_freq = ROPE_THETA ** (-jnp.arange(half, dtype=f32) / half)
    ang = jnp.arange(t, dtype=f32)[:, None] * inv_freq[None, :]
    cos, sin = jnp.cos(ang), jnp.sin(ang)
    return jnp.concatenate([cos, cos] * 2, axis=1), jnp.concatenate([-sin, sin] * 2, axis=1)


def _relu2_epi(u):
    r = jnp.maximum(u, 0.0)
    return u, r * r


def _add_epi(r, res):
    return (r + res,)


def _drelu2_epi(da, u):
    return (da * 2.0 * jnp.maximum(u.astype(f32), 0.0),)


ROWWISE_EPI_TM = 512


def _residual_norms_epi(r, res, *gains):
    h = r + res
    return (h, *[_rms(h, g) for g in gains])


def _residual_out(a, w, h, gains, name):
    res = _mm(a, w, name=name, outs=(f32,) + (bf16,) * len(gains), epi=_residual_norms_epi, extras=(h, *gains),
              tm=ROWWISE_EPI_TM)
    return res if gains else [res]


def _dnorm_epi(dy, x, dres, gain):
    dx, dg = _rms_bwd(x, gain, dy)
    return dx + dres, dx + dres, dg


def _mlp_fwd(h, xm, w_up, w_down, tag, next_gains, gather=None):
    u, a, *gathered = _mm(xm, w_up, name=f"mlp{tag}_up", outs=(bf16, bf16), epi=_relu2_epi, gather=gather)
    h_out, *normed = _residual_out(a, w_down, h, next_gains, f"mlp{tag}_down")
    return h_out, normed, (xm, u, a), gathered


def _mlp_bwd(dh, dh16, h, gain, w_up, w_down, saved, tag):
    xm, u, a = saved
    du = _mm(dh16, w_down, tb=True, name=f"mlp{tag}_dact", outs=(bf16,), epi=_drelu2_epi, extras=(u,))
    d_down = _wgrad(a, dh16, f"mlp{tag}_dwdown")
    d_up = _wgrad(xm, du, f"mlp{tag}_dwup")
    dh_in, dh_in16, d_gain = _mm(du, w_up, tb=True, name=f"mlp{tag}_dxm", outs=(f32, bf16), accs=1, epi=_dnorm_epi,
                                 extras=(h, dh, gain), tm=ROWWISE_EPI_TM)
    return dh_in, dh_in16, d_gain, d_up, d_down


def _local_step(x, target, w, comm=None):
    w = dict(w)
    t = x.shape[0]
    cos, sgn_sin = _rope_tables(t)
    grads = {}

    xn0 = _rw(lambda xx, g: (_rms(xx, g),), [x], [w["hgrn_norm"]], [(D_MODEL, bf16)], name="hgrn_norm")[0]
    p4 = _mm(xn0, w["hgrn_w4"], name="hgrn_proj", tn=2048)

    if comm is None:
        y, states = _gla_fwd(p4, w["hgrn_lb_logits"], w["hgrn_g_norm"])
    else:
        y, states, gathered = _gla_fwd(p4, w["hgrn_lb_logits"], w["hgrn_g_norm"], gather=comm.shard["gla"])
        w.update(comm.unpack["gla"](gathered))
    h1, xm0 = _residual_out(y, w["hgrn_w_o"], x, [w["mlp_norm"][0:1]], "hgrn_out")
    h2, (hk, xn1), mlp0, gathered = _mlp_fwd(h1, xm0, w["mlp_w_up", 0], w["mlp_w_down", 0], 0,
                                             [w["kv_in_norm"], w["mla_norm"]],
                                             gather=None if comm is None else comm.shard["mlp0_up"])
    if comm is not None:
        w.update(comm.unpack["mlp0_up"](gathered[0]))

    ckr = _mm(hk, w["kv_w_dkv"], name="kv_down")

    def ckv_fn(c, cs, sn, g):
        kr = _rope(c[:, MLA_KV_LORA:], cs, sn)
        return _rms(c[:, :MLA_KV_LORA], g), jnp.concatenate([jnp.zeros_like(kr), kr], axis=1)

    c_kv, kr_head = _rw(ckv_fn, [ckr, cos, sgn_sin], [w["kv_norm"]], [(MLA_KV_LORA, bf16), (ATT_QK, f32)],
                        name="kv_norm_rope")
    kc = _mm(c_kv, w["kv_w_kcat"], name="kv_up_k", outs=(bf16,), extras=(kr_head,),
             epi=lambda r, kr: (r + _tile_lanes(kr, r.shape[1]),))
    v_att = _mm(c_kv, w["kv_w_uv"], name="kv_up_v", outs=(bf16,))
    cq0, c_q = _mm(xn1, w["mla_w_dq"], name="q_down", outs=(f32, bf16), extras=(w["mla_q_norm"],),
                   epi=lambda r, g: (r, _rms(r, g)))
    qc = _mm(c_q, w["mla_w_qcat"], name="q_up", outs=(bf16,), extras=(cos, sgn_sin),
             epi=lambda r, cs, sn: (_rope_heads(r, cs, sn, 1.0, ATT_SCALE),))
    if comm is None:
        o_att, lse = _attn_fwd(qc, kc, v_att)
    else:
        o_att, lse, gathered = _attn_fwd(qc, kc, v_att, gather=comm.shard["attn"])
        w.update(comm.unpack["attn"](gathered))
    h3, xm1 = _residual_out(o_att, w["mla_w_o"], h2, [w["mlp_norm"][1:2]], "mla_out")
    u1, a1 = _mm(xm1, w["mlp_w_up", 1], name="mlp1_up", outs=(bf16, bf16), epi=_relu2_epi)
    mlp1 = (xm1, u1, a1)

    def loss_epi(r, res, tgt, gain):
        def f(a, b):
            e = _rms(a, b) - tgt
            return 0.5 * jnp.sum(jnp.sum(e * e, axis=-1, keepdims=True) / D_MODEL, axis=0, keepdims=True)
        val, vjp = jax.vjp(f, r + res, gain)
        dh, dg = vjp(jnp.ones((1, 1), f32))
        return dh, dh, jnp.broadcast_to(val, (1, D_MODEL)), dg

    dh4, dh4_16, loss_acc, grads["final_norm"] = _mm(
        a1, w["mlp_w_down", 1], name="mlp1_down_loss", outs=(f32, bf16), accs=2, epi=loss_epi,
        extras=(h3, target, w["final_norm"]), tm=ROWWISE_EPI_TM)
    loss = loss_acc[0, 0]

    dh3, dh3_16, g_n1, grads["mlp_w_up", 1], grads["mlp_w_down", 1] = _mlp_bwd(
        dh4, dh4_16, h3, w["mlp_norm"][1:2], w["mlp_w_up", 1], w["mlp_w_down", 1], mlp1, 1)
    do_att = _mm(dh3_16, w["mla_w_o"], tb=True, name="mla_dout", outs=(bf16,))
    grads["mla_w_o"] = _wgrad(o_att, dh3_16, "mla_dwo")

    if comm is None:
        dqc, dkc, dv = _attn_bwd(qc, kc, v_att, do_att, o_att, lse)
    else:
        dqc, dkc, dv, comm.received["attn"] = _attn_bwd(qc, kc, v_att, do_att, o_att, lse,
                                                        exchange=comm.reduce(grads, "attn"))
    dqf = _rw(lambda a, cs, sn: (_rope_heads(a, cs, sn, -1.0, ATT_SCALE),), [dqc, cos, sgn_sin], [],
              [(MLA_HEADS * ATT_QK, bf16)], name="dq_rope")[0]
    dcq0, grads["mla_q_norm"] = _mm(dqf, w["mla_w_qcat"], tb=True, name="q_up_dx", outs=(bf16,), accs=1,
                                    extras=(cq0, w["mla_q_norm"]), epi=lambda dy, c, g: _rms_bwd(c, g, dy))
    grads["mla_w_qcat"] = _wgrad(c_q, dqf, "q_up_dw")
    dxn1 = _mm(dcq0, w["mla_w_dq"], tb=True, name="q_down_dx")
    grads["mla_w_dq"] = _wgrad(xn1, dcq0, "q_down_dw")

    dc_kv = _mm(dkc, w["kv_w_kcat"], tb=True, name="kv_up_dx_k")
    dc_kv = _mm(dv, w["kv_w_uv"], tb=True, name="kv_up_dx_v", epi=_add_epi, extras=(dc_kv,))
    grads["kv_w_kcat"] = _wgrad(c_kv, dkc, "kv_up_dw_k")
    grads["kv_w_uv"] = _wgrad(c_kv, dv, "kv_up_dw_v")

    def dckr_fn(c, dc, dk_heads, cs, sn, g):
        tot = dk_heads[:, LANES:ATT_QK].astype(f32)
        for h in range(1, MLA_HEADS):
            tot = tot + dk_heads[:, ATT_QK * h + LANES:ATT_QK * (h + 1)].astype(f32)
        lo, _ = _pair_masks(tot.shape)
        dkr = jnp.where(lo, _rope(tot, cs, sn, -1.0), 0.0)
        dcc, dg = _rms_bwd(c[:, :MLA_KV_LORA], g, dc)
        return jnp.concatenate([dcc, dkr], axis=1), dg

    dckr, grads["kv_norm"] = _rw(dckr_fn, [ckr, dc_kv, dkc, cos, sgn_sin], [w["kv_norm"]],
                                 [(MLA_KV_LORA + LANES, bf16)], [(1, MLA_KV_LORA)], name="kv_dnorm_rope")
    grads["kv_w_dkv"] = _wgrad(hk, dckr, "kv_down_dw")

    def dh2_epi(d1, hh, d2, dres, g1, g2):
        a, ga = _rms_bwd(hh, g1, d1)
        b, gb = _rms_bwd(hh, g2, d2)
        return a + b + dres, a + b + dres, ga, gb

    dh2, dh2_16, grads["kv_in_norm"], grads["mla_norm"] = _mm(
        dckr, w["kv_w_dkv"], tb=True, name="kv_down_dx", outs=(f32, bf16), accs=2, epi=dh2_epi,
        extras=(h2, dxn1, dh3, w["kv_in_norm"], w["mla_norm"]), tm=ROWWISE_EPI_TM)

    dh1, dh1_16, g_n0, grads["mlp_w_up", 0], grads["mlp_w_down", 0] = _mlp_bwd(
        dh2, dh2_16, h1, w["mlp_norm"][0:1], w["mlp_w_up", 0], w["mlp_w_down", 0], mlp0, 0)
    grads["mlp_norm"] = jnp.concatenate([g_n0, g_n1], axis=0)
    dy = _mm(dh1_16, w["hgrn_w_o"], tb=True, name="hgrn_dout")
    grads["hgrn_w_o"] = _wgrad(y, dh1_16, "hgrn_dwo")

    gla_args = (p4, w["hgrn_lb_logits"], w["hgrn_g_norm"], states, dy)
    if comm is None:
        dp4, grads["hgrn_lb_logits"], grads["hgrn_g_norm"] = _gla_bwd(*gla_args)
    else:
        dp4, grads["hgrn_lb_logits"], grads["hgrn_g_norm"], comm.received["gla"] = _gla_bwd(
            *gla_args, exchange=comm.reduce(grads, "gla"))
    grads["hgrn_w4"] = _mm(xn0, dp4, ta=True, name="hgrn_proj_dw")
    grad_x, grads["hgrn_norm"], *received = _mm(
        dp4, w["hgrn_w4"], tb=True, name="hgrn_proj_dx", outs=(f32,), accs=1, epi=lambda *args: _dnorm_epi(*args)[1:],
        extras=(x, dh1, w["hgrn_norm"]), tm=ROWWISE_EPI_TM, exchange=None if comm is None else comm.reduce(grads, "late"))
    if comm is not None:
        comm.received["late"] = received[0]
    return loss, grad_x, grads


HBM = pl.BlockSpec(memory_space=pltpu.HBM)


def _me():
    return lax.axis_index("x"), lax.axis_index("y"), lax.axis_index("c")


def _flip(x, y, f):
    return (1 - x if f & 1 else x), (1 - y if f & 2 else y)


def _rcopy(src, dst, sems, k, dev):
    return pltpu.make_async_remote_copy(src_ref=src, dst_ref=dst, send_sem=sems.at[0, k], recv_sem=sems.at[1, k],
                                        device_id=dev, device_id_type=MESH)


def _my_half(rows, c, mine=True):
    half = rows // 2
    return pl.ds(pl.multiple_of((c if mine else 1 - c) * half, 16), half)


def _gather_start(wp_ref, out_ref, sems):
    x, y, c = _me()
    half = _my_half(wp_ref.shape[0], c)
    for f in (1, 2, 3):
        px, py = _flip(x, y, f)
        _rcopy(wp_ref.at[half], out_ref.at[2 * x + y, half], sems, f - 1, (px, py, c)).start()


def _gather_finish(wp_ref, out_ref, sems):
    x, y, c = _me()
    half, other = _my_half(wp_ref.shape[0], c), _my_half(wp_ref.shape[0], c, mine=False)
    sends = []
    for f in (1, 2, 3):
        px, py = _flip(x, y, f)
        landed = out_ref.at[2 * px + py, half]
        _rcopy(landed, landed, sems, f - 1, (px, py, c)).wait_recv()
        sends.append(_rcopy(landed, landed, sems, 2 + f, (x, y, 1 - c)))
        sends[-1].start()
    for f in (1, 2, 3):
        px, py = _flip(x, y, f)
        theirs = out_ref.at[2 * px + py, other]
        _rcopy(theirs, theirs, sems, 2 + f, (x, y, 1 - c)).wait_recv()
        sends.append(_rcopy(wp_ref.at[half], out_ref.at[2 * x + y, half], sems, f - 1, (px, py, c)))
    for cp in sends:
        cp.wait_send()


def _gather_base(wp):
    return jnp.broadcast_to(wp[None], (4,) + wp.shape)


def _all_gather_weights(wp, sv):
    def body(wp_ref, sv_ref, base_ref, out_ref, svs_ref, sems, local_sem):
        x, y, c = _me()
        mine = pltpu.make_async_copy(sv_ref, svs_ref.at[2 * x + y], local_sem)
        mine.start()
        _gather_start(wp_ref, out_ref, sems)
        small = []
        for f in (1, 2, 3):
            px, py = _flip(x, y, f)
            small.append(_rcopy(sv_ref, svs_ref.at[2 * x + y], sems, 5 + f, (px, py, c)))
            small[-1].start()
        _gather_finish(wp_ref, out_ref, sems)
        for f in (1, 2, 3):
            px, py = _flip(x, y, f)
            _rcopy(sv_ref, svs_ref.at[2 * px + py], sems, 5 + f, (px, py, c)).wait_recv()
        for cp in small:
            cp.wait_send()
        mine.wait()

    return _pc(body, name="weights_all_gather", in_specs=[HBM, HBM, HBM], out_specs=[HBM, HBM],
               out_shape=[_sds((4,) + wp.shape, bf16), _sds((4, 8, 256), f32)], aliases={2: 0},
               scratch=[pltpu.SemaphoreType.DMA((2, 9)), pltpu.SemaphoreType.DMA])(wp, sv, _gather_base(wp))


def _send_half_to_sibling(gp, name):
    rows = gp.shape[1]

    def body(gp_ref, out_ref, sems):
        x, y, c = _me()
        cp = _rcopy(gp_ref.at[:, _my_half(rows, c, mine=False)], out_ref, sems, 0, (x, y, 1 - c))
        cp.start()
        cp.wait()

    return _pc(body, name=name, in_specs=[HBM], out_specs=HBM, out_shape=_sds((4, rows // 2, D_MODEL), gp.dtype),
               scratch=[pltpu.SemaphoreType.DMA((2, 1))])(gp)


def _chips_start(sb_ref, out_ref, sems):
    x, y, c = _me()
    for f in (1, 2, 3):
        px, py = _flip(x, y, f)
        _rcopy(sb_ref.at[2 * px + py], out_ref.at[f - 1], sems, f - 1, (px, py, c)).start()


def _chips_finish(sb_ref, out_ref, sems):
    x, y, c = _me()
    for f in (1, 2, 3):
        _rcopy(sb_ref.at[0], out_ref.at[f - 1], sems, f - 1, (x, y, c)).wait_recv()
    for f in (1, 2, 3):
        px, py = _flip(x, y, f)
        _rcopy(sb_ref.at[2 * px + py], out_ref.at[f - 1], sems, f - 1, (px, py, c)).wait_send()


def _final_exchange(small, tots):
    n = len(tots)

    def body(small_ref, *refs):
        tot_refs, smalls_ref, out_refs, sems, local_sem = refs[:n], refs[n], refs[n + 1:2 * n + 1], refs[-2], refs[-1]
        x, y, c = _me()
        me = 4 * x + 2 * y + c
        mine = pltpu.make_async_copy(small_ref, smalls_ref.at[me], local_sem)
        mine.start()
        sends = []
        for f in range(1, 8):
            px, py = _flip(x, y, f)
            pc = 1 - c if f & 4 else c
            sends.append(_rcopy(small_ref, smalls_ref.at[me], sems, f - 1, (px, py, pc)))
        for i in range(n):
            half = _my_half(tot_refs[i].shape[0], c)
            sends.append(_rcopy(tot_refs[i].at[half], out_refs[i].at[half], sems, 7 + i, (x, y, 1 - c)))
        for cp in sends:
            cp.start()
        for f in range(1, 8):
            px, py = _flip(x, y, f)
            pc = 1 - c if f & 4 else c
            _rcopy(small_ref, smalls_ref.at[4 * px + 2 * py + pc], sems, f - 1, (x, y, c)).wait_recv()
        for i in range(n):
            theirs = out_refs[i].at[_my_half(tot_refs[i].shape[0], c, mine=False)]
            _rcopy(theirs, theirs, sems, 7 + i, (x, y, 1 - c)).wait_recv()
        for cp in sends:
            cp.wait_send()
        mine.wait()

    return _pc(body, name="final_exchange", in_specs=[HBM] * (1 + n), out_specs=[HBM] * (1 + n),
               out_shape=[_sds((8, SMALL_ROWS, D_MODEL), f32)] + [_sds(t.shape, f32) for t in tots],
               aliases={1 + i: 1 + i for i in range(n)},
               scratch=[pltpu.SemaphoreType.DMA((2, 7 + n)), pltpu.SemaphoreType.DMA])(small, *tots)


def _sum_rows(half):
    return max(r for r in range(16, 513, 16) if half % r == 0)


def _sum_over_cores(gp, recv, cq, name):
    half = recv.shape[1]
    tr = _sum_rows(half)
    nb = half // tr

    def body(cq_ref, g_ref, r_ref, o32_ref, o16_ref):
        s = g_ref[...].astype(f32) + r_ref[...].astype(f32)
        o32_ref[...] = s
        o16_ref[...] = s.astype(bf16)

    spec = pl.BlockSpec((1, tr, D_MODEL), lambda b, i, cq_ref: (b, i, 0))
    gs = pltpu.PrefetchScalarGridSpec(
        num_scalar_prefetch=1, grid=(4, nb),
        in_specs=[pl.BlockSpec((1, tr, D_MODEL), lambda b, i, cq_ref: (b, cq_ref[0] * nb + i, 0)), spec],
        out_specs=[spec, spec])
    return _pc(body, name=name, grid_spec=gs, sem=("arbitrary", "arbitrary"),
               out_shape=[_sds((4, half, D_MODEL), f32), _sds((4, half, D_MODEL), bf16)])(cq, gp, recv)


def _sum_over_chips(s32, recv, cq, name):
    half = recv.shape[1]
    tr = _sum_rows(half)
    nb = half // tr

    def body(cq_ref, own_ref, r_ref, o_ref):
        o_ref[...] = ((own_ref[0] + r_ref[0].astype(f32)) + r_ref[1].astype(f32)) + r_ref[2].astype(f32)

    gs = pltpu.PrefetchScalarGridSpec(
        num_scalar_prefetch=1, grid=(nb,),
        in_specs=[pl.BlockSpec((1, tr, D_MODEL), lambda i, cq_ref: (cq_ref[1], i, 0)),
                  pl.BlockSpec((3, tr, D_MODEL), lambda i, cq_ref: (0, i, 0))],
        out_specs=pl.BlockSpec((tr, D_MODEL), lambda i, cq_ref: (cq_ref[0] * nb + i, 0)))
    return _pc(body, name=name, grid_spec=gs, sem=("arbitrary",),
               out_shape=_sds((2 * half, D_MODEL), f32))(cq, s32, recv)


def _sum_small(smalls):
    def body(s_ref, o_ref):
        tot = s_ref[0]
        for d in range(1, 8):
            tot = tot + s_ref[d]
        o_ref[...] = tot

    return _pc(body, name="small_sum", out_shape=_sds((SMALL_ROWS, D_MODEL), f32))(smalls)


def _adamw_math(w, g, m, v):
    m = ADAM_B1 * m + (1.0 - ADAM_B1) * g
    v = ADAM_B2 * v + (1.0 - ADAM_B2) * jnp.square(g)
    m_hat = m / (1.0 - ADAM_B1 ** ADAM_STEP)
    v_hat = v / (1.0 - ADAM_B2 ** ADAM_STEP)
    delta = -ADAM_LR * (m_hat / (jnp.sqrt(v_hat) + ADAM_EPS) + ADAM_WD * w)
    return delta, m, v


def _adamw(w, g, m, v, name):
    cols = w.shape[1]
    return _rw(_adamw_math, [w, g, m, v], [], [(cols, f32)] * 3, name=name, tr=256)


def _adamw_small(items, name):
    n = len(items)

    def body(*refs):
        ins, outs = refs[:4 * n], refs[4 * n:]
        for i in range(n):
            res = _adamw_math(*[r[...] for r in ins[4 * i:4 * i + 4]])
            for o, val in zip(outs[3 * i:3 * i + 3], res):
                o[...] = val

    flat = [a for it in items for a in it]
    out_shape = [_sds(it[0].shape, f32) for it in items for _ in range(3)]
    res = _pc(body, name=name, out_shape=out_shape)(*flat)
    return [tuple(res[3 * i:3 * i + 3]) for i in range(n)]


def _pack_shards(sh, layout, pad):
    parts = [(sh[n] if layer is None else sh[n][layer]).reshape(-1, D_MODEL).astype(bf16) for n, layer, _ in layout]
    if pad:
        parts.append(jnp.zeros((pad, D_MODEL), bf16))
    return jnp.concatenate(parts, axis=0)


def _mlp_full(g4, off, layer):
    o, r = off["mlp_w_up", layer]
    up = g4[:, o:o + r].transpose(1, 0, 2).reshape(D_MODEL, D_FF)
    o, r = off["mlp_w_down", layer]
    return {("mlp_w_up", layer): up, ("mlp_w_down", layer): g4[:, o:o + r].reshape(D_FF, D_MODEL)}


def _unpack_early(g4):
    hg = g4[:, 0:1024].reshape(4, 4, 256, D_MODEL)
    o, r = W_EARLY_OFF["hgrn_w_o", None]
    return {"hgrn_w4": hg.transpose(0, 2, 1, 3).reshape(D_MODEL, 4 * D_MODEL),
            "hgrn_w_o": g4[:, o:o + r].reshape(D_MODEL, D_MODEL)}


def _unpack_gla(g4):
    return _mlp_full(g4, W_GLA_OFF, 0)


def _unpack_last(g4):
    return _mlp_full(g4, W_LAST_OFF, 1)


def _unpack_mid(g4):
    def rows(name):
        o, r = W_MID_OFF[name, None]
        return g4[:, o:o + r]

    w = {"mla_w_dq": rows("mla_w_dq").reshape(D_MODEL, MLA_Q_LORA)}
    uq = rows("mla_w_uq").reshape(4, MLA_Q_LORA, 768).transpose(1, 0, 2).reshape(MLA_Q_LORA, MLA_HEADS, MLA_NOPE + MLA_ROPE)
    w["mla_w_qcat"] = jnp.pad(uq, ((0, 0), (0, 0), (0, ATT_QK - MLA_NOPE - MLA_ROPE))).reshape(MLA_Q_LORA, MLA_HEADS * ATT_QK)
    w["mla_w_o"] = rows("mla_w_o").reshape(MLA_HEADS * MLA_V, D_MODEL)
    dkv = rows("kv_w_dkv").reshape(D_MODEL, MLA_KV_LORA + MLA_ROPE)
    w["kv_w_dkv"] = jnp.pad(dkv, ((0, 0), (0, LANES - MLA_ROPE)))
    uk = rows("kv_w_uk").reshape(4, MLA_KV_LORA, 512).transpose(1, 0, 2).reshape(MLA_KV_LORA, MLA_HEADS, MLA_NOPE)
    w["kv_w_kcat"] = jnp.pad(uk, ((0, 0), (0, 0), (0, ATT_QK - MLA_NOPE))).reshape(MLA_KV_LORA, MLA_HEADS * ATT_QK)
    w["kv_w_uv"] = rows("kv_w_uv").reshape(4, MLA_KV_LORA, 512).transpose(1, 0, 2).reshape(MLA_KV_LORA, MLA_HEADS * MLA_V)
    return w


def _pack_grads_late(g):
    return g["hgrn_w4"].reshape(4, 256, 4, D_MODEL).transpose(0, 2, 1, 3).reshape(4, G_LATE_ROWS, D_MODEL)


def _grad_rows(g, name, layer):
    if name in ("mlp_w_up", "mlp_w_down"):
        full = g[name, layer]
        return full.reshape(D_MODEL, 4, 1024).transpose(1, 0, 2) if name == "mlp_w_up" else full.reshape(4, 1024, D_MODEL)
    if name == "mla_w_uq":
        uq = g["mla_w_qcat"].reshape(MLA_Q_LORA, MLA_HEADS, ATT_QK)[:, :, :MLA_NOPE + MLA_ROPE]
        return uq.reshape(MLA_Q_LORA, 4, 768).transpose(1, 0, 2).reshape(4, 192, D_MODEL)
    if name == "kv_w_uk":
        uk = g["kv_w_kcat"].reshape(MLA_KV_LORA, MLA_HEADS, ATT_QK)[:, :, :MLA_NOPE]
        return uk.reshape(MLA_KV_LORA, 4, 512).transpose(1, 0, 2).reshape(4, 128, D_MODEL)
    if name == "kv_w_uv":
        return g[name].reshape(MLA_KV_LORA, 4, 512).transpose(1, 0, 2).reshape(4, 128, D_MODEL)
    if name == "kv_w_dkv":
        return g[name][:, :MLA_KV_LORA + MLA_ROPE].reshape(4, 80, D_MODEL)
    return g[name].reshape(4, -1, D_MODEL)


def _pack_grads(g, layout):
    parts = [_grad_rows(g, name, layer) for name, layer, _ in layout]
    if layout in PADDED:
        parts.append(jnp.zeros((4, PACK_PAD, D_MODEL), bf16))
    return jnp.concatenate(parts, axis=1)


LOSS_ROW = 11


def _pack_small(g, loss):
    rows = []
    for name, _, r, wd in SMALL:
        a = g[name].reshape(r, wd)
        rows.append(jnp.pad(a, ((0, 0), (0, D_MODEL - wd))) if wd < D_MODEL else a)
    assert sum(r for _, _, r, _ in SMALL) == LOSS_ROW
    rows.append(jnp.full((1, D_MODEL), loss, f32))
    rows.append(jnp.zeros((SMALL_ROWS - LOSS_ROW - 1, D_MODEL), f32))
    return jnp.concatenate(rows, axis=0)


def kernel(x, hgrn_norm, hgrn_w_q, hgrn_w_f, hgrn_w_i, hgrn_w_g, hgrn_g_norm, hgrn_w_o, hgrn_lb_logits, mla_norm, mla_w_dq, mla_q_norm, mla_w_uq, mla_w_o, kv_in_norm, kv_w_dkv, kv_norm, kv_w_uk, kv_w_uv, mlp_norm, mlp_w_up, mlp_w_down, final_norm, loss_target, m_hgrn_norm, m_hgrn_w_q, m_hgrn_w_f, m_hgrn_w_i, m_hgrn_w_g, m_hgrn_g_norm, m_hgrn_w_o, m_hgrn_lb_logits, m_mla_norm, m_mla_w_dq, m_mla_q_norm, m_mla_w_uq, m_mla_w_o, m_kv_in_norm, m_kv_w_dkv, m_kv_norm, m_kv_w_uk, m_kv_w_uv, m_mlp_norm, m_mlp_w_up, m_mlp_w_down, m_final_norm, v_hgrn_norm, v_hgrn_w_q, v_hgrn_w_f, v_hgrn_w_i, v_hgrn_w_g, v_hgrn_g_norm, v_hgrn_w_o, v_hgrn_lb_logits, v_mla_norm, v_mla_w_dq, v_mla_q_norm, v_mla_w_uq, v_mla_w_o, v_kv_in_norm, v_kv_w_dkv, v_kv_norm, v_kv_w_uk, v_kv_w_uv, v_mlp_norm, v_mlp_w_up, v_mlp_w_down, v_final_norm):
    given = dict(locals())
    wsh = {n: given[n] for n in WEIGHTS}
    msh = {n: given["m_" + n] for n in WEIGHTS}
    vsh = {n: given["v_" + n] for n in WEIGHTS}
    xi, yi, ci = _me()
    chip = 2 * xi + yi
    cq = jnp.stack([ci, chip]).astype(jnp.int32)

    small_w = {n: wsh[n].reshape(r, -1) for n, _, r, _ in SMALL}
    sv = jnp.concatenate([small_w["hgrn_norm"], small_w["hgrn_lb_logits"], jnp.zeros((5, 256), f32)], axis=0)
    g4, sv4 = _all_gather_weights(_pack_shards(wsh, W_EARLY, 0), sv)
    w = _unpack_early(g4)
    w["hgrn_norm"] = sv4[:, 0, :].reshape(1, D_MODEL)
    w["hgrn_lb_logits"] = sv4[:, 1:3, :].transpose(1, 0, 2).reshape(2, D_MODEL)
    for n in ("hgrn_g_norm", "mla_norm", "mla_q_norm", "kv_in_norm", "kv_norm", "mlp_norm", "final_norm"):
        w[n] = small_w[n]

    class Comm:
        shard = {"gla": _pack_shards(wsh, W_GLA, 0), "mlp0_up": _pack_shards(wsh, W_MID, PACK_PAD),
                 "attn": _pack_shards(wsh, W_LAST, 0)}
        unpack = {"gla": _unpack_gla, "mlp0_up": _unpack_mid, "attn": _unpack_last}
        layout = {"gla": G_GLA, "attn": G_ATTN}
        received, s32 = {}, {}

        @staticmethod
        def reduce(grads, part):
            gp = _pack_grads_late(grads) if part == "late" else _pack_grads(grads, Comm.layout[part])
            Comm.s32[part], s16 = _sum_over_cores(gp, _send_half_to_sibling(gp, "grads_to_sibling_" + part), cq,
                                                  "grads_sum_cores_" + part)
            return s16

    loss, grad_x, g = _local_step(x.reshape(-1, D_MODEL), loss_target.reshape(-1, D_MODEL), w, Comm)

    parts = ("late", "gla", "attn")
    halves = [_sum_over_chips(Comm.s32[p], Comm.received[p], cq, "grads_sum_chips_" + p) for p in parts]
    smalls, *totals = _final_exchange(_pack_small(g, loss), halves)
    total = dict(zip(parts, totals))
    small_tot = _sum_small(smalls)
    loss = small_tot[LOSS_ROW, 0]

    where = {}
    for part, offsets in (("late", G_LATE_OFF), ("gla", G_GLA_OFF), ("attn", G_ATTN_OFF)):
        for (n, layer), (o, r) in offsets.items():
            where.setdefault(n, []).append(total[part][o:o + r])
    grad, delta, new_m, new_v = {}, {}, {}, {}
    groups = {"hgrn": [], "mla_kv": []}
    for n, pieces in where.items():
        shp = wsh[n].shape
        two_d = (-1, shp[-1])
        grad[n] = (pieces[0] if len(pieces) == 1 else jnp.concatenate(pieces, axis=0)).reshape(shp)
        operands = (wsh[n].reshape(two_d), grad[n].reshape(two_d), msh[n].reshape(two_d), vsh[n].reshape(two_d))
        if n.startswith("mlp"):
            res = _adamw(*operands, "adamw_" + n)
            delta[n], new_m[n], new_v[n] = (a.reshape(shp) for a in res)
        else:
            groups["hgrn" if n.startswith("hgrn") else "mla_kv"].append((n, operands))
    for gname, members in groups.items():
        for (n, _), res in zip(members, _adamw_small([ops for _, ops in members], "adamw_" + gname)):
            delta[n], new_m[n], new_v[n] = (a.reshape(wsh[n].shape) for a in res)
    items = []
    for n, row, r, wd in SMALL:
        gs = small_tot[row:row + r, :wd]
        if n in ("hgrn_norm", "hgrn_lb_logits"):
            gs = lax.dynamic_slice(gs, (0, 256 * chip), (r, 256))
        grad[n] = gs.reshape(wsh[n].shape)
        items.append((small_w[n], gs, msh[n].reshape(gs.shape), vsh[n].reshape(gs.shape)))
    for (n, _, _, _), (d, m2, v2) in zip(SMALL, _adamw_small(items, "adamw_small")):
        shp = wsh[n].shape
        delta[n], new_m[n], new_v[n] = d.reshape(shp), m2.reshape(shp), v2.reshape(shp)

    return (loss, grad_x.reshape(x.shape), *[grad[n] for n in WEIGHTS], *[delta[n] for n in WEIGHTS],
            *[new_m[n] for n in WEIGHTS], *[new_v[n] for n in WEIGHTS])
```

```python
import functools

import jax
import jax.numpy as jnp
from jax import lax
from jax.experimental import pallas as pl
from jax.experimental.pallas import tpu as pltpu

f32, bf16 = jnp.float32, jnp.bfloat16
HI = lax.Precision.HIGHEST
MESH = pl.DeviceIdType.MESH

D_MODEL = 1024
D_FF = 4096
EPS = 1e-6
HGRN_HEADS, HGRN_DK, HGRN_CHUNK, HGRN_SUB = 8, 128, 64, 16
MLA_HEADS, MLA_NOPE, MLA_ROPE, MLA_V = 16, 128, 64, 128
MLA_Q_LORA, MLA_KV_LORA = 256, 256
ROPE_THETA = 10000.0
ATT_SCALE = (MLA_NOPE + MLA_ROPE) ** -0.5
EXP_CLAMP = 80.0

ADAM_LR, ADAM_B1, ADAM_B2, ADAM_EPS, ADAM_WD, ADAM_STEP = 0.001, 0.9, 0.999, 1e-08, 0.01, 10

V7X_VMEM_BYTES = 64 * 1024 * 1024
VMEM_LIMIT = V7X_VMEM_BYTES - 8 * 1024 * 1024
LANES = 128

PACK_PAD = 16
W_EARLY = (("hgrn_w_q", None, 256), ("hgrn_w_f", None, 256), ("hgrn_w_i", None, 256), ("hgrn_w_g", None, 256),
           ("hgrn_w_o", None, 256))
W_GLA = (("mlp_w_up", 0, 1024), ("mlp_w_down", 0, 1024))
W_MID = (("mla_w_dq", None, 64), ("mla_w_uq", None, 192), ("mla_w_o", None, 512), ("kv_w_dkv", None, 80),
         ("kv_w_uk", None, 128), ("kv_w_uv", None, 128))
W_LAST = (("mlp_w_up", 1, 1024), ("mlp_w_down", 1, 1024))
G_LATE = (("hgrn_w_q", None, 256), ("hgrn_w_f", None, 256), ("hgrn_w_i", None, 256), ("hgrn_w_g", None, 256))
G_ATTN = (("mla_w_o", None, 512), ("mlp_w_up", 1, 1024), ("mlp_w_down", 1, 1024))
G_GLA = (("hgrn_w_o", None, 256), ("mla_w_dq", None, 64), ("mla_w_uq", None, 192), ("kv_w_dkv", None, 80),
         ("kv_w_uk", None, 128), ("kv_w_uv", None, 128), ("mlp_w_up", 0, 1024), ("mlp_w_down", 0, 1024))
PADDED = (W_MID, G_GLA)


def _offsets(layout):
    out, o = {}, 0
    for name, layer, rows in layout:
        out[name, layer] = (o, rows)
        o += rows
    return out, o + (PACK_PAD if layout in PADDED else 0)


W_EARLY_OFF, W_EARLY_ROWS = _offsets(W_EARLY)
W_GLA_OFF, W_GLA_ROWS = _offsets(W_GLA)
W_MID_OFF, W_MID_ROWS = _offsets(W_MID)
W_LAST_OFF, W_LAST_ROWS = _offsets(W_LAST)
G_LATE_OFF, G_LATE_ROWS = _offsets(G_LATE)
G_ATTN_OFF, G_ATTN_ROWS = _offsets(G_ATTN)
G_GLA_OFF, G_GLA_ROWS = _offsets(G_GLA)
assert all(r % 32 == 0 for r in (W_EARLY_ROWS, W_MID_ROWS, W_LAST_ROWS, G_LATE_ROWS, G_ATTN_ROWS, G_GLA_ROWS))

WEIGHTS = ("hgrn_norm", "hgrn_w_q", "hgrn_w_f", "hgrn_w_i", "hgrn_w_g", "hgrn_g_norm", "hgrn_w_o", "hgrn_lb_logits",
           "mla_norm", "mla_w_dq", "mla_q_norm", "mla_w_uq", "mla_w_o", "kv_in_norm", "kv_w_dkv", "kv_norm", "kv_w_uk",
           "kv_w_uv", "mlp_norm", "mlp_w_up", "mlp_w_down", "final_norm")
SMALL = (("hgrn_norm", 0, 1, 1024), ("hgrn_lb_logits", 1, 2, 1024), ("hgrn_g_norm", 3, 1, 128),
         ("mla_norm", 4, 1, 1024), ("mla_q_norm", 5, 1, 256), ("kv_in_norm", 6, 1, 1024), ("kv_norm", 7, 1, 256),
         ("mlp_norm", 8, 2, 1024), ("final_norm", 10, 1, 1024))
SMALL_ROWS = 16


def _pc(body, *, name, out_shape, grid=None, in_specs=None, out_specs=None, scratch=(), sem=None, grid_spec=None,
        aliases=None):
    params = pltpu.CompilerParams(dimension_semantics=sem, vmem_limit_bytes=VMEM_LIMIT)
    if grid_spec is not None:
        return pl.pallas_call(body, name=name, out_shape=out_shape, grid_spec=grid_spec, compiler_params=params,
                              interpret=False)
    kw = {k: v for k, v in (("grid", grid), ("in_specs", in_specs), ("out_specs", out_specs),
                            ("input_output_aliases", aliases)) if v is not None}
    return pl.pallas_call(body, name=name, out_shape=out_shape, scratch_shapes=list(scratch), compiler_params=params,
                          interpret=False, **kw)


def _sds(shape, dtype):
    return jax.ShapeDtypeStruct(tuple(shape), dtype)


def _mm(a, b, *, name, ta=False, tb=False, outs=(f32,), epi=None, extras=(), accs=0, gather=None, exchange=None,
        tm=1024, tn=1024, tk=4096):
    m, k = (a.shape[1], a.shape[0]) if ta else a.shape
    n = b.shape[0] if tb else b.shape[1]
    tm, tn, tk = min(tm, m), min(tn, n), min(tk, k)
    assert m % tm == 0 and n % tn == 0 and k % tk == 0, (name, m, n, k)
    nk = k // tk
    assert accs == 0 or (tn == n and nk == 1), name
    a_spec = pl.BlockSpec((tk, tm), lambda i, j, kk: (kk, i)) if ta else pl.BlockSpec((tm, tk), lambda i, j, kk: (i, kk))
    b_spec = pl.BlockSpec((tn, tk), lambda i, j, kk: (j, kk)) if tb else pl.BlockSpec((tk, tn), lambda i, j, kk: (kk, j))

    def extra_spec(e):
        if e.shape == (m, n):
            return pl.BlockSpec((tm, tn), lambda i, j, kk: (i, j))
        if e.shape[0] == m:
            return pl.BlockSpec((tm, e.shape[1]), lambda i, j, kk: (i, 0))
        return pl.BlockSpec((e.shape[0], tn), lambda i, j, kk: (0, j))

    e_specs = [extra_spec(e) for e in extras]
    dn = (((0 if ta else 1,), (1 if tb else 0,)), ((), ()))
    n_e, n_o = len(extras), len(outs)

    def finish(r, e_refs, o_refs):
        res = epi(r, *[e[...] for e in e_refs]) if epi is not None else (r,)
        for o, v in zip(o_refs[:n_o], res[:n_o]):
            o[...] = v.astype(o.dtype)
        for o, v in zip(o_refs[n_o:], res[n_o:]):
            @pl.when(pl.program_id(0) == 0)
            def _(o=o):
                o[...] = jnp.zeros_like(o)
            o[...] += v

    grid = (m // tm, n // tn, nk)
    assert gather is None or exchange is None
    n_g = 2 if gather is not None else (1 if exchange is not None else 0)

    def body(*refs):
        a_ref, b_ref = refs[0], refs[1]
        e_refs = refs[2:2 + n_e]
        o_refs = refs[2 + n_e + n_g:2 + n_e + n_g + n_o + accs]
        if n_g:
            src_ref, dst_ref, sems = refs[2 + n_e], refs[2 + n_e + n_g + n_o + accs], refs[-1]
            start, fin = (_gather_start, _gather_finish) if gather is not None else (_chips_start, _chips_finish)
            pid = [pl.program_id(d) for d in range(3)]
            pl.when((pid[0] == 0) & (pid[1] == 0) & (pid[2] == 0))(lambda: start(src_ref, dst_ref, sems))
            pl.when((pid[0] == grid[0] - 1) & (pid[1] == grid[1] - 1) & (pid[2] == grid[2] - 1))(
                lambda: fin(src_ref, dst_ref, sems))
        prod = lax.dot_general(a_ref[...].astype(bf16), b_ref[...].astype(bf16), dn, preferred_element_type=f32)
        if nk == 1:
            finish(prod, e_refs, o_refs)
            return
        acc = refs[2 + n_e + n_g + n_o + accs + (1 if n_g else 0)]
        kk = pl.program_id(2)

        @pl.when(kk == 0)
        def _():
            acc[...] = jnp.zeros_like(acc)

        acc[...] += prod

        @pl.when(kk == nk - 1)
        def _():
            finish(acc[...], e_refs, o_refs)

    out_specs = ([pl.BlockSpec((tm, tn), lambda i, j, kk: (i, j)) for _ in outs] +
                 [pl.BlockSpec((1, n), lambda i, j, kk: (0, 0))] * accs)
    out_shape = [_sds((m, n), dt) for dt in outs] + [_sds((1, n), f32)] * accs
    scratch = [pltpu.VMEM((tm, tn), f32)] if nk > 1 else []
    if not n_g:
        out = _pc(body, name=name, grid=grid, in_specs=[a_spec, b_spec] + e_specs, out_specs=out_specs,
                  out_shape=out_shape, scratch=scratch,
                  sem=("arbitrary" if accs else "parallel", "parallel", "arbitrary"))(a, b, *extras)
    elif exchange is not None:
        out = _pc(body, name=name + "_exchange", grid=grid, in_specs=[a_spec, b_spec] + e_specs + [HBM],
                  out_specs=out_specs + [HBM], out_shape=out_shape + [_sds((3,) + exchange.shape[1:], bf16)],
                  scratch=scratch + [pltpu.SemaphoreType.DMA((2, 3))], sem=("arbitrary",) * 3)(a, b, *extras, exchange)
    else:
        out = _pc(body, name=name + "_gather", grid=grid, in_specs=[a_spec, b_spec] + e_specs + [HBM, HBM],
                  out_specs=out_specs + [HBM], out_shape=out_shape + [_sds((4,) + gather.shape, bf16)],
                  aliases={2 + n_e + 1: n_o + accs}, scratch=scratch + [pltpu.SemaphoreType.DMA((2, 6))],
                  sem=("arbitrary",) * 3)(a, b, *extras, gather, _gather_base(gather))
    return out[0] if len(out) == 1 else out


def _wgrad(a, b, name):
    return _mm(a, b, ta=True, name=name, outs=(bf16,))


def _rw(fn, rows, bcast, outs, accs=(), *, name, tr=256):
    t = rows[0].shape[0]
    tr = min(tr, t)
    assert t % tr == 0
    n_r, n_b, n_o, n_a = len(rows), len(bcast), len(outs), len(accs)

    def body(*refs):
        r_refs = refs[:n_r]
        b_refs = refs[n_r:n_r + n_b]
        o_refs = refs[n_r + n_b:n_r + n_b + n_o]
        a_refs = refs[n_r + n_b + n_o:]
        res = fn(*[r[...] for r in r_refs], *[b[...] for b in b_refs])
        for o, v in zip(o_refs, res[:n_o]):
            o[...] = v.astype(o.dtype)
        i = pl.program_id(0)
        for a_ref, v in zip(a_refs, res[n_o:]):
            @pl.when(i == 0)
            def _(a_ref=a_ref):
                a_ref[...] = jnp.zeros_like(a_ref)
            a_ref[...] += v

    in_specs = [pl.BlockSpec((tr, r.shape[1]), lambda i: (i, 0)) for r in rows]
    in_specs += [pl.BlockSpec(b.shape, lambda i: (0, 0)) for b in bcast]
    out_specs = [pl.BlockSpec((tr, w), lambda i: (i, 0)) for w, _ in outs]
    out_specs += [pl.BlockSpec(s, lambda i: (0, 0)) for s in accs]
    out_shape = [_sds((t, w), dt) for w, dt in outs] + [_sds(s, f32) for s in accs]
    res = _pc(body, name=name, grid=(t // tr,), in_specs=in_specs, out_specs=out_specs, out_shape=out_shape,
              sem=("arbitrary",))(*rows, *bcast)
    return res


def _rms(x, gain):
    return x * lax.rsqrt(jnp.mean(x * x, axis=-1, keepdims=True) + EPS) * gain


def _rms_bwd(x, gain, dy):
    _, vjp = jax.vjp(_rms, x, gain)
    return vjp(dy)


def _lower_bound(lbl):
    l0, l1 = lbl[0:1, :], lbl[1:2, :]
    mx = jnp.maximum(l0, l1)
    e0, e1 = jnp.exp(l0 - mx), jnp.exp(l1 - mx)
    return e0 / (e0 + e1)


def _gates(qpre, fpre, lbl):
    lb = _lower_bound(lbl)
    q = jax.nn.silu(qpre)
    forget = lb + (1.0 - lb) * jax.nn.sigmoid(fpre)
    return q, 1.0 - forget, jnp.log(forget)


def _head_norm_gate(o, gpre, gn):
    return _rms(o, gn) * jax.nn.silu(gpre)


def _swap_halves(x):
    w = x.shape[1]
    lane = lax.broadcasted_iota(jnp.int32, x.shape, 1)
    return jnp.where((lane % MLA_ROPE) < MLA_ROPE // 2, pltpu.roll(x, w - MLA_ROPE // 2, 1),
                     pltpu.roll(x, MLA_ROPE // 2, 1))


def _tile_lanes(tab, w):
    return tab if w == tab.shape[1] else jnp.concatenate([tab] * (w // tab.shape[1]), axis=1)


def _rope(x, cos, sgn_sin, sign=1.0):
    w = x.shape[1]
    return x * _tile_lanes(cos, w) + sign * _swap_halves(x) * _tile_lanes(sgn_sin, w)


def _rope_heads(x, cos, sgn_sin, sign, scale):
    parts = []
    for h in range(x.shape[1] // (2 * LANES)):
        parts.append(x[:, 2 * LANES * h:2 * LANES * h + LANES] * scale)
        parts.append(_rope(x[:, 2 * LANES * h + LANES:2 * LANES * (h + 1)], cos, sgn_sin, sign) * scale)
    return jnp.concatenate(parts, axis=1)


def _bd(a, b, ca, cb):
    return lax.dot_general(a.astype(bf16), b.astype(bf16), (((ca,), (cb,)), ((), ())), preferred_element_type=f32)


@jax.custom_vjp
def _dot_nn(a, b):
    return _bd(a, b, 1, 0)


@jax.custom_vjp
def _dot_nt(a, b):
    return _bd(a, b, 1, 1)


@jax.custom_vjp
def _dot_tn(a, b):
    return _bd(a, b, 0, 0)


_dot_nn.defvjp(lambda a, b: (_bd(a, b, 1, 0), (a, b)), lambda r, g: (_bd(g, r[1], 1, 1), _bd(r[0], g, 0, 0)))
_dot_nt.defvjp(lambda a, b: (_bd(a, b, 1, 1), (a, b)), lambda r, g: (_bd(g, r[1], 1, 0), _bd(g, r[0], 0, 0)))
_dot_tn.defvjp(lambda a, b: (_bd(a, b, 0, 0), (a, b)), lambda r, g: (_bd(r[1], g, 1, 1), _bd(r[0], g, 1, 0)))


def _scan_rows(x, reverse):
    n = x.shape[0]
    row = lax.broadcasted_iota(jnp.int32, x.shape, 0)
    s = 1
    while s < n:
        if reverse:
            x = x + jnp.where(row < n - s, pltpu.roll(x, n - s, 0), 0.0)
        else:
            x = x + jnp.where(row >= s, pltpu.roll(x, s, 0), 0.0)
        s *= 2
    return x


@jax.custom_vjp
def _cumsum_rows(g):
    return _scan_rows(g, False)


_cumsum_rows.defvjp(lambda g: (_scan_rows(g, False), None), lambda _, ct: (_scan_rows(ct, True),))

HGRN_PAIRS = HGRN_HEADS // 2
HGRN_PAIR = 2 * HGRN_DK
GLA_STATE = (HGRN_PAIRS, HGRN_PAIR, HGRN_PAIR)


def _gla_consts():
    s = HGRN_SUB
    r = lax.broadcasted_iota(jnp.int32, (HGRN_PAIR, HGRN_PAIR), 0)
    c = lax.broadcasted_iota(jnp.int32, (HGRN_PAIR, HGRN_PAIR), 1)
    pair_mask = (r < HGRN_DK) == (c < HGRN_DK)
    masks = []
    for i in range(HGRN_CHUNK // s):
        n = s * (i + 1)
        row = lax.broadcasted_iota(jnp.int32, (HGRN_HEADS * s, HGRN_HEADS * n), 0)
        col = lax.broadcasted_iota(jnp.int32, (HGRN_HEADS * s, HGRN_HEADS * n), 1)
        col_head = sum((col >= m * n).astype(jnp.int32) for m in range(1, HGRN_HEADS))
        masks.append((col_head == row // s) & (col - col_head * n <= s * i + row % s))
    return pair_mask, masks


def _heads_to_rows(x):
    return jnp.concatenate([x[:, HGRN_DK * h:HGRN_DK * (h + 1)] for h in range(HGRN_HEADS)], axis=0)


def _gla_chunk(consts, dots, q, k, v, g, st):
    pair_mask, masks = consts
    dot_nn, dot_nt, dot_tn = dots
    c, s = HGRN_CHUNK, HGRN_SUB
    b = _cumsum_rows(g)
    b_last = b[c - 1:c, :]
    q_in, k_out = q * jnp.exp(b), k * jnp.exp(b_last - b)
    o_inter, st_new = [], []
    for p in range(HGRN_PAIRS):
        cols = slice(HGRN_PAIR * p, HGRN_PAIR * (p + 1))
        o_inter.append(dot_nt(q_in[:, cols], st[p]))
        st_new.append(st[p] * jnp.exp(b_last[:, cols]) + jnp.where(pair_mask, dot_tn(v[:, cols], k_out[:, cols]), 0.0))
    intra = []
    for i in range(c // s):
        n = s * (i + 1)
        ref = b[s * i - 1:s * i, :] if i else jnp.zeros_like(b_last)
        qt = _heads_to_rows(q[s * i:n] * jnp.exp(b[s * i:n] - ref))
        kt = _heads_to_rows(k[:n] * jnp.exp(jnp.minimum(ref - b[:n], EXP_CLAMP)))
        sc = jnp.where(masks[i], dot_nt(qt, kt), 0.0)
        oi = dot_nn(sc, _heads_to_rows(v[:n]))
        intra.append(jnp.concatenate([oi[s * h:s * (h + 1)] for h in range(HGRN_HEADS)], axis=1))
    return jnp.concatenate(o_inter, axis=1) + jnp.concatenate(intra, axis=0), st_new


_PLAIN_DOTS = (lambda a, b: _bd(a, b, 1, 0), lambda a, b: _bd(a, b, 1, 1), lambda a, b: _bd(a, b, 0, 0))
_VJP_DOTS = (_dot_nn, _dot_nt, _dot_tn)


def _hgrn_mix(consts, dots, qpre, fpre, v, gpre, lbl, gn, st):
    q, k, g = _gates(qpre, fpre, lbl)
    o, st_new = _gla_chunk(consts, dots, q, k, v, g, st)
    y = [_head_norm_gate(o[:, HGRN_DK * h:HGRN_DK * (h + 1)], gpre[:, HGRN_DK * h:HGRN_DK * (h + 1)], gn)
         for h in range(HGRN_HEADS)]
    return jnp.concatenate(y, axis=1), st_new


def _gla_fwd(p4, lbl, gn, gather=None):
    t = p4.shape[0]
    nc = t // HGRN_CHUNK

    def body(q_ref, k_ref, v_ref, g_ref, lbl_ref, gn_ref, *rest):
        if gather is None:
            o_ref, s_ref, st = rest
        else:
            wp_ref, _, o_ref, s_ref, gathered_ref, st, sems = rest

        @pl.when(pl.program_id(0) == 0)
        def _():
            st[...] = jnp.zeros_like(st)
            if gather is not None:
                _gather_start(wp_ref, gathered_ref, sems)

        if gather is not None:
            @pl.when(pl.program_id(0) == nc - 1)
            def _():
                _gather_finish(wp_ref, gathered_ref, sems)

        s_in = [st[p] for p in range(HGRN_PAIRS)]
        y, st_new = _hgrn_mix(_gla_consts(), _PLAIN_DOTS, q_ref[...], k_ref[...], v_ref[...], g_ref[...],
                              lbl_ref[...], gn_ref[...], s_in)
        o_ref[...] = y.astype(o_ref.dtype)
        for p in range(HGRN_PAIRS):
            s_ref[0, p] = s_in[p]
            st[p] = st_new[p]

    blk = lambda off: pl.BlockSpec((HGRN_CHUNK, D_MODEL), lambda c: (c, off))
    whole = lambda a: pl.BlockSpec(a.shape, lambda c: (0, 0))
    state_shape = GLA_STATE
    in_specs = [blk(0), blk(1), blk(2), blk(3), whole(lbl), whole(gn)]
    out_specs = [blk(0), pl.BlockSpec((1,) + state_shape, lambda c: (c, 0, 0, 0))]
    out_shape = [_sds((t, D_MODEL), bf16), _sds((nc,) + state_shape, f32)]
    scratch = [pltpu.VMEM(state_shape, f32)]
    if gather is None:
        return _pc(body, name="gla_fwd", grid=(nc,), in_specs=in_specs, out_specs=out_specs, out_shape=out_shape,
                   scratch=scratch, sem=("arbitrary",))(p4, p4, p4, p4, lbl, gn)
    return _pc(body, name="gla_fwd_gather", grid=(nc,), in_specs=in_specs + [HBM, HBM], out_specs=out_specs + [HBM],
               out_shape=out_shape + [_sds((4,) + gather.shape, bf16)], aliases={7: 2},
               scratch=scratch + [pltpu.SemaphoreType.DMA((2, 6))], sem=("arbitrary",))(
                   p4, p4, p4, p4, lbl, gn, gather, _gather_base(gather))


def _gla_bwd(p4, lbl, gn, states, dy, exchange=None):
    t = p4.shape[0]
    nc = t // HGRN_CHUNK

    def body(q_ref, k_ref, v_ref, g_ref, lbl_ref, gn_ref, s_ref, dy_ref, *rest):
        if exchange is None:
            dp_ref, dlbl_ref, dgn_ref, dst = rest
        else:
            sb_ref, dp_ref, dlbl_ref, dgn_ref, recv_ref, dst, sems = rest

        @pl.when(pl.program_id(0) == 0)
        def _():
            dst[...] = jnp.zeros_like(dst)
            dlbl_ref[...] = jnp.zeros_like(dlbl_ref)
            dgn_ref[...] = jnp.zeros_like(dgn_ref)
            if exchange is not None:
                _chips_start(sb_ref, recv_ref, sems)

        if exchange is not None:
            @pl.when(pl.program_id(0) == nc - 1)
            def _():
                _chips_finish(sb_ref, recv_ref, sems)

        consts = _gla_consts()
        fn = lambda *args: _hgrn_mix(consts, _VJP_DOTS, *args)
        pairs = range(HGRN_PAIRS)
        _, vjp = jax.vjp(fn, q_ref[...], k_ref[...], v_ref[...], g_ref[...], lbl_ref[...], gn_ref[...],
                         [s_ref[0, p] for p in pairs])
        *d_proj, dlbl, dgn, ds = vjp((dy_ref[...], [dst[p] for p in pairs]))
        for i, d in enumerate(d_proj):
            dp_ref[:, D_MODEL * i:D_MODEL * (i + 1)] = d.astype(dp_ref.dtype)
        dlbl_ref[...] += dlbl
        dgn_ref[...] += dgn
        for p in pairs:
            dst[p] = ds[p]

    blk = lambda off: pl.BlockSpec((HGRN_CHUNK, D_MODEL), lambda c: (nc - 1 - c, off))
    whole = lambda a: pl.BlockSpec(a.shape, lambda c: (0, 0))
    state_shape = GLA_STATE
    in_specs = [blk(0), blk(1), blk(2), blk(3), whole(lbl), whole(gn),
                pl.BlockSpec((1,) + state_shape, lambda c: (nc - 1 - c, 0, 0, 0)), blk(0)]
    out_specs = [pl.BlockSpec((HGRN_CHUNK, 4 * D_MODEL), lambda c: (nc - 1 - c, 0)), whole(lbl), whole(gn)]
    out_shape = [_sds((t, 4 * D_MODEL), bf16), _sds(lbl.shape, f32), _sds(gn.shape, f32)]
    scratch = [pltpu.VMEM(state_shape, f32)]
    if exchange is None:
        return _pc(body, name="gla_bwd", grid=(nc,), in_specs=in_specs, out_specs=out_specs, out_shape=out_shape,
                   scratch=scratch, sem=("arbitrary",))(p4, p4, p4, p4, lbl, gn, states, dy)
    return _pc(body, name="gla_bwd_exchange", grid=(nc,), in_specs=in_specs + [HBM], out_specs=out_specs + [HBM],
               out_shape=out_shape + [_sds((3,) + exchange.shape[1:], bf16)],
               scratch=scratch + [pltpu.SemaphoreType.DMA((2, 3))], sem=("arbitrary",))(
                   p4, p4, p4, p4, lbl, gn, states, dy, exchange)


ATT_FWD_TQ, ATT_FWD_TK = 1024, 1024
ATT_BWD_TQ, ATT_BWD_TK = 1024, 512
ATT_QK = 2 * LANES
NEG = -1e30


def _pair_masks(shape):
    lane = lax.broadcasted_iota(jnp.int32, shape, 1)
    return lane < MLA_ROPE, lane >= MLA_ROPE


def _causal(shape, row0, col0):
    row = row0 + lax.broadcasted_iota(jnp.int32, shape, 0)
    col = col0 + lax.broadcasted_iota(jnp.int32, shape, 1)
    return col <= row


def _qk_cols(e):
    return slice(ATT_QK * e, ATT_QK * (e + 1))


def _v_cols(e):
    return slice(MLA_V * e, MLA_V * (e + 1))


def _first_last_step(n0, n1):
    p, i = pl.program_id(0), pl.program_id(1)
    return (p == 0) & (i == 0), (p == n0 - 1) & (i == n1 - 1)


def _attn_fwd(qc, kc, v, gather=None):
    t = qc.shape[0]
    tq, tk = min(ATT_FWD_TQ, t), min(ATT_FWD_TK, t)
    assert tq == tk, "the diagonal block is split in the body on the premise of square blocks"
    npair = MLA_HEADS // 2

    def body(q_ref, k_ref, v_ref, *rest):
        if gather is None:
            o_ref, lse_ref = rest
        else:
            wp_ref, _, o_ref, lse_ref, gathered_ref, sems = rest
            first, last = _first_last_step(npair, t // tq)
            pl.when(first)(lambda: _gather_start(wp_ref, gathered_ref, sems))
            pl.when(last)(lambda: _gather_finish(wp_ref, gathered_ref, sems))
        i = pl.program_id(1)
        q = [q_ref[:, _qk_cols(e)] for e in range(2)]

        def update(state, q_rows, e, ks, ok):
            m, l, acc = state
            s = _bd(q_rows, k_ref[ks, _qk_cols(e)], 1, 1)
            if ok is not None:
                s = jnp.where(ok, s, NEG)
            m_new = jnp.maximum(m, jnp.max(s, axis=-1, keepdims=True))
            p = jnp.exp(s - m_new)
            alpha = jnp.exp(m - m_new)
            return m_new, alpha * l + jnp.sum(p, axis=-1, keepdims=True), alpha * acc + _bd(p, v_ref[ks, _v_cols(e)], 1, 0)

        def step(j, carry):
            ks = pl.ds(pl.multiple_of(j * tk, tk), tk)
            return tuple(update(carry[e], q[e], e, ks, None) for e in range(2))

        one = (jnp.full((tq, 1), NEG, f32), jnp.zeros((tq, 1), f32), jnp.zeros((tq, MLA_V), f32))
        carry = lax.fori_loop(0, i, step, (one, one))
        half = tq // 2
        outs, lses = [], []
        for e in range(2):
            top = update(tuple(a[:half] for a in carry[e]), q[e][:half], e,
                         pl.ds(pl.multiple_of(i * tk, tk), half), _causal((half, half), 0, 0))
            bottom = update(tuple(a[half:] for a in carry[e]), q[e][half:], e,
                            pl.ds(pl.multiple_of(i * tk, tk), tk), _causal((half, tk), half, 0))
            m, l, acc = (jnp.concatenate(ab, axis=0) for ab in zip(top, bottom))
            outs.append(acc / l)
            lses.append(m + jnp.log(l))
        o_ref[...] = jnp.concatenate(outs, axis=1).astype(o_ref.dtype)
        lo, _ = _pair_masks((tq, LANES))
        lse_ref[...] = jnp.where(lo, *lses)

    in_specs = [pl.BlockSpec((tq, 2 * ATT_QK), lambda p, i: (i, p)),
                pl.BlockSpec((t, 2 * ATT_QK), lambda p, i: (0, p)),
                pl.BlockSpec((t, 2 * MLA_V), lambda p, i: (0, p))]
    out_specs = [pl.BlockSpec((tq, 2 * MLA_V), lambda p, i: (i, p)), pl.BlockSpec((tq, LANES), lambda p, i: (i, p))]
    out_shape = [_sds((t, MLA_HEADS * MLA_V), bf16), _sds((t, npair * LANES), f32)]
    if gather is None:
        return _pc(body, name="attn_fwd", grid=(npair, t // tq), in_specs=in_specs, out_specs=out_specs,
                   out_shape=out_shape, sem=("arbitrary", "arbitrary"))(qc, kc, v)
    return _pc(body, name="attn_fwd_gather", grid=(npair, t // tq), in_specs=in_specs + [HBM, HBM],
               out_specs=out_specs + [HBM], out_shape=out_shape + [_sds((4,) + gather.shape, bf16)], aliases={4: 2},
               scratch=[pltpu.SemaphoreType.DMA((2, 6))], sem=("arbitrary", "arbitrary"))(
                   qc, kc, v, gather, _gather_base(gather))


def _attn_bwd(qc, kc, v, do, o, lse, exchange=None):
    t = qc.shape[0]
    tq, tk = min(ATT_BWD_TQ, t), min(ATT_BWD_TK, t)
    npair = MLA_HEADS // 2
    nq = t // tq
    sub = tq // tk
    assert sub * tk == tq

    def body(q_ref, do_ref, o_ref, lse_ref, k_ref, v_ref, *rest):
        if exchange is None:
            dq_ref, dk_ref, dv_ref, dl_ref = rest
        else:
            sb_ref, dq_ref, dk_ref, dv_ref, recv_ref, dl_ref, sems = rest
            first, last = _first_last_step(npair, t // tk)
            pl.when(first)(lambda: _chips_start(sb_ref, recv_ref, sems))
            pl.when(last)(lambda: _chips_finish(sb_ref, recv_ref, sems))
        j = pl.program_id(1)

        @pl.when(j == 0)
        def _():
            dq_ref[...] = jnp.zeros_like(dq_ref)
            lo, _ = _pair_masks((tq, LANES))

            def fill_delta(i, carry):
                rows = pl.ds(pl.multiple_of(i * tq, tq), tq)
                prod = do_ref[rows, :].astype(f32) * o_ref[rows, :].astype(f32)
                dl_ref[rows, :] = jnp.where(lo, *[jnp.sum(prod[:, _v_cols(e)], axis=-1, keepdims=True) for e in range(2)])
                return carry

            lax.fori_loop(0, nq, fill_delta, 0)

        k = [k_ref[:, _qk_cols(e)] for e in range(2)]
        vv = [v_ref[:, _v_cols(e)] for e in range(2)]

        def rows_step(carry, row0, rows, masked):
            qs = pl.ds(pl.multiple_of(row0, rows), rows)
            ok = _causal((rows, tk), row0, j * tk) if masked else None
            lse2, dl2 = lse_ref[qs, :], dl_ref[qs, :]
            new = []
            for e in range(2):
                dk, dv = carry[e]
                q_e, do_e = q_ref[qs, _qk_cols(e)], do_ref[qs, _v_cols(e)]
                p = jnp.exp(_bd(q_e, k[e], 1, 1) - lse2[:, MLA_ROPE * e:MLA_ROPE * e + 1])
                if masked:
                    p = jnp.where(ok, p, 0.0)
                dv = dv + _bd(p, do_e, 0, 0)
                dp = _bd(do_e, vv[e], 1, 1)
                ds = (p * (dp - dl2[:, MLA_ROPE * e:MLA_ROPE * e + 1])).astype(bf16)
                dk = dk + _bd(ds, q_e, 0, 0)
                dq_ref[qs, _qk_cols(e)] += _bd(ds, k[e], 1, 0)
                new.append((dk, dv))
            return tuple(new)

        one = (jnp.zeros((tk, ATT_QK), f32), jnp.zeros((tk, MLA_V), f32))
        i0 = (j * tk) // tq
        j_local = j - i0 * sub
        carry = (one, one)
        for r in range(sub):
            run = functools.partial(rows_step, row0=i0 * tq + r * tk, rows=tk, masked=True)
            carry = run(carry) if r == sub - 1 else lax.cond(r >= j_local, run, lambda c: c, carry)
        carry = lax.fori_loop(i0 + 1, nq, lambda i, c: rows_step(c, i * tq, tq, False), carry)
        for e in range(2):
            dk_ref[:, _qk_cols(e)] = carry[e][0].astype(dk_ref.dtype)
            dv_ref[:, _v_cols(e)] = carry[e][1].astype(dv_ref.dtype)

    res = lambda w: pl.BlockSpec((t, w), lambda p, j: (0, p))
    blk = lambda w: pl.BlockSpec((tk, w), lambda p, j: (j, p))
    in_specs = [res(2 * ATT_QK), res(2 * MLA_V), res(2 * MLA_V), res(LANES), blk(2 * ATT_QK), blk(2 * MLA_V)]
    out_specs = [res(2 * ATT_QK), blk(2 * ATT_QK), blk(2 * MLA_V)]
    out_shape = [_sds((t, MLA_HEADS * ATT_QK), f32), _sds((t, MLA_HEADS * ATT_QK), bf16), _sds((t, MLA_HEADS * MLA_V), bf16)]
    scratch = [pltpu.VMEM((t, LANES), f32)]
    if exchange is None:
        return _pc(body, name="attn_bwd", grid=(npair, t // tk), in_specs=in_specs, out_specs=out_specs,
                   out_shape=out_shape, scratch=scratch, sem=("arbitrary", "arbitrary"))(qc, do, o, lse, kc, v)
    return _pc(body, name="attn_bwd_exchange", grid=(npair, t // tk), in_specs=in_specs + [HBM],
               out_specs=out_specs + [HBM], out_shape=out_shape + [_sds((3,) + exchange.shape[1:], bf16)],
               scratch=scratch + [pltpu.SemaphoreType.DMA((2, 3))], sem=("arbitrary", "arbitrary"))(
                   qc, do, o, lse, kc, v, exchange)


def _rope_tables(t):
    half = MLA_ROPE // 2
    inv_freq = ROPE_THETA ** (-jnp.arange(half, dtype=f32) / half)
    ang = jnp.arange(t, dtype=f32)[:, None] * inv_freq[None, :]
    cos, sin = jnp.cos(ang), jnp.sin(ang)
    return jnp.concatenate([cos, cos] * 2, axis=1), jnp.concatenate([-sin, sin] * 2, axis=1)


def _relu2_epi(u):
    r = jnp.maximum(u, 0.0)
    return u, r * r


def _add_epi(r, res):
    return (r + res,)


def _drelu2_epi(da, u):
    return (da * 2.0 * jnp.maximum(u.astype(f32), 0.0),)


ROWWISE_EPI_TM = 512
SHALLOW_TM = 2048


def _residual_norms_epi(r, res, *gains):
    h = r + res
    return (h, *[_rms(h, g) for g in gains])


def _residual_out(a, w, h, gains, name):
    res = _mm(a, w, name=name, outs=(f32,) + (bf16,) * len(gains), epi=_residual_norms_epi, extras=(h, *gains),
              tm=ROWWISE_EPI_TM)
    return res if gains else [res]


def _dnorm_epi(dy, x, dres, gain):
    dx, dg = _rms_bwd(x, gain, dy)
    return dx + dres, dx + dres, dg


def _mlp_fwd(h, xm, w_up, w_down, tag, next_gains, gather=None):
    u, a, *gathered = _mm(xm, w_up, name=f"mlp{tag}_up", outs=(bf16, bf16), epi=_relu2_epi, gather=gather,
                          tm=SHALLOW_TM)
    h_out, *normed = _residual_out(a, w_down, h, next_gains, f"mlp{tag}_down")
    return h_out, normed, (xm, u, a), gathered


def _mlp_bwd(dh, dh16, h, gain, w_up, w_down, saved, tag):
    xm, u, a = saved
    du = _mm(dh16, w_down, tb=True, name=f"mlp{tag}_dact", outs=(bf16,), epi=_drelu2_epi, extras=(u,), tm=SHALLOW_TM)
    d_down = _wgrad(a, dh16, f"mlp{tag}_dwdown")
    d_up = _wgrad(xm, du, f"mlp{tag}_dwup")
    dh_in, dh_in16, d_gain = _mm(du, w_up, tb=True, name=f"mlp{tag}_dxm", outs=(f32, bf16), accs=1, epi=_dnorm_epi,
                                 extras=(h, dh, gain), tm=ROWWISE_EPI_TM)
    return dh_in, dh_in16, d_gain, d_up, d_down


def _local_step(x, target, w, comm=None):
    w = dict(w)
    t = x.shape[0]
    cos, sgn_sin = _rope_tables(t)
    grads = {}

    xn0 = _rw(lambda xx, g: (_rms(xx, g),), [x], [w["hgrn_norm"]], [(D_MODEL, bf16)], name="hgrn_norm")[0]
    p4 = _mm(xn0, w["hgrn_w4"], name="hgrn_proj", tn=2048)

    if comm is None:
        y, states = _gla_fwd(p4, w["hgrn_lb_logits"], w["hgrn_g_norm"])
    else:
        y, states, gathered = _gla_fwd(p4, w["hgrn_lb_logits"], w["hgrn_g_norm"], gather=comm.shard["gla"])
        w.update(comm.unpack["gla"](gathered))
    h1, xm0 = _residual_out(y, w["hgrn_w_o"], x, [w["mlp_norm"][0:1]], "hgrn_out")
    h2, (hk, xn1), mlp0, gathered = _mlp_fwd(h1, xm0, w["mlp_w_up", 0], w["mlp_w_down", 0], 0,
                                             [w["kv_in_norm"], w["mla_norm"]],
                                             gather=None if comm is None else comm.shard["mlp0_up"])
    if comm is not None:
        w.update(comm.unpack["mlp0_up"](gathered[0]))

    ckr = _mm(hk, w["kv_w_dkv"], name="kv_down")

    def ckv_fn(c, cs, sn, g):
        kr = _rope(c[:, MLA_KV_LORA:], cs, sn)
        return _rms(c[:, :MLA_KV_LORA], g), jnp.concatenate([jnp.zeros_like(kr), kr], axis=1)

    c_kv, kr_head = _rw(ckv_fn, [ckr, cos, sgn_sin], [w["kv_norm"]], [(MLA_KV_LORA, bf16), (ATT_QK, f32)],
                        name="kv_norm_rope")
    kc = _mm(c_kv, w["kv_w_kcat"], name="kv_up_k", outs=(bf16,), extras=(kr_head,),
             epi=lambda r, kr: (r + _tile_lanes(kr, r.shape[1]),))
    v_att = _mm(c_kv, w["kv_w_uv"], name="kv_up_v", outs=(bf16,))
    cq0, c_q = _mm(xn1, w["mla_w_dq"], name="q_down", outs=(f32, bf16), extras=(w["mla_q_norm"],),
                   epi=lambda r, g: (r, _rms(r, g)))
    qc = _mm(c_q, w["mla_w_qcat"], name="q_up", outs=(bf16,), extras=(cos, sgn_sin),
             epi=lambda r, cs, sn: (_rope_heads(r, cs, sn, 1.0, ATT_SCALE),))
    if comm is None:
        o_att, lse = _attn_fwd(qc, kc, v_att)
    else:
        o_att, lse, gathered = _attn_fwd(qc, kc, v_att, gather=comm.shard["attn"])
        w.update(comm.unpack["attn"](gathered))
    h3, xm1 = _residual_out(o_att, w["mla_w_o"], h2, [w["mlp_norm"][1:2]], "mla_out")
    u1, a1 = _mm(xm1, w["mlp_w_up", 1], name="mlp1_up", outs=(bf16, bf16), epi=_relu2_epi, tm=SHALLOW_TM)
    mlp1 = (xm1, u1, a1)

    def loss_epi(r, res, tgt, gain):
        def f(a, b):
            e = _rms(a, b) - tgt
            return 0.5 * jnp.sum(jnp.sum(e * e, axis=-1, keepdims=True) / D_MODEL, axis=0, keepdims=True)
        val, vjp = jax.vjp(f, r + res, gain)
        dh, dg = vjp(jnp.ones((1, 1), f32))
        return dh, dh, jnp.broadcast_to(val, (1, D_MODEL)), dg

    dh4, dh4_16, loss_acc, grads["final_norm"] = _mm(
        a1, w["mlp_w_down", 1], name="mlp1_down_loss", outs=(f32, bf16), accs=2, epi=loss_epi,
        extras=(h3, target, w["final_norm"]), tm=ROWWISE_EPI_TM)
    loss = loss_acc[0, 0]

    dh3, dh3_16, g_n1, grads["mlp_w_up", 1], grads["mlp_w_down", 1] = _mlp_bwd(
        dh4, dh4_16, h3, w["mlp_norm"][1:2], w["mlp_w_up", 1], w["mlp_w_down", 1], mlp1, 1)
    do_att = _mm(dh3_16, w["mla_w_o"], tb=True, name="mla_dout", outs=(bf16,))
    grads["mla_w_o"] = _wgrad(o_att, dh3_16, "mla_dwo")

    if comm is None:
        dqc, dkc, dv = _attn_bwd(qc, kc, v_att, do_att, o_att, lse)
    else:
        dqc, dkc, dv, comm.received["attn"] = _attn_bwd(qc, kc, v_att, do_att, o_att, lse,
                                                        exchange=comm.reduce(grads, "attn"))
    dqf = _rw(lambda a, cs, sn: (_rope_heads(a, cs, sn, -1.0, ATT_SCALE),), [dqc, cos, sgn_sin], [],
              [(MLA_HEADS * ATT_QK, bf16)], name="dq_rope")[0]
    dcq0, grads["mla_q_norm"] = _mm(dqf, w["mla_w_qcat"], tb=True, name="q_up_dx", outs=(bf16,), accs=1,
                                    extras=(cq0, w["mla_q_norm"]), epi=lambda dy, c, g: _rms_bwd(c, g, dy))
    grads["mla_w_qcat"] = _wgrad(c_q, dqf, "q_up_dw")
    dxn1 = _mm(dcq0, w["mla_w_dq"], tb=True, name="q_down_dx")
    grads["mla_w_dq"] = _wgrad(xn1, dcq0, "q_down_dw")

    dc_kv = _mm(dkc, w["kv_w_kcat"], tb=True, name="kv_up_dx_k")
    dc_kv = _mm(dv, w["kv_w_uv"], tb=True, name="kv_up_dx_v", epi=_add_epi, extras=(dc_kv,))
    grads["kv_w_kcat"] = _wgrad(c_kv, dkc, "kv_up_dw_k")
    grads["kv_w_uv"] = _wgrad(c_kv, dv, "kv_up_dw_v")

    def dckr_fn(c, dc, dk_heads, cs, sn, g):
        tot = dk_heads[:, LANES:ATT_QK].astype(f32)
        for h in range(1, MLA_HEADS):
            tot = tot + dk_heads[:, ATT_QK * h + LANES:ATT_QK * (h + 1)].astype(f32)
        lo, _ = _pair_masks(tot.shape)
        dkr = jnp.where(lo, _rope(tot, cs, sn, -1.0), 0.0)
        dcc, dg = _rms_bwd(c[:, :MLA_KV_LORA], g, dc)
        return jnp.concatenate([dcc, dkr], axis=1), dg

    dckr, grads["kv_norm"] = _rw(dckr_fn, [ckr, dc_kv, dkc, cos, sgn_sin], [w["kv_norm"]],
                                 [(MLA_KV_LORA + LANES, bf16)], [(1, MLA_KV_LORA)], name="kv_dnorm_rope")
    grads["kv_w_dkv"] = _wgrad(hk, dckr, "kv_down_dw")

    def dh2_epi(d1, hh, d2, dres, g1, g2):
        a, ga = _rms_bwd(hh, g1, d1)
        b, gb = _rms_bwd(hh, g2, d2)
        return a + b + dres, a + b + dres, ga, gb

    dh2, dh2_16, grads["kv_in_norm"], grads["mla_norm"] = _mm(
        dckr, w["kv_w_dkv"], tb=True, name="kv_down_dx", outs=(f32, bf16), accs=2, epi=dh2_epi,
        extras=(h2, dxn1, dh3, w["kv_in_norm"], w["mla_norm"]), tm=ROWWISE_EPI_TM)

    dh1, dh1_16, g_n0, grads["mlp_w_up", 0], grads["mlp_w_down", 0] = _mlp_bwd(
        dh2, dh2_16, h1, w["mlp_norm"][0:1], w["mlp_w_up", 0], w["mlp_w_down", 0], mlp0, 0)
    grads["mlp_norm"] = jnp.concatenate([g_n0, g_n1], axis=0)
    dy = _mm(dh1_16, w["hgrn_w_o"], tb=True, name="hgrn_dout")
    grads["hgrn_w_o"] = _wgrad(y, dh1_16, "hgrn_dwo")

    gla_args = (p4, w["hgrn_lb_logits"], w["hgrn_g_norm"], states, dy)
    if comm is None:
        dp4, grads["hgrn_lb_logits"], grads["hgrn_g_norm"] = _gla_bwd(*gla_args)
    else:
        dp4, grads["hgrn_lb_logits"], grads["hgrn_g_norm"], comm.received["gla"] = _gla_bwd(
            *gla_args, exchange=comm.reduce(grads, "gla"))
    grads["hgrn_w4"] = _mm(xn0, dp4, ta=True, name="hgrn_proj_dw")
    grad_x, grads["hgrn_norm"], *received = _mm(
        dp4, w["hgrn_w4"], tb=True, name="hgrn_proj_dx", outs=(f32,), accs=1, epi=lambda *args: _dnorm_epi(*args)[1:],
        extras=(x, dh1, w["hgrn_norm"]), tm=ROWWISE_EPI_TM, exchange=None if comm is None else comm.reduce(grads, "late"))
    if comm is not None:
        comm.received["late"] = received[0]
    return loss, grad_x, grads


HBM = pl.BlockSpec(memory_space=pltpu.HBM)


def _me():
    return lax.axis_index("x"), lax.axis_index("y"), lax.axis_index("c")


def _flip(x, y, f):
    return (1 - x if f & 1 else x), (1 - y if f & 2 else y)


def _rcopy(src, dst, sems, k, dev):
    return pltpu.make_async_remote_copy(src_ref=src, dst_ref=dst, send_sem=sems.at[0, k], recv_sem=sems.at[1, k],
                                        device_id=dev, device_id_type=MESH)


def _my_half(rows, c, mine=True):
    half = rows // 2
    return pl.ds(pl.multiple_of((c if mine else 1 - c) * half, 16), half)


def _gather_start(wp_ref, out_ref, sems):
    x, y, c = _me()
    half = _my_half(wp_ref.shape[0], c)
    for f in (1, 2, 3):
        px, py = _flip(x, y, f)
        _rcopy(wp_ref.at[half], out_ref.at[2 * x + y, half], sems, f - 1, (px, py, c)).start()


def _gather_finish(wp_ref, out_ref, sems):
    x, y, c = _me()
    half, other = _my_half(wp_ref.shape[0], c), _my_half(wp_ref.shape[0], c, mine=False)
    sends = []
    for f in (1, 2, 3):
        px, py = _flip(x, y, f)
        landed = out_ref.at[2 * px + py, half]
        _rcopy(landed, landed, sems, f - 1, (px, py, c)).wait_recv()
        sends.append(_rcopy(landed, landed, sems, 2 + f, (x, y, 1 - c)))
        sends[-1].start()
    for f in (1, 2, 3):
        px, py = _flip(x, y, f)
        theirs = out_ref.at[2 * px + py, other]
        _rcopy(theirs, theirs, sems, 2 + f, (x, y, 1 - c)).wait_recv()
        sends.append(_rcopy(wp_ref.at[half], out_ref.at[2 * x + y, half], sems, f - 1, (px, py, c)))
    for cp in sends:
        cp.wait_send()


def _gather_base(wp):
    return jnp.broadcast_to(wp[None], (4,) + wp.shape)


def _all_gather_weights(wp, sv):
    def body(wp_ref, sv_ref, base_ref, out_ref, svs_ref, sems, local_sem):
        x, y, c = _me()
        mine = pltpu.make_async_copy(sv_ref, svs_ref.at[2 * x + y], local_sem)
        mine.start()
        _gather_start(wp_ref, out_ref, sems)
        small = []
        for f in (1, 2, 3):
            px, py = _flip(x, y, f)
            small.append(_rcopy(sv_ref, svs_ref.at[2 * x + y], sems, 5 + f, (px, py, c)))
            small[-1].start()
        _gather_finish(wp_ref, out_ref, sems)
        for f in (1, 2, 3):
            px, py = _flip(x, y, f)
            _rcopy(sv_ref, svs_ref.at[2 * px + py], sems, 5 + f, (px, py, c)).wait_recv()
        for cp in small:
            cp.wait_send()
        mine.wait()

    return _pc(body, name="weights_all_gather", in_specs=[HBM, HBM, HBM], out_specs=[HBM, HBM],
               out_shape=[_sds((4,) + wp.shape, bf16), _sds((4, 8, 256), f32)], aliases={2: 0},
               scratch=[pltpu.SemaphoreType.DMA((2, 9)), pltpu.SemaphoreType.DMA])(wp, sv, _gather_base(wp))


def _send_half_to_sibling(gp, name):
    rows = gp.shape[1]

    def body(gp_ref, out_ref, sems):
        x, y, c = _me()
        cp = _rcopy(gp_ref.at[:, _my_half(rows, c, mine=False)], out_ref, sems, 0, (x, y, 1 - c))
        cp.start()
        cp.wait()

    return _pc(body, name=name, in_specs=[HBM], out_specs=HBM, out_shape=_sds((4, rows // 2, D_MODEL), gp.dtype),
               scratch=[pltpu.SemaphoreType.DMA((2, 1))])(gp)


def _chips_start(sb_ref, out_ref, sems):
    x, y, c = _me()
    for f in (1, 2, 3):
        px, py = _flip(x, y, f)
        _rcopy(sb_ref.at[2 * px + py], out_ref.at[f - 1], sems, f - 1, (px, py, c)).start()


def _chips_finish(sb_ref, out_ref, sems):
    x, y, c = _me()
    for f in (1, 2, 3):
        _rcopy(sb_ref.at[0], out_ref.at[f - 1], sems, f - 1, (x, y, c)).wait_recv()
    for f in (1, 2, 3):
        px, py = _flip(x, y, f)
        _rcopy(sb_ref.at[2 * px + py], out_ref.at[f - 1], sems, f - 1, (px, py, c)).wait_send()


def _final_exchange(small, tots):
    n = len(tots)

    def body(small_ref, *refs):
        tot_refs, smalls_ref, out_refs, sems, local_sem = refs[:n], refs[n], refs[n + 1:2 * n + 1], refs[-2], refs[-1]
        x, y, c = _me()
        me = 4 * x + 2 * y + c
        mine = pltpu.make_async_copy(small_ref, smalls_ref.at[me], local_sem)
        mine.start()
        sends = []
        for f in range(1, 8):
            px, py = _flip(x, y, f)
            pc = 1 - c if f & 4 else c
            sends.append(_rcopy(small_ref, smalls_ref.at[me], sems, f - 1, (px, py, pc)))
        for i in range(n):
            half = _my_half(tot_refs[i].shape[0], c)
            sends.append(_rcopy(tot_refs[i].at[half], out_refs[i].at[half], sems, 7 + i, (x, y, 1 - c)))
        for cp in sends:
            cp.start()
        for f in range(1, 8):
            px, py = _flip(x, y, f)
            pc = 1 - c if f & 4 else c
            _rcopy(small_ref, smalls_ref.at[4 * px + 2 * py + pc], sems, f - 1, (x, y, c)).wait_recv()
        for i in range(n):
            theirs = out_refs[i].at[_my_half(tot_refs[i].shape[0], c, mine=False)]
            _rcopy(theirs, theirs, sems, 7 + i, (x, y, 1 - c)).wait_recv()
        for cp in sends:
            cp.wait_send()
        mine.wait()

    return _pc(body, name="final_exchange", in_specs=[HBM] * (1 + n), out_specs=[HBM] * (1 + n),
               out_shape=[_sds((8, SMALL_ROWS, D_MODEL), f32)] + [_sds(t.shape, f32) for t in tots],
               aliases={1 + i: 1 + i for i in range(n)},
               scratch=[pltpu.SemaphoreType.DMA((2, 7 + n)), pltpu.SemaphoreType.DMA])(small, *tots)


def _sum_rows(half):
    return max(r for r in range(16, 513, 16) if half % r == 0)


def _sum_over_cores(gp, recv, cq, name):
    half = recv.shape[1]
    tr = _sum_rows(half)
    nb = half // tr

    def body(cq_ref, g_ref, r_ref, o32_ref, o16_ref):
        s = g_ref[...].astype(f32) + r_ref[...].astype(f32)
        o32_ref[...] = s
        o16_ref[...] = s.astype(bf16)

    spec = pl.BlockSpec((1, tr, D_MODEL), lambda b, i, cq_ref: (b, i, 0))
    gs = pltpu.PrefetchScalarGridSpec(
        num_scalar_prefetch=1, grid=(4, nb),
        in_specs=[pl.BlockSpec((1, tr, D_MODEL), lambda b, i, cq_ref: (b, cq_ref[0] * nb + i, 0)), spec],
        out_specs=[spec, spec])
    return _pc(body, name=name, grid_spec=gs, sem=("arbitrary", "arbitrary"),
               out_shape=[_sds((4, half, D_MODEL), f32), _sds((4, half, D_MODEL), bf16)])(cq, gp, recv)


def _sum_over_chips(s32, recv, cq, name):
    half = recv.shape[1]
    tr = _sum_rows(half)
    nb = half // tr

    def body(cq_ref, own_ref, r_ref, o_ref):
        o_ref[...] = ((own_ref[0] + r_ref[0].astype(f32)) + r_ref[1].astype(f32)) + r_ref[2].astype(f32)

    gs = pltpu.PrefetchScalarGridSpec(
        num_scalar_prefetch=1, grid=(nb,),
        in_specs=[pl.BlockSpec((1, tr, D_MODEL), lambda i, cq_ref: (cq_ref[1], i, 0)),
                  pl.BlockSpec((3, tr, D_MODEL), lambda i, cq_ref: (0, i, 0))],
        out_specs=pl.BlockSpec((tr, D_MODEL), lambda i, cq_ref: (cq_ref[0] * nb + i, 0)))
    return _pc(body, name=name, grid_spec=gs, sem=("arbitrary",),
               out_shape=_sds((2 * half, D_MODEL), f32))(cq, s32, recv)


def _sum_small(smalls):
    def body(s_ref, o_ref):
        tot = s_ref[0]
        for d in range(1, 8):
            tot = tot + s_ref[d]
        o_ref[...] = tot

    return _pc(body, name="small_sum", out_shape=_sds((SMALL_ROWS, D_MODEL), f32))(smalls)


def _adamw_math(w, g, m, v):
    m = ADAM_B1 * m + (1.0 - ADAM_B1) * g
    v = ADAM_B2 * v + (1.0 - ADAM_B2) * jnp.square(g)
    m_hat = m / (1.0 - ADAM_B1 ** ADAM_STEP)
    v_hat = v / (1.0 - ADAM_B2 ** ADAM_STEP)
    delta = -ADAM_LR * (m_hat / (jnp.sqrt(v_hat) + ADAM_EPS) + ADAM_WD * w)
    return delta, m, v


def _adamw(w, g, m, v, name):
    cols = w.shape[1]
    return _rw(_adamw_math, [w, g, m, v], [], [(cols, f32)] * 3, name=name, tr=256)


def _adamw_small(items, name):
    n = len(items)

    def body(*refs):
        ins, outs = refs[:4 * n], refs[4 * n:]
        for i in range(n):
            res = _adamw_math(*[r[...] for r in ins[4 * i:4 * i + 4]])
            for o, val in zip(outs[3 * i:3 * i + 3], res):
                o[...] = val

    flat = [a for it in items for a in it]
    out_shape = [_sds(it[0].shape, f32) for it in items for _ in range(3)]
    res = _pc(body, name=name, out_shape=out_shape)(*flat)
    return [tuple(res[3 * i:3 * i + 3]) for i in range(n)]


def _pack_shards(sh, layout, pad):
    parts = [(sh[n] if layer is None else sh[n][layer]).reshape(-1, D_MODEL).astype(bf16) for n, layer, _ in layout]
    if pad:
        parts.append(jnp.zeros((pad, D_MODEL), bf16))
    return jnp.concatenate(parts, axis=0)


def _mlp_full(g4, off, layer):
    o, r = off["mlp_w_up", layer]
    up = g4[:, o:o + r].transpose(1, 0, 2).reshape(D_MODEL, D_FF)
    o, r = off["mlp_w_down", layer]
    return {("mlp_w_up", layer): up, ("mlp_w_down", layer): g4[:, o:o + r].reshape(D_FF, D_MODEL)}


def _unpack_early(g4):
    hg = g4[:, 0:1024].reshape(4, 4, 256, D_MODEL)
    o, r = W_EARLY_OFF["hgrn_w_o", None]
    return {"hgrn_w4": hg.transpose(0, 2, 1, 3).reshape(D_MODEL, 4 * D_MODEL),
            "hgrn_w_o": g4[:, o:o + r].reshape(D_MODEL, D_MODEL)}


def _unpack_gla(g4):
    return _mlp_full(g4, W_GLA_OFF, 0)


def _unpack_last(g4):
    return _mlp_full(g4, W_LAST_OFF, 1)


def _unpack_mid(g4):
    def rows(name):
        o, r = W_MID_OFF[name, None]
        return g4[:, o:o + r]

    w = {"mla_w_dq": rows("mla_w_dq").reshape(D_MODEL, MLA_Q_LORA)}
    uq = rows("mla_w_uq").reshape(4, MLA_Q_LORA, 768).transpose(1, 0, 2).reshape(MLA_Q_LORA, MLA_HEADS, MLA_NOPE + MLA_ROPE)
    w["mla_w_qcat"] = jnp.pad(uq, ((0, 0), (0, 0), (0, ATT_QK - MLA_NOPE - MLA_ROPE))).reshape(MLA_Q_LORA, MLA_HEADS * ATT_QK)
    w["mla_w_o"] = rows("mla_w_o").reshape(MLA_HEADS * MLA_V, D_MODEL)
    dkv = rows("kv_w_dkv").reshape(D_MODEL, MLA_KV_LORA + MLA_ROPE)
    w["kv_w_dkv"] = jnp.pad(dkv, ((0, 0), (0, LANES - MLA_ROPE)))
    uk = rows("kv_w_uk").reshape(4, MLA_KV_LORA, 512).transpose(1, 0, 2).reshape(MLA_KV_LORA, MLA_HEADS, MLA_NOPE)
    w["kv_w_kcat"] = jnp.pad(uk, ((0, 0), (0, 0), (0, ATT_QK - MLA_NOPE))).reshape(MLA_KV_LORA, MLA_HEADS * ATT_QK)
    w["kv_w_uv"] = rows("kv_w_uv").reshape(4, MLA_KV_LORA, 512).transpose(1, 0, 2).reshape(MLA_KV_LORA, MLA_HEADS * MLA_V)
    return w


def _pack_grads_late(g):
    return g["hgrn_w4"].reshape(4, 256, 4, D_MODEL).transpose(0, 2, 1, 3).reshape(4, G_LATE_ROWS, D_MODEL)


def _grad_rows(g, name, layer):
    if name in ("mlp_w_up", "mlp_w_down"):
        full = g[name, layer]
        return full.reshape(D_MODEL, 4, 1024).transpose(1, 0, 2) if name == "mlp_w_up" else full.reshape(4, 1024, D_MODEL)
    if name == "mla_w_uq":
        uq = g["mla_w_qcat"].reshape(MLA_Q_LORA, MLA_HEADS, ATT_QK)[:, :, :MLA_NOPE + MLA_ROPE]
        return uq.reshape(MLA_Q_LORA, 4, 768).transpose(1, 0, 2).reshape(4, 192, D_MODEL)
    if name == "kv_w_uk":
        uk = g["kv_w_kcat"].reshape(MLA_KV_LORA, MLA_HEADS, ATT_QK)[:, :, :MLA_NOPE]
        return uk.reshape(MLA_KV_LORA, 4, 512).transpose(1, 0, 2).reshape(4, 128, D_MODEL)
    if name == "kv_w_uv":
        return g[name].reshape(MLA_KV_LORA, 4, 512).transpose(1, 0, 2).reshape(4, 128, D_MODEL)
    if name == "kv_w_dkv":
        return g[name][:, :MLA_KV_LORA + MLA_ROPE].reshape(4, 80, D_MODEL)
    return g[name].reshape(4, -1, D_MODEL)


def _pack_grads(g, layout):
    parts = [_grad_rows(g, name, layer) for name, layer, _ in layout]
    if layout in PADDED:
        parts.append(jnp.zeros((4, PACK_PAD, D_MODEL), bf16))
    return jnp.concatenate(parts, axis=1)


LOSS_ROW = 11


def _pack_small(g, loss):
    rows = []
    for name, _, r, wd in SMALL:
        a = g[name].reshape(r, wd)
        rows.append(jnp.pad(a, ((0, 0), (0, D_MODEL - wd))) if wd < D_MODEL else a)
    assert sum(r for _, _, r, _ in SMALL) == LOSS_ROW
    rows.append(jnp.full((1, D_MODEL), loss, f32))
    rows.append(jnp.zeros((SMALL_ROWS - LOSS_ROW - 1, D_MODEL), f32))
    return jnp.concatenate(rows, axis=0)


def kernel(x, hgrn_norm, hgrn_w_q, hgrn_w_f, hgrn_w_i, hgrn_w_g, hgrn_g_norm, hgrn_w_o, hgrn_lb_logits, mla_norm, mla_w_dq, mla_q_norm, mla_w_uq, mla_w_o, kv_in_norm, kv_w_dkv, kv_norm, kv_w_uk, kv_w_uv, mlp_norm, mlp_w_up, mlp_w_down, final_norm, loss_target, m_hgrn_norm, m_hgrn_w_q, m_hgrn_w_f, m_hgrn_w_i, m_hgrn_w_g, m_hgrn_g_norm, m_hgrn_w_o, m_hgrn_lb_logits, m_mla_norm, m_mla_w_dq, m_mla_q_norm, m_mla_w_uq, m_mla_w_o, m_kv_in_norm, m_kv_w_dkv, m_kv_norm, m_kv_w_uk, m_kv_w_uv, m_mlp_norm, m_mlp_w_up, m_mlp_w_down, m_final_norm, v_hgrn_norm, v_hgrn_w_q, v_hgrn_w_f, v_hgrn_w_i, v_hgrn_w_g, v_hgrn_g_norm, v_hgrn_w_o, v_hgrn_lb_logits, v_mla_norm, v_mla_w_dq, v_mla_q_norm, v_mla_w_uq, v_mla_w_o, v_kv_in_norm, v_kv_w_dkv, v_kv_norm, v_kv_w_uk, v_kv_w_uv, v_mlp_norm, v_mlp_w_up, v_mlp_w_down, v_final_norm):
    given = dict(locals())
    wsh = {n: given[n] for n in WEIGHTS}
    msh = {n: given["m_" + n] for n in WEIGHTS}
    vsh = {n: given["v_" + n] for n in WEIGHTS}
    xi, yi, ci = _me()
    chip = 2 * xi + yi
    cq = jnp.stack([ci, chip]).astype(jnp.int32)

    small_w = {n: wsh[n].reshape(r, -1) for n, _, r, _ in SMALL}
    sv = jnp.concatenate([small_w["hgrn_norm"], small_w["hgrn_lb_logits"], jnp.zeros((5, 256), f32)], axis=0)
    g4, sv4 = _all_gather_weights(_pack_shards(wsh, W_EARLY, 0), sv)
    w = _unpack_early(g4)
    w["hgrn_norm"] = sv4[:, 0, :].reshape(1, D_MODEL)
    w["hgrn_lb_logits"] = sv4[:, 1:3, :].transpose(1, 0, 2).reshape(2, D_MODEL)
    for n in ("hgrn_g_norm", "mla_norm", "mla_q_norm", "kv_in_norm", "kv_norm", "mlp_norm", "final_norm"):
        w[n] = small_w[n]

    class Comm:
        shard = {"gla": _pack_shards(wsh, W_GLA, 0), "mlp0_up": _pack_shards(wsh, W_MID, PACK_PAD),
                 "attn": _pack_shards(wsh, W_LAST, 0)}
        unpack = {"gla": _unpack_gla, "mlp0_up": _unpack_mid, "attn": _unpack_last}
        layout = {"gla": G_GLA, "attn": G_ATTN}
        received, s32 = {}, {}

        @staticmethod
        def reduce(grads, part):
            gp = _pack_grads_late(grads) if part == "late" else _pack_grads(grads, Comm.layout[part])
            Comm.s32[part], s16 = _sum_over_cores(gp, _send_half_to_sibling(gp, "grads_to_sibling_" + part), cq,
                                                  "grads_sum_cores_" + part)
            return s16

    loss, grad_x, g = _local_step(x.reshape(-1, D_MODEL), loss_target.reshape(-1, D_MODEL), w, Comm)

    parts = ("late", "gla", "attn")
    halves = [_sum_over_chips(Comm.s32[p], Comm.received[p], cq, "grads_sum_chips_" + p) for p in parts]
    smalls, *totals = _final_exchange(_pack_small(g, loss), halves)
    total = dict(zip(parts, totals))
    small_tot = _sum_small(smalls)
    loss = small_tot[LOSS_ROW, 0]

    where = {}
    for part, offsets in (("late", G_LATE_OFF), ("gla", G_GLA_OFF), ("attn", G_ATTN_OFF)):
        for (n, layer), (o, r) in offsets.items():
            where.setdefault(n, []).append(total[part][o:o + r])
    grad, delta, new_m, new_v = {}, {}, {}, {}
    groups = {"hgrn": [], "mla_kv": []}
    for n, pieces in where.items():
        shp = wsh[n].shape
        two_d = (-1, shp[-1])
        grad[n] = (pieces[0] if len(pieces) == 1 else jnp.concatenate(pieces, axis=0)).reshape(shp)
        operands = (wsh[n].reshape(two_d), grad[n].reshape(two_d), msh[n].reshape(two_d), vsh[n].reshape(two_d))
        if n.startswith("mlp"):
            res = _adamw(*operands, "adamw_" + n)
            delta[n], new_m[n], new_v[n] = (a.reshape(shp) for a in res)
        else:
            groups["hgrn" if n.startswith("hgrn") else "mla_kv"].append((n, operands))
    for gname, members in groups.items():
        for (n, _), res in zip(members, _adamw_small([ops for _, ops in members], "adamw_" + gname)):
            delta[n], new_m[n], new_v[n] = (a.reshape(wsh[n].shape) for a in res)
    items = []
    for n, row, r, wd in SMALL:
        gs = small_tot[row:row + r, :wd]
        if n in ("hgrn_norm", "hgrn_lb_logits"):
            gs = lax.dynamic_slice(gs, (0, 256 * chip), (r, 256))
        grad[n] = gs.reshape(wsh[n].shape)
        items.append((small_w[n], gs, msh[n].reshape(gs.shape), vsh[n].reshape(gs.shape)))
    for (n, _, _, _), (d, m2, v2) in zip(SMALL, _adamw_small(items, "adamw_small")):
        shp = wsh[n].shape
        delta[n], new_m[n], new_v[n] = d.reshape(shp), m2.reshape(shp), v2.reshape(shp)

    return (loss, grad_x.reshape(x.shape), *[grad[n] for n in WEIGHTS], *[delta[n] for n in WEIGHTS],
            *[new_m[n] for n in WEIGHTS], *[new_v[n] for n in WEIGHTS])
```

```python
import functools

import jax
import jax.numpy as jnp
from jax import lax
from jax.experimental import pallas as pl
from jax.experimental.pallas import tpu as pltpu

f32, bf16 = jnp.float32, jnp.bfloat16
HI = lax.Precision.HIGHEST
MESH = pl.DeviceIdType.MESH

D_MODEL = 1024
D_FF = 4096
EPS = 1e-6
HGRN_HEADS, HGRN_DK, HGRN_CHUNK, HGRN_SUB = 8, 128, 64, 16
MLA_HEADS, MLA_NOPE, MLA_ROPE, MLA_V = 16, 128, 64, 128
MLA_Q_LORA, MLA_KV_LORA = 256, 256
ROPE_THETA = 10000.0
ATT_SCALE = (MLA_NOPE + MLA_ROPE) ** -0.5
EXP_CLAMP = 80.0

ADAM_LR, ADAM_B1, ADAM_B2, ADAM_EPS, ADAM_WD, ADAM_STEP = 0.001, 0.9, 0.999, 1e-08, 0.01, 10

V7X_VMEM_BYTES = 64 * 1024 * 1024
VMEM_LIMIT = V7X_VMEM_BYTES - 8 * 1024 * 1024
LANES = 128

PACK_PAD = 16
W_EARLY = (("hgrn_w_q", None, 256), ("hgrn_w_f", None, 256), ("hgrn_w_i", None, 256), ("hgrn_w_g", None, 256),
           ("hgrn_w_o", None, 256))
W_GLA = (("mlp_w_up", 0, 1024), ("mlp_w_down", 0, 1024))
W_MID = (("mla_w_dq", None, 64), ("mla_w_uq", None, 192), ("mla_w_o", None, 512), ("kv_w_dkv", None, 80),
         ("kv_w_uk", None, 128), ("kv_w_uv", None, 128))
W_LAST = (("mlp_w_up", 1, 1024), ("mlp_w_down", 1, 1024))
G_LATE = (("hgrn_w_q", None, 256), ("hgrn_w_f", None, 256), ("hgrn_w_i", None, 256), ("hgrn_w_g", None, 256))
G_ATTN = (("mla_w_o", None, 512), ("mlp_w_up", 1, 1024), ("mlp_w_down", 1, 1024))
G_GLA = (("hgrn_w_o", None, 256), ("mla_w_dq", None, 64), ("mla_w_uq", None, 192), ("kv_w_dkv", None, 80),
         ("kv_w_uk", None, 128), ("kv_w_uv", None, 128), ("mlp_w_up", 0, 1024), ("mlp_w_down", 0, 1024))
PADDED = (W_MID, G_GLA)


def _offsets(layout):
    out, o = {}, 0
    for name, layer, rows in layout:
        out[name, layer] = (o, rows)
        o += rows
    return out, o + (PACK_PAD if layout in PADDED else 0)


W_EARLY_OFF, W_EARLY_ROWS = _offsets(W_EARLY)
W_GLA_OFF, W_GLA_ROWS = _offsets(W_GLA)
W_MID_OFF, W_MID_ROWS = _offsets(W_MID)
W_LAST_OFF, W_LAST_ROWS = _offsets(W_LAST)
G_LATE_OFF, G_LATE_ROWS = _offsets(G_LATE)
G_ATTN_OFF, G_ATTN_ROWS = _offsets(G_ATTN)
G_GLA_OFF, G_GLA_ROWS = _offsets(G_GLA)
assert all(r % 32 == 0 for r in (W_EARLY_ROWS, W_MID_ROWS, W_LAST_ROWS, G_LATE_ROWS, G_ATTN_ROWS, G_GLA_ROWS))

WEIGHTS = ("hgrn_norm", "hgrn_w_q", "hgrn_w_f", "hgrn_w_i", "hgrn_w_g", "hgrn_g_norm", "hgrn_w_o", "hgrn_lb_logits",
           "mla_norm", "mla_w_dq", "mla_q_norm", "mla_w_uq", "mla_w_o", "kv_in_norm", "kv_w_dkv", "kv_norm", "kv_w_uk",
           "kv_w_uv", "mlp_norm", "mlp_w_up", "mlp_w_down", "final_norm")
SMALL = (("hgrn_norm", 0, 1, 1024), ("hgrn_lb_logits", 1, 2, 1024), ("hgrn_g_norm", 3, 1, 128),
         ("mla_norm", 4, 1, 1024), ("mla_q_norm", 5, 1, 256), ("kv_in_norm", 6, 1, 1024), ("kv_norm", 7, 1, 256),
         ("mlp_norm", 8, 2, 1024), ("final_norm", 10, 1, 1024))
SMALL_ROWS = 16


def _pc(body, *, name, out_shape, grid=None, in_specs=None, out_specs=None, scratch=(), sem=None, grid_spec=None,
        aliases=None):
    params = pltpu.CompilerParams(dimension_semantics=sem, vmem_limit_bytes=VMEM_LIMIT)
    if grid_spec is not None:
        return pl.pallas_call(body, name=name, out_shape=out_shape, grid_spec=grid_spec, compiler_params=params,
                              interpret=False)
    kw = {k: v for k, v in (("grid", grid), ("in_specs", in_specs), ("out_specs", out_specs),
                            ("input_output_aliases", aliases)) if v is not None}
    return pl.pallas_call(body, name=name, out_shape=out_shape, scratch_shapes=list(scratch), compiler_params=params,
                          interpret=False, **kw)


def _sds(shape, dtype):
    return jax.ShapeDtypeStruct(tuple(shape), dtype)


def _mm(a, b, *, name, ta=False, tb=False, outs=(f32,), epi=None, extras=(), accs=0, gather=None, exchange=None,
        tm=1024, tn=1024, tk=4096):
    m, k = (a.shape[1], a.shape[0]) if ta else a.shape
    n = b.shape[0] if tb else b.shape[1]
    tm, tn, tk = min(tm, m), min(tn, n), min(tk, k)
    assert m % tm == 0 and n % tn == 0 and k % tk == 0, (name, m, n, k)
    nk = k // tk
    assert accs == 0 or (tn == n and nk == 1), name
    a_spec = pl.BlockSpec((tk, tm), lambda i, j, kk: (kk, i)) if ta else pl.BlockSpec((tm, tk), lambda i, j, kk: (i, kk))
    b_spec = pl.BlockSpec((tn, tk), lambda i, j, kk: (j, kk)) if tb else pl.BlockSpec((tk, tn), lambda i, j, kk: (kk, j))

    def extra_spec(e):
        if e.shape == (m, n):
            return pl.BlockSpec((tm, tn), lambda i, j, kk: (i, j))
        if e.shape[0] == m:
            return pl.BlockSpec((tm, e.shape[1]), lambda i, j, kk: (i, 0))
        return pl.BlockSpec((e.shape[0], tn), lambda i, j, kk: (0, j))

    e_specs = [extra_spec(e) for e in extras]
    dn = (((0 if ta else 1,), (1 if tb else 0,)), ((), ()))
    n_e, n_o = len(extras), len(outs)

    def finish(r, e_refs, o_refs):
        res = epi(r, *[e[...] for e in e_refs]) if epi is not None else (r,)
        for o, v in zip(o_refs[:n_o], res[:n_o]):
            o[...] = v.astype(o.dtype)
        for o, v in zip(o_refs[n_o:], res[n_o:]):
            @pl.when(pl.program_id(0) == 0)
            def _(o=o):
                o[...] = jnp.zeros_like(o)
            o[...] += v

    grid = (m // tm, n // tn, nk)
    assert gather is None or exchange is None
    n_g = 2 if gather is not None else (1 if exchange is not None else 0)

    def body(*refs):
        a_ref, b_ref = refs[0], refs[1]
        e_refs = refs[2:2 + n_e]
        o_refs = refs[2 + n_e + n_g:2 + n_e + n_g + n_o + accs]
        if n_g:
            src_ref, dst_ref, sems = refs[2 + n_e], refs[2 + n_e + n_g + n_o + accs], refs[-1]
            start, fin = (_gather_start, _gather_finish) if gather is not None else (_chips_start, _chips_finish)
            pid = [pl.program_id(d) for d in range(3)]
            pl.when((pid[0] == 0) & (pid[1] == 0) & (pid[2] == 0))(lambda: start(src_ref, dst_ref, sems))
            pl.when((pid[0] == grid[0] - 1) & (pid[1] == grid[1] - 1) & (pid[2] == grid[2] - 1))(
                lambda: fin(src_ref, dst_ref, sems))
        prod = lax.dot_general(a_ref[...].astype(bf16), b_ref[...].astype(bf16), dn, preferred_element_type=f32)
        if nk == 1:
            finish(prod, e_refs, o_refs)
            return
        acc = refs[2 + n_e + n_g + n_o + accs + (1 if n_g else 0)]
        kk = pl.program_id(2)

        @pl.when(kk == 0)
        def _():
            acc[...] = jnp.zeros_like(acc)

        acc[...] += prod

        @pl.when(kk == nk - 1)
        def _():
            finish(acc[...], e_refs, o_refs)

    out_specs = ([pl.BlockSpec((tm, tn), lambda i, j, kk: (i, j)) for _ in outs] +
                 [pl.BlockSpec((1, n), lambda i, j, kk: (0, 0))] * accs)
    out_shape = [_sds((m, n), dt) for dt in outs] + [_sds((1, n), f32)] * accs
    scratch = [pltpu.VMEM((tm, tn), f32)] if nk > 1 else []
    if not n_g:
        out = _pc(body, name=name, grid=grid, in_specs=[a_spec, b_spec] + e_specs, out_specs=out_specs,
                  out_shape=out_shape, scratch=scratch,
                  sem=("arbitrary" if accs else "parallel", "parallel", "arbitrary"))(a, b, *extras)
    elif exchange is not None:
        out = _pc(body, name=name + "_exchange", grid=grid, in_specs=[a_spec, b_spec] + e_specs + [HBM],
                  out_specs=out_specs + [HBM], out_shape=out_shape + [_sds((3,) + exchange.shape[1:], bf16)],
                  scratch=scratch + [pltpu.SemaphoreType.DMA((2, 3))], sem=("arbitrary",) * 3)(a, b, *extras, exchange)
    else:
        out = _pc(body, name=name + "_gather", grid=grid, in_specs=[a_spec, b_spec] + e_specs + [HBM, HBM],
                  out_specs=out_specs + [HBM], out_shape=out_shape + [_sds((4,) + gather.shape, bf16)],
                  aliases={2 + n_e + 1: n_o + accs}, scratch=scratch + [pltpu.SemaphoreType.DMA((2, 6))],
                  sem=("arbitrary",) * 3)(a, b, *extras, gather, _gather_base(gather))
    return out[0] if len(out) == 1 else out


def _wgrad(a, b, name):
    return _mm(a, b, ta=True, name=name, outs=(bf16,))


def _rw(fn, rows, bcast, outs, accs=(), *, name, tr=256):
    t = rows[0].shape[0]
    tr = min(tr, t)
    assert t % tr == 0
    n_r, n_b, n_o, n_a = len(rows), len(bcast), len(outs), len(accs)

    def body(*refs):
        r_refs = refs[:n_r]
        b_refs = refs[n_r:n_r + n_b]
        o_refs = refs[n_r + n_b:n_r + n_b + n_o]
        a_refs = refs[n_r + n_b + n_o:]
        res = fn(*[r[...] for r in r_refs], *[b[...] for b in b_refs])
        for o, v in zip(o_refs, res[:n_o]):
            o[...] = v.astype(o.dtype)
        i = pl.program_id(0)
        for a_ref, v in zip(a_refs, res[n_o:]):
            @pl.when(i == 0)
            def _(a_ref=a_ref):
                a_ref[...] = jnp.zeros_like(a_ref)
            a_ref[...] += v

    in_specs = [pl.BlockSpec((tr, r.shape[1]), lambda i: (i, 0)) for r in rows]
    in_specs += [pl.BlockSpec(b.shape, lambda i: (0, 0)) for b in bcast]
    out_specs = [pl.BlockSpec((tr, w), lambda i: (i, 0)) for w, _ in outs]
    out_specs += [pl.BlockSpec(s, lambda i: (0, 0)) for s in accs]
    out_shape = [_sds((t, w), dt) for w, dt in outs] + [_sds(s, f32) for s in accs]
    res = _pc(body, name=name, grid=(t // tr,), in_specs=in_specs, out_specs=out_specs, out_shape=out_shape,
              sem=("arbitrary",))(*rows, *bcast)
    return res


def _rms(x, gain):
    return x * lax.rsqrt(jnp.mean(x * x, axis=-1, keepdims=True) + EPS) * gain


def _rms_bwd(x, gain, dy):
    _, vjp = jax.vjp(_rms, x, gain)
    return vjp(dy)


def _lower_bound(lbl):
    l0, l1 = lbl[0:1, :], lbl[1:2, :]
    mx = jnp.maximum(l0, l1)
    e0, e1 = jnp.exp(l0 - mx), jnp.exp(l1 - mx)
    return e0 / (e0 + e1)


def _gates(qpre, fpre, lbl):
    lb = _lower_bound(lbl)
    q = jax.nn.silu(qpre)
    forget = lb + (1.0 - lb) * jax.nn.sigmoid(fpre)
    return q, 1.0 - forget, jnp.log(forget)


def _head_norm_gate(o, gpre, gn):
    return _rms(o, gn) * jax.nn.silu(gpre)


def _swap_halves(x):
    w = x.shape[1]
    lane = lax.broadcasted_iota(jnp.int32, x.shape, 1)
    return jnp.where((lane % MLA_ROPE) < MLA_ROPE // 2, pltpu.roll(x, w - MLA_ROPE // 2, 1),
                     pltpu.roll(x, MLA_ROPE // 2, 1))


def _tile_lanes(tab, w):
    return tab if w == tab.shape[1] else jnp.concatenate([tab] * (w // tab.shape[1]), axis=1)


def _rope(x, cos, sgn_sin, sign=1.0):
    w = x.shape[1]
    return x * _tile_lanes(cos, w) + sign * _swap_halves(x) * _tile_lanes(sgn_sin, w)


def _rope_heads(x, cos, sgn_sin, sign, scale):
    parts = []
    for h in range(x.shape[1] // (2 * LANES)):
        parts.append(x[:, 2 * LANES * h:2 * LANES * h + LANES] * scale)
        parts.append(_rope(x[:, 2 * LANES * h + LANES:2 * LANES * (h + 1)], cos, sgn_sin, sign) * scale)
    return jnp.concatenate(parts, axis=1)


def _bd(a, b, ca, cb):
    return lax.dot_general(a.astype(bf16), b.astype(bf16), (((ca,), (cb,)), ((), ())), preferred_element_type=f32)


@jax.custom_vjp
def _dot_nn(a, b):
    return _bd(a, b, 1, 0)


@jax.custom_vjp
def _dot_nt(a, b):
    return _bd(a, b, 1, 1)


@jax.custom_vjp
def _dot_tn(a, b):
    return _bd(a, b, 0, 0)


_dot_nn.defvjp(lambda a, b: (_bd(a, b, 1, 0), (a, b)), lambda r, g: (_bd(g, r[1], 1, 1), _bd(r[0], g, 0, 0)))
_dot_nt.defvjp(lambda a, b: (_bd(a, b, 1, 1), (a, b)), lambda r, g: (_bd(g, r[1], 1, 0), _bd(g, r[0], 0, 0)))
_dot_tn.defvjp(lambda a, b: (_bd(a, b, 0, 0), (a, b)), lambda r, g: (_bd(r[1], g, 1, 1), _bd(r[0], g, 1, 0)))


def _scan_rows(x, reverse):
    n = x.shape[0]
    row = lax.broadcasted_iota(jnp.int32, x.shape, 0)
    s = 1
    while s < n:
        if reverse:
            x = x + jnp.where(row < n - s, pltpu.roll(x, n - s, 0), 0.0)
        else:
            x = x + jnp.where(row >= s, pltpu.roll(x, s, 0), 0.0)
        s *= 2
    return x


@jax.custom_vjp
def _cumsum_rows(g):
    return _scan_rows(g, False)


_cumsum_rows.defvjp(lambda g: (_scan_rows(g, False), None), lambda _, ct: (_scan_rows(ct, True),))

HGRN_PAIRS = HGRN_HEADS // 2
HGRN_PAIR = 2 * HGRN_DK
GLA_STATE = (HGRN_PAIRS, HGRN_PAIR, HGRN_PAIR)


def _gla_consts():
    s = HGRN_SUB
    r = lax.broadcasted_iota(jnp.int32, (HGRN_PAIR, HGRN_PAIR), 0)
    c = lax.broadcasted_iota(jnp.int32, (HGRN_PAIR, HGRN_PAIR), 1)
    pair_mask = (r < HGRN_DK) == (c < HGRN_DK)
    masks = []
    for i in range(HGRN_CHUNK // s):
        n = s * (i + 1)
        row = lax.broadcasted_iota(jnp.int32, (HGRN_HEADS * s, HGRN_HEADS * n), 0)
        col = lax.broadcasted_iota(jnp.int32, (HGRN_HEADS * s, HGRN_HEADS * n), 1)
        col_head = sum((col >= m * n).astype(jnp.int32) for m in range(1, HGRN_HEADS))
        masks.append((col_head == row // s) & (col - col_head * n <= s * i + row % s))
    return pair_mask, masks


def _heads_to_rows(x):
    return jnp.concatenate([x[:, HGRN_DK * h:HGRN_DK * (h + 1)] for h in range(HGRN_HEADS)], axis=0)


def _gla_chunk(consts, dots, q, k, v, g, st):
    pair_mask, masks = consts
    dot_nn, dot_nt, dot_tn = dots
    c, s = HGRN_CHUNK, HGRN_SUB
    b = _cumsum_rows(g)
    b_last = b[c - 1:c, :]
    q_in, k_out = q * jnp.exp(b), k * jnp.exp(b_last - b)
    o_inter, st_new = [], []
    for p in range(HGRN_PAIRS):
        cols = slice(HGRN_PAIR * p, HGRN_PAIR * (p + 1))
        o_inter.append(dot_nt(q_in[:, cols], st[p]))
        st_new.append(st[p] * jnp.exp(b_last[:, cols]) + jnp.where(pair_mask, dot_tn(v[:, cols], k_out[:, cols]), 0.0))
    intra = []
    for i in range(c // s):
        n = s * (i + 1)
        ref = b[s * i - 1:s * i, :] if i else jnp.zeros_like(b_last)
        qt = _heads_to_rows(q[s * i:n] * jnp.exp(b[s * i:n] - ref))
        kt = _heads_to_rows(k[:n] * jnp.exp(jnp.minimum(ref - b[:n], EXP_CLAMP)))
        sc = jnp.where(masks[i], dot_nt(qt, kt), 0.0)
        oi = dot_nn(sc, _heads_to_rows(v[:n]))
        intra.append(jnp.concatenate([oi[s * h:s * (h + 1)] for h in range(HGRN_HEADS)], axis=1))
    return jnp.concatenate(o_inter, axis=1) + jnp.concatenate(intra, axis=0), st_new


_PLAIN_DOTS = (lambda a, b: _bd(a, b, 1, 0), lambda a, b: _bd(a, b, 1, 1), lambda a, b: _bd(a, b, 0, 0))
_VJP_DOTS = (_dot_nn, _dot_nt, _dot_tn)


def _hgrn_mix(consts, dots, qpre, fpre, v, gpre, lbl, gn, st):
    q, k, g = _gates(qpre, fpre, lbl)
    o, st_new = _gla_chunk(consts, dots, q, k, v, g, st)
    y = [_head_norm_gate(o[:, HGRN_DK * h:HGRN_DK * (h + 1)], gpre[:, HGRN_DK * h:HGRN_DK * (h + 1)], gn)
         for h in range(HGRN_HEADS)]
    return jnp.concatenate(y, axis=1), st_new


def _gla_fwd(p4, lbl, gn, gather=None):
    t = p4.shape[0]
    nc = t // HGRN_CHUNK

    def body(q_ref, k_ref, v_ref, g_ref, lbl_ref, gn_ref, *rest):
        if gather is None:
            o_ref, s_ref, st = rest
        else:
            wp_ref, _, o_ref, s_ref, gathered_ref, st, sems = rest

        @pl.when(pl.program_id(0) == 0)
        def _():
            st[...] = jnp.zeros_like(st)
            if gather is not None:
                _gather_start(wp_ref, gathered_ref, sems)

        if gather is not None:
            @pl.when(pl.program_id(0) == nc - 1)
            def _():
                _gather_finish(wp_ref, gathered_ref, sems)

        s_in = [st[p] for p in range(HGRN_PAIRS)]
        y, st_new = _hgrn_mix(_gla_consts(), _PLAIN_DOTS, q_ref[...], k_ref[...], v_ref[...], g_ref[...],
                              lbl_ref[...], gn_ref[...], s_in)
        o_ref[...] = y.astype(o_ref.dtype)
        for p in range(HGRN_PAIRS):
            s_ref[0, p] = s_in[p]
            st[p] = st_new[p]

    blk = lambda off: pl.BlockSpec((HGRN_CHUNK, D_MODEL), lambda c: (c, off))
    whole = lambda a: pl.BlockSpec(a.shape, lambda c: (0, 0))
    state_shape = GLA_STATE
    in_specs = [blk(0), blk(1), blk(2), blk(3), whole(lbl), whole(gn)]
    out_specs = [blk(0), pl.BlockSpec((1,) + state_shape, lambda c: (c, 0, 0, 0))]
    out_shape = [_sds((t, D_MODEL), bf16), _sds((nc,) + state_shape, f32)]
    scratch = [pltpu.VMEM(state_shape, f32)]
    if gather is None:
        return _pc(body, name="gla_fwd", grid=(nc,), in_specs=in_specs, out_specs=out_specs, out_shape=out_shape,
                   scratch=scratch, sem=("arbitrary",))(p4, p4, p4, p4, lbl, gn)
    return _pc(body, name="gla_fwd_gather", grid=(nc,), in_specs=in_specs + [HBM, HBM], out_specs=out_specs + [HBM],
               out_shape=out_shape + [_sds((4,) + gather.shape, bf16)], aliases={7: 2},
               scratch=scratch + [pltpu.SemaphoreType.DMA((2, 6))], sem=("arbitrary",))(
                   p4, p4, p4, p4, lbl, gn, gather, _gather_base(gather))


def _gla_bwd(p4, lbl, gn, states, dy, exchange=None):
    t = p4.shape[0]
    nc = t // HGRN_CHUNK

    def body(q_ref, k_ref, v_ref, g_ref, lbl_ref, gn_ref, s_ref, dy_ref, *rest):
        if exchange is None:
            dp_ref, dlbl_ref, dgn_ref, dst = rest
        else:
            sb_ref, dp_ref, dlbl_ref, dgn_ref, recv_ref, dst, sems = rest

        @pl.when(pl.program_id(0) == 0)
        def _():
            dst[...] = jnp.zeros_like(dst)
            dlbl_ref[...] = jnp.zeros_like(dlbl_ref)
            dgn_ref[...] = jnp.zeros_like(dgn_ref)
            if exchange is not None:
                _chips_start(sb_ref, recv_ref, sems)

        if exchange is not None:
            @pl.when(pl.program_id(0) == nc - 1)
            def _():
                _chips_finish(sb_ref, recv_ref, sems)

        consts = _gla_consts()
        fn = lambda *args: _hgrn_mix(consts, _VJP_DOTS, *args)
        pairs = range(HGRN_PAIRS)
        _, vjp = jax.vjp(fn, q_ref[...], k_ref[...], v_ref[...], g_ref[...], lbl_ref[...], gn_ref[...],
                         [s_ref[0, p] for p in pairs])
        *d_proj, dlbl, dgn, ds = vjp((dy_ref[...], [dst[p] for p in pairs]))
        for i, d in enumerate(d_proj):
            dp_ref[:, D_MODEL * i:D_MODEL * (i + 1)] = d.astype(dp_ref.dtype)
        dlbl_ref[...] += dlbl
        dgn_ref[...] += dgn
        for p in pairs:
            dst[p] = ds[p]

    blk = lambda off: pl.BlockSpec((HGRN_CHUNK, D_MODEL), lambda c: (nc - 1 - c, off))
    whole = lambda a: pl.BlockSpec(a.shape, lambda c: (0, 0))
    state_shape = GLA_STATE
    in_specs = [blk(0), blk(1), blk(2), blk(3), whole(lbl), whole(gn),
                pl.BlockSpec((1,) + state_shape, lambda c: (nc - 1 - c, 0, 0, 0)), blk(0)]
    out_specs = [pl.BlockSpec((HGRN_CHUNK, 4 * D_MODEL), lambda c: (nc - 1 - c, 0)), whole(lbl), whole(gn)]
    out_shape = [_sds((t, 4 * D_MODEL), bf16), _sds(lbl.shape, f32), _sds(gn.shape, f32)]
    scratch = [pltpu.VMEM(state_shape, f32)]
    if exchange is None:
        return _pc(body, name="gla_bwd", grid=(nc,), in_specs=in_specs, out_specs=out_specs, out_shape=out_shape,
                   scratch=scratch, sem=("arbitrary",))(p4, p4, p4, p4, lbl, gn, states, dy)
    return _pc(body, name="gla_bwd_exchange", grid=(nc,), in_specs=in_specs + [HBM], out_specs=out_specs + [HBM],
               out_shape=out_shape + [_sds((3,) + exchange.shape[1:], bf16)],
               scratch=scratch + [pltpu.SemaphoreType.DMA((2, 3))], sem=("arbitrary",))(
                   p4, p4, p4, p4, lbl, gn, states, dy, exchange)


ATT_FWD_TQ, ATT_FWD_TK = 1024, 1024
ATT_BWD_TQ, ATT_BWD_TK = 1024, 512
ATT_QK = 2 * LANES
NEG = -1e30


def _pair_masks(shape):
    lane = lax.broadcasted_iota(jnp.int32, shape, 1)
    return lane < MLA_ROPE, lane >= MLA_ROPE


def _causal(shape, row0, col0):
    row = row0 + lax.broadcasted_iota(jnp.int32, shape, 0)
    col = col0 + lax.broadcasted_iota(jnp.int32, shape, 1)
    return col <= row


def _qk_cols(e):
    return slice(ATT_QK * e, ATT_QK * (e + 1))


def _v_cols(e):
    return slice(MLA_V * e, MLA_V * (e + 1))


def _first_last_step(n0, n1):
    p, i = pl.program_id(0), pl.program_id(1)
    return (p == 0) & (i == 0), (p == n0 - 1) & (i == n1 - 1)


def _attn_fwd(qc, kc, v, gather=None):
    t = qc.shape[0]
    tq, tk = min(ATT_FWD_TQ, t), min(ATT_FWD_TK, t)
    assert tq == tk, "the diagonal block is split in the body on the premise of square blocks"
    npair = MLA_HEADS // 2

    def body(q_ref, k_ref, v_ref, *rest):
        if gather is None:
            o_ref, lse_ref = rest
        else:
            wp_ref, _, o_ref, lse_ref, gathered_ref, sems = rest
            first, last = _first_last_step(npair, t // tq)
            pl.when(first)(lambda: _gather_start(wp_ref, gathered_ref, sems))
            pl.when(last)(lambda: _gather_finish(wp_ref, gathered_ref, sems))
        i = pl.program_id(1)
        q = [q_ref[:, _qk_cols(e)] for e in range(2)]

        def update(state, q_rows, e, ks, ok):
            m, l, acc = state
            s = _bd(q_rows, k_ref[ks, _qk_cols(e)], 1, 1)
            if ok is not None:
                s = jnp.where(ok, s, NEG)
            m_new = jnp.maximum(m, jnp.max(s, axis=-1, keepdims=True))
            p = jnp.exp(s - m_new)
            alpha = jnp.exp(m - m_new)
            return m_new, alpha * l + jnp.sum(p, axis=-1, keepdims=True), alpha * acc + _bd(p, v_ref[ks, _v_cols(e)], 1, 0)

        def step(j, carry):
            ks = pl.ds(pl.multiple_of(j * tk, tk), tk)
            return tuple(update(carry[e], q[e], e, ks, None) for e in range(2))

        one = (jnp.full((tq, 1), NEG, f32), jnp.zeros((tq, 1), f32), jnp.zeros((tq, MLA_V), f32))
        carry = lax.fori_loop(0, i, step, (one, one))
        half = tq // 2
        outs, lses = [], []
        for e in range(2):
            top = update(tuple(a[:half] for a in carry[e]), q[e][:half], e,
                         pl.ds(pl.multiple_of(i * tk, tk), half), _causal((half, half), 0, 0))
            bottom = update(tuple(a[half:] for a in carry[e]), q[e][half:], e,
                            pl.ds(pl.multiple_of(i * tk, tk), tk), _causal((half, tk), half, 0))
            m, l, acc = (jnp.concatenate(ab, axis=0) for ab in zip(top, bottom))
            outs.append(acc / l)
            lses.append(m + jnp.log(l))
        o_ref[...] = jnp.concatenate(outs, axis=1).astype(o_ref.dtype)
        lo, _ = _pair_masks((tq, LANES))
        lse_ref[...] = jnp.where(lo, *lses)

    in_specs = [pl.BlockSpec((tq, 2 * ATT_QK), lambda p, i: (i, p)),
                pl.BlockSpec((t, 2 * ATT_QK), lambda p, i: (0, p)),
                pl.BlockSpec((t, 2 * MLA_V), lambda p, i: (0, p))]
    out_specs = [pl.BlockSpec((tq, 2 * MLA_V), lambda p, i: (i, p)), pl.BlockSpec((tq, LANES), lambda p, i: (i, p))]
    out_shape = [_sds((t, MLA_HEADS * MLA_V), bf16), _sds((t, npair * LANES), f32)]
    if gather is None:
        return _pc(body, name="attn_fwd", grid=(npair, t // tq), in_specs=in_specs, out_specs=out_specs,
                   out_shape=out_shape, sem=("arbitrary", "arbitrary"))(qc, kc, v)
    return _pc(body, name="attn_fwd_gather", grid=(npair, t // tq), in_specs=in_specs + [HBM, HBM],
               out_specs=out_specs + [HBM], out_shape=out_shape + [_sds((4,) + gather.shape, bf16)], aliases={4: 2},
               scratch=[pltpu.SemaphoreType.DMA((2, 6))], sem=("arbitrary", "arbitrary"))(
                   qc, kc, v, gather, _gather_base(gather))


def _attn_bwd(qc, kc, v, do, o, lse, exchange=None):
    t = qc.shape[0]
    tq, tk = min(ATT_BWD_TQ, t), min(ATT_BWD_TK, t)
    npair = MLA_HEADS // 2
    nq = t // tq
    sub = tq // tk
    assert sub * tk == tq

    def body(q_ref, do_ref, o_ref, lse_ref, k_ref, v_ref, *rest):
        if exchange is None:
            dq_ref, dk_ref, dv_ref, dl_ref = rest
        else:
            sb_ref, dq_ref, dk_ref, dv_ref, recv_ref, dl_ref, sems = rest
            first, last = _first_last_step(npair, t // tk)
            pl.when(first)(lambda: _chips_start(sb_ref, recv_ref, sems))
            pl.when(last)(lambda: _chips_finish(sb_ref, recv_ref, sems))
        j = pl.program_id(1)

        @pl.when(j == 0)
        def _():
            dq_ref[...] = jnp.zeros_like(dq_ref)
            lo, _ = _pair_masks((tq, LANES))

            def fill_delta(i, carry):
                rows = pl.ds(pl.multiple_of(i * tq, tq), tq)
                prod = do_ref[rows, :].astype(f32) * o_ref[rows, :].astype(f32)
                dl_ref[rows, :] = jnp.where(lo, *[jnp.sum(prod[:, _v_cols(e)], axis=-1, keepdims=True) for e in range(2)])
                return carry

            lax.fori_loop(0, nq, fill_delta, 0)

        k = [k_ref[:, _qk_cols(e)] for e in range(2)]
        vv = [v_ref[:, _v_cols(e)] for e in range(2)]

        def rows_step(carry, row0, rows, masked):
            qs = pl.ds(pl.multiple_of(row0, rows), rows)
            ok = _causal((rows, tk), row0, j * tk) if masked else None
            lse2, dl2 = lse_ref[qs, :], dl_ref[qs, :]
            new = []
            for e in range(2):
                dk, dv = carry[e]
                q_e, do_e = q_ref[qs, _qk_cols(e)], do_ref[qs, _v_cols(e)]
                p = jnp.exp(_bd(q_e, k[e], 1, 1) - lse2[:, MLA_ROPE * e:MLA_ROPE * e + 1])
                if masked:
                    p = jnp.where(ok, p, 0.0)
                dv = dv + _bd(p, do_e, 0, 0)
                dp = _bd(do_e, vv[e], 1, 1)
                ds = (p * (dp - dl2[:, MLA_ROPE * e:MLA_ROPE * e + 1])).astype(bf16)
                dk = dk + _bd(ds, q_e, 0, 0)
                dq_ref[qs, _qk_cols(e)] += _bd(ds, k[e], 1, 0)
                new.append((dk, dv))
            return tuple(new)

        one = (jnp.zeros((tk, ATT_QK), f32), jnp.zeros((tk, MLA_V), f32))
        i0 = (j * tk) // tq
        j_local = j - i0 * sub
        carry = (one, one)
        for r in range(sub):
            run = functools.partial(rows_step, row0=i0 * tq + r * tk, rows=tk, masked=True)
            carry = run(carry) if r == sub - 1 else lax.cond(r >= j_local, run, lambda c: c, carry)
        carry = lax.fori_loop(i0 + 1, nq, lambda i, c: rows_step(c, i * tq, tq, False), carry)
        for e in range(2):
            dk_ref[:, _qk_cols(e)] = carry[e][0].astype(dk_ref.dtype)
            dv_ref[:, _v_cols(e)] = carry[e][1].astype(dv_ref.dtype)

    res = lambda w: pl.BlockSpec((t, w), lambda p, j: (0, p))
    blk = lambda w: pl.BlockSpec((tk, w), lambda p, j: (j, p))
    in_specs = [res(2 * ATT_QK), res(2 * MLA_V), res(2 * MLA_V), res(LANES), blk(2 * ATT_QK), blk(2 * MLA_V)]
    out_specs = [res(2 * ATT_QK), blk(2 * ATT_QK), blk(2 * MLA_V)]
    out_shape = [_sds((t, MLA_HEADS * ATT_QK), f32), _sds((t, MLA_HEADS * ATT_QK), bf16), _sds((t, MLA_HEADS * MLA_V), bf16)]
    scratch = [pltpu.VMEM((t, LANES), f32)]
    if exchange is None:
        return _pc(body, name="attn_bwd", grid=(npair, t // tk), in_specs=in_specs, out_specs=out_specs,
                   out_shape=out_shape, scratch=scratch, sem=("arbitrary", "arbitrary"))(qc, do, o, lse, kc, v)
    return _pc(body, name="attn_bwd_exchange", grid=(npair, t // tk), in_specs=in_specs + [HBM],
               out_specs=out_specs + [HBM], out_shape=out_shape + [_sds((3,) + exchange.shape[1:], bf16)],
               scratch=scratch + [pltpu.SemaphoreType.DMA((2, 3))], sem=("arbitrary", "arbitrary"))(
                   qc, do, o, lse, kc, v, exchange)


def _rope_tables(t):
    half = MLA_ROPE // 2
    inv_freq = ROPE_THETA ** (-jnp.arange(half, dtype=f32) / half)
    ang = jnp.arange(t, dtype=f32)[:, None] * inv_freq[None, :]
    cos, sin = jnp.cos(ang), jnp.sin(ang)
    return jnp.concatenate([cos, cos] * 2, axis=1), jnp.concatenate([-sin, sin] * 2, axis=1)


def _relu2_epi(u):
    r = jnp.maximum(u, 0.0)
    return u, r * r


def _add_epi(r, res):
    return (r + res,)


def _drelu2_epi(da, u):
    return (da * 2.0 * jnp.maximum(u.astype(f32), 0.0),)


ROWWISE_EPI_TM = 512
SHALLOW_TM = 2048


def _residual_norms_epi(r, res, *gains):
    h = r + res
    return (h, *[_rms(h, g) for g in gains])


def _residual_out(a, w, h, gains, name):
    res = _mm(a, w, name=name, outs=(f32,) + (bf16,) * len(gains), epi=_residual_norms_epi, extras=(h, *gains),
              tm=ROWWISE_EPI_TM)
    return res if gains else [res]


def _dnorm_epi(dy, x, dres, gain):
    dx, dg = _rms_bwd(x, gain, dy)
    return dx + dres, dx + dres, dg


def _mlp_fwd(h, xm, w_up, w_down, tag, next_gains, gather=None):
    u, a, *gathered = _mm(xm, w_up, name=f"mlp{tag}_up", outs=(bf16, bf16), epi=_relu2_epi, gather=gather,
                          tm=SHALLOW_TM)
    h_out, *normed = _residual_out(a, w_down, h, next_gains, f"mlp{tag}_down")
    return h_out, normed, (xm, u, a), gathered


def _mlp_bwd(dh, dh16, h, gain, w_up, w_down, saved, tag):
    xm, u, a = saved
    du = _mm(dh16, w_down, tb=True, name=f"mlp{tag}_dact", outs=(bf16,), epi=_drelu2_epi, extras=(u,), tm=SHALLOW_TM)
    d_down = _wgrad(a, dh16, f"mlp{tag}_dwdown")
    d_up = _wgrad(xm, du, f"mlp{tag}_dwup")
    dh_in, dh_in16, d_gain = _mm(du, w_up, tb=True, name=f"mlp{tag}_dxm", outs=(f32, bf16), accs=1, epi=_dnorm_epi,
                                 extras=(h, dh, gain), tm=ROWWISE_EPI_TM)
    return dh_in, dh_in16, d_gain, d_up, d_down


def _local_step(x, target, w, comm=None):
    w = dict(w)
    t = x.shape[0]
    cos, sgn_sin = _rope_tables(t)
    grads = {}

    xn0 = _rw(lambda xx, g: (_rms(xx, g),), [x], [w["hgrn_norm"]], [(D_MODEL, bf16)], name="hgrn_norm")[0]
    p4 = _mm(xn0, w["hgrn_w4"], name="hgrn_proj", tn=2048)

    if comm is None:
        y, states = _gla_fwd(p4, w["hgrn_lb_logits"], w["hgrn_g_norm"])
    else:
        y, states, gathered = _gla_fwd(p4, w["hgrn_lb_logits"], w["hgrn_g_norm"], gather=comm.shard["gla"])
        w.update(comm.unpack["gla"](gathered))
    h1, xm0 = _residual_out(y, w["hgrn_w_o"], x, [w["mlp_norm"][0:1]], "hgrn_out")
    h2, (hk, xn1), mlp0, gathered = _mlp_fwd(h1, xm0, w["mlp_w_up", 0], w["mlp_w_down", 0], 0,
                                             [w["kv_in_norm"], w["mla_norm"]],
                                             gather=None if comm is None else comm.shard["mlp0_up"])
    if comm is not None:
        w.update(comm.unpack["mlp0_up"](gathered[0]))

    ckr = _mm(hk, w["kv_w_dkv"], name="kv_down")

    def ckv_fn(c, cs, sn, g):
        kr = _rope(c[:, MLA_KV_LORA:], cs, sn)
        return _rms(c[:, :MLA_KV_LORA], g), jnp.concatenate([jnp.zeros_like(kr), kr], axis=1)

    c_kv, kr_head = _rw(ckv_fn, [ckr, cos, sgn_sin], [w["kv_norm"]], [(MLA_KV_LORA, bf16), (ATT_QK, f32)],
                        name="kv_norm_rope")
    kc = _mm(c_kv, w["kv_w_kcat"], name="kv_up_k", outs=(bf16,), extras=(kr_head,),
             epi=lambda r, kr: (r + _tile_lanes(kr, r.shape[1]),))
    v_att = _mm(c_kv, w["kv_w_uv"], name="kv_up_v", outs=(bf16,))
    cq0, c_q = _mm(xn1, w["mla_w_dq"], name="q_down", outs=(f32, bf16), extras=(w["mla_q_norm"],),
                   epi=lambda r, g: (r, _rms(r, g)))
    qc = _mm(c_q, w["mla_w_qcat"], name="q_up", outs=(bf16,), extras=(cos, sgn_sin),
             epi=lambda r, cs, sn: (_rope_heads(r, cs, sn, 1.0, ATT_SCALE),))
    if comm is None:
        o_att, lse = _attn_fwd(qc, kc, v_att)
    else:
        o_att, lse, gathered = _attn_fwd(qc, kc, v_att, gather=comm.shard["attn"])
        w.update(comm.unpack["attn"](gathered))
    h3, xm1 = _residual_out(o_att, w["mla_w_o"], h2, [w["mlp_norm"][1:2]], "mla_out")
    u1, a1 = _mm(xm1, w["mlp_w_up", 1], name="mlp1_up", outs=(bf16, bf16), epi=_relu2_epi, tm=SHALLOW_TM)
    mlp1 = (xm1, u1, a1)

    def loss_epi(r, res, tgt, gain):
        def f(a, b):
            e = _rms(a, b) - tgt
            return 0.5 * jnp.sum(jnp.sum(e * e, axis=-1, keepdims=True) / D_MODEL, axis=0, keepdims=True)
        val, vjp = jax.vjp(f, r + res, gain)
        dh, dg = vjp(jnp.ones((1, 1), f32))
        return dh, dh, jnp.broadcast_to(val, (1, D_MODEL)), dg

    dh4, dh4_16, loss_acc, grads["final_norm"] = _mm(
        a1, w["mlp_w_down", 1], name="mlp1_down_loss", outs=(f32, bf16), accs=2, epi=loss_epi,
        extras=(h3, target, w["final_norm"]), tm=ROWWISE_EPI_TM)
    loss = loss_acc[0, 0]

    dh3, dh3_16, g_n1, grads["mlp_w_up", 1], grads["mlp_w_down", 1] = _mlp_bwd(
        dh4, dh4_16, h3, w["mlp_norm"][1:2], w["mlp_w_up", 1], w["mlp_w_down", 1], mlp1, 1)
    do_att = _mm(dh3_16, w["mla_w_o"], tb=True, name="mla_dout", outs=(bf16,))
    grads["mla_w_o"] = _wgrad(o_att, dh3_16, "mla_dwo")

    if comm is None:
        dqc, dkc, dv = _attn_bwd(qc, kc, v_att, do_att, o_att, lse)
    else:
        dqc, dkc, dv, comm.received["attn"] = _attn_bwd(qc, kc, v_att, do_att, o_att, lse,
                                                        exchange=comm.reduce(grads, "attn"))
    dqf = _rw(lambda a, cs, sn: (_rope_heads(a, cs, sn, -1.0, ATT_SCALE),), [dqc, cos, sgn_sin], [],
              [(MLA_HEADS * ATT_QK, bf16)], name="dq_rope")[0]
    dcq0, grads["mla_q_norm"] = _mm(dqf, w["mla_w_qcat"], tb=True, name="q_up_dx", outs=(bf16,), accs=1,
                                    extras=(cq0, w["mla_q_norm"]), epi=lambda dy, c, g: _rms_bwd(c, g, dy))
    grads["mla_w_qcat"] = _wgrad(c_q, dqf, "q_up_dw")
    dxn1 = _mm(dcq0, w["mla_w_dq"], tb=True, name="q_down_dx")
    grads["mla_w_dq"] = _wgrad(xn1, dcq0, "q_down_dw")

    dc_kv = _mm(dkc, w["kv_w_kcat"], tb=True, name="kv_up_dx_k")
    dc_kv = _mm(dv, w["kv_w_uv"], tb=True, name="kv_up_dx_v", epi=_add_epi, extras=(dc_kv,))
    grads["kv_w_kcat"] = _wgrad(c_kv, dkc, "kv_up_dw_k")
    grads["kv_w_uv"] = _wgrad(c_kv, dv, "kv_up_dw_v")

    def dckr_fn(c, dc, dk_heads, cs, sn, g):
        tot = dk_heads[:, LANES:ATT_QK].astype(f32)
        for h in range(1, MLA_HEADS):
            tot = tot + dk_heads[:, ATT_QK * h + LANES:ATT_QK * (h + 1)].astype(f32)
        lo, _ = _pair_masks(tot.shape)
        dkr = jnp.where(lo, _rope(tot, cs, sn, -1.0), 0.0)
        dcc, dg = _rms_bwd(c[:, :MLA_KV_LORA], g, dc)
        return jnp.concatenate([dcc, dkr], axis=1), dg

    dckr, grads["kv_norm"] = _rw(dckr_fn, [ckr, dc_kv, dkc, cos, sgn_sin], [w["kv_norm"]],
                                 [(MLA_KV_LORA + LANES, bf16)], [(1, MLA_KV_LORA)], name="kv_dnorm_rope")
    grads["kv_w_dkv"] = _wgrad(hk, dckr, "kv_down_dw")

    def dh2_epi(d1, hh, d2, dres, g1, g2):
        a, ga = _rms_bwd(hh, g1, d1)
        b, gb = _rms_bwd(hh, g2, d2)
        return a + b + dres, a + b + dres, ga, gb

    dh2, dh2_16, grads["kv_in_norm"], grads["mla_norm"] = _mm(
        dckr, w["kv_w_dkv"], tb=True, name="kv_down_dx", outs=(f32, bf16), accs=2, epi=dh2_epi,
        extras=(h2, dxn1, dh3, w["kv_in_norm"], w["mla_norm"]), tm=ROWWISE_EPI_TM)

    dh1, dh1_16, g_n0, grads["mlp_w_up", 0], grads["mlp_w_down", 0] = _mlp_bwd(
        dh2, dh2_16, h1, w["mlp_norm"][0:1], w["mlp_w_up", 0], w["mlp_w_down", 0], mlp0, 0)
    grads["mlp_norm"] = jnp.concatenate([g_n0, g_n1], axis=0)
    dy = _mm(dh1_16, w["hgrn_w_o"], tb=True, name="hgrn_dout")
    grads["hgrn_w_o"] = _wgrad(y, dh1_16, "hgrn_dwo")

    gla_args = (p4, w["hgrn_lb_logits"], w["hgrn_g_norm"], states, dy)
    if comm is None:
        dp4, grads["hgrn_lb_logits"], grads["hgrn_g_norm"] = _gla_bwd(*gla_args)
    else:
        dp4, grads["hgrn_lb_logits"], grads["hgrn_g_norm"], comm.received["gla"] = _gla_bwd(
            *gla_args, exchange=comm.reduce(grads, "gla"))
    grads["hgrn_w4"] = _mm(xn0, dp4, ta=True, name="hgrn_proj_dw")
    grad_x, grads["hgrn_norm"], *received = _mm(
        dp4, w["hgrn_w4"], tb=True, name="hgrn_proj_dx", outs=(f32,), accs=1, epi=lambda *args: _dnorm_epi(*args)[1:],
        extras=(x, dh1, w["hgrn_norm"]), tm=ROWWISE_EPI_TM, exchange=None if comm is None else comm.reduce(grads, "late"))
    if comm is not None:
        comm.received["late"] = received[0]
    return loss, grad_x, grads


HBM = pl.BlockSpec(memory_space=pltpu.HBM)


def _me():
    return lax.axis_index("x"), lax.axis_index("y"), lax.axis_index("c")


def _flip(x, y, f):
    return (1 - x if f & 1 else x), (1 - y if f & 2 else y)


def _rcopy(src, dst, sems, k, dev):
    return pltpu.make_async_remote_copy(src_ref=src, dst_ref=dst, send_sem=sems.at[0, k], recv_sem=sems.at[1, k],
                                        device_id=dev, device_id_type=MESH)


def _my_half(rows, c, mine=True):
    half = rows // 2
    return pl.ds(pl.multiple_of((c if mine else 1 - c) * half, 16), half)


def _gather_start(wp_ref, out_ref, sems):
    x, y, c = _me()
    half = _my_half(wp_ref.shape[0], c)
    for f in (1, 2, 3):
        px, py = _flip(x, y, f)
        _rcopy(wp_ref.at[half], out_ref.at[2 * x + y, half], sems, f - 1, (px, py, c)).start()


def _gather_finish(wp_ref, out_ref, sems):
    x, y, c = _me()
    half, other = _my_half(wp_ref.shape[0], c), _my_half(wp_ref.shape[0], c, mine=False)
    sends = []
    for f in (1, 2, 3):
        px, py = _flip(x, y, f)
        landed = out_ref.at[2 * px + py, half]
        _rcopy(landed, landed, sems, f - 1, (px, py, c)).wait_recv()
        sends.append(_rcopy(landed, landed, sems, 2 + f, (x, y, 1 - c)))
        sends[-1].start()
    for f in (1, 2, 3):
        px, py = _flip(x, y, f)
        theirs = out_ref.at[2 * px + py, other]
        _rcopy(theirs, theirs, sems, 2 + f, (x, y, 1 - c)).wait_recv()
        sends.append(_rcopy(wp_ref.at[half], out_ref.at[2 * x + y, half], sems, f - 1, (px, py, c)))
    for cp in sends:
        cp.wait_send()


def _gather_base(wp):
    return jnp.broadcast_to(wp[None], (4,) + wp.shape)


def _all_gather_weights(wp, sv):
    def body(wp_ref, sv_ref, base_ref, out_ref, svs_ref, sems, local_sem):
        x, y, c = _me()
        mine = pltpu.make_async_copy(sv_ref, svs_ref.at[2 * x + y], local_sem)
        mine.start()
        _gather_start(wp_ref, out_ref, sems)
        small = []
        for f in (1, 2, 3):
            px, py = _flip(x, y, f)
            small.append(_rcopy(sv_ref, svs_ref.at[2 * x + y], sems, 5 + f, (px, py, c)))
            small[-1].start()
        _gather_finish(wp_ref, out_ref, sems)
        for f in (1, 2, 3):
            px, py = _flip(x, y, f)
            _rcopy(sv_ref, svs_ref.at[2 * px + py], sems, 5 + f, (px, py, c)).wait_recv()
        for cp in small:
            cp.wait_send()
        mine.wait()

    return _pc(body, name="weights_all_gather", in_specs=[HBM, HBM, HBM], out_specs=[HBM, HBM],
               out_shape=[_sds((4,) + wp.shape, bf16), _sds((4, 8, 256), f32)], aliases={2: 0},
               scratch=[pltpu.SemaphoreType.DMA((2, 9)), pltpu.SemaphoreType.DMA])(wp, sv, _gather_base(wp))


def _send_half_to_sibling(gp, name):
    rows = gp.shape[1]

    def body(gp_ref, out_ref, sems):
        x, y, c = _me()
        cp = _rcopy(gp_ref.at[:, _my_half(rows, c, mine=False)], out_ref, sems, 0, (x, y, 1 - c))
        cp.start()
        cp.wait()

    return _pc(body, name=name, in_specs=[HBM], out_specs=HBM, out_shape=_sds((4, rows // 2, D_MODEL), gp.dtype),
               scratch=[pltpu.SemaphoreType.DMA((2, 1))])(gp)


def _chips_start(sb_ref, out_ref, sems):
    x, y, c = _me()
    for f in (1, 2, 3):
        px, py = _flip(x, y, f)
        _rcopy(sb_ref.at[2 * px + py], out_ref.at[f - 1], sems, f - 1, (px, py, c)).start()


def _chips_finish(sb_ref, out_ref, sems):
    x, y, c = _me()
    for f in (1, 2, 3):
        _rcopy(sb_ref.at[0], out_ref.at[f - 1], sems, f - 1, (x, y, c)).wait_recv()
    for f in (1, 2, 3):
        px, py = _flip(x, y, f)
        _rcopy(sb_ref.at[2 * px + py], out_ref.at[f - 1], sems, f - 1, (px, py, c)).wait_send()


def _final_exchange(small, tots):
    n = len(tots)

    def body(small_ref, *refs):
        tot_refs, smalls_ref, out_refs, sems, local_sem = refs[:n], refs[n], refs[n + 1:2 * n + 1], refs[-2], refs[-1]
        x, y, c = _me()
        me = 4 * x + 2 * y + c
        mine = pltpu.make_async_copy(small_ref, smalls_ref.at[me], local_sem)
        mine.start()
        sends = []
        for f in range(1, 8):
            px, py = _flip(x, y, f)
            pc = 1 - c if f & 4 else c
            sends.append(_rcopy(small_ref, smalls_ref.at[me], sems, f - 1, (px, py, pc)))
        for i in range(n):
            half = _my_half(tot_refs[i].shape[0], c)
            sends.append(_rcopy(tot_refs[i].at[half], out_refs[i].at[half], sems, 7 + i, (x, y, 1 - c)))
        for cp in sends:
            cp.start()
        for f in range(1, 8):
            px, py = _flip(x, y, f)
            pc = 1 - c if f & 4 else c
            _rcopy(small_ref, smalls_ref.at[4 * px + 2 * py + pc], sems, f - 1, (x, y, c)).wait_recv()
        for i in range(n):
            theirs = out_refs[i].at[_my_half(tot_refs[i].shape[0], c, mine=False)]
            _rcopy(theirs, theirs, sems, 7 + i, (x, y, 1 - c)).wait_recv()
        for cp in sends:
            cp.wait_send()
        mine.wait()

    return _pc(body, name="final_exchange", in_specs=[HBM] * (1 + n), out_specs=[HBM] * (1 + n),
               out_shape=[_sds((8, SMALL_ROWS, D_MODEL), f32)] + [_sds(t.shape, f32) for t in tots],
               aliases={1 + i: 1 + i for i in range(n)},
               scratch=[pltpu.SemaphoreType.DMA((2, 7 + n)), pltpu.SemaphoreType.DMA])(small, *tots)


def _sum_rows(half):
    return max(r for r in range(16, 513, 16) if half % r == 0)


def _sum_over_cores(gp, recv, cq, name):
    half = recv.shape[1]
    tr = _sum_rows(half)
    nb = half // tr

    def body(cq_ref, g_ref, r_ref, o32_ref, o16_ref):
        s = g_ref[...].astype(f32) + r_ref[...].astype(f32)
        o16_ref[...] = s.astype(bf16)

        @pl.when(pl.program_id(1) == cq_ref[1])
        def _():
            o32_ref[...] = s[0]

    spec = pl.BlockSpec((1, tr, D_MODEL), lambda i, b, cq_ref: (b, i, 0))
    gs = pltpu.PrefetchScalarGridSpec(
        num_scalar_prefetch=1, grid=(nb, 4),
        in_specs=[pl.BlockSpec((1, tr, D_MODEL), lambda i, b, cq_ref: (b, cq_ref[0] * nb + i, 0)), spec],
        out_specs=[pl.BlockSpec((tr, D_MODEL), lambda i, b, cq_ref: (i, 0)), spec])
    return _pc(body, name=name, grid_spec=gs, sem=("arbitrary", "arbitrary"),
               out_shape=[_sds((half, D_MODEL), f32), _sds((4, half, D_MODEL), bf16)])(cq, gp, recv)


def _sum_over_chips(s32, recv, cq, name):
    half = recv.shape[1]
    tr = _sum_rows(half)
    nb = half // tr

    def body(cq_ref, own_ref, r_ref, o_ref):
        o_ref[...] = ((own_ref[...] + r_ref[0].astype(f32)) + r_ref[1].astype(f32)) + r_ref[2].astype(f32)

    gs = pltpu.PrefetchScalarGridSpec(
        num_scalar_prefetch=1, grid=(nb,),
        in_specs=[pl.BlockSpec((tr, D_MODEL), lambda i, cq_ref: (i, 0)),
                  pl.BlockSpec((3, tr, D_MODEL), lambda i, cq_ref: (0, i, 0))],
        out_specs=pl.BlockSpec((tr, D_MODEL), lambda i, cq_ref: (cq_ref[0] * nb + i, 0)))
    return _pc(body, name=name, grid_spec=gs, sem=("arbitrary",),
               out_shape=_sds((2 * half, D_MODEL), f32))(cq, s32, recv)


def _sum_small(smalls):
    def body(s_ref, o_ref):
        tot = s_ref[0]
        for d in range(1, 8):
            tot = tot + s_ref[d]
        o_ref[...] = tot

    return _pc(body, name="small_sum", out_shape=_sds((SMALL_ROWS, D_MODEL), f32))(smalls)


def _adamw_math(w, g, m, v):
    m = ADAM_B1 * m + (1.0 - ADAM_B1) * g
    v = ADAM_B2 * v + (1.0 - ADAM_B2) * jnp.square(g)
    m_hat = m / (1.0 - ADAM_B1 ** ADAM_STEP)
    v_hat = v / (1.0 - ADAM_B2 ** ADAM_STEP)
    delta = -ADAM_LR * (m_hat / (jnp.sqrt(v_hat) + ADAM_EPS) + ADAM_WD * w)
    return delta, m, v


def _adamw(w, g, m, v, name):
    cols = w.shape[1]
    return _rw(_adamw_math, [w, g, m, v], [], [(cols, f32)] * 3, name=name, tr=256)


def _adamw_small(items, name):
    n = len(items)

    def body(*refs):
        ins, outs = refs[:4 * n], refs[4 * n:]
        for i in range(n):
            res = _adamw_math(*[r[...] for r in ins[4 * i:4 * i + 4]])
            for o, val in zip(outs[3 * i:3 * i + 3], res):
                o[...] = val

    flat = [a for it in items for a in it]
    out_shape = [_sds(it[0].shape, f32) for it in items for _ in range(3)]
    res = _pc(body, name=name, out_shape=out_shape)(*flat)
    return [tuple(res[3 * i:3 * i + 3]) for i in range(n)]


def _pack_shards(sh, layout, pad):
    parts = [(sh[n] if layer is None else sh[n][layer]).reshape(-1, D_MODEL).astype(bf16) for n, layer, _ in layout]
    if pad:
        parts.append(jnp.zeros((pad, D_MODEL), bf16))
    return jnp.concatenate(parts, axis=0)


def _mlp_full(g4, off, layer):
    o, r = off["mlp_w_up", layer]
    up = g4[:, o:o + r].transpose(1, 0, 2).reshape(D_MODEL, D_FF)
    o, r = off["mlp_w_down", layer]
    return {("mlp_w_up", layer): up, ("mlp_w_down", layer): g4[:, o:o + r].reshape(D_FF, D_MODEL)}


def _unpack_early(g4):
    hg = g4[:, 0:1024].reshape(4, 4, 256, D_MODEL)
    o, r = W_EARLY_OFF["hgrn_w_o", None]
    return {"hgrn_w4": hg.transpose(0, 2, 1, 3).reshape(D_MODEL, 4 * D_MODEL),
            "hgrn_w_o": g4[:, o:o + r].reshape(D_MODEL, D_MODEL)}


def _unpack_gla(g4):
    return _mlp_full(g4, W_GLA_OFF, 0)


def _unpack_last(g4):
    return _mlp_full(g4, W_LAST_OFF, 1)


def _unpack_mid(g4):
    def rows(name):
        o, r = W_MID_OFF[name, None]
        return g4[:, o:o + r]

    w = {"mla_w_dq": rows("mla_w_dq").reshape(D_MODEL, MLA_Q_LORA)}
    uq = rows("mla_w_uq").reshape(4, MLA_Q_LORA, 768).transpose(1, 0, 2).reshape(MLA_Q_LORA, MLA_HEADS, MLA_NOPE + MLA_ROPE)
    w["mla_w_qcat"] = jnp.pad(uq, ((0, 0), (0, 0), (0, ATT_QK - MLA_NOPE - MLA_ROPE))).reshape(MLA_Q_LORA, MLA_HEADS * ATT_QK)
    w["mla_w_o"] = rows("mla_w_o").reshape(MLA_HEADS * MLA_V, D_MODEL)
    dkv = rows("kv_w_dkv").reshape(D_MODEL, MLA_KV_LORA + MLA_ROPE)
    w["kv_w_dkv"] = jnp.pad(dkv, ((0, 0), (0, LANES - MLA_ROPE)))
    uk = rows("kv_w_uk").reshape(4, MLA_KV_LORA, 512).transpose(1, 0, 2).reshape(MLA_KV_LORA, MLA_HEADS, MLA_NOPE)
    w["kv_w_kcat"] = jnp.pad(uk, ((0, 0), (0, 0), (0, ATT_QK - MLA_NOPE))).reshape(MLA_KV_LORA, MLA_HEADS * ATT_QK)
    w["kv_w_uv"] = rows("kv_w_uv").reshape(4, MLA_KV_LORA, 512).transpose(1, 0, 2).reshape(MLA_KV_LORA, MLA_HEADS * MLA_V)
    return w


def _pack_grads_late(g):
    return g["hgrn_w4"].reshape(4, 256, 4, D_MODEL).transpose(0, 2, 1, 3).reshape(4, G_LATE_ROWS, D_MODEL)


def _grad_rows(g, name, layer):
    if name in ("mlp_w_up", "mlp_w_down"):
        full = g[name, layer]
        return full.reshape(D_MODEL, 4, 1024).transpose(1, 0, 2) if name == "mlp_w_up" else full.reshape(4, 1024, D_MODEL)
    if name == "mla_w_uq":
        uq = g["mla_w_qcat"].reshape(MLA_Q_LORA, MLA_HEADS, ATT_QK)[:, :, :MLA_NOPE + MLA_ROPE]
        return uq.reshape(MLA_Q_LORA, 4, 768).transpose(1, 0, 2).reshape(4, 192, D_MODEL)
    if name == "kv_w_uk":
        uk = g["kv_w_kcat"].reshape(MLA_KV_LORA, MLA_HEADS, ATT_QK)[:, :, :MLA_NOPE]
        return uk.reshape(MLA_KV_LORA, 4, 512).transpose(1, 0, 2).reshape(4, 128, D_MODEL)
    if name == "kv_w_uv":
        return g[name].reshape(MLA_KV_LORA, 4, 512).transpose(1, 0, 2).reshape(4, 128, D_MODEL)
    if name == "kv_w_dkv":
        return g[name][:, :MLA_KV_LORA + MLA_ROPE].reshape(4, 80, D_MODEL)
    return g[name].reshape(4, -1, D_MODEL)


def _pack_grads(g, layout):
    parts = [_grad_rows(g, name, layer) for name, layer, _ in layout]
    if layout in PADDED:
        parts.append(jnp.zeros((4, PACK_PAD, D_MODEL), bf16))
    return jnp.concatenate(parts, axis=1)


LOSS_ROW = 11


def _pack_small(g, loss):
    rows = []
    for name, _, r, wd in SMALL:
        a = g[name].reshape(r, wd)
        rows.append(jnp.pad(a, ((0, 0), (0, D_MODEL - wd))) if wd < D_MODEL else a)
    assert sum(r for _, _, r, _ in SMALL) == LOSS_ROW
    rows.append(jnp.full((1, D_MODEL), loss, f32))
    rows.append(jnp.zeros((SMALL_ROWS - LOSS_ROW - 1, D_MODEL), f32))
    return jnp.concatenate(rows, axis=0)


def kernel(x, hgrn_norm, hgrn_w_q, hgrn_w_f, hgrn_w_i, hgrn_w_g, hgrn_g_norm, hgrn_w_o, hgrn_lb_logits, mla_norm, mla_w_dq, mla_q_norm, mla_w_uq, mla_w_o, kv_in_norm, kv_w_dkv, kv_norm, kv_w_uk, kv_w_uv, mlp_norm, mlp_w_up, mlp_w_down, final_norm, loss_target, m_hgrn_norm, m_hgrn_w_q, m_hgrn_w_f, m_hgrn_w_i, m_hgrn_w_g, m_hgrn_g_norm, m_hgrn_w_o, m_hgrn_lb_logits, m_mla_norm, m_mla_w_dq, m_mla_q_norm, m_mla_w_uq, m_mla_w_o, m_kv_in_norm, m_kv_w_dkv, m_kv_norm, m_kv_w_uk, m_kv_w_uv, m_mlp_norm, m_mlp_w_up, m_mlp_w_down, m_final_norm, v_hgrn_norm, v_hgrn_w_q, v_hgrn_w_f, v_hgrn_w_i, v_hgrn_w_g, v_hgrn_g_norm, v_hgrn_w_o, v_hgrn_lb_logits, v_mla_norm, v_mla_w_dq, v_mla_q_norm, v_mla_w_uq, v_mla_w_o, v_kv_in_norm, v_kv_w_dkv, v_kv_norm, v_kv_w_uk, v_kv_w_uv, v_mlp_norm, v_mlp_w_up, v_mlp_w_down, v_final_norm):
    given = dict(locals())
    wsh = {n: given[n] for n in WEIGHTS}
    msh = {n: given["m_" + n] for n in WEIGHTS}
    vsh = {n: given["v_" + n] for n in WEIGHTS}
    xi, yi, ci = _me()
    chip = 2 * xi + yi
    cq = jnp.stack([ci, chip]).astype(jnp.int32)

    small_w = {n: wsh[n].reshape(r, -1) for n, _, r, _ in SMALL}
    sv = jnp.concatenate([small_w["hgrn_norm"], small_w["hgrn_lb_logits"], jnp.zeros((5, 256), f32)], axis=0)
    g4, sv4 = _all_gather_weights(_pack_shards(wsh, W_EARLY, 0), sv)
    w = _unpack_early(g4)
    w["hgrn_norm"] = sv4[:, 0, :].reshape(1, D_MODEL)
    w["hgrn_lb_logits"] = sv4[:, 1:3, :].transpose(1, 0, 2).reshape(2, D_MODEL)
    for n in ("hgrn_g_norm", "mla_norm", "mla_q_norm", "kv_in_norm", "kv_norm", "mlp_norm", "final_norm"):
        w[n] = small_w[n]

    class Comm:
        shard = {"gla": _pack_shards(wsh, W_GLA, 0), "mlp0_up": _pack_shards(wsh, W_MID, PACK_PAD),
                 "attn": _pack_shards(wsh, W_LAST, 0)}
        unpack = {"gla": _unpack_gla, "mlp0_up": _unpack_mid, "attn": _unpack_last}
        layout = {"gla": G_GLA, "attn": G_ATTN}
        received, s32 = {}, {}

        @staticmethod
        def reduce(grads, part):
            gp = _pack_grads_late(grads) if part == "late" else _pack_grads(grads, Comm.layout[part])
            Comm.s32[part], s16 = _sum_over_cores(gp, _send_half_to_sibling(gp, "grads_to_sibling_" + part), cq,
                                                  "grads_sum_cores_" + part)
            return s16

    loss, grad_x, g = _local_step(x.reshape(-1, D_MODEL), loss_target.reshape(-1, D_MODEL), w, Comm)

    parts = ("late", "gla", "attn")
    halves = [_sum_over_chips(Comm.s32[p], Comm.received[p], cq, "grads_sum_chips_" + p) for p in parts]
    smalls, *totals = _final_exchange(_pack_small(g, loss), halves)
    total = dict(zip(parts, totals))
    small_tot = _sum_small(smalls)
    loss = small_tot[LOSS_ROW, 0]

    where = {}
    for part, offsets in (("late", G_LATE_OFF), ("gla", G_GLA_OFF), ("attn", G_ATTN_OFF)):
        for (n, layer), (o, r) in offsets.items():
            where.setdefault(n, []).append(total[part][o:o + r])
    grad, delta, new_m, new_v = {}, {}, {}, {}
    groups = {"hgrn": [], "mla_kv": []}
    for n, pieces in where.items():
        shp = wsh[n].shape
        two_d = (-1, shp[-1])
        grad[n] = (pieces[0] if len(pieces) == 1 else jnp.concatenate(pieces, axis=0)).reshape(shp)
        operands = (wsh[n].reshape(two_d), grad[n].reshape(two_d), msh[n].reshape(two_d), vsh[n].reshape(two_d))
        if n.startswith("mlp"):
            res = _adamw(*operands, "adamw_" + n)
            delta[n], new_m[n], new_v[n] = (a.reshape(shp) for a in res)
        else:
            groups["hgrn" if n.startswith("hgrn") else "mla_kv"].append((n, operands))
    for gname, members in groups.items():
        for (n, _), res in zip(members, _adamw_small([ops for _, ops in members], "adamw_" + gname)):
            delta[n], new_m[n], new_v[n] = (a.reshape(wsh[n].shape) for a in res)
    items = []
    for n, row, r, wd in SMALL:
        gs = small_tot[row:row + r, :wd]
        if n in ("hgrn_norm", "hgrn_lb_logits"):
            gs = lax.dynamic_slice(gs, (0, 256 * chip), (r, 256))
        grad[n] = gs.reshape(wsh[n].shape)
        items.append((small_w[n], gs, msh[n].reshape(gs.shape), vsh[n].reshape(gs.shape)))
    for (n, _, _, _), (d, m2, v2) in zip(SMALL, _adamw_small(items, "adamw_small")):
        shp = wsh[n].shape
        delta[n], new_m[n], new_v[n] = d.reshape(shp), m2.reshape(shp), v2.reshape(shp)

    return (loss, grad_x.reshape(x.shape), *[grad[n] for n in WEIGHTS], *[delta[n] for n in WEIGHTS],
            *[new_m[n] for n in WEIGHTS], *[new_v[n] for n in WEIGHTS])
```
